```python
import math
import jax, jax.numpy as jnp
from jax import lax
import numpy as np

D_MODEL = 2048
BATCH = 8
SEQ = 4096
DEPTH = 1

N_Q_HEADS = 32
N_KV_HEADS = 4
GROUP = N_Q_HEADS // N_KV_HEADS
HEAD_DIM = 64
ATTN_WIDTH = N_Q_HEADS * HEAD_DIM
KV_WIDTH = N_KV_HEADS * HEAD_DIM
WINDOW = 128
BLOCK = 128
NEG_INF = -1e30
N_BUCKETS = 32
MAX_DISTANCE = 128
LRU_WIDTH = D_MODEL
LRU_BLOCKS = 16
LRU_BLOCK_W = LRU_WIDTH // LRU_BLOCKS
CONV_WIDTH = 4
LRU_C = 8.0
D_FF = 4 * D_MODEL
EPS = 1e-6
IN_SPLITS = (LRU_WIDTH, LRU_WIDTH, ATTN_WIDTH, KV_WIDTH, KV_WIDTH, D_MODEL, D_MODEL)
IN_WIDTH = sum(IN_SPLITS)

kernel_name = "hybrid_rglru_swa_sink_sqrelu_adaln"


def _rms_norm(x, g):
    xf = x.astype(jnp.float32)
    y = xf * lax.rsqrt(jnp.mean(xf * xf, axis=-1, keepdims=True) + EPS)
    return (y * g.astype(jnp.float32)).astype(x.dtype)


def _modulate(h, shift, scale):
    return h * (1.0 + scale[:, None, :]) + shift[:, None, :]


def _causal_depthwise_conv(x, w, b):
    s = x.shape[1]
    xp = jnp.pad(x, ((0, 0), (CONV_WIDTH - 1, 0), (0, 0)))
    y = b
    for k in range(CONV_WIDTH):
        y = y + xp[:, k:k + s] * w[k]
    return y


def _rg_lru(x, wa, ba, wx, bx, lam):
    b_, s, w = x.shape
    xb = x.reshape(b_, s, LRU_BLOCKS, LRU_BLOCK_W)
    r = jax.nn.sigmoid(jnp.einsum("bshi,hij->bshj", xb, wa).reshape(b_, s, w) + ba)
    i = jax.nn.sigmoid(jnp.einsum("bshi,hij->bshj", xb, wx).reshape(b_, s, w) + bx)
    log_a = -LRU_C * r.astype(jnp.float32) * jax.nn.softplus(-lam.astype(jnp.float32))
    a = jnp.exp(log_a)
    mult = jnp.sqrt(-jnp.expm1(2.0 * log_a))
    mult = jnp.where(jnp.arange(s)[None, :, None] == 0, 1.0, mult)
    u = mult * (i * x).astype(jnp.float32)

    def step(h, au):
        a_t, u_t = au
        h = a_t * h + u_t
        return h, h

    h0 = jnp.zeros((b_, w), jnp.float32)
    _, hs = lax.scan(step, h0, (jnp.swapaxes(a, 0, 1), jnp.swapaxes(u, 0, 1)))
    return jnp.swapaxes(hs, 0, 1).astype(x.dtype)


def _t5_causal_bucket(rel):
    max_exact = N_BUCKETS // 2
    relf = jnp.maximum(rel, 1).astype(jnp.float32)
    large = max_exact + (jnp.log(relf / max_exact) / math.log(MAX_DISTANCE / max_exact)
                         * (N_BUCKETS - max_exact)).astype(jnp.int32)
    large = jnp.minimum(large, N_BUCKETS - 1)
    return jnp.where(rel < max_exact, rel, large)


def _band_bias_and_mask(rel_bias, n_blocks):
    qi = jnp.arange(BLOCK)[:, None]
    ki = jnp.arange(2 * BLOCK)[None, :]
    rel = qi + BLOCK - ki
    bucket = _t5_causal_bucket(jnp.maximum(rel, 0))
    bias = jnp.transpose(rel_bias[bucket], (2, 0, 1)).astype(jnp.float32)
    bias = bias.reshape(N_KV_HEADS, GROUP, BLOCK, 2 * BLOCK)
    kpos = jnp.arange(n_blocks)[:, None, None] * BLOCK - BLOCK + ki[None]
    valid = (kpos >= 0) & (rel[None] >= 0) & (rel[None] < WINDOW)
    return bias, valid


def _swa_sink_attention(q, k, v, sinks, rel_bias):
    b_, s, _ = q.shape
    n = s // BLOCK
    bias, valid = _band_bias_and_mask(rel_bias, n)
    q = q.reshape(b_, n, BLOCK, N_KV_HEADS, GROUP, HEAD_DIM)

    def band(t):
        t = t.reshape(b_, s, N_KV_HEADS, HEAD_DIM)
        tp = jnp.pad(t, ((0, 0), (BLOCK, 0), (0, 0), (0, 0)))
        tp = tp.reshape(b_, n + 1, BLOCK, N_KV_HEADS, HEAD_DIM)
        return jnp.concatenate([tp[:, :-1], tp[:, 1:]], axis=2)

    kb, vb = band(k), band(v)
    logits = jnp.einsum("bnqkgd,bnskd->bnkgqs", q, kb,
                        preferred_element_type=jnp.float32) * (HEAD_DIM ** -0.5)
    logits = jnp.where(valid[None, :, None, None], logits + bias[None, None], NEG_INF)
    sink = sinks.astype(jnp.float32).reshape(N_KV_HEADS, GROUP)[None, None, :, :, None, None]
    m = jnp.maximum(jnp.max(logits, axis=-1, keepdims=True), sink)
    e = jnp.exp(logits - m)
    p = e / (jnp.sum(e, axis=-1, keepdims=True) + jnp.exp(sink - m))
    o = jnp.einsum("bnkgqs,bnskd->bnqkgd", p.astype(vb.dtype), vb)
    return o.reshape(b_, s, ATTN_WIDTH)


def _fwd_setup_inputs(seed: int = 0) -> dict:
    key = jax.random.key(seed)
    ks = jax.random.split(key, 24)
    f32 = jnp.float32
    L, D = DEPTH, D_MODEL

    def nrm(k, shape, scale):
        return jax.random.normal(k, shape, f32) * scale

    a_c = jax.random.uniform(ks[12], (L, LRU_WIDTH), f32, 0.9, 0.999)
    a0 = a_c ** (1.0 / LRU_C)
    lam = jnp.log(a0) - jnp.log1p(-a0)
    return {
        "x": nrm(ks[0], (BATCH, SEQ, D), 1.0),
        "c": nrm(ks[1], (BATCH, D), 1.0),
        "w_ada": nrm(ks[2], (L, D, 6 * D), 0.5 * D ** -0.5),
        "b_ada": nrm(ks[3], (L, 6 * D), 0.02),
        "norm1_g": 1.0 + nrm(ks[4], (L, D), 0.02),
        "w_in": nrm(ks[5], (L, D, IN_WIDTH), D ** -0.5),
        "conv_w": nrm(ks[6], (L, CONV_WIDTH, LRU_WIDTH), CONV_WIDTH ** -0.5),
        "conv_b": nrm(ks[7], (L, LRU_WIDTH), 0.02),
        "lru_wa": nrm(ks[8], (L, LRU_BLOCKS, LRU_BLOCK_W, LRU_BLOCK_W), LRU_BLOCK_W ** -0.5),
        "lru_ba": nrm(ks[9], (L, LRU_WIDTH), 0.02),
        "lru_wx": nrm(ks[10], (L, LRU_BLOCKS, LRU_BLOCK_W, LRU_BLOCK_W), LRU_BLOCK_W ** -0.5),
        "lru_bx": nrm(ks[11], (L, LRU_WIDTH), 0.02),
        "lru_lambda": lam,
        "w_lru_out": nrm(ks[13], (L, LRU_WIDTH, D), LRU_WIDTH ** -0.5),
        "w_attn_out": nrm(ks[14], (L, ATTN_WIDTH, D), ATTN_WIDTH ** -0.5),
        "attn_sinks": nrm(ks[15], (L, N_Q_HEADS), 1.0),
        "rel_bias": nrm(ks[16], (N_BUCKETS, N_Q_HEADS), 0.5),
        "w_out": nrm(ks[17], (L, D, D), D ** -0.5),
        "norm2_g": 1.0 + nrm(ks[18], (L, D), 0.02),
        "w_ff1": nrm(ks[19], (L, D, D_FF), D ** -0.5),
        "w_ff2": nrm(ks[20], (L, D_FF, D), D_FF ** -0.5),
        "final_g": 1.0 + nrm(ks[21], (D,), 0.02),
    }


def _fwd_reference(x, c, w_ada, b_ada, norm1_g, w_in, conv_w, conv_b, lru_wa, lru_ba, lru_wx,
              lru_bx, lru_lambda, w_lru_out, w_attn_out, attn_sinks, rel_bias, w_out,
              norm2_g, w_ff1, w_ff2, final_g):
    split_idx = list(np.cumsum(IN_SPLITS)[:-1])
    c_act = jax.nn.silu(c)
    for l in range(DEPTH):
        mod = jnp.dot(c_act, w_ada[l]) + b_ada[l]
        shift1, scale1, gate1, shift2, scale2, gate2 = jnp.split(mod, 6, axis=-1)

        h = _modulate(_rms_norm(x, norm1_g[l]), shift1, scale1)
        proj = jnp.einsum("bsd,de->bse", h, w_in[l])
        lru_x, lru_gate, q, k, v, g_a, g_b = jnp.split(proj, split_idx, axis=-1)

        xc = _causal_depthwise_conv(lru_x, conv_w[l], conv_b[l])
        rec = _rg_lru(xc, lru_wa[l], lru_ba[l], lru_wx[l], lru_bx[l], lru_lambda[l])
        y_a = jnp.einsum("bsw,wd->bsd", rec * jax.nn.gelu(lru_gate, approximate=True),
                         w_lru_out[l])

        att = _swa_sink_attention(q, k, v, attn_sinks[l], rel_bias)
        y_b = jnp.einsum("bsw,wd->bsd", att, w_attn_out[l])

        merged = jax.nn.sigmoid(g_a) * y_a + jax.nn.sigmoid(g_b) * y_b
        x = x + gate1[:, None, :] * jnp.einsum("bsd,de->bse", merged, w_out[l])

        h2 = _modulate(_rms_norm(x, norm2_g[l]), shift2, scale2)
        ff = jnp.square(jax.nn.relu(jnp.einsum("bsd,df->bsf", h2, w_ff1[l])))
        x = x + gate2[:, None, :] * jnp.einsum("bsf,fd->bsd", ff, w_ff2[l])

    return _rms_norm(x, final_g)


import jax as _jax
import jax.numpy as _jnp

TWIN_FORMAT = 'train_step'
FWD_PARAMS = ['x', 'c', 'w_ada', 'b_ada', 'norm1_g', 'w_in', 'conv_w', 'conv_b', 'lru_wa', 'lru_ba', 'lru_wx', 'lru_bx', 'lru_lambda', 'w_lru_out', 'w_attn_out', 'attn_sinks', 'rel_bias', 'w_out', 'norm2_g', 'w_ff1', 'w_ff2', 'final_g']
TWIN_WEIGHTS = ['w_ada', 'b_ada', 'norm1_g', 'w_in', 'conv_w', 'conv_b', 'lru_wa', 'lru_ba', 'lru_wx', 'lru_bx', 'lru_lambda', 'w_lru_out', 'w_attn_out', 'attn_sinks', 'rel_bias', 'w_out', 'norm2_g', 'w_ff1', 'w_ff2', 'final_g']
TWIN_DIFF_INPUT = 'x'
TWIN_INPUTS = ['x', 'c', 'w_ada', 'b_ada', 'norm1_g', 'w_in', 'conv_w', 'conv_b', 'lru_wa', 'lru_ba', 'lru_wx', 'lru_bx', 'lru_lambda', 'w_lru_out', 'w_attn_out', 'attn_sinks', 'rel_bias', 'w_out', 'norm2_g', 'w_ff1', 'w_ff2', 'final_g', 'loss_target', 'm_w_ada', 'm_b_ada', 'm_norm1_g', 'm_w_in', 'm_conv_w', 'm_conv_b', 'm_lru_wa', 'm_lru_ba', 'm_lru_wx', 'm_lru_bx', 'm_lru_lambda', 'm_w_lru_out', 'm_w_attn_out', 'm_attn_sinks', 'm_rel_bias', 'm_w_out', 'm_norm2_g', 'm_w_ff1', 'm_w_ff2', 'm_final_g', 'v_w_ada', 'v_b_ada', 'v_norm1_g', 'v_w_in', 'v_conv_w', 'v_conv_b', 'v_lru_wa', 'v_lru_ba', 'v_lru_wx', 'v_lru_bx', 'v_lru_lambda', 'v_w_lru_out', 'v_w_attn_out', 'v_attn_sinks', 'v_rel_bias', 'v_w_out', 'v_norm2_g', 'v_w_ff1', 'v_w_ff2', 'v_final_g']
TWIN_OUTPUTS = ['loss', 'grad_x', 'grad_w_ada', 'grad_b_ada', 'grad_norm1_g', 'grad_w_in', 'grad_conv_w', 'grad_conv_b', 'grad_lru_wa', 'grad_lru_ba', 'grad_lru_wx', 'grad_lru_bx', 'grad_lru_lambda', 'grad_w_lru_out', 'grad_w_attn_out', 'grad_attn_sinks', 'grad_rel_bias', 'grad_w_out', 'grad_norm2_g', 'grad_w_ff1', 'grad_w_ff2', 'grad_final_g', 'delta_w_ada', 'delta_b_ada', 'delta_norm1_g', 'delta_w_in', 'delta_conv_w', 'delta_conv_b', 'delta_lru_wa', 'delta_lru_ba', 'delta_lru_wx', 'delta_lru_bx', 'delta_lru_lambda', 'delta_w_lru_out', 'delta_w_attn_out', 'delta_attn_sinks', 'delta_rel_bias', 'delta_w_out', 'delta_norm2_g', 'delta_w_ff1', 'delta_w_ff2', 'delta_final_g', 'new_m_w_ada', 'new_m_b_ada', 'new_m_norm1_g', 'new_m_w_in', 'new_m_conv_w', 'new_m_conv_b', 'new_m_lru_wa', 'new_m_lru_ba', 'new_m_lru_wx', 'new_m_lru_bx', 'new_m_lru_lambda', 'new_m_w_lru_out', 'new_m_w_attn_out', 'new_m_attn_sinks', 'new_m_rel_bias', 'new_m_w_out', 'new_m_norm2_g', 'new_m_w_ff1', 'new_m_w_ff2', 'new_m_final_g', 'new_v_w_ada', 'new_v_b_ada', 'new_v_norm1_g', 'new_v_w_in', 'new_v_conv_w', 'new_v_conv_b', 'new_v_lru_wa', 'new_v_lru_ba', 'new_v_lru_wx', 'new_v_lru_bx', 'new_v_lru_lambda', 'new_v_w_lru_out', 'new_v_w_attn_out', 'new_v_attn_sinks', 'new_v_rel_bias', 'new_v_w_out', 'new_v_norm2_g', 'new_v_w_ff1', 'new_v_w_ff2', 'new_v_final_g']
TWIN_LEAF_KINDS = {'loss': 'loss', 'grad_x': 'grad_x', 'grad_w_ada': 'grad_w', 'grad_b_ada': 'grad_w', 'grad_norm1_g': 'grad_w', 'grad_w_in': 'grad_w', 'grad_conv_w': 'grad_w', 'grad_conv_b': 'grad_w', 'grad_lru_wa': 'grad_w', 'grad_lru_ba': 'grad_w', 'grad_lru_wx': 'grad_w', 'grad_lru_bx': 'grad_w', 'grad_lru_lambda': 'grad_w', 'grad_w_lru_out': 'grad_w', 'grad_w_attn_out': 'grad_w', 'grad_attn_sinks': 'grad_w', 'grad_rel_bias': 'grad_w', 'grad_w_out': 'grad_w', 'grad_norm2_g': 'grad_w', 'grad_w_ff1': 'grad_w', 'grad_w_ff2': 'grad_w', 'grad_final_g': 'grad_w', 'delta_w_ada': 'delta_w', 'delta_b_ada': 'delta_w', 'delta_norm1_g': 'delta_w', 'delta_w_in': 'delta_w', 'delta_conv_w': 'delta_w', 'delta_conv_b': 'delta_w', 'delta_lru_wa': 'delta_w', 'delta_lru_ba': 'delta_w', 'delta_lru_wx': 'delta_w', 'delta_lru_bx': 'delta_w', 'delta_lru_lambda': 'delta_w', 'delta_w_lru_out': 'delta_w', 'delta_w_attn_out': 'delta_w', 'delta_attn_sinks': 'delta_w', 'delta_rel_bias': 'delta_w', 'delta_w_out': 'delta_w', 'delta_norm2_g': 'delta_w', 'delta_w_ff1': 'delta_w', 'delta_w_ff2': 'delta_w', 'delta_final_g': 'delta_w', 'new_m_w_ada': 'new_m', 'new_m_b_ada': 'new_m', 'new_m_norm1_g': 'new_m', 'new_m_w_in': 'new_m', 'new_m_conv_w': 'new_m', 'new_m_conv_b': 'new_m', 'new_m_lru_wa': 'new_m', 'new_m_lru_ba': 'new_m', 'new_m_lru_wx': 'new_m', 'new_m_lru_bx': 'new_m', 'new_m_lru_lambda': 'new_m', 'new_m_w_lru_out': 'new_m', 'new_m_w_attn_out': 'new_m', 'new_m_attn_sinks': 'new_m', 'new_m_rel_bias': 'new_m', 'new_m_w_out': 'new_m', 'new_m_norm2_g': 'new_m', 'new_m_w_ff1': 'new_m', 'new_m_w_ff2': 'new_m', 'new_m_final_g': 'new_m', 'new_v_w_ada': 'new_v', 'new_v_b_ada': 'new_v', 'new_v_norm1_g': 'new_v', 'new_v_w_in': 'new_v', 'new_v_conv_w': 'new_v', 'new_v_conv_b': 'new_v', 'new_v_lru_wa': 'new_v', 'new_v_lru_ba': 'new_v', 'new_v_lru_wx': 'new_v', 'new_v_lru_bx': 'new_v', 'new_v_lru_lambda': 'new_v', 'new_v_w_lru_out': 'new_v', 'new_v_w_attn_out': 'new_v', 'new_v_attn_sinks': 'new_v', 'new_v_rel_bias': 'new_v', 'new_v_w_out': 'new_v', 'new_v_norm2_g': 'new_v', 'new_v_w_ff1': 'new_v', 'new_v_w_ff2': 'new_v', 'new_v_final_g': 'new_v'}


def _forward(args):
    return _fwd_reference(*[args[k] for k in FWD_PARAMS])


def _output_shape():
    out = _jax.eval_shape(lambda: _forward(_fwd_setup_inputs(0)))
    return out.shape, out.dtype

N_MICROBATCH = 1
ADAM_LR = 0.001
ADAM_B1 = 0.9
ADAM_B2 = 0.999
ADAM_EPS = 1e-08
ADAM_WD = 0.01
ADAM_STEP = 10
PER_EXAMPLE_BATCH_AXIS = {'x': 0, 'c': 0, 'loss_target': 0}
SHARED_INPUTS = []
_WEIGHT_DTYPES = {'w_ada': _jnp.float32, 'b_ada': _jnp.float32, 'norm1_g': _jnp.float32, 'w_in': _jnp.float32, 'conv_w': _jnp.float32, 'conv_b': _jnp.float32, 'lru_wa': _jnp.float32, 'lru_ba': _jnp.float32, 'lru_wx': _jnp.float32, 'lru_bx': _jnp.float32, 'lru_lambda': _jnp.float32, 'w_lru_out': _jnp.float32, 'w_attn_out': _jnp.float32, 'attn_sinks': _jnp.float32, 'rel_bias': _jnp.float32, 'w_out': _jnp.float32, 'norm2_g': _jnp.float32, 'w_ff1': _jnp.float32, 'w_ff2': _jnp.float32, 'final_g': _jnp.float32}
MOMENT_SCALE = {'w_ada': 5.005782e-02, 'b_ada': 8.565353e-02, 'norm1_g': 2.966579e-02, 'w_in': 1.785396e-02, 'conv_w': 2.784917e-02, 'conv_b': 9.040988e-02, 'lru_wa': 3.057778e-03, 'lru_ba': 6.233723e-03, 'lru_wx': 6.095672e-03, 'lru_bx': 1.190346e-02, 'lru_lambda': 1.777784e-02, 'w_lru_out': 2.794451e-02, 'w_attn_out': 5.666419e-03, 'attn_sinks': 2.790739e-03, 'rel_bias': 4.338972e-03, 'w_out': 2.756098e-02, 'norm2_g': 4.036243e-02, 'w_ff1': 2.051199e-02, 'w_ff2': 3.729515e-02, 'final_g': 1.610468e+01}


def _to_microbatches(a, axis):
    t = _jnp.moveaxis(a, axis, 0)
    t = t.reshape((N_MICROBATCH, t.shape[0] // N_MICROBATCH) + t.shape[1:])
    return _jnp.moveaxis(t, 1, axis + 1)


def setup_inputs(seed: int = 0) -> dict:
    inp = _fwd_setup_inputs(seed)
    key = _jax.random.fold_in(_jax.random.key(seed), 7919)
    shape, _ = _output_shape()
    out = dict(inp)
    out["loss_target"] = _jax.random.normal(_jax.random.fold_in(key, 0), shape, _jnp.float32)
    for i, name in enumerate(TWIN_WEIGHTS):
        w = inp[name].astype(_jnp.float32)
        if MOMENT_SCALE is None:
            s = _jnp.sqrt(_jnp.mean(_jnp.square(w)) + 1e-30)
        else:
            s = MOMENT_SCALE[name]
        km, kv = _jax.random.split(_jax.random.fold_in(key, i + 1))
        out[name] = w
        out["m_" + name] = s * _jax.random.normal(km, w.shape, _jnp.float32)
        out["v_" + name] = (s * s) * _jax.random.uniform(kv, w.shape, _jnp.float32, 0.5, 1.5)
    if N_MICROBATCH > 1:
        for name, axis in PER_EXAMPLE_BATCH_AXIS.items():
            out[name] = _to_microbatches(out[name], axis)
    return {'x': out['x'], 'c': out['c'], 'w_ada': out['w_ada'], 'b_ada': out['b_ada'], 'norm1_g': out['norm1_g'], 'w_in': out['w_in'], 'conv_w': out['conv_w'], 'conv_b': out['conv_b'], 'lru_wa': out['lru_wa'], 'lru_ba': out['lru_ba'], 'lru_wx': out['lru_wx'], 'lru_bx': out['lru_bx'], 'lru_lambda': out['lru_lambda'], 'w_lru_out': out['w_lru_out'], 'w_attn_out': out['w_attn_out'], 'attn_sinks': out['attn_sinks'], 'rel_bias': out['rel_bias'], 'w_out': out['w_out'], 'norm2_g': out['norm2_g'], 'w_ff1': out['w_ff1'], 'w_ff2': out['w_ff2'], 'final_g': out['final_g'], 'loss_target': out['loss_target'], 'm_w_ada': out['m_w_ada'], 'm_b_ada': out['m_b_ada'], 'm_norm1_g': out['m_norm1_g'], 'm_w_in': out['m_w_in'], 'm_conv_w': out['m_conv_w'], 'm_conv_b': out['m_conv_b'], 'm_lru_wa': out['m_lru_wa'], 'm_lru_ba': out['m_lru_ba'], 'm_lru_wx': out['m_lru_wx'], 'm_lru_bx': out['m_lru_bx'], 'm_lru_lambda': out['m_lru_lambda'], 'm_w_lru_out': out['m_w_lru_out'], 'm_w_attn_out': out['m_w_attn_out'], 'm_attn_sinks': out['m_attn_sinks'], 'm_rel_bias': out['m_rel_bias'], 'm_w_out': out['m_w_out'], 'm_norm2_g': out['m_norm2_g'], 'm_w_ff1': out['m_w_ff1'], 'm_w_ff2': out['m_w_ff2'], 'm_final_g': out['m_final_g'], 'v_w_ada': out['v_w_ada'], 'v_b_ada': out['v_b_ada'], 'v_norm1_g': out['v_norm1_g'], 'v_w_in': out['v_w_in'], 'v_conv_w': out['v_conv_w'], 'v_conv_b': out['v_conv_b'], 'v_lru_wa': out['v_lru_wa'], 'v_lru_ba': out['v_lru_ba'], 'v_lru_wx': out['v_lru_wx'], 'v_lru_bx': out['v_lru_bx'], 'v_lru_lambda': out['v_lru_lambda'], 'v_w_lru_out': out['v_w_lru_out'], 'v_w_attn_out': out['v_w_attn_out'], 'v_attn_sinks': out['v_attn_sinks'], 'v_rel_bias': out['v_rel_bias'], 'v_w_out': out['v_w_out'], 'v_norm2_g': out['v_norm2_g'], 'v_w_ff1': out['v_w_ff1'], 'v_w_ff2': out['v_w_ff2'], 'v_final_g': out['v_final_g']}


def _loss(weights, diff, rest, loss_target):
    with _jax.named_scope("forward"):
        args = {**rest, TWIN_DIFF_INPUT: diff, **{k: w.astype(_WEIGHT_DTYPES[k]) for k, w in weights.items()}}
        y = _forward(args)
    with _jax.named_scope("loss_head"):
        err = _jnp.square(y.astype(_jnp.float32) - loss_target)
        return 0.5 * _jnp.sum(_jnp.mean(err, axis=-1)) if err.ndim else 0.5 * err


def _adamw(w, g, m, v):
    m = ADAM_B1 * m + (1.0 - ADAM_B1) * g
    v = ADAM_B2 * v + (1.0 - ADAM_B2) * _jnp.square(g)
    m_hat = m / (1.0 - ADAM_B1 ** ADAM_STEP)
    v_hat = v / (1.0 - ADAM_B2 ** ADAM_STEP)
    delta = -ADAM_LR * (m_hat / (_jnp.sqrt(v_hat) + ADAM_EPS) + ADAM_WD * w)
    return delta, m, v


def reference(x, c, w_ada, b_ada, norm1_g, w_in, conv_w, conv_b, lru_wa, lru_ba, lru_wx, lru_bx, lru_lambda, w_lru_out, w_attn_out, attn_sinks, rel_bias, w_out, norm2_g, w_ff1, w_ff2, final_g, loss_target, m_w_ada, m_b_ada, m_norm1_g, m_w_in, m_conv_w, m_conv_b, m_lru_wa, m_lru_ba, m_lru_wx, m_lru_bx, m_lru_lambda, m_w_lru_out, m_w_attn_out, m_attn_sinks, m_rel_bias, m_w_out, m_norm2_g, m_w_ff1, m_w_ff2, m_final_g, v_w_ada, v_b_ada, v_norm1_g, v_w_in, v_conv_w, v_conv_b, v_lru_wa, v_lru_ba, v_lru_wx, v_lru_bx, v_lru_lambda, v_w_lru_out, v_w_attn_out, v_attn_sinks, v_rel_bias, v_w_out, v_norm2_g, v_w_ff1, v_w_ff2, v_final_g):
    given = dict(x=x, c=c, w_ada=w_ada, b_ada=b_ada, norm1_g=norm1_g, w_in=w_in, conv_w=conv_w, conv_b=conv_b, lru_wa=lru_wa, lru_ba=lru_ba, lru_wx=lru_wx, lru_bx=lru_bx, lru_lambda=lru_lambda, w_lru_out=w_lru_out, w_attn_out=w_attn_out, attn_sinks=attn_sinks, rel_bias=rel_bias, w_out=w_out, norm2_g=norm2_g, w_ff1=w_ff1, w_ff2=w_ff2, final_g=final_g, loss_target=loss_target, m_w_ada=m_w_ada, m_b_ada=m_b_ada, m_norm1_g=m_norm1_g, m_w_in=m_w_in, m_conv_w=m_conv_w, m_conv_b=m_conv_b, m_lru_wa=m_lru_wa, m_lru_ba=m_lru_ba, m_lru_wx=m_lru_wx, m_lru_bx=m_lru_bx, m_lru_lambda=m_lru_lambda, m_w_lru_out=m_w_lru_out, m_w_attn_out=m_w_attn_out, m_attn_sinks=m_attn_sinks, m_rel_bias=m_rel_bias, m_w_out=m_w_out, m_norm2_g=m_norm2_g, m_w_ff1=m_w_ff1, m_w_ff2=m_w_ff2, m_final_g=m_final_g, v_w_ada=v_w_ada, v_b_ada=v_b_ada, v_norm1_g=v_norm1_g, v_w_in=v_w_in, v_conv_w=v_conv_w, v_conv_b=v_conv_b, v_lru_wa=v_lru_wa, v_lru_ba=v_lru_ba, v_lru_wx=v_lru_wx, v_lru_bx=v_lru_bx, v_lru_lambda=v_lru_lambda, v_w_lru_out=v_w_lru_out, v_w_attn_out=v_w_attn_out, v_attn_sinks=v_attn_sinks, v_rel_bias=v_rel_bias, v_w_out=v_w_out, v_norm2_g=v_norm2_g, v_w_ff1=v_w_ff1, v_w_ff2=v_w_ff2, v_final_g=v_final_g)
    weights = {n: given[n] for n in TWIN_WEIGHTS}
    shared = {n: given[n] for n in SHARED_INPUTS}
    per_example = {n: given[n] for n in ['x', 'c']}
    grad_fn = _jax.value_and_grad(_loss, argnums=(0, 1))

    def one_microbatch(ex, loss_target):
        ex = dict(ex)
        diff = ex.pop(TWIN_DIFF_INPUT)
        return grad_fn(weights, diff, {**shared, **ex}, loss_target)

    if N_MICROBATCH == 1:
        loss, (grad_w, grad_x) = one_microbatch(per_example, given["loss_target"])
    else:
        def body(carry, xs):
            loss_sum, grad_sum = carry
            l_k, (gw_k, gx_k) = one_microbatch(xs[0], xs[1])
            with _jax.named_scope("update"):
                return (loss_sum + l_k, _jax.tree.map(_jnp.add, grad_sum, gw_k)), gx_k

        init = (_jnp.zeros((), _jnp.float32), _jax.tree.map(_jnp.zeros_like, weights))
        (loss, grad_w), grad_x = _jax.lax.scan(body, init, (per_example, given["loss_target"]))
    with _jax.named_scope("update"):
        delta_w, new_m, new_v = {}, {}, {}
        for n in TWIN_WEIGHTS:
            delta_w[n], new_m[n], new_v[n] = _adamw(weights[n], grad_w[n], given["m_" + n], given["v_" + n])
    return (loss, grad_x, *[grad_w[n] for n in TWIN_WEIGHTS], *[delta_w[n] for n in TWIN_WEIGHTS],
            *[new_m[n] for n in TWIN_WEIGHTS], *[new_v[n] for n in TWIN_WEIGHTS])
```

```python
import functools
import math

import numpy as np
import jax
import jax.numpy as jnp
from jax import lax
from jax.experimental import pallas as pl
from jax.experimental.pallas import tpu as pltpu

F32 = jnp.float32
BF16 = jnp.bfloat16
MESH = pl.DeviceIdType.MESH

D_MODEL = 2048
N_Q_HEADS = 32
N_KV_HEADS = 4
GROUP = N_Q_HEADS // N_KV_HEADS
HEAD_DIM = 64
KV_WIDTH = N_KV_HEADS * HEAD_DIM
BLOCK = 128
NEG_INF = -1e30
N_BUCKETS = 32
MAX_DISTANCE = 128
LRU_BLOCKS = 16
LRU_BLOCK_W = 128
CONV_WIDTH = 4
LRU_C = 8.0
D_FF = 4 * D_MODEL
EPS = 1e-6
IN_WIDTH = 5 * D_MODEL + 2 * KV_WIDTH
OFF_LX, OFF_LG, OFF_Q, OFF_K, OFF_V, OFF_GA, OFF_GB = 0, 2048, 4096, 6144, 6400, 6656, 8704

ADAM_LR, ADAM_B1, ADAM_B2, ADAM_EPS, ADAM_WD, ADAM_STEP = 0.001, 0.9, 0.999, 1e-08, 0.01, 10

N_DEV = 8
VMEM_LIMIT_BYTES = 48 * 1024 * 1024
LANE = 128
LRU_CW = 512


def _params(sem, **kw):
    return pltpu.CompilerParams(dimension_semantics=sem, vmem_limit_bytes=VMEM_LIMIT_BYTES, **kw)


def _tile(n, pref, mult=8):
    if n <= pref:
        return n
    t = (pref // mult) * mult
    while t >= mult:
        if n % t == 0:
            return t
        t -= mult
    return n


def _sigmoid(x):
    return 1.0 / (1.0 + jnp.exp(-x))


def _gelu_tanh(x):
    k = math.sqrt(2.0 / math.pi)
    t = jnp.tanh(k * (x + 0.044715 * x * x * x))
    return 0.5 * x * (1.0 + t), t


def _gelu_tanh_grad(x, t):
    k = math.sqrt(2.0 / math.pi)
    return 0.5 * (1.0 + t) + 0.5 * x * (1.0 - t * t) * k * (1.0 + 3.0 * 0.044715 * x * x)


def _log1p(e):
    u = 1.0 + e
    return jnp.where(u == 1.0, e, jnp.log(u) * (e / jnp.where(u == 1.0, 1.0, u - 1.0)))


def _softplus(x):
    return jnp.maximum(x, 0.0) + _log1p(jnp.exp(-jnp.abs(x)))


def _neg_expm1(x):
    series = -x * (1.0 + x * (0.5 + x * (1.0 / 6.0 + x * (1.0 / 24.0 + x * (1.0 / 120.0)))))
    return jnp.where(x > -0.1, series, 1.0 - jnp.exp(x))


def _my_place():
    return lax.axis_index("x"), lax.axis_index("y"), lax.axis_index("c")


def _all_gather(arrs, name):
    n = len(arrs)

    def body(*refs):
        ins, outs = refs[:n], refs[n:2 * n]
        send_sems, recv_sems, local_sems = refs[2 * n:]
        x, y, c = _my_place()
        sibling = (x, y, 1 - c)
        chips = [(1 - x, y), (x, 1 - y), (1 - x, 1 - y)]

        def slot(px, py, pc):
            return 4 * px + 2 * py + pc

        def copy(a, k, block, to, src=None):
            dst = outs[a].at[slot(*block)]
            return pltpu.make_async_remote_copy(
                src_ref=dst if src is None else src, dst_ref=dst,
                send_sem=send_sems.at[a, k], recv_sem=recv_sems.at[a, k],
                device_id=to, device_id_type=MESH)

        me = (x, y, c)
        mine = [pltpu.make_async_copy(ins[a], outs[a].at[slot(*me)], local_sems.at[a]) for a in range(n)]
        for cp in mine:
            cp.start()
        first = []
        for a in range(n):
            first.append(copy(a, 0, me, sibling, src=ins[a]))
            first += [copy(a, 1 + j, me, (*chip, c), src=ins[a]) for j, chip in enumerate(chips)]
        for cp in first:
            cp.start()
        passed = []
        for j, chip in enumerate(chips):
            for a in range(n):
                copy(a, 1 + j, (*chip, c), me).wait_recv()
                cp = copy(a, 4 + j, (*chip, c), sibling)
                cp.start()
                passed.append(cp)
        for a in range(n):
            copy(a, 0, sibling, me).wait_recv()
            for j, chip in enumerate(chips):
                copy(a, 4 + j, (*chip, 1 - c), me).wait_recv()
        for cp in first + passed:
            cp.wait_send()
        for cp in mine:
            cp.wait()

    any_spec = pl.BlockSpec(memory_space=pl.ANY)
    return pl.pallas_call(
        body, name=name,
        out_shape=[jax.ShapeDtypeStruct((N_DEV,) + a.shape, a.dtype) for a in arrs],
        in_specs=[any_spec] * n, out_specs=[any_spec] * n,
        scratch_shapes=[pltpu.SemaphoreType.DMA((n, 7)), pltpu.SemaphoreType.DMA((n, 7)),
                        pltpu.SemaphoreType.DMA((n,))],
    )(*arrs)


def _exchange_sibling(arrs, name):
    n = len(arrs)

    def body(*refs):
        ins, outs = refs[:n], refs[n:2 * n]
        send_sems, recv_sems = refs[2 * n:]
        x, y, c = _my_place()
        copies = []
        for a in range(n):
            for k in range(4):
                copies.append(pltpu.make_async_remote_copy(
                    src_ref=ins[a].at[2 * k + 1 - c], dst_ref=outs[a].at[k],
                    send_sem=send_sems.at[a, k], recv_sem=recv_sems.at[a, k],
                    device_id=(x, y, 1 - c), device_id_type=MESH))
        for cp in copies:
            cp.start()
        for cp in copies:
            cp.wait()

    any_spec = pl.BlockSpec(memory_space=pl.ANY)
    return pl.pallas_call(
        body, name=name,
        out_shape=[jax.ShapeDtypeStruct((4,) + a.shape[1:], a.dtype) for a in arrs],
        in_specs=[any_spec] * n, out_specs=[any_spec] * n,
        scratch_shapes=[pltpu.SemaphoreType.DMA((n, 4)), pltpu.SemaphoreType.DMA((n, 4))],
    )(*arrs)


def _exchange_chips(arrs, name):
    n = len(arrs)

    def body(*refs):
        ins, outs = refs[:n], refs[n:2 * n]
        send_sems, recv_sems = refs[2 * n:]
        x, y, c = _my_place()
        chips = [(1 - x, y), (x, 1 - y), (1 - x, 1 - y)]
        copies = []
        for a in range(n):
            for j, (px, py) in enumerate(chips):
                copies.append(pltpu.make_async_remote_copy(
                    src_ref=ins[a].at[2 * px + py], dst_ref=outs[a].at[j],
                    send_sem=send_sems.at[a, j], recv_sem=recv_sems.at[a, j],
                    device_id=(px, py, c), device_id_type=MESH))
        for cp in copies:
            cp.start()
        for cp in copies:
            cp.wait()

    any_spec = pl.BlockSpec(memory_space=pl.ANY)
    return pl.pallas_call(
        body, name=name,
        out_shape=[jax.ShapeDtypeStruct((3,) + a.shape[1:], a.dtype) for a in arrs],
        in_specs=[any_spec] * n, out_specs=[any_spec] * n,
        scratch_shapes=[pltpu.SemaphoreType.DMA((n, 3)), pltpu.SemaphoreType.DMA((n, 3))],
    )(*arrs)


def _pair_add(stack, from_sibling, c_idx, name):
    _, r, cc = stack.shape
    tr = _tile(r, 512)

    def body(c_ref, a_ref, b_ref, o_ref):
        o_ref[...] = (a_ref[...].astype(F32) + b_ref[...].astype(F32)).astype(o_ref.dtype)

    grid_spec = pltpu.PrefetchScalarGridSpec(
        num_scalar_prefetch=1, grid=(4, r // tr),
        in_specs=[pl.BlockSpec((None, tr, cc), lambda k, i, c_ref: (2 * k + c_ref[0], i, 0)),
                  pl.BlockSpec((None, tr, cc), lambda k, i, c_ref: (k, i, 0))],
        out_specs=pl.BlockSpec((None, tr, cc), lambda k, i, c_ref: (k, i, 0)))
    return pl.pallas_call(
        body, name=name, grid_spec=grid_spec,
        out_shape=jax.ShapeDtypeStruct((4, r, cc), BF16),
        compiler_params=_params(("parallel", "parallel")),
    )(c_idx, stack, from_sibling)


_DIMS = {"nn": ((1,), (0,)), "nt": ((1,), (1,)), "tn": ((0,), (0,))}


def _mm(a, b, mode, *, tm, tn, tk=None, outs, epi=None, tiles=(), rows=(), name):
    if mode == "tn":
        kk, m = a.shape
    else:
        m, kk = a.shape
    nn = b.shape[0] if mode == "nt" else b.shape[1]
    tk = kk if tk is None else tk
    assert m % tm == 0 and nn % tn == 0 and kk % tk == 0, (name, a.shape, b.shape, tm, tn, tk)
    nk = kk // tk
    nt, nr, no = len(tiles), len(rows), len(outs)

    def body(*refs):
        a_ref, b_ref = refs[:2]
        tile_refs = refs[2:2 + nt]
        row_refs = refs[2 + nt:2 + nt + nr]
        out_refs = refs[2 + nt + nr:2 + nt + nr + no]
        part = lax.dot_general(a_ref[...], b_ref[...], (_DIMS[mode], ((), ())), preferred_element_type=F32)

        def finish(acc):
            if epi is None:
                res = (acc,)
            else:
                res = epi(acc, *[t[...] for t in tile_refs], *[r[...] for r in row_refs])
            for o_ref, val in zip(out_refs, res):
                o_ref[...] = val.astype(o_ref.dtype)

        if nk == 1:
            finish(part)
        else:
            acc_ref = refs[-1]
            k = pl.program_id(2)

            @pl.when(k == 0)
            def _():
                acc_ref[...] = part

            @pl.when(k > 0)
            def _():
                acc_ref[...] += part

            @pl.when(k == nk - 1)
            def _():
                finish(acc_ref[...])

    if mode == "tn":
        a_spec = pl.BlockSpec((tk, tm), lambda i, j, k: (k, i))
    else:
        a_spec = pl.BlockSpec((tm, tk), lambda i, j, k: (i, k))
    if mode == "nt":
        b_spec = pl.BlockSpec((tn, tk), lambda i, j, k: (j, k))
    else:
        b_spec = pl.BlockSpec((tk, tn), lambda i, j, k: (k, j))
    tile_specs = [pl.BlockSpec((tm, tn), functools.partial(lambda i, j, k, off: (i, j + off), off=off))
                  for _, off in tiles]
    row_specs = [pl.BlockSpec((r.shape[0], tn), lambda i, j, k: (0, j)) for r in rows]
    out_spec = pl.BlockSpec((tm, tn), lambda i, j, k: (i, j))
    res = pl.pallas_call(
        body, name=name, grid=(m // tm, nn // tn, nk),
        in_specs=[a_spec, b_spec] + tile_specs + row_specs,
        out_specs=[out_spec] * no,
        out_shape=[jax.ShapeDtypeStruct((m, nn), dt) for dt in outs],
        scratch_shapes=[pltpu.VMEM((tm, tn), F32)] if nk > 1 else [],
        compiler_params=_params(("parallel", "parallel", "arbitrary")),
    )(a, b, *[t for t, _ in tiles], *rows)
    return res


def _ada_fwd(c_all, w_ada, b_ada_cols, name):
    nb, dd = c_all.shape
    ncol = w_ada.shape[1]
    tn = _tile(ncol, 512, LANE)

    def body(c_ref, w_ref, b_ref, o_ref):
        cv = c_ref[...]
        act = (cv * _sigmoid(cv)).astype(BF16)
        o_ref[...] = jnp.dot(act, w_ref[...].astype(BF16), preferred_element_type=F32) + b_ref[...]

    return pl.pallas_call(
        body, name=name, grid=(ncol // tn,),
        in_specs=[pl.BlockSpec((nb, dd), lambda j: (0, 0)), pl.BlockSpec((dd, tn), lambda j: (0, j)),
                  pl.BlockSpec((1, tn), lambda j: (0, j))],
        out_specs=pl.BlockSpec((nb, tn), lambda j: (0, j)),
        out_shape=jax.ShapeDtypeStruct((nb, ncol), F32),
        compiler_params=_params(("parallel",)),
    )(c_all, w_ada, b_ada_cols)


def _ada_bwd(c_all_t, dmod_cols, name):
    dd, nb = c_all_t.shape
    ncol = dmod_cols.shape[1]
    tr = _tile(dd, 256)

    def body(c_ref, d_ref, o_ref):
        cv = c_ref[...]
        act = (cv * _sigmoid(cv)).astype(BF16).astype(F32)
        dm = d_ref[...].astype(BF16).astype(F32)
        acc = act[:, 0:1] * dm[0:1, :]
        for bi in range(1, nb):
            acc = acc + act[:, bi:bi + 1] * dm[bi:bi + 1, :]
        o_ref[...] = acc

    return pl.pallas_call(
        body, name=name, grid=(dd // tr,),
        in_specs=[pl.BlockSpec((tr, nb), lambda i: (i, 0)), pl.BlockSpec((nb, ncol), lambda i: (0, 0))],
        out_specs=pl.BlockSpec((tr, ncol), lambda i: (i, 0)),
        out_shape=jax.ShapeDtypeStruct((dd, ncol), F32),
        compiler_params=_params(("parallel",)),
    )(c_all_t, dmod_cols)


def _norm_mod(x, gain, mod, row0, name):
    t, dd = x.shape
    tt = _tile(t, 512)

    def body(x_ref, g_ref, m_ref, h_ref):
        xv = x_ref[...]
        r = lax.rsqrt(jnp.mean(xv * xv, axis=-1, keepdims=True) + EPS)
        shift, scale = m_ref[row0:row0 + 1, :], m_ref[row0 + 1:row0 + 2, :]
        h_ref[...] = ((xv * r * g_ref[...]) * (1.0 + scale) + shift).astype(BF16)

    return pl.pallas_call(
        body, name=name, grid=(t // tt,),
        in_specs=[pl.BlockSpec((tt, dd), lambda i: (i, 0)), pl.BlockSpec((1, dd), lambda i: (0, 0)),
                  pl.BlockSpec((8, dd), lambda i: (0, 0))],
        out_specs=pl.BlockSpec((tt, dd), lambda i: (i, 0)),
        out_shape=jax.ShapeDtypeStruct((t, dd), BF16),
        compiler_params=_params(("parallel",)),
    )(x, gain, mod)


def _norm_mod_bwd(x, gain, mod, row0, dh, dres, branch, name):
    t, dd = x.shape
    tt = _tile(t, 256)
    has_branch = branch is not None

    def body(*refs):
        if has_branch:
            x_ref, g_ref, m_ref, dh_ref, dres_ref, o_ref, dx_ref, st_ref, do_ref = refs
        else:
            x_ref, g_ref, m_ref, dh_ref, dres_ref, dx_ref, st_ref = refs

        @pl.when(pl.program_id(0) == 0)
        def _():
            st_ref[...] = jnp.zeros_like(st_ref)

        xv = x_ref[...]
        g = g_ref[...]
        scale = m_ref[row0 + 1:row0 + 2, :]
        r = lax.rsqrt(jnp.mean(xv * xv, axis=-1, keepdims=True) + EPS)
        nv = xv * r
        dhv = dh_ref[...].astype(F32)
        st_ref[0:1, :] += jnp.sum(dhv, axis=0, keepdims=True)
        st_ref[1:2, :] += jnp.sum(dhv * (nv * g), axis=0, keepdims=True)
        dng = dhv * (1.0 + scale)
        st_ref[2:3, :] += jnp.sum(dng * nv, axis=0, keepdims=True)
        dn = dng * g
        dx = dres_ref[...] + r * (dn - nv * jnp.mean(dn * nv, axis=-1, keepdims=True))
        dx_ref[...] = dx
        if has_branch:
            ov = o_ref[...].astype(F32)
            gate = m_ref[branch[1]:branch[1] + 1, :]
            st_ref[3:4, :] += jnp.sum(dx * ov, axis=0, keepdims=True)
            do_ref[...] = (dx * gate).astype(BF16)

    row = pl.BlockSpec((tt, dd), lambda i: (i, 0))
    in_specs = [row, pl.BlockSpec((1, dd), lambda i: (0, 0)), pl.BlockSpec((8, dd), lambda i: (0, 0)), row, row]
    args = [x, gain, mod, dh, dres]
    out_specs = [row, pl.BlockSpec((8, dd), lambda i: (0, 0))]
    out_shape = [jax.ShapeDtypeStruct((t, dd), F32), jax.ShapeDtypeStruct((8, dd), F32)]
    if has_branch:
        in_specs.append(row)
        args.append(branch[0])
        out_specs.append(row)
        out_shape.append(jax.ShapeDtypeStruct((t, dd), BF16))
    return pl.pallas_call(
        body, name=name, grid=(t // tt,), in_specs=in_specs, out_specs=out_specs, out_shape=out_shape,
        compiler_params=_params(("arbitrary",)),
    )(*args)


def _final_loss(x2, o2, final_g, target, mod, name):
    t, dd = x2.shape
    tt = _tile(t, 256)
    nsteps = t // tt

    def body(x_ref, o_ref, g_ref, t_ref, m_ref, loss_ref, dx_ref, do_ref, st_ref, lacc):
        i = pl.program_id(0)

        @pl.when(i == 0)
        def _():
            st_ref[...] = jnp.zeros_like(st_ref)
            lacc[...] = jnp.zeros_like(lacc)

        xv = x_ref[...]
        g = g_ref[...]
        r = lax.rsqrt(jnp.mean(xv * xv, axis=-1, keepdims=True) + EPS)
        nv = xv * r
        err = nv * g - t_ref[...]
        lacc[...] += jnp.sum(err * err, axis=0, keepdims=True)
        dy = err * (1.0 / dd)
        st_ref[0:1, :] += jnp.sum(dy * nv, axis=0, keepdims=True)
        dn = dy * g
        dx = r * (dn - nv * jnp.mean(dn * nv, axis=-1, keepdims=True))
        dx_ref[...] = dx
        st_ref[1:2, :] += jnp.sum(dx * o_ref[...].astype(F32), axis=0, keepdims=True)
        do_ref[...] = (dx * m_ref[5:6, :]).astype(BF16)

        @pl.when(i == nsteps - 1)
        def _():
            total = jnp.sum(lacc[...], axis=-1, keepdims=True) * (0.5 / dd)
            loss_ref[...] = jnp.broadcast_to(total, loss_ref.shape)

    row = pl.BlockSpec((tt, dd), lambda i: (i, 0))
    return pl.pallas_call(
        body, name=name, grid=(nsteps,),
        in_specs=[row, row, pl.BlockSpec((1, dd), lambda i: (0, 0)), row, pl.BlockSpec((8, dd), lambda i: (0, 0))],
        out_specs=[pl.BlockSpec((8, LANE), lambda i: (0, 0)), row, row, pl.BlockSpec((8, dd), lambda i: (0, 0))],
        out_shape=[jax.ShapeDtypeStruct((8, LANE), F32), jax.ShapeDtypeStruct((t, dd), F32),
                   jax.ShapeDtypeStruct((t, dd), BF16), jax.ShapeDtypeStruct((8, dd), F32)],
        scratch_shapes=[pltpu.VMEM((1, dd), F32)],
        compiler_params=_params(("arbitrary",)),
    )(x2, o2, final_g, target, mod)


def _lru_gates(xt, prev8, lp, wa_ref, wx_ref, first_tile, cw):
    tt = xt.shape[0]
    ext = jnp.concatenate([prev8, xt], axis=0)
    xs = [pltpu.roll(ext, s, 0)[8:, :] if s else xt for s in range(CONV_WIDTH)]
    xc = lp[0:1, :] + lp[7:8, :] * xs[0] + lp[6:7, :] * xs[1] + lp[5:6, :] * xs[2] + lp[4:5, :] * xs[3]
    xcb = xc.astype(BF16)
    za, zi = [], []
    for jb in range(cw // LRU_BLOCK_W):
        sl = slice(jb * LRU_BLOCK_W, (jb + 1) * LRU_BLOCK_W)
        za.append(jnp.dot(xcb[:, sl], wa_ref[jb].astype(BF16), preferred_element_type=F32))
        zi.append(jnp.dot(xcb[:, sl], wx_ref[jb].astype(BF16), preferred_element_type=F32))
    ra = _sigmoid(jnp.concatenate(za, axis=1) + lp[1:2, :])
    ri = _sigmoid(jnp.concatenate(zi, axis=1) + lp[2:3, :])
    sp = _softplus(-lp[3:4, :])
    log_a = -LRU_C * ra * sp
    av = jnp.exp(log_a)
    seq_start = jnp.logical_and(first_tile, lax.broadcasted_iota(jnp.int32, (tt, 1), 0) == 0)
    mult = jnp.where(seq_start, 1.0, jnp.sqrt(_neg_expm1(2.0 * log_a)))
    return xs, xc, xcb, ra, ri, sp, av, mult, seq_start


def _lru_fwd(proj, lp, wa, wx, name):
    t = proj.shape[0]
    w = D_MODEL
    cw = LRU_CW
    tt = _tile(t, 256)
    nb = cw // LRU_BLOCK_W

    def body(lx_ref, lg_ref, lp_ref, wa_ref, wx_ref, rec_ref, ya_ref, a_s, u_s, h_s, halo, carry):
        ti = pl.program_id(1)

        @pl.when(ti == 0)
        def _():
            halo[...] = jnp.zeros_like(halo)
            carry[...] = jnp.zeros_like(carry)

        xt = lx_ref[...].astype(F32)
        lp_v = lp_ref[...]
        _, xc, _, _, ri, _, av, mult, _ = _lru_gates(xt, halo[...], lp_v, wa_ref, wx_ref, ti == 0, cw)
        halo[...] = xt[tt - 8:, :]
        a_s[...] = av
        u_s[...] = mult * (ri * xc)

        def step(s, h):
            h = a_s[pl.ds(s, 1), :] * h + u_s[pl.ds(s, 1), :]
            h_s[pl.ds(s, 1), :] = h
            return h

        carry[...] = lax.fori_loop(0, tt, step, carry[...], unroll=8)
        rec = h_s[...]
        rec_ref[...] = rec.astype(BF16)
        gelu, _ = _gelu_tanh(lg_ref[...].astype(F32))
        ya_ref[...] = (rec * gelu).astype(BF16)

    off_g = OFF_LG // cw
    return pl.pallas_call(
        body, name=name, grid=(w // cw, t // tt),
        in_specs=[pl.BlockSpec((tt, cw), lambda ci, ti: (ti, ci)),
                  pl.BlockSpec((tt, cw), lambda ci, ti: (ti, off_g + ci)),
                  pl.BlockSpec((8, cw), lambda ci, ti: (0, ci)),
                  pl.BlockSpec((nb, LRU_BLOCK_W, LRU_BLOCK_W), lambda ci, ti: (ci, 0, 0)),
                  pl.BlockSpec((nb, LRU_BLOCK_W, LRU_BLOCK_W), lambda ci, ti: (ci, 0, 0))],
        out_specs=[pl.BlockSpec((tt, cw), lambda ci, ti: (ti, ci))] * 2,
        out_shape=[jax.ShapeDtypeStruct((t, w), BF16)] * 2,
        scratch_shapes=[pltpu.VMEM((tt, cw), F32)] * 3 + [pltpu.VMEM((8, cw), F32), pltpu.VMEM((1, cw), F32)],
        compiler_params=_params(("parallel", "arbitrary")),
    )(proj, proj, lp, wa, wx)


def _lru_bwd(proj, rec, dya, lp, wa, wx, name):
    t = proj.shape[0]
    w = D_MODEL
    cw = LRU_CW
    tt = _tile(t, 256)
    nt = t // tt
    nb = cw // LRU_BLOCK_W
    r8 = tt // 8

    def body(lx_ref, lxp_ref, lg_ref, rec_ref, recp_ref, dya_ref, lp_ref, wa_ref, wx_ref,
             dlx_ref, dlg_ref, st_ref, dwa_ref, dwx_ref, a_s, d_s, dh_s, nhalo, carry):
        step_i = pl.program_id(1)
        ti = nt - 1 - step_i

        @pl.when(step_i == 0)
        def _():
            st_ref[...] = jnp.zeros_like(st_ref)
            dwa_ref[...] = jnp.zeros_like(dwa_ref)
            dwx_ref[...] = jnp.zeros_like(dwx_ref)
            nhalo[...] = jnp.zeros_like(nhalo)
            carry[...] = jnp.zeros_like(carry)

        first = ti == 0
        keep = jnp.where(first, 0.0, 1.0)
        xt = lx_ref[...].astype(F32)
        prev8 = lxp_ref[...].astype(F32) * keep
        lp_v = lp_ref[...]
        xs, xc, xcb, ra, ri, sp, av, mult, seq_start = _lru_gates(xt, prev8, lp_v, wa_ref, wx_ref, first, cw)

        lg = lg_ref[...].astype(F32)
        gelu, th = _gelu_tanh(lg)
        dyav = dya_ref[...].astype(F32)
        recv = rec_ref[...].astype(F32)
        dlg_ref[...] = (dyav * recv * _gelu_tanh_grad(lg, th)).astype(BF16)

        a_s[...] = av
        d_s[...] = dyav * gelu

        def step(s, e):
            row = tt - 1 - s
            dh = d_s[pl.ds(row, 1), :] + e
            dh_s[pl.ds(row, 1), :] = dh
            return a_s[pl.ds(row, 1), :] * dh

        carry[...] = lax.fori_loop(0, tt, step, carry[...], unroll=8)
        dh = dh_s[...]
        rprev8 = recp_ref[...].astype(F32) * keep
        hprev = pltpu.roll(jnp.concatenate([rprev8, recv], axis=0), 1, 0)[8:, :]
        da = dh * hprev
        dmult = jnp.where(seq_start, 0.0, dh * ri * xc)
        dri = dh * mult * xc
        dxc = dh * mult * ri
        dlog_a = da * av - dmult * (av * av) / mult
        dra = dlog_a * (-LRU_C * sp)
        lam = lp_v[3:4, :]
        st_ref[3:4, :] += jnp.sum(dlog_a * ra, axis=0, keepdims=True) * (LRU_C * _sigmoid(-lam))
        dza = dra * ra * (1.0 - ra)
        dzi = dri * ri * (1.0 - ri)
        st_ref[1:2, :] += jnp.sum(dza, axis=0, keepdims=True)
        st_ref[2:3, :] += jnp.sum(dzi, axis=0, keepdims=True)
        dzab, dzib = dza.astype(BF16), dzi.astype(BF16)
        back = []
        for jb in range(nb):
            sl = slice(jb * LRU_BLOCK_W, (jb + 1) * LRU_BLOCK_W)
            dwa_ref[jb] += lax.dot_general(xcb[:, sl], dzab[:, sl], (_DIMS["tn"], ((), ())), preferred_element_type=F32)
            dwx_ref[jb] += lax.dot_general(xcb[:, sl], dzib[:, sl], (_DIMS["tn"], ((), ())), preferred_element_type=F32)
            back.append(
                lax.dot_general(dzab[:, sl], wa_ref[jb].astype(BF16), (_DIMS["nt"], ((), ())), preferred_element_type=F32)
                + lax.dot_general(dzib[:, sl], wx_ref[jb].astype(BF16), (_DIMS["nt"], ((), ())), preferred_element_type=F32))
        dxc = dxc + jnp.concatenate(back, axis=1)
        st_ref[0:1, :] += jnp.sum(dxc, axis=0, keepdims=True)
        for k in range(CONV_WIDTH):
            st_ref[4 + k:5 + k, :] += jnp.sum(dxc * xs[CONV_WIDTH - 1 - k], axis=0, keepdims=True)
        ext = jnp.concatenate([dxc, nhalo[...]], axis=0)
        dlx = lp_v[7:8, :] * dxc
        for s in range(1, CONV_WIDTH):
            dlx = dlx + lp_v[7 - s:8 - s, :] * pltpu.roll(ext, tt + 8 - s, 0)[:tt, :]
        dlx_ref[...] = dlx.astype(BF16)
        nhalo[...] = dxc[0:8, :]

    off_g = OFF_LG // cw
    tile = lambda off: pl.BlockSpec((tt, cw), lambda ci, si: (nt - 1 - si, off + ci))
    prev = lambda off: pl.BlockSpec((8, cw), lambda ci, si: (jnp.maximum((nt - 1 - si) * r8 - 1, 0), off + ci))
    wspec = pl.BlockSpec((nb, LRU_BLOCK_W, LRU_BLOCK_W), lambda ci, si: (ci, 0, 0))
    st_spec = pl.BlockSpec((8, cw), lambda ci, si: (0, ci))
    return pl.pallas_call(
        body, name=name, grid=(w // cw, nt),
        in_specs=[tile(0), prev(0), tile(off_g), tile(0), prev(0), tile(0), st_spec, wspec, wspec],
        out_specs=[tile(0), tile(0), st_spec, wspec, wspec],
        out_shape=[jax.ShapeDtypeStruct((t, w), BF16), jax.ShapeDtypeStruct((t, w), BF16),
                   jax.ShapeDtypeStruct((8, w), F32),
                   jax.ShapeDtypeStruct((LRU_BLOCKS, LRU_BLOCK_W, LRU_BLOCK_W), F32),
                   jax.ShapeDtypeStruct((LRU_BLOCKS, LRU_BLOCK_W, LRU_BLOCK_W), F32)],
        scratch_shapes=[pltpu.VMEM((tt, cw), F32)] * 3 + [pltpu.VMEM((8, cw), F32), pltpu.VMEM((1, cw), F32)],
        compiler_params=_params(("parallel", "arbitrary")),
    )(proj, proj, proj, rec, rec, dya, lp, wa, wx)


def _band_valid(first_block):
    qi = lax.broadcasted_iota(jnp.int32, (BLOCK, 2 * BLOCK), 0)
    ki = lax.broadcasted_iota(jnp.int32, (BLOCK, 2 * BLOCK), 1)
    rel = qi + BLOCK - ki
    valid = jnp.logical_and(rel >= 0, rel < BLOCK)
    return jnp.logical_and(valid, jnp.logical_or(ki >= BLOCK, jnp.logical_not(first_block)))


def _attn_probs(qh, kk, bias_h, sink, valid):
    s = lax.dot_general(qh, kk, (_DIMS["nt"], ((), ())), preferred_element_type=F32) * (HEAD_DIM ** -0.5)
    s = jnp.where(valid, s + bias_h, NEG_INF)
    m = jnp.maximum(jnp.max(s, axis=-1, keepdims=True), sink)
    e = jnp.exp(s - m)
    es = jnp.exp(sink - m)
    inv = 1.0 / (jnp.sum(e, axis=-1, keepdims=True) + es)
    return e * inv, es * inv


def _attn_specs(t):
    nblk = t // BLOCK
    q_spec = pl.BlockSpec((BLOCK, D_MODEL), lambda n: (n, OFF_Q // D_MODEL))
    kc = pl.BlockSpec((BLOCK, KV_WIDTH), lambda n: (n, OFF_K // KV_WIDTH))
    kp = pl.BlockSpec((BLOCK, KV_WIDTH), lambda n: (jnp.maximum(n - 1, 0), OFF_K // KV_WIDTH))
    vc = pl.BlockSpec((BLOCK, KV_WIDTH), lambda n: (n, OFF_V // KV_WIDTH))
    vp = pl.BlockSpec((BLOCK, KV_WIDTH), lambda n: (jnp.maximum(n - 1, 0), OFF_V // KV_WIDTH))
    bias_spec = pl.BlockSpec((N_Q_HEADS, BLOCK, 2 * BLOCK), lambda n: (0, 0, 0))
    sink_spec = pl.BlockSpec(memory_space=pltpu.SMEM)
    return nblk, [q_spec, kp, kc, vp, vc, bias_spec, sink_spec]


def _attn_fwd(proj, bias, sinks, name):
    t = proj.shape[0]
    nblk, in_specs = _attn_specs(t)

    def body(q_ref, kp_ref, kc_ref, vp_ref, vc_ref, b_ref, s_ref, o_ref):
        valid = _band_valid(pl.program_id(0) == 0)
        q = q_ref[...]
        kall = jnp.concatenate([kp_ref[...], kc_ref[...]], axis=0)
        vall = jnp.concatenate([vp_ref[...], vc_ref[...]], axis=0)
        outs = []
        for h in range(N_Q_HEADS):
            kv = h // GROUP
            kk = kall[:, kv * HEAD_DIM:(kv + 1) * HEAD_DIM]
            vv = vall[:, kv * HEAD_DIM:(kv + 1) * HEAD_DIM]
            p, _ = _attn_probs(q[:, h * HEAD_DIM:(h + 1) * HEAD_DIM], kk, b_ref[h], s_ref[h], valid)
            outs.append(jnp.dot(p.astype(BF16), vv, preferred_element_type=F32))
        o_ref[...] = jnp.concatenate(outs, axis=1).astype(BF16)

    return pl.pallas_call(
        body, name=name, grid=(nblk,), in_specs=in_specs,
        out_specs=pl.BlockSpec((BLOCK, D_MODEL), lambda n: (n, 0)),
        out_shape=jax.ShapeDtypeStruct((t, D_MODEL), BF16),
        compiler_params=_params(("parallel",)),
    )(proj, proj, proj, proj, proj, bias, sinks)


def _attn_bwd(proj, datt, bias, sinks, name):
    t = proj.shape[0]
    nblk, in_specs = _attn_specs(t)
    in_specs = in_specs + [pl.BlockSpec((BLOCK, D_MODEL), lambda n: (n, 0))]
    scale = HEAD_DIM ** -0.5

    def body(q_ref, kp_ref, kc_ref, vp_ref, vc_ref, b_ref, s_ref, do_ref, dq_ref, dk_ref, dv_ref, db_ref, ds_ref):
        n = pl.program_id(0)

        @pl.when(n == 0)
        def _():
            dk_ref[...] = jnp.zeros_like(dk_ref)
            dv_ref[...] = jnp.zeros_like(dv_ref)
            db_ref[...] = jnp.zeros_like(db_ref)
            ds_ref[...] = jnp.zeros_like(ds_ref)

        valid = _band_valid(n == 0)
        lane = lax.broadcasted_iota(jnp.int32, (1, LANE), 1)
        q = q_ref[...]
        do = do_ref[...]
        kall = jnp.concatenate([kp_ref[...], kc_ref[...]], axis=0)
        vall = jnp.concatenate([vp_ref[...], vc_ref[...]], axis=0)
        dqs, dks, dvs = [], [], []
        dsink_row = jnp.zeros((1, LANE), F32)
        for kv in range(N_KV_HEADS):
            kk = kall[:, kv * HEAD_DIM:(kv + 1) * HEAD_DIM]
            vv = vall[:, kv * HEAD_DIM:(kv + 1) * HEAD_DIM]
            dk_acc = jnp.zeros((2 * BLOCK, HEAD_DIM), F32)
            dv_acc = jnp.zeros((2 * BLOCK, HEAD_DIM), F32)
            for g in range(GROUP):
                h = kv * GROUP + g
                qh = q[:, h * HEAD_DIM:(h + 1) * HEAD_DIM]
                doh = do[:, h * HEAD_DIM:(h + 1) * HEAD_DIM]
                p, psink = _attn_probs(qh, kk, b_ref[h], s_ref[h], valid)
                dp = lax.dot_general(doh, vv, (_DIMS["nt"], ((), ())), preferred_element_type=F32)
                delta = jnp.sum(p * dp, axis=-1, keepdims=True)
                ds = p * (dp - delta)
                db_ref[h] += ds
                dsink = -jnp.sum(psink * delta, axis=0, keepdims=True)
                dsink_row = dsink_row + jnp.where(lane == h, dsink, 0.0)
                dsb = ds.astype(BF16)
                dqs.append(jnp.dot(dsb, kk, preferred_element_type=F32) * scale)
                dk_acc = dk_acc + lax.dot_general(dsb, qh, (_DIMS["tn"], ((), ())), preferred_element_type=F32)
                dv_acc = dv_acc + lax.dot_general(p.astype(BF16), doh, (_DIMS["tn"], ((), ())), preferred_element_type=F32)
            dks.append(dk_acc * scale)
            dvs.append(dv_acc)
        dq_ref[...] = jnp.concatenate(dqs, axis=1).astype(BF16)
        rows = pl.ds(pl.multiple_of(n * BLOCK, BLOCK), 2 * BLOCK)
        dk_ref[rows, :] += jnp.concatenate(dks, axis=1)
        dv_ref[rows, :] += jnp.concatenate(dvs, axis=1)
        ds_ref[0:1, :] += dsink_row

    return pl.pallas_call(
        body, name=name, grid=(nblk,), in_specs=in_specs,
        out_specs=[pl.BlockSpec((BLOCK, D_MODEL), lambda n: (n, 0)),
                   pl.BlockSpec((t + BLOCK, KV_WIDTH), lambda n: (0, 0)),
                   pl.BlockSpec((t + BLOCK, KV_WIDTH), lambda n: (0, 0)),
                   pl.BlockSpec((N_Q_HEADS, BLOCK, 2 * BLOCK), lambda n: (0, 0, 0)),
                   pl.BlockSpec((8, LANE), lambda n: (0, 0))],
        out_shape=[jax.ShapeDtypeStruct((t, D_MODEL), BF16),
                   jax.ShapeDtypeStruct((t + BLOCK, KV_WIDTH), F32),
                   jax.ShapeDtypeStruct((t + BLOCK, KV_WIDTH), F32),
                   jax.ShapeDtypeStruct((N_Q_HEADS, BLOCK, 2 * BLOCK), F32),
                   jax.ShapeDtypeStruct((8, LANE), F32)],
        compiler_params=_params(("arbitrary",)),
    )(proj, proj, proj, proj, proj, bias, sinks, datt)


def _bucket_table():
    qi = np.arange(BLOCK)[:, None]
    ki = np.arange(2 * BLOCK)[None, :]
    rel = np.maximum(qi + BLOCK - ki, 0)
    max_exact = N_BUCKETS // 2
    relf = np.maximum(rel, 1).astype(np.float32)
    large = max_exact + (np.log(relf / max_exact) / math.log(MAX_DISTANCE / max_exact)
                         * (N_BUCKETS - max_exact)).astype(np.int32)
    large = np.minimum(large, N_BUCKETS - 1)
    return np.where(rel < max_exact, rel, large).astype(np.int32)


def _rel_bias_bwd(dbias, onehot, name):
    def body(d_ref, o_ref, out_ref):
        out_ref[...] = lax.dot_general(d_ref[...], o_ref[...], (_DIMS["nt"], ((), ())),
                                       precision=lax.Precision.HIGHEST, preferred_element_type=F32)

    full = pl.BlockSpec(dbias.shape, lambda: (0, 0))
    return pl.pallas_call(
        body, name=name, in_specs=[full, full],
        out_specs=pl.BlockSpec((N_Q_HEADS, N_BUCKETS), lambda: (0, 0)),
        out_shape=jax.ShapeDtypeStruct((N_Q_HEADS, N_BUCKETS), F32),
        compiler_params=_params(()),
    )(dbias, onehot)


def _adamw(w, m, v, gparts, name, part_index=None):
    r, cc = w.shape
    tr = _tile(r, 128)
    np_ = len(gparts)
    part_index = part_index or {}
    dyn = [i for i, ix in part_index.items() if not isinstance(ix, int)]
    assert len(dyn) <= 1
    bc1 = 1.0 / (1.0 - ADAM_B1 ** ADAM_STEP)
    bc2 = 1.0 / (1.0 - ADAM_B2 ** ADAM_STEP)

    def body(*refs):
        refs = refs[len(dyn):]
        w_ref, m_ref, v_ref = refs[:3]
        g_refs = refs[3:3 + np_]
        g_out, d_out, m_out, v_out = refs[3 + np_:]
        g = g_refs[0][...].astype(F32)
        for gr in g_refs[1:]:
            g = g + gr[...].astype(F32)
        mn = ADAM_B1 * m_ref[...] + (1.0 - ADAM_B1) * g
        vn = ADAM_B2 * v_ref[...] + (1.0 - ADAM_B2) * (g * g)
        g_out[...] = g
        m_out[...] = mn
        v_out[...] = vn
        d_out[...] = -ADAM_LR * ((mn * bc1) / (jnp.sqrt(vn * bc2) + ADAM_EPS) + ADAM_WD * w_ref[...])

    blk = pl.BlockSpec((tr, cc), lambda i, *_: (i, 0))
    g_specs = []
    for i, gp in enumerate(gparts):
        if i in part_index:
            ix = part_index[i]
            if isinstance(ix, int):
                g_specs.append(pl.BlockSpec((None, tr, cc), functools.partial(lambda i, *_, ix: (ix, i, 0), ix=ix)))
            else:
                g_specs.append(pl.BlockSpec((None, tr, cc), lambda i, ix_ref: (ix_ref[0], i, 0)))
        else:
            g_specs.append(blk)
    grid_spec = pltpu.PrefetchScalarGridSpec(
        num_scalar_prefetch=len(dyn), grid=(r // tr,),
        in_specs=[blk, blk, blk] + g_specs, out_specs=[blk] * 4)
    return pl.pallas_call(
        body, name=name, grid_spec=grid_spec,
        out_shape=[jax.ShapeDtypeStruct((r, cc), F32)] * 4,
        compiler_params=_params(("parallel",)),
    )(*[part_index[i] for i in dyn], w, m, v, *gparts)


def _row(vec):
    return vec.reshape(1, -1)


def kernel(x, c, w_ada, b_ada, norm1_g, w_in, conv_w, conv_b, lru_wa, lru_ba, lru_wx, lru_bx, lru_lambda, w_lru_out, w_attn_out, attn_sinks, rel_bias, w_out, norm2_g, w_ff1, w_ff2, final_g, loss_target, m_w_ada, m_b_ada, m_norm1_g, m_w_in, m_conv_w, m_conv_b, m_lru_wa, m_lru_ba, m_lru_wx, m_lru_bx, m_lru_lambda, m_w_lru_out, m_w_attn_out, m_attn_sinks, m_rel_bias, m_w_out, m_norm2_g, m_w_ff1, m_w_ff2, m_final_g, v_w_ada, v_b_ada, v_norm1_g, v_w_in, v_conv_w, v_conv_b, v_lru_wa, v_lru_ba, v_lru_wx, v_lru_bx, v_lru_lambda, v_w_lru_out, v_w_attn_out, v_attn_sinks, v_rel_bias, v_w_out, v_norm2_g, v_w_ff1, v_w_ff2, v_final_g):
    dd = D_MODEL
    t = x.shape[1]
    ix, iy, ic = lax.axis_index("x"), lax.axis_index("y"), lax.axis_index("c")
    me = 4 * ix + 2 * iy + ic
    chip = 2 * ix + iy
    c_idx = jnp.reshape(ic, (1,)).astype(jnp.int32)
    chip_idx = jnp.reshape(chip, (1,)).astype(jnp.int32)

    xs = x[0]
    target = loss_target[0]
    ncol_ada = w_ada.shape[2]
    cshard = conv_w.shape[2]

    shards = [jnp.transpose(w_in[0]).astype(BF16), w_lru_out[0].astype(BF16), w_attn_out[0].astype(BF16),
              w_out[0].astype(BF16), jnp.transpose(w_ff1[0]).astype(BF16), w_ff2[0].astype(BF16)]
    gathered = _all_gather(shards, "ag_weights")
    win_t, wlo, wao, wo, wf1_t, wf2 = [g.reshape(N_DEV * g.shape[1], dd) for g in gathered]

    pack0 = jnp.zeros((8, dd), F32).at[0:1, :].set(c).at[1:1 + CONV_WIDTH, 0:cshard].set(conv_w[0])
    (g0,) = _all_gather([pack0], "ag_cond")
    c_all = g0[:, 0, :]
    conv_w_full = jnp.transpose(g0[:, 1:1 + CONV_WIDTH, 0:cshard], (1, 0, 2)).reshape(CONV_WIDTH, dd)
    b_cols = lax.dynamic_slice(b_ada, (0, me * ncol_ada), (1, ncol_ada))
    mod_cols = _ada_fwd(c_all, w_ada[0], b_cols, "ada_fwd")
    (g1,) = _all_gather([mod_cols], "ag_mod")
    mod_mine = lax.dynamic_index_in_dim(g1, me, axis=1, keepdims=False).reshape(6, dd)
    mod = jnp.concatenate([mod_mine, jnp.zeros((2, dd), F32)], axis=0)

    bucket = _bucket_table()
    bias = jnp.transpose(rel_bias[bucket], (2, 0, 1)).astype(F32)
    sinks = attn_sinks.reshape(N_Q_HEADS)
    lp = jnp.concatenate([conv_b, lru_ba, lru_bx, lru_lambda, conv_w_full], axis=0)
    wa, wx = lru_wa[0], lru_wx[0]

    tm = _tile(t, 1024)
    tn = 512
    h = _norm_mod(xs, norm1_g, mod, 0, "norm1")
    (proj,) = _mm(h, win_t, "nt", tm=tm, tn=tn, outs=[BF16], name="in_proj")
    rec, ya = _lru_fwd(proj, lp, wa, wx, "lru_fwd")
    att = _attn_fwd(proj, bias, sinks, "attn_fwd")
    (y_a,) = _mm(ya, wlo, "nn", tm=tm, tn=tn, outs=[BF16], name="lru_out")

    def merge_epi(acc, ya_t, ga_t, gb_t):
        return acc, _sigmoid(ga_t.astype(F32)) * ya_t.astype(F32) + _sigmoid(gb_t.astype(F32)) * acc

    y_b, merged = _mm(att, wao, "nn", tm=tm, tn=tn, outs=[BF16, BF16], epi=merge_epi,
                      tiles=[(y_a, 0), (proj, OFF_GA // tn), (proj, OFF_GB // tn)], name="attn_out_merge")

    def resid_epi(row):
        def epi(acc, x_t, mod_t):
            return acc, x_t + mod_t[row:row + 1, :] * acc
        return epi

    o1, x1 = _mm(merged, wo, "nn", tm=tm, tn=tn, outs=[BF16, F32], epi=resid_epi(2),
                 tiles=[(xs, 0)], rows=[mod], name="out_proj")
    h2 = _norm_mod(x1, norm2_g, mod, 3, "norm2")

    def relu2_epi(acc):
        rl = jnp.maximum(acc, 0.0)
        return acc, rl * rl

    f_pre, ff = _mm(h2, wf1_t, "nt", tm=tm, tn=tn, outs=[BF16, BF16], epi=relu2_epi, name="ff1")
    o2, x2 = _mm(ff, wf2, "nn", tm=tm, tn=tn, tk=2048, outs=[BF16, F32], epi=resid_epi(5),
                 tiles=[(x1, 0)], rows=[mod], name="ff2")

    loss_blk, dx2, do2, st_f = _final_loss(x2, o2, _row(final_g), target, mod, "final_loss")
    loss = lax.psum(loss_blk[0, 0], ("x", "y", "c"))

    def drelu2_epi(acc, f_t):
        return (acc * (2.0 * jnp.maximum(f_t.astype(F32), 0.0)),)

    tmw = 512
    (df,) = _mm(do2, wf2, "nt", tm=tm, tn=tn, outs=[BF16], epi=drelu2_epi, tiles=[(f_pre, 0)], name="ff2_dx")
    (dwf2,) = _mm(ff, do2, "tn", tm=tmw, tn=tn, outs=[BF16], name="ff2_dw")
    (dh2,) = _mm(df, wf1_t, "nn", tm=tm, tn=tn, tk=2048, outs=[F32], name="ff1_dx")
    (dwf1_t,) = _mm(df, h2, "tn", tm=tmw, tn=tn, outs=[BF16], name="ff1_dw")
    dx1, st_2, do1 = _norm_mod_bwd(x1, norm2_g, mod, 3, dh2, dx2, (o1, 2), "norm2_bwd")

    def dmerge_epi(acc, ga_t, gb_t, ya_t, yb_t):
        sa, sb = _sigmoid(ga_t.astype(F32)), _sigmoid(gb_t.astype(F32))
        return (acc * sa, acc * sb, acc * ya_t.astype(F32) * sa * (1.0 - sa), acc * yb_t.astype(F32) * sb * (1.0 - sb))

    tmh = _tile(t, 512)
    dy_a, dy_b, dga, dgb = _mm(do1, wo, "nt", tm=tmh, tn=tn, outs=[BF16] * 4, epi=dmerge_epi,
                               tiles=[(proj, OFF_GA // tn), (proj, OFF_GB // tn), (y_a, 0), (y_b, 0)], name="out_proj_dx")
    (dwo,) = _mm(merged, do1, "tn", tm=tmw, tn=tn, outs=[BF16], name="out_proj_dw")
    (datt,) = _mm(dy_b, wao, "nt", tm=tm, tn=tn, outs=[BF16], name="attn_out_dx")
    (dwao,) = _mm(att, dy_b, "tn", tm=tmw, tn=tn, outs=[BF16], name="attn_out_dw")
    (dya,) = _mm(dy_a, wlo, "nt", tm=tm, tn=tn, outs=[BF16], name="lru_out_dx")
    (dwlo,) = _mm(ya, dy_a, "tn", tm=tmw, tn=tn, outs=[BF16], name="lru_out_dw")
    dq, dk_pad, dv_pad, dbias, dsinks = _attn_bwd(proj, datt, bias, sinks, "attn_bwd")
    dlx, dlg, st_l, dwa, dwx = _lru_bwd(proj, rec, dya, lp, wa, wx, "lru_bwd")
    dproj = jnp.concatenate([dlx, dlg, dq, dk_pad[BLOCK:].astype(BF16), dv_pad[BLOCK:].astype(BF16), dga, dgb], axis=1)
    (dh,) = _mm(dproj, win_t, "nn", tm=tm, tn=tn, tk=IN_WIDTH // 4, outs=[F32], name="in_proj_dx")
    (dwin_t,) = _mm(dproj, h, "tn", tm=384, tn=tn, outs=[BF16], name="in_proj_dw")
    grad_x, st_1 = _norm_mod_bwd(xs, norm1_g, mod, 0, dh, dx1, None, "norm1_bwd")

    onehot = jnp.asarray((bucket.reshape(1, -1) == np.arange(N_BUCKETS)[:, None]).astype(np.float32))
    drel = jnp.transpose(_rel_bias_bwd(dbias.reshape(N_Q_HEADS, -1), onehot, "rel_bias_bwd"))

    misc = jnp.concatenate([drel.reshape(1, -1), dsinks[0:1, 0:N_Q_HEADS],
                            jnp.zeros((1, dd - N_BUCKETS * N_Q_HEADS - N_Q_HEADS), F32)], axis=1)
    zero_row = jnp.zeros((1, dd), F32)
    small = jnp.concatenate([
        st_1[0:2], st_2[3:4], st_2[0:2], st_f[1:2], zero_row, zero_row,
        st_1[2:3], st_l[0:1], st_l[1:2], st_l[2:3], st_l[3:4], st_2[2:3], st_f[0:1], misc,
        st_l[4:8], jnp.zeros((4, dd), F32),
        dwa.reshape(-1, dd), dwx.reshape(-1, dd)], axis=0)
    nsmall = small.shape[0]
    (small_all,) = _all_gather([small], "ag_small")

    def pack_small(b_, n1, cb, ba, bx, lam, n2, fg, rb, sk, wa_, wx_):
        misc_ = jnp.concatenate([rb.reshape(1, -1), sk.reshape(1, -1),
                                 jnp.zeros((1, dd - N_BUCKETS * N_Q_HEADS - N_Q_HEADS), F32)], axis=1)
        return jnp.concatenate([b_.reshape(6, dd), jnp.zeros((2, dd), F32), n1, cb, ba, bx, lam, n2, _row(fg), misc_,
                                jnp.zeros((8, dd), F32), wa_.reshape(-1, dd), wx_.reshape(-1, dd)], axis=0)

    w_s = pack_small(b_ada, norm1_g, conv_b, lru_ba, lru_bx, lru_lambda, norm2_g, final_g, rel_bias, attn_sinks, lru_wa, lru_wx)
    m_s = pack_small(m_b_ada, m_norm1_g, m_conv_b, m_lru_ba, m_lru_bx, m_lru_lambda, m_norm2_g, m_final_g, m_rel_bias, m_attn_sinks, m_lru_wa, m_lru_wx)
    v_s = pack_small(v_b_ada, v_norm1_g, v_conv_b, v_lru_ba, v_lru_bx, v_lru_lambda, v_norm2_g, v_final_g, v_rel_bias, v_attn_sinks, v_lru_wa, v_lru_wx)
    small_res = _adamw(w_s, m_s, v_s, [small_all] * N_DEV, "adamw_small", part_index={i: i for i in range(N_DEV)})

    def unpack_small(s):
        nb_ = N_BUCKETS * N_Q_HEADS
        return dict(
            b_ada=s[0:6].reshape(1, 6 * dd), norm1_g=s[8:9], conv_b=s[9:10], lru_ba=s[10:11], lru_bx=s[11:12],
            lru_lambda=s[12:13], norm2_g=s[13:14], final_g=s[14], rel_bias=s[15, 0:nb_].reshape(N_BUCKETS, N_Q_HEADS),
            attn_sinks=s[15:16, nb_:nb_ + N_Q_HEADS],
            lru_wa=s[24:24 + 128].reshape(1, LRU_BLOCKS, LRU_BLOCK_W, LRU_BLOCK_W),
            lru_wx=s[152:152 + 128].reshape(1, LRU_BLOCKS, LRU_BLOCK_W, LRU_BLOCK_W))

    res = {k: [None] * 4 for k in ("w_ada", "w_in", "conv_w", "w_lru_out", "w_attn_out", "w_out", "w_ff1", "w_ff2")}
    for q_, s_ in enumerate(small_res):
        for k_, val in unpack_small(s_).items():
            res.setdefault(k_, [None] * 4)[q_] = val

    g_conv = lax.dynamic_slice(small_res[0][16:16 + CONV_WIDTH], (0, me * cshard), (CONV_WIDTH, cshard))
    conv_res = _adamw(conv_w[0], m_conv_w[0], v_conv_w[0], [g_conv], "adamw_conv")
    res["conv_w"] = [r_[None] for r_ in conv_res]

    dmod_all = small_all[:, 0:6, :].reshape(N_DEV, 6 * dd)
    dmod_cols = lax.dynamic_slice(dmod_all, (0, me * ncol_ada), (N_DEV, ncol_ada))
    g_ada = _ada_bwd(jnp.transpose(c_all), dmod_cols, "ada_bwd")
    ada_res = _adamw(w_ada[0], m_w_ada[0], v_w_ada[0], [g_ada], "adamw_ada")
    res["w_ada"] = [r_[None] for r_ in ada_res]

    stacks = [g.reshape(N_DEV, g.shape[0] // N_DEV, dd) for g in (dwin_t, dwlo, dwao, dwo, dwf1_t, dwf2)]
    from_sib = _exchange_sibling(stacks, "rs_sibling")
    pair = [_pair_add(s_, f_, c_idx, "rs_pair_add_%d" % i) for i, (s_, f_) in enumerate(zip(stacks, from_sib))]
    from_chips = _exchange_chips(pair, "rs_chips")

    def summed(i):
        return [pair[i], from_chips[i], from_chips[i], from_chips[i]], {0: chip_idx, 1: 0, 2: 1, 3: 2}

    def sum_only(i, name):
        parts, index = summed(i)
        r_, cc_ = parts[0].shape[1:]
        zero = jnp.zeros((r_, cc_), F32)
        return _adamw(zero, zero, zero, parts, name, part_index=index)[0]

    g_in = jnp.transpose(sum_only(0, "sum_w_in"))
    res["w_in"] = [r_[None] for r_ in _adamw(w_in[0], m_w_in[0], v_w_in[0], [g_in], "adamw_w_in")]
    g_ff1 = jnp.transpose(sum_only(4, "sum_w_ff1"))
    res["w_ff1"] = [r_[None] for r_ in _adamw(w_ff1[0], m_w_ff1[0], v_w_ff1[0], [g_ff1], "adamw_w_ff1")]
    for i, (nm, w_, m_, v_) in {1: ("w_lru_out", w_lru_out, m_w_lru_out, v_w_lru_out),
                                2: ("w_attn_out", w_attn_out, m_w_attn_out, v_w_attn_out),
                                3: ("w_out", w_out, m_w_out, v_w_out),
                                5: ("w_ff2", w_ff2, m_w_ff2, v_w_ff2)}.items():
        parts, index = summed(i)
        res[nm] = [r_[None] for r_ in _adamw(w_[0], m_[0], v_[0], parts, "adamw_" + nm, part_index=index)]

    order = ["w_ada", "b_ada", "norm1_g", "w_in", "conv_w", "conv_b", "lru_wa", "lru_ba", "lru_wx", "lru_bx",
             "lru_lambda", "w_lru_out", "w_attn_out", "attn_sinks", "rel_bias", "w_out", "norm2_g", "w_ff1",
             "w_ff2", "final_g"]
    out = [loss, grad_x[None]]
    for q_ in range(4):
        out += [res[k_][q_] for k_ in order]
    return tuple(out)
```

```python
import functools
import math

import numpy as np
import jax
import jax.numpy as jnp
from jax import lax
from jax.experimental import pallas as pl
from jax.experimental.pallas import tpu as pltpu

F32 = jnp.float32
BF16 = jnp.bfloat16
MESH = pl.DeviceIdType.MESH

D_MODEL = 2048
N_Q_HEADS = 32
N_KV_HEADS = 4
GROUP = N_Q_HEADS // N_KV_HEADS
HEAD_DIM = 64
KV_WIDTH = N_KV_HEADS * HEAD_DIM
BLOCK = 128
NEG_INF = -1e30
N_BUCKETS = 32
MAX_DISTANCE = 128
LRU_BLOCKS = 16
LRU_BLOCK_W = 128
CONV_WIDTH = 4
LRU_C = 8.0
D_FF = 4 * D_MODEL
EPS = 1e-6
IN_WIDTH = 5 * D_MODEL + 2 * KV_WIDTH
OFF_LX, OFF_LG, OFF_Q, OFF_K, OFF_V, OFF_GA, OFF_GB = 0, 2048, 4096, 6144, 6400, 6656, 8704

ADAM_LR, ADAM_B1, ADAM_B2, ADAM_EPS, ADAM_WD, ADAM_STEP = 0.001, 0.9, 0.999, 1e-08, 0.01, 10

N_DEV = 8
VMEM_LIMIT_BYTES = 48 * 1024 * 1024
LANE = 128
LRU_CW = 512


def _params(sem, **kw):
    return pltpu.CompilerParams(dimension_semantics=sem, vmem_limit_bytes=VMEM_LIMIT_BYTES, **kw)


def _tile(n, pref, mult=8):
    if n <= pref:
        return n
    t = (pref // mult) * mult
    while t >= mult:
        if n % t == 0:
            return t
        t -= mult
    return n


def _sigmoid(x):
    return 1.0 / (1.0 + jnp.exp(-x))


def _gelu_tanh(x):
    k = math.sqrt(2.0 / math.pi)
    t = jnp.tanh(k * (x + 0.044715 * x * x * x))
    return 0.5 * x * (1.0 + t), t


def _gelu_tanh_grad(x, t):
    k = math.sqrt(2.0 / math.pi)
    return 0.5 * (1.0 + t) + 0.5 * x * (1.0 - t * t) * k * (1.0 + 3.0 * 0.044715 * x * x)


def _log1p(e):
    u = 1.0 + e
    return jnp.where(u == 1.0, e, jnp.log(u) * (e / jnp.where(u == 1.0, 1.0, u - 1.0)))


def _softplus(x):
    return jnp.maximum(x, 0.0) + _log1p(jnp.exp(-jnp.abs(x)))


def _neg_expm1(x):
    series = -x * (1.0 + x * (0.5 + x * (1.0 / 6.0 + x * (1.0 / 24.0 + x * (1.0 / 120.0)))))
    return jnp.where(x > -0.1, series, 1.0 - jnp.exp(x))


def _my_place():
    return lax.axis_index("x"), lax.axis_index("y"), lax.axis_index("c")


def _all_gather(arrs, name):
    return _run_job(_gather_job(arrs), name)


class _Job:
    def __init__(self, ins, outs, sems, start, finish, forward=None, forward_at=0.8):
        self.ins, self.outs, self.sems = list(ins), list(outs), list(sems)
        self.start, self.finish, self.forward, self.forward_at = start, finish, forward, forward_at


def _gather_job(arrs, forward_at=0.8):
    n = len(arrs)

    def copies(ins, outs, sems):
        send_sems, recv_sems, local_sems = sems
        x, y, c = _my_place()
        me, sibling = (x, y, c), (x, y, 1 - c)
        chips = [(1 - x, y), (x, 1 - y), (1 - x, 1 - y)]

        def slot(px, py, pc):
            return 4 * px + 2 * py + pc

        def copy(a, k, block, to, src=None):
            dst = outs[a].at[slot(*block)]
            return pltpu.make_async_remote_copy(
                src_ref=dst if src is None else src, dst_ref=dst,
                send_sem=send_sems.at[a, k], recv_sem=recv_sems.at[a, k],
                device_id=to, device_id_type=MESH)

        mine = [pltpu.make_async_copy(ins[a], outs[a].at[slot(*me)], local_sems.at[a]) for a in range(n)]
        first = []
        for a in range(n):
            first.append(copy(a, 0, me, sibling, src=ins[a]))
            first += [copy(a, 1 + j, me, (*chip, c), src=ins[a]) for j, chip in enumerate(chips)]
        relay = [(copy(a, 1 + j, (*chip, c), me), copy(a, 4 + j, (*chip, c), sibling))
                 for j, chip in enumerate(chips) for a in range(n)]
        last = [copy(a, 0, sibling, me) for a in range(n)]
        last += [copy(a, 4 + j, (*chip, 1 - c), me) for a in range(n) for j, chip in enumerate(chips)]
        return mine, first, relay, last

    def start(ins, outs, sems):
        mine, first, _, _ = copies(ins, outs, sems)
        for cp in mine + first:
            cp.start()

    def forward(ins, outs, sems):
        _, _, relay, _ = copies(ins, outs, sems)
        for arrival, onward in relay:
            arrival.wait_recv()
            onward.start()

    def finish(ins, outs, sems):
        mine, first, relay, last = copies(ins, outs, sems)
        for cp in last:
            cp.wait_recv()
        for cp in first + [onward for _, onward in relay]:
            cp.wait_send()
        for cp in mine:
            cp.wait()

    return _Job(arrs, [jax.ShapeDtypeStruct((N_DEV,) + a.shape, a.dtype) for a in arrs],
                [pltpu.SemaphoreType.DMA((n, 7)), pltpu.SemaphoreType.DMA((n, 7)), pltpu.SemaphoreType.DMA((n,))],
                start, finish, forward, forward_at)


def _run_job(job, name):
    ni, no = len(job.ins), len(job.outs)

    def body(*refs):
        ins, outs, sems = refs[:ni], refs[ni:ni + no], refs[ni + no:]
        job.start(ins, outs, sems)
        if job.forward is not None:
            job.forward(ins, outs, sems)
        job.finish(ins, outs, sems)

    any_spec = pl.BlockSpec(memory_space=pl.ANY)
    return pl.pallas_call(
        body, name=name, out_shape=job.outs, in_specs=[any_spec] * ni, out_specs=[any_spec] * no,
        scratch_shapes=job.sems,
    )(*job.ins)


def _call(body, *, name, grid, in_specs, out_specs, out_shape, scratch_shapes=(), sem, args, jobs=()):
    in_specs, out_specs, out_shape, scratch_shapes = list(in_specs), list(out_specs), list(out_shape), list(scratch_shapes)
    if not jobs:
        res = pl.pallas_call(body, name=name, grid=grid, in_specs=in_specs, out_specs=out_specs, out_shape=out_shape,
                             scratch_shapes=scratch_shapes, compiler_params=_params(sem))(*args)
        return list(res), []
    n_in, n_out, n_scr = len(in_specs), len(out_specs), len(scratch_shapes)
    j_in = [len(j.ins) for j in jobs]
    j_out = [len(j.outs) for j in jobs]
    j_sem = [len(j.sems) for j in jobs]
    total = int(np.prod(grid))
    any_spec = pl.BlockSpec(memory_space=pl.ANY)

    def wrapped(*refs):
        pos = [0]

        def take(k):
            part = refs[pos[0]:pos[0] + k]
            pos[0] += k
            return part

        ins = take(n_in)
        jins = [take(k) for k in j_in]
        outs = take(n_out)
        jouts = [take(k) for k in j_out]
        scr = take(n_scr)
        jsems = [take(k) for k in j_sem]
        step = pl.program_id(0)
        for d in range(1, len(grid)):
            step = step * grid[d] + pl.program_id(d)
        for j, job in enumerate(jobs):
            pl.when(step == 0)(functools.partial(job.start, jins[j], jouts[j], jsems[j]))
            if job.forward is not None:
                at = min(int(total * job.forward_at), total - 1)
                pl.when(step == at)(functools.partial(job.forward, jins[j], jouts[j], jsems[j]))
        body(*ins, *outs, *scr)
        for j, job in enumerate(jobs):
            pl.when(step == total - 1)(functools.partial(job.finish, jins[j], jouts[j], jsems[j]))

    res = pl.pallas_call(
        wrapped, name=name, grid=grid,
        in_specs=in_specs + [any_spec] * sum(j_in),
        out_specs=out_specs + [any_spec] * sum(j_out),
        out_shape=out_shape + [o for j in jobs for o in j.outs],
        scratch_shapes=scratch_shapes + [s for j in jobs for s in j.sems],
        compiler_params=_params(("arbitrary",) * len(grid)),
    )(*args, *[a for j in jobs for a in j.ins])
    res = list(res)
    own, rest = res[:n_out], res[n_out:]
    per_job = []
    for k in j_out:
        per_job.append(rest[:k])
        rest = rest[k:]
    return own, per_job


def _exchange_sibling(arrs, name):
    return _run_job(_sibling_job(arrs), name)


def _sibling_job(arrs):
    n = len(arrs)

    def copies(ins, outs, sems):
        send_sems, recv_sems = sems
        x, y, c = _my_place()
        return [pltpu.make_async_remote_copy(
            src_ref=ins[a].at[2 * k + 1 - c], dst_ref=outs[a].at[k],
            send_sem=send_sems.at[a, k], recv_sem=recv_sems.at[a, k],
            device_id=(x, y, 1 - c), device_id_type=MESH) for a in range(n) for k in range(4)]

    def start(ins, outs, sems):
        for cp in copies(ins, outs, sems):
            cp.start()

    def finish(ins, outs, sems):
        for cp in copies(ins, outs, sems):
            cp.wait()

    return _Job(arrs, [jax.ShapeDtypeStruct((4,) + a.shape[1:], a.dtype) for a in arrs],
                [pltpu.SemaphoreType.DMA((n, 4)), pltpu.SemaphoreType.DMA((n, 4))], start, finish)


def _chips_job(arrs):
    n = len(arrs)

    def copies(ins, outs, sems):
        send_sems, recv_sems = sems
        x, y, c = _my_place()
        chips = [(1 - x, y), (x, 1 - y), (1 - x, 1 - y)]
        return [pltpu.make_async_remote_copy(
            src_ref=ins[a].at[2 * px + py], dst_ref=outs[a].at[j],
            send_sem=send_sems.at[a, j], recv_sem=recv_sems.at[a, j],
            device_id=(px, py, c), device_id_type=MESH) for a in range(n) for j, (px, py) in enumerate(chips)]

    def start(ins, outs, sems):
        for cp in copies(ins, outs, sems):
            cp.start()

    def finish(ins, outs, sems):
        for cp in copies(ins, outs, sems):
            cp.wait()

    return _Job(arrs, [jax.ShapeDtypeStruct((3,) + a.shape[1:], a.dtype) for a in arrs],
                [pltpu.SemaphoreType.DMA((n, 3)), pltpu.SemaphoreType.DMA((n, 3))], start, finish)


def _pair_add(stack, from_sibling, c_idx, name):
    _, r, cc = stack.shape
    tr = _tile(r, 512)

    def body(c_ref, a_ref, b_ref, o_ref):
        o_ref[...] = (a_ref[...].astype(F32) + b_ref[...].astype(F32)).astype(o_ref.dtype)

    grid_spec = pltpu.PrefetchScalarGridSpec(
        num_scalar_prefetch=1, grid=(4, r // tr),
        in_specs=[pl.BlockSpec((None, tr, cc), lambda k, i, c_ref: (2 * k + c_ref[0], i, 0)),
                  pl.BlockSpec((None, tr, cc), lambda k, i, c_ref: (k, i, 0))],
        out_specs=pl.BlockSpec((None, tr, cc), lambda k, i, c_ref: (k, i, 0)))
    return pl.pallas_call(
        body, name=name, grid_spec=grid_spec,
        out_shape=jax.ShapeDtypeStruct((4, r, cc), BF16),
        compiler_params=_params(("parallel", "parallel")),
    )(c_idx, stack, from_sibling)


_DIMS = {"nn": ((1,), (0,)), "nt": ((1,), (1,)), "tn": ((0,), (0,))}


def _mm(a, b, mode, *, tm, tn, tk=None, outs, epi=None, tiles=(), rows=(), name, jobs=()):
    if mode == "tn":
        kk, m = a.shape
    else:
        m, kk = a.shape
    nn = b.shape[0] if mode == "nt" else b.shape[1]
    tk = kk if tk is None else tk
    assert m % tm == 0 and nn % tn == 0 and kk % tk == 0, (name, a.shape, b.shape, tm, tn, tk)
    nk = kk // tk
    nt, nr, no = len(tiles), len(rows), len(outs)

    def body(*refs):
        a_ref, b_ref = refs[:2]
        tile_refs = refs[2:2 + nt]
        row_refs = refs[2 + nt:2 + nt + nr]
        out_refs = refs[2 + nt + nr:2 + nt + nr + no]
        part = lax.dot_general(a_ref[...], b_ref[...], (_DIMS[mode], ((), ())), preferred_element_type=F32)

        def finish(acc):
            if epi is None:
                res = (acc,)
            else:
                res = epi(acc, *[t[...] for t in tile_refs], *[r[...] for r in row_refs])
            for o_ref, val in zip(out_refs, res):
                o_ref[...] = val.astype(o_ref.dtype)

        if nk == 1:
            finish(part)
        else:
            acc_ref = refs[-1]
            k = pl.program_id(2)

            @pl.when(k == 0)
            def _():
                acc_ref[...] = part

            @pl.when(k > 0)
            def _():
                acc_ref[...] += part

            @pl.when(k == nk - 1)
            def _():
                finish(acc_ref[...])

    if mode == "tn":
        a_spec = pl.BlockSpec((tk, tm), lambda i, j, k: (k, i))
    else:
        a_spec = pl.BlockSpec((tm, tk), lambda i, j, k: (i, k))
    if mode == "nt":
        b_spec = pl.BlockSpec((tn, tk), lambda i, j, k: (j, k))
    else:
        b_spec = pl.BlockSpec((tk, tn), lambda i, j, k: (k, j))
    tile_specs = [pl.BlockSpec((tm, tn), functools.partial(lambda i, j, k, off: (i, j + off), off=off))
                  for _, off in tiles]
    row_specs = [pl.BlockSpec((r.shape[0], tn), lambda i, j, k: (0, j)) for r in rows]
    out_spec = pl.BlockSpec((tm, tn), lambda i, j, k: (i, j))
    res, jres = _call(
        body, name=name, grid=(m // tm, nn // tn, nk),
        in_specs=[a_spec, b_spec] + tile_specs + row_specs,
        out_specs=[out_spec] * no,
        out_shape=[jax.ShapeDtypeStruct((m, nn), dt) for dt in outs],
        scratch_shapes=[pltpu.VMEM((tm, tn), F32)] if nk > 1 else [],
        sem=("parallel", "parallel", "arbitrary"),
        args=[a, b, *[t for t, _ in tiles], *rows], jobs=jobs)
    return (res, jres) if jobs else res


def _ada_fwd(c_all, w_ada, b_ada_cols, name):
    nb, dd = c_all.shape
    ncol = w_ada.shape[1]
    tn = _tile(ncol, 512, LANE)

    def body(c_ref, w_ref, b_ref, o_ref):
        cv = c_ref[...]
        act = (cv * _sigmoid(cv)).astype(BF16)
        o_ref[...] = jnp.dot(act, w_ref[...].astype(BF16), preferred_element_type=F32) + b_ref[...]

    return pl.pallas_call(
        body, name=name, grid=(ncol // tn,),
        in_specs=[pl.BlockSpec((nb, dd), lambda j: (0, 0)), pl.BlockSpec((dd, tn), lambda j: (0, j)),
                  pl.BlockSpec((1, tn), lambda j: (0, j))],
        out_specs=pl.BlockSpec((nb, tn), lambda j: (0, j)),
        out_shape=jax.ShapeDtypeStruct((nb, ncol), F32),
        compiler_params=_params(("parallel",)),
    )(c_all, w_ada, b_ada_cols)


def _ada_bwd(c_all_t, dmod_cols, name):
    dd, nb = c_all_t.shape
    ncol = dmod_cols.shape[1]
    tr = _tile(dd, 256)

    def body(c_ref, d_ref, o_ref):
        cv = c_ref[...]
        act = (cv * _sigmoid(cv)).astype(BF16).astype(F32)
        dm = d_ref[...].astype(BF16).astype(F32)
        acc = act[:, 0:1] * dm[0:1, :]
        for bi in range(1, nb):
            acc = acc + act[:, bi:bi + 1] * dm[bi:bi + 1, :]
        o_ref[...] = acc

    return pl.pallas_call(
        body, name=name, grid=(dd // tr,),
        in_specs=[pl.BlockSpec((tr, nb), lambda i: (i, 0)), pl.BlockSpec((nb, ncol), lambda i: (0, 0))],
        out_specs=pl.BlockSpec((tr, ncol), lambda i: (i, 0)),
        out_shape=jax.ShapeDtypeStruct((dd, ncol), F32),
        compiler_params=_params(("parallel",)),
    )(c_all_t, dmod_cols)


def _norm_mod(x, gain, mod, row0, name):
    t, dd = x.shape
    tt = _tile(t, 512)

    def body(x_ref, g_ref, m_ref, h_ref):
        xv = x_ref[...]
        r = lax.rsqrt(jnp.mean(xv * xv, axis=-1, keepdims=True) + EPS)
        shift, scale = m_ref[row0:row0 + 1, :], m_ref[row0 + 1:row0 + 2, :]
        h_ref[...] = ((xv * r * g_ref[...]) * (1.0 + scale) + shift).astype(BF16)

    return pl.pallas_call(
        body, name=name, grid=(t // tt,),
        in_specs=[pl.BlockSpec((tt, dd), lambda i: (i, 0)), pl.BlockSpec((1, dd), lambda i: (0, 0)),
                  pl.BlockSpec((8, dd), lambda i: (0, 0))],
        out_specs=pl.BlockSpec((tt, dd), lambda i: (i, 0)),
        out_shape=jax.ShapeDtypeStruct((t, dd), BF16),
        compiler_params=_params(("parallel",)),
    )(x, gain, mod)


def _norm_mod_bwd(x, gain, mod, row0, dh, dres, branch, name, jobs=()):
    t, dd = x.shape
    tt = _tile(t, 256)
    has_branch = branch is not None

    def body(*refs):
        if has_branch:
            x_ref, g_ref, m_ref, dh_ref, dres_ref, o_ref, dx_ref, st_ref, do_ref = refs
        else:
            x_ref, g_ref, m_ref, dh_ref, dres_ref, dx_ref, st_ref = refs

        @pl.when(pl.program_id(0) == 0)
        def _():
            st_ref[...] = jnp.zeros_like(st_ref)

        xv = x_ref[...]
        g = g_ref[...]
        scale = m_ref[row0 + 1:row0 + 2, :]
        r = lax.rsqrt(jnp.mean(xv * xv, axis=-1, keepdims=True) + EPS)
        nv = xv * r
        dhv = dh_ref[...].astype(F32)
        st_ref[0:1, :] += jnp.sum(dhv, axis=0, keepdims=True)
        st_ref[1:2, :] += jnp.sum(dhv * (nv * g), axis=0, keepdims=True)
        dng = dhv * (1.0 + scale)
        st_ref[2:3, :] += jnp.sum(dng * nv, axis=0, keepdims=True)
        dn = dng * g
        dx = dres_ref[...] + r * (dn - nv * jnp.mean(dn * nv, axis=-1, keepdims=True))
        dx_ref[...] = dx
        if has_branch:
            ov = o_ref[...].astype(F32)
            gate = m_ref[branch[1]:branch[1] + 1, :]
            st_ref[3:4, :] += jnp.sum(dx * ov, axis=0, keepdims=True)
            do_ref[...] = (dx * gate).astype(BF16)

    row = pl.BlockSpec((tt, dd), lambda i: (i, 0))
    in_specs = [row, pl.BlockSpec((1, dd), lambda i: (0, 0)), pl.BlockSpec((8, dd), lambda i: (0, 0)), row, row]
    args = [x, gain, mod, dh, dres]
    out_specs = [row, pl.BlockSpec((8, dd), lambda i: (0, 0))]
    out_shape = [jax.ShapeDtypeStruct((t, dd), F32), jax.ShapeDtypeStruct((8, dd), F32)]
    if has_branch:
        in_specs.append(row)
        args.append(branch[0])
        out_specs.append(row)
        out_shape.append(jax.ShapeDtypeStruct((t, dd), BF16))
    res, jres = _call(body, name=name, grid=(t // tt,), in_specs=in_specs, out_specs=out_specs, out_shape=out_shape,
                      sem=("arbitrary",), args=args, jobs=jobs)
    return (res, jres) if jobs else res


def _final_loss(x2, o2, final_g, target, mod, name):
    t, dd = x2.shape
    tt = _tile(t, 256)
    nsteps = t // tt

    def body(x_ref, o_ref, g_ref, t_ref, m_ref, loss_ref, dx_ref, do_ref, st_ref, lacc):
        i = pl.program_id(0)

        @pl.when(i == 0)
        def _():
            st_ref[...] = jnp.zeros_like(st_ref)
            lacc[...] = jnp.zeros_like(lacc)

        xv = x_ref[...]
        g = g_ref[...]
        r = lax.rsqrt(jnp.mean(xv * xv, axis=-1, keepdims=True) + EPS)
        nv = xv * r
        err = nv * g - t_ref[...]
        lacc[...] += jnp.sum(err * err, axis=0, keepdims=True)
        dy = err * (1.0 / dd)
        st_ref[0:1, :] += jnp.sum(dy * nv, axis=0, keepdims=True)
        dn = dy * g
        dx = r * (dn - nv * jnp.mean(dn * nv, axis=-1, keepdims=True))
        dx_ref[...] = dx
        st_ref[1:2, :] += jnp.sum(dx * o_ref[...].astype(F32), axis=0, keepdims=True)
        do_ref[...] = (dx * m_ref[5:6, :]).astype(BF16)

        @pl.when(i == nsteps - 1)
        def _():
            total = jnp.sum(lacc[...], axis=-1, keepdims=True) * (0.5 / dd)
            loss_ref[...] = jnp.broadcast_to(total, loss_ref.shape)

    row = pl.BlockSpec((tt, dd), lambda i: (i, 0))
    return pl.pallas_call(
        body, name=name, grid=(nsteps,),
        in_specs=[row, row, pl.BlockSpec((1, dd), lambda i: (0, 0)), row, pl.BlockSpec((8, dd), lambda i: (0, 0))],
        out_specs=[pl.BlockSpec((8, LANE), lambda i: (0, 0)), row, row, pl.BlockSpec((8, dd), lambda i: (0, 0))],
        out_shape=[jax.ShapeDtypeStruct((8, LANE), F32), jax.ShapeDtypeStruct((t, dd), F32),
                   jax.ShapeDtypeStruct((t, dd), BF16), jax.ShapeDtypeStruct((8, dd), F32)],
        scratch_shapes=[pltpu.VMEM((1, dd), F32)],
        compiler_params=_params(("arbitrary",)),
    )(x2, o2, final_g, target, mod)


def _lru_gates(xt, prev8, lp, wa_ref, wx_ref, first_tile, cw):
    tt = xt.shape[0]
    ext = jnp.concatenate([prev8, xt], axis=0)
    xs = [pltpu.roll(ext, s, 0)[8:, :] if s else xt for s in range(CONV_WIDTH)]
    xc = lp[0:1, :] + lp[7:8, :] * xs[0] + lp[6:7, :] * xs[1] + lp[5:6, :] * xs[2] + lp[4:5, :] * xs[3]
    xcb = xc.astype(BF16)
    za, zi = [], []
    for jb in range(cw // LRU_BLOCK_W):
        sl = slice(jb * LRU_BLOCK_W, (jb + 1) * LRU_BLOCK_W)
        za.append(jnp.dot(xcb[:, sl], wa_ref[jb].astype(BF16), preferred_element_type=F32))
        zi.append(jnp.dot(xcb[:, sl], wx_ref[jb].astype(BF16), preferred_element_type=F32))
    ra = _sigmoid(jnp.concatenate(za, axis=1) + lp[1:2, :])
    ri = _sigmoid(jnp.concatenate(zi, axis=1) + lp[2:3, :])
    sp = _softplus(-lp[3:4, :])
    log_a = -LRU_C * ra * sp
    av = jnp.exp(log_a)
    seq_start = jnp.logical_and(first_tile, lax.broadcasted_iota(jnp.int32, (tt, 1), 0) == 0)
    mult = jnp.where(seq_start, 1.0, jnp.sqrt(_neg_expm1(2.0 * log_a)))
    return xs, xc, xcb, ra, ri, sp, av, mult, seq_start


def _lru_fwd(proj, lp, wa, wx, name, jobs=()):
    t = proj.shape[0]
    w = D_MODEL
    cw = LRU_CW
    tt = _tile(t, 256)
    nb = cw // LRU_BLOCK_W

    def body(lx_ref, lg_ref, lp_ref, wa_ref, wx_ref, rec_ref, ya_ref, a_s, u_s, h_s, halo, carry):
        ti = pl.program_id(1)

        @pl.when(ti == 0)
        def _():
            halo[...] = jnp.zeros_like(halo)
            carry[...] = jnp.zeros_like(carry)

        xt = lx_ref[...].astype(F32)
        lp_v = lp_ref[...]
        _, xc, _, _, ri, _, av, mult, _ = _lru_gates(xt, halo[...], lp_v, wa_ref, wx_ref, ti == 0, cw)
        halo[...] = xt[tt - 8:, :]
        a_s[...] = av
        u_s[...] = mult * (ri * xc)

        def step(s, h):
            h = a_s[pl.ds(s, 1), :] * h + u_s[pl.ds(s, 1), :]
            h_s[pl.ds(s, 1), :] = h
            return h

        carry[...] = lax.fori_loop(0, tt, step, carry[...], unroll=8)
        rec = h_s[...]
        rec_ref[...] = rec.astype(BF16)
        gelu, _ = _gelu_tanh(lg_ref[...].astype(F32))
        ya_ref[...] = (rec * gelu).astype(BF16)

    off_g = OFF_LG // cw
    res, jres = _call(
        body, name=name, grid=(w // cw, t // tt),
        in_specs=[pl.BlockSpec((tt, cw), lambda ci, ti: (ti, ci)),
                  pl.BlockSpec((tt, cw), lambda ci, ti: (ti, off_g + ci)),
                  pl.BlockSpec((8, cw), lambda ci, ti: (0, ci)),
                  pl.BlockSpec((nb, LRU_BLOCK_W, LRU_BLOCK_W), lambda ci, ti: (ci, 0, 0)),
                  pl.BlockSpec((nb, LRU_BLOCK_W, LRU_BLOCK_W), lambda ci, ti: (ci, 0, 0))],
        out_specs=[pl.BlockSpec((tt, cw), lambda ci, ti: (ti, ci))] * 2,
        out_shape=[jax.ShapeDtypeStruct((t, w), BF16)] * 2,
        scratch_shapes=[pltpu.VMEM((tt, cw), F32)] * 3 + [pltpu.VMEM((8, cw), F32), pltpu.VMEM((1, cw), F32)],
        sem=("parallel", "arbitrary"), args=[proj, proj, lp, wa, wx], jobs=jobs)
    return (res, jres) if jobs else res


def _lru_bwd(proj, rec, dya, lp, wa, wx, name, jobs=()):
    t = proj.shape[0]
    w = D_MODEL
    cw = LRU_CW
    tt = _tile(t, 256)
    nt = t // tt
    nb = cw // LRU_BLOCK_W
    r8 = tt // 8

    def body(lx_ref, lxp_ref, lg_ref, rec_ref, recp_ref, dya_ref, lp_ref, wa_ref, wx_ref,
             dlx_ref, dlg_ref, st_ref, dwa_ref, dwx_ref, a_s, d_s, dh_s, nhalo, carry):
        step_i = pl.program_id(1)
        ti = nt - 1 - step_i

        @pl.when(step_i == 0)
        def _():
            st_ref[...] = jnp.zeros_like(st_ref)
            dwa_ref[...] = jnp.zeros_like(dwa_ref)
            dwx_ref[...] = jnp.zeros_like(dwx_ref)
            nhalo[...] = jnp.zeros_like(nhalo)
            carry[...] = jnp.zeros_like(carry)

        first = ti == 0
        keep = jnp.where(first, 0.0, 1.0)
        xt = lx_ref[...].astype(F32)
        prev8 = lxp_ref[...].astype(F32) * keep
        lp_v = lp_ref[...]
        xs, xc, xcb, ra, ri, sp, av, mult, seq_start = _lru_gates(xt, prev8, lp_v, wa_ref, wx_ref, first, cw)

        lg = lg_ref[...].astype(F32)
        gelu, th = _gelu_tanh(lg)
        dyav = dya_ref[...].astype(F32)
        recv = rec_ref[...].astype(F32)
        dlg_ref[...] = (dyav * recv * _gelu_tanh_grad(lg, th)).astype(BF16)

        a_s[...] = av
        d_s[...] = dyav * gelu

        def step(s, e):
            row = tt - 1 - s
            dh = d_s[pl.ds(row, 1), :] + e
            dh_s[pl.ds(row, 1), :] = dh
            return a_s[pl.ds(row, 1), :] * dh

        carry[...] = lax.fori_loop(0, tt, step, carry[...], unroll=8)
        dh = dh_s[...]
        rprev8 = recp_ref[...].astype(F32) * keep
        hprev = pltpu.roll(jnp.concatenate([rprev8, recv], axis=0), 1, 0)[8:, :]
        da = dh * hprev
        dmult = jnp.where(seq_start, 0.0, dh * ri * xc)
        dri = dh * mult * xc
        dxc = dh * mult * ri
        dlog_a = da * av - dmult * (av * av) / mult
        dra = dlog_a * (-LRU_C * sp)
        lam = lp_v[3:4, :]
        st_ref[3:4, :] += jnp.sum(dlog_a * ra, axis=0, keepdims=True) * (LRU_C * _sigmoid(-lam))
        dza = dra * ra * (1.0 - ra)
        dzi = dri * ri * (1.0 - ri)
        st_ref[1:2, :] += jnp.sum(dza, axis=0, keepdims=True)
        st_ref[2:3, :] += jnp.sum(dzi, axis=0, keepdims=True)
        dzab, dzib = dza.astype(BF16), dzi.astype(BF16)
        back = []
        for jb in range(nb):
            sl = slice(jb * LRU_BLOCK_W, (jb + 1) * LRU_BLOCK_W)
            dwa_ref[jb] += lax.dot_general(xcb[:, sl], dzab[:, sl], (_DIMS["tn"], ((), ())), preferred_element_type=F32)
            dwx_ref[jb] += lax.dot_general(xcb[:, sl], dzib[:, sl], (_DIMS["tn"], ((), ())), preferred_element_type=F32)
            back.append(
                lax.dot_general(dzab[:, sl], wa_ref[jb].astype(BF16), (_DIMS["nt"], ((), ())), preferred_element_type=F32)
                + lax.dot_general(dzib[:, sl], wx_ref[jb].astype(BF16), (_DIMS["nt"], ((), ())), preferred_element_type=F32))
        dxc = dxc + jnp.concatenate(back, axis=1)
        st_ref[0:1, :] += jnp.sum(dxc, axis=0, keepdims=True)
        for k in range(CONV_WIDTH):
            st_ref[4 + k:5 + k, :] += jnp.sum(dxc * xs[CONV_WIDTH - 1 - k], axis=0, keepdims=True)
        ext = jnp.concatenate([dxc, nhalo[...]], axis=0)
        dlx = lp_v[7:8, :] * dxc
        for s in range(1, CONV_WIDTH):
            dlx = dlx + lp_v[7 - s:8 - s, :] * pltpu.roll(ext, tt + 8 - s, 0)[:tt, :]
        dlx_ref[...] = dlx.astype(BF16)
        nhalo[...] = dxc[0:8, :]

    off_g = OFF_LG // cw
    tile = lambda off: pl.BlockSpec((tt, cw), lambda ci, si: (nt - 1 - si, off + ci))
    prev = lambda off: pl.BlockSpec((8, cw), lambda ci, si: (jnp.maximum((nt - 1 - si) * r8 - 1, 0), off + ci))
    wspec = pl.BlockSpec((nb, LRU_BLOCK_W, LRU_BLOCK_W), lambda ci, si: (ci, 0, 0))
    st_spec = pl.BlockSpec((8, cw), lambda ci, si: (0, ci))
    res, jres = _call(
        body, name=name, grid=(w // cw, nt),
        in_specs=[tile(0), prev(0), tile(off_g), tile(0), prev(0), tile(0), st_spec, wspec, wspec],
        out_specs=[tile(0), tile(0), st_spec, wspec, wspec],
        out_shape=[jax.ShapeDtypeStruct((t, w), BF16), jax.ShapeDtypeStruct((t, w), BF16),
                   jax.ShapeDtypeStruct((8, w), F32),
                   jax.ShapeDtypeStruct((LRU_BLOCKS, LRU_BLOCK_W, LRU_BLOCK_W), F32),
                   jax.ShapeDtypeStruct((LRU_BLOCKS, LRU_BLOCK_W, LRU_BLOCK_W), F32)],
        scratch_shapes=[pltpu.VMEM((tt, cw), F32)] * 3 + [pltpu.VMEM((8, cw), F32), pltpu.VMEM((1, cw), F32)],
        sem=("parallel", "arbitrary"), args=[proj, proj, proj, rec, rec, dya, lp, wa, wx], jobs=jobs)
    return (res, jres) if jobs else res


def _band_valid(first_block):
    qi = lax.broadcasted_iota(jnp.int32, (BLOCK, 2 * BLOCK), 0)
    ki = lax.broadcasted_iota(jnp.int32, (BLOCK, 2 * BLOCK), 1)
    rel = qi + BLOCK - ki
    valid = jnp.logical_and(rel >= 0, rel < BLOCK)
    return jnp.logical_and(valid, jnp.logical_or(ki >= BLOCK, jnp.logical_not(first_block)))


ATTN_STACK_BWD = 8


def _low_half():
    return lax.broadcasted_iota(jnp.int32, (1, LANE), 1) < HEAD_DIM


def _stack_heads(x, h0, ns):
    low = _low_half()
    parts = []
    for g in range(ns):
        h = h0 + g
        slab = x[:, (h // 2) * LANE:(h // 2 + 1) * LANE]
        parts.append(jnp.where(low if h % 2 == 0 else jnp.logical_not(low), slab, jnp.zeros_like(slab)))
    return jnp.concatenate(parts, axis=0)


def _unstack_heads(y, ns):
    low = _low_half()
    pairs = [jnp.where(low, y[(2 * j) * BLOCK:(2 * j + 1) * BLOCK], y[(2 * j + 1) * BLOCK:(2 * j + 2) * BLOCK])
             for j in range(ns // 2)]
    return pairs[0] if len(pairs) == 1 else jnp.concatenate(pairs, axis=1)


def _dup_kv(xall, kv):
    part = xall[:, kv * HEAD_DIM:(kv + 1) * HEAD_DIM]
    return jnp.concatenate([part, part], axis=1)


def _fold_halves(a):
    return a + pltpu.roll(a, HEAD_DIM, 1)


def _group_probs(qg, k2, bias_g, sink_g, valid):
    s = lax.dot_general(qg, k2, (_DIMS["nt"], ((), ())), preferred_element_type=F32) * (HEAD_DIM ** -0.5)
    s = jnp.where(valid[None], s.reshape(bias_g.shape) + bias_g, NEG_INF)
    m = jnp.maximum(jnp.max(s, axis=-1, keepdims=True), sink_g)
    e = jnp.exp(s - m)
    es = jnp.exp(sink_g - m)
    inv = 1.0 / (jnp.sum(e, axis=-1, keepdims=True) + es)
    return e * inv, es * inv


def _group_sinks(s_ref, h0, ns):
    return jnp.concatenate([jnp.full((1, BLOCK, 1), s_ref[h0 + g], F32) for g in range(ns)], axis=0)


def _attn_specs(t):
    nblk = t // BLOCK
    q_spec = pl.BlockSpec((BLOCK, D_MODEL), lambda n: (n, OFF_Q // D_MODEL))
    kc = pl.BlockSpec((BLOCK, KV_WIDTH), lambda n: (n, OFF_K // KV_WIDTH))
    kp = pl.BlockSpec((BLOCK, KV_WIDTH), lambda n: (jnp.maximum(n - 1, 0), OFF_K // KV_WIDTH))
    vc = pl.BlockSpec((BLOCK, KV_WIDTH), lambda n: (n, OFF_V // KV_WIDTH))
    vp = pl.BlockSpec((BLOCK, KV_WIDTH), lambda n: (jnp.maximum(n - 1, 0), OFF_V // KV_WIDTH))
    bias_spec = pl.BlockSpec((N_Q_HEADS, BLOCK, 2 * BLOCK), lambda n: (0, 0, 0))
    sink_spec = pl.BlockSpec(memory_space=pltpu.SMEM)
    return nblk, [q_spec, kp, kc, vp, vc, bias_spec, sink_spec]


def _attn_fwd(proj, bias, sinks, name, jobs=()):
    t = proj.shape[0]
    nblk, in_specs = _attn_specs(t)

    def body(q_ref, kp_ref, kc_ref, vp_ref, vc_ref, b_ref, s_ref, o_ref):
        valid = _band_valid(pl.program_id(0) == 0)
        q = q_ref[...]
        kall = jnp.concatenate([kp_ref[...], kc_ref[...]], axis=0)
        vall = jnp.concatenate([vp_ref[...], vc_ref[...]], axis=0)
        outs = []
        for h in range(N_Q_HEADS):
            kv = h // GROUP
            kk = kall[:, kv * HEAD_DIM:(kv + 1) * HEAD_DIM]
            vv = vall[:, kv * HEAD_DIM:(kv + 1) * HEAD_DIM]
            s = lax.dot_general(q[:, h * HEAD_DIM:(h + 1) * HEAD_DIM], kk, (_DIMS["nt"], ((), ())),
                                preferred_element_type=F32) * (HEAD_DIM ** -0.5)
            s = jnp.where(valid, s + b_ref[h], NEG_INF)
            sink = s_ref[h]
            m = jnp.maximum(jnp.max(s, axis=-1, keepdims=True), sink)
            e = jnp.exp(s - m)
            p = e * (1.0 / (jnp.sum(e, axis=-1, keepdims=True) + jnp.exp(sink - m)))
            outs.append(jnp.dot(p.astype(BF16), vv, preferred_element_type=F32))
        o_ref[...] = jnp.concatenate(outs, axis=1).astype(BF16)

    res, jres = _call(
        body, name=name, grid=(nblk,), in_specs=in_specs,
        out_specs=[pl.BlockSpec((BLOCK, D_MODEL), lambda n: (n, 0))],
        out_shape=[jax.ShapeDtypeStruct((t, D_MODEL), BF16)],
        sem=("parallel",), args=[proj, proj, proj, proj, proj, bias, sinks], jobs=jobs)
    return (res[0], jres) if jobs else res[0]


def _attn_bwd(proj, datt, bias, sinks, name, jobs=()):
    t = proj.shape[0]
    nblk, in_specs = _attn_specs(t)
    in_specs = in_specs + [pl.BlockSpec((BLOCK, D_MODEL), lambda n: (n, 0))]
    scale = HEAD_DIM ** -0.5

    def body(q_ref, kp_ref, kc_ref, vp_ref, vc_ref, b_ref, s_ref, do_ref, dq_ref, dk_ref, dv_ref, db_ref, ds_ref):
        n = pl.program_id(0)

        @pl.when(n == 0)
        def _():
            dk_ref[...] = jnp.zeros_like(dk_ref)
            dv_ref[...] = jnp.zeros_like(dv_ref)
            db_ref[...] = jnp.zeros_like(db_ref)
            ds_ref[...] = jnp.zeros_like(ds_ref)

        valid = _band_valid(n == 0)
        lane = lax.broadcasted_iota(jnp.int32, (1, LANE), 1)
        q = q_ref[...]
        do = do_ref[...]
        kall = jnp.concatenate([kp_ref[...], kc_ref[...]], axis=0)
        vall = jnp.concatenate([vp_ref[...], vc_ref[...]], axis=0)
        low = _low_half()
        dqs, dks, dvs = [], [], []
        dsink_row = jnp.zeros((1, LANE), F32)
        ns = ATTN_STACK_BWD
        for kv in range(N_KV_HEADS):
            k2, v2 = _dup_kv(kall, kv), _dup_kv(vall, kv)
            dk_acc = jnp.zeros((2 * BLOCK, LANE), F32)
            dv_acc = jnp.zeros((2 * BLOCK, LANE), F32)
            for h0 in range(kv * GROUP, (kv + 1) * GROUP, ns):
                qg, dog = _stack_heads(q, h0, ns), _stack_heads(do, h0, ns)
                p, psink = _group_probs(qg, k2, b_ref[h0:h0 + ns], _group_sinks(s_ref, h0, ns), valid)
                dp = lax.dot_general(dog, v2, (_DIMS["nt"], ((), ())), preferred_element_type=F32)
                dp = dp.reshape(ns, BLOCK, 2 * BLOCK)
                delta = jnp.sum(p * dp, axis=-1, keepdims=True)
                ds = p * (dp - delta)
                db_ref[h0:h0 + ns] += ds
                dsink = -jnp.sum(psink * delta, axis=1, keepdims=True)
                for g in range(ns):
                    dsink_row = dsink_row + jnp.where(lane == h0 + g, dsink[g], 0.0)
                dsb = ds.reshape(ns * BLOCK, 2 * BLOCK).astype(BF16)
                pb = p.reshape(ns * BLOCK, 2 * BLOCK).astype(BF16)
                dqs.append(_unstack_heads(jnp.dot(dsb, k2, preferred_element_type=F32) * scale, ns))
                dk_acc = dk_acc + lax.dot_general(dsb, qg, (_DIMS["tn"], ((), ())), preferred_element_type=F32)
                dv_acc = dv_acc + lax.dot_general(pb, dog, (_DIMS["tn"], ((), ())), preferred_element_type=F32)
            dks.append(_fold_halves(dk_acc) * scale)
            dvs.append(_fold_halves(dv_acc))
        dq_ref[...] = jnp.concatenate(dqs, axis=1).astype(BF16)
        rows = pl.ds(pl.multiple_of(n * BLOCK, BLOCK), 2 * BLOCK)
        dk_ref[rows, :] += jnp.concatenate([jnp.where(low, dks[0], dks[1]), jnp.where(low, dks[2], dks[3])], axis=1)
        dv_ref[rows, :] += jnp.concatenate([jnp.where(low, dvs[0], dvs[1]), jnp.where(low, dvs[2], dvs[3])], axis=1)
        ds_ref[0:1, :] += dsink_row

    res, jres = _call(
        body, name=name, grid=(nblk,), in_specs=in_specs,
        out_specs=[pl.BlockSpec((BLOCK, D_MODEL), lambda n: (n, 0)),
                   pl.BlockSpec((t + BLOCK, KV_WIDTH), lambda n: (0, 0)),
                   pl.BlockSpec((t + BLOCK, KV_WIDTH), lambda n: (0, 0)),
                   pl.BlockSpec((N_Q_HEADS, BLOCK, 2 * BLOCK), lambda n: (0, 0, 0)),
                   pl.BlockSpec((8, LANE), lambda n: (0, 0))],
        out_shape=[jax.ShapeDtypeStruct((t, D_MODEL), BF16),
                   jax.ShapeDtypeStruct((t + BLOCK, KV_WIDTH), F32),
                   jax.ShapeDtypeStruct((t + BLOCK, KV_WIDTH), F32),
                   jax.ShapeDtypeStruct((N_Q_HEADS, BLOCK, 2 * BLOCK), F32),
                   jax.ShapeDtypeStruct((8, LANE), F32)],
        sem=("arbitrary",), args=[proj, proj, proj, proj, proj, bias, sinks, datt], jobs=jobs)
    return (res, jres) if jobs else res


def _bucket_table():
    qi = np.arange(BLOCK)[:, None]
    ki = np.arange(2 * BLOCK)[None, :]
    rel = np.maximum(qi + BLOCK - ki, 0)
    max_exact = N_BUCKETS // 2
    relf = np.maximum(rel, 1).astype(np.float32)
    large = max_exact + (np.log(relf / max_exact) / math.log(MAX_DISTANCE / max_exact)
                         * (N_BUCKETS - max_exact)).astype(np.int32)
    large = np.minimum(large, N_BUCKETS - 1)
    return np.where(rel < max_exact, rel, large).astype(np.int32)


def _rel_bias_fwd(rel_bias, onehot, name):
    n = onehot.shape[1]
    tn = _tile(n, 8192, LANE)

    def body(r_ref, o_ref, out_ref):
        out_ref[...] = lax.dot_general(r_ref[...], o_ref[...], (_DIMS["tn"], ((), ())),
                                       precision=lax.Precision.HIGHEST, preferred_element_type=F32)

    return pl.pallas_call(
        body, name=name, grid=(n // tn,),
        in_specs=[pl.BlockSpec(rel_bias.shape, lambda j: (0, 0)), pl.BlockSpec((N_BUCKETS, tn), lambda j: (0, j))],
        out_specs=pl.BlockSpec((N_Q_HEADS, tn), lambda j: (0, j)),
        out_shape=jax.ShapeDtypeStruct((N_Q_HEADS, n), F32),
        compiler_params=_params(("parallel",)),
    )(rel_bias, onehot)


def _rel_bias_bwd(dbias, onehot, name):
    def body(d_ref, o_ref, out_ref):
        out_ref[...] = lax.dot_general(d_ref[...], o_ref[...], (_DIMS["nt"], ((), ())),
                                       precision=lax.Precision.HIGHEST, preferred_element_type=F32)

    full = pl.BlockSpec(dbias.shape, lambda: (0, 0))
    return pl.pallas_call(
        body, name=name, in_specs=[full, full],
        out_specs=pl.BlockSpec((N_Q_HEADS, N_BUCKETS), lambda: (0, 0)),
        out_shape=jax.ShapeDtypeStruct((N_Q_HEADS, N_BUCKETS), F32),
        compiler_params=_params(()),
    )(dbias, onehot)


def _part_specs(gparts, part_index, tr, cc):
    blk = pl.BlockSpec((tr, cc), lambda i, *_: (i, 0))
    specs = []
    for i, _ in enumerate(gparts):
        if i in part_index:
            ix = part_index[i]
            if isinstance(ix, int):
                specs.append(pl.BlockSpec((None, tr, cc), functools.partial(lambda i, *_, ix: (ix, i, 0), ix=ix)))
            else:
                specs.append(pl.BlockSpec((None, tr, cc), lambda i, ix_ref: (ix_ref[0], i, 0)))
        else:
            specs.append(blk)
    return blk, specs


def _sum_parts(gparts, part_index, name):
    r, cc = gparts[0].shape[-2:]
    tr = _tile(r, 256)
    dyn = [i for i, ix in part_index.items() if not isinstance(ix, int)]
    assert len(dyn) <= 1

    def body(*refs):
        refs = refs[len(dyn):]
        g = refs[0][...].astype(F32)
        for gr in refs[1:-1]:
            g = g + gr[...].astype(F32)
        refs[-1][...] = g

    blk, specs = _part_specs(gparts, part_index, tr, cc)
    grid_spec = pltpu.PrefetchScalarGridSpec(num_scalar_prefetch=len(dyn), grid=(r // tr,), in_specs=specs, out_specs=blk)
    return pl.pallas_call(
        body, name=name, grid_spec=grid_spec, out_shape=jax.ShapeDtypeStruct((r, cc), F32),
        compiler_params=_params(("parallel",)),
    )(*[part_index[i] for i in dyn], *gparts)


def _adamw(w, m, v, gparts, name, part_index=None):
    r, cc = w.shape
    tr = _tile(r, 128)
    np_ = len(gparts)
    part_index = part_index or {}
    dyn = [i for i, ix in part_index.items() if not isinstance(ix, int)]
    assert len(dyn) <= 1
    bc1 = 1.0 / (1.0 - ADAM_B1 ** ADAM_STEP)
    bc2 = 1.0 / (1.0 - ADAM_B2 ** ADAM_STEP)

    def body(*refs):
        refs = refs[len(dyn):]
        w_ref, m_ref, v_ref = refs[:3]
        g_refs = refs[3:3 + np_]
        g_out, d_out, m_out, v_out = refs[3 + np_:]
        g = g_refs[0][...].astype(F32)
        for gr in g_refs[1:]:
            g = g + gr[...].astype(F32)
        mn = ADAM_B1 * m_ref[...] + (1.0 - ADAM_B1) * g
        vn = ADAM_B2 * v_ref[...] + (1.0 - ADAM_B2) * (g * g)
        g_out[...] = g
        m_out[...] = mn
        v_out[...] = vn
        d_out[...] = -ADAM_LR * ((mn * bc1) / (jnp.sqrt(vn * bc2) + ADAM_EPS) + ADAM_WD * w_ref[...])

    blk, g_specs = _part_specs(gparts, part_index, tr, cc)
    grid_spec = pltpu.PrefetchScalarGridSpec(
        num_scalar_prefetch=len(dyn), grid=(r // tr,),
        in_specs=[blk, blk, blk] + g_specs, out_specs=[blk] * 4)
    return pl.pallas_call(
        body, name=name, grid_spec=grid_spec,
        out_shape=[jax.ShapeDtypeStruct((r, cc), F32)] * 4,
        compiler_params=_params(("parallel",)),
    )(*[part_index[i] for i in dyn], w, m, v, *gparts)


def _row(vec):
    return vec.reshape(1, -1)


def kernel(x, c, w_ada, b_ada, norm1_g, w_in, conv_w, conv_b, lru_wa, lru_ba, lru_wx, lru_bx, lru_lambda, w_lru_out, w_attn_out, attn_sinks, rel_bias, w_out, norm2_g, w_ff1, w_ff2, final_g, loss_target, m_w_ada, m_b_ada, m_norm1_g, m_w_in, m_conv_w, m_conv_b, m_lru_wa, m_lru_ba, m_lru_wx, m_lru_bx, m_lru_lambda, m_w_lru_out, m_w_attn_out, m_attn_sinks, m_rel_bias, m_w_out, m_norm2_g, m_w_ff1, m_w_ff2, m_final_g, v_w_ada, v_b_ada, v_norm1_g, v_w_in, v_conv_w, v_conv_b, v_lru_wa, v_lru_ba, v_lru_wx, v_lru_bx, v_lru_lambda, v_w_lru_out, v_w_attn_out, v_attn_sinks, v_rel_bias, v_w_out, v_norm2_g, v_w_ff1, v_w_ff2, v_final_g):
    dd = D_MODEL
    t = x.shape[1]
    ix, iy, ic = lax.axis_index("x"), lax.axis_index("y"), lax.axis_index("c")
    me = 4 * ix + 2 * iy + ic
    chip = 2 * ix + iy
    c_idx = jnp.reshape(ic, (1,)).astype(jnp.int32)
    chip_idx = jnp.reshape(chip, (1,)).astype(jnp.int32)

    xs = x[0]
    target = loss_target[0]
    ncol_ada = w_ada.shape[2]
    cshard = conv_w.shape[2]

    shards = [jnp.transpose(w_in[0]).astype(BF16), w_lru_out[0].astype(BF16), w_attn_out[0].astype(BF16),
              w_out[0].astype(BF16), jnp.transpose(w_ff1[0]).astype(BF16), w_ff2[0].astype(BF16)]
    s_in, s_lo, s_ao, s_o, s_f1, s_f2 = shards

    def full(g):
        return g.reshape(N_DEV * g.shape[1], dd)

    def stack(g):
        return g.reshape(N_DEV, g.shape[0] // N_DEV, dd)

    (g_in,) = _all_gather([s_in], "ag_w_in")
    win_t = full(g_in)

    pack0 = jnp.zeros((8, dd), F32).at[0:1, :].set(c).at[1:1 + CONV_WIDTH, 0:cshard].set(conv_w[0])
    (g0,) = _all_gather([pack0], "ag_cond")
    c_all = g0[:, 0, :]
    conv_w_full = jnp.transpose(g0[:, 1:1 + CONV_WIDTH, 0:cshard], (1, 0, 2)).reshape(CONV_WIDTH, dd)
    b_cols = lax.dynamic_slice(b_ada, (0, me * ncol_ada), (1, ncol_ada))
    mod_cols = _ada_fwd(c_all, w_ada[0], b_cols, "ada_fwd")
    (g1,) = _all_gather([mod_cols], "ag_mod")
    mod_mine = lax.dynamic_index_in_dim(g1, me, axis=1, keepdims=False).reshape(6, dd)
    mod = jnp.concatenate([mod_mine, jnp.zeros((2, dd), F32)], axis=0)

    bucket = _bucket_table()
    onehot = jnp.asarray((bucket.reshape(1, -1) == np.arange(N_BUCKETS)[:, None]).astype(np.float32))
    bias = _rel_bias_fwd(rel_bias, onehot, "rel_bias_fwd").reshape(N_Q_HEADS, BLOCK, 2 * BLOCK)
    sinks = attn_sinks.reshape(N_Q_HEADS)
    lp = jnp.concatenate([conv_b, lru_ba, lru_bx, lru_lambda, conv_w_full], axis=0)
    wa, wx = lru_wa[0], lru_wx[0]

    tm = _tile(t, 1024)
    tn = 512
    h = _norm_mod(xs, norm1_g, mod, 0, "norm1")
    (proj,), ((g_o,), (g_f1,)) = _mm(h, win_t, "nt", tm=tm, tn=tn, outs=[BF16], name="in_proj",
                                     jobs=[_gather_job([s_o], 0.25), _gather_job([s_f1], 0.95)])
    (rec, ya), ((g_f2,),) = _lru_fwd(proj, lp, wa, wx, "lru_fwd", jobs=[_gather_job([s_f2], 0.95)])
    att, ((g_lo,), (g_ao,)) = _attn_fwd(proj, bias, sinks, "attn_fwd",
                                        jobs=[_gather_job([s_lo], 0.45), _gather_job([s_ao], 0.9)])
    wlo, wao, wo, wf1_t, wf2 = full(g_lo), full(g_ao), full(g_o), full(g_f1), full(g_f2)
    (y_a,) = _mm(ya, wlo, "nn", tm=tm, tn=tn, outs=[BF16], name="lru_out")

    def merge_epi(acc, ya_t, ga_t, gb_t):
        return acc, _sigmoid(ga_t.astype(F32)) * ya_t.astype(F32) + _sigmoid(gb_t.astype(F32)) * acc

    y_b, merged = _mm(att, wao, "nn", tm=tm, tn=tn, outs=[BF16, BF16], epi=merge_epi,
                      tiles=[(y_a, 0), (proj, OFF_GA // tn), (proj, OFF_GB // tn)], name="attn_out_merge")

    def resid_epi(row):
        def epi(acc, x_t, mod_t):
            return acc, x_t + mod_t[row:row + 1, :] * acc
        return epi

    o1, x1 = _mm(merged, wo, "nn", tm=tm, tn=tn, outs=[BF16, F32], epi=resid_epi(2),
                 tiles=[(xs, 0)], rows=[mod], name="out_proj")
    h2 = _norm_mod(x1, norm2_g, mod, 3, "norm2")

    def relu2_epi(acc):
        rl = jnp.maximum(acc, 0.0)
        return acc, rl * rl

    f_pre, ff = _mm(h2, wf1_t, "nt", tm=tm, tn=tn, outs=[BF16, BF16], epi=relu2_epi, name="ff1")
    o2, x2 = _mm(ff, wf2, "nn", tm=tm, tn=tn, tk=2048, outs=[BF16, F32], epi=resid_epi(5),
                 tiles=[(x1, 0)], rows=[mod], name="ff2")

    loss_blk, dx2, do2, st_f = _final_loss(x2, o2, _row(final_g), target, mod, "final_loss")
    loss = lax.psum(loss_blk[0, 0], ("x", "y", "c"))

    def drelu2_epi(acc, f_t):
        return (acc * (2.0 * jnp.maximum(f_t.astype(F32), 0.0)),)

    tmw = 512
    (df,) = _mm(do2, wf2, "nt", tm=tm, tn=tn, outs=[BF16], epi=drelu2_epi, tiles=[(f_pre, 0)], name="ff2_dx")
    (dwf2,) = _mm(ff, do2, "tn", tm=tmw, tn=tn, outs=[BF16], name="ff2_dw")
    (dh2,), ((sib_f2,),) = _mm(df, wf1_t, "nn", tm=tm, tn=tn, tk=2048, outs=[F32], name="ff1_dx",
                              jobs=[_sibling_job([stack(dwf2)])])
    pair_f2 = _pair_add(stack(dwf2), sib_f2, c_idx, "rs_pair_add_w_ff2")
    (dwf1_t,), ((chips_f2,),) = _mm(df, h2, "tn", tm=tmw, tn=tn, outs=[BF16], name="ff1_dw", jobs=[_chips_job([pair_f2])])
    (dx1, st_2, do1), ((sib_f1,),) = _norm_mod_bwd(x1, norm2_g, mod, 3, dh2, dx2, (o1, 2), "norm2_bwd",
                                                   jobs=[_sibling_job([stack(dwf1_t)])])
    pair_f1 = _pair_add(stack(dwf1_t), sib_f1, c_idx, "rs_pair_add_w_ff1")

    def dmerge_epi(acc, ga_t, gb_t, ya_t, yb_t):
        sa, sb = _sigmoid(ga_t.astype(F32)), _sigmoid(gb_t.astype(F32))
        return (acc * sa, acc * sb, acc * ya_t.astype(F32) * sa * (1.0 - sa), acc * yb_t.astype(F32) * sb * (1.0 - sb))

    tmh = _tile(t, 512)
    dy_a, dy_b, dga, dgb = _mm(do1, wo, "nt", tm=tmh, tn=tn, outs=[BF16] * 4, epi=dmerge_epi,
                               tiles=[(proj, OFF_GA // tn), (proj, OFF_GB // tn), (y_a, 0), (y_b, 0)], name="out_proj_dx")
    (dwo,) = _mm(merged, do1, "tn", tm=tmw, tn=tn, outs=[BF16], name="out_proj_dw")
    (datt,) = _mm(dy_b, wao, "nt", tm=tm, tn=tn, outs=[BF16], name="attn_out_dx")
    (dwao,) = _mm(att, dy_b, "tn", tm=tmw, tn=tn, outs=[BF16], name="attn_out_dw")
    (dya,) = _mm(dy_a, wlo, "nt", tm=tm, tn=tn, outs=[BF16], name="lru_out_dx")
    (dwlo,) = _mm(ya, dy_a, "tn", tm=tmw, tn=tn, outs=[BF16], name="lru_out_dw")
    (dq, dk_pad, dv_pad, dbias, dsinks), (sib_3, (chips_f1,)) = _attn_bwd(
        proj, datt, bias, sinks, "attn_bwd",
        jobs=[_sibling_job([stack(dwlo), stack(dwao), stack(dwo)]), _chips_job([pair_f1])])
    pair_lo, pair_ao, pair_o = [_pair_add(stack(g_), s_, c_idx, "rs_pair_add_" + nm_)
                                for g_, s_, nm_ in zip((dwlo, dwao, dwo), sib_3, ("w_lru_out", "w_attn_out", "w_out"))]
    (dlx, dlg, st_l, dwa, dwx), ((chips_lo, chips_ao, chips_o),) = _lru_bwd(
        proj, rec, dya, lp, wa, wx, "lru_bwd", jobs=[_chips_job([pair_lo, pair_ao, pair_o])])
    dproj = jnp.concatenate([dlx, dlg, dq, dk_pad[BLOCK:].astype(BF16), dv_pad[BLOCK:].astype(BF16), dga, dgb], axis=1)
    (dwin_t,) = _mm(dproj, h, "tn", tm=384, tn=tn, outs=[BF16], name="in_proj_dw")
    (sib_in,) = _exchange_sibling([stack(dwin_t)], "rs_sibling_w_in")
    pair_in = _pair_add(stack(dwin_t), sib_in, c_idx, "rs_pair_add_w_in")
    (dh,), ((chips_in,),) = _mm(dproj, win_t, "nn", tm=tm, tn=tn, tk=IN_WIDTH // 4, outs=[F32], name="in_proj_dx",
                              jobs=[_chips_job([pair_in])])
    grad_x, st_1 = _norm_mod_bwd(xs, norm1_g, mod, 0, dh, dx1, None, "norm1_bwd")

    drel = jnp.transpose(_rel_bias_bwd(dbias.reshape(N_Q_HEADS, -1), onehot, "rel_bias_bwd"))

    misc = jnp.concatenate([drel.reshape(1, -1), dsinks[0:1, 0:N_Q_HEADS],
                            jnp.zeros((1, dd - N_BUCKETS * N_Q_HEADS - N_Q_HEADS), F32)], axis=1)
    zero_row = jnp.zeros((1, dd), F32)
    small = jnp.concatenate([
        st_1[0:2], st_2[3:4], st_2[0:2], st_f[1:2], zero_row, zero_row,
        st_1[2:3], st_l[0:1], st_l[1:2], st_l[2:3], st_l[3:4], st_2[2:3], st_f[0:1], misc,
        st_l[4:8], jnp.zeros((4, dd), F32),
        dwa.reshape(-1, dd), dwx.reshape(-1, dd)], axis=0)
    nsmall = small.shape[0]
    (small_all,) = _all_gather([small], "ag_small")

    def pack_small(b_, n1, cb, ba, bx, lam, n2, fg, rb, sk, wa_, wx_):
        misc_ = jnp.concatenate([rb.reshape(1, -1), sk.reshape(1, -1),
                                 jnp.zeros((1, dd - N_BUCKETS * N_Q_HEADS - N_Q_HEADS), F32)], axis=1)
        return jnp.concatenate([b_.reshape(6, dd), jnp.zeros((2, dd), F32), n1, cb, ba, bx, lam, n2, _row(fg), misc_,
                                jnp.zeros((8, dd), F32), wa_.reshape(-1, dd), wx_.reshape(-1, dd)], axis=0)

    w_s = pack_small(b_ada, norm1_g, conv_b, lru_ba, lru_bx, lru_lambda, norm2_g, final_g, rel_bias, attn_sinks, lru_wa, lru_wx)
    m_s = pack_small(m_b_ada, m_norm1_g, m_conv_b, m_lru_ba, m_lru_bx, m_lru_lambda, m_norm2_g, m_final_g, m_rel_bias, m_attn_sinks, m_lru_wa, m_lru_wx)
    v_s = pack_small(v_b_ada, v_norm1_g, v_conv_b, v_lru_ba, v_lru_bx, v_lru_lambda, v_norm2_g, v_final_g, v_rel_bias, v_attn_sinks, v_lru_wa, v_lru_wx)
    small_res = _adamw(w_s, m_s, v_s, [small_all] * N_DEV, "adamw_small", part_index={i: i for i in range(N_DEV)})

    def unpack_small(s):
        nb_ = N_BUCKETS * N_Q_HEADS
        return dict(
            b_ada=s[0:6].reshape(1, 6 * dd), norm1_g=s[8:9], conv_b=s[9:10], lru_ba=s[10:11], lru_bx=s[11:12],
            lru_lambda=s[12:13], norm2_g=s[13:14], final_g=s[14], rel_bias=s[15, 0:nb_].reshape(N_BUCKETS, N_Q_HEADS),
            attn_sinks=s[15:16, nb_:nb_ + N_Q_HEADS],
            lru_wa=s[24:24 + 128].reshape(1, LRU_BLOCKS, LRU_BLOCK_W, LRU_BLOCK_W),
            lru_wx=s[152:152 + 128].reshape(1, LRU_BLOCKS, LRU_BLOCK_W, LRU_BLOCK_W))

    res = {k: [None] * 4 for k in ("w_ada", "w_in", "conv_w", "w_lru_out", "w_attn_out", "w_out", "w_ff1", "w_ff2")}
    for q_, s_ in enumerate(small_res):
        for k_, val in unpack_small(s_).items():
            res.setdefault(k_, [None] * 4)[q_] = val

    g_conv = lax.dynamic_slice(small_res[0][16:16 + CONV_WIDTH], (0, me * cshard), (CONV_WIDTH, cshard))
    conv_res = _adamw(conv_w[0], m_conv_w[0], v_conv_w[0], [g_conv], "adamw_conv")
    res["conv_w"] = [r_[None] for r_ in conv_res]

    dmod_all = small_all[:, 0:6, :].reshape(N_DEV, 6 * dd)
    dmod_cols = lax.dynamic_slice(dmod_all, (0, me * ncol_ada), (N_DEV, ncol_ada))
    g_ada = _ada_bwd(jnp.transpose(c_all), dmod_cols, "ada_bwd")
    ada_res = _adamw(w_ada[0], m_w_ada[0], v_w_ada[0], [g_ada], "adamw_ada")
    res["w_ada"] = [r_[None] for r_ in ada_res]

    pair = [pair_in, pair_lo, pair_ao, pair_o, pair_f1, pair_f2]
    from_chips = [chips_in, chips_lo, chips_ao, chips_o, chips_f1, chips_f2]

    def summed(i):
        return [pair[i], from_chips[i], from_chips[i], from_chips[i]], {0: chip_idx, 1: 0, 2: 1, 3: 2}

    def sum_only(i, name):
        parts, index = summed(i)
        return _sum_parts(parts, index, name)

    g_in = jnp.transpose(sum_only(0, "sum_w_in"))
    res["w_in"] = [r_[None] for r_ in _adamw(w_in[0], m_w_in[0], v_w_in[0], [g_in], "adamw_w_in")]
    g_ff1 = jnp.transpose(sum_only(4, "sum_w_ff1"))
    res["w_ff1"] = [r_[None] for r_ in _adamw(w_ff1[0], m_w_ff1[0], v_w_ff1[0], [g_ff1], "adamw_w_ff1")]
    for i, (nm, w_, m_, v_) in {1: ("w_lru_out", w_lru_out, m_w_lru_out, v_w_lru_out),
                                2: ("w_attn_out", w_attn_out, m_w_attn_out, v_w_attn_out),
                                3: ("w_out", w_out, m_w_out, v_w_out),
                                5: ("w_ff2", w_ff2, m_w_ff2, v_w_ff2)}.items():
        parts, index = summed(i)
        res[nm] = [r_[None] for r_ in _adamw(w_[0], m_[0], v_[0], parts, "adamw_" + nm, part_index=index)]

    order = ["w_ada", "b_ada", "norm1_g", "w_in", "conv_w", "conv_b", "lru_wa", "lru_ba", "lru_wx", "lru_bx",
             "lru_lambda", "w_lru_out", "w_attn_out", "attn_sinks", "rel_bias", "w_out", "norm2_g", "w_ff1",
             "w_ff2", "final_g"]
    out = [loss, grad_x[None]]
    for q_ in range(4):
        out += [res[k_][q_] for k_ in order]
    return tuple(out)
```

```python
import functools
import math

import numpy as np
import jax
import jax.numpy as jnp
from jax import lax
from jax.experimental import pallas as pl
from jax.experimental.pallas import tpu as pltpu

F32 = jnp.float32
BF16 = jnp.bfloat16
MESH = pl.DeviceIdType.MESH

D_MODEL = 2048
N_Q_HEADS = 32
N_KV_HEADS = 4
GROUP = N_Q_HEADS // N_KV_HEADS
HEAD_DIM = 64
KV_WIDTH = N_KV_HEADS * HEAD_DIM
BLOCK = 128
NEG_INF = -1e30
N_BUCKETS = 32
MAX_DISTANCE = 128
LRU_BLOCKS = 16
LRU_BLOCK_W = 128
CONV_WIDTH = 4
LRU_C = 8.0
D_FF = 4 * D_MODEL
EPS = 1e-6
IN_WIDTH = 5 * D_MODEL + 2 * KV_WIDTH
OFF_LX, OFF_LG, OFF_Q, OFF_K, OFF_V, OFF_GA, OFF_GB = 0, 2048, 4096, 6144, 6400, 6656, 8704

ADAM_LR, ADAM_B1, ADAM_B2, ADAM_EPS, ADAM_WD, ADAM_STEP = 0.001, 0.9, 0.999, 1e-08, 0.01, 10

N_DEV = 8
VMEM_LIMIT_BYTES = 48 * 1024 * 1024
LANE = 128
LRU_CW = 512


def _params(sem, **kw):
    return pltpu.CompilerParams(dimension_semantics=sem, vmem_limit_bytes=VMEM_LIMIT_BYTES, **kw)


def _tile(n, pref, mult=8):
    if n <= pref:
        return n
    t = (pref // mult) * mult
    while t >= mult:
        if n % t == 0:
            return t
        t -= mult
    return n


def _sigmoid(x):
    return 1.0 / (1.0 + jnp.exp(-x))


def _gelu_tanh(x):
    k = math.sqrt(2.0 / math.pi)
    t = jnp.tanh(k * (x + 0.044715 * x * x * x))
    return 0.5 * x * (1.0 + t), t


def _gelu_tanh_grad(x, t):
    k = math.sqrt(2.0 / math.pi)
    return 0.5 * (1.0 + t) + 0.5 * x * (1.0 - t * t) * k * (1.0 + 3.0 * 0.044715 * x * x)


def _log1p(e):
    u = 1.0 + e
    return jnp.where(u == 1.0, e, jnp.log(u) * (e / jnp.where(u == 1.0, 1.0, u - 1.0)))


def _softplus(x):
    return jnp.maximum(x, 0.0) + _log1p(jnp.exp(-jnp.abs(x)))


def _neg_expm1(x):
    series = -x * (1.0 + x * (0.5 + x * (1.0 / 6.0 + x * (1.0 / 24.0 + x * (1.0 / 120.0)))))
    return jnp.where(x > -0.1, series, 1.0 - jnp.exp(x))


def _my_place():
    return lax.axis_index("x"), lax.axis_index("y"), lax.axis_index("c")


def _all_gather(arrs, name):
    return _run_job(_gather_job(arrs), name)


class _Job:
    def __init__(self, ins, outs, sems, start, finish, forward=None, forward_at=0.8):
        self.ins, self.outs, self.sems = list(ins), list(outs), list(sems)
        self.start, self.finish, self.forward, self.forward_at = start, finish, forward, forward_at


def _gather_job(arrs, forward_at=0.8):
    n = len(arrs)

    def copies(ins, outs, sems):
        send_sems, recv_sems, local_sems = sems
        x, y, c = _my_place()
        me, sibling = (x, y, c), (x, y, 1 - c)
        chips = [(1 - x, y), (x, 1 - y), (1 - x, 1 - y)]

        def slot(px, py, pc):
            return 4 * px + 2 * py + pc

        def copy(a, k, block, to, src=None):
            dst = outs[a].at[slot(*block)]
            return pltpu.make_async_remote_copy(
                src_ref=dst if src is None else src, dst_ref=dst,
                send_sem=send_sems.at[a, k], recv_sem=recv_sems.at[a, k],
                device_id=to, device_id_type=MESH)

        mine = [pltpu.make_async_copy(ins[a], outs[a].at[slot(*me)], local_sems.at[a]) for a in range(n)]
        first = []
        for a in range(n):
            first.append(copy(a, 0, me, sibling, src=ins[a]))
            first += [copy(a, 1 + j, me, (*chip, c), src=ins[a]) for j, chip in enumerate(chips)]
        relay = [(copy(a, 1 + j, (*chip, c), me), copy(a, 4 + j, (*chip, c), sibling))
                 for j, chip in enumerate(chips) for a in range(n)]
        last = [copy(a, 0, sibling, me) for a in range(n)]
        last += [copy(a, 4 + j, (*chip, 1 - c), me) for a in range(n) for j, chip in enumerate(chips)]
        return mine, first, relay, last

    def start(ins, outs, sems):
        mine, first, _, _ = copies(ins, outs, sems)
        for cp in mine + first:
            cp.start()

    def forward(ins, outs, sems):
        _, _, relay, _ = copies(ins, outs, sems)
        for arrival, onward in relay:
            arrival.wait_recv()
            onward.start()

    def finish(ins, outs, sems):
        mine, first, relay, last = copies(ins, outs, sems)
        for cp in last:
            cp.wait_recv()
        for cp in first + [onward for _, onward in relay]:
            cp.wait_send()
        for cp in mine:
            cp.wait()

    return _Job(arrs, [jax.ShapeDtypeStruct((N_DEV,) + a.shape, a.dtype) for a in arrs],
                [pltpu.SemaphoreType.DMA((n, 7)), pltpu.SemaphoreType.DMA((n, 7)), pltpu.SemaphoreType.DMA((n,))],
                start, finish, forward, forward_at)


def _run_job(job, name):
    ni, no = len(job.ins), len(job.outs)

    def body(*refs):
        ins, outs, sems = refs[:ni], refs[ni:ni + no], refs[ni + no:]
        job.start(ins, outs, sems)
        if job.forward is not None:
            job.forward(ins, outs, sems)
        job.finish(ins, outs, sems)

    any_spec = pl.BlockSpec(memory_space=pl.ANY)
    return pl.pallas_call(
        body, name=name, out_shape=job.outs, in_specs=[any_spec] * ni, out_specs=[any_spec] * no,
        scratch_shapes=job.sems,
    )(*job.ins)


def _call(body, *, name, grid, in_specs, out_specs, out_shape, scratch_shapes=(), sem, args, jobs=()):
    in_specs, out_specs, out_shape, scratch_shapes = list(in_specs), list(out_specs), list(out_shape), list(scratch_shapes)
    if not jobs:
        res = pl.pallas_call(body, name=name, grid=grid, in_specs=in_specs, out_specs=out_specs, out_shape=out_shape,
                             scratch_shapes=scratch_shapes, compiler_params=_params(sem))(*args)
        return list(res), []
    n_in, n_out, n_scr = len(in_specs), len(out_specs), len(scratch_shapes)
    j_in = [len(j.ins) for j in jobs]
    j_out = [len(j.outs) for j in jobs]
    j_sem = [len(j.sems) for j in jobs]
    total = int(np.prod(grid))
    any_spec = pl.BlockSpec(memory_space=pl.ANY)

    def wrapped(*refs):
        pos = [0]

        def take(k):
            part = refs[pos[0]:pos[0] + k]
            pos[0] += k
            return part

        ins = take(n_in)
        jins = [take(k) for k in j_in]
        outs = take(n_out)
        jouts = [take(k) for k in j_out]
        scr = take(n_scr)
        jsems = [take(k) for k in j_sem]
        step = pl.program_id(0)
        for d in range(1, len(grid)):
            step = step * grid[d] + pl.program_id(d)
        for j, job in enumerate(jobs):
            pl.when(step == 0)(functools.partial(job.start, jins[j], jouts[j], jsems[j]))
            if job.forward is not None:
                at = min(int(total * job.forward_at), total - 1)
                pl.when(step == at)(functools.partial(job.forward, jins[j], jouts[j], jsems[j]))
        body(*ins, *outs, *scr)
        for j, job in enumerate(jobs):
            pl.when(step == total - 1)(functools.partial(job.finish, jins[j], jouts[j], jsems[j]))

    res = pl.pallas_call(
        wrapped, name=name, grid=grid,
        in_specs=in_specs + [any_spec] * sum(j_in),
        out_specs=out_specs + [any_spec] * sum(j_out),
        out_shape=out_shape + [o for j in jobs for o in j.outs],
        scratch_shapes=scratch_shapes + [s for j in jobs for s in j.sems],
        compiler_params=_params(("arbitrary",) * len(grid)),
    )(*args, *[a for j in jobs for a in j.ins])
    res = list(res)
    own, rest = res[:n_out], res[n_out:]
    per_job = []
    for k in j_out:
        per_job.append(rest[:k])
        rest = rest[k:]
    return own, per_job


def _exchange_sibling(arrs, name):
    return _run_job(_sibling_job(arrs), name)


def _sibling_job(arrs):
    n = len(arrs)

    def copies(ins, outs, sems):
        send_sems, recv_sems = sems
        x, y, c = _my_place()
        return [pltpu.make_async_remote_copy(
            src_ref=ins[a].at[2 * k + 1 - c], dst_ref=outs[a].at[k],
            send_sem=send_sems.at[a, k], recv_sem=recv_sems.at[a, k],
            device_id=(x, y, 1 - c), device_id_type=MESH) for a in range(n) for k in range(4)]

    def start(ins, outs, sems):
        for cp in copies(ins, outs, sems):
            cp.start()

    def finish(ins, outs, sems):
        for cp in copies(ins, outs, sems):
            cp.wait()

    return _Job(arrs, [jax.ShapeDtypeStruct((4,) + a.shape[1:], a.dtype) for a in arrs],
                [pltpu.SemaphoreType.DMA((n, 4)), pltpu.SemaphoreType.DMA((n, 4))], start, finish)


def _chips_job(arrs):
    n = len(arrs)

    def copies(ins, outs, sems):
        send_sems, recv_sems = sems
        x, y, c = _my_place()
        chips = [(1 - x, y), (x, 1 - y), (1 - x, 1 - y)]
        return [pltpu.make_async_remote_copy(
            src_ref=ins[a].at[2 * px + py], dst_ref=outs[a].at[j],
            send_sem=send_sems.at[a, j], recv_sem=recv_sems.at[a, j],
            device_id=(px, py, c), device_id_type=MESH) for a in range(n) for j, (px, py) in enumerate(chips)]

    def start(ins, outs, sems):
        for cp in copies(ins, outs, sems):
            cp.start()

    def finish(ins, outs, sems):
        for cp in copies(ins, outs, sems):
            cp.wait()

    return _Job(arrs, [jax.ShapeDtypeStruct((3,) + a.shape[1:], a.dtype) for a in arrs],
                [pltpu.SemaphoreType.DMA((n, 3)), pltpu.SemaphoreType.DMA((n, 3))], start, finish)


def _pair_add(stack, from_sibling, c_idx, name):
    _, r, cc = stack.shape
    tr = _tile(r, 512)

    def body(c_ref, a_ref, b_ref, o_ref):
        o_ref[...] = (a_ref[...].astype(F32) + b_ref[...].astype(F32)).astype(o_ref.dtype)

    grid_spec = pltpu.PrefetchScalarGridSpec(
        num_scalar_prefetch=1, grid=(4, r // tr),
        in_specs=[pl.BlockSpec((None, tr, cc), lambda k, i, c_ref: (2 * k + c_ref[0], i, 0)),
                  pl.BlockSpec((None, tr, cc), lambda k, i, c_ref: (k, i, 0))],
        out_specs=pl.BlockSpec((None, tr, cc), lambda k, i, c_ref: (k, i, 0)))
    return pl.pallas_call(
        body, name=name, grid_spec=grid_spec,
        out_shape=jax.ShapeDtypeStruct((4, r, cc), BF16),
        compiler_params=_params(("parallel", "parallel")),
    )(c_idx, stack, from_sibling)


_DIMS = {"nn": ((1,), (0,)), "nt": ((1,), (1,)), "tn": ((0,), (0,))}


def _mm(a, b, mode, *, tm, tn, tk=None, outs, epi=None, tiles=(), rows=(), name, jobs=()):
    if mode == "tn":
        kk, m = a.shape
    else:
        m, kk = a.shape
    nn = b.shape[0] if mode == "nt" else b.shape[1]
    tk = kk if tk is None else tk
    assert m % tm == 0 and nn % tn == 0 and kk % tk == 0, (name, a.shape, b.shape, tm, tn, tk)
    nk = kk // tk
    nt, nr, no = len(tiles), len(rows), len(outs)

    def body(*refs):
        a_ref, b_ref = refs[:2]
        tile_refs = refs[2:2 + nt]
        row_refs = refs[2 + nt:2 + nt + nr]
        out_refs = refs[2 + nt + nr:2 + nt + nr + no]
        part = lax.dot_general(a_ref[...], b_ref[...], (_DIMS[mode], ((), ())), preferred_element_type=F32)

        def finish(acc):
            if epi is None:
                res = (acc,)
            else:
                res = epi(acc, *[t[...] for t in tile_refs], *[r[...] for r in row_refs])
            for o_ref, val in zip(out_refs, res):
                o_ref[...] = val.astype(o_ref.dtype)

        if nk == 1:
            finish(part)
        else:
            acc_ref = refs[-1]
            k = pl.program_id(2)

            @pl.when(k == 0)
            def _():
                acc_ref[...] = part

            @pl.when(k > 0)
            def _():
                acc_ref[...] += part

            @pl.when(k == nk - 1)
            def _():
                finish(acc_ref[...])

    if mode == "tn":
        a_spec = pl.BlockSpec((tk, tm), lambda i, j, k: (k, i))
    else:
        a_spec = pl.BlockSpec((tm, tk), lambda i, j, k: (i, k))
    if mode == "nt":
        b_spec = pl.BlockSpec((tn, tk), lambda i, j, k: (j, k))
    else:
        b_spec = pl.BlockSpec((tk, tn), lambda i, j, k: (k, j))
    tile_specs = [pl.BlockSpec((tm, tn), functools.partial(lambda i, j, k, off: (i, j + off), off=off))
                  for _, off in tiles]
    row_specs = [pl.BlockSpec((r.shape[0], tn), lambda i, j, k: (0, j)) for r in rows]
    out_spec = pl.BlockSpec((tm, tn), lambda i, j, k: (i, j))
    res, jres = _call(
        body, name=name, grid=(m // tm, nn // tn, nk),
        in_specs=[a_spec, b_spec] + tile_specs + row_specs,
        out_specs=[out_spec] * no,
        out_shape=[jax.ShapeDtypeStruct((m, nn), dt) for dt in outs],
        scratch_shapes=[pltpu.VMEM((tm, tn), F32)] if nk > 1 else [],
        sem=("parallel", "parallel", "arbitrary"),
        args=[a, b, *[t for t, _ in tiles], *rows], jobs=jobs)
    return (res, jres) if jobs else res


def _ada_fwd(c_all, w_ada, b_ada_cols, name):
    nb, dd = c_all.shape
    ncol = w_ada.shape[1]
    tn = _tile(ncol, 512, LANE)

    def body(c_ref, w_ref, b_ref, o_ref):
        cv = c_ref[...]
        act = (cv * _sigmoid(cv)).astype(BF16)
        o_ref[...] = jnp.dot(act, w_ref[...].astype(BF16), preferred_element_type=F32) + b_ref[...]

    return pl.pallas_call(
        body, name=name, grid=(ncol // tn,),
        in_specs=[pl.BlockSpec((nb, dd), lambda j: (0, 0)), pl.BlockSpec((dd, tn), lambda j: (0, j)),
                  pl.BlockSpec((1, tn), lambda j: (0, j))],
        out_specs=pl.BlockSpec((nb, tn), lambda j: (0, j)),
        out_shape=jax.ShapeDtypeStruct((nb, ncol), F32),
        compiler_params=_params(("parallel",)),
    )(c_all, w_ada, b_ada_cols)


def _ada_bwd(c_all_t, dmod_cols, name):
    dd, nb = c_all_t.shape
    ncol = dmod_cols.shape[1]
    tr = _tile(dd, 256)

    def body(c_ref, d_ref, o_ref):
        cv = c_ref[...]
        act = (cv * _sigmoid(cv)).astype(BF16).astype(F32)
        dm = d_ref[...].astype(BF16).astype(F32)
        acc = act[:, 0:1] * dm[0:1, :]
        for bi in range(1, nb):
            acc = acc + act[:, bi:bi + 1] * dm[bi:bi + 1, :]
        o_ref[...] = acc

    return pl.pallas_call(
        body, name=name, grid=(dd // tr,),
        in_specs=[pl.BlockSpec((tr, nb), lambda i: (i, 0)), pl.BlockSpec((nb, ncol), lambda i: (0, 0))],
        out_specs=pl.BlockSpec((tr, ncol), lambda i: (i, 0)),
        out_shape=jax.ShapeDtypeStruct((dd, ncol), F32),
        compiler_params=_params(("parallel",)),
    )(c_all_t, dmod_cols)


def _norm_mod(x, gain, mod, row0, name):
    t, dd = x.shape
    tt = _tile(t, 512)

    def body(x_ref, g_ref, m_ref, h_ref):
        xv = x_ref[...]
        r = lax.rsqrt(jnp.mean(xv * xv, axis=-1, keepdims=True) + EPS)
        shift, scale = m_ref[row0:row0 + 1, :], m_ref[row0 + 1:row0 + 2, :]
        h_ref[...] = ((xv * r * g_ref[...]) * (1.0 + scale) + shift).astype(BF16)

    return pl.pallas_call(
        body, name=name, grid=(t // tt,),
        in_specs=[pl.BlockSpec((tt, dd), lambda i: (i, 0)), pl.BlockSpec((1, dd), lambda i: (0, 0)),
                  pl.BlockSpec((8, dd), lambda i: (0, 0))],
        out_specs=pl.BlockSpec((tt, dd), lambda i: (i, 0)),
        out_shape=jax.ShapeDtypeStruct((t, dd), BF16),
        compiler_params=_params(("parallel",)),
    )(x, gain, mod)


def _norm_mod_bwd(x, gain, mod, row0, dh, dres, branch, name, jobs=()):
    t, dd = x.shape
    tt = _tile(t, 256)
    has_branch = branch is not None

    def body(*refs):
        if has_branch:
            x_ref, g_ref, m_ref, dh_ref, dres_ref, o_ref, dx_ref, st_ref, do_ref = refs
        else:
            x_ref, g_ref, m_ref, dh_ref, dres_ref, dx_ref, st_ref = refs

        @pl.when(pl.program_id(0) == 0)
        def _():
            st_ref[...] = jnp.zeros_like(st_ref)

        xv = x_ref[...]
        g = g_ref[...]
        scale = m_ref[row0 + 1:row0 + 2, :]
        r = lax.rsqrt(jnp.mean(xv * xv, axis=-1, keepdims=True) + EPS)
        nv = xv * r
        dhv = dh_ref[...].astype(F32)
        st_ref[0:1, :] += jnp.sum(dhv, axis=0, keepdims=True)
        st_ref[1:2, :] += jnp.sum(dhv * (nv * g), axis=0, keepdims=True)
        dng = dhv * (1.0 + scale)
        st_ref[2:3, :] += jnp.sum(dng * nv, axis=0, keepdims=True)
        dn = dng * g
        dx = dres_ref[...] + r * (dn - nv * jnp.mean(dn * nv, axis=-1, keepdims=True))
        dx_ref[...] = dx
        if has_branch:
            ov = o_ref[...].astype(F32)
            gate = m_ref[branch[1]:branch[1] + 1, :]
            st_ref[3:4, :] += jnp.sum(dx * ov, axis=0, keepdims=True)
            do_ref[...] = (dx * gate).astype(BF16)

    row = pl.BlockSpec((tt, dd), lambda i: (i, 0))
    in_specs = [row, pl.BlockSpec((1, dd), lambda i: (0, 0)), pl.BlockSpec((8, dd), lambda i: (0, 0)), row, row]
    args = [x, gain, mod, dh, dres]
    out_specs = [row, pl.BlockSpec((8, dd), lambda i: (0, 0))]
    out_shape = [jax.ShapeDtypeStruct((t, dd), F32), jax.ShapeDtypeStruct((8, dd), F32)]
    if has_branch:
        in_specs.append(row)
        args.append(branch[0])
        out_specs.append(row)
        out_shape.append(jax.ShapeDtypeStruct((t, dd), BF16))
    res, jres = _call(body, name=name, grid=(t // tt,), in_specs=in_specs, out_specs=out_specs, out_shape=out_shape,
                      sem=("arbitrary",), args=args, jobs=jobs)
    return (res, jres) if jobs else res


def _final_loss(x2, o2, final_g, target, mod, name):
    t, dd = x2.shape
    tt = _tile(t, 256)
    nsteps = t // tt

    def body(x_ref, o_ref, g_ref, t_ref, m_ref, loss_ref, dx_ref, do_ref, st_ref, lacc):
        i = pl.program_id(0)

        @pl.when(i == 0)
        def _():
            st_ref[...] = jnp.zeros_like(st_ref)
            lacc[...] = jnp.zeros_like(lacc)

        xv = x_ref[...]
        g = g_ref[...]
        r = lax.rsqrt(jnp.mean(xv * xv, axis=-1, keepdims=True) + EPS)
        nv = xv * r
        err = nv * g - t_ref[...]
        lacc[...] += jnp.sum(err * err, axis=0, keepdims=True)
        dy = err * (1.0 / dd)
        st_ref[0:1, :] += jnp.sum(dy * nv, axis=0, keepdims=True)
        dn = dy * g
        dx = r * (dn - nv * jnp.mean(dn * nv, axis=-1, keepdims=True))
        dx_ref[...] = dx
        st_ref[1:2, :] += jnp.sum(dx * o_ref[...].astype(F32), axis=0, keepdims=True)
        do_ref[...] = (dx * m_ref[5:6, :]).astype(BF16)

        @pl.when(i == nsteps - 1)
        def _():
            total = jnp.sum(lacc[...], axis=-1, keepdims=True) * (0.5 / dd)
            loss_ref[...] = jnp.broadcast_to(total, loss_ref.shape)

    row = pl.BlockSpec((tt, dd), lambda i: (i, 0))
    return pl.pallas_call(
        body, name=name, grid=(nsteps,),
        in_specs=[row, row, pl.BlockSpec((1, dd), lambda i: (0, 0)), row, pl.BlockSpec((8, dd), lambda i: (0, 0))],
        out_specs=[pl.BlockSpec((8, LANE), lambda i: (0, 0)), row, row, pl.BlockSpec((8, dd), lambda i: (0, 0))],
        out_shape=[jax.ShapeDtypeStruct((8, LANE), F32), jax.ShapeDtypeStruct((t, dd), F32),
                   jax.ShapeDtypeStruct((t, dd), BF16), jax.ShapeDtypeStruct((8, dd), F32)],
        scratch_shapes=[pltpu.VMEM((1, dd), F32)],
        compiler_params=_params(("arbitrary",)),
    )(x2, o2, final_g, target, mod)


def _lru_gates(xt, prev8, lp, wa_ref, wx_ref, first_tile, cw):
    tt = xt.shape[0]
    ext = jnp.concatenate([prev8, xt], axis=0)
    xs = [pltpu.roll(ext, s, 0)[8:, :] if s else xt for s in range(CONV_WIDTH)]
    xc = lp[0:1, :] + lp[7:8, :] * xs[0] + lp[6:7, :] * xs[1] + lp[5:6, :] * xs[2] + lp[4:5, :] * xs[3]
    xcb = xc.astype(BF16)
    za, zi = [], []
    for jb in range(cw // LRU_BLOCK_W):
        sl = slice(jb * LRU_BLOCK_W, (jb + 1) * LRU_BLOCK_W)
        za.append(jnp.dot(xcb[:, sl], wa_ref[jb].astype(BF16), preferred_element_type=F32))
        zi.append(jnp.dot(xcb[:, sl], wx_ref[jb].astype(BF16), preferred_element_type=F32))
    ra = _sigmoid(jnp.concatenate(za, axis=1) + lp[1:2, :])
    ri = _sigmoid(jnp.concatenate(zi, axis=1) + lp[2:3, :])
    sp = _softplus(-lp[3:4, :])
    log_a = -LRU_C * ra * sp
    av = jnp.exp(log_a)
    seq_start = jnp.logical_and(first_tile, lax.broadcasted_iota(jnp.int32, (tt, 1), 0) == 0)
    mult = jnp.where(seq_start, 1.0, jnp.sqrt(_neg_expm1(2.0 * log_a)))
    return xs, xc, xcb, ra, ri, sp, av, mult, seq_start


def _scan_groups(a, u, reverse):
    tt, cw = a.shape
    a3, u3 = a.reshape(tt // 8, 8, cw), u.reshape(tt // 8, 8, cw)
    r = lax.broadcasted_iota(jnp.int32, (1, 8, 1), 1)
    for s in (1, 2, 4):
        shift = 8 - s if reverse else s
        take = r < 8 - s if reverse else r >= s
        u3 = jnp.where(take, a3 * pltpu.roll(u3, shift, 1) + u3, u3)
        a3 = jnp.where(take, a3 * pltpu.roll(a3, shift, 1), a3)
    return a3, u3


def _scan_carry(a_s, u_s, h_s, carry, reverse):
    ng = a_s.shape[0]

    def step(i, h):
        g = ng - 1 - i if reverse else i
        hg = u_s[g] + a_s[g] * h
        h_s[g] = hg
        return hg[0:1, :] if reverse else hg[7:8, :]

    return lax.fori_loop(0, ng, step, carry, unroll=4)


def _lru_fwd(proj, lp, wa, wx, name, jobs=()):
    t = proj.shape[0]
    w = D_MODEL
    cw = LRU_CW
    tt = _tile(t, 256)
    nb = cw // LRU_BLOCK_W

    def body(lx_ref, lg_ref, lp_ref, wa_ref, wx_ref, rec_ref, ya_ref, a_s, u_s, h_s, halo, carry):
        ti = pl.program_id(1)

        @pl.when(ti == 0)
        def _():
            halo[...] = jnp.zeros_like(halo)
            carry[...] = jnp.zeros_like(carry)

        xt = lx_ref[...].astype(F32)
        lp_v = lp_ref[...]
        _, xc, _, _, ri, _, av, mult, _ = _lru_gates(xt, halo[...], lp_v, wa_ref, wx_ref, ti == 0, cw)
        halo[...] = xt[tt - 8:, :]
        a_s[...], u_s[...] = _scan_groups(av, mult * (ri * xc), False)
        carry[...] = _scan_carry(a_s, u_s, h_s, carry[...], False)
        rec = h_s[...].reshape(tt, cw)
        rec_ref[...] = rec.astype(BF16)
        gelu, _ = _gelu_tanh(lg_ref[...].astype(F32))
        ya_ref[...] = (rec * gelu).astype(BF16)

    off_g = OFF_LG // cw
    res, jres = _call(
        body, name=name, grid=(w // cw, t // tt),
        in_specs=[pl.BlockSpec((tt, cw), lambda ci, ti: (ti, ci)),
                  pl.BlockSpec((tt, cw), lambda ci, ti: (ti, off_g + ci)),
                  pl.BlockSpec((8, cw), lambda ci, ti: (0, ci)),
                  pl.BlockSpec((nb, LRU_BLOCK_W, LRU_BLOCK_W), lambda ci, ti: (ci, 0, 0)),
                  pl.BlockSpec((nb, LRU_BLOCK_W, LRU_BLOCK_W), lambda ci, ti: (ci, 0, 0))],
        out_specs=[pl.BlockSpec((tt, cw), lambda ci, ti: (ti, ci))] * 2,
        out_shape=[jax.ShapeDtypeStruct((t, w), BF16)] * 2,
        scratch_shapes=[pltpu.VMEM((tt // 8, 8, cw), F32)] * 3 + [pltpu.VMEM((8, cw), F32), pltpu.VMEM((1, cw), F32)],
        sem=("parallel", "arbitrary"), args=[proj, proj, lp, wa, wx], jobs=jobs)
    return (res, jres) if jobs else res


def _lru_bwd(proj, rec, dya, lp, wa, wx, name, jobs=()):
    t = proj.shape[0]
    w = D_MODEL
    cw = LRU_CW
    tt = _tile(t, 256)
    nt = t // tt
    nb = cw // LRU_BLOCK_W
    r8 = tt // 8

    def body(lx_ref, lxp_ref, lg_ref, rec_ref, recp_ref, dya_ref, lp_ref, wa_ref, wx_ref,
             dlx_ref, dlg_ref, st_ref, dwa_ref, dwx_ref, a_s, d_s, dh_s, nhalo, carry):
        step_i = pl.program_id(1)
        ti = nt - 1 - step_i

        @pl.when(step_i == 0)
        def _():
            st_ref[...] = jnp.zeros_like(st_ref)
            dwa_ref[...] = jnp.zeros_like(dwa_ref)
            dwx_ref[...] = jnp.zeros_like(dwx_ref)
            nhalo[...] = jnp.zeros_like(nhalo)
            carry[...] = jnp.zeros_like(carry)

        first = ti == 0
        keep = jnp.where(first, 0.0, 1.0)
        xt = lx_ref[...].astype(F32)
        prev8 = lxp_ref[...].astype(F32) * keep
        lp_v = lp_ref[...]
        xs, xc, xcb, ra, ri, sp, av, mult, seq_start = _lru_gates(xt, prev8, lp_v, wa_ref, wx_ref, first, cw)

        lg = lg_ref[...].astype(F32)
        gelu, th = _gelu_tanh(lg)
        dyav = dya_ref[...].astype(F32)
        recv = rec_ref[...].astype(F32)
        dlg_ref[...] = (dyav * recv * _gelu_tanh_grad(lg, th)).astype(BF16)

        drec = dyav * gelu
        e_in = carry[...]
        a_s[...], d_s[...] = _scan_groups(av, av * drec, True)
        carry[...] = _scan_carry(a_s, d_s, dh_s, e_in, True)
        e_next = jnp.concatenate([dh_s[...].reshape(tt, cw), jnp.broadcast_to(e_in, (8, cw))], axis=0)
        dh = drec + pltpu.roll(e_next, tt + 7, 0)[:tt, :]
        rprev8 = recp_ref[...].astype(F32) * keep
        hprev = pltpu.roll(jnp.concatenate([rprev8, recv], axis=0), 1, 0)[8:, :]
        da = dh * hprev
        dmult = jnp.where(seq_start, 0.0, dh * ri * xc)
        dri = dh * mult * xc
        dxc = dh * mult * ri
        dlog_a = da * av - dmult * (av * av) / mult
        dra = dlog_a * (-LRU_C * sp)
        lam = lp_v[3:4, :]
        st_ref[3:4, :] += jnp.sum(dlog_a * ra, axis=0, keepdims=True) * (LRU_C * _sigmoid(-lam))
        dza = dra * ra * (1.0 - ra)
        dzi = dri * ri * (1.0 - ri)
        st_ref[1:2, :] += jnp.sum(dza, axis=0, keepdims=True)
        st_ref[2:3, :] += jnp.sum(dzi, axis=0, keepdims=True)
        dzab, dzib = dza.astype(BF16), dzi.astype(BF16)
        back = []
        for jb in range(nb):
            sl = slice(jb * LRU_BLOCK_W, (jb + 1) * LRU_BLOCK_W)
            dwa_ref[jb] += lax.dot_general(xcb[:, sl], dzab[:, sl], (_DIMS["tn"], ((), ())), preferred_element_type=F32)
            dwx_ref[jb] += lax.dot_general(xcb[:, sl], dzib[:, sl], (_DIMS["tn"], ((), ())), preferred_element_type=F32)
            back.append(
                lax.dot_general(dzab[:, sl], wa_ref[jb].astype(BF16), (_DIMS["nt"], ((), ())), preferred_element_type=F32)
                + lax.dot_general(dzib[:, sl], wx_ref[jb].astype(BF16), (_DIMS["nt"], ((), ())), preferred_element_type=F32))
        dxc = dxc + jnp.concatenate(back, axis=1)
        st_ref[0:1, :] += jnp.sum(dxc, axis=0, keepdims=True)
        for k in range(CONV_WIDTH):
            st_ref[4 + k:5 + k, :] += jnp.sum(dxc * xs[CONV_WIDTH - 1 - k], axis=0, keepdims=True)
        ext = jnp.concatenate([dxc, nhalo[...]], axis=0)
        dlx = lp_v[7:8, :] * dxc
        for s in range(1, CONV_WIDTH):
            dlx = dlx + lp_v[7 - s:8 - s, :] * pltpu.roll(ext, tt + 8 - s, 0)[:tt, :]
        dlx_ref[...] = dlx.astype(BF16)
        nhalo[...] = dxc[0:8, :]

    off_g = OFF_LG // cw
    tile = lambda off: pl.BlockSpec((tt, cw), lambda ci, si: (nt - 1 - si, off + ci))
    prev = lambda off: pl.BlockSpec((8, cw), lambda ci, si: (jnp.maximum((nt - 1 - si) * r8 - 1, 0), off + ci))
    wspec = pl.BlockSpec((nb, LRU_BLOCK_W, LRU_BLOCK_W), lambda ci, si: (ci, 0, 0))
    st_spec = pl.BlockSpec((8, cw), lambda ci, si: (0, ci))
    res, jres = _call(
        body, name=name, grid=(w // cw, nt),
        in_specs=[tile(0), prev(0), tile(off_g), tile(0), prev(0), tile(0), st_spec, wspec, wspec],
        out_specs=[tile(0), tile(0), st_spec, wspec, wspec],
        out_shape=[jax.ShapeDtypeStruct((t, w), BF16), jax.ShapeDtypeStruct((t, w), BF16),
                   jax.ShapeDtypeStruct((8, w), F32),
                   jax.ShapeDtypeStruct((LRU_BLOCKS, LRU_BLOCK_W, LRU_BLOCK_W), F32),
                   jax.ShapeDtypeStruct((LRU_BLOCKS, LRU_BLOCK_W, LRU_BLOCK_W), F32)],
        scratch_shapes=[pltpu.VMEM((tt // 8, 8, cw), F32)] * 3 + [pltpu.VMEM((8, cw), F32), pltpu.VMEM((1, cw), F32)],
        sem=("parallel", "arbitrary"), args=[proj, proj, proj, rec, rec, dya, lp, wa, wx], jobs=jobs)
    return (res, jres) if jobs else res


def _band_valid(first_block):
    qi = lax.broadcasted_iota(jnp.int32, (BLOCK, 2 * BLOCK), 0)
    ki = lax.broadcasted_iota(jnp.int32, (BLOCK, 2 * BLOCK), 1)
    rel = qi + BLOCK - ki
    valid = jnp.logical_and(rel >= 0, rel < BLOCK)
    return jnp.logical_and(valid, jnp.logical_or(ki >= BLOCK, jnp.logical_not(first_block)))


ATTN_STACK_BWD = 8


def _low_half():
    return lax.broadcasted_iota(jnp.int32, (1, LANE), 1) < HEAD_DIM


def _stack_heads(x, h0, ns):
    low = _low_half()
    parts = []
    for g in range(ns):
        h = h0 + g
        slab = x[:, (h // 2) * LANE:(h // 2 + 1) * LANE]
        parts.append(jnp.where(low if h % 2 == 0 else jnp.logical_not(low), slab, jnp.zeros_like(slab)))
    return jnp.concatenate(parts, axis=0)


def _unstack_heads(y, ns):
    low = _low_half()
    pairs = [jnp.where(low, y[(2 * j) * BLOCK:(2 * j + 1) * BLOCK], y[(2 * j + 1) * BLOCK:(2 * j + 2) * BLOCK])
             for j in range(ns // 2)]
    return pairs[0] if len(pairs) == 1 else jnp.concatenate(pairs, axis=1)


def _dup_kv(xall, kv):
    part = xall[:, kv * HEAD_DIM:(kv + 1) * HEAD_DIM]
    return jnp.concatenate([part, part], axis=1)


def _fold_halves(a):
    return a + pltpu.roll(a, HEAD_DIM, 1)


def _group_probs(qg, k2, bias_g, sink_g, valid):
    s = lax.dot_general(qg, k2, (_DIMS["nt"], ((), ())), preferred_element_type=F32) * (HEAD_DIM ** -0.5)
    s = jnp.where(valid[None], s.reshape(bias_g.shape) + bias_g, NEG_INF)
    m = jnp.maximum(jnp.max(s, axis=-1, keepdims=True), sink_g)
    e = jnp.exp(s - m)
    es = jnp.exp(sink_g - m)
    inv = 1.0 / (jnp.sum(e, axis=-1, keepdims=True) + es)
    return e * inv, es * inv


def _group_sinks(s_ref, h0, ns):
    return jnp.concatenate([jnp.full((1, BLOCK, 1), s_ref[h0 + g], F32) for g in range(ns)], axis=0)


def _attn_specs(t):
    nblk = t // BLOCK
    q_spec = pl.BlockSpec((BLOCK, D_MODEL), lambda n: (n, OFF_Q // D_MODEL))
    kc = pl.BlockSpec((BLOCK, KV_WIDTH), lambda n: (n, OFF_K // KV_WIDTH))
    kp = pl.BlockSpec((BLOCK, KV_WIDTH), lambda n: (jnp.maximum(n - 1, 0), OFF_K // KV_WIDTH))
    vc = pl.BlockSpec((BLOCK, KV_WIDTH), lambda n: (n, OFF_V // KV_WIDTH))
    vp = pl.BlockSpec((BLOCK, KV_WIDTH), lambda n: (jnp.maximum(n - 1, 0), OFF_V // KV_WIDTH))
    bias_spec = pl.BlockSpec((N_Q_HEADS, BLOCK, 2 * BLOCK), lambda n: (0, 0, 0))
    sink_spec = pl.BlockSpec(memory_space=pltpu.SMEM)
    return nblk, [q_spec, kp, kc, vp, vc, bias_spec, sink_spec]


def _attn_fwd(proj, bias, sinks, name, jobs=()):
    t = proj.shape[0]
    nblk, in_specs = _attn_specs(t)

    def body(q_ref, kp_ref, kc_ref, vp_ref, vc_ref, b_ref, s_ref, o_ref):
        valid = _band_valid(pl.program_id(0) == 0)
        q = q_ref[...]
        kall = jnp.concatenate([kp_ref[...], kc_ref[...]], axis=0)
        vall = jnp.concatenate([vp_ref[...], vc_ref[...]], axis=0)
        outs = []
        for h in range(N_Q_HEADS):
            kv = h // GROUP
            kk = kall[:, kv * HEAD_DIM:(kv + 1) * HEAD_DIM]
            vv = vall[:, kv * HEAD_DIM:(kv + 1) * HEAD_DIM]
            s = lax.dot_general(q[:, h * HEAD_DIM:(h + 1) * HEAD_DIM], kk, (_DIMS["nt"], ((), ())),
                                preferred_element_type=F32) * (HEAD_DIM ** -0.5)
            s = jnp.where(valid, s + b_ref[h], NEG_INF)
            sink = s_ref[h]
            m = jnp.maximum(jnp.max(s, axis=-1, keepdims=True), sink)
            e = jnp.exp(s - m)
            p = e * (1.0 / (jnp.sum(e, axis=-1, keepdims=True) + jnp.exp(sink - m)))
            outs.append(jnp.dot(p.astype(BF16), vv, preferred_element_type=F32))
        o_ref[...] = jnp.concatenate(outs, axis=1).astype(BF16)

    res, jres = _call(
        body, name=name, grid=(nblk,), in_specs=in_specs,
        out_specs=[pl.BlockSpec((BLOCK, D_MODEL), lambda n: (n, 0))],
        out_shape=[jax.ShapeDtypeStruct((t, D_MODEL), BF16)],
        sem=("parallel",), args=[proj, proj, proj, proj, proj, bias, sinks], jobs=jobs)
    return (res[0], jres) if jobs else res[0]


def _attn_bwd(proj, datt, bias, sinks, name, jobs=()):
    t = proj.shape[0]
    nblk, in_specs = _attn_specs(t)
    in_specs = in_specs + [pl.BlockSpec((BLOCK, D_MODEL), lambda n: (n, 0))]
    scale = HEAD_DIM ** -0.5

    def body(q_ref, kp_ref, kc_ref, vp_ref, vc_ref, b_ref, s_ref, do_ref, dq_ref, dk_ref, dv_ref, db_ref, ds_ref):
        n = pl.program_id(0)

        @pl.when(n == 0)
        def _():
            dk_ref[...] = jnp.zeros_like(dk_ref)
            dv_ref[...] = jnp.zeros_like(dv_ref)
            db_ref[...] = jnp.zeros_like(db_ref)
            ds_ref[...] = jnp.zeros_like(ds_ref)

        valid = _band_valid(n == 0)
        lane = lax.broadcasted_iota(jnp.int32, (1, LANE), 1)
        q = q_ref[...]
        do = do_ref[...]
        kall = jnp.concatenate([kp_ref[...], kc_ref[...]], axis=0)
        vall = jnp.concatenate([vp_ref[...], vc_ref[...]], axis=0)
        low = _low_half()
        dqs, dks, dvs = [], [], []
        dsink_row = jnp.zeros((1, LANE), F32)
        ns = ATTN_STACK_BWD
        for kv in range(N_KV_HEADS):
            k2, v2 = _dup_kv(kall, kv), _dup_kv(vall, kv)
            dk_acc = jnp.zeros((2 * BLOCK, LANE), F32)
            dv_acc = jnp.zeros((2 * BLOCK, LANE), F32)
            for h0 in range(kv * GROUP, (kv + 1) * GROUP, ns):
                qg, dog = _stack_heads(q, h0, ns), _stack_heads(do, h0, ns)
                p, psink = _group_probs(qg, k2, b_ref[h0:h0 + ns], _group_sinks(s_ref, h0, ns), valid)
                dp = lax.dot_general(dog, v2, (_DIMS["nt"], ((), ())), preferred_element_type=F32)
                dp = dp.reshape(ns, BLOCK, 2 * BLOCK)
                delta = jnp.sum(p * dp, axis=-1, keepdims=True)
                ds = p * (dp - delta)
                db_ref[h0:h0 + ns] += ds
                dsink = -jnp.sum(psink * delta, axis=1, keepdims=True)
                for g in range(ns):
                    dsink_row = dsink_row + jnp.where(lane == h0 + g, dsink[g], 0.0)
                dsb = ds.reshape(ns * BLOCK, 2 * BLOCK).astype(BF16)
                pb = p.reshape(ns * BLOCK, 2 * BLOCK).astype(BF16)
                dqs.append(_unstack_heads(jnp.dot(dsb, k2, preferred_element_type=F32) * scale, ns))
                dk_acc = dk_acc + lax.dot_general(dsb, qg, (_DIMS["tn"], ((), ())), preferred_element_type=F32)
                dv_acc = dv_acc + lax.dot_general(pb, dog, (_DIMS["tn"], ((), ())), preferred_element_type=F32)
            dks.append(_fold_halves(dk_acc) * scale)
            dvs.append(_fold_halves(dv_acc))
        dq_ref[...] = jnp.concatenate(dqs, axis=1).astype(BF16)
        rows = pl.ds(pl.multiple_of(n * BLOCK, BLOCK), 2 * BLOCK)
        dk_ref[rows, :] += jnp.concatenate([jnp.where(low, dks[0], dks[1]), jnp.where(low, dks[2], dks[3])], axis=1)
        dv_ref[rows, :] += jnp.concatenate([jnp.where(low, dvs[0], dvs[1]), jnp.where(low, dvs[2], dvs[3])], axis=1)
        ds_ref[0:1, :] += dsink_row

    res, jres = _call(
        body, name=name, grid=(nblk,), in_specs=in_specs,
        out_specs=[pl.BlockSpec((BLOCK, D_MODEL), lambda n: (n, 0)),
                   pl.BlockSpec((t + BLOCK, KV_WIDTH), lambda n: (0, 0)),
                   pl.BlockSpec((t + BLOCK, KV_WIDTH), lambda n: (0, 0)),
                   pl.BlockSpec((N_Q_HEADS, BLOCK, 2 * BLOCK), lambda n: (0, 0, 0)),
                   pl.BlockSpec((8, LANE), lambda n: (0, 0))],
        out_shape=[jax.ShapeDtypeStruct((t, D_MODEL), BF16),
                   jax.ShapeDtypeStruct((t + BLOCK, KV_WIDTH), F32),
                   jax.ShapeDtypeStruct((t + BLOCK, KV_WIDTH), F32),
                   jax.ShapeDtypeStruct((N_Q_HEADS, BLOCK, 2 * BLOCK), F32),
                   jax.ShapeDtypeStruct((8, LANE), F32)],
        sem=("arbitrary",), args=[proj, proj, proj, proj, proj, bias, sinks, datt], jobs=jobs)
    return (res, jres) if jobs else res


def _bucket_table():
    qi = np.arange(BLOCK)[:, None]
    ki = np.arange(2 * BLOCK)[None, :]
    rel = np.maximum(qi + BLOCK - ki, 0)
    max_exact = N_BUCKETS // 2
    relf = np.maximum(rel, 1).astype(np.float32)
    large = max_exact + (np.log(relf / max_exact) / math.log(MAX_DISTANCE / max_exact)
                         * (N_BUCKETS - max_exact)).astype(np.int32)
    large = np.minimum(large, N_BUCKETS - 1)
    return np.where(rel < max_exact, rel, large).astype(np.int32)


def _rel_bias_fwd(rel_bias, onehot, name):
    n = onehot.shape[1]
    tn = _tile(n, 8192, LANE)

    def body(r_ref, o_ref, out_ref):
        out_ref[...] = lax.dot_general(r_ref[...], o_ref[...], (_DIMS["tn"], ((), ())),
                                       precision=lax.Precision.HIGHEST, preferred_element_type=F32)

    return pl.pallas_call(
        body, name=name, grid=(n // tn,),
        in_specs=[pl.BlockSpec(rel_bias.shape, lambda j: (0, 0)), pl.BlockSpec((N_BUCKETS, tn), lambda j: (0, j))],
        out_specs=pl.BlockSpec((N_Q_HEADS, tn), lambda j: (0, j)),
        out_shape=jax.ShapeDtypeStruct((N_Q_HEADS, n), F32),
        compiler_params=_params(("parallel",)),
    )(rel_bias, onehot)


def _rel_bias_bwd(dbias, onehot, name):
    def body(d_ref, o_ref, out_ref):
        out_ref[...] = lax.dot_general(d_ref[...], o_ref[...], (_DIMS["nt"], ((), ())),
                                       precision=lax.Precision.HIGHEST, preferred_element_type=F32)

    full = pl.BlockSpec(dbias.shape, lambda: (0, 0))
    return pl.pallas_call(
        body, name=name, in_specs=[full, full],
        out_specs=pl.BlockSpec((N_Q_HEADS, N_BUCKETS), lambda: (0, 0)),
        out_shape=jax.ShapeDtypeStruct((N_Q_HEADS, N_BUCKETS), F32),
        compiler_params=_params(()),
    )(dbias, onehot)


def _part_specs(gparts, part_index, tr, cc):
    blk = pl.BlockSpec((tr, cc), lambda i, *_: (i, 0))
    specs = []
    for i, _ in enumerate(gparts):
        if i in part_index:
            ix = part_index[i]
            if isinstance(ix, int):
                specs.append(pl.BlockSpec((None, tr, cc), functools.partial(lambda i, *_, ix: (ix, i, 0), ix=ix)))
            else:
                specs.append(pl.BlockSpec((None, tr, cc), lambda i, ix_ref: (ix_ref[0], i, 0)))
        else:
            specs.append(blk)
    return blk, specs


def _sum_parts(gparts, part_index, name):
    r, cc = gparts[0].shape[-2:]
    tr = _tile(r, 256)
    dyn = [i for i, ix in part_index.items() if not isinstance(ix, int)]
    assert len(dyn) <= 1

    def body(*refs):
        refs = refs[len(dyn):]
        g = refs[0][...].astype(F32)
        for gr in refs[1:-1]:
            g = g + gr[...].astype(F32)
        refs[-1][...] = g

    blk, specs = _part_specs(gparts, part_index, tr, cc)
    grid_spec = pltpu.PrefetchScalarGridSpec(num_scalar_prefetch=len(dyn), grid=(r // tr,), in_specs=specs, out_specs=blk)
    return pl.pallas_call(
        body, name=name, grid_spec=grid_spec, out_shape=jax.ShapeDtypeStruct((r, cc), F32),
        compiler_params=_params(("parallel",)),
    )(*[part_index[i] for i in dyn], *gparts)


def _adamw(w, m, v, gparts, name, part_index=None):
    r, cc = w.shape
    tr = _tile(r, 128)
    np_ = len(gparts)
    part_index = part_index or {}
    dyn = [i for i, ix in part_index.items() if not isinstance(ix, int)]
    assert len(dyn) <= 1
    bc1 = 1.0 / (1.0 - ADAM_B1 ** ADAM_STEP)
    bc2 = 1.0 / (1.0 - ADAM_B2 ** ADAM_STEP)

    def body(*refs):
        refs = refs[len(dyn):]
        w_ref, m_ref, v_ref = refs[:3]
        g_refs = refs[3:3 + np_]
        g_out, d_out, m_out, v_out = refs[3 + np_:]
        g = g_refs[0][...].astype(F32)
        for gr in g_refs[1:]:
            g = g + gr[...].astype(F32)
        mn = ADAM_B1 * m_ref[...] + (1.0 - ADAM_B1) * g
        vn = ADAM_B2 * v_ref[...] + (1.0 - ADAM_B2) * (g * g)
        g_out[...] = g
        m_out[...] = mn
        v_out[...] = vn
        d_out[...] = -ADAM_LR * ((mn * bc1) / (jnp.sqrt(vn * bc2) + ADAM_EPS) + ADAM_WD * w_ref[...])

    blk, g_specs = _part_specs(gparts, part_index, tr, cc)
    grid_spec = pltpu.PrefetchScalarGridSpec(
        num_scalar_prefetch=len(dyn), grid=(r // tr,),
        in_specs=[blk, blk, blk] + g_specs, out_specs=[blk] * 4)
    return pl.pallas_call(
        body, name=name, grid_spec=grid_spec,
        out_shape=[jax.ShapeDtypeStruct((r, cc), F32)] * 4,
        compiler_params=_params(("parallel",)),
    )(*[part_index[i] for i in dyn], w, m, v, *gparts)


def _row(vec):
    return vec.reshape(1, -1)


def kernel(x, c, w_ada, b_ada, norm1_g, w_in, conv_w, conv_b, lru_wa, lru_ba, lru_wx, lru_bx, lru_lambda, w_lru_out, w_attn_out, attn_sinks, rel_bias, w_out, norm2_g, w_ff1, w_ff2, final_g, loss_target, m_w_ada, m_b_ada, m_norm1_g, m_w_in, m_conv_w, m_conv_b, m_lru_wa, m_lru_ba, m_lru_wx, m_lru_bx, m_lru_lambda, m_w_lru_out, m_w_attn_out, m_attn_sinks, m_rel_bias, m_w_out, m_norm2_g, m_w_ff1, m_w_ff2, m_final_g, v_w_ada, v_b_ada, v_norm1_g, v_w_in, v_conv_w, v_conv_b, v_lru_wa, v_lru_ba, v_lru_wx, v_lru_bx, v_lru_lambda, v_w_lru_out, v_w_attn_out, v_attn_sinks, v_rel_bias, v_w_out, v_norm2_g, v_w_ff1, v_w_ff2, v_final_g):
    dd = D_MODEL
    t = x.shape[1]
    ix, iy, ic = lax.axis_index("x"), lax.axis_index("y"), lax.axis_index("c")
    me = 4 * ix + 2 * iy + ic
    chip = 2 * ix + iy
    c_idx = jnp.reshape(ic, (1,)).astype(jnp.int32)
    chip_idx = jnp.reshape(chip, (1,)).astype(jnp.int32)

    xs = x[0]
    target = loss_target[0]
    ncol_ada = w_ada.shape[2]
    cshard = conv_w.shape[2]

    shards = [jnp.transpose(w_in[0]).astype(BF16), w_lru_out[0].astype(BF16), w_attn_out[0].astype(BF16),
              w_out[0].astype(BF16), jnp.transpose(w_ff1[0]).astype(BF16), w_ff2[0].astype(BF16)]
    s_in, s_lo, s_ao, s_o, s_f1, s_f2 = shards

    def full(g):
        return g.reshape(N_DEV * g.shape[1], dd)

    def stack(g):
        return g.reshape(N_DEV, g.shape[0] // N_DEV, dd)

    (g_in,) = _all_gather([s_in], "ag_w_in")
    win_t = full(g_in)

    pack0 = jnp.zeros((8, dd), F32).at[0:1, :].set(c).at[1:1 + CONV_WIDTH, 0:cshard].set(conv_w[0])
    (g0,) = _all_gather([pack0], "ag_cond")
    c_all = g0[:, 0, :]
    conv_w_full = jnp.transpose(g0[:, 1:1 + CONV_WIDTH, 0:cshard], (1, 0, 2)).reshape(CONV_WIDTH, dd)
    b_cols = lax.dynamic_slice(b_ada, (0, me * ncol_ada), (1, ncol_ada))
    mod_cols = _ada_fwd(c_all, w_ada[0], b_cols, "ada_fwd")
    (g1,) = _all_gather([mod_cols], "ag_mod")
    mod_mine = lax.dynamic_index_in_dim(g1, me, axis=1, keepdims=False).reshape(6, dd)
    mod = jnp.concatenate([mod_mine, jnp.zeros((2, dd), F32)], axis=0)

    bucket = _bucket_table()
    onehot = jnp.asarray((bucket.reshape(1, -1) == np.arange(N_BUCKETS)[:, None]).astype(np.float32))
    bias = _rel_bias_fwd(rel_bias, onehot, "rel_bias_fwd").reshape(N_Q_HEADS, BLOCK, 2 * BLOCK)
    sinks = attn_sinks.reshape(N_Q_HEADS)
    lp = jnp.concatenate([conv_b, lru_ba, lru_bx, lru_lambda, conv_w_full], axis=0)
    wa, wx = lru_wa[0], lru_wx[0]

    tm = _tile(t, 1024)
    tn = 512
    h = _norm_mod(xs, norm1_g, mod, 0, "norm1")
    (proj,), ((g_o,), (g_f1,)) = _mm(h, win_t, "nt", tm=tm, tn=768, outs=[BF16], name="in_proj",
                                     jobs=[_gather_job([s_o], 0.25), _gather_job([s_f1], 0.95)])
    (rec, ya), ((g_f2,),) = _lru_fwd(proj, lp, wa, wx, "lru_fwd", jobs=[_gather_job([s_f2], 0.95)])
    att, ((g_lo,), (g_ao,)) = _attn_fwd(proj, bias, sinks, "attn_fwd",
                                        jobs=[_gather_job([s_lo], 0.45), _gather_job([s_ao], 0.9)])
    wlo, wao, wo, wf1_t, wf2 = full(g_lo), full(g_ao), full(g_o), full(g_f1), full(g_f2)
    (y_a,) = _mm(ya, wlo, "nn", tm=tm, tn=tn, outs=[BF16], name="lru_out")

    def merge_epi(acc, ya_t, ga_t, gb_t):
        return acc, _sigmoid(ga_t.astype(F32)) * ya_t.astype(F32) + _sigmoid(gb_t.astype(F32)) * acc

    y_b, merged = _mm(att, wao, "nn", tm=tm, tn=tn, outs=[BF16, BF16], epi=merge_epi,
                      tiles=[(y_a, 0), (proj, OFF_GA // tn), (proj, OFF_GB // tn)], name="attn_out_merge")

    def resid_epi(row):
        def epi(acc, x_t, mod_t):
            return acc, x_t + mod_t[row:row + 1, :] * acc
        return epi

    o1, x1 = _mm(merged, wo, "nn", tm=tm, tn=tn, outs=[BF16, F32], epi=resid_epi(2),
                 tiles=[(xs, 0)], rows=[mod], name="out_proj")
    h2 = _norm_mod(x1, norm2_g, mod, 3, "norm2")

    def relu2_epi(acc):
        rl = jnp.maximum(acc, 0.0)
        return acc, rl * rl

    f_pre, ff = _mm(h2, wf1_t, "nt", tm=tm, tn=1024, outs=[BF16, BF16], epi=relu2_epi, name="ff1")
    tmh = _tile(t, 512)
    o2, x2 = _mm(ff, wf2, "nn", tm=tmh, tn=tn, outs=[BF16, F32], epi=resid_epi(5),
                 tiles=[(x1, 0)], rows=[mod], name="ff2")

    loss_blk, dx2, do2, st_f = _final_loss(x2, o2, _row(final_g), target, mod, "final_loss")
    loss = lax.psum(loss_blk[0, 0], ("x", "y", "c"))

    def drelu2_epi(acc, f_t):
        return (acc * (2.0 * jnp.maximum(f_t.astype(F32), 0.0)),)

    tmw, tnw, tkw = 512, dd, _tile(t, 2048)
    (df,) = _mm(do2, wf2, "nt", tm=tm, tn=1024, outs=[BF16], epi=drelu2_epi, tiles=[(f_pre, 0)], name="ff2_dx")
    (dwf2,) = _mm(ff, do2, "tn", tm=tmw, tn=tnw, tk=tkw, outs=[BF16], name="ff2_dw")
    (dh2,), ((sib_f2,),) = _mm(df, wf1_t, "nn", tm=tmh, tn=tn, outs=[F32], name="ff1_dx",
                              jobs=[_sibling_job([stack(dwf2)])])
    pair_f2 = _pair_add(stack(dwf2), sib_f2, c_idx, "rs_pair_add_w_ff2")
    (dwf1_t,), ((chips_f2,),) = _mm(df, h2, "tn", tm=tmw, tn=tnw, tk=tkw, outs=[BF16], name="ff1_dw",
                                    jobs=[_chips_job([pair_f2])])
    (dx1, st_2, do1), ((sib_f1,),) = _norm_mod_bwd(x1, norm2_g, mod, 3, dh2, dx2, (o1, 2), "norm2_bwd",
                                                   jobs=[_sibling_job([stack(dwf1_t)])])
    pair_f1 = _pair_add(stack(dwf1_t), sib_f1, c_idx, "rs_pair_add_w_ff1")

    def dmerge_epi(acc, ga_t, gb_t, ya_t, yb_t):
        sa, sb = _sigmoid(ga_t.astype(F32)), _sigmoid(gb_t.astype(F32))
        return (acc * sa, acc * sb, acc * ya_t.astype(F32) * sa * (1.0 - sa), acc * yb_t.astype(F32) * sb * (1.0 - sb))

    tmh = _tile(t, 512)
    dy_a, dy_b, dga, dgb = _mm(do1, wo, "nt", tm=tmh, tn=tn, outs=[BF16] * 4, epi=dmerge_epi,
                               tiles=[(proj, OFF_GA // tn), (proj, OFF_GB // tn), (y_a, 0), (y_b, 0)], name="out_proj_dx")
    (dwo,) = _mm(merged, do1, "tn", tm=tmw, tn=tnw, tk=tkw, outs=[BF16], name="out_proj_dw")
    (datt,) = _mm(dy_b, wao, "nt", tm=tm, tn=tn, outs=[BF16], name="attn_out_dx")
    (dwao,) = _mm(att, dy_b, "tn", tm=tmw, tn=tnw, tk=tkw, outs=[BF16], name="attn_out_dw")
    (dya,) = _mm(dy_a, wlo, "nt", tm=tm, tn=tn, outs=[BF16], name="lru_out_dx")
    (dwlo,) = _mm(ya, dy_a, "tn", tm=tmw, tn=tnw, tk=tkw, outs=[BF16], name="lru_out_dw")
    (dq, dk_pad, dv_pad, dbias, dsinks), (sib_3, (chips_f1,)) = _attn_bwd(
        proj, datt, bias, sinks, "attn_bwd",
        jobs=[_sibling_job([stack(dwlo), stack(dwao), stack(dwo)]), _chips_job([pair_f1])])
    pair_lo, pair_ao, pair_o = [_pair_add(stack(g_), s_, c_idx, "rs_pair_add_" + nm_)
                                for g_, s_, nm_ in zip((dwlo, dwao, dwo), sib_3, ("w_lru_out", "w_attn_out", "w_out"))]
    (dlx, dlg, st_l, dwa, dwx), ((chips_lo, chips_ao, chips_o),) = _lru_bwd(
        proj, rec, dya, lp, wa, wx, "lru_bwd", jobs=[_chips_job([pair_lo, pair_ao, pair_o])])
    dproj = jnp.concatenate([dlx, dlg, dq, dk_pad[BLOCK:].astype(BF16), dv_pad[BLOCK:].astype(BF16), dga, dgb], axis=1)
    lru_mats = jnp.concatenate([dwa.reshape(-1, dd), dwx.reshape(-1, dd)], axis=0)
    (dwin_t,), ((mats_all,),) = _mm(dproj, h, "tn", tm=384, tn=tnw, tk=tkw, outs=[BF16], name="in_proj_dw",
                                   jobs=[_gather_job([lru_mats], 0.5)])
    (sib_in,) = _exchange_sibling([stack(dwin_t)], "rs_sibling_w_in")
    pair_in = _pair_add(stack(dwin_t), sib_in, c_idx, "rs_pair_add_w_in")
    (dh,), ((chips_in,),) = _mm(dproj, win_t, "nn", tm=tmh, tn=tn, tk=IN_WIDTH // 2, outs=[F32], name="in_proj_dx",
                              jobs=[_chips_job([pair_in])])
    grad_x, st_1 = _norm_mod_bwd(xs, norm1_g, mod, 0, dh, dx1, None, "norm1_bwd")

    drel = jnp.transpose(_rel_bias_bwd(dbias.reshape(N_Q_HEADS, -1), onehot, "rel_bias_bwd"))

    misc = jnp.concatenate([drel.reshape(1, -1), dsinks[0:1, 0:N_Q_HEADS],
                            jnp.zeros((1, dd - N_BUCKETS * N_Q_HEADS - N_Q_HEADS), F32)], axis=1)
    zero_row = jnp.zeros((1, dd), F32)
    small = jnp.concatenate([
        st_1[0:2], st_2[3:4], st_2[0:2], st_f[1:2], zero_row, zero_row,
        st_1[2:3], st_l[0:1], st_l[1:2], st_l[2:3], st_l[3:4], st_2[2:3], st_f[0:1], misc,
        st_l[4:8], jnp.zeros((4, dd), F32)], axis=0)
    (small_all,) = _all_gather([small], "ag_small")

    def pack_small(b_, n1, cb, ba, bx, lam, n2, fg, rb, sk):
        misc_ = jnp.concatenate([rb.reshape(1, -1), sk.reshape(1, -1),
                                 jnp.zeros((1, dd - N_BUCKETS * N_Q_HEADS - N_Q_HEADS), F32)], axis=1)
        return jnp.concatenate([b_.reshape(6, dd), jnp.zeros((2, dd), F32), n1, cb, ba, bx, lam, n2, _row(fg), misc_,
                                jnp.zeros((8, dd), F32)], axis=0)

    def pack_mats(wa_, wx_):
        return jnp.concatenate([wa_.reshape(-1, dd), wx_.reshape(-1, dd)], axis=0)

    every = {i: i for i in range(N_DEV)}
    w_s = pack_small(b_ada, norm1_g, conv_b, lru_ba, lru_bx, lru_lambda, norm2_g, final_g, rel_bias, attn_sinks)
    m_s = pack_small(m_b_ada, m_norm1_g, m_conv_b, m_lru_ba, m_lru_bx, m_lru_lambda, m_norm2_g, m_final_g, m_rel_bias, m_attn_sinks)
    v_s = pack_small(v_b_ada, v_norm1_g, v_conv_b, v_lru_ba, v_lru_bx, v_lru_lambda, v_norm2_g, v_final_g, v_rel_bias, v_attn_sinks)
    small_res = _adamw(w_s, m_s, v_s, [small_all] * N_DEV, "adamw_small", part_index=every)
    mats_res = _adamw(pack_mats(lru_wa, lru_wx), pack_mats(m_lru_wa, m_lru_wx), pack_mats(v_lru_wa, v_lru_wx),
                      [mats_all] * N_DEV, "adamw_lru_mats", part_index=every)

    def unpack_small(s, mt):
        nb_ = N_BUCKETS * N_Q_HEADS
        half = mt.shape[0] // 2
        return dict(
            b_ada=s[0:6].reshape(1, 6 * dd), norm1_g=s[8:9], conv_b=s[9:10], lru_ba=s[10:11], lru_bx=s[11:12],
            lru_lambda=s[12:13], norm2_g=s[13:14], final_g=s[14], rel_bias=s[15, 0:nb_].reshape(N_BUCKETS, N_Q_HEADS),
            attn_sinks=s[15:16, nb_:nb_ + N_Q_HEADS],
            lru_wa=mt[:half].reshape(1, LRU_BLOCKS, LRU_BLOCK_W, LRU_BLOCK_W),
            lru_wx=mt[half:].reshape(1, LRU_BLOCKS, LRU_BLOCK_W, LRU_BLOCK_W))

    res = {k: [None] * 4 for k in ("w_ada", "w_in", "conv_w", "w_lru_out", "w_attn_out", "w_out", "w_ff1", "w_ff2")}
    for q_, (s_, mt_) in enumerate(zip(small_res, mats_res)):
        for k_, val in unpack_small(s_, mt_).items():
            res.setdefault(k_, [None] * 4)[q_] = val

    g_conv = lax.dynamic_slice(small_res[0][16:16 + CONV_WIDTH], (0, me * cshard), (CONV_WIDTH, cshard))
    conv_res = _adamw(conv_w[0], m_conv_w[0], v_conv_w[0], [g_conv], "adamw_conv")
    res["conv_w"] = [r_[None] for r_ in conv_res]

    dmod_all = small_all[:, 0:6, :].reshape(N_DEV, 6 * dd)
    dmod_cols = lax.dynamic_slice(dmod_all, (0, me * ncol_ada), (N_DEV, ncol_ada))
    g_ada = _ada_bwd(jnp.transpose(c_all), dmod_cols, "ada_bwd")
    ada_res = _adamw(w_ada[0], m_w_ada[0], v_w_ada[0], [g_ada], "adamw_ada")
    res["w_ada"] = [r_[None] for r_ in ada_res]

    pair = [pair_in, pair_lo, pair_ao, pair_o, pair_f1, pair_f2]
    from_chips = [chips_in, chips_lo, chips_ao, chips_o, chips_f1, chips_f2]

    def summed(i):
        return [pair[i], from_chips[i], from_chips[i], from_chips[i]], {0: chip_idx, 1: 0, 2: 1, 3: 2}

    def sum_only(i, name):
        parts, index = summed(i)
        return _sum_parts(parts, index, name)

    g_in = jnp.transpose(sum_only(0, "sum_w_in"))
    res["w_in"] = [r_[None] for r_ in _adamw(w_in[0], m_w_in[0], v_w_in[0], [g_in], "adamw_w_in")]
    g_ff1 = jnp.transpose(sum_only(4, "sum_w_ff1"))
    res["w_ff1"] = [r_[None] for r_ in _adamw(w_ff1[0], m_w_ff1[0], v_w_ff1[0], [g_ff1], "adamw_w_ff1")]
    for i, (nm, w_, m_, v_) in {1: ("w_lru_out", w_lru_out, m_w_lru_out, v_w_lru_out),
                                2: ("w_attn_out", w_attn_out, m_w_attn_out, v_w_attn_out),
                                3: ("w_out", w_out, m_w_out, v_w_out),
                                5: ("w_ff2", w_ff2, m_w_ff2, v_w_ff2)}.items():
        parts, index = summed(i)
        res[nm] = [r_[None] for r_ in _adamw(w_[0], m_[0], v_[0], parts, "adamw_" + nm, part_index=index)]

    order = ["w_ada", "b_ada", "norm1_g", "w_in", "conv_w", "conv_b", "lru_wa", "lru_ba", "lru_wx", "lru_bx",
             "lru_lambda", "w_lru_out", "w_attn_out", "attn_sinks", "rel_bias", "w_out", "norm2_g", "w_ff1",
             "w_ff2", "final_g"]
    out = [loss, grad_x[None]]
    for q_ in range(4):
        out += [res[k_][q_] for k_ in order]
    return tuple(out)
```

```python
import functools
import math

import numpy as np
import jax
import jax.numpy as jnp
from jax import lax
from jax.experimental import pallas as pl
from jax.experimental.pallas import tpu as pltpu

F32 = jnp.float32
BF16 = jnp.bfloat16
MESH = pl.DeviceIdType.MESH

D_MODEL = 2048
N_Q_HEADS = 32
N_KV_HEADS = 4
GROUP = N_Q_HEADS // N_KV_HEADS
HEAD_DIM = 64
KV_WIDTH = N_KV_HEADS * HEAD_DIM
BLOCK = 128
NEG_INF = -1e30
N_BUCKETS = 32
MAX_DISTANCE = 128
LRU_BLOCKS = 16
LRU_BLOCK_W = 128
CONV_WIDTH = 4
LRU_C = 8.0
D_FF = 4 * D_MODEL
EPS = 1e-6
IN_WIDTH = 5 * D_MODEL + 2 * KV_WIDTH
OFF_LX, OFF_LG, OFF_Q, OFF_K, OFF_V, OFF_GA, OFF_GB = 0, 2048, 4096, 6144, 6400, 6656, 8704

ADAM_LR, ADAM_B1, ADAM_B2, ADAM_EPS, ADAM_WD, ADAM_STEP = 0.001, 0.9, 0.999, 1e-08, 0.01, 10

N_DEV = 8
VMEM_LIMIT_BYTES = 48 * 1024 * 1024
LANE = 128
LRU_CW = 512


def _params(sem, **kw):
    return pltpu.CompilerParams(dimension_semantics=sem, vmem_limit_bytes=VMEM_LIMIT_BYTES, **kw)


def _tile(n, pref, mult=8):
    if n <= pref:
        return n
    t = (pref // mult) * mult
    while t >= mult:
        if n % t == 0:
            return t
        t -= mult
    return n


def _sigmoid(x):
    return 1.0 / (1.0 + jnp.exp(-x))


def _gelu_tanh(x):
    k = math.sqrt(2.0 / math.pi)
    t = jnp.tanh(k * (x + 0.044715 * x * x * x))
    return 0.5 * x * (1.0 + t), t


def _gelu_tanh_grad(x, t):
    k = math.sqrt(2.0 / math.pi)
    return 0.5 * (1.0 + t) + 0.5 * x * (1.0 - t * t) * k * (1.0 + 3.0 * 0.044715 * x * x)


def _log1p(e):
    u = 1.0 + e
    return jnp.where(u == 1.0, e, jnp.log(u) * (e / jnp.where(u == 1.0, 1.0, u - 1.0)))


def _softplus(x):
    return jnp.maximum(x, 0.0) + _log1p(jnp.exp(-jnp.abs(x)))


def _neg_expm1(x):
    series = -x * (1.0 + x * (0.5 + x * (1.0 / 6.0 + x * (1.0 / 24.0 + x * (1.0 / 120.0)))))
    return jnp.where(x > -0.1, series, 1.0 - jnp.exp(x))


def _my_place():
    return lax.axis_index("x"), lax.axis_index("y"), lax.axis_index("c")


def _all_gather(arrs, name):
    return _run_job(_gather_job(arrs), name)


class _Job:
    def __init__(self, ins, outs, sems, start, finish, forwards=(), forward_at=()):
        self.ins, self.outs, self.sems = list(ins), list(outs), list(sems)
        self.start, self.finish, self.forwards, self.forward_at = start, finish, list(forwards), list(forward_at)
        assert len(self.forwards) <= len(self.forward_at)


def _gather_job(arrs, forward_at=(0.6, 0.85), part="all"):
    n = len(arrs)
    whole, near, far = part == "all", part == "near", part == "far"

    def copies(ins, outs, sems):
        send_sems, recv_sems, local_sems = sems
        x, y, c = _my_place()
        sib = (x, y, 1 - c)
        xn, yn, dg = (1 - x, y), (x, 1 - y), (1 - x, 1 - y)

        def slot(px, py, pc):
            return 4 * px + 2 * py + pc

        def cp(a, k, src, dst_slot, to):
            return pltpu.make_async_remote_copy(
                src_ref=src, dst_ref=outs[a].at[dst_slot], send_sem=send_sems.at[a, k], recv_sem=recv_sems.at[a, k],
                device_id=to, device_id_type=MESH)

        def arrival(a, k, dst_slot):
            return cp(a, k, outs[a].at[dst_slot], dst_slot, sib)

        me_slot = slot(x, y, c)
        sends, local, relay, passes, arrive = {}, [], {}, {}, {}
        for a in range(n):
            if whole:
                own = ins[a]
                local.append(pltpu.make_async_copy(own, outs[a].at[me_slot], local_sems.at[a]))
                sends[a, 0] = cp(a, 0, own, me_slot, sib)
            if near:
                own = ins[a].at[c]
                local.append(pltpu.make_async_copy(ins[a], outs[a].at[pl.ds(2 * (2 * x + y), 2)], local_sems.at[a]))
            if whole or near:
                sends[a, 1] = cp(a, 1, own, me_slot, (*xn, c))
                sends[a, 2] = cp(a, 2, own, me_slot, (*yn, c))
                passes[a, 4] = cp(a, 4, outs[a].at[slot(*xn, c)], slot(*xn, c), sib)
                passes[a, 5] = cp(a, 5, outs[a].at[slot(*yn, c)], slot(*yn, c), sib)
            if whole or far:
                source = outs[a] if whole else ins[a]
                relayed = c * slot(*yn, c) + (1 - c) * slot(*xn, c)
                relay[a, 3] = cp(a, 3, source.at[relayed], relayed, (x + c * (1 - 2 * x), y + (1 - c) * (1 - 2 * y), c))
                passes[a, 6] = cp(a, 6, outs[a].at[slot(*dg, c)], slot(*dg, c), sib)
            arrive[a, 0] = arrival(a, 0, slot(x, y, 1 - c))
            arrive[a, 1] = arrival(a, 1, slot(*xn, c))
            arrive[a, 2] = arrival(a, 2, slot(*yn, c))
            arrive[a, 3] = arrival(a, 3, slot(*dg, c))
            arrive[a, 4] = arrival(a, 4, slot(*xn, 1 - c))
            arrive[a, 5] = arrival(a, 5, slot(*yn, 1 - c))
            arrive[a, 6] = arrival(a, 6, slot(*dg, 1 - c))
        return sends, local, relay, passes, arrive

    def pick(d, ks):
        return [d[a, k] for a in range(n) for k in ks if (a, k) in d]

    def start(ins, outs, sems):
        sends, local, relay, _, _ = copies(ins, outs, sems)
        for cp in local + pick(sends, (0, 1, 2)) + (pick(relay, (3,)) if far else []):
            cp.start()

    def forward_neighbours(ins, outs, sems):
        _, _, relay, passes, arrive = copies(ins, outs, sems)
        for cp in pick(arrive, (1, 2)):
            cp.wait_recv()
        for cp in pick(relay, (3,)) + pick(passes, (4, 5)):
            cp.start()

    def forward_diagonal(ins, outs, sems):
        _, _, _, passes, arrive = copies(ins, outs, sems)
        for cp in pick(arrive, (3,)):
            cp.wait_recv()
        for cp in pick(passes, (6,)):
            cp.start()

    def finish(ins, outs, sems):
        sends, local, relay, passes, arrive = copies(ins, outs, sems)
        for cp in pick(arrive, (6,) if far else (0, 4, 5, 6) if whole else (4, 5)):
            cp.wait_recv()
        for cp in pick(sends, (0, 1, 2)) + pick(relay, (3,)) + pick(passes, (4, 5, 6)):
            cp.wait_send()
        for cp in local:
            cp.wait()

    shapes = [a.shape[-2:] for a in arrs]
    forwards = [forward_diagonal] if far else [forward_neighbours] if near else [forward_neighbours, forward_diagonal]
    return _Job(arrs, [jax.ShapeDtypeStruct((N_DEV,) + s, a.dtype) for s, a in zip(shapes, arrs)],
                [pltpu.SemaphoreType.DMA((n, 7)), pltpu.SemaphoreType.DMA((n, 7)), pltpu.SemaphoreType.DMA((n,))],
                start, finish, forwards, forward_at)


def _pair_job(arrs):
    n = len(arrs)

    def copies(ins, outs, sems):
        send_sems, recv_sems, local_sems = sems
        x, y, c = _my_place()
        remote = [pltpu.make_async_remote_copy(
            src_ref=ins[a], dst_ref=outs[a].at[c], send_sem=send_sems.at[a], recv_sem=recv_sems.at[a],
            device_id=(x, y, 1 - c), device_id_type=MESH) for a in range(n)]
        local = [pltpu.make_async_copy(ins[a], outs[a].at[c], local_sems.at[a]) for a in range(n)]
        arrive = [pltpu.make_async_remote_copy(
            src_ref=ins[a], dst_ref=outs[a].at[1 - c], send_sem=send_sems.at[a], recv_sem=recv_sems.at[a],
            device_id=(x, y, 1 - c), device_id_type=MESH) for a in range(n)]
        return remote, local, arrive

    def start(ins, outs, sems):
        remote, local, _ = copies(ins, outs, sems)
        for cp in remote + local:
            cp.start()

    def finish(ins, outs, sems):
        remote, local, arrive = copies(ins, outs, sems)
        for cp in arrive:
            cp.wait_recv()
        for cp in remote:
            cp.wait_send()
        for cp in local:
            cp.wait()

    return _Job(arrs, [jax.ShapeDtypeStruct((2,) + a.shape, a.dtype) for a in arrs],
                [pltpu.SemaphoreType.DMA((n,)), pltpu.SemaphoreType.DMA((n,)), pltpu.SemaphoreType.DMA((n,))],
                start, finish)


def _run_job(job, name):
    ni, no = len(job.ins), len(job.outs)

    def body(*refs):
        ins, outs, sems = refs[:ni], refs[ni:ni + no], refs[ni + no:]
        job.start(ins, outs, sems)
        for fwd in job.forwards:
            fwd(ins, outs, sems)
        job.finish(ins, outs, sems)

    any_spec = pl.BlockSpec(memory_space=pl.ANY)
    return pl.pallas_call(
        body, name=name, out_shape=job.outs, in_specs=[any_spec] * ni, out_specs=[any_spec] * no,
        scratch_shapes=job.sems,
    )(*job.ins)


def _call(body, *, name, grid, in_specs, out_specs, out_shape, scratch_shapes=(), sem, args, jobs=(),
          prefetch=(), aliases=None):
    in_specs, out_specs, out_shape, scratch_shapes = list(in_specs), list(out_specs), list(out_shape), list(scratch_shapes)
    prefetch = list(prefetch)
    if not jobs and not prefetch and not aliases:
        res = pl.pallas_call(body, name=name, grid=grid, in_specs=in_specs, out_specs=out_specs, out_shape=out_shape,
                             scratch_shapes=scratch_shapes, compiler_params=_params(sem))(*args)
        return list(res), []
    n_in, n_out, n_scr = len(in_specs), len(out_specs), len(scratch_shapes)
    j_in = [len(j.ins) for j in jobs]
    j_out = [len(j.outs) for j in jobs]
    j_sem = [len(j.sems) for j in jobs]
    total = int(np.prod(grid))
    any_spec = pl.BlockSpec(memory_space=pl.ANY)

    def wrapped(*refs):
        pos = [0]

        def take(k):
            part = refs[pos[0]:pos[0] + k]
            pos[0] += k
            return part

        take(len(prefetch))
        ins = take(n_in)
        jins = [take(k) for k in j_in]
        outs = take(n_out)
        jouts = [take(k) for k in j_out]
        scr = take(n_scr)
        jsems = [take(k) for k in j_sem]
        step = pl.program_id(0)
        for d in range(1, len(grid)):
            step = step * grid[d] + pl.program_id(d)
        for j, job in enumerate(jobs):
            pl.when(step == 0)(functools.partial(job.start, jins[j], jouts[j], jsems[j]))
            for fwd, frac in zip(job.forwards, job.forward_at):
                at = min(int(total * frac), total - 1)
                pl.when(step == at)(functools.partial(fwd, jins[j], jouts[j], jsems[j]))
        body(*ins, *outs, *scr)
        for j, job in enumerate(jobs):
            pl.when(step == total - 1)(functools.partial(job.finish, jins[j], jouts[j], jsems[j]))

    grid_spec = pltpu.PrefetchScalarGridSpec(
        num_scalar_prefetch=len(prefetch), grid=grid,
        in_specs=in_specs + [any_spec] * sum(j_in),
        out_specs=out_specs + [any_spec] * sum(j_out),
        scratch_shapes=scratch_shapes + [s for j in jobs for s in j.sems])
    res = pl.pallas_call(
        wrapped, name=name, grid_spec=grid_spec,
        out_shape=out_shape + [o for j in jobs for o in j.outs],
        input_output_aliases={len(prefetch) + i: o for i, o in (aliases or {}).items()},
        compiler_params=_params(("arbitrary",) * len(grid)),
    )(*prefetch, *args, *[a for j in jobs for a in j.ins])
    res = list(res)
    own, rest = res[:n_out], res[n_out:]
    per_job = []
    for k in j_out:
        per_job.append(rest[:k])
        rest = rest[k:]
    return own, per_job


def _exchange_sibling(arrs, name):
    return _run_job(_sibling_job(arrs), name)


def _sibling_job(arrs):
    n = len(arrs)

    def copies(ins, outs, sems):
        send_sems, recv_sems = sems
        x, y, c = _my_place()
        return [pltpu.make_async_remote_copy(
            src_ref=ins[a].at[2 * k + 1 - c], dst_ref=outs[a].at[k],
            send_sem=send_sems.at[a, k], recv_sem=recv_sems.at[a, k],
            device_id=(x, y, 1 - c), device_id_type=MESH) for a in range(n) for k in range(4)]

    def start(ins, outs, sems):
        for cp in copies(ins, outs, sems):
            cp.start()

    def finish(ins, outs, sems):
        for cp in copies(ins, outs, sems):
            cp.wait()

    return _Job(arrs, [jax.ShapeDtypeStruct((4,) + a.shape[1:], a.dtype) for a in arrs],
                [pltpu.SemaphoreType.DMA((n, 4)), pltpu.SemaphoreType.DMA((n, 4))], start, finish)


def _chips_job(arrs):
    n = len(arrs)

    def copies(ins, outs, sems):
        send_sems, recv_sems = sems
        x, y, c = _my_place()
        chips = [(1 - x, y), (x, 1 - y), (1 - x, 1 - y)]
        return [pltpu.make_async_remote_copy(
            src_ref=ins[a].at[2 * px + py], dst_ref=outs[a].at[j],
            send_sem=send_sems.at[a, j], recv_sem=recv_sems.at[a, j],
            device_id=(px, py, c), device_id_type=MESH) for a in range(n) for j, (px, py) in enumerate(chips)]

    def start(ins, outs, sems):
        for cp in copies(ins, outs, sems):
            cp.start()

    def finish(ins, outs, sems):
        for cp in copies(ins, outs, sems):
            cp.wait()

    return _Job(arrs, [jax.ShapeDtypeStruct((3,) + a.shape[1:], a.dtype) for a in arrs],
                [pltpu.SemaphoreType.DMA((n, 3)), pltpu.SemaphoreType.DMA((n, 3))], start, finish)


def _pair_add(stack, from_sibling, c_idx, name):
    _, r, cc = stack.shape
    tr = _tile(r, 512)

    def body(c_ref, a_ref, b_ref, o_ref):
        o_ref[...] = (a_ref[...].astype(F32) + b_ref[...].astype(F32)).astype(o_ref.dtype)

    grid_spec = pltpu.PrefetchScalarGridSpec(
        num_scalar_prefetch=1, grid=(4, r // tr),
        in_specs=[pl.BlockSpec((None, tr, cc), lambda k, i, c_ref: (2 * k + c_ref[0], i, 0)),
                  pl.BlockSpec((None, tr, cc), lambda k, i, c_ref: (k, i, 0))],
        out_specs=pl.BlockSpec((None, tr, cc), lambda k, i, c_ref: (k, i, 0)))
    return pl.pallas_call(
        body, name=name, grid_spec=grid_spec,
        out_shape=jax.ShapeDtypeStruct((4, r, cc), BF16),
        compiler_params=_params(("parallel", "parallel")),
    )(c_idx, stack, from_sibling)


_DIMS = {"nn": ((1,), (0,)), "nt": ((1,), (1,)), "tn": ((0,), (0,))}


def _mm(a, b, mode, *, tm, tn, tk=None, outs, epi=None, tiles=(), rows=(), name, jobs=(), chunks=None, into=None,
        n_total=None):
    if mode == "tn":
        kk, m = a.shape
    else:
        m, kk = a.shape
    n_all = b.shape[0] if mode == "nt" else b.shape[1]
    nn = n_all if chunks is None else chunks[0].shape[0] * chunks[1]
    n_all = n_all if n_total is None else n_total
    tk = kk if tk is None else tk
    assert m % tm == 0 and nn % tn == 0 and kk % tk == 0, (name, a.shape, b.shape, tm, tn, tk)
    nk = kk // tk
    nt, nr, no = len(tiles), len(rows), len(outs)
    n_into = 0 if into is None else 1

    def body(*refs):
        a_ref, b_ref = refs[:2]
        tile_refs = refs[2:2 + nt]
        row_refs = refs[2 + nt:2 + nt + nr]
        out_refs = refs[2 + nt + nr + n_into:2 + nt + nr + n_into + no]
        part = lax.dot_general(a_ref[...], b_ref[...], (_DIMS[mode], ((), ())), preferred_element_type=F32)

        def finish(acc):
            if epi is None:
                res = (acc,)
            else:
                res = epi(acc, *[t[...] for t in tile_refs], *[r[...] for r in row_refs])
            for o_ref, val in zip(out_refs, res):
                o_ref[...] = val.astype(o_ref.dtype)

        if nk == 1:
            finish(part)
        else:
            acc_ref = refs[-1]
            k = pl.program_id(2)

            @pl.when(k == 0)
            def _():
                acc_ref[...] = part

            @pl.when(k > 0)
            def _():
                acc_ref[...] += part

            @pl.when(k == nk - 1)
            def _():
                finish(acc_ref[...])

    if chunks is None:
        col = b_row = lambda j, *_: j
        prefetch = []
    else:
        assert mode == "nt" and not tiles and not rows and chunks[1] % tn == 0
        per = chunks[1] // tn
        col = lambda j, ids, b_ids: ids[j // per] * per + j % per
        b_row = lambda j, ids, b_ids: b_ids[j // per] * per + j % per
        prefetch = [chunks[0], chunks[2]]
    if mode == "tn":
        a_spec = pl.BlockSpec((tk, tm), lambda i, j, k, *_: (k, i))
    else:
        a_spec = pl.BlockSpec((tm, tk), lambda i, j, k, *_: (i, k))
    if mode == "nt":
        b_spec = pl.BlockSpec((tn, tk), lambda i, j, k, *s: (b_row(j, *s), k))
    else:
        b_spec = pl.BlockSpec((tk, tn), lambda i, j, k, *_: (k, j))
    tile_specs = [pl.BlockSpec((tm, tn), functools.partial(lambda i, j, k, off: (i, j + off), off=off))
                  for _, off in tiles]
    row_specs = [pl.BlockSpec((r.shape[0], tn), lambda i, j, k: (0, j)) for r in rows]
    out_spec = pl.BlockSpec((tm, tn), lambda i, j, k, *s: (i, col(j, *s)))
    args = [a, b, *[t for t, _ in tiles], *rows]
    in_specs = [a_spec, b_spec] + tile_specs + row_specs
    aliases = None
    if into is not None:
        aliases = {len(args): 0}
        args.append(into)
        in_specs.append(pl.BlockSpec(memory_space=pl.ANY))
    res, jres = _call(
        body, name=name, grid=(m // tm, nn // tn, nk),
        in_specs=in_specs, out_specs=[out_spec] * no,
        out_shape=[jax.ShapeDtypeStruct((m, n_all), dt) for dt in outs],
        scratch_shapes=[pltpu.VMEM((tm, tn), F32)] if nk > 1 else [],
        sem=("parallel", "parallel", "arbitrary"),
        args=args, jobs=jobs, prefetch=prefetch, aliases=aliases)
    return (res, jres) if jobs else res


def _ada_fwd(c_all, w_ada, b_ada_cols, name):
    nb, dd = c_all.shape
    ncol = w_ada.shape[1]
    tn = _tile(ncol, 512, LANE)

    def body(c_ref, w_ref, b_ref, o_ref):
        cv = c_ref[...]
        act = (cv * _sigmoid(cv)).astype(BF16)
        o_ref[...] = jnp.dot(act, w_ref[...].astype(BF16), preferred_element_type=F32) + b_ref[...]

    return pl.pallas_call(
        body, name=name, grid=(ncol // tn,),
        in_specs=[pl.BlockSpec((nb, dd), lambda j: (0, 0)), pl.BlockSpec((dd, tn), lambda j: (0, j)),
                  pl.BlockSpec((1, tn), lambda j: (0, j))],
        out_specs=pl.BlockSpec((nb, tn), lambda j: (0, j)),
        out_shape=jax.ShapeDtypeStruct((nb, ncol), F32),
        compiler_params=_params(("parallel",)),
    )(c_all, w_ada, b_ada_cols)


def _ada_bwd(c_all_t, dmod_cols, name):
    dd, nb = c_all_t.shape
    ncol = dmod_cols.shape[1]
    tr = _tile(dd, 256)

    def body(c_ref, d_ref, o_ref):
        cv = c_ref[...]
        act = (cv * _sigmoid(cv)).astype(BF16).astype(F32)
        dm = d_ref[...].astype(BF16).astype(F32)
        acc = act[:, 0:1] * dm[0:1, :]
        for bi in range(1, nb):
            acc = acc + act[:, bi:bi + 1] * dm[bi:bi + 1, :]
        o_ref[...] = acc

    return pl.pallas_call(
        body, name=name, grid=(dd // tr,),
        in_specs=[pl.BlockSpec((tr, nb), lambda i: (i, 0)), pl.BlockSpec((nb, ncol), lambda i: (0, 0))],
        out_specs=pl.BlockSpec((tr, ncol), lambda i: (i, 0)),
        out_shape=jax.ShapeDtypeStruct((dd, ncol), F32),
        compiler_params=_params(("parallel",)),
    )(c_all_t, dmod_cols)


def _norm_mod(x, gain, mod, row0, name):
    t, dd = x.shape
    tt = _tile(t, 512)

    def body(x_ref, g_ref, m_ref, h_ref):
        xv = x_ref[...]
        r = lax.rsqrt(jnp.mean(xv * xv, axis=-1, keepdims=True) + EPS)
        shift, scale = m_ref[row0:row0 + 1, :], m_ref[row0 + 1:row0 + 2, :]
        h_ref[...] = ((xv * r * g_ref[...]) * (1.0 + scale) + shift).astype(BF16)

    return pl.pallas_call(
        body, name=name, grid=(t // tt,),
        in_specs=[pl.BlockSpec((tt, dd), lambda i: (i, 0)), pl.BlockSpec((1, dd), lambda i: (0, 0)),
                  pl.BlockSpec((8, dd), lambda i: (0, 0))],
        out_specs=pl.BlockSpec((tt, dd), lambda i: (i, 0)),
        out_shape=jax.ShapeDtypeStruct((t, dd), BF16),
        compiler_params=_params(("parallel",)),
    )(x, gain, mod)


def _norm_mod_bwd(x, gain, mod, row0, dh, dres, branch, name, jobs=()):
    t, dd = x.shape
    tt = _tile(t, 256)
    has_branch = branch is not None

    def body(*refs):
        if has_branch:
            x_ref, g_ref, m_ref, dh_ref, dres_ref, o_ref, dx_ref, st_ref, do_ref = refs
        else:
            x_ref, g_ref, m_ref, dh_ref, dres_ref, dx_ref, st_ref = refs

        @pl.when(pl.program_id(0) == 0)
        def _():
            st_ref[...] = jnp.zeros_like(st_ref)

        xv = x_ref[...]
        g = g_ref[...]
        scale = m_ref[row0 + 1:row0 + 2, :]
        r = lax.rsqrt(jnp.mean(xv * xv, axis=-1, keepdims=True) + EPS)
        nv = xv * r
        dhv = dh_ref[...].astype(F32)
        st_ref[0:1, :] += jnp.sum(dhv, axis=0, keepdims=True)
        st_ref[1:2, :] += jnp.sum(dhv * (nv * g), axis=0, keepdims=True)
        dng = dhv * (1.0 + scale)
        st_ref[2:3, :] += jnp.sum(dng * nv, axis=0, keepdims=True)
        dn = dng * g
        dx = dres_ref[...] + r * (dn - nv * jnp.mean(dn * nv, axis=-1, keepdims=True))
        dx_ref[...] = dx
        if has_branch:
            ov = o_ref[...].astype(F32)
            gate = m_ref[branch[1]:branch[1] + 1, :]
            st_ref[3:4, :] += jnp.sum(dx * ov, axis=0, keepdims=True)
            do_ref[...] = (dx * gate).astype(BF16)

    row = pl.BlockSpec((tt, dd), lambda i: (i, 0))
    in_specs = [row, pl.BlockSpec((1, dd), lambda i: (0, 0)), pl.BlockSpec((8, dd), lambda i: (0, 0)), row, row]
    args = [x, gain, mod, dh, dres]
    out_specs = [row, pl.BlockSpec((8, dd), lambda i: (0, 0))]
    out_shape = [jax.ShapeDtypeStruct((t, dd), F32), jax.ShapeDtypeStruct((8, dd), F32)]
    if has_branch:
        in_specs.append(row)
        args.append(branch[0])
        out_specs.append(row)
        out_shape.append(jax.ShapeDtypeStruct((t, dd), BF16))
    res, jres = _call(body, name=name, grid=(t // tt,), in_specs=in_specs, out_specs=out_specs, out_shape=out_shape,
                      sem=("arbitrary",), args=args, jobs=jobs)
    return (res, jres) if jobs else res


def _final_loss(x2, o2, final_g, target, mod, name):
    t, dd = x2.shape
    tt = _tile(t, 256)
    nsteps = t // tt

    def body(x_ref, o_ref, g_ref, t_ref, m_ref, loss_ref, dx_ref, do_ref, st_ref, lacc):
        i = pl.program_id(0)

        @pl.when(i == 0)
        def _():
            st_ref[...] = jnp.zeros_like(st_ref)
            lacc[...] = jnp.zeros_like(lacc)

        xv = x_ref[...]
        g = g_ref[...]
        r = lax.rsqrt(jnp.mean(xv * xv, axis=-1, keepdims=True) + EPS)
        nv = xv * r
        err = nv * g - t_ref[...]
        lacc[...] += jnp.sum(err * err, axis=0, keepdims=True)
        dy = err * (1.0 / dd)
        st_ref[0:1, :] += jnp.sum(dy * nv, axis=0, keepdims=True)
        dn = dy * g
        dx = r * (dn - nv * jnp.mean(dn * nv, axis=-1, keepdims=True))
        dx_ref[...] = dx
        st_ref[1:2, :] += jnp.sum(dx * o_ref[...].astype(F32), axis=0, keepdims=True)
        do_ref[...] = (dx * m_ref[5:6, :]).astype(BF16)

        @pl.when(i == nsteps - 1)
        def _():
            total = jnp.sum(lacc[...], axis=-1, keepdims=True) * (0.5 / dd)
            loss_ref[...] = jnp.broadcast_to(total, loss_ref.shape)

    row = pl.BlockSpec((tt, dd), lambda i: (i, 0))
    return pl.pallas_call(
        body, name=name, grid=(nsteps,),
        in_specs=[row, row, pl.BlockSpec((1, dd), lambda i: (0, 0)), row, pl.BlockSpec((8, dd), lambda i: (0, 0))],
        out_specs=[pl.BlockSpec((8, LANE), lambda i: (0, 0)), row, row, pl.BlockSpec((8, dd), lambda i: (0, 0))],
        out_shape=[jax.ShapeDtypeStruct((8, LANE), F32), jax.ShapeDtypeStruct((t, dd), F32),
                   jax.ShapeDtypeStruct((t, dd), BF16), jax.ShapeDtypeStruct((8, dd), F32)],
        scratch_shapes=[pltpu.VMEM((1, dd), F32)],
        compiler_params=_params(("arbitrary",)),
    )(x2, o2, final_g, target, mod)


def _lru_gates(xt, prev8, lp, wa_ref, wx_ref, first_tile, cw):
    tt = xt.shape[0]
    ext = jnp.concatenate([prev8, xt], axis=0)
    xs = [pltpu.roll(ext, s, 0)[8:, :] if s else xt for s in range(CONV_WIDTH)]
    xc = lp[0:1, :] + lp[7:8, :] * xs[0] + lp[6:7, :] * xs[1] + lp[5:6, :] * xs[2] + lp[4:5, :] * xs[3]
    xcb = xc.astype(BF16)
    za, zi = [], []
    for jb in range(cw // LRU_BLOCK_W):
        sl = slice(jb * LRU_BLOCK_W, (jb + 1) * LRU_BLOCK_W)
        za.append(jnp.dot(xcb[:, sl], wa_ref[jb].astype(BF16), preferred_element_type=F32))
        zi.append(jnp.dot(xcb[:, sl], wx_ref[jb].astype(BF16), preferred_element_type=F32))
    ra = _sigmoid(jnp.concatenate(za, axis=1) + lp[1:2, :])
    ri = _sigmoid(jnp.concatenate(zi, axis=1) + lp[2:3, :])
    sp = _softplus(-lp[3:4, :])
    log_a = -LRU_C * ra * sp
    av = jnp.exp(log_a)
    seq_start = jnp.logical_and(first_tile, lax.broadcasted_iota(jnp.int32, (tt, 1), 0) == 0)
    mult = jnp.where(seq_start, 1.0, jnp.sqrt(_neg_expm1(2.0 * log_a)))
    return xs, xc, xcb, ra, ri, sp, av, mult, seq_start


def _scan_groups(a, u, reverse):
    tt, cw = a.shape
    a3, u3 = a.reshape(tt // 8, 8, cw), u.reshape(tt // 8, 8, cw)
    r = lax.broadcasted_iota(jnp.int32, (1, 8, 1), 1)
    for s in (1, 2, 4):
        shift = 8 - s if reverse else s
        take = r < 8 - s if reverse else r >= s
        u3 = jnp.where(take, a3 * pltpu.roll(u3, shift, 1) + u3, u3)
        a3 = jnp.where(take, a3 * pltpu.roll(a3, shift, 1), a3)
    return a3, u3


def _scan_carry(a_s, u_s, h_s, carry, reverse):
    ng = a_s.shape[0]

    def step(i, h):
        g = ng - 1 - i if reverse else i
        hg = u_s[g] + a_s[g] * h
        h_s[g] = hg
        return hg[0:1, :] if reverse else hg[7:8, :]

    return lax.fori_loop(0, ng, step, carry, unroll=4)


def _lru_fwd(proj, lp, wa, wx, name, jobs=()):
    t = proj.shape[0]
    w = D_MODEL
    cw = LRU_CW
    tt = _tile(t, 256)
    nb = cw // LRU_BLOCK_W

    def body(lx_ref, lg_ref, lp_ref, wa_ref, wx_ref, rec_ref, ya_ref, a_s, u_s, h_s, halo, carry):
        ti = pl.program_id(1)

        @pl.when(ti == 0)
        def _():
            halo[...] = jnp.zeros_like(halo)
            carry[...] = jnp.zeros_like(carry)

        xt = lx_ref[...].astype(F32)
        lp_v = lp_ref[...]
        _, xc, _, _, ri, _, av, mult, _ = _lru_gates(xt, halo[...], lp_v, wa_ref, wx_ref, ti == 0, cw)
        halo[...] = xt[tt - 8:, :]
        a_s[...], u_s[...] = _scan_groups(av, mult * (ri * xc), False)
        carry[...] = _scan_carry(a_s, u_s, h_s, carry[...], False)
        rec = h_s[...].reshape(tt, cw)
        rec_ref[...] = rec.astype(BF16)
        gelu, _ = _gelu_tanh(lg_ref[...].astype(F32))
        ya_ref[...] = (rec * gelu).astype(BF16)

    off_g = OFF_LG // cw
    res, jres = _call(
        body, name=name, grid=(w // cw, t // tt),
        in_specs=[pl.BlockSpec((tt, cw), lambda ci, ti: (ti, ci)),
                  pl.BlockSpec((tt, cw), lambda ci, ti: (ti, off_g + ci)),
                  pl.BlockSpec((8, cw), lambda ci, ti: (0, ci)),
                  pl.BlockSpec((nb, LRU_BLOCK_W, LRU_BLOCK_W), lambda ci, ti: (ci, 0, 0)),
                  pl.BlockSpec((nb, LRU_BLOCK_W, LRU_BLOCK_W), lambda ci, ti: (ci, 0, 0))],
        out_specs=[pl.BlockSpec((tt, cw), lambda ci, ti: (ti, ci))] * 2,
        out_shape=[jax.ShapeDtypeStruct((t, w), BF16)] * 2,
        scratch_shapes=[pltpu.VMEM((tt // 8, 8, cw), F32)] * 3 + [pltpu.VMEM((8, cw), F32), pltpu.VMEM((1, cw), F32)],
        sem=("parallel", "arbitrary"), args=[proj, proj, lp, wa, wx], jobs=jobs)
    return (res, jres) if jobs else res


def _lru_bwd(proj, rec, dya, lp, wa, wx, name, jobs=()):
    t = proj.shape[0]
    w = D_MODEL
    cw = LRU_CW
    tt = _tile(t, 256)
    nt = t // tt
    nb = cw // LRU_BLOCK_W
    r8 = tt // 8

    def body(lx_ref, lxp_ref, lg_ref, rec_ref, recp_ref, dya_ref, lp_ref, wa_ref, wx_ref,
             dlx_ref, dlg_ref, st_ref, dwa_ref, dwx_ref, a_s, d_s, dh_s, nhalo, carry):
        step_i = pl.program_id(1)
        ti = nt - 1 - step_i

        @pl.when(step_i == 0)
        def _():
            st_ref[...] = jnp.zeros_like(st_ref)
            dwa_ref[...] = jnp.zeros_like(dwa_ref)
            dwx_ref[...] = jnp.zeros_like(dwx_ref)
            nhalo[...] = jnp.zeros_like(nhalo)
            carry[...] = jnp.zeros_like(carry)

        first = ti == 0
        keep = jnp.where(first, 0.0, 1.0)
        xt = lx_ref[...].astype(F32)
        prev8 = lxp_ref[...].astype(F32) * keep
        lp_v = lp_ref[...]
        xs, xc, xcb, ra, ri, sp, av, mult, seq_start = _lru_gates(xt, prev8, lp_v, wa_ref, wx_ref, first, cw)

        lg = lg_ref[...].astype(F32)
        gelu, th = _gelu_tanh(lg)
        dyav = dya_ref[...].astype(F32)
        recv = rec_ref[...].astype(F32)
        dlg_ref[...] = (dyav * recv * _gelu_tanh_grad(lg, th)).astype(BF16)

        drec = dyav * gelu
        e_in = carry[...]
        a_s[...], d_s[...] = _scan_groups(av, av * drec, True)
        carry[...] = _scan_carry(a_s, d_s, dh_s, e_in, True)
        e_next = jnp.concatenate([dh_s[...].reshape(tt, cw), jnp.broadcast_to(e_in, (8, cw))], axis=0)
        dh = drec + pltpu.roll(e_next, tt + 7, 0)[:tt, :]
        rprev8 = recp_ref[...].astype(F32) * keep
        hprev = pltpu.roll(jnp.concatenate([rprev8, recv], axis=0), 1, 0)[8:, :]
        da = dh * hprev
        dmult = jnp.where(seq_start, 0.0, dh * ri * xc)
        dri = dh * mult * xc
        dxc = dh * mult * ri
        dlog_a = da * av - dmult * (av * av) / mult
        dra = dlog_a * (-LRU_C * sp)
        lam = lp_v[3:4, :]
        st_ref[3:4, :] += jnp.sum(dlog_a * ra, axis=0, keepdims=True) * (LRU_C * _sigmoid(-lam))
        dza = dra * ra * (1.0 - ra)
        dzi = dri * ri * (1.0 - ri)
        st_ref[1:2, :] += jnp.sum(dza, axis=0, keepdims=True)
        st_ref[2:3, :] += jnp.sum(dzi, axis=0, keepdims=True)
        dzab, dzib = dza.astype(BF16), dzi.astype(BF16)
        back = []
        for jb in range(nb):
            sl = slice(jb * LRU_BLOCK_W, (jb + 1) * LRU_BLOCK_W)
            dwa_ref[jb] += lax.dot_general(xcb[:, sl], dzab[:, sl], (_DIMS["tn"], ((), ())), preferred_element_type=F32)
            dwx_ref[jb] += lax.dot_general(xcb[:, sl], dzib[:, sl], (_DIMS["tn"], ((), ())), preferred_element_type=F32)
            back.append(
                lax.dot_general(dzab[:, sl], wa_ref[jb].astype(BF16), (_DIMS["nt"], ((), ())), preferred_element_type=F32)
                + lax.dot_general(dzib[:, sl], wx_ref[jb].astype(BF16), (_DIMS["nt"], ((), ())), preferred_element_type=F32))
        dxc = dxc + jnp.concatenate(back, axis=1)
        st_ref[0:1, :] += jnp.sum(dxc, axis=0, keepdims=True)
        for k in range(CONV_WIDTH):
            st_ref[4 + k:5 + k, :] += jnp.sum(dxc * xs[CONV_WIDTH - 1 - k], axis=0, keepdims=True)
        ext = jnp.concatenate([dxc, nhalo[...]], axis=0)
        dlx = lp_v[7:8, :] * dxc
        for s in range(1, CONV_WIDTH):
            dlx = dlx + lp_v[7 - s:8 - s, :] * pltpu.roll(ext, tt + 8 - s, 0)[:tt, :]
        dlx_ref[...] = dlx.astype(BF16)
        nhalo[...] = dxc[0:8, :]

    off_g = OFF_LG // cw
    tile = lambda off: pl.BlockSpec((tt, cw), lambda ci, si: (nt - 1 - si, off + ci))
    prev = lambda off: pl.BlockSpec((8, cw), lambda ci, si: (jnp.maximum((nt - 1 - si) * r8 - 1, 0), off + ci))
    wspec = pl.BlockSpec((nb, LRU_BLOCK_W, LRU_BLOCK_W), lambda ci, si: (ci, 0, 0))
    st_spec = pl.BlockSpec((8, cw), lambda ci, si: (0, ci))
    res, jres = _call(
        body, name=name, grid=(w // cw, nt),
        in_specs=[tile(0), prev(0), tile(off_g), tile(0), prev(0), tile(0), st_spec, wspec, wspec],
        out_specs=[tile(0), tile(0), st_spec, wspec, wspec],
        out_shape=[jax.ShapeDtypeStruct((t, w), BF16), jax.ShapeDtypeStruct((t, w), BF16),
                   jax.ShapeDtypeStruct((8, w), F32),
                   jax.ShapeDtypeStruct((LRU_BLOCKS, LRU_BLOCK_W, LRU_BLOCK_W), F32),
                   jax.ShapeDtypeStruct((LRU_BLOCKS, LRU_BLOCK_W, LRU_BLOCK_W), F32)],
        scratch_shapes=[pltpu.VMEM((tt // 8, 8, cw), F32)] * 3 + [pltpu.VMEM((8, cw), F32), pltpu.VMEM((1, cw), F32)],
        sem=("parallel", "arbitrary"), args=[proj, proj, proj, rec, rec, dya, lp, wa, wx], jobs=jobs)
    return (res, jres) if jobs else res


def _band_valid(first_block):
    qi = lax.broadcasted_iota(jnp.int32, (BLOCK, 2 * BLOCK), 0)
    ki = lax.broadcasted_iota(jnp.int32, (BLOCK, 2 * BLOCK), 1)
    rel = qi + BLOCK - ki
    valid = jnp.logical_and(rel >= 0, rel < BLOCK)
    return jnp.logical_and(valid, jnp.logical_or(ki >= BLOCK, jnp.logical_not(first_block)))


ATTN_STACK_BWD = 8


def _low_half():
    return lax.broadcasted_iota(jnp.int32, (1, LANE), 1) < HEAD_DIM


def _stack_heads(x, h0, ns):
    low = _low_half()
    parts = []
    for g in range(ns):
        h = h0 + g
        slab = x[:, (h // 2) * LANE:(h // 2 + 1) * LANE]
        parts.append(jnp.where(low if h % 2 == 0 else jnp.logical_not(low), slab, jnp.zeros_like(slab)))
    return jnp.concatenate(parts, axis=0)


def _unstack_heads(y, ns):
    low = _low_half()
    pairs = [jnp.where(low, y[(2 * j) * BLOCK:(2 * j + 1) * BLOCK], y[(2 * j + 1) * BLOCK:(2 * j + 2) * BLOCK])
             for j in range(ns // 2)]
    return pairs[0] if len(pairs) == 1 else jnp.concatenate(pairs, axis=1)


def _dup_kv(xall, kv):
    part = xall[:, kv * HEAD_DIM:(kv + 1) * HEAD_DIM]
    return jnp.concatenate([part, part], axis=1)


def _fold_halves(a):
    return a + pltpu.roll(a, HEAD_DIM, 1)


def _group_probs(qg, k2, bias_g, sink_g, valid):
    s = lax.dot_general(qg, k2, (_DIMS["nt"], ((), ())), preferred_element_type=F32) * (HEAD_DIM ** -0.5)
    s = jnp.where(valid[None], s.reshape(bias_g.shape) + bias_g, NEG_INF)
    m = jnp.maximum(jnp.max(s, axis=-1, keepdims=True), sink_g)
    e = jnp.exp(s - m)
    es = jnp.exp(sink_g - m)
    inv = 1.0 / (jnp.sum(e, axis=-1, keepdims=True) + es)
    return e * inv, es * inv


def _group_sinks(s_ref, h0, ns):
    return jnp.concatenate([jnp.full((1, BLOCK, 1), s_ref[h0 + g], F32) for g in range(ns)], axis=0)


def _attn_specs(t):
    nblk = t // BLOCK
    q_spec = pl.BlockSpec((BLOCK, D_MODEL), lambda n: (n, OFF_Q // D_MODEL))
    kc = pl.BlockSpec((BLOCK, KV_WIDTH), lambda n: (n, OFF_K // KV_WIDTH))
    kp = pl.BlockSpec((BLOCK, KV_WIDTH), lambda n: (jnp.maximum(n - 1, 0), OFF_K // KV_WIDTH))
    vc = pl.BlockSpec((BLOCK, KV_WIDTH), lambda n: (n, OFF_V // KV_WIDTH))
    vp = pl.BlockSpec((BLOCK, KV_WIDTH), lambda n: (jnp.maximum(n - 1, 0), OFF_V // KV_WIDTH))
    bias_spec = pl.BlockSpec((N_Q_HEADS, BLOCK, 2 * BLOCK), lambda n: (0, 0, 0))
    sink_spec = pl.BlockSpec(memory_space=pltpu.SMEM)
    return nblk, [q_spec, kp, kc, vp, vc, bias_spec, sink_spec]


def _attn_fwd(proj, bias, sinks, name, jobs=()):
    t = proj.shape[0]
    nblk, in_specs = _attn_specs(t)

    def body(q_ref, kp_ref, kc_ref, vp_ref, vc_ref, b_ref, s_ref, o_ref):
        valid = _band_valid(pl.program_id(0) == 0)
        q = q_ref[...]
        kall = jnp.concatenate([kp_ref[...], kc_ref[...]], axis=0)
        vall = jnp.concatenate([vp_ref[...], vc_ref[...]], axis=0)
        outs = []
        for h in range(N_Q_HEADS):
            kv = h // GROUP
            kk = kall[:, kv * HEAD_DIM:(kv + 1) * HEAD_DIM]
            vv = vall[:, kv * HEAD_DIM:(kv + 1) * HEAD_DIM]
            s = lax.dot_general(q[:, h * HEAD_DIM:(h + 1) * HEAD_DIM], kk, (_DIMS["nt"], ((), ())),
                                preferred_element_type=F32) * (HEAD_DIM ** -0.5)
            s = jnp.where(valid, s + b_ref[h], NEG_INF)
            sink = s_ref[h]
            m = jnp.maximum(jnp.max(s, axis=-1, keepdims=True), sink)
            e = jnp.exp(s - m)
            p = e * (1.0 / (jnp.sum(e, axis=-1, keepdims=True) + jnp.exp(sink - m)))
            outs.append(jnp.dot(p.astype(BF16), vv, preferred_element_type=F32))
        o_ref[...] = jnp.concatenate(outs, axis=1).astype(BF16)

    res, jres = _call(
        body, name=name, grid=(nblk,), in_specs=in_specs,
        out_specs=[pl.BlockSpec((BLOCK, D_MODEL), lambda n: (n, 0))],
        out_shape=[jax.ShapeDtypeStruct((t, D_MODEL), BF16)],
        sem=("parallel",), args=[proj, proj, proj, proj, proj, bias, sinks], jobs=jobs)
    return (res[0], jres) if jobs else res[0]


def _attn_bwd(proj, datt, bias, sinks, name, jobs=()):
    t = proj.shape[0]
    nblk, in_specs = _attn_specs(t)
    in_specs = in_specs + [pl.BlockSpec((BLOCK, D_MODEL), lambda n: (n, 0))]
    scale = HEAD_DIM ** -0.5

    def body(q_ref, kp_ref, kc_ref, vp_ref, vc_ref, b_ref, s_ref, do_ref, dq_ref, dk_ref, dv_ref, db_ref, ds_ref):
        n = pl.program_id(0)

        @pl.when(n == 0)
        def _():
            dk_ref[...] = jnp.zeros_like(dk_ref)
            dv_ref[...] = jnp.zeros_like(dv_ref)
            db_ref[...] = jnp.zeros_like(db_ref)
            ds_ref[...] = jnp.zeros_like(ds_ref)

        valid = _band_valid(n == 0)
        lane = lax.broadcasted_iota(jnp.int32, (1, LANE), 1)
        q = q_ref[...]
        do = do_ref[...]
        kall = jnp.concatenate([kp_ref[...], kc_ref[...]], axis=0)
        vall = jnp.concatenate([vp_ref[...], vc_ref[...]], axis=0)
        low = _low_half()
        dqs, dks, dvs = [], [], []
        dsink_row = jnp.zeros((1, LANE), F32)
        ns = ATTN_STACK_BWD
        for kv in range(N_KV_HEADS):
            k2, v2 = _dup_kv(kall, kv), _dup_kv(vall, kv)
            dk_acc = jnp.zeros((2 * BLOCK, LANE), F32)
            dv_acc = jnp.zeros((2 * BLOCK, LANE), F32)
            for h0 in range(kv * GROUP, (kv + 1) * GROUP, ns):
                qg, dog = _stack_heads(q, h0, ns), _stack_heads(do, h0, ns)
                p, psink = _group_probs(qg, k2, b_ref[h0:h0 + ns], _group_sinks(s_ref, h0, ns), valid)
                dp = lax.dot_general(dog, v2, (_DIMS["nt"], ((), ())), preferred_element_type=F32)
                dp = dp.reshape(ns, BLOCK, 2 * BLOCK)
                delta = jnp.sum(p * dp, axis=-1, keepdims=True)
                ds = p * (dp - delta)
                db_ref[h0:h0 + ns] += ds
                dsink = -jnp.sum(psink * delta, axis=1, keepdims=True)
                for g in range(ns):
                    dsink_row = dsink_row + jnp.where(lane == h0 + g, dsink[g], 0.0)
                dsb = ds.reshape(ns * BLOCK, 2 * BLOCK).astype(BF16)
                pb = p.reshape(ns * BLOCK, 2 * BLOCK).astype(BF16)
                dqs.append(_unstack_heads(jnp.dot(dsb, k2, preferred_element_type=F32) * scale, ns))
                dk_acc = dk_acc + lax.dot_general(dsb, qg, (_DIMS["tn"], ((), ())), preferred_element_type=F32)
                dv_acc = dv_acc + lax.dot_general(pb, dog, (_DIMS["tn"], ((), ())), preferred_element_type=F32)
            dks.append(_fold_halves(dk_acc) * scale)
            dvs.append(_fold_halves(dv_acc))
        dq_ref[...] = jnp.concatenate(dqs, axis=1).astype(BF16)
        rows = pl.ds(pl.multiple_of(n * BLOCK, BLOCK), 2 * BLOCK)
        dk_ref[rows, :] += jnp.concatenate([jnp.where(low, dks[0], dks[1]), jnp.where(low, dks[2], dks[3])], axis=1)
        dv_ref[rows, :] += jnp.concatenate([jnp.where(low, dvs[0], dvs[1]), jnp.where(low, dvs[2], dvs[3])], axis=1)
        ds_ref[0:1, :] += dsink_row

    res, jres = _call(
        body, name=name, grid=(nblk,), in_specs=in_specs,
        out_specs=[pl.BlockSpec((BLOCK, D_MODEL), lambda n: (n, 0)),
                   pl.BlockSpec((t + BLOCK, KV_WIDTH), lambda n: (0, 0)),
                   pl.BlockSpec((t + BLOCK, KV_WIDTH), lambda n: (0, 0)),
                   pl.BlockSpec((N_Q_HEADS, BLOCK, 2 * BLOCK), lambda n: (0, 0, 0)),
                   pl.BlockSpec((8, LANE), lambda n: (0, 0))],
        out_shape=[jax.ShapeDtypeStruct((t, D_MODEL), BF16),
                   jax.ShapeDtypeStruct((t + BLOCK, KV_WIDTH), F32),
                   jax.ShapeDtypeStruct((t + BLOCK, KV_WIDTH), F32),
                   jax.ShapeDtypeStruct((N_Q_HEADS, BLOCK, 2 * BLOCK), F32),
                   jax.ShapeDtypeStruct((8, LANE), F32)],
        sem=("arbitrary",), args=[proj, proj, proj, proj, proj, bias, sinks, datt], jobs=jobs)
    return (res, jres) if jobs else res


def _bucket_table():
    qi = np.arange(BLOCK)[:, None]
    ki = np.arange(2 * BLOCK)[None, :]
    rel = np.maximum(qi + BLOCK - ki, 0)
    max_exact = N_BUCKETS // 2
    relf = np.maximum(rel, 1).astype(np.float32)
    large = max_exact + (np.log(relf / max_exact) / math.log(MAX_DISTANCE / max_exact)
                         * (N_BUCKETS - max_exact)).astype(np.int32)
    large = np.minimum(large, N_BUCKETS - 1)
    return np.where(rel < max_exact, rel, large).astype(np.int32)


def _rel_bias_fwd(rel_bias, onehot, name):
    n = onehot.shape[1]
    tn = _tile(n, 8192, LANE)

    def body(r_ref, o_ref, out_ref):
        out_ref[...] = lax.dot_general(r_ref[...], o_ref[...], (_DIMS["tn"], ((), ())),
                                       precision=lax.Precision.HIGHEST, preferred_element_type=F32)

    return pl.pallas_call(
        body, name=name, grid=(n // tn,),
        in_specs=[pl.BlockSpec(rel_bias.shape, lambda j: (0, 0)), pl.BlockSpec((N_BUCKETS, tn), lambda j: (0, j))],
        out_specs=pl.BlockSpec((N_Q_HEADS, tn), lambda j: (0, j)),
        out_shape=jax.ShapeDtypeStruct((N_Q_HEADS, n), F32),
        compiler_params=_params(("parallel",)),
    )(rel_bias, onehot)


def _rel_bias_bwd(dbias, onehot, name):
    def body(d_ref, o_ref, out_ref):
        out_ref[...] = lax.dot_general(d_ref[...], o_ref[...], (_DIMS["nt"], ((), ())),
                                       precision=lax.Precision.HIGHEST, preferred_element_type=F32)

    full = pl.BlockSpec(dbias.shape, lambda: (0, 0))
    return pl.pallas_call(
        body, name=name, in_specs=[full, full],
        out_specs=pl.BlockSpec((N_Q_HEADS, N_BUCKETS), lambda: (0, 0)),
        out_shape=jax.ShapeDtypeStruct((N_Q_HEADS, N_BUCKETS), F32),
        compiler_params=_params(()),
    )(dbias, onehot)


def _part_specs(gparts, part_index, tr, cc):
    blk = pl.BlockSpec((tr, cc), lambda i, *_: (i, 0))
    specs = []
    for i, _ in enumerate(gparts):
        if i in part_index:
            ix = part_index[i]
            if isinstance(ix, int):
                specs.append(pl.BlockSpec((None, tr, cc), functools.partial(lambda i, *_, ix: (ix, i, 0), ix=ix)))
            else:
                specs.append(pl.BlockSpec((None, tr, cc), lambda i, ix_ref: (ix_ref[0], i, 0)))
        else:
            specs.append(blk)
    return blk, specs


def _sum_parts(gparts, part_index, name):
    r, cc = gparts[0].shape[-2:]
    tr = _tile(r, 256)
    dyn = [i for i, ix in part_index.items() if not isinstance(ix, int)]
    assert len(dyn) <= 1

    def body(*refs):
        refs = refs[len(dyn):]
        g = refs[0][...].astype(F32)
        for gr in refs[1:-1]:
            g = g + gr[...].astype(F32)
        refs[-1][...] = g

    blk, specs = _part_specs(gparts, part_index, tr, cc)
    grid_spec = pltpu.PrefetchScalarGridSpec(num_scalar_prefetch=len(dyn), grid=(r // tr,), in_specs=specs, out_specs=blk)
    return pl.pallas_call(
        body, name=name, grid_spec=grid_spec, out_shape=jax.ShapeDtypeStruct((r, cc), F32),
        compiler_params=_params(("parallel",)),
    )(*[part_index[i] for i in dyn], *gparts)


def _adamw(w, m, v, gparts, name, part_index=None):
    r, cc = w.shape
    tr = _tile(r, 128)
    np_ = len(gparts)
    part_index = part_index or {}
    dyn = [i for i, ix in part_index.items() if not isinstance(ix, int)]
    assert len(dyn) <= 1
    bc1 = 1.0 / (1.0 - ADAM_B1 ** ADAM_STEP)
    bc2 = 1.0 / (1.0 - ADAM_B2 ** ADAM_STEP)

    def body(*refs):
        refs = refs[len(dyn):]
        w_ref, m_ref, v_ref = refs[:3]
        g_refs = refs[3:3 + np_]
        g_out, d_out, m_out, v_out = refs[3 + np_:]
        g = g_refs[0][...].astype(F32)
        for gr in g_refs[1:]:
            g = g + gr[...].astype(F32)
        mn = ADAM_B1 * m_ref[...] + (1.0 - ADAM_B1) * g
        vn = ADAM_B2 * v_ref[...] + (1.0 - ADAM_B2) * (g * g)
        g_out[...] = g
        m_out[...] = mn
        v_out[...] = vn
        d_out[...] = -ADAM_LR * ((mn * bc1) / (jnp.sqrt(vn * bc2) + ADAM_EPS) + ADAM_WD * w_ref[...])

    blk, g_specs = _part_specs(gparts, part_index, tr, cc)
    grid_spec = pltpu.PrefetchScalarGridSpec(
        num_scalar_prefetch=len(dyn), grid=(r // tr,),
        in_specs=[blk, blk, blk] + g_specs, out_specs=[blk] * 4)
    return pl.pallas_call(
        body, name=name, grid_spec=grid_spec,
        out_shape=[jax.ShapeDtypeStruct((r, cc), F32)] * 4,
        compiler_params=_params(("parallel",)),
    )(*[part_index[i] for i in dyn], w, m, v, *gparts)


def _row(vec):
    return vec.reshape(1, -1)


def kernel(x, c, w_ada, b_ada, norm1_g, w_in, conv_w, conv_b, lru_wa, lru_ba, lru_wx, lru_bx, lru_lambda, w_lru_out, w_attn_out, attn_sinks, rel_bias, w_out, norm2_g, w_ff1, w_ff2, final_g, loss_target, m_w_ada, m_b_ada, m_norm1_g, m_w_in, m_conv_w, m_conv_b, m_lru_wa, m_lru_ba, m_lru_wx, m_lru_bx, m_lru_lambda, m_w_lru_out, m_w_attn_out, m_attn_sinks, m_rel_bias, m_w_out, m_norm2_g, m_w_ff1, m_w_ff2, m_final_g, v_w_ada, v_b_ada, v_norm1_g, v_w_in, v_conv_w, v_conv_b, v_lru_wa, v_lru_ba, v_lru_wx, v_lru_bx, v_lru_lambda, v_w_lru_out, v_w_attn_out, v_attn_sinks, v_rel_bias, v_w_out, v_norm2_g, v_w_ff1, v_w_ff2, v_final_g):
    dd = D_MODEL
    t = x.shape[1]
    ix, iy, ic = lax.axis_index("x"), lax.axis_index("y"), lax.axis_index("c")
    me = 4 * ix + 2 * iy + ic
    chip = 2 * ix + iy
    c_idx = jnp.reshape(ic, (1,)).astype(jnp.int32)
    chip_idx = jnp.reshape(chip, (1,)).astype(jnp.int32)

    xs = x[0]
    target = loss_target[0]
    ncol_ada = w_ada.shape[2]
    cshard = conv_w.shape[2]

    shards = [jnp.transpose(w_in[0]).astype(BF16), w_lru_out[0].astype(BF16), w_attn_out[0].astype(BF16),
              w_out[0].astype(BF16), jnp.transpose(w_ff1[0]).astype(BF16), w_ff2[0].astype(BF16)]
    s_in, s_lo, s_ao, s_o, s_f1, s_f2 = shards

    def full(g):
        return g.reshape(N_DEV * g.shape[1], dd)

    def stack(g):
        return g.reshape(N_DEV, g.shape[0] // N_DEV, dd)

    (w_in_pair,) = _run_job(_pair_job([s_in]), "ag_w_in_pair")

    pack0 = jnp.zeros((8, dd), F32).at[0:1, :].set(c).at[1:1 + CONV_WIDTH, 0:cshard].set(conv_w[0])
    (g0,) = _all_gather([pack0], "ag_cond")
    c_all = g0[:, 0, :]
    conv_w_full = jnp.transpose(g0[:, 1:1 + CONV_WIDTH, 0:cshard], (1, 0, 2)).reshape(CONV_WIDTH, dd)
    b_cols = lax.dynamic_slice(b_ada, (0, me * ncol_ada), (1, ncol_ada))
    mod_cols = _ada_fwd(c_all, w_ada[0], b_cols, "ada_fwd")
    (g1,) = _all_gather([mod_cols], "ag_mod")
    mod_mine = lax.dynamic_index_in_dim(g1, me, axis=1, keepdims=False).reshape(6, dd)
    mod = jnp.concatenate([mod_mine, jnp.zeros((2, dd), F32)], axis=0)

    bucket = _bucket_table()
    onehot = jnp.asarray((bucket.reshape(1, -1) == np.arange(N_BUCKETS)[:, None]).astype(np.float32))
    bias = _rel_bias_fwd(rel_bias, onehot, "rel_bias_fwd").reshape(N_Q_HEADS, BLOCK, 2 * BLOCK)
    sinks = attn_sinks.reshape(N_Q_HEADS)
    lp = jnp.concatenate([conv_b, lru_ba, lru_bx, lru_lambda, conv_w_full], axis=0)
    wa, wx = lru_wa[0], lru_wx[0]

    tm = _tile(t, 1024)
    tn = 512
    h = _norm_mod(xs, norm1_g, mod, 0, "norm1")
    tn2 = 1024
    chunk_w = 2 * s_in.shape[0]

    def ids(*ks):
        return jnp.stack([jnp.asarray(k, jnp.int32) for k in ks])

    tn_in = _tile(chunk_w, 1024, LANE)
    (proj,), ((g_near,),) = _mm(
        h, w_in_pair.reshape(chunk_w, dd), "nt", tm=tm, tn=tn_in, outs=[BF16], name="in_proj_own",
        chunks=(ids(chip), chunk_w, ids(0)), n_total=IN_WIDTH, jobs=[_gather_job([w_in_pair], (0.7,), "near")])
    near_ids = ids(2 * (1 - ix) + iy, 2 * ix + 1 - iy)
    (proj,), ((g_far,),) = _mm(
        h, full(g_near), "nt", tm=tm, tn=tn_in, outs=[BF16], name="in_proj_near",
        chunks=(near_ids, chunk_w, near_ids), into=proj, jobs=[_gather_job([g_near], (0.6,), "far")])
    far_chip = 2 * (1 - ix) + 1 - iy
    (proj,), ((g_o,),) = _mm(
        h, full(g_far), "nt", tm=tm, tn=tn_in, outs=[BF16], name="in_proj_far",
        chunks=(ids(far_chip), chunk_w, ids(far_chip)), into=proj, jobs=[_gather_job([s_o], (0.5, 0.8))])
    far_slots = lax.dynamic_slice_in_dim(g_far, 2 * far_chip, 2, axis=0)
    win_t = full(lax.dynamic_update_slice_in_dim(g_near, far_slots, 2 * far_chip, axis=0))
    (rec, ya), ((g_f1,),) = _lru_fwd(proj, lp, wa, wx, "lru_fwd", jobs=[_gather_job([s_f1], (0.6, 0.9))])
    att, ((g_lo,), (g_ao,)) = _attn_fwd(proj, bias, sinks, "attn_fwd",
                                        jobs=[_gather_job([s_lo], (0.3, 0.45)), _gather_job([s_ao], (0.75, 0.9))])
    wlo, wao, wo, wf1_t = full(g_lo), full(g_ao), full(g_o), full(g_f1)
    (y_a,) = _mm(ya, wlo, "nn", tm=tm, tn=tn2, outs=[BF16], name="lru_out")

    def merge_epi(acc, ya_t, ga_t, gb_t):
        return acc, _sigmoid(ga_t.astype(F32)) * ya_t.astype(F32) + _sigmoid(gb_t.astype(F32)) * acc

    y_b, merged = _mm(att, wao, "nn", tm=tm, tn=tn, outs=[BF16, BF16], epi=merge_epi,
                      tiles=[(y_a, 0), (proj, OFF_GA // tn), (proj, OFF_GB // tn)], name="attn_out_merge")

    def resid_epi(row):
        def epi(acc, x_t, mod_t):
            return acc, x_t + mod_t[row:row + 1, :] * acc
        return epi

    o1, x1 = _mm(merged, wo, "nn", tm=tm, tn=tn2, outs=[BF16, F32], epi=resid_epi(2),
                 tiles=[(xs, 0)], rows=[mod], name="out_proj")
    h2 = _norm_mod(x1, norm2_g, mod, 3, "norm2")

    def relu2_epi(acc):
        rl = jnp.maximum(acc, 0.0)
        return acc, rl * rl

    (f_pre, ff), ((g_f2,),) = _mm(h2, wf1_t, "nt", tm=tm, tn=1024, outs=[BF16, BF16], epi=relu2_epi, name="ff1",
                                  jobs=[_gather_job([s_f2], (0.6, 0.9))])
    wf2 = full(g_f2)
    tmh = _tile(t, 512)
    o2, x2 = _mm(ff, wf2, "nn", tm=tmh, tn=tn, outs=[BF16, F32], epi=resid_epi(5),
                 tiles=[(x1, 0)], rows=[mod], name="ff2")

    loss_blk, dx2, do2, st_f = _final_loss(x2, o2, _row(final_g), target, mod, "final_loss")
    loss = lax.psum(loss_blk[0, 0], ("x", "y", "c"))

    def drelu2_epi(acc, f_t):
        return (acc * (2.0 * jnp.maximum(f_t.astype(F32), 0.0)),)

    tmw, tnw, tkw = 512, dd, _tile(t, 2048)
    (df,) = _mm(do2, wf2, "nt", tm=tm, tn=1024, outs=[BF16], epi=drelu2_epi, tiles=[(f_pre, 0)], name="ff2_dx")
    (dwf2,) = _mm(ff, do2, "tn", tm=tmw, tn=tnw, tk=tkw, outs=[BF16], name="ff2_dw")
    (dh2,), ((sib_f2,),) = _mm(df, wf1_t, "nn", tm=tmh, tn=tn, outs=[F32], name="ff1_dx",
                              jobs=[_sibling_job([stack(dwf2)])])
    pair_f2 = _pair_add(stack(dwf2), sib_f2, c_idx, "rs_pair_add_w_ff2")
    (dwf1_t,), ((chips_f2,),) = _mm(df, h2, "tn", tm=tmw, tn=tnw, tk=tkw, outs=[BF16], name="ff1_dw",
                                    jobs=[_chips_job([pair_f2])])
    (dx1, st_2, do1), ((sib_f1,),) = _norm_mod_bwd(x1, norm2_g, mod, 3, dh2, dx2, (o1, 2), "norm2_bwd",
                                                   jobs=[_sibling_job([stack(dwf1_t)])])
    pair_f1 = _pair_add(stack(dwf1_t), sib_f1, c_idx, "rs_pair_add_w_ff1")

    def dmerge_epi(acc, ga_t, gb_t, ya_t, yb_t):
        sa, sb = _sigmoid(ga_t.astype(F32)), _sigmoid(gb_t.astype(F32))
        return (acc * sa, acc * sb, acc * ya_t.astype(F32) * sa * (1.0 - sa), acc * yb_t.astype(F32) * sb * (1.0 - sb))

    tmh = _tile(t, 512)
    dy_a, dy_b, dga, dgb = _mm(do1, wo, "nt", tm=tmh, tn=tn, outs=[BF16] * 4, epi=dmerge_epi,
                               tiles=[(proj, OFF_GA // tn), (proj, OFF_GB // tn), (y_a, 0), (y_b, 0)], name="out_proj_dx")
    (dwo,) = _mm(merged, do1, "tn", tm=tmw, tn=tnw, tk=tkw, outs=[BF16], name="out_proj_dw")
    (datt,) = _mm(dy_b, wao, "nt", tm=tm, tn=tn2, outs=[BF16], name="attn_out_dx")
    (dwao,) = _mm(att, dy_b, "tn", tm=tmw, tn=tnw, tk=tkw, outs=[BF16], name="attn_out_dw")
    (dya,) = _mm(dy_a, wlo, "nt", tm=tm, tn=tn2, outs=[BF16], name="lru_out_dx")
    (dwlo,) = _mm(ya, dy_a, "tn", tm=tmw, tn=tnw, tk=tkw, outs=[BF16], name="lru_out_dw")
    (dq, dk_pad, dv_pad, dbias, dsinks), (sib_3, (chips_f1,)) = _attn_bwd(
        proj, datt, bias, sinks, "attn_bwd",
        jobs=[_sibling_job([stack(dwlo), stack(dwao), stack(dwo)]), _chips_job([pair_f1])])
    pair_lo, pair_ao, pair_o = [_pair_add(stack(g_), s_, c_idx, "rs_pair_add_" + nm_)
                                for g_, s_, nm_ in zip((dwlo, dwao, dwo), sib_3, ("w_lru_out", "w_attn_out", "w_out"))]
    (dlx, dlg, st_l, dwa, dwx), ((chips_lo, chips_ao, chips_o),) = _lru_bwd(
        proj, rec, dya, lp, wa, wx, "lru_bwd", jobs=[_chips_job([pair_lo, pair_ao, pair_o])])
    dproj = jnp.concatenate([dlx, dlg, dq, dk_pad[BLOCK:].astype(BF16), dv_pad[BLOCK:].astype(BF16), dga, dgb], axis=1)
    lru_mats = jnp.concatenate([dwa.reshape(-1, dd), dwx.reshape(-1, dd)], axis=0)
    (dwin_t,), ((mats_all,),) = _mm(dproj, h, "tn", tm=384, tn=tnw, tk=tkw, outs=[BF16], name="in_proj_dw",
                                   jobs=[_gather_job([lru_mats], (0.3, 0.6))])
    (sib_in,) = _exchange_sibling([stack(dwin_t)], "rs_sibling_w_in")
    pair_in = _pair_add(stack(dwin_t), sib_in, c_idx, "rs_pair_add_w_in")
    (dh,), ((chips_in,),) = _mm(dproj, win_t, "nn", tm=tmh, tn=tn, tk=IN_WIDTH // 2, outs=[F32], name="in_proj_dx",
                              jobs=[_chips_job([pair_in])])
    grad_x, st_1 = _norm_mod_bwd(xs, norm1_g, mod, 0, dh, dx1, None, "norm1_bwd")

    drel = jnp.transpose(_rel_bias_bwd(dbias.reshape(N_Q_HEADS, -1), onehot, "rel_bias_bwd"))

    misc = jnp.concatenate([drel.reshape(1, -1), dsinks[0:1, 0:N_Q_HEADS],
                            jnp.zeros((1, dd - N_BUCKETS * N_Q_HEADS - N_Q_HEADS), F32)], axis=1)
    zero_row = jnp.zeros((1, dd), F32)
    small = jnp.concatenate([
        st_1[0:2], st_2[3:4], st_2[0:2], st_f[1:2], zero_row, zero_row,
        st_1[2:3], st_l[0:1], st_l[1:2], st_l[2:3], st_l[3:4], st_2[2:3], st_f[0:1], misc,
        st_l[4:8], jnp.zeros((4, dd), F32)], axis=0)
    (small_all,) = _all_gather([small], "ag_small")

    def pack_small(b_, n1, cb, ba, bx, lam, n2, fg, rb, sk):
        misc_ = jnp.concatenate([rb.reshape(1, -1), sk.reshape(1, -1),
                                 jnp.zeros((1, dd - N_BUCKETS * N_Q_HEADS - N_Q_HEADS), F32)], axis=1)
        return jnp.concatenate([b_.reshape(6, dd), jnp.zeros((2, dd), F32), n1, cb, ba, bx, lam, n2, _row(fg), misc_,
                                jnp.zeros((8, dd), F32)], axis=0)

    def pack_mats(wa_, wx_):
        return jnp.concatenate([wa_.reshape(-1, dd), wx_.reshape(-1, dd)], axis=0)

    every = {i: i for i in range(N_DEV)}
    w_s = pack_small(b_ada, norm1_g, conv_b, lru_ba, lru_bx, lru_lambda, norm2_g, final_g, rel_bias, attn_sinks)
    m_s = pack_small(m_b_ada, m_norm1_g, m_conv_b, m_lru_ba, m_lru_bx, m_lru_lambda, m_norm2_g, m_final_g, m_rel_bias, m_attn_sinks)
    v_s = pack_small(v_b_ada, v_norm1_g, v_conv_b, v_lru_ba, v_lru_bx, v_lru_lambda, v_norm2_g, v_final_g, v_rel_bias, v_attn_sinks)
    small_res = _adamw(w_s, m_s, v_s, [small_all] * N_DEV, "adamw_small", part_index=every)
    mats_res = _adamw(pack_mats(lru_wa, lru_wx), pack_mats(m_lru_wa, m_lru_wx), pack_mats(v_lru_wa, v_lru_wx),
                      [mats_all] * N_DEV, "adamw_lru_mats", part_index=every)

    def unpack_small(s, mt):
        nb_ = N_BUCKETS * N_Q_HEADS
        half = mt.shape[0] // 2
        return dict(
            b_ada=s[0:6].reshape(1, 6 * dd), norm1_g=s[8:9], conv_b=s[9:10], lru_ba=s[10:11], lru_bx=s[11:12],
            lru_lambda=s[12:13], norm2_g=s[13:14], final_g=s[14], rel_bias=s[15, 0:nb_].reshape(N_BUCKETS, N_Q_HEADS),
            attn_sinks=s[15:16, nb_:nb_ + N_Q_HEADS],
            lru_wa=mt[:half].reshape(1, LRU_BLOCKS, LRU_BLOCK_W, LRU_BLOCK_W),
            lru_wx=mt[half:].reshape(1, LRU_BLOCKS, LRU_BLOCK_W, LRU_BLOCK_W))

    res = {k: [None] * 4 for k in ("w_ada", "w_in", "conv_w", "w_lru_out", "w_attn_out", "w_out", "w_ff1", "w_ff2")}
    for q_, (s_, mt_) in enumerate(zip(small_res, mats_res)):
        for k_, val in unpack_small(s_, mt_).items():
            res.setdefault(k_, [None] * 4)[q_] = val

    g_conv = lax.dynamic_slice(small_res[0][16:16 + CONV_WIDTH], (0, me * cshard), (CONV_WIDTH, cshard))
    conv_res = _adamw(conv_w[0], m_conv_w[0], v_conv_w[0], [g_conv], "adamw_conv")
    res["conv_w"] = [r_[None] for r_ in conv_res]

    dmod_all = small_all[:, 0:6, :].reshape(N_DEV, 6 * dd)
    dmod_cols = lax.dynamic_slice(dmod_all, (0, me * ncol_ada), (N_DEV, ncol_ada))
    g_ada = _ada_bwd(jnp.transpose(c_all), dmod_cols, "ada_bwd")
    ada_res = _adamw(w_ada[0], m_w_ada[0], v_w_ada[0], [g_ada], "adamw_ada")
    res["w_ada"] = [r_[None] for r_ in ada_res]

    pair = [pair_in, pair_lo, pair_ao, pair_o, pair_f1, pair_f2]
    from_chips = [chips_in, chips_lo, chips_ao, chips_o, chips_f1, chips_f2]

    def summed(i):
        return [pair[i], from_chips[i], from_chips[i], from_chips[i]], {0: chip_idx, 1: 0, 2: 1, 3: 2}

    def sum_only(i, name):
        parts, index = summed(i)
        return _sum_parts(parts, index, name)

    g_in = jnp.transpose(sum_only(0, "sum_w_in"))
    res["w_in"] = [r_[None] for r_ in _adamw(w_in[0], m_w_in[0], v_w_in[0], [g_in], "adamw_w_in")]
    g_ff1 = jnp.transpose(sum_only(4, "sum_w_ff1"))
    res["w_ff1"] = [r_[None] for r_ in _adamw(w_ff1[0], m_w_ff1[0], v_w_ff1[0], [g_ff1], "adamw_w_ff1")]
    for i, (nm, w_, m_, v_) in {1: ("w_lru_out", w_lru_out, m_w_lru_out, v_w_lru_out),
                                2: ("w_attn_out", w_attn_out, m_w_attn_out, v_w_attn_out),
                                3: ("w_out", w_out, m_w_out, v_w_out),
                                5: ("w_ff2", w_ff2, m_w_ff2, v_w_ff2)}.items():
        parts, index = summed(i)
        res[nm] = [r_[None] for r_ in _adamw(w_[0], m_[0], v_[0], parts, "adamw_" + nm, part_index=index)]

    order = ["w_ada", "b_ada", "norm1_g", "w_in", "conv_w", "conv_b", "lru_wa", "lru_ba", "lru_wx", "lru_bx",
             "lru_lambda", "w_lru_out", "w_attn_out", "attn_sinks", "rel_bias", "w_out", "norm2_g", "w_ff1",
             "w_ff2", "final_g"]
    out = [loss, grad_x[None]]
    for q_ in range(4):
        out += [res[k_][q_] for k_ in order]
    return tuple(out)
```

```python
import functools
import math

import numpy as np
import jax
import jax.numpy as jnp
from jax import lax
from jax.experimental import pallas as pl
from jax.experimental.pallas import tpu as pltpu

F32 = jnp.float32
BF16 = jnp.bfloat16
MESH = pl.DeviceIdType.MESH

D_MODEL = 2048
N_Q_HEADS = 32
N_KV_HEADS = 4
GROUP = N_Q_HEADS // N_KV_HEADS
HEAD_DIM = 64
KV_WIDTH = N_KV_HEADS * HEAD_DIM
BLOCK = 128
NEG_INF = -1e30
N_BUCKETS = 32
MAX_DISTANCE = 128
LRU_BLOCKS = 16
LRU_BLOCK_W = 128
CONV_WIDTH = 4
LRU_C = 8.0
D_FF = 4 * D_MODEL
EPS = 1e-6
IN_WIDTH = 5 * D_MODEL + 2 * KV_WIDTH
OFF_LX, OFF_LG, OFF_Q, OFF_K, OFF_V, OFF_GA, OFF_GB = 0, 2048, 4096, 6144, 6400, 6656, 8704

ADAM_LR, ADAM_B1, ADAM_B2, ADAM_EPS, ADAM_WD, ADAM_STEP = 0.001, 0.9, 0.999, 1e-08, 0.01, 10

N_DEV = 8
VMEM_LIMIT_BYTES = 48 * 1024 * 1024
LANE = 128
LRU_CW = 512


def _params(sem, **kw):
    return pltpu.CompilerParams(dimension_semantics=sem, vmem_limit_bytes=VMEM_LIMIT_BYTES, **kw)


def _tile(n, pref, mult=8):
    if n <= pref:
        return n
    t = (pref // mult) * mult
    while t >= mult:
        if n % t == 0:
            return t
        t -= mult
    return n


def _sigmoid(x):
    return 1.0 / (1.0 + jnp.exp(-x))


def _gelu_tanh(x):
    k = math.sqrt(2.0 / math.pi)
    t = jnp.tanh(k * (x + 0.044715 * x * x * x))
    return 0.5 * x * (1.0 + t), t


def _gelu_tanh_grad(x, t):
    k = math.sqrt(2.0 / math.pi)
    return 0.5 * (1.0 + t) + 0.5 * x * (1.0 - t * t) * k * (1.0 + 3.0 * 0.044715 * x * x)


def _log1p(e):
    u = 1.0 + e
    return jnp.where(u == 1.0, e, jnp.log(u) * (e / jnp.where(u == 1.0, 1.0, u - 1.0)))


def _softplus(x):
    return jnp.maximum(x, 0.0) + _log1p(jnp.exp(-jnp.abs(x)))


def _neg_expm1(x):
    series = -x * (1.0 + x * (0.5 + x * (1.0 / 6.0 + x * (1.0 / 24.0 + x * (1.0 / 120.0)))))
    return jnp.where(x > -0.1, series, 1.0 - jnp.exp(x))


def _my_place():
    return lax.axis_index("x"), lax.axis_index("y"), lax.axis_index("c")


def _all_gather(arrs, name):
    return _fill_own(_run_job(_gather_job(arrs), name), arrs)


def _fill_own(stacks, shards):
    me = 4 * lax.axis_index("x") + 2 * lax.axis_index("y") + lax.axis_index("c")
    return [lax.dynamic_update_slice_in_dim(g, s[None], me, axis=0) for g, s in zip(stacks, shards)]


class _Job:
    def __init__(self, ins, outs, sems, start, finish, forwards=(), forward_at=()):
        self.ins, self.outs, self.sems = list(ins), list(outs), list(sems)
        self.start, self.finish, self.forwards, self.forward_at = start, finish, list(forwards), list(forward_at)
        assert len(self.forwards) <= len(self.forward_at)


def _gather_job(arrs, forward_at=(0.6, 0.85), part="all"):
    n = len(arrs)
    whole, near, far = part == "all", part == "near", part == "far"

    def copies(ins, outs, sems):
        send_sems, recv_sems, local_sems = sems
        x, y, c = _my_place()
        sib = (x, y, 1 - c)
        xn, yn, dg = (1 - x, y), (x, 1 - y), (1 - x, 1 - y)

        def slot(px, py, pc):
            return 4 * px + 2 * py + pc

        def cp(a, k, src, dst_slot, to):
            return pltpu.make_async_remote_copy(
                src_ref=src, dst_ref=outs[a].at[dst_slot], send_sem=send_sems.at[a, k], recv_sem=recv_sems.at[a, k],
                device_id=to, device_id_type=MESH)

        def arrival(a, k, dst_slot):
            return cp(a, k, outs[a].at[dst_slot], dst_slot, sib)

        me_slot = slot(x, y, c)
        sends, local, relay, passes, arrive = {}, [], {}, {}, {}
        for a in range(n):
            if whole:
                own = ins[a]
                sends[a, 0] = cp(a, 0, own, me_slot, sib)
            if near:
                own = ins[a].at[c]
            if whole or near:
                sends[a, 1] = cp(a, 1, own, me_slot, (*xn, c))
                sends[a, 2] = cp(a, 2, own, me_slot, (*yn, c))
                passes[a, 4] = cp(a, 4, outs[a].at[slot(*xn, c)], slot(*xn, c), sib)
                passes[a, 5] = cp(a, 5, outs[a].at[slot(*yn, c)], slot(*yn, c), sib)
            if whole or far:
                source = outs[a] if whole else ins[a]
                relayed = c * slot(*yn, c) + (1 - c) * slot(*xn, c)
                relay[a, 3] = cp(a, 3, source.at[relayed], relayed, (x + c * (1 - 2 * x), y + (1 - c) * (1 - 2 * y), c))
                passes[a, 6] = cp(a, 6, outs[a].at[slot(*dg, c)], slot(*dg, c), sib)
            arrive[a, 0] = arrival(a, 0, slot(x, y, 1 - c))
            arrive[a, 1] = arrival(a, 1, slot(*xn, c))
            arrive[a, 2] = arrival(a, 2, slot(*yn, c))
            arrive[a, 3] = arrival(a, 3, slot(*dg, c))
            arrive[a, 4] = arrival(a, 4, slot(*xn, 1 - c))
            arrive[a, 5] = arrival(a, 5, slot(*yn, 1 - c))
            arrive[a, 6] = arrival(a, 6, slot(*dg, 1 - c))
        return sends, local, relay, passes, arrive

    def pick(d, ks):
        return [d[a, k] for a in range(n) for k in ks if (a, k) in d]

    def start(ins, outs, sems):
        sends, local, relay, _, _ = copies(ins, outs, sems)
        for cp in local + pick(sends, (0, 1, 2)) + (pick(relay, (3,)) if far else []):
            cp.start()

    def forward_neighbours(ins, outs, sems):
        _, _, relay, passes, arrive = copies(ins, outs, sems)
        for cp in pick(arrive, (1, 2)):
            cp.wait_recv()
        for cp in pick(relay, (3,)) + pick(passes, (4, 5)):
            cp.start()

    def forward_diagonal(ins, outs, sems):
        _, _, _, passes, arrive = copies(ins, outs, sems)
        for cp in pick(arrive, (3,)):
            cp.wait_recv()
        for cp in pick(passes, (6,)):
            cp.start()

    def finish(ins, outs, sems):
        sends, local, relay, passes, arrive = copies(ins, outs, sems)
        for cp in pick(arrive, (6,) if far else (0, 4, 5, 6) if whole else (4, 5)):
            cp.wait_recv()
        for cp in pick(sends, (0, 1, 2)) + pick(relay, (3,)) + pick(passes, (4, 5, 6)):
            cp.wait_send()
        for cp in local:
            cp.wait()

    shapes = [a.shape[-2:] for a in arrs]
    forwards = [forward_diagonal] if far else [forward_neighbours] if near else [forward_neighbours, forward_diagonal]
    return _Job(arrs, [jax.ShapeDtypeStruct((N_DEV,) + s, a.dtype) for s, a in zip(shapes, arrs)],
                [pltpu.SemaphoreType.DMA((n, 7)), pltpu.SemaphoreType.DMA((n, 7)), pltpu.SemaphoreType.DMA((n,))],
                start, finish, forwards, forward_at)


def _pair_job(arrs):
    n = len(arrs)

    def copies(ins, outs, sems):
        send_sems, recv_sems, local_sems = sems
        x, y, c = _my_place()
        remote = [pltpu.make_async_remote_copy(
            src_ref=ins[a], dst_ref=outs[a].at[c], send_sem=send_sems.at[a], recv_sem=recv_sems.at[a],
            device_id=(x, y, 1 - c), device_id_type=MESH) for a in range(n)]
        local = []
        arrive = [pltpu.make_async_remote_copy(
            src_ref=ins[a], dst_ref=outs[a].at[1 - c], send_sem=send_sems.at[a], recv_sem=recv_sems.at[a],
            device_id=(x, y, 1 - c), device_id_type=MESH) for a in range(n)]
        return remote, local, arrive

    def start(ins, outs, sems):
        remote, local, _ = copies(ins, outs, sems)
        for cp in remote + local:
            cp.start()

    def finish(ins, outs, sems):
        remote, local, arrive = copies(ins, outs, sems)
        for cp in arrive:
            cp.wait_recv()
        for cp in remote:
            cp.wait_send()
        for cp in local:
            cp.wait()

    return _Job(arrs, [jax.ShapeDtypeStruct((2,) + a.shape, a.dtype) for a in arrs],
                [pltpu.SemaphoreType.DMA((n,)), pltpu.SemaphoreType.DMA((n,)), pltpu.SemaphoreType.DMA((n,))],
                start, finish)


def _run_job(job, name):
    ni, no = len(job.ins), len(job.outs)

    def body(*refs):
        ins, outs, sems = refs[:ni], refs[ni:ni + no], refs[ni + no:]
        job.start(ins, outs, sems)
        for fwd in job.forwards:
            fwd(ins, outs, sems)
        job.finish(ins, outs, sems)

    any_spec = pl.BlockSpec(memory_space=pl.ANY)
    return pl.pallas_call(
        body, name=name, out_shape=job.outs, in_specs=[any_spec] * ni, out_specs=[any_spec] * no,
        scratch_shapes=job.sems,
    )(*job.ins)


def _call(body, *, name, grid, in_specs, out_specs, out_shape, scratch_shapes=(), sem, args, jobs=(),
          prefetch=(), aliases=None):
    in_specs, out_specs, out_shape, scratch_shapes = list(in_specs), list(out_specs), list(out_shape), list(scratch_shapes)
    prefetch = list(prefetch)
    if not jobs and not prefetch and not aliases:
        res = pl.pallas_call(body, name=name, grid=grid, in_specs=in_specs, out_specs=out_specs, out_shape=out_shape,
                             scratch_shapes=scratch_shapes, compiler_params=_params(sem))(*args)
        return list(res), []
    n_in, n_out, n_scr = len(in_specs), len(out_specs), len(scratch_shapes)
    j_in = [len(j.ins) for j in jobs]
    j_out = [len(j.outs) for j in jobs]
    j_sem = [len(j.sems) for j in jobs]
    total = int(np.prod(grid))
    any_spec = pl.BlockSpec(memory_space=pl.ANY)

    def wrapped(*refs):
        pos = [0]

        def take(k):
            part = refs[pos[0]:pos[0] + k]
            pos[0] += k
            return part

        take(len(prefetch))
        ins = take(n_in)
        jins = [take(k) for k in j_in]
        outs = take(n_out)
        jouts = [take(k) for k in j_out]
        scr = take(n_scr)
        jsems = [take(k) for k in j_sem]
        step = pl.program_id(0)
        for d in range(1, len(grid)):
            step = step * grid[d] + pl.program_id(d)
        for j, job in enumerate(jobs):
            pl.when(step == 0)(functools.partial(job.start, jins[j], jouts[j], jsems[j]))
            for fwd, frac in zip(job.forwards, job.forward_at):
                at = min(int(total * frac), total - 1)
                pl.when(step == at)(functools.partial(fwd, jins[j], jouts[j], jsems[j]))
        body(*ins, *outs, *scr)
        for j, job in enumerate(jobs):
            pl.when(step == total - 1)(functools.partial(job.finish, jins[j], jouts[j], jsems[j]))

    grid_spec = pltpu.PrefetchScalarGridSpec(
        num_scalar_prefetch=len(prefetch), grid=grid,
        in_specs=in_specs + [any_spec] * sum(j_in),
        out_specs=out_specs + [any_spec] * sum(j_out),
        scratch_shapes=scratch_shapes + [s for j in jobs for s in j.sems])
    res = pl.pallas_call(
        wrapped, name=name, grid_spec=grid_spec,
        out_shape=out_shape + [o for j in jobs for o in j.outs],
        input_output_aliases={len(prefetch) + i: o for i, o in (aliases or {}).items()},
        compiler_params=_params(("arbitrary",) * len(grid)),
    )(*prefetch, *args, *[a for j in jobs for a in j.ins])
    res = list(res)
    own, rest = res[:n_out], res[n_out:]
    per_job = []
    for k in j_out:
        per_job.append(rest[:k])
        rest = rest[k:]
    return own, per_job


def _exchange_sibling(arrs, name):
    return _run_job(_sibling_job(arrs), name)


def _sibling_job(arrs):
    n = len(arrs)

    def copies(ins, outs, sems):
        send_sems, recv_sems = sems
        x, y, c = _my_place()
        return [pltpu.make_async_remote_copy(
            src_ref=ins[a].at[2 * k + 1 - c], dst_ref=outs[a].at[k],
            send_sem=send_sems.at[a, k], recv_sem=recv_sems.at[a, k],
            device_id=(x, y, 1 - c), device_id_type=MESH) for a in range(n) for k in range(4)]

    def start(ins, outs, sems):
        for cp in copies(ins, outs, sems):
            cp.start()

    def finish(ins, outs, sems):
        for cp in copies(ins, outs, sems):
            cp.wait()

    return _Job(arrs, [jax.ShapeDtypeStruct((4,) + a.shape[1:], a.dtype) for a in arrs],
                [pltpu.SemaphoreType.DMA((n, 4)), pltpu.SemaphoreType.DMA((n, 4))], start, finish)


def _chips_job(arrs):
    n = len(arrs)

    def copies(ins, outs, sems):
        send_sems, recv_sems = sems
        x, y, c = _my_place()
        chips = [(1 - x, y), (x, 1 - y), (1 - x, 1 - y)]
        return [pltpu.make_async_remote_copy(
            src_ref=ins[a].at[2 * px + py], dst_ref=outs[a].at[j],
            send_sem=send_sems.at[a, j], recv_sem=recv_sems.at[a, j],
            device_id=(px, py, c), device_id_type=MESH) for a in range(n) for j, (px, py) in enumerate(chips)]

    def start(ins, outs, sems):
        for cp in copies(ins, outs, sems):
            cp.start()

    def finish(ins, outs, sems):
        for cp in copies(ins, outs, sems):
            cp.wait()

    return _Job(arrs, [jax.ShapeDtypeStruct((3,) + a.shape[1:], a.dtype) for a in arrs],
                [pltpu.SemaphoreType.DMA((n, 3)), pltpu.SemaphoreType.DMA((n, 3))], start, finish)


def _pair_add(stack, from_sibling, c_idx, name):
    _, r, cc = stack.shape
    tr = _tile(r, 512)

    def body(c_ref, a_ref, b_ref, o_ref):
        o_ref[...] = (a_ref[...].astype(F32) + b_ref[...].astype(F32)).astype(o_ref.dtype)

    grid_spec = pltpu.PrefetchScalarGridSpec(
        num_scalar_prefetch=1, grid=(4, r // tr),
        in_specs=[pl.BlockSpec((None, tr, cc), lambda k, i, c_ref: (2 * k + c_ref[0], i, 0)),
                  pl.BlockSpec((None, tr, cc), lambda k, i, c_ref: (k, i, 0))],
        out_specs=pl.BlockSpec((None, tr, cc), lambda k, i, c_ref: (k, i, 0)))
    return pl.pallas_call(
        body, name=name, grid_spec=grid_spec,
        out_shape=jax.ShapeDtypeStruct((4, r, cc), BF16),
        compiler_params=_params(("parallel", "parallel")),
    )(c_idx, stack, from_sibling)


_DIMS = {"nn": ((1,), (0,)), "nt": ((1,), (1,)), "tn": ((0,), (0,))}


def _mm(a, b, mode, *, tm, tn, tk=None, outs, epi=None, tiles=(), rows=(), name, jobs=(), chunks=None, into=None,
        n_total=None):
    if mode == "tn":
        kk, m = a.shape
    else:
        m, kk = a.shape
    n_all = b.shape[0] if mode == "nt" else b.shape[1]
    nn = n_all if chunks is None else chunks[0].shape[0] * chunks[1]
    n_all = n_all if n_total is None else n_total
    tk = kk if tk is None else tk
    assert m % tm == 0 and nn % tn == 0 and kk % tk == 0, (name, a.shape, b.shape, tm, tn, tk)
    nk = kk // tk
    nt, nr, no = len(tiles), len(rows), len(outs)
    n_into = 0 if into is None else 1

    def body(*refs):
        a_ref, b_ref = refs[:2]
        tile_refs = refs[2:2 + nt]
        row_refs = refs[2 + nt:2 + nt + nr]
        out_refs = refs[2 + nt + nr + n_into:2 + nt + nr + n_into + no]
        part = lax.dot_general(a_ref[...], b_ref[...], (_DIMS[mode], ((), ())), preferred_element_type=F32)

        def finish(acc):
            if epi is None:
                res = (acc,)
            else:
                res = epi(acc, *[t[...] for t in tile_refs], *[r[...] for r in row_refs])
            for o_ref, val in zip(out_refs, res):
                o_ref[...] = val.astype(o_ref.dtype)

        if nk == 1:
            finish(part)
        else:
            acc_ref = refs[-1]
            k = pl.program_id(2)

            @pl.when(k == 0)
            def _():
                acc_ref[...] = part

            @pl.when(k > 0)
            def _():
                acc_ref[...] += part

            @pl.when(k == nk - 1)
            def _():
                finish(acc_ref[...])

    if chunks is None:
        col = b_row = lambda j, *_: j
        prefetch = []
    else:
        assert mode == "nt" and not tiles and not rows and chunks[1] % tn == 0
        per = chunks[1] // tn
        col = lambda j, ids, b_ids: ids[j // per] * per + j % per
        b_row = lambda j, ids, b_ids: b_ids[j // per] * per + j % per
        prefetch = [chunks[0], chunks[2]]
    if mode == "tn":
        a_spec = pl.BlockSpec((tk, tm), lambda i, j, k, *_: (k, i))
    else:
        a_spec = pl.BlockSpec((tm, tk), lambda i, j, k, *_: (i, k))
    if mode == "nt":
        b_spec = pl.BlockSpec((tn, tk), lambda i, j, k, *s: (b_row(j, *s), k))
    else:
        b_spec = pl.BlockSpec((tk, tn), lambda i, j, k, *_: (k, j))
    tile_specs = [pl.BlockSpec((tm, tn), functools.partial(lambda i, j, k, off: (i, j + off), off=off))
                  for _, off in tiles]
    row_specs = [pl.BlockSpec((r.shape[0], tn), lambda i, j, k: (0, j)) for r in rows]
    out_spec = pl.BlockSpec((tm, tn), lambda i, j, k, *s: (i, col(j, *s)))
    args = [a, b, *[t for t, _ in tiles], *rows]
    in_specs = [a_spec, b_spec] + tile_specs + row_specs
    aliases = None
    if into is not None:
        aliases = {len(args): 0}
        args.append(into)
        in_specs.append(pl.BlockSpec(memory_space=pl.ANY))
    res, jres = _call(
        body, name=name, grid=(m // tm, nn // tn, nk),
        in_specs=in_specs, out_specs=[out_spec] * no,
        out_shape=[jax.ShapeDtypeStruct((m, n_all), dt) for dt in outs],
        scratch_shapes=[pltpu.VMEM((tm, tn), F32)] if nk > 1 else [],
        sem=("parallel", "parallel", "arbitrary"),
        args=args, jobs=jobs, prefetch=prefetch, aliases=aliases)
    return (res, jres) if jobs else res


def _ada_fwd(c_all, w_ada, b_ada_cols, name):
    nb, dd = c_all.shape
    ncol = w_ada.shape[1]
    tn = _tile(ncol, 512, LANE)

    def body(c_ref, w_ref, b_ref, o_ref):
        cv = c_ref[...]
        act = (cv * _sigmoid(cv)).astype(BF16)
        o_ref[...] = jnp.dot(act, w_ref[...].astype(BF16), preferred_element_type=F32) + b_ref[...]

    return pl.pallas_call(
        body, name=name, grid=(ncol // tn,),
        in_specs=[pl.BlockSpec((nb, dd), lambda j: (0, 0)), pl.BlockSpec((dd, tn), lambda j: (0, j)),
                  pl.BlockSpec((1, tn), lambda j: (0, j))],
        out_specs=pl.BlockSpec((nb, tn), lambda j: (0, j)),
        out_shape=jax.ShapeDtypeStruct((nb, ncol), F32),
        compiler_params=_params(("parallel",)),
    )(c_all, w_ada, b_ada_cols)


def _ada_bwd(c_all_t, dmod_cols, name):
    dd, nb = c_all_t.shape
    ncol = dmod_cols.shape[1]
    tr = _tile(dd, 256)

    def body(c_ref, d_ref, o_ref):
        cv = c_ref[...]
        act = (cv * _sigmoid(cv)).astype(BF16).astype(F32)
        dm = d_ref[...].astype(BF16).astype(F32)
        acc = act[:, 0:1] * dm[0:1, :]
        for bi in range(1, nb):
            acc = acc + act[:, bi:bi + 1] * dm[bi:bi + 1, :]
        o_ref[...] = acc

    return pl.pallas_call(
        body, name=name, grid=(dd // tr,),
        in_specs=[pl.BlockSpec((tr, nb), lambda i: (i, 0)), pl.BlockSpec((nb, ncol), lambda i: (0, 0))],
        out_specs=pl.BlockSpec((tr, ncol), lambda i: (i, 0)),
        out_shape=jax.ShapeDtypeStruct((dd, ncol), F32),
        compiler_params=_params(("parallel",)),
    )(c_all_t, dmod_cols)


def _norm_mod(x, gain, mod, row0, name):
    t, dd = x.shape
    tt = _tile(t, 512)

    def body(x_ref, g_ref, m_ref, h_ref):
        xv = x_ref[...]
        r = lax.rsqrt(jnp.mean(xv * xv, axis=-1, keepdims=True) + EPS)
        shift, scale = m_ref[row0:row0 + 1, :], m_ref[row0 + 1:row0 + 2, :]
        h_ref[...] = ((xv * r * g_ref[...]) * (1.0 + scale) + shift).astype(BF16)

    return pl.pallas_call(
        body, name=name, grid=(t // tt,),
        in_specs=[pl.BlockSpec((tt, dd), lambda i: (i, 0)), pl.BlockSpec((1, dd), lambda i: (0, 0)),
                  pl.BlockSpec((8, dd), lambda i: (0, 0))],
        out_specs=pl.BlockSpec((tt, dd), lambda i: (i, 0)),
        out_shape=jax.ShapeDtypeStruct((t, dd), BF16),
        compiler_params=_params(("parallel",)),
    )(x, gain, mod)


def _norm_mod_bwd(x, gain, mod, row0, dh, dres, branch, name, jobs=()):
    t, dd = x.shape
    tt = _tile(t, 256)
    has_branch = branch is not None

    def body(*refs):
        if has_branch:
            x_ref, g_ref, m_ref, dh_ref, dres_ref, o_ref, dx_ref, st_ref, do_ref = refs
        else:
            x_ref, g_ref, m_ref, dh_ref, dres_ref, dx_ref, st_ref = refs

        @pl.when(pl.program_id(0) == 0)
        def _():
            st_ref[...] = jnp.zeros_like(st_ref)

        xv = x_ref[...]
        g = g_ref[...]
        scale = m_ref[row0 + 1:row0 + 2, :]
        r = lax.rsqrt(jnp.mean(xv * xv, axis=-1, keepdims=True) + EPS)
        nv = xv * r
        dhv = dh_ref[...].astype(F32)
        st_ref[0:1, :] += jnp.sum(dhv, axis=0, keepdims=True)
        st_ref[1:2, :] += jnp.sum(dhv * (nv * g), axis=0, keepdims=True)
        dng = dhv * (1.0 + scale)
        st_ref[2:3, :] += jnp.sum(dng * nv, axis=0, keepdims=True)
        dn = dng * g
        dx = dres_ref[...] + r * (dn - nv * jnp.mean(dn * nv, axis=-1, keepdims=True))
        dx_ref[...] = dx
        if has_branch:
            ov = o_ref[...].astype(F32)
            gate = m_ref[branch[1]:branch[1] + 1, :]
            st_ref[3:4, :] += jnp.sum(dx * ov, axis=0, keepdims=True)
            do_ref[...] = (dx * gate).astype(BF16)

    row = pl.BlockSpec((tt, dd), lambda i: (i, 0))
    in_specs = [row, pl.BlockSpec((1, dd), lambda i: (0, 0)), pl.BlockSpec((8, dd), lambda i: (0, 0)), row, row]
    args = [x, gain, mod, dh, dres]
    out_specs = [row, pl.BlockSpec((8, dd), lambda i: (0, 0))]
    out_shape = [jax.ShapeDtypeStruct((t, dd), F32), jax.ShapeDtypeStruct((8, dd), F32)]
    if has_branch:
        in_specs.append(row)
        args.append(branch[0])
        out_specs.append(row)
        out_shape.append(jax.ShapeDtypeStruct((t, dd), BF16))
    res, jres = _call(body, name=name, grid=(t // tt,), in_specs=in_specs, out_specs=out_specs, out_shape=out_shape,
                      sem=("arbitrary",), args=args, jobs=jobs)
    return (res, jres) if jobs else res


def _final_loss(x2, o2, final_g, target, mod, name):
    t, dd = x2.shape
    tt = _tile(t, 256)
    nsteps = t // tt

    def body(x_ref, o_ref, g_ref, t_ref, m_ref, loss_ref, dx_ref, do_ref, st_ref, lacc):
        i = pl.program_id(0)

        @pl.when(i == 0)
        def _():
            st_ref[...] = jnp.zeros_like(st_ref)
            lacc[...] = jnp.zeros_like(lacc)

        xv = x_ref[...]
        g = g_ref[...]
        r = lax.rsqrt(jnp.mean(xv * xv, axis=-1, keepdims=True) + EPS)
        nv = xv * r
        err = nv * g - t_ref[...]
        lacc[...] += jnp.sum(err * err, axis=0, keepdims=True)
        dy = err * (1.0 / dd)
        st_ref[0:1, :] += jnp.sum(dy * nv, axis=0, keepdims=True)
        dn = dy * g
        dx = r * (dn - nv * jnp.mean(dn * nv, axis=-1, keepdims=True))
        dx_ref[...] = dx
        st_ref[1:2, :] += jnp.sum(dx * o_ref[...].astype(F32), axis=0, keepdims=True)
        do_ref[...] = (dx * m_ref[5:6, :]).astype(BF16)

        @pl.when(i == nsteps - 1)
        def _():
            total = jnp.sum(lacc[...], axis=-1, keepdims=True) * (0.5 / dd)
            loss_ref[...] = jnp.broadcast_to(total, loss_ref.shape)

    row = pl.BlockSpec((tt, dd), lambda i: (i, 0))
    return pl.pallas_call(
        body, name=name, grid=(nsteps,),
        in_specs=[row, row, pl.BlockSpec((1, dd), lambda i: (0, 0)), row, pl.BlockSpec((8, dd), lambda i: (0, 0))],
        out_specs=[pl.BlockSpec((8, LANE), lambda i: (0, 0)), row, row, pl.BlockSpec((8, dd), lambda i: (0, 0))],
        out_shape=[jax.ShapeDtypeStruct((8, LANE), F32), jax.ShapeDtypeStruct((t, dd), F32),
                   jax.ShapeDtypeStruct((t, dd), BF16), jax.ShapeDtypeStruct((8, dd), F32)],
        scratch_shapes=[pltpu.VMEM((1, dd), F32)],
        compiler_params=_params(("arbitrary",)),
    )(x2, o2, final_g, target, mod)


def _lru_gates(xt, prev8, lp, wa_ref, wx_ref, first_tile, cw):
    tt = xt.shape[0]
    ext = jnp.concatenate([prev8, xt], axis=0)
    xs = [pltpu.roll(ext, s, 0)[8:, :] if s else xt for s in range(CONV_WIDTH)]
    xc = lp[0:1, :] + lp[7:8, :] * xs[0] + lp[6:7, :] * xs[1] + lp[5:6, :] * xs[2] + lp[4:5, :] * xs[3]
    xcb = xc.astype(BF16)
    za, zi = [], []
    for jb in range(cw // LRU_BLOCK_W):
        sl = slice(jb * LRU_BLOCK_W, (jb + 1) * LRU_BLOCK_W)
        za.append(jnp.dot(xcb[:, sl], wa_ref[jb].astype(BF16), preferred_element_type=F32))
        zi.append(jnp.dot(xcb[:, sl], wx_ref[jb].astype(BF16), preferred_element_type=F32))
    ra = _sigmoid(jnp.concatenate(za, axis=1) + lp[1:2, :])
    ri = _sigmoid(jnp.concatenate(zi, axis=1) + lp[2:3, :])
    sp = _softplus(-lp[3:4, :])
    log_a = -LRU_C * ra * sp
    av = jnp.exp(log_a)
    seq_start = jnp.logical_and(first_tile, lax.broadcasted_iota(jnp.int32, (tt, 1), 0) == 0)
    mult = jnp.where(seq_start, 1.0, jnp.sqrt(_neg_expm1(2.0 * log_a)))
    return xs, xc, xcb, ra, ri, sp, av, mult, seq_start


def _scan_groups(a, u, reverse):
    tt, cw = a.shape
    a3, u3 = a.reshape(tt // 8, 8, cw), u.reshape(tt // 8, 8, cw)
    r = lax.broadcasted_iota(jnp.int32, (1, 8, 1), 1)
    for s in (1, 2, 4):
        shift = 8 - s if reverse else s
        take = r < 8 - s if reverse else r >= s
        u3 = jnp.where(take, a3 * pltpu.roll(u3, shift, 1) + u3, u3)
        a3 = jnp.where(take, a3 * pltpu.roll(a3, shift, 1), a3)
    return a3, u3


def _scan_carry(a_s, u_s, h_s, carry, reverse):
    ng = a_s.shape[0]

    def step(i, h):
        g = ng - 1 - i if reverse else i
        hg = u_s[g] + a_s[g] * h
        h_s[g] = hg
        return hg[0:1, :] if reverse else hg[7:8, :]

    return lax.fori_loop(0, ng, step, carry, unroll=4)


def _lru_fwd(proj, lp, wa, wx, name, jobs=()):
    t = proj.shape[0]
    w = D_MODEL
    cw = LRU_CW
    tt = _tile(t, 256)
    nb = cw // LRU_BLOCK_W

    def body(lx_ref, lg_ref, lp_ref, wa_ref, wx_ref, rec_ref, ya_ref, a_s, u_s, h_s, halo, carry):
        ti = pl.program_id(1)

        @pl.when(ti == 0)
        def _():
            halo[...] = jnp.zeros_like(halo)
            carry[...] = jnp.zeros_like(carry)

        xt = lx_ref[...].astype(F32)
        lp_v = lp_ref[...]
        _, xc, _, _, ri, _, av, mult, _ = _lru_gates(xt, halo[...], lp_v, wa_ref, wx_ref, ti == 0, cw)
        halo[...] = xt[tt - 8:, :]
        a_s[...], u_s[...] = _scan_groups(av, mult * (ri * xc), False)
        carry[...] = _scan_carry(a_s, u_s, h_s, carry[...], False)
        rec = h_s[...].reshape(tt, cw)
        rec_ref[...] = rec.astype(BF16)
        gelu, _ = _gelu_tanh(lg_ref[...].astype(F32))
        ya_ref[...] = (rec * gelu).astype(BF16)

    off_g = OFF_LG // cw
    res, jres = _call(
        body, name=name, grid=(w // cw, t // tt),
        in_specs=[pl.BlockSpec((tt, cw), lambda ci, ti: (ti, ci)),
                  pl.BlockSpec((tt, cw), lambda ci, ti: (ti, off_g + ci)),
                  pl.BlockSpec((8, cw), lambda ci, ti: (0, ci)),
                  pl.BlockSpec((nb, LRU_BLOCK_W, LRU_BLOCK_W), lambda ci, ti: (ci, 0, 0)),
                  pl.BlockSpec((nb, LRU_BLOCK_W, LRU_BLOCK_W), lambda ci, ti: (ci, 0, 0))],
        out_specs=[pl.BlockSpec((tt, cw), lambda ci, ti: (ti, ci))] * 2,
        out_shape=[jax.ShapeDtypeStruct((t, w), BF16)] * 2,
        scratch_shapes=[pltpu.VMEM((tt // 8, 8, cw), F32)] * 3 + [pltpu.VMEM((8, cw), F32), pltpu.VMEM((1, cw), F32)],
        sem=("parallel", "arbitrary"), args=[proj, proj, lp, wa, wx], jobs=jobs)
    return (res, jres) if jobs else res


def _lru_bwd(proj, rec, dya, lp, wa, wx, name, jobs=()):
    t = proj.shape[0]
    w = D_MODEL
    cw = LRU_CW
    tt = _tile(t, 256)
    nt = t // tt
    nb = cw // LRU_BLOCK_W
    r8 = tt // 8

    def body(lx_ref, lxp_ref, lg_ref, rec_ref, recp_ref, dya_ref, lp_ref, wa_ref, wx_ref,
             dlx_ref, dlg_ref, st_ref, dwa_ref, dwx_ref, a_s, d_s, dh_s, nhalo, carry):
        step_i = pl.program_id(1)
        ti = nt - 1 - step_i

        @pl.when(step_i == 0)
        def _():
            st_ref[...] = jnp.zeros_like(st_ref)
            dwa_ref[...] = jnp.zeros_like(dwa_ref)
            dwx_ref[...] = jnp.zeros_like(dwx_ref)
            nhalo[...] = jnp.zeros_like(nhalo)
            carry[...] = jnp.zeros_like(carry)

        first = ti == 0
        keep = jnp.where(first, 0.0, 1.0)
        xt = lx_ref[...].astype(F32)
        prev8 = lxp_ref[...].astype(F32) * keep
        lp_v = lp_ref[...]
        xs, xc, xcb, ra, ri, sp, av, mult, seq_start = _lru_gates(xt, prev8, lp_v, wa_ref, wx_ref, first, cw)

        lg = lg_ref[...].astype(F32)
        gelu, th = _gelu_tanh(lg)
        dyav = dya_ref[...].astype(F32)
        recv = rec_ref[...].astype(F32)
        dlg_ref[...] = (dyav * recv * _gelu_tanh_grad(lg, th)).astype(BF16)

        drec = dyav * gelu
        e_in = carry[...]
        a_s[...], d_s[...] = _scan_groups(av, av * drec, True)
        carry[...] = _scan_carry(a_s, d_s, dh_s, e_in, True)
        e_next = jnp.concatenate([dh_s[...].reshape(tt, cw), jnp.broadcast_to(e_in, (8, cw))], axis=0)
        dh = drec + pltpu.roll(e_next, tt + 7, 0)[:tt, :]
        rprev8 = recp_ref[...].astype(F32) * keep
        hprev = pltpu.roll(jnp.concatenate([rprev8, recv], axis=0), 1, 0)[8:, :]
        da = dh * hprev
        dmult = jnp.where(seq_start, 0.0, dh * ri * xc)
        dri = dh * mult * xc
        dxc = dh * mult * ri
        dlog_a = da * av - dmult * (av * av) / mult
        dra = dlog_a * (-LRU_C * sp)
        lam = lp_v[3:4, :]
        st_ref[3:4, :] += jnp.sum(dlog_a * ra, axis=0, keepdims=True) * (LRU_C * _sigmoid(-lam))
        dza = dra * ra * (1.0 - ra)
        dzi = dri * ri * (1.0 - ri)
        st_ref[1:2, :] += jnp.sum(dza, axis=0, keepdims=True)
        st_ref[2:3, :] += jnp.sum(dzi, axis=0, keepdims=True)
        dzab, dzib = dza.astype(BF16), dzi.astype(BF16)
        back = []
        for jb in range(nb):
            sl = slice(jb * LRU_BLOCK_W, (jb + 1) * LRU_BLOCK_W)
            dwa_ref[jb] += lax.dot_general(xcb[:, sl], dzab[:, sl], (_DIMS["tn"], ((), ())), preferred_element_type=F32)
            dwx_ref[jb] += lax.dot_general(xcb[:, sl], dzib[:, sl], (_DIMS["tn"], ((), ())), preferred_element_type=F32)
            back.append(
                lax.dot_general(dzab[:, sl], wa_ref[jb].astype(BF16), (_DIMS["nt"], ((), ())), preferred_element_type=F32)
                + lax.dot_general(dzib[:, sl], wx_ref[jb].astype(BF16), (_DIMS["nt"], ((), ())), preferred_element_type=F32))
        dxc = dxc + jnp.concatenate(back, axis=1)
        st_ref[0:1, :] += jnp.sum(dxc, axis=0, keepdims=True)
        for k in range(CONV_WIDTH):
            st_ref[4 + k:5 + k, :] += jnp.sum(dxc * xs[CONV_WIDTH - 1 - k], axis=0, keepdims=True)
        ext = jnp.concatenate([dxc, nhalo[...]], axis=0)
        dlx = lp_v[7:8, :] * dxc
        for s in range(1, CONV_WIDTH):
            dlx = dlx + lp_v[7 - s:8 - s, :] * pltpu.roll(ext, tt + 8 - s, 0)[:tt, :]
        dlx_ref[...] = dlx.astype(BF16)
        nhalo[...] = dxc[0:8, :]

    off_g = OFF_LG // cw
    tile = lambda off: pl.BlockSpec((tt, cw), lambda ci, si: (nt - 1 - si, off + ci))
    prev = lambda off: pl.BlockSpec((8, cw), lambda ci, si: (jnp.maximum((nt - 1 - si) * r8 - 1, 0), off + ci))
    wspec = pl.BlockSpec((nb, LRU_BLOCK_W, LRU_BLOCK_W), lambda ci, si: (ci, 0, 0))
    st_spec = pl.BlockSpec((8, cw), lambda ci, si: (0, ci))
    res, jres = _call(
        body, name=name, grid=(w // cw, nt),
        in_specs=[tile(0), prev(0), tile(off_g), tile(0), prev(0), tile(0), st_spec, wspec, wspec],
        out_specs=[tile(0), tile(0), st_spec, wspec, wspec],
        out_shape=[jax.ShapeDtypeStruct((t, w), BF16), jax.ShapeDtypeStruct((t, w), BF16),
                   jax.ShapeDtypeStruct((8, w), F32),
                   jax.ShapeDtypeStruct((LRU_BLOCKS, LRU_BLOCK_W, LRU_BLOCK_W), F32),
                   jax.ShapeDtypeStruct((LRU_BLOCKS, LRU_BLOCK_W, LRU_BLOCK_W), F32)],
        scratch_shapes=[pltpu.VMEM((tt // 8, 8, cw), F32)] * 3 + [pltpu.VMEM((8, cw), F32), pltpu.VMEM((1, cw), F32)],
        sem=("parallel", "arbitrary"), args=[proj, proj, proj, rec, rec, dya, lp, wa, wx], jobs=jobs)
    return (res, jres) if jobs else res


def _band_valid(first_block):
    qi = lax.broadcasted_iota(jnp.int32, (BLOCK, 2 * BLOCK), 0)
    ki = lax.broadcasted_iota(jnp.int32, (BLOCK, 2 * BLOCK), 1)
    rel = qi + BLOCK - ki
    valid = jnp.logical_and(rel >= 0, rel < BLOCK)
    return jnp.logical_and(valid, jnp.logical_or(ki >= BLOCK, jnp.logical_not(first_block)))


ATTN_STACK_BWD = 8


def _low_half():
    return lax.broadcasted_iota(jnp.int32, (1, LANE), 1) < HEAD_DIM


def _stack_heads(x, h0, ns):
    low = _low_half()
    parts = []
    for g in range(ns):
        h = h0 + g
        slab = x[:, (h // 2) * LANE:(h // 2 + 1) * LANE]
        parts.append(jnp.where(low if h % 2 == 0 else jnp.logical_not(low), slab, jnp.zeros_like(slab)))
    return jnp.concatenate(parts, axis=0)


def _unstack_heads(y, ns):
    low = _low_half()
    pairs = [jnp.where(low, y[(2 * j) * BLOCK:(2 * j + 1) * BLOCK], y[(2 * j + 1) * BLOCK:(2 * j + 2) * BLOCK])
             for j in range(ns // 2)]
    return pairs[0] if len(pairs) == 1 else jnp.concatenate(pairs, axis=1)


def _dup_kv(xall, kv):
    part = xall[:, kv * HEAD_DIM:(kv + 1) * HEAD_DIM]
    return jnp.concatenate([part, part], axis=1)


def _fold_halves(a):
    return a + pltpu.roll(a, HEAD_DIM, 1)


def _group_probs(qg, k2, bias_g, sink_g, valid):
    s = lax.dot_general(qg, k2, (_DIMS["nt"], ((), ())), preferred_element_type=F32) * (HEAD_DIM ** -0.5)
    s = jnp.where(valid[None], s.reshape(bias_g.shape) + bias_g, NEG_INF)
    m = jnp.maximum(jnp.max(s, axis=-1, keepdims=True), sink_g)
    e = jnp.exp(s - m)
    es = jnp.exp(sink_g - m)
    inv = 1.0 / (jnp.sum(e, axis=-1, keepdims=True) + es)
    return e * inv, es * inv


def _group_sinks(s_ref, h0, ns):
    return jnp.concatenate([jnp.full((1, BLOCK, 1), s_ref[h0 + g], F32) for g in range(ns)], axis=0)


def _attn_specs(t):
    nblk = t // BLOCK
    q_spec = pl.BlockSpec((BLOCK, D_MODEL), lambda n: (n, OFF_Q // D_MODEL))
    kc = pl.BlockSpec((BLOCK, KV_WIDTH), lambda n: (n, OFF_K // KV_WIDTH))
    kp = pl.BlockSpec((BLOCK, KV_WIDTH), lambda n: (jnp.maximum(n - 1, 0), OFF_K // KV_WIDTH))
    vc = pl.BlockSpec((BLOCK, KV_WIDTH), lambda n: (n, OFF_V // KV_WIDTH))
    vp = pl.BlockSpec((BLOCK, KV_WIDTH), lambda n: (jnp.maximum(n - 1, 0), OFF_V // KV_WIDTH))
    bias_spec = pl.BlockSpec((N_Q_HEADS, BLOCK, 2 * BLOCK), lambda n: (0, 0, 0))
    sink_spec = pl.BlockSpec(memory_space=pltpu.SMEM)
    return nblk, [q_spec, kp, kc, vp, vc, bias_spec, sink_spec]


def _attn_fwd(proj, bias, sinks, name, jobs=()):
    t = proj.shape[0]
    nblk, in_specs = _attn_specs(t)

    def body(q_ref, kp_ref, kc_ref, vp_ref, vc_ref, b_ref, s_ref, o_ref):
        valid = _band_valid(pl.program_id(0) == 0)
        q = q_ref[...]
        kall = jnp.concatenate([kp_ref[...], kc_ref[...]], axis=0)
        vall = jnp.concatenate([vp_ref[...], vc_ref[...]], axis=0)
        outs = []
        for h in range(N_Q_HEADS):
            kv = h // GROUP
            kk = kall[:, kv * HEAD_DIM:(kv + 1) * HEAD_DIM]
            vv = vall[:, kv * HEAD_DIM:(kv + 1) * HEAD_DIM]
            s = lax.dot_general(q[:, h * HEAD_DIM:(h + 1) * HEAD_DIM], kk, (_DIMS["nt"], ((), ())),
                                preferred_element_type=F32) * (HEAD_DIM ** -0.5)
            s = jnp.where(valid, s + b_ref[h], NEG_INF)
            sink = s_ref[h]
            m = jnp.maximum(jnp.max(s, axis=-1, keepdims=True), sink)
            e = jnp.exp(s - m)
            p = e * (1.0 / (jnp.sum(e, axis=-1, keepdims=True) + jnp.exp(sink - m)))
            outs.append(jnp.dot(p.astype(BF16), vv, preferred_element_type=F32))
        o_ref[...] = jnp.concatenate(outs, axis=1).astype(BF16)

    res, jres = _call(
        body, name=name, grid=(nblk,), in_specs=in_specs,
        out_specs=[pl.BlockSpec((BLOCK, D_MODEL), lambda n: (n, 0))],
        out_shape=[jax.ShapeDtypeStruct((t, D_MODEL), BF16)],
        sem=("parallel",), args=[proj, proj, proj, proj, proj, bias, sinks], jobs=jobs)
    return (res[0], jres) if jobs else res[0]


def _attn_bwd(proj, datt, bias, sinks, name, jobs=()):
    t = proj.shape[0]
    nblk, in_specs = _attn_specs(t)
    in_specs = in_specs + [pl.BlockSpec((BLOCK, D_MODEL), lambda n: (n, 0))]
    scale = HEAD_DIM ** -0.5

    def body(q_ref, kp_ref, kc_ref, vp_ref, vc_ref, b_ref, s_ref, do_ref, dq_ref, dk_ref, dv_ref, db_ref, ds_ref):
        n = pl.program_id(0)

        @pl.when(n == 0)
        def _():
            dk_ref[...] = jnp.zeros_like(dk_ref)
            dv_ref[...] = jnp.zeros_like(dv_ref)
            db_ref[...] = jnp.zeros_like(db_ref)
            ds_ref[...] = jnp.zeros_like(ds_ref)

        valid = _band_valid(n == 0)
        lane = lax.broadcasted_iota(jnp.int32, (1, LANE), 1)
        q = q_ref[...]
        do = do_ref[...]
        kall = jnp.concatenate([kp_ref[...], kc_ref[...]], axis=0)
        vall = jnp.concatenate([vp_ref[...], vc_ref[...]], axis=0)
        low = _low_half()
        dqs, dks, dvs = [], [], []
        dsink_row = jnp.zeros((1, LANE), F32)
        ns = ATTN_STACK_BWD
        for kv in range(N_KV_HEADS):
            k2, v2 = _dup_kv(kall, kv), _dup_kv(vall, kv)
            dk_acc = jnp.zeros((2 * BLOCK, LANE), F32)
            dv_acc = jnp.zeros((2 * BLOCK, LANE), F32)
            for h0 in range(kv * GROUP, (kv + 1) * GROUP, ns):
                qg, dog = _stack_heads(q, h0, ns), _stack_heads(do, h0, ns)
                p, psink = _group_probs(qg, k2, b_ref[h0:h0 + ns], _group_sinks(s_ref, h0, ns), valid)
                dp = lax.dot_general(dog, v2, (_DIMS["nt"], ((), ())), preferred_element_type=F32)
                dp = dp.reshape(ns, BLOCK, 2 * BLOCK)
                delta = jnp.sum(p * dp, axis=-1, keepdims=True)
                ds = p * (dp - delta)
                db_ref[h0:h0 + ns] += ds
                dsink = -jnp.sum(psink * delta, axis=1, keepdims=True)
                for g in range(ns):
                    dsink_row = dsink_row + jnp.where(lane == h0 + g, dsink[g], 0.0)
                dsb = ds.reshape(ns * BLOCK, 2 * BLOCK).astype(BF16)
                pb = p.reshape(ns * BLOCK, 2 * BLOCK).astype(BF16)
                dqs.append(_unstack_heads(jnp.dot(dsb, k2, preferred_element_type=F32) * scale, ns))
                dk_acc = dk_acc + lax.dot_general(dsb, qg, (_DIMS["tn"], ((), ())), preferred_element_type=F32)
                dv_acc = dv_acc + lax.dot_general(pb, dog, (_DIMS["tn"], ((), ())), preferred_element_type=F32)
            dks.append(_fold_halves(dk_acc) * scale)
            dvs.append(_fold_halves(dv_acc))
        dq_ref[...] = jnp.concatenate(dqs, axis=1).astype(BF16)
        rows = pl.ds(pl.multiple_of(n * BLOCK, BLOCK), 2 * BLOCK)
        dk_ref[rows, :] += jnp.concatenate([jnp.where(low, dks[0], dks[1]), jnp.where(low, dks[2], dks[3])], axis=1)
        dv_ref[rows, :] += jnp.concatenate([jnp.where(low, dvs[0], dvs[1]), jnp.where(low, dvs[2], dvs[3])], axis=1)
        ds_ref[0:1, :] += dsink_row

    res, jres = _call(
        body, name=name, grid=(nblk,), in_specs=in_specs,
        out_specs=[pl.BlockSpec((BLOCK, D_MODEL), lambda n: (n, 0)),
                   pl.BlockSpec((t + BLOCK, KV_WIDTH), lambda n: (0, 0)),
                   pl.BlockSpec((t + BLOCK, KV_WIDTH), lambda n: (0, 0)),
                   pl.BlockSpec((N_Q_HEADS, BLOCK, 2 * BLOCK), lambda n: (0, 0, 0)),
                   pl.BlockSpec((8, LANE), lambda n: (0, 0))],
        out_shape=[jax.ShapeDtypeStruct((t, D_MODEL), BF16),
                   jax.ShapeDtypeStruct((t + BLOCK, KV_WIDTH), F32),
                   jax.ShapeDtypeStruct((t + BLOCK, KV_WIDTH), F32),
                   jax.ShapeDtypeStruct((N_Q_HEADS, BLOCK, 2 * BLOCK), F32),
                   jax.ShapeDtypeStruct((8, LANE), F32)],
        sem=("arbitrary",), args=[proj, proj, proj, proj, proj, bias, sinks, datt], jobs=jobs)
    return (res, jres) if jobs else res


def _bucket_table():
    qi = np.arange(BLOCK)[:, None]
    ki = np.arange(2 * BLOCK)[None, :]
    rel = np.maximum(qi + BLOCK - ki, 0)
    max_exact = N_BUCKETS // 2
    relf = np.maximum(rel, 1).astype(np.float32)
    large = max_exact + (np.log(relf / max_exact) / math.log(MAX_DISTANCE / max_exact)
                         * (N_BUCKETS - max_exact)).astype(np.int32)
    large = np.minimum(large, N_BUCKETS - 1)
    return np.where(rel < max_exact, rel, large).astype(np.int32)


def _rel_bias_fwd(rel_bias, onehot, name):
    n = onehot.shape[1]
    tn = _tile(n, 8192, LANE)

    def body(r_ref, o_ref, out_ref):
        out_ref[...] = lax.dot_general(r_ref[...], o_ref[...], (_DIMS["tn"], ((), ())),
                                       precision=lax.Precision.HIGHEST, preferred_element_type=F32)

    return pl.pallas_call(
        body, name=name, grid=(n // tn,),
        in_specs=[pl.BlockSpec(rel_bias.shape, lambda j: (0, 0)), pl.BlockSpec((N_BUCKETS, tn), lambda j: (0, j))],
        out_specs=pl.BlockSpec((N_Q_HEADS, tn), lambda j: (0, j)),
        out_shape=jax.ShapeDtypeStruct((N_Q_HEADS, n), F32),
        compiler_params=_params(("parallel",)),
    )(rel_bias, onehot)


def _rel_bias_bwd(dbias, onehot, name):
    def body(d_ref, o_ref, out_ref):
        out_ref[...] = lax.dot_general(d_ref[...], o_ref[...], (_DIMS["nt"], ((), ())),
                                       precision=lax.Precision.HIGHEST, preferred_element_type=F32)

    full = pl.BlockSpec(dbias.shape, lambda: (0, 0))
    return pl.pallas_call(
        body, name=name, in_specs=[full, full],
        out_specs=pl.BlockSpec((N_Q_HEADS, N_BUCKETS), lambda: (0, 0)),
        out_shape=jax.ShapeDtypeStruct((N_Q_HEADS, N_BUCKETS), F32),
        compiler_params=_params(()),
    )(dbias, onehot)


def _part_specs(gparts, part_index, tr, cc):
    blk = pl.BlockSpec((tr, cc), lambda i, *_: (i, 0))
    specs = []
    for i, _ in enumerate(gparts):
        if i in part_index:
            ix = part_index[i]
            if isinstance(ix, int):
                specs.append(pl.BlockSpec((None, tr, cc), functools.partial(lambda i, *_, ix: (ix, i, 0), ix=ix)))
            else:
                specs.append(pl.BlockSpec((None, tr, cc), lambda i, ix_ref: (ix_ref[0], i, 0)))
        else:
            specs.append(blk)
    return blk, specs


def _sum_parts(gparts, part_index, name):
    r, cc = gparts[0].shape[-2:]
    tr = _tile(r, 256)
    dyn = [i for i, ix in part_index.items() if not isinstance(ix, int)]
    assert len(dyn) <= 1

    def body(*refs):
        refs = refs[len(dyn):]
        g = refs[0][...].astype(F32)
        for gr in refs[1:-1]:
            g = g + gr[...].astype(F32)
        refs[-1][...] = g

    blk, specs = _part_specs(gparts, part_index, tr, cc)
    grid_spec = pltpu.PrefetchScalarGridSpec(num_scalar_prefetch=len(dyn), grid=(r // tr,), in_specs=specs, out_specs=blk)
    return pl.pallas_call(
        body, name=name, grid_spec=grid_spec, out_shape=jax.ShapeDtypeStruct((r, cc), F32),
        compiler_params=_params(("parallel",)),
    )(*[part_index[i] for i in dyn], *gparts)


def _adamw(w, m, v, gparts, name, part_index=None):
    r, cc = w.shape
    tr = _tile(r, 128)
    np_ = len(gparts)
    part_index = part_index or {}
    dyn = [i for i, ix in part_index.items() if not isinstance(ix, int)]
    assert len(dyn) <= 1
    bc1 = 1.0 / (1.0 - ADAM_B1 ** ADAM_STEP)
    bc2 = 1.0 / (1.0 - ADAM_B2 ** ADAM_STEP)

    def body(*refs):
        refs = refs[len(dyn):]
        w_ref, m_ref, v_ref = refs[:3]
        g_refs = refs[3:3 + np_]
        g_out, d_out, m_out, v_out = refs[3 + np_:]
        g = g_refs[0][...].astype(F32)
        for gr in g_refs[1:]:
            g = g + gr[...].astype(F32)
        mn = ADAM_B1 * m_ref[...] + (1.0 - ADAM_B1) * g
        vn = ADAM_B2 * v_ref[...] + (1.0 - ADAM_B2) * (g * g)
        g_out[...] = g
        m_out[...] = mn
        v_out[...] = vn
        d_out[...] = -ADAM_LR * ((mn * bc1) / (jnp.sqrt(vn * bc2) + ADAM_EPS) + ADAM_WD * w_ref[...])

    blk, g_specs = _part_specs(gparts, part_index, tr, cc)
    grid_spec = pltpu.PrefetchScalarGridSpec(
        num_scalar_prefetch=len(dyn), grid=(r // tr,),
        in_specs=[blk, blk, blk] + g_specs, out_specs=[blk] * 4)
    return pl.pallas_call(
        body, name=name, grid_spec=grid_spec,
        out_shape=[jax.ShapeDtypeStruct((r, cc), F32)] * 4,
        compiler_params=_params(("parallel",)),
    )(*[part_index[i] for i in dyn], w, m, v, *gparts)


def _row(vec):
    return vec.reshape(1, -1)


def kernel(x, c, w_ada, b_ada, norm1_g, w_in, conv_w, conv_b, lru_wa, lru_ba, lru_wx, lru_bx, lru_lambda, w_lru_out, w_attn_out, attn_sinks, rel_bias, w_out, norm2_g, w_ff1, w_ff2, final_g, loss_target, m_w_ada, m_b_ada, m_norm1_g, m_w_in, m_conv_w, m_conv_b, m_lru_wa, m_lru_ba, m_lru_wx, m_lru_bx, m_lru_lambda, m_w_lru_out, m_w_attn_out, m_attn_sinks, m_rel_bias, m_w_out, m_norm2_g, m_w_ff1, m_w_ff2, m_final_g, v_w_ada, v_b_ada, v_norm1_g, v_w_in, v_conv_w, v_conv_b, v_lru_wa, v_lru_ba, v_lru_wx, v_lru_bx, v_lru_lambda, v_w_lru_out, v_w_attn_out, v_attn_sinks, v_rel_bias, v_w_out, v_norm2_g, v_w_ff1, v_w_ff2, v_final_g):
    dd = D_MODEL
    t = x.shape[1]
    ix, iy, ic = lax.axis_index("x"), lax.axis_index("y"), lax.axis_index("c")
    me = 4 * ix + 2 * iy + ic
    chip = 2 * ix + iy
    c_idx = jnp.reshape(ic, (1,)).astype(jnp.int32)
    chip_idx = jnp.reshape(chip, (1,)).astype(jnp.int32)

    xs = x[0]
    target = loss_target[0]
    ncol_ada = w_ada.shape[2]
    cshard = conv_w.shape[2]

    shards = [jnp.transpose(w_in[0]).astype(BF16), w_lru_out[0].astype(BF16), w_attn_out[0].astype(BF16),
              w_out[0].astype(BF16), jnp.transpose(w_ff1[0]).astype(BF16), w_ff2[0].astype(BF16)]
    s_in, s_lo, s_ao, s_o, s_f1, s_f2 = shards

    def full(g):
        return g.reshape(N_DEV * g.shape[1], dd)

    def stack(g):
        return g.reshape(N_DEV, g.shape[0] // N_DEV, dd)

    (w_in_pair,) = _run_job(_pair_job([s_in]), "ag_w_in_pair")
    w_in_pair = lax.dynamic_update_slice_in_dim(w_in_pair, s_in[None], ic, axis=0)

    pack0 = jnp.zeros((8, dd), F32).at[0:1, :].set(c).at[1:1 + CONV_WIDTH, 0:cshard].set(conv_w[0])
    (g0,) = _all_gather([pack0], "ag_cond")
    c_all = g0[:, 0, :]
    conv_w_full = jnp.transpose(g0[:, 1:1 + CONV_WIDTH, 0:cshard], (1, 0, 2)).reshape(CONV_WIDTH, dd)
    b_cols = lax.dynamic_slice(b_ada, (0, me * ncol_ada), (1, ncol_ada))
    mod_cols = _ada_fwd(c_all, w_ada[0], b_cols, "ada_fwd")
    (g1,) = _all_gather([mod_cols], "ag_mod")
    mod_mine = lax.dynamic_index_in_dim(g1, me, axis=1, keepdims=False).reshape(6, dd)
    mod = jnp.concatenate([mod_mine, jnp.zeros((2, dd), F32)], axis=0)

    bucket = _bucket_table()
    onehot = jnp.asarray((bucket.reshape(1, -1) == np.arange(N_BUCKETS)[:, None]).astype(np.float32))
    bias = _rel_bias_fwd(rel_bias, onehot, "rel_bias_fwd").reshape(N_Q_HEADS, BLOCK, 2 * BLOCK)
    sinks = attn_sinks.reshape(N_Q_HEADS)
    lp = jnp.concatenate([conv_b, lru_ba, lru_bx, lru_lambda, conv_w_full], axis=0)
    wa, wx = lru_wa[0], lru_wx[0]

    tm = _tile(t, 1024)
    tn = 512
    h = _norm_mod(xs, norm1_g, mod, 0, "norm1")
    tn2 = 1024
    chunk_w = 2 * s_in.shape[0]

    def ids(*ks):
        return jnp.stack([jnp.asarray(k, jnp.int32) for k in ks])

    tn_in = _tile(chunk_w, 1024, LANE)
    (proj,), ((g_near,),) = _mm(
        h, w_in_pair.reshape(chunk_w, dd), "nt", tm=tm, tn=tn_in, outs=[BF16], name="in_proj_own",
        chunks=(ids(chip), chunk_w, ids(0)), n_total=IN_WIDTH, jobs=[_gather_job([w_in_pair], (0.7,), "near")])
    near_ids = ids(2 * (1 - ix) + iy, 2 * ix + 1 - iy)
    (proj,), ((g_far,),) = _mm(
        h, full(g_near), "nt", tm=tm, tn=tn_in, outs=[BF16], name="in_proj_near",
        chunks=(near_ids, chunk_w, near_ids), into=proj, jobs=[_gather_job([g_near], (0.6,), "far")])
    far_chip = 2 * (1 - ix) + 1 - iy
    (proj,), ((g_o,),) = _mm(
        h, full(g_far), "nt", tm=tm, tn=tn_in, outs=[BF16], name="in_proj_far",
        chunks=(ids(far_chip), chunk_w, ids(far_chip)), into=proj, jobs=[_gather_job([s_o], (0.5, 0.8))])
    far_slots = lax.dynamic_slice_in_dim(g_far, 2 * far_chip, 2, axis=0)
    win_t = lax.dynamic_update_slice_in_dim(g_near, far_slots, 2 * far_chip, axis=0)
    win_t = full(lax.dynamic_update_slice_in_dim(win_t, w_in_pair, 2 * chip, axis=0))
    (g_o,) = _fill_own([g_o], [s_o])
    (rec, ya), ((g_f1,),) = _lru_fwd(proj, lp, wa, wx, "lru_fwd", jobs=[_gather_job([s_f1], (0.6, 0.9))])
    att, ((g_lo,), (g_ao,)) = _attn_fwd(proj, bias, sinks, "attn_fwd",
                                        jobs=[_gather_job([s_lo], (0.3, 0.45)), _gather_job([s_ao], (0.75, 0.9))])
    g_lo, g_ao, g_f1 = _fill_own([g_lo, g_ao, g_f1], [s_lo, s_ao, s_f1])
    wlo, wao, wo, wf1_t = full(g_lo), full(g_ao), full(g_o), full(g_f1)
    (y_a,) = _mm(ya, wlo, "nn", tm=tm, tn=tn2, outs=[BF16], name="lru_out")

    def merge_epi(acc, ya_t, ga_t, gb_t):
        return acc, _sigmoid(ga_t.astype(F32)) * ya_t.astype(F32) + _sigmoid(gb_t.astype(F32)) * acc

    y_b, merged = _mm(att, wao, "nn", tm=tm, tn=tn, outs=[BF16, BF16], epi=merge_epi,
                      tiles=[(y_a, 0), (proj, OFF_GA // tn), (proj, OFF_GB // tn)], name="attn_out_merge")

    def resid_epi(row):
        def epi(acc, x_t, mod_t):
            return acc, x_t + mod_t[row:row + 1, :] * acc
        return epi

    o1, x1 = _mm(merged, wo, "nn", tm=tm, tn=tn2, outs=[BF16, F32], epi=resid_epi(2),
                 tiles=[(xs, 0)], rows=[mod], name="out_proj")
    h2 = _norm_mod(x1, norm2_g, mod, 3, "norm2")

    def relu2_epi(acc):
        rl = jnp.maximum(acc, 0.0)
        return acc, rl * rl

    (f_pre, ff), ((g_f2,),) = _mm(h2, wf1_t, "nt", tm=tm, tn=1024, outs=[BF16, BF16], epi=relu2_epi, name="ff1",
                                  jobs=[_gather_job([s_f2], (0.6, 0.9))])
    wf2 = full(_fill_own([g_f2], [s_f2])[0])
    tmh = _tile(t, 512)
    o2, x2 = _mm(ff, wf2, "nn", tm=tmh, tn=tn, outs=[BF16, F32], epi=resid_epi(5),
                 tiles=[(x1, 0)], rows=[mod], name="ff2")

    loss_blk, dx2, do2, st_f = _final_loss(x2, o2, _row(final_g), target, mod, "final_loss")
    loss = lax.psum(loss_blk[0, 0], ("x", "y", "c"))

    def drelu2_epi(acc, f_t):
        return (acc * (2.0 * jnp.maximum(f_t.astype(F32), 0.0)),)

    tmw, tnw, tkw = 512, dd, _tile(t, 2048)
    (df,) = _mm(do2, wf2, "nt", tm=tm, tn=1024, outs=[BF16], epi=drelu2_epi, tiles=[(f_pre, 0)], name="ff2_dx")
    (dwf2,) = _mm(ff, do2, "tn", tm=tmw, tn=tnw, tk=tkw, outs=[BF16], name="ff2_dw")
    (dh2,), ((sib_f2,),) = _mm(df, wf1_t, "nn", tm=tmh, tn=tn, outs=[F32], name="ff1_dx",
                              jobs=[_sibling_job([stack(dwf2)])])
    pair_f2 = _pair_add(stack(dwf2), sib_f2, c_idx, "rs_pair_add_w_ff2")
    (dwf1_t,), ((chips_f2,),) = _mm(df, h2, "tn", tm=tmw, tn=tnw, tk=tkw, outs=[BF16], name="ff1_dw",
                                    jobs=[_chips_job([pair_f2])])
    (dx1, st_2, do1), ((sib_f1,),) = _norm_mod_bwd(x1, norm2_g, mod, 3, dh2, dx2, (o1, 2), "norm2_bwd",
                                                   jobs=[_sibling_job([stack(dwf1_t)])])
    pair_f1 = _pair_add(stack(dwf1_t), sib_f1, c_idx, "rs_pair_add_w_ff1")

    def dmerge_epi(acc, ga_t, gb_t, ya_t, yb_t):
        sa, sb = _sigmoid(ga_t.astype(F32)), _sigmoid(gb_t.astype(F32))
        return (acc * sa, acc * sb, acc * ya_t.astype(F32) * sa * (1.0 - sa), acc * yb_t.astype(F32) * sb * (1.0 - sb))

    tmh = _tile(t, 512)
    dy_a, dy_b, dga, dgb = _mm(do1, wo, "nt", tm=tmh, tn=tn, outs=[BF16] * 4, epi=dmerge_epi,
                               tiles=[(proj, OFF_GA // tn), (proj, OFF_GB // tn), (y_a, 0), (y_b, 0)], name="out_proj_dx")
    (dwo,) = _mm(merged, do1, "tn", tm=tmw, tn=tnw, tk=tkw, outs=[BF16], name="out_proj_dw")
    (datt,) = _mm(dy_b, wao, "nt", tm=tm, tn=tn2, outs=[BF16], name="attn_out_dx")
    (dwao,) = _mm(att, dy_b, "tn", tm=tmw, tn=tnw, tk=tkw, outs=[BF16], name="attn_out_dw")
    (dya,) = _mm(dy_a, wlo, "nt", tm=tm, tn=tn2, outs=[BF16], name="lru_out_dx")
    (dwlo,) = _mm(ya, dy_a, "tn", tm=tmw, tn=tnw, tk=tkw, outs=[BF16], name="lru_out_dw")
    (dq, dk_pad, dv_pad, dbias, dsinks), (sib_3, (chips_f1,)) = _attn_bwd(
        proj, datt, bias, sinks, "attn_bwd",
        jobs=[_sibling_job([stack(dwlo), stack(dwao), stack(dwo)]), _chips_job([pair_f1])])
    pair_lo, pair_ao, pair_o = [_pair_add(stack(g_), s_, c_idx, "rs_pair_add_" + nm_)
                                for g_, s_, nm_ in zip((dwlo, dwao, dwo), sib_3, ("w_lru_out", "w_attn_out", "w_out"))]
    (dlx, dlg, st_l, dwa, dwx), ((chips_lo, chips_ao, chips_o),) = _lru_bwd(
        proj, rec, dya, lp, wa, wx, "lru_bwd", jobs=[_chips_job([pair_lo, pair_ao, pair_o])])
    dproj = jnp.concatenate([dlx, dlg, dq, dk_pad[BLOCK:].astype(BF16), dv_pad[BLOCK:].astype(BF16), dga, dgb], axis=1)
    lru_mats = jnp.concatenate([dwa.reshape(-1, dd), dwx.reshape(-1, dd)], axis=0)
    (dwin_t,), ((mats_all,),) = _mm(dproj, h, "tn", tm=384, tn=tnw, tk=tkw, outs=[BF16], name="in_proj_dw",
                                   jobs=[_gather_job([lru_mats], (0.3, 0.6))])
    (mats_all,) = _fill_own([mats_all], [lru_mats])
    (sib_in,) = _exchange_sibling([stack(dwin_t)], "rs_sibling_w_in")
    pair_in = _pair_add(stack(dwin_t), sib_in, c_idx, "rs_pair_add_w_in")
    (dh,), ((chips_in,),) = _mm(dproj, win_t, "nn", tm=tmh, tn=tn, tk=IN_WIDTH // 2, outs=[F32], name="in_proj_dx",
                              jobs=[_chips_job([pair_in])])
    grad_x, st_1 = _norm_mod_bwd(xs, norm1_g, mod, 0, dh, dx1, None, "norm1_bwd")

    drel = jnp.transpose(_rel_bias_bwd(dbias.reshape(N_Q_HEADS, -1), onehot, "rel_bias_bwd"))

    misc = jnp.concatenate([drel.reshape(1, -1), dsinks[0:1, 0:N_Q_HEADS],
                            jnp.zeros((1, dd - N_BUCKETS * N_Q_HEADS - N_Q_HEADS), F32)], axis=1)
    zero_row = jnp.zeros((1, dd), F32)
    small = jnp.concatenate([
        st_1[0:2], st_2[3:4], st_2[0:2], st_f[1:2], zero_row, zero_row,
        st_1[2:3], st_l[0:1], st_l[1:2], st_l[2:3], st_l[3:4], st_2[2:3], st_f[0:1], misc,
        st_l[4:8], jnp.zeros((4, dd), F32)], axis=0)
    (small_all,) = _all_gather([small], "ag_small")

    def pack_small(b_, n1, cb, ba, bx, lam, n2, fg, rb, sk):
        misc_ = jnp.concatenate([rb.reshape(1, -1), sk.reshape(1, -1),
                                 jnp.zeros((1, dd - N_BUCKETS * N_Q_HEADS - N_Q_HEADS), F32)], axis=1)
        return jnp.concatenate([b_.reshape(6, dd), jnp.zeros((2, dd), F32), n1, cb, ba, bx, lam, n2, _row(fg), misc_,
                                jnp.zeros((8, dd), F32)], axis=0)

    def pack_mats(wa_, wx_):
        return jnp.concatenate([wa_.reshape(-1, dd), wx_.reshape(-1, dd)], axis=0)

    every = {i: i for i in range(N_DEV)}
    w_s = pack_small(b_ada, norm1_g, conv_b, lru_ba, lru_bx, lru_lambda, norm2_g, final_g, rel_bias, attn_sinks)
    m_s = pack_small(m_b_ada, m_norm1_g, m_conv_b, m_lru_ba, m_lru_bx, m_lru_lambda, m_norm2_g, m_final_g, m_rel_bias, m_attn_sinks)
    v_s = pack_small(v_b_ada, v_norm1_g, v_conv_b, v_lru_ba, v_lru_bx, v_lru_lambda, v_norm2_g, v_final_g, v_rel_bias, v_attn_sinks)
    small_res = _adamw(w_s, m_s, v_s, [small_all] * N_DEV, "adamw_small", part_index=every)
    mats_res = _adamw(pack_mats(lru_wa, lru_wx), pack_mats(m_lru_wa, m_lru_wx), pack_mats(v_lru_wa, v_lru_wx),
                      [mats_all] * N_DEV, "adamw_lru_mats", part_index=every)

    def unpack_small(s, mt):
        nb_ = N_BUCKETS * N_Q_HEADS
        half = mt.shape[0] // 2
        return dict(
            b_ada=s[0:6].reshape(1, 6 * dd), norm1_g=s[8:9], conv_b=s[9:10], lru_ba=s[10:11], lru_bx=s[11:12],
            lru_lambda=s[12:13], norm2_g=s[13:14], final_g=s[14], rel_bias=s[15, 0:nb_].reshape(N_BUCKETS, N_Q_HEADS),
            attn_sinks=s[15:16, nb_:nb_ + N_Q_HEADS],
            lru_wa=mt[:half].reshape(1, LRU_BLOCKS, LRU_BLOCK_W, LRU_BLOCK_W),
            lru_wx=mt[half:].reshape(1, LRU_BLOCKS, LRU_BLOCK_W, LRU_BLOCK_W))

    res = {k: [None] * 4 for k in ("w_ada", "w_in", "conv_w", "w_lru_out", "w_attn_out", "w_out", "w_ff1", "w_ff2")}
    for q_, (s_, mt_) in enumerate(zip(small_res, mats_res)):
        for k_, val in unpack_small(s_, mt_).items():
            res.setdefault(k_, [None] * 4)[q_] = val

    g_conv = lax.dynamic_slice(small_res[0][16:16 + CONV_WIDTH], (0, me * cshard), (CONV_WIDTH, cshard))
    conv_res = _adamw(conv_w[0], m_conv_w[0], v_conv_w[0], [g_conv], "adamw_conv")
    res["conv_w"] = [r_[None] for r_ in conv_res]

    dmod_all = small_all[:, 0:6, :].reshape(N_DEV, 6 * dd)
    dmod_cols = lax.dynamic_slice(dmod_all, (0, me * ncol_ada), (N_DEV, ncol_ada))
    g_ada = _ada_bwd(jnp.transpose(c_all), dmod_cols, "ada_bwd")
    ada_res = _adamw(w_ada[0], m_w_ada[0], v_w_ada[0], [g_ada], "adamw_ada")
    res["w_ada"] = [r_[None] for r_ in ada_res]

    pair = [pair_in, pair_lo, pair_ao, pair_o, pair_f1, pair_f2]
    from_chips = [chips_in, chips_lo, chips_ao, chips_o, chips_f1, chips_f2]

    def summed(i):
        return [pair[i], from_chips[i], from_chips[i], from_chips[i]], {0: chip_idx, 1: 0, 2: 1, 3: 2}

    def sum_only(i, name):
        parts, index = summed(i)
        return _sum_parts(parts, index, name)

    g_in = jnp.transpose(sum_only(0, "sum_w_in"))
    res["w_in"] = [r_[None] for r_ in _adamw(w_in[0], m_w_in[0], v_w_in[0], [g_in], "adamw_w_in")]
    g_ff1 = jnp.transpose(sum_only(4, "sum_w_ff1"))
    res["w_ff1"] = [r_[None] for r_ in _adamw(w_ff1[0], m_w_ff1[0], v_w_ff1[0], [g_ff1], "adamw_w_ff1")]
    for i, (nm, w_, m_, v_) in {1: ("w_lru_out", w_lru_out, m_w_lru_out, v_w_lru_out),
                                2: ("w_attn_out", w_attn_out, m_w_attn_out, v_w_attn_out),
                                3: ("w_out", w_out, m_w_out, v_w_out),
                                5: ("w_ff2", w_ff2, m_w_ff2, v_w_ff2)}.items():
        parts, index = summed(i)
        res[nm] = [r_[None] for r_ in _adamw(w_[0], m_[0], v_[0], parts, "adamw_" + nm, part_index=index)]

    order = ["w_ada", "b_ada", "norm1_g", "w_in", "conv_w", "conv_b", "lru_wa", "lru_ba", "lru_wx", "lru_bx",
             "lru_lambda", "w_lru_out", "w_attn_out", "attn_sinks", "rel_bias", "w_out", "norm2_g", "w_ff1",
             "w_ff2", "final_g"]
    out = [loss, grad_x[None]]
    for q_ in range(4):
        out += [res[k_][q_] for k_ in order]
    return tuple(out)
```

```python
import functools
import math

import numpy as np
import jax
import jax.numpy as jnp
from jax import lax
from jax.experimental import pallas as pl
from jax.experimental.pallas import tpu as pltpu

F32 = jnp.float32
BF16 = jnp.bfloat16
MESH = pl.DeviceIdType.MESH

D_MODEL = 2048
N_Q_HEADS = 32
N_KV_HEADS = 4
GROUP = N_Q_HEADS // N_KV_HEADS
HEAD_DIM = 64
KV_WIDTH = N_KV_HEADS * HEAD_DIM
BLOCK = 128
NEG_INF = -1e30
N_BUCKETS = 32
MAX_DISTANCE = 128
LRU_BLOCKS = 16
LRU_BLOCK_W = 128
CONV_WIDTH = 4
LRU_C = 8.0
D_FF = 4 * D_MODEL
EPS = 1e-6
IN_WIDTH = 5 * D_MODEL + 2 * KV_WIDTH
OFF_LX, OFF_LG, OFF_Q, OFF_K, OFF_V, OFF_GA, OFF_GB = 0, 2048, 4096, 6144, 6400, 6656, 8704

ADAM_LR, ADAM_B1, ADAM_B2, ADAM_EPS, ADAM_WD, ADAM_STEP = 0.001, 0.9, 0.999, 1e-08, 0.01, 10

N_DEV = 8
VMEM_LIMIT_BYTES = 48 * 1024 * 1024
LANE = 128
LRU_CW = 512


def _params(sem, **kw):
    return pltpu.CompilerParams(dimension_semantics=sem, vmem_limit_bytes=VMEM_LIMIT_BYTES, **kw)


def _tile(n, pref, mult=8):
    if n <= pref:
        return n
    t = (pref // mult) * mult
    while t >= mult:
        if n % t == 0:
            return t
        t -= mult
    return n


def _sigmoid(x):
    return 0.5 * jnp.tanh(0.5 * x) + 0.5


def _gelu_tanh(x):
    k = math.sqrt(2.0 / math.pi)
    t = jnp.tanh(k * (x + 0.044715 * x * x * x))
    return 0.5 * x * (1.0 + t), t


def _gelu_tanh_grad(x, t):
    k = math.sqrt(2.0 / math.pi)
    return 0.5 * (1.0 + t) + 0.5 * x * (1.0 - t * t) * k * (1.0 + 3.0 * 0.044715 * x * x)


def _log1p(e):
    u = 1.0 + e
    return jnp.where(u == 1.0, e, jnp.log(u) * (e / jnp.where(u == 1.0, 1.0, u - 1.0)))


def _softplus(x):
    return jnp.maximum(x, 0.0) + _log1p(jnp.exp(-jnp.abs(x)))


def _neg_expm1(x):
    series = -x * (1.0 + x * (0.5 + x * (1.0 / 6.0 + x * (1.0 / 24.0 + x * (1.0 / 120.0)))))
    return jnp.where(x > -0.1, series, 1.0 - jnp.exp(x))


def _my_place():
    return lax.axis_index("x"), lax.axis_index("y"), lax.axis_index("c")


def _all_gather(arrs, name):
    return _fill_own(_run_job(_gather_job(arrs), name), arrs)


def _fill_own(stacks, shards):
    me = 4 * lax.axis_index("x") + 2 * lax.axis_index("y") + lax.axis_index("c")
    return [lax.dynamic_update_slice_in_dim(g, s[None], me, axis=0) for g, s in zip(stacks, shards)]


class _Job:
    def __init__(self, ins, outs, sems, start, finish, forwards=(), forward_at=()):
        self.ins, self.outs, self.sems = list(ins), list(outs), list(sems)
        self.start, self.finish, self.forwards, self.forward_at = start, finish, list(forwards), list(forward_at)
        assert len(self.forwards) <= len(self.forward_at)


def _gather_job(arrs, forward_at=(0.6, 0.85), part="all"):
    n = len(arrs)
    whole, near, far = part == "all", part == "near", part == "far"

    def copies(ins, outs, sems):
        send_sems, recv_sems, local_sems = sems
        x, y, c = _my_place()
        sib = (x, y, 1 - c)
        xn, yn, dg = (1 - x, y), (x, 1 - y), (1 - x, 1 - y)

        def slot(px, py, pc):
            return 4 * px + 2 * py + pc

        def cp(a, k, src, dst_slot, to):
            return pltpu.make_async_remote_copy(
                src_ref=src, dst_ref=outs[a].at[dst_slot], send_sem=send_sems.at[a, k], recv_sem=recv_sems.at[a, k],
                device_id=to, device_id_type=MESH)

        def arrival(a, k, dst_slot):
            return cp(a, k, outs[a].at[dst_slot], dst_slot, sib)

        me_slot = slot(x, y, c)
        sends, local, relay, passes, arrive = {}, [], {}, {}, {}
        for a in range(n):
            if whole:
                own = ins[a]
                sends[a, 0] = cp(a, 0, own, me_slot, sib)
            if near:
                own = ins[a].at[c]
            if whole or near:
                sends[a, 1] = cp(a, 1, own, me_slot, (*xn, c))
                sends[a, 2] = cp(a, 2, own, me_slot, (*yn, c))
                passes[a, 4] = cp(a, 4, outs[a].at[slot(*xn, c)], slot(*xn, c), sib)
                passes[a, 5] = cp(a, 5, outs[a].at[slot(*yn, c)], slot(*yn, c), sib)
            if whole or far:
                source = outs[a] if whole else ins[a]
                relayed = c * slot(*yn, c) + (1 - c) * slot(*xn, c)
                relay[a, 3] = cp(a, 3, source.at[relayed], relayed, (x + c * (1 - 2 * x), y + (1 - c) * (1 - 2 * y), c))
                passes[a, 6] = cp(a, 6, outs[a].at[slot(*dg, c)], slot(*dg, c), sib)
            arrive[a, 0] = arrival(a, 0, slot(x, y, 1 - c))
            arrive[a, 1] = arrival(a, 1, slot(*xn, c))
            arrive[a, 2] = arrival(a, 2, slot(*yn, c))
            arrive[a, 3] = arrival(a, 3, slot(*dg, c))
            arrive[a, 4] = arrival(a, 4, slot(*xn, 1 - c))
            arrive[a, 5] = arrival(a, 5, slot(*yn, 1 - c))
            arrive[a, 6] = arrival(a, 6, slot(*dg, 1 - c))
        return sends, local, relay, passes, arrive

    def pick(d, ks):
        return [d[a, k] for a in range(n) for k in ks if (a, k) in d]

    def start(ins, outs, sems):
        sends, local, relay, _, _ = copies(ins, outs, sems)
        for cp in local + pick(sends, (0, 1, 2)) + (pick(relay, (3,)) if far else []):
            cp.start()

    def forward_neighbours(ins, outs, sems):
        _, _, relay, passes, arrive = copies(ins, outs, sems)
        for cp in pick(arrive, (1, 2)):
            cp.wait_recv()
        for cp in pick(relay, (3,)) + pick(passes, (4, 5)):
            cp.start()

    def forward_diagonal(ins, outs, sems):
        _, _, _, passes, arrive = copies(ins, outs, sems)
        for cp in pick(arrive, (3,)):
            cp.wait_recv()
        for cp in pick(passes, (6,)):
            cp.start()

    def finish(ins, outs, sems):
        sends, local, relay, passes, arrive = copies(ins, outs, sems)
        for cp in pick(arrive, (6,) if far else (0, 4, 5, 6) if whole else (4, 5)):
            cp.wait_recv()
        for cp in pick(sends, (0, 1, 2)) + pick(relay, (3,)) + pick(passes, (4, 5, 6)):
            cp.wait_send()
        for cp in local:
            cp.wait()

    shapes = [a.shape[-2:] for a in arrs]
    forwards = [forward_diagonal] if far else [forward_neighbours] if near else [forward_neighbours, forward_diagonal]
    return _Job(arrs, [jax.ShapeDtypeStruct((N_DEV,) + s, a.dtype) for s, a in zip(shapes, arrs)],
                [pltpu.SemaphoreType.DMA((n, 7)), pltpu.SemaphoreType.DMA((n, 7)), pltpu.SemaphoreType.DMA((n,))],
                start, finish, forwards, forward_at)


def _pair_job(arrs):
    n = len(arrs)

    def copies(ins, outs, sems):
        send_sems, recv_sems, local_sems = sems
        x, y, c = _my_place()
        remote = [pltpu.make_async_remote_copy(
            src_ref=ins[a], dst_ref=outs[a].at[c], send_sem=send_sems.at[a], recv_sem=recv_sems.at[a],
            device_id=(x, y, 1 - c), device_id_type=MESH) for a in range(n)]
        local = []
        arrive = [pltpu.make_async_remote_copy(
            src_ref=ins[a], dst_ref=outs[a].at[1 - c], send_sem=send_sems.at[a], recv_sem=recv_sems.at[a],
            device_id=(x, y, 1 - c), device_id_type=MESH) for a in range(n)]
        return remote, local, arrive

    def start(ins, outs, sems):
        remote, local, _ = copies(ins, outs, sems)
        for cp in remote + local:
            cp.start()

    def finish(ins, outs, sems):
        remote, local, arrive = copies(ins, outs, sems)
        for cp in arrive:
            cp.wait_recv()
        for cp in remote:
            cp.wait_send()
        for cp in local:
            cp.wait()

    return _Job(arrs, [jax.ShapeDtypeStruct((2,) + a.shape, a.dtype) for a in arrs],
                [pltpu.SemaphoreType.DMA((n,)), pltpu.SemaphoreType.DMA((n,)), pltpu.SemaphoreType.DMA((n,))],
                start, finish)


def _run_job(job, name):
    ni, no = len(job.ins), len(job.outs)

    def body(*refs):
        ins, outs, sems = refs[:ni], refs[ni:ni + no], refs[ni + no:]
        job.start(ins, outs, sems)
        for fwd in job.forwards:
            fwd(ins, outs, sems)
        job.finish(ins, outs, sems)

    any_spec = pl.BlockSpec(memory_space=pl.ANY)
    return pl.pallas_call(
        body, name=name, out_shape=job.outs, in_specs=[any_spec] * ni, out_specs=[any_spec] * no,
        scratch_shapes=job.sems,
    )(*job.ins)


def _call(body, *, name, grid, in_specs, out_specs, out_shape, scratch_shapes=(), sem, args, jobs=(),
          prefetch=(), aliases=None):
    in_specs, out_specs, out_shape, scratch_shapes = list(in_specs), list(out_specs), list(out_shape), list(scratch_shapes)
    prefetch = list(prefetch)
    if not jobs and not prefetch and not aliases:
        res = pl.pallas_call(body, name=name, grid=grid, in_specs=in_specs, out_specs=out_specs, out_shape=out_shape,
                             scratch_shapes=scratch_shapes, compiler_params=_params(sem))(*args)
        return list(res), []
    n_in, n_out, n_scr = len(in_specs), len(out_specs), len(scratch_shapes)
    j_in = [len(j.ins) for j in jobs]
    j_out = [len(j.outs) for j in jobs]
    j_sem = [len(j.sems) for j in jobs]
    total = int(np.prod(grid))
    any_spec = pl.BlockSpec(memory_space=pl.ANY)

    def wrapped(*refs):
        pos = [0]

        def take(k):
            part = refs[pos[0]:pos[0] + k]
            pos[0] += k
            return part

        take(len(prefetch))
        ins = take(n_in)
        jins = [take(k) for k in j_in]
        outs = take(n_out)
        jouts = [take(k) for k in j_out]
        scr = take(n_scr)
        jsems = [take(k) for k in j_sem]
        step = pl.program_id(0)
        for d in range(1, len(grid)):
            step = step * grid[d] + pl.program_id(d)
        for j, job in enumerate(jobs):
            pl.when(step == 0)(functools.partial(job.start, jins[j], jouts[j], jsems[j]))
            for fwd, frac in zip(job.forwards, job.forward_at):
                at = min(int(total * frac), total - 1)
                pl.when(step == at)(functools.partial(fwd, jins[j], jouts[j], jsems[j]))
        body(*ins, *outs, *scr)
        for j, job in enumerate(jobs):
            pl.when(step == total - 1)(functools.partial(job.finish, jins[j], jouts[j], jsems[j]))

    grid_spec = pltpu.PrefetchScalarGridSpec(
        num_scalar_prefetch=len(prefetch), grid=grid,
        in_specs=in_specs + [any_spec] * sum(j_in),
        out_specs=out_specs + [any_spec] * sum(j_out),
        scratch_shapes=scratch_shapes + [s for j in jobs for s in j.sems])
    res = pl.pallas_call(
        wrapped, name=name, grid_spec=grid_spec,
        out_shape=out_shape + [o for j in jobs for o in j.outs],
        input_output_aliases={len(prefetch) + i: o for i, o in (aliases or {}).items()},
        compiler_params=_params(("arbitrary",) * len(grid)),
    )(*prefetch, *args, *[a for j in jobs for a in j.ins])
    res = list(res)
    own, rest = res[:n_out], res[n_out:]
    per_job = []
    for k in j_out:
        per_job.append(rest[:k])
        rest = rest[k:]
    return own, per_job


def _exchange_sibling(arrs, name):
    return _run_job(_sibling_job(arrs), name)


def _sibling_job(arrs):
    n = len(arrs)

    def copies(ins, outs, sems):
        send_sems, recv_sems = sems
        x, y, c = _my_place()
        return [pltpu.make_async_remote_copy(
            src_ref=ins[a].at[2 * k + 1 - c], dst_ref=outs[a].at[k],
            send_sem=send_sems.at[a, k], recv_sem=recv_sems.at[a, k],
            device_id=(x, y, 1 - c), device_id_type=MESH) for a in range(n) for k in range(4)]

    def start(ins, outs, sems):
        for cp in copies(ins, outs, sems):
            cp.start()

    def finish(ins, outs, sems):
        for cp in copies(ins, outs, sems):
            cp.wait()

    return _Job(arrs, [jax.ShapeDtypeStruct((4,) + a.shape[1:], a.dtype) for a in arrs],
                [pltpu.SemaphoreType.DMA((n, 4)), pltpu.SemaphoreType.DMA((n, 4))], start, finish)


def _chips_job(arrs):
    n = len(arrs)

    def copies(ins, outs, sems):
        send_sems, recv_sems = sems
        x, y, c = _my_place()
        chips = [(1 - x, y), (x, 1 - y), (1 - x, 1 - y)]
        return [pltpu.make_async_remote_copy(
            src_ref=ins[a].at[2 * px + py], dst_ref=outs[a].at[j],
            send_sem=send_sems.at[a, j], recv_sem=recv_sems.at[a, j],
            device_id=(px, py, c), device_id_type=MESH) for a in range(n) for j, (px, py) in enumerate(chips)]

    def start(ins, outs, sems):
        for cp in copies(ins, outs, sems):
            cp.start()

    def finish(ins, outs, sems):
        for cp in copies(ins, outs, sems):
            cp.wait()

    return _Job(arrs, [jax.ShapeDtypeStruct((3,) + a.shape[1:], a.dtype) for a in arrs],
                [pltpu.SemaphoreType.DMA((n, 3)), pltpu.SemaphoreType.DMA((n, 3))], start, finish)


def _pair_add(stack, from_sibling, c_idx, name):
    _, r, cc = stack.shape
    tr = _tile(r, 512)

    def body(c_ref, a_ref, b_ref, o_ref):
        o_ref[...] = (a_ref[...].astype(F32) + b_ref[...].astype(F32)).astype(o_ref.dtype)

    grid_spec = pltpu.PrefetchScalarGridSpec(
        num_scalar_prefetch=1, grid=(4, r // tr),
        in_specs=[pl.BlockSpec((None, tr, cc), lambda k, i, c_ref: (2 * k + c_ref[0], i, 0)),
                  pl.BlockSpec((None, tr, cc), lambda k, i, c_ref: (k, i, 0))],
        out_specs=pl.BlockSpec((None, tr, cc), lambda k, i, c_ref: (k, i, 0)))
    return pl.pallas_call(
        body, name=name, grid_spec=grid_spec,
        out_shape=jax.ShapeDtypeStruct((4, r, cc), BF16),
        compiler_params=_params(("parallel", "parallel")),
    )(c_idx, stack, from_sibling)


_DIMS = {"nn": ((1,), (0,)), "nt": ((1,), (1,)), "tn": ((0,), (0,))}


def _mm(a, b, mode, *, tm, tn, tk=None, outs, epi=None, tiles=(), rows=(), name, jobs=(), chunks=None, into=None,
        n_total=None):
    if mode == "tn":
        kk, m = a.shape
    else:
        m, kk = a.shape
    n_all = b.shape[0] if mode == "nt" else b.shape[1]
    nn = n_all if chunks is None else chunks[0].shape[0] * chunks[1]
    n_all = n_all if n_total is None else n_total
    tk = kk if tk is None else tk
    assert m % tm == 0 and nn % tn == 0 and kk % tk == 0, (name, a.shape, b.shape, tm, tn, tk)
    nk = kk // tk
    nt, nr, no = len(tiles), len(rows), len(outs)
    n_into = 0 if into is None else 1

    def body(*refs):
        a_ref, b_ref = refs[:2]
        tile_refs = refs[2:2 + nt]
        row_refs = refs[2 + nt:2 + nt + nr]
        out_refs = refs[2 + nt + nr + n_into:2 + nt + nr + n_into + no]
        part = lax.dot_general(a_ref[...], b_ref[...], (_DIMS[mode], ((), ())), preferred_element_type=F32)

        def finish(acc):
            if epi is None:
                res = (acc,)
            else:
                res = epi(acc, *[t[...] for t in tile_refs], *[r[...] for r in row_refs])
            for o_ref, val in zip(out_refs, res):
                o_ref[...] = val.astype(o_ref.dtype)

        if nk == 1:
            finish(part)
        else:
            acc_ref = refs[-1]
            k = pl.program_id(2)

            @pl.when(k == 0)
            def _():
                acc_ref[...] = part

            @pl.when(k > 0)
            def _():
                acc_ref[...] += part

            @pl.when(k == nk - 1)
            def _():
                finish(acc_ref[...])

    if chunks is None:
        col = b_row = lambda j, *_: j
        prefetch = []
    else:
        assert mode == "nt" and not tiles and not rows and chunks[1] % tn == 0
        per = chunks[1] // tn
        col = lambda j, ids, b_ids: ids[j // per] * per + j % per
        b_row = lambda j, ids, b_ids: b_ids[j // per] * per + j % per
        prefetch = [chunks[0], chunks[2]]
    if mode == "tn":
        a_spec = pl.BlockSpec((tk, tm), lambda i, j, k, *_: (k, i))
    else:
        a_spec = pl.BlockSpec((tm, tk), lambda i, j, k, *_: (i, k))
    if mode == "nt":
        b_spec = pl.BlockSpec((tn, tk), lambda i, j, k, *s: (b_row(j, *s), k))
    else:
        b_spec = pl.BlockSpec((tk, tn), lambda i, j, k, *_: (k, j))
    tile_specs = [pl.BlockSpec((tm, tn), functools.partial(lambda i, j, k, off: (i, j + off), off=off))
                  for _, off in tiles]
    row_specs = [pl.BlockSpec((r.shape[0], tn), lambda i, j, k: (0, j)) for r in rows]
    out_spec = pl.BlockSpec((tm, tn), lambda i, j, k, *s: (i, col(j, *s)))
    args = [a, b, *[t for t, _ in tiles], *rows]
    in_specs = [a_spec, b_spec] + tile_specs + row_specs
    aliases = None
    if into is not None:
        aliases = {len(args): 0}
        args.append(into)
        in_specs.append(pl.BlockSpec(memory_space=pl.ANY))
    res, jres = _call(
        body, name=name, grid=(m // tm, nn // tn, nk),
        in_specs=in_specs, out_specs=[out_spec] * no,
        out_shape=[jax.ShapeDtypeStruct((m, n_all), dt) for dt in outs],
        scratch_shapes=[pltpu.VMEM((tm, tn), F32)] if nk > 1 else [],
        sem=("parallel", "parallel", "arbitrary"),
        args=args, jobs=jobs, prefetch=prefetch, aliases=aliases)
    return (res, jres) if jobs else res


def _ada_fwd(c_all, w_ada, b_ada_cols, name):
    nb, dd = c_all.shape
    ncol = w_ada.shape[1]
    tn = _tile(ncol, 512, LANE)

    def body(c_ref, w_ref, b_ref, o_ref):
        cv = c_ref[...]
        act = (cv * _sigmoid(cv)).astype(BF16)
        o_ref[...] = jnp.dot(act, w_ref[...].astype(BF16), preferred_element_type=F32) + b_ref[...]

    return pl.pallas_call(
        body, name=name, grid=(ncol // tn,),
        in_specs=[pl.BlockSpec((nb, dd), lambda j: (0, 0)), pl.BlockSpec((dd, tn), lambda j: (0, j)),
                  pl.BlockSpec((1, tn), lambda j: (0, j))],
        out_specs=pl.BlockSpec((nb, tn), lambda j: (0, j)),
        out_shape=jax.ShapeDtypeStruct((nb, ncol), F32),
        compiler_params=_params(("parallel",)),
    )(c_all, w_ada, b_ada_cols)


def _ada_bwd(c_all_t, dmod_cols, name):
    dd, nb = c_all_t.shape
    ncol = dmod_cols.shape[1]
    tr = _tile(dd, 256)

    def body(c_ref, d_ref, o_ref):
        cv = c_ref[...]
        act = (cv * _sigmoid(cv)).astype(BF16).astype(F32)
        dm = d_ref[...].astype(BF16).astype(F32)
        acc = act[:, 0:1] * dm[0:1, :]
        for bi in range(1, nb):
            acc = acc + act[:, bi:bi + 1] * dm[bi:bi + 1, :]
        o_ref[...] = acc

    return pl.pallas_call(
        body, name=name, grid=(dd // tr,),
        in_specs=[pl.BlockSpec((tr, nb), lambda i: (i, 0)), pl.BlockSpec((nb, ncol), lambda i: (0, 0))],
        out_specs=pl.BlockSpec((tr, ncol), lambda i: (i, 0)),
        out_shape=jax.ShapeDtypeStruct((dd, ncol), F32),
        compiler_params=_params(("parallel",)),
    )(c_all_t, dmod_cols)


def _norm_mod(x, gain, mod, row0, name):
    t, dd = x.shape
    tt = _tile(t, 512)

    def body(x_ref, g_ref, m_ref, h_ref):
        xv = x_ref[...]
        r = lax.rsqrt(jnp.mean(xv * xv, axis=-1, keepdims=True) + EPS)
        shift, scale = m_ref[row0:row0 + 1, :], m_ref[row0 + 1:row0 + 2, :]
        h_ref[...] = ((xv * r * g_ref[...]) * (1.0 + scale) + shift).astype(BF16)

    return pl.pallas_call(
        body, name=name, grid=(t // tt,),
        in_specs=[pl.BlockSpec((tt, dd), lambda i: (i, 0)), pl.BlockSpec((1, dd), lambda i: (0, 0)),
                  pl.BlockSpec((8, dd), lambda i: (0, 0))],
        out_specs=pl.BlockSpec((tt, dd), lambda i: (i, 0)),
        out_shape=jax.ShapeDtypeStruct((t, dd), BF16),
        compiler_params=_params(("parallel",)),
    )(x, gain, mod)


def _norm_mod_bwd(x, gain, mod, row0, dh, dres, branch, name, jobs=()):
    t, dd = x.shape
    tt = _tile(t, 256)
    has_branch = branch is not None

    def body(*refs):
        if has_branch:
            x_ref, g_ref, m_ref, dh_ref, dres_ref, o_ref, dx_ref, st_ref, do_ref = refs
        else:
            x_ref, g_ref, m_ref, dh_ref, dres_ref, dx_ref, st_ref = refs

        @pl.when(pl.program_id(0) == 0)
        def _():
            st_ref[...] = jnp.zeros_like(st_ref)

        xv = x_ref[...]
        g = g_ref[...]
        scale = m_ref[row0 + 1:row0 + 2, :]
        r = lax.rsqrt(jnp.mean(xv * xv, axis=-1, keepdims=True) + EPS)
        nv = xv * r
        dhv = dh_ref[...].astype(F32)
        st_ref[0:1, :] += jnp.sum(dhv, axis=0, keepdims=True)
        st_ref[1:2, :] += jnp.sum(dhv * (nv * g), axis=0, keepdims=True)
        dng = dhv * (1.0 + scale)
        st_ref[2:3, :] += jnp.sum(dng * nv, axis=0, keepdims=True)
        dn = dng * g
        dx = dres_ref[...] + r * (dn - nv * jnp.mean(dn * nv, axis=-1, keepdims=True))
        dx_ref[...] = dx
        if has_branch:
            ov = o_ref[...].astype(F32)
            gate = m_ref[branch[1]:branch[1] + 1, :]
            st_ref[3:4, :] += jnp.sum(dx * ov, axis=0, keepdims=True)
            do_ref[...] = (dx * gate).astype(BF16)

    row = pl.BlockSpec((tt, dd), lambda i: (i, 0))
    in_specs = [row, pl.BlockSpec((1, dd), lambda i: (0, 0)), pl.BlockSpec((8, dd), lambda i: (0, 0)), row, row]
    args = [x, gain, mod, dh, dres]
    out_specs = [row, pl.BlockSpec((8, dd), lambda i: (0, 0))]
    out_shape = [jax.ShapeDtypeStruct((t, dd), F32), jax.ShapeDtypeStruct((8, dd), F32)]
    if has_branch:
        in_specs.append(row)
        args.append(branch[0])
        out_specs.append(row)
        out_shape.append(jax.ShapeDtypeStruct((t, dd), BF16))
    res, jres = _call(body, name=name, grid=(t // tt,), in_specs=in_specs, out_specs=out_specs, out_shape=out_shape,
                      sem=("arbitrary",), args=args, jobs=jobs)
    return (res, jres) if jobs else res


def _final_loss(x2, o2, final_g, target, mod, name):
    t, dd = x2.shape
    tt = _tile(t, 256)
    nsteps = t // tt

    def body(x_ref, o_ref, g_ref, t_ref, m_ref, loss_ref, dx_ref, do_ref, st_ref, lacc):
        i = pl.program_id(0)

        @pl.when(i == 0)
        def _():
            st_ref[...] = jnp.zeros_like(st_ref)
            lacc[...] = jnp.zeros_like(lacc)

        xv = x_ref[...]
        g = g_ref[...]
        r = lax.rsqrt(jnp.mean(xv * xv, axis=-1, keepdims=True) + EPS)
        nv = xv * r
        err = nv * g - t_ref[...]
        lacc[...] += jnp.sum(err * err, axis=0, keepdims=True)
        dy = err * (1.0 / dd)
        st_ref[0:1, :] += jnp.sum(dy * nv, axis=0, keepdims=True)
        dn = dy * g
        dx = r * (dn - nv * jnp.mean(dn * nv, axis=-1, keepdims=True))
        dx_ref[...] = dx
        st_ref[1:2, :] += jnp.sum(dx * o_ref[...].astype(F32), axis=0, keepdims=True)
        do_ref[...] = (dx * m_ref[5:6, :]).astype(BF16)

        @pl.when(i == nsteps - 1)
        def _():
            total = jnp.sum(lacc[...], axis=-1, keepdims=True) * (0.5 / dd)
            loss_ref[...] = jnp.broadcast_to(total, loss_ref.shape)

    row = pl.BlockSpec((tt, dd), lambda i: (i, 0))
    return pl.pallas_call(
        body, name=name, grid=(nsteps,),
        in_specs=[row, row, pl.BlockSpec((1, dd), lambda i: (0, 0)), row, pl.BlockSpec((8, dd), lambda i: (0, 0))],
        out_specs=[pl.BlockSpec((8, LANE), lambda i: (0, 0)), row, row, pl.BlockSpec((8, dd), lambda i: (0, 0))],
        out_shape=[jax.ShapeDtypeStruct((8, LANE), F32), jax.ShapeDtypeStruct((t, dd), F32),
                   jax.ShapeDtypeStruct((t, dd), BF16), jax.ShapeDtypeStruct((8, dd), F32)],
        scratch_shapes=[pltpu.VMEM((1, dd), F32)],
        compiler_params=_params(("arbitrary",)),
    )(x2, o2, final_g, target, mod)


def _lru_gates(xt, prev8, lp, wa_ref, wx_ref, first_tile, cw):
    tt = xt.shape[0]
    ext = jnp.concatenate([prev8, xt], axis=0)
    xs = [pltpu.roll(ext, s, 0)[8:, :] if s else xt for s in range(CONV_WIDTH)]
    xc = lp[0:1, :] + lp[7:8, :] * xs[0] + lp[6:7, :] * xs[1] + lp[5:6, :] * xs[2] + lp[4:5, :] * xs[3]
    xcb = xc.astype(BF16)
    za, zi = [], []
    for jb in range(cw // LRU_BLOCK_W):
        sl = slice(jb * LRU_BLOCK_W, (jb + 1) * LRU_BLOCK_W)
        za.append(jnp.dot(xcb[:, sl], wa_ref[jb].astype(BF16), preferred_element_type=F32))
        zi.append(jnp.dot(xcb[:, sl], wx_ref[jb].astype(BF16), preferred_element_type=F32))
    ra = _sigmoid(jnp.concatenate(za, axis=1) + lp[1:2, :])
    ri = _sigmoid(jnp.concatenate(zi, axis=1) + lp[2:3, :])
    sp = _softplus(-lp[3:4, :])
    log_a = -LRU_C * ra * sp
    av = jnp.exp(log_a)
    seq_start = jnp.logical_and(first_tile, lax.broadcasted_iota(jnp.int32, (tt, 1), 0) == 0)
    mult = jnp.where(seq_start, 1.0, jnp.sqrt(_neg_expm1(2.0 * log_a)))
    return xs, xc, xcb, ra, ri, sp, av, mult, seq_start


def _scan_groups(a, u, reverse):
    tt, cw = a.shape
    a3, u3 = a.reshape(tt // 8, 8, cw), u.reshape(tt // 8, 8, cw)
    r = lax.broadcasted_iota(jnp.int32, (1, 8, 1), 1)
    for s in (1, 2, 4):
        shift = 8 - s if reverse else s
        take = r < 8 - s if reverse else r >= s
        u3 = jnp.where(take, a3 * pltpu.roll(u3, shift, 1) + u3, u3)
        a3 = jnp.where(take, a3 * pltpu.roll(a3, shift, 1), a3)
    return a3, u3


def _scan_carry(a_s, u_s, h_s, carry, reverse):
    ng = a_s.shape[0]

    def step(i, h):
        g = ng - 1 - i if reverse else i
        hg = u_s[g] + a_s[g] * h
        h_s[g] = hg
        return hg[0:1, :] if reverse else hg[7:8, :]

    return lax.fori_loop(0, ng, step, carry, unroll=4)


def _lru_fwd(proj, lp, wa, wx, name, jobs=()):
    t = proj.shape[0]
    w = D_MODEL
    cw = LRU_CW
    tt = _tile(t, 256)
    nb = cw // LRU_BLOCK_W

    def body(lx_ref, lg_ref, lp_ref, wa_ref, wx_ref, rec_ref, ya_ref, a_s, u_s, h_s, halo, carry):
        ti = pl.program_id(1)

        @pl.when(ti == 0)
        def _():
            halo[...] = jnp.zeros_like(halo)
            carry[...] = jnp.zeros_like(carry)

        xt = lx_ref[...].astype(F32)
        lp_v = lp_ref[...]
        _, xc, _, _, ri, _, av, mult, _ = _lru_gates(xt, halo[...], lp_v, wa_ref, wx_ref, ti == 0, cw)
        halo[...] = xt[tt - 8:, :]
        a_s[...], u_s[...] = _scan_groups(av, mult * (ri * xc), False)
        carry[...] = _scan_carry(a_s, u_s, h_s, carry[...], False)
        rec = h_s[...].reshape(tt, cw)
        rec_ref[...] = rec.astype(BF16)
        gelu, _ = _gelu_tanh(lg_ref[...].astype(F32))
        ya_ref[...] = (rec * gelu).astype(BF16)

    off_g = OFF_LG // cw
    res, jres = _call(
        body, name=name, grid=(w // cw, t // tt),
        in_specs=[pl.BlockSpec((tt, cw), lambda ci, ti: (ti, ci)),
                  pl.BlockSpec((tt, cw), lambda ci, ti: (ti, off_g + ci)),
                  pl.BlockSpec((8, cw), lambda ci, ti: (0, ci)),
                  pl.BlockSpec((nb, LRU_BLOCK_W, LRU_BLOCK_W), lambda ci, ti: (ci, 0, 0)),
                  pl.BlockSpec((nb, LRU_BLOCK_W, LRU_BLOCK_W), lambda ci, ti: (ci, 0, 0))],
        out_specs=[pl.BlockSpec((tt, cw), lambda ci, ti: (ti, ci))] * 2,
        out_shape=[jax.ShapeDtypeStruct((t, w), BF16)] * 2,
        scratch_shapes=[pltpu.VMEM((tt // 8, 8, cw), F32)] * 3 + [pltpu.VMEM((8, cw), F32), pltpu.VMEM((1, cw), F32)],
        sem=("parallel", "arbitrary"), args=[proj, proj, lp, wa, wx], jobs=jobs)
    return (res, jres) if jobs else res


def _lru_bwd(proj, rec, dya, lp, wa, wx, name, jobs=()):
    t = proj.shape[0]
    w = D_MODEL
    cw = LRU_CW
    tt = _tile(t, 256)
    nt = t // tt
    nb = cw // LRU_BLOCK_W
    r8 = tt // 8

    def body(lx_ref, lxp_ref, lg_ref, rec_ref, recp_ref, dya_ref, lp_ref, wa_ref, wx_ref,
             dlx_ref, dlg_ref, st_ref, dwa_ref, dwx_ref, a_s, d_s, dh_s, nhalo, carry):
        step_i = pl.program_id(1)
        ti = nt - 1 - step_i

        @pl.when(step_i == 0)
        def _():
            st_ref[...] = jnp.zeros_like(st_ref)
            dwa_ref[...] = jnp.zeros_like(dwa_ref)
            dwx_ref[...] = jnp.zeros_like(dwx_ref)
            nhalo[...] = jnp.zeros_like(nhalo)
            carry[...] = jnp.zeros_like(carry)

        first = ti == 0
        keep = jnp.where(first, 0.0, 1.0)
        xt = lx_ref[...].astype(F32)
        prev8 = lxp_ref[...].astype(F32) * keep
        lp_v = lp_ref[...]
        xs, xc, xcb, ra, ri, sp, av, mult, seq_start = _lru_gates(xt, prev8, lp_v, wa_ref, wx_ref, first, cw)

        lg = lg_ref[...].astype(F32)
        gelu, th = _gelu_tanh(lg)
        dyav = dya_ref[...].astype(F32)
        recv = rec_ref[...].astype(F32)
        dlg_ref[...] = (dyav * recv * _gelu_tanh_grad(lg, th)).astype(BF16)

        drec = dyav * gelu
        e_in = carry[...]
        a_s[...], d_s[...] = _scan_groups(av, av * drec, True)
        carry[...] = _scan_carry(a_s, d_s, dh_s, e_in, True)
        e_next = jnp.concatenate([dh_s[...].reshape(tt, cw), jnp.broadcast_to(e_in, (8, cw))], axis=0)
        dh = drec + pltpu.roll(e_next, tt + 7, 0)[:tt, :]
        rprev8 = recp_ref[...].astype(F32) * keep
        hprev = pltpu.roll(jnp.concatenate([rprev8, recv], axis=0), 1, 0)[8:, :]
        da = dh * hprev
        dmult = jnp.where(seq_start, 0.0, dh * ri * xc)
        dri = dh * mult * xc
        dxc = dh * mult * ri
        dlog_a = da * av - dmult * (av * av) / mult
        dra = dlog_a * (-LRU_C * sp)
        lam = lp_v[3:4, :]
        st_ref[3:4, :] += jnp.sum(dlog_a * ra, axis=0, keepdims=True) * (LRU_C * _sigmoid(-lam))
        dza = dra * ra * (1.0 - ra)
        dzi = dri * ri * (1.0 - ri)
        st_ref[1:2, :] += jnp.sum(dza, axis=0, keepdims=True)
        st_ref[2:3, :] += jnp.sum(dzi, axis=0, keepdims=True)
        dzab, dzib = dza.astype(BF16), dzi.astype(BF16)
        back = []
        for jb in range(nb):
            sl = slice(jb * LRU_BLOCK_W, (jb + 1) * LRU_BLOCK_W)
            dwa_ref[jb] += lax.dot_general(xcb[:, sl], dzab[:, sl], (_DIMS["tn"], ((), ())), preferred_element_type=F32)
            dwx_ref[jb] += lax.dot_general(xcb[:, sl], dzib[:, sl], (_DIMS["tn"], ((), ())), preferred_element_type=F32)
            back.append(
                lax.dot_general(dzab[:, sl], wa_ref[jb].astype(BF16), (_DIMS["nt"], ((), ())), preferred_element_type=F32)
                + lax.dot_general(dzib[:, sl], wx_ref[jb].astype(BF16), (_DIMS["nt"], ((), ())), preferred_element_type=F32))
        dxc = dxc + jnp.concatenate(back, axis=1)
        st_ref[0:1, :] += jnp.sum(dxc, axis=0, keepdims=True)
        for k in range(CONV_WIDTH):
            st_ref[4 + k:5 + k, :] += jnp.sum(dxc * xs[CONV_WIDTH - 1 - k], axis=0, keepdims=True)
        ext = jnp.concatenate([dxc, nhalo[...]], axis=0)
        dlx = lp_v[7:8, :] * dxc
        for s in range(1, CONV_WIDTH):
            dlx = dlx + lp_v[7 - s:8 - s, :] * pltpu.roll(ext, tt + 8 - s, 0)[:tt, :]
        dlx_ref[...] = dlx.astype(BF16)
        nhalo[...] = dxc[0:8, :]

    off_g = OFF_LG // cw
    tile = lambda off: pl.BlockSpec((tt, cw), lambda ci, si: (nt - 1 - si, off + ci))
    prev = lambda off: pl.BlockSpec((8, cw), lambda ci, si: (jnp.maximum((nt - 1 - si) * r8 - 1, 0), off + ci))
    wspec = pl.BlockSpec((nb, LRU_BLOCK_W, LRU_BLOCK_W), lambda ci, si: (ci, 0, 0))
    st_spec = pl.BlockSpec((8, cw), lambda ci, si: (0, ci))
    res, jres = _call(
        body, name=name, grid=(w // cw, nt),
        in_specs=[tile(0), prev(0), tile(off_g), tile(0), prev(0), tile(0), st_spec, wspec, wspec],
        out_specs=[tile(0), tile(0), st_spec, wspec, wspec],
        out_shape=[jax.ShapeDtypeStruct((t, w), BF16), jax.ShapeDtypeStruct((t, w), BF16),
                   jax.ShapeDtypeStruct((8, w), F32),
                   jax.ShapeDtypeStruct((LRU_BLOCKS, LRU_BLOCK_W, LRU_BLOCK_W), F32),
                   jax.ShapeDtypeStruct((LRU_BLOCKS, LRU_BLOCK_W, LRU_BLOCK_W), F32)],
        scratch_shapes=[pltpu.VMEM((tt // 8, 8, cw), F32)] * 3 + [pltpu.VMEM((8, cw), F32), pltpu.VMEM((1, cw), F32)],
        sem=("parallel", "arbitrary"), args=[proj, proj, proj, rec, rec, dya, lp, wa, wx], jobs=jobs)
    return (res, jres) if jobs else res


def _band_valid(first_block):
    qi = lax.broadcasted_iota(jnp.int32, (BLOCK, 2 * BLOCK), 0)
    ki = lax.broadcasted_iota(jnp.int32, (BLOCK, 2 * BLOCK), 1)
    rel = qi + BLOCK - ki
    valid = jnp.logical_and(rel >= 0, rel < BLOCK)
    return jnp.logical_and(valid, jnp.logical_or(ki >= BLOCK, jnp.logical_not(first_block)))


ATTN_STACK_BWD = 8


def _low_half():
    return lax.broadcasted_iota(jnp.int32, (1, LANE), 1) < HEAD_DIM


def _stack_heads(x, h0, ns):
    low = _low_half()
    parts = []
    for g in range(ns):
        h = h0 + g
        slab = x[:, (h // 2) * LANE:(h // 2 + 1) * LANE]
        parts.append(jnp.where(low if h % 2 == 0 else jnp.logical_not(low), slab, jnp.zeros_like(slab)))
    return jnp.concatenate(parts, axis=0)


def _unstack_heads(y, ns):
    low = _low_half()
    pairs = [jnp.where(low, y[(2 * j) * BLOCK:(2 * j + 1) * BLOCK], y[(2 * j + 1) * BLOCK:(2 * j + 2) * BLOCK])
             for j in range(ns // 2)]
    return pairs[0] if len(pairs) == 1 else jnp.concatenate(pairs, axis=1)


def _dup_kv(xall, kv):
    part = xall[:, kv * HEAD_DIM:(kv + 1) * HEAD_DIM]
    return jnp.concatenate([part, part], axis=1)


def _fold_halves(a):
    return a + pltpu.roll(a, HEAD_DIM, 1)


def _group_probs(qg, k2, bias_g, sink_g, lse_g, valid):
    s = lax.dot_general(qg, k2, (_DIMS["nt"], ((), ())), preferred_element_type=F32) * (HEAD_DIM ** -0.5)
    s = jnp.where(valid[None], s.reshape(bias_g.shape) + bias_g, NEG_INF)
    return jnp.exp(s - lse_g), jnp.exp(sink_g - lse_g)


def _group_sinks(s_ref, h0, ns):
    return jnp.concatenate([jnp.full((1, BLOCK, 1), s_ref[h0 + g], F32) for g in range(ns)], axis=0)


def _attn_specs(t):
    nblk = t // BLOCK
    q_spec = pl.BlockSpec((BLOCK, D_MODEL), lambda n: (n, OFF_Q // D_MODEL))
    kc = pl.BlockSpec((BLOCK, KV_WIDTH), lambda n: (n, OFF_K // KV_WIDTH))
    kp = pl.BlockSpec((BLOCK, KV_WIDTH), lambda n: (jnp.maximum(n - 1, 0), OFF_K // KV_WIDTH))
    vc = pl.BlockSpec((BLOCK, KV_WIDTH), lambda n: (n, OFF_V // KV_WIDTH))
    vp = pl.BlockSpec((BLOCK, KV_WIDTH), lambda n: (jnp.maximum(n - 1, 0), OFF_V // KV_WIDTH))
    bias_spec = pl.BlockSpec((N_Q_HEADS, BLOCK, 2 * BLOCK), lambda n: (0, 0, 0))
    sink_spec = pl.BlockSpec(memory_space=pltpu.SMEM)
    return nblk, [q_spec, kp, kc, vp, vc, bias_spec, sink_spec]


def _attn_fwd(proj, bias, sinks, name, jobs=()):
    t = proj.shape[0]
    nblk, in_specs = _attn_specs(t)

    def body(q_ref, kp_ref, kc_ref, vp_ref, vc_ref, b_ref, s_ref, o_ref, lse_ref):
        valid = _band_valid(pl.program_id(0) == 0)
        lane = lax.broadcasted_iota(jnp.int32, (1, LANE), 1)
        q = q_ref[...]
        kall = jnp.concatenate([kp_ref[...], kc_ref[...]], axis=0)
        vall = jnp.concatenate([vp_ref[...], vc_ref[...]], axis=0)
        outs = []
        lse = jnp.zeros((BLOCK, LANE), F32)
        for h in range(N_Q_HEADS):
            kv = h // GROUP
            kk = kall[:, kv * HEAD_DIM:(kv + 1) * HEAD_DIM]
            vv = vall[:, kv * HEAD_DIM:(kv + 1) * HEAD_DIM]
            s = lax.dot_general(q[:, h * HEAD_DIM:(h + 1) * HEAD_DIM], kk, (_DIMS["nt"], ((), ())),
                                preferred_element_type=F32) * (HEAD_DIM ** -0.5)
            s = jnp.where(valid, s + b_ref[h], NEG_INF)
            sink = s_ref[h]
            m = jnp.maximum(jnp.max(s, axis=-1, keepdims=True), sink)
            e = jnp.exp(s - m)
            den = jnp.sum(e, axis=-1, keepdims=True) + jnp.exp(sink - m)
            lse = jnp.where(lane == h, m + jnp.log(den), lse)
            outs.append(jnp.dot((e * (1.0 / den)).astype(BF16), vv, preferred_element_type=F32))
        o_ref[...] = jnp.concatenate(outs, axis=1).astype(BF16)
        lse_ref[...] = lse

    res, jres = _call(
        body, name=name, grid=(nblk,), in_specs=in_specs,
        out_specs=[pl.BlockSpec((BLOCK, D_MODEL), lambda n: (n, 0)), pl.BlockSpec((BLOCK, LANE), lambda n: (n, 0))],
        out_shape=[jax.ShapeDtypeStruct((t, D_MODEL), BF16), jax.ShapeDtypeStruct((t, LANE), F32)],
        sem=("parallel",), args=[proj, proj, proj, proj, proj, bias, sinks], jobs=jobs)
    return (res, jres) if jobs else res


def _attn_bwd(proj, datt, lse, bias, sinks, name, jobs=()):
    t = proj.shape[0]
    nblk, in_specs = _attn_specs(t)
    in_specs = in_specs + [pl.BlockSpec((BLOCK, D_MODEL), lambda n: (n, 0)), pl.BlockSpec((BLOCK, LANE), lambda n: (n, 0))]
    scale = HEAD_DIM ** -0.5

    def body(q_ref, kp_ref, kc_ref, vp_ref, vc_ref, b_ref, s_ref, do_ref, lse_ref,
             dq_ref, dk_ref, dv_ref, db_ref, ds_ref):
        n = pl.program_id(0)

        @pl.when(n == 0)
        def _():
            dk_ref[...] = jnp.zeros_like(dk_ref)
            dv_ref[...] = jnp.zeros_like(dv_ref)
            db_ref[...] = jnp.zeros_like(db_ref)
            ds_ref[...] = jnp.zeros_like(ds_ref)

        valid = _band_valid(n == 0)
        lane = lax.broadcasted_iota(jnp.int32, (1, LANE), 1)
        q = q_ref[...]
        do = do_ref[...]
        lse_all = lse_ref[...]
        kall = jnp.concatenate([kp_ref[...], kc_ref[...]], axis=0)
        vall = jnp.concatenate([vp_ref[...], vc_ref[...]], axis=0)
        low = _low_half()
        dqs, dks, dvs = [], [], []
        dsink_row = jnp.zeros((1, LANE), F32)
        ns = ATTN_STACK_BWD
        for kv in range(N_KV_HEADS):
            k2, v2 = _dup_kv(kall, kv), _dup_kv(vall, kv)
            dk_acc = jnp.zeros((2 * BLOCK, LANE), F32)
            dv_acc = jnp.zeros((2 * BLOCK, LANE), F32)
            for h0 in range(kv * GROUP, (kv + 1) * GROUP, ns):
                qg, dog = _stack_heads(q, h0, ns), _stack_heads(do, h0, ns)
                lse_g = jnp.concatenate([lse_all[:, h0 + g:h0 + g + 1][None] for g in range(ns)], axis=0)
                p, psink = _group_probs(qg, k2, b_ref[h0:h0 + ns], _group_sinks(s_ref, h0, ns), lse_g, valid)
                dp = lax.dot_general(dog, v2, (_DIMS["nt"], ((), ())), preferred_element_type=F32)
                dp = dp.reshape(ns, BLOCK, 2 * BLOCK)
                delta = jnp.sum(p * dp, axis=-1, keepdims=True)
                ds = p * (dp - delta)
                db_ref[h0:h0 + ns] += ds
                dsink = -jnp.sum(psink * delta, axis=1, keepdims=True)
                for g in range(ns):
                    dsink_row = dsink_row + jnp.where(lane == h0 + g, dsink[g], 0.0)
                dsb = ds.reshape(ns * BLOCK, 2 * BLOCK).astype(BF16)
                pb = p.reshape(ns * BLOCK, 2 * BLOCK).astype(BF16)
                dqs.append(_unstack_heads(jnp.dot(dsb, k2, preferred_element_type=F32) * scale, ns))
                dk_acc = dk_acc + lax.dot_general(dsb, qg, (_DIMS["tn"], ((), ())), preferred_element_type=F32)
                dv_acc = dv_acc + lax.dot_general(pb, dog, (_DIMS["tn"], ((), ())), preferred_element_type=F32)
            dks.append(_fold_halves(dk_acc) * scale)
            dvs.append(_fold_halves(dv_acc))
        dq_ref[...] = jnp.concatenate(dqs, axis=1).astype(BF16)
        rows = pl.ds(pl.multiple_of(n * BLOCK, BLOCK), 2 * BLOCK)
        dk_ref[rows, :] += jnp.concatenate([jnp.where(low, dks[0], dks[1]), jnp.where(low, dks[2], dks[3])], axis=1)
        dv_ref[rows, :] += jnp.concatenate([jnp.where(low, dvs[0], dvs[1]), jnp.where(low, dvs[2], dvs[3])], axis=1)
        ds_ref[0:1, :] += dsink_row

    res, jres = _call(
        body, name=name, grid=(nblk,), in_specs=in_specs,
        out_specs=[pl.BlockSpec((BLOCK, D_MODEL), lambda n: (n, 0)),
                   pl.BlockSpec((t + BLOCK, KV_WIDTH), lambda n: (0, 0)),
                   pl.BlockSpec((t + BLOCK, KV_WIDTH), lambda n: (0, 0)),
                   pl.BlockSpec((N_Q_HEADS, BLOCK, 2 * BLOCK), lambda n: (0, 0, 0)),
                   pl.BlockSpec((8, LANE), lambda n: (0, 0))],
        out_shape=[jax.ShapeDtypeStruct((t, D_MODEL), BF16),
                   jax.ShapeDtypeStruct((t + BLOCK, KV_WIDTH), F32),
                   jax.ShapeDtypeStruct((t + BLOCK, KV_WIDTH), F32),
                   jax.ShapeDtypeStruct((N_Q_HEADS, BLOCK, 2 * BLOCK), F32),
                   jax.ShapeDtypeStruct((8, LANE), F32)],
        sem=("arbitrary",), args=[proj, proj, proj, proj, proj, bias, sinks, datt, lse], jobs=jobs)
    return (res, jres) if jobs else res


def _bucket_table():
    qi = np.arange(BLOCK)[:, None]
    ki = np.arange(2 * BLOCK)[None, :]
    rel = np.maximum(qi + BLOCK - ki, 0)
    max_exact = N_BUCKETS // 2
    relf = np.maximum(rel, 1).astype(np.float32)
    large = max_exact + (np.log(relf / max_exact) / math.log(MAX_DISTANCE / max_exact)
                         * (N_BUCKETS - max_exact)).astype(np.int32)
    large = np.minimum(large, N_BUCKETS - 1)
    return np.where(rel < max_exact, rel, large).astype(np.int32)


def _rel_bias_fwd(rel_bias, onehot, name):
    n = onehot.shape[1]
    tn = _tile(n, 8192, LANE)

    def body(r_ref, o_ref, out_ref):
        out_ref[...] = lax.dot_general(r_ref[...], o_ref[...], (_DIMS["tn"], ((), ())),
                                       precision=lax.Precision.HIGHEST, preferred_element_type=F32)

    return pl.pallas_call(
        body, name=name, grid=(n // tn,),
        in_specs=[pl.BlockSpec(rel_bias.shape, lambda j: (0, 0)), pl.BlockSpec((N_BUCKETS, tn), lambda j: (0, j))],
        out_specs=pl.BlockSpec((N_Q_HEADS, tn), lambda j: (0, j)),
        out_shape=jax.ShapeDtypeStruct((N_Q_HEADS, n), F32),
        compiler_params=_params(("parallel",)),
    )(rel_bias, onehot)


def _rel_bias_bwd(dbias, onehot, name):
    def body(d_ref, o_ref, out_ref):
        out_ref[...] = lax.dot_general(d_ref[...], o_ref[...], (_DIMS["nt"], ((), ())),
                                       precision=lax.Precision.HIGHEST, preferred_element_type=F32)

    full = pl.BlockSpec(dbias.shape, lambda: (0, 0))
    return pl.pallas_call(
        body, name=name, in_specs=[full, full],
        out_specs=pl.BlockSpec((N_Q_HEADS, N_BUCKETS), lambda: (0, 0)),
        out_shape=jax.ShapeDtypeStruct((N_Q_HEADS, N_BUCKETS), F32),
        compiler_params=_params(()),
    )(dbias, onehot)


def _part_specs(gparts, part_index, tr, cc):
    blk = pl.BlockSpec((tr, cc), lambda i, *_: (i, 0))
    specs = []
    for i, _ in enumerate(gparts):
        if i in part_index:
            ix = part_index[i]
            if isinstance(ix, int):
                specs.append(pl.BlockSpec((None, tr, cc), functools.partial(lambda i, *_, ix: (ix, i, 0), ix=ix)))
            else:
                specs.append(pl.BlockSpec((None, tr, cc), lambda i, ix_ref: (ix_ref[0], i, 0)))
        else:
            specs.append(blk)
    return blk, specs


def _sum_parts(gparts, part_index, name):
    r, cc = gparts[0].shape[-2:]
    tr = _tile(r, 256)
    dyn = [i for i, ix in part_index.items() if not isinstance(ix, int)]
    assert len(dyn) <= 1

    def body(*refs):
        refs = refs[len(dyn):]
        g = refs[0][...].astype(F32)
        for gr in refs[1:-1]:
            g = g + gr[...].astype(F32)
        refs[-1][...] = g

    blk, specs = _part_specs(gparts, part_index, tr, cc)
    grid_spec = pltpu.PrefetchScalarGridSpec(num_scalar_prefetch=len(dyn), grid=(r // tr,), in_specs=specs, out_specs=blk)
    return pl.pallas_call(
        body, name=name, grid_spec=grid_spec, out_shape=jax.ShapeDtypeStruct((r, cc), F32),
        compiler_params=_params(("parallel",)),
    )(*[part_index[i] for i in dyn], *gparts)


def _adamw(w, m, v, gparts, name, part_index=None):
    r, cc = w.shape
    tr = _tile(r, 128)
    np_ = len(gparts)
    part_index = part_index or {}
    dyn = [i for i, ix in part_index.items() if not isinstance(ix, int)]
    assert len(dyn) <= 1
    bc1 = 1.0 / (1.0 - ADAM_B1 ** ADAM_STEP)
    bc2 = 1.0 / (1.0 - ADAM_B2 ** ADAM_STEP)

    def body(*refs):
        refs = refs[len(dyn):]
        w_ref, m_ref, v_ref = refs[:3]
        g_refs = refs[3:3 + np_]
        g_out, d_out, m_out, v_out = refs[3 + np_:]
        g = g_refs[0][...].astype(F32)
        for gr in g_refs[1:]:
            g = g + gr[...].astype(F32)
        mn = ADAM_B1 * m_ref[...] + (1.0 - ADAM_B1) * g
        vn = ADAM_B2 * v_ref[...] + (1.0 - ADAM_B2) * (g * g)
        g_out[...] = g
        m_out[...] = mn
        v_out[...] = vn
        d_out[...] = -ADAM_LR * ((mn * bc1) / (jnp.sqrt(vn * bc2) + ADAM_EPS) + ADAM_WD * w_ref[...])

    blk, g_specs = _part_specs(gparts, part_index, tr, cc)
    grid_spec = pltpu.PrefetchScalarGridSpec(
        num_scalar_prefetch=len(dyn), grid=(r // tr,),
        in_specs=[blk, blk, blk] + g_specs, out_specs=[blk] * 4)
    return pl.pallas_call(
        body, name=name, grid_spec=grid_spec,
        out_shape=[jax.ShapeDtypeStruct((r, cc), F32)] * 4,
        compiler_params=_params(("parallel",)),
    )(*[part_index[i] for i in dyn], w, m, v, *gparts)


def _row(vec):
    return vec.reshape(1, -1)


def kernel(x, c, w_ada, b_ada, norm1_g, w_in, conv_w, conv_b, lru_wa, lru_ba, lru_wx, lru_bx, lru_lambda, w_lru_out, w_attn_out, attn_sinks, rel_bias, w_out, norm2_g, w_ff1, w_ff2, final_g, loss_target, m_w_ada, m_b_ada, m_norm1_g, m_w_in, m_conv_w, m_conv_b, m_lru_wa, m_lru_ba, m_lru_wx, m_lru_bx, m_lru_lambda, m_w_lru_out, m_w_attn_out, m_attn_sinks, m_rel_bias, m_w_out, m_norm2_g, m_w_ff1, m_w_ff2, m_final_g, v_w_ada, v_b_ada, v_norm1_g, v_w_in, v_conv_w, v_conv_b, v_lru_wa, v_lru_ba, v_lru_wx, v_lru_bx, v_lru_lambda, v_w_lru_out, v_w_attn_out, v_attn_sinks, v_rel_bias, v_w_out, v_norm2_g, v_w_ff1, v_w_ff2, v_final_g):
    dd = D_MODEL
    t = x.shape[1]
    ix, iy, ic = lax.axis_index("x"), lax.axis_index("y"), lax.axis_index("c")
    me = 4 * ix + 2 * iy + ic
    chip = 2 * ix + iy
    c_idx = jnp.reshape(ic, (1,)).astype(jnp.int32)
    chip_idx = jnp.reshape(chip, (1,)).astype(jnp.int32)

    xs = x[0]
    target = loss_target[0]
    ncol_ada = w_ada.shape[2]
    cshard = conv_w.shape[2]

    shards = [jnp.transpose(w_in[0]).astype(BF16), w_lru_out[0].astype(BF16), w_attn_out[0].astype(BF16),
              w_out[0].astype(BF16), jnp.transpose(w_ff1[0]).astype(BF16), w_ff2[0].astype(BF16)]
    s_in, s_lo, s_ao, s_o, s_f1, s_f2 = shards

    def full(g):
        return g.reshape(N_DEV * g.shape[1], dd)

    def stack(g):
        return g.reshape(N_DEV, g.shape[0] // N_DEV, dd)

    (w_in_pair,) = _run_job(_pair_job([s_in]), "ag_w_in_pair")
    w_in_pair = lax.dynamic_update_slice_in_dim(w_in_pair, s_in[None], ic, axis=0)

    pack0 = jnp.zeros((8, dd), F32).at[0:1, :].set(c).at[1:1 + CONV_WIDTH, 0:cshard].set(conv_w[0])
    (g0,) = _all_gather([pack0], "ag_cond")
    c_all = g0[:, 0, :]
    conv_w_full = jnp.transpose(g0[:, 1:1 + CONV_WIDTH, 0:cshard], (1, 0, 2)).reshape(CONV_WIDTH, dd)
    b_cols = lax.dynamic_slice(b_ada, (0, me * ncol_ada), (1, ncol_ada))
    mod_cols = _ada_fwd(c_all, w_ada[0], b_cols, "ada_fwd")
    (g1,) = _all_gather([mod_cols], "ag_mod")
    mod_mine = lax.dynamic_index_in_dim(g1, me, axis=1, keepdims=False).reshape(6, dd)
    mod = jnp.concatenate([mod_mine, jnp.zeros((2, dd), F32)], axis=0)

    bucket = _bucket_table()
    onehot = jnp.asarray((bucket.reshape(1, -1) == np.arange(N_BUCKETS)[:, None]).astype(np.float32))
    bias = _rel_bias_fwd(rel_bias, onehot, "rel_bias_fwd").reshape(N_Q_HEADS, BLOCK, 2 * BLOCK)
    sinks = attn_sinks.reshape(N_Q_HEADS)
    lp = jnp.concatenate([conv_b, lru_ba, lru_bx, lru_lambda, conv_w_full], axis=0)
    wa, wx = lru_wa[0], lru_wx[0]

    tm = _tile(t, 1024)
    tn = 512
    h = _norm_mod(xs, norm1_g, mod, 0, "norm1")
    tn2 = 1024
    chunk_w = 2 * s_in.shape[0]

    def ids(*ks):
        return jnp.stack([jnp.asarray(k, jnp.int32) for k in ks])

    tn_in = _tile(chunk_w, 1024, LANE)
    (proj,), ((g_near,),) = _mm(
        h, w_in_pair.reshape(chunk_w, dd), "nt", tm=tm, tn=tn_in, outs=[BF16], name="in_proj_own",
        chunks=(ids(chip), chunk_w, ids(0)), n_total=IN_WIDTH, jobs=[_gather_job([w_in_pair], (0.7,), "near")])
    near_ids = ids(2 * (1 - ix) + iy, 2 * ix + 1 - iy)
    (proj,), ((g_far,),) = _mm(
        h, full(g_near), "nt", tm=tm, tn=tn_in, outs=[BF16], name="in_proj_near",
        chunks=(near_ids, chunk_w, near_ids), into=proj, jobs=[_gather_job([g_near], (0.6,), "far")])
    far_chip = 2 * (1 - ix) + 1 - iy
    (proj,), ((g_o,),) = _mm(
        h, full(g_far), "nt", tm=tm, tn=tn_in, outs=[BF16], name="in_proj_far",
        chunks=(ids(far_chip), chunk_w, ids(far_chip)), into=proj, jobs=[_gather_job([s_o], (0.5, 0.8))])
    far_slots = lax.dynamic_slice_in_dim(g_far, 2 * far_chip, 2, axis=0)
    win_t = lax.dynamic_update_slice_in_dim(g_near, far_slots, 2 * far_chip, axis=0)
    win_t = full(lax.dynamic_update_slice_in_dim(win_t, w_in_pair, 2 * chip, axis=0))
    (g_o,) = _fill_own([g_o], [s_o])
    (rec, ya), ((g_f1,),) = _lru_fwd(proj, lp, wa, wx, "lru_fwd", jobs=[_gather_job([s_f1], (0.6, 0.9))])
    (att, lse), ((g_lo,), (g_ao,)) = _attn_fwd(proj, bias, sinks, "attn_fwd",
                                        jobs=[_gather_job([s_lo], (0.3, 0.45)), _gather_job([s_ao], (0.75, 0.9))])
    g_lo, g_ao, g_f1 = _fill_own([g_lo, g_ao, g_f1], [s_lo, s_ao, s_f1])
    wlo, wao, wo, wf1_t = full(g_lo), full(g_ao), full(g_o), full(g_f1)
    (y_a,) = _mm(ya, wlo, "nn", tm=tm, tn=tn2, outs=[BF16], name="lru_out")

    def merge_epi(acc, ya_t, ga_t, gb_t):
        return acc, _sigmoid(ga_t.astype(F32)) * ya_t.astype(F32) + _sigmoid(gb_t.astype(F32)) * acc

    y_b, merged = _mm(att, wao, "nn", tm=tm, tn=tn, outs=[BF16, BF16], epi=merge_epi,
                      tiles=[(y_a, 0), (proj, OFF_GA // tn), (proj, OFF_GB // tn)], name="attn_out_merge")

    def resid_epi(row):
        def epi(acc, x_t, mod_t):
            return acc, x_t + mod_t[row:row + 1, :] * acc
        return epi

    o1, x1 = _mm(merged, wo, "nn", tm=tm, tn=tn2, outs=[BF16, F32], epi=resid_epi(2),
                 tiles=[(xs, 0)], rows=[mod], name="out_proj")
    h2 = _norm_mod(x1, norm2_g, mod, 3, "norm2")

    def relu2_epi(acc):
        rl = jnp.maximum(acc, 0.0)
        return acc, rl * rl

    (f_pre, ff), ((g_f2,),) = _mm(h2, wf1_t, "nt", tm=tm, tn=1024, outs=[BF16, BF16], epi=relu2_epi, name="ff1",
                                  jobs=[_gather_job([s_f2], (0.6, 0.9))])
    wf2 = full(_fill_own([g_f2], [s_f2])[0])
    tmh = _tile(t, 512)
    o2, x2 = _mm(ff, wf2, "nn", tm=tmh, tn=tn, outs=[BF16, F32], epi=resid_epi(5),
                 tiles=[(x1, 0)], rows=[mod], name="ff2")

    loss_blk, dx2, do2, st_f = _final_loss(x2, o2, _row(final_g), target, mod, "final_loss")
    loss = lax.psum(loss_blk[0, 0], ("x", "y", "c"))

    def drelu2_epi(acc, f_t):
        return (acc * (2.0 * jnp.maximum(f_t.astype(F32), 0.0)),)

    tmw, tnw, tkw = 512, dd, _tile(t, 2048)
    (df,) = _mm(do2, wf2, "nt", tm=tm, tn=1024, outs=[BF16], epi=drelu2_epi, tiles=[(f_pre, 0)], name="ff2_dx")
    (dwf2,) = _mm(ff, do2, "tn", tm=tmw, tn=tnw, tk=tkw, outs=[BF16], name="ff2_dw")
    (dh2,), ((sib_f2,),) = _mm(df, wf1_t, "nn", tm=tmh, tn=tn, outs=[BF16], name="ff1_dx",
                              jobs=[_sibling_job([stack(dwf2)])])
    pair_f2 = _pair_add(stack(dwf2), sib_f2, c_idx, "rs_pair_add_w_ff2")
    (dwf1_t,), ((chips_f2,),) = _mm(df, h2, "tn", tm=tmw, tn=tnw, tk=tkw, outs=[BF16], name="ff1_dw",
                                    jobs=[_chips_job([pair_f2])])
    (dx1, st_2, do1), ((sib_f1,),) = _norm_mod_bwd(x1, norm2_g, mod, 3, dh2, dx2, (o1, 2), "norm2_bwd",
                                                   jobs=[_sibling_job([stack(dwf1_t)])])
    pair_f1 = _pair_add(stack(dwf1_t), sib_f1, c_idx, "rs_pair_add_w_ff1")

    def dmerge_epi(acc, ga_t, gb_t, ya_t, yb_t):
        sa, sb = _sigmoid(ga_t.astype(F32)), _sigmoid(gb_t.astype(F32))
        return (acc * sa, acc * sb, acc * ya_t.astype(F32) * sa * (1.0 - sa), acc * yb_t.astype(F32) * sb * (1.0 - sb))

    tmh = _tile(t, 512)
    dy_a, dy_b, dga, dgb = _mm(do1, wo, "nt", tm=tmh, tn=tn, outs=[BF16] * 4, epi=dmerge_epi,
                               tiles=[(proj, OFF_GA // tn), (proj, OFF_GB // tn), (y_a, 0), (y_b, 0)], name="out_proj_dx")
    (dwo,) = _mm(merged, do1, "tn", tm=tmw, tn=tnw, tk=tkw, outs=[BF16], name="out_proj_dw")
    (datt,) = _mm(dy_b, wao, "nt", tm=tm, tn=tn2, outs=[BF16], name="attn_out_dx")
    (dwao,) = _mm(att, dy_b, "tn", tm=tmw, tn=tnw, tk=tkw, outs=[BF16], name="attn_out_dw")
    (dya,) = _mm(dy_a, wlo, "nt", tm=tm, tn=tn2, outs=[BF16], name="lru_out_dx")
    (dwlo,) = _mm(ya, dy_a, "tn", tm=tmw, tn=tnw, tk=tkw, outs=[BF16], name="lru_out_dw")
    (dq, dk_pad, dv_pad, dbias, dsinks), (sib_3, (chips_f1,)) = _attn_bwd(
        proj, datt, lse, bias, sinks, "attn_bwd",
        jobs=[_sibling_job([stack(dwlo), stack(dwao), stack(dwo)]), _chips_job([pair_f1])])
    pair_lo, pair_ao, pair_o = [_pair_add(stack(g_), s_, c_idx, "rs_pair_add_" + nm_)
                                for g_, s_, nm_ in zip((dwlo, dwao, dwo), sib_3, ("w_lru_out", "w_attn_out", "w_out"))]
    (dlx, dlg, st_l, dwa, dwx), ((chips_lo, chips_ao, chips_o),) = _lru_bwd(
        proj, rec, dya, lp, wa, wx, "lru_bwd", jobs=[_chips_job([pair_lo, pair_ao, pair_o])])
    dproj = jnp.concatenate([dlx, dlg, dq, dk_pad[BLOCK:].astype(BF16), dv_pad[BLOCK:].astype(BF16), dga, dgb], axis=1)
    lru_mats = jnp.concatenate([dwa.reshape(-1, dd), dwx.reshape(-1, dd)], axis=0)
    (dwin_t,), ((mats_all,),) = _mm(dproj, h, "tn", tm=384, tn=tnw, tk=tkw, outs=[BF16], name="in_proj_dw",
                                   jobs=[_gather_job([lru_mats], (0.3, 0.6))])
    (mats_all,) = _fill_own([mats_all], [lru_mats])
    (sib_in,) = _exchange_sibling([stack(dwin_t)], "rs_sibling_w_in")
    pair_in = _pair_add(stack(dwin_t), sib_in, c_idx, "rs_pair_add_w_in")
    (dh,), ((chips_in,),) = _mm(dproj, win_t, "nn", tm=tmh, tn=tn, tk=IN_WIDTH // 2, outs=[BF16], name="in_proj_dx",
                              jobs=[_chips_job([pair_in])])
    grad_x, st_1 = _norm_mod_bwd(xs, norm1_g, mod, 0, dh, dx1, None, "norm1_bwd")

    drel = jnp.transpose(_rel_bias_bwd(dbias.reshape(N_Q_HEADS, -1), onehot, "rel_bias_bwd"))

    misc = jnp.concatenate([drel.reshape(1, -1), dsinks[0:1, 0:N_Q_HEADS],
                            jnp.zeros((1, dd - N_BUCKETS * N_Q_HEADS - N_Q_HEADS), F32)], axis=1)
    zero_row = jnp.zeros((1, dd), F32)
    small = jnp.concatenate([
        st_1[0:2], st_2[3:4], st_2[0:2], st_f[1:2], zero_row, zero_row,
        st_1[2:3], st_l[0:1], st_l[1:2], st_l[2:3], st_l[3:4], st_2[2:3], st_f[0:1], misc,
        st_l[4:8], jnp.zeros((4, dd), F32)], axis=0)
    (small_all,) = _all_gather([small], "ag_small")

    def pack_small(b_, n1, cb, ba, bx, lam, n2, fg, rb, sk):
        misc_ = jnp.concatenate([rb.reshape(1, -1), sk.reshape(1, -1),
                                 jnp.zeros((1, dd - N_BUCKETS * N_Q_HEADS - N_Q_HEADS), F32)], axis=1)
        return jnp.concatenate([b_.reshape(6, dd), jnp.zeros((2, dd), F32), n1, cb, ba, bx, lam, n2, _row(fg), misc_,
                                jnp.zeros((8, dd), F32)], axis=0)

    def pack_mats(wa_, wx_):
        return jnp.concatenate([wa_.reshape(-1, dd), wx_.reshape(-1, dd)], axis=0)

    every = {i: i for i in range(N_DEV)}
    w_s = pack_small(b_ada, norm1_g, conv_b, lru_ba, lru_bx, lru_lambda, norm2_g, final_g, rel_bias, attn_sinks)
    m_s = pack_small(m_b_ada, m_norm1_g, m_conv_b, m_lru_ba, m_lru_bx, m_lru_lambda, m_norm2_g, m_final_g, m_rel_bias, m_attn_sinks)
    v_s = pack_small(v_b_ada, v_norm1_g, v_conv_b, v_lru_ba, v_lru_bx, v_lru_lambda, v_norm2_g, v_final_g, v_rel_bias, v_attn_sinks)
    small_res = _adamw(w_s, m_s, v_s, [small_all] * N_DEV, "adamw_small", part_index=every)
    mats_res = _adamw(pack_mats(lru_wa, lru_wx), pack_mats(m_lru_wa, m_lru_wx), pack_mats(v_lru_wa, v_lru_wx),
                      [mats_all] * N_DEV, "adamw_lru_mats", part_index=every)

    def unpack_small(s, mt):
        nb_ = N_BUCKETS * N_Q_HEADS
        half = mt.shape[0] // 2
        return dict(
            b_ada=s[0:6].reshape(1, 6 * dd), norm1_g=s[8:9], conv_b=s[9:10], lru_ba=s[10:11], lru_bx=s[11:12],
            lru_lambda=s[12:13], norm2_g=s[13:14], final_g=s[14], rel_bias=s[15, 0:nb_].reshape(N_BUCKETS, N_Q_HEADS),
            attn_sinks=s[15:16, nb_:nb_ + N_Q_HEADS],
            lru_wa=mt[:half].reshape(1, LRU_BLOCKS, LRU_BLOCK_W, LRU_BLOCK_W),
            lru_wx=mt[half:].reshape(1, LRU_BLOCKS, LRU_BLOCK_W, LRU_BLOCK_W))

    res = {k: [None] * 4 for k in ("w_ada", "w_in", "conv_w", "w_lru_out", "w_attn_out", "w_out", "w_ff1", "w_ff2")}
    for q_, (s_, mt_) in enumerate(zip(small_res, mats_res)):
        for k_, val in unpack_small(s_, mt_).items():
            res.setdefault(k_, [None] * 4)[q_] = val

    g_conv = lax.dynamic_slice(small_res[0][16:16 + CONV_WIDTH], (0, me * cshard), (CONV_WIDTH, cshard))
    conv_res = _adamw(conv_w[0], m_conv_w[0], v_conv_w[0], [g_conv], "adamw_conv")
    res["conv_w"] = [r_[None] for r_ in conv_res]

    dmod_all = small_all[:, 0:6, :].reshape(N_DEV, 6 * dd)
    dmod_cols = lax.dynamic_slice(dmod_all, (0, me * ncol_ada), (N_DEV, ncol_ada))
    g_ada = _ada_bwd(jnp.transpose(c_all), dmod_cols, "ada_bwd")
    ada_res = _adamw(w_ada[0], m_w_ada[0], v_w_ada[0], [g_ada], "adamw_ada")
    res["w_ada"] = [r_[None] for r_ in ada_res]

    pair = [pair_in, pair_lo, pair_ao, pair_o, pair_f1, pair_f2]
    from_chips = [chips_in, chips_lo, chips_ao, chips_o, chips_f1, chips_f2]

    def summed(i):
        return [pair[i], from_chips[i], from_chips[i], from_chips[i]], {0: chip_idx, 1: 0, 2: 1, 3: 2}

    def sum_only(i, name):
        parts, index = summed(i)
        return _sum_parts(parts, index, name)

    g_in = jnp.transpose(sum_only(0, "sum_w_in"))
    res["w_in"] = [r_[None] for r_ in _adamw(w_in[0], m_w_in[0], v_w_in[0], [g_in], "adamw_w_in")]
    g_ff1 = jnp.transpose(sum_only(4, "sum_w_ff1"))
    res["w_ff1"] = [r_[None] for r_ in _adamw(w_ff1[0], m_w_ff1[0], v_w_ff1[0], [g_ff1], "adamw_w_ff1")]
    for i, (nm, w_, m_, v_) in {1: ("w_lru_out", w_lru_out, m_w_lru_out, v_w_lru_out),
                                2: ("w_attn_out", w_attn_out, m_w_attn_out, v_w_attn_out),
                                3: ("w_out", w_out, m_w_out, v_w_out),
                                5: ("w_ff2", w_ff2, m_w_ff2, v_w_ff2)}.items():
        parts, index = summed(i)
        res[nm] = [r_[None] for r_ in _adamw(w_[0], m_[0], v_[0], parts, "adamw_" + nm, part_index=index)]

    order = ["w_ada", "b_ada", "norm1_g", "w_in", "conv_w", "conv_b", "lru_wa", "lru_ba", "lru_wx", "lru_bx",
             "lru_lambda", "w_lru_out", "w_attn_out", "attn_sinks", "rel_bias", "w_out", "norm2_g", "w_ff1",
             "w_ff2", "final_g"]
    out = [loss, grad_x[None]]
    for q_ in range(4):
        out += [res[k_][q_] for k_ in order]
    return tuple(out)
```

```python
import functools
import math

import numpy as np
import jax
import jax.numpy as jnp
from jax import lax
from jax.experimental import pallas as pl
from jax.experimental.pallas import tpu as pltpu

F32 = jnp.float32
BF16 = jnp.bfloat16
MESH = pl.DeviceIdType.MESH

D_MODEL = 2048
N_Q_HEADS = 32
N_KV_HEADS = 4
GROUP = N_Q_HEADS // N_KV_HEADS
HEAD_DIM = 64
KV_WIDTH = N_KV_HEADS * HEAD_DIM
BLOCK = 128
NEG_INF = -1e30
N_BUCKETS = 32
MAX_DISTANCE = 128
LRU_BLOCKS = 16
LRU_BLOCK_W = 128
CONV_WIDTH = 4
LRU_C = 8.0
D_FF = 4 * D_MODEL
EPS = 1e-6
IN_WIDTH = 5 * D_MODEL + 2 * KV_WIDTH
OFF_LX, OFF_LG, OFF_Q, OFF_K, OFF_V, OFF_GA, OFF_GB = 0, 2048, 4096, 6144, 6400, 6656, 8704

ADAM_LR, ADAM_B1, ADAM_B2, ADAM_EPS, ADAM_WD, ADAM_STEP = 0.001, 0.9, 0.999, 1e-08, 0.01, 10

N_DEV = 8
VMEM_LIMIT_BYTES = 48 * 1024 * 1024
LANE = 128
LRU_CW = 512


def _params(sem, **kw):
    return pltpu.CompilerParams(dimension_semantics=sem, vmem_limit_bytes=VMEM_LIMIT_BYTES, **kw)


def _tile(n, pref, mult=8):
    if n <= pref:
        return n
    t = (pref // mult) * mult
    while t >= mult:
        if n % t == 0:
            return t
        t -= mult
    return n


def _sigmoid(x):
    return 0.5 * jnp.tanh(0.5 * x) + 0.5


def _gelu_tanh(x):
    k = math.sqrt(2.0 / math.pi)
    t = jnp.tanh(k * (x + 0.044715 * x * x * x))
    return 0.5 * x * (1.0 + t), t


def _gelu_tanh_grad(x, t):
    k = math.sqrt(2.0 / math.pi)
    return 0.5 * (1.0 + t) + 0.5 * x * (1.0 - t * t) * k * (1.0 + 3.0 * 0.044715 * x * x)


def _log1p(e):
    u = 1.0 + e
    return jnp.where(u == 1.0, e, jnp.log(u) * (e / jnp.where(u == 1.0, 1.0, u - 1.0)))


def _softplus(x):
    return jnp.maximum(x, 0.0) + _log1p(jnp.exp(-jnp.abs(x)))


def _neg_expm1(x, exp_x):
    series = -x * (1.0 + x * (0.5 + x * (1.0 / 6.0 + x * (1.0 / 24.0))))
    return jnp.where(x > -0.0625, series, 1.0 - exp_x)


def _my_place():
    return lax.axis_index("x"), lax.axis_index("y"), lax.axis_index("c")


def _all_gather(arrs, name):
    return _fill_own(_run_job(_gather_job(arrs), name), arrs)


def _fill_own(stacks, shards):
    me = 4 * lax.axis_index("x") + 2 * lax.axis_index("y") + lax.axis_index("c")
    out = []
    for g, s in zip(stacks, shards):
        if g.size * g.dtype.itemsize <= (1 << 20):
            slot = lax.broadcasted_iota(jnp.int32, g.shape, 0)
            out.append(jnp.where(slot == me, s[None], g))
        else:
            out.append(lax.dynamic_update_slice_in_dim(g, s[None], me, axis=0))
    return out


class _Job:
    def __init__(self, ins, outs, sems, start, finish, forwards=(), forward_at=()):
        self.ins, self.outs, self.sems = list(ins), list(outs), list(sems)
        self.start, self.finish, self.forwards, self.forward_at = start, finish, list(forwards), list(forward_at)
        assert len(self.forwards) <= len(self.forward_at)


def _gather_job(arrs, forward_at=(0.6, 0.85), part="all"):
    n = len(arrs)
    whole, near, far = part == "all", part == "near", part == "far"

    def copies(ins, outs, sems):
        send_sems, recv_sems, local_sems = sems
        x, y, c = _my_place()
        sib = (x, y, 1 - c)
        xn, yn, dg = (1 - x, y), (x, 1 - y), (1 - x, 1 - y)

        def slot(px, py, pc):
            return 4 * px + 2 * py + pc

        def cp(a, k, src, dst_slot, to):
            return pltpu.make_async_remote_copy(
                src_ref=src, dst_ref=outs[a].at[dst_slot], send_sem=send_sems.at[a, k], recv_sem=recv_sems.at[a, k],
                device_id=to, device_id_type=MESH)

        def arrival(a, k, dst_slot):
            return cp(a, k, outs[a].at[dst_slot], dst_slot, sib)

        me_slot = slot(x, y, c)
        sends, local, relay, passes, arrive = {}, [], {}, {}, {}
        for a in range(n):
            if whole:
                own = ins[a]
                sends[a, 0] = cp(a, 0, own, me_slot, sib)
            if near:
                own = ins[a].at[c]
            if whole or near:
                sends[a, 1] = cp(a, 1, own, me_slot, (*xn, c))
                sends[a, 2] = cp(a, 2, own, me_slot, (*yn, c))
                passes[a, 4] = cp(a, 4, outs[a].at[slot(*xn, c)], slot(*xn, c), sib)
                passes[a, 5] = cp(a, 5, outs[a].at[slot(*yn, c)], slot(*yn, c), sib)
            if whole or far:
                source = outs[a] if whole else ins[a]
                relayed = c * slot(*yn, c) + (1 - c) * slot(*xn, c)
                relay[a, 3] = cp(a, 3, source.at[relayed], relayed, (x + c * (1 - 2 * x), y + (1 - c) * (1 - 2 * y), c))
                passes[a, 6] = cp(a, 6, outs[a].at[slot(*dg, c)], slot(*dg, c), sib)
            arrive[a, 0] = arrival(a, 0, slot(x, y, 1 - c))
            arrive[a, 1] = arrival(a, 1, slot(*xn, c))
            arrive[a, 2] = arrival(a, 2, slot(*yn, c))
            arrive[a, 3] = arrival(a, 3, slot(*dg, c))
            arrive[a, 4] = arrival(a, 4, slot(*xn, 1 - c))
            arrive[a, 5] = arrival(a, 5, slot(*yn, 1 - c))
            arrive[a, 6] = arrival(a, 6, slot(*dg, 1 - c))
        return sends, local, relay, passes, arrive

    def pick(d, ks):
        return [d[a, k] for a in range(n) for k in ks if (a, k) in d]

    def start(ins, outs, sems):
        sends, local, relay, _, _ = copies(ins, outs, sems)
        for cp in local + pick(sends, (0, 1, 2)) + (pick(relay, (3,)) if far else []):
            cp.start()

    def forward_neighbours(ins, outs, sems):
        _, _, relay, passes, arrive = copies(ins, outs, sems)
        for cp in pick(arrive, (1, 2)):
            cp.wait_recv()
        for cp in pick(relay, (3,)) + pick(passes, (4, 5)):
            cp.start()

    def forward_diagonal(ins, outs, sems):
        _, _, _, passes, arrive = copies(ins, outs, sems)
        for cp in pick(arrive, (3,)):
            cp.wait_recv()
        for cp in pick(passes, (6,)):
            cp.start()

    def finish(ins, outs, sems):
        sends, local, relay, passes, arrive = copies(ins, outs, sems)
        for cp in pick(arrive, (6,) if far else (0, 4, 5, 6) if whole else (4, 5)):
            cp.wait_recv()
        for cp in pick(sends, (0, 1, 2)) + pick(relay, (3,)) + pick(passes, (4, 5, 6)):
            cp.wait_send()
        for cp in local:
            cp.wait()

    shapes = [a.shape[-2:] for a in arrs]
    forwards = [forward_diagonal] if far else [forward_neighbours] if near else [forward_neighbours, forward_diagonal]
    return _Job(arrs, [jax.ShapeDtypeStruct((N_DEV,) + s, a.dtype) for s, a in zip(shapes, arrs)],
                [pltpu.SemaphoreType.DMA((n, 7)), pltpu.SemaphoreType.DMA((n, 7)), pltpu.SemaphoreType.DMA((n,))],
                start, finish, forwards, forward_at)


def _pair_job(arrs):
    n = len(arrs)

    def copies(ins, outs, sems):
        send_sems, recv_sems, local_sems = sems
        x, y, c = _my_place()
        remote = [pltpu.make_async_remote_copy(
            src_ref=ins[a], dst_ref=outs[a].at[c], send_sem=send_sems.at[a], recv_sem=recv_sems.at[a],
            device_id=(x, y, 1 - c), device_id_type=MESH) for a in range(n)]
        local = []
        arrive = [pltpu.make_async_remote_copy(
            src_ref=ins[a], dst_ref=outs[a].at[1 - c], send_sem=send_sems.at[a], recv_sem=recv_sems.at[a],
            device_id=(x, y, 1 - c), device_id_type=MESH) for a in range(n)]
        return remote, local, arrive

    def start(ins, outs, sems):
        remote, local, _ = copies(ins, outs, sems)
        for cp in remote + local:
            cp.start()

    def finish(ins, outs, sems):
        remote, local, arrive = copies(ins, outs, sems)
        for cp in arrive:
            cp.wait_recv()
        for cp in remote:
            cp.wait_send()
        for cp in local:
            cp.wait()

    return _Job(arrs, [jax.ShapeDtypeStruct((2,) + a.shape, a.dtype) for a in arrs],
                [pltpu.SemaphoreType.DMA((n,)), pltpu.SemaphoreType.DMA((n,)), pltpu.SemaphoreType.DMA((n,))],
                start, finish)


def _run_job(job, name):
    ni, no = len(job.ins), len(job.outs)

    def body(*refs):
        ins, outs, sems = refs[:ni], refs[ni:ni + no], refs[ni + no:]
        job.start(ins, outs, sems)
        for fwd in job.forwards:
            fwd(ins, outs, sems)
        job.finish(ins, outs, sems)

    any_spec = pl.BlockSpec(memory_space=pl.ANY)
    return pl.pallas_call(
        body, name=name, out_shape=job.outs, in_specs=[any_spec] * ni, out_specs=[any_spec] * no,
        scratch_shapes=job.sems,
    )(*job.ins)


def _call(body, *, name, grid, in_specs, out_specs, out_shape, scratch_shapes=(), sem, args, jobs=(),
          prefetch=(), aliases=None):
    in_specs, out_specs, out_shape, scratch_shapes = list(in_specs), list(out_specs), list(out_shape), list(scratch_shapes)
    prefetch = list(prefetch)
    if not jobs and not prefetch and not aliases:
        res = pl.pallas_call(body, name=name, grid=grid, in_specs=in_specs, out_specs=out_specs, out_shape=out_shape,
                             scratch_shapes=scratch_shapes, compiler_params=_params(sem))(*args)
        return list(res), []
    n_in, n_out, n_scr = len(in_specs), len(out_specs), len(scratch_shapes)
    j_in = [len(j.ins) for j in jobs]
    j_out = [len(j.outs) for j in jobs]
    j_sem = [len(j.sems) for j in jobs]
    total = int(np.prod(grid))
    any_spec = pl.BlockSpec(memory_space=pl.ANY)

    def wrapped(*refs):
        pos = [0]

        def take(k):
            part = refs[pos[0]:pos[0] + k]
            pos[0] += k
            return part

        take(len(prefetch))
        ins = take(n_in)
        jins = [take(k) for k in j_in]
        outs = take(n_out)
        jouts = [take(k) for k in j_out]
        scr = take(n_scr)
        jsems = [take(k) for k in j_sem]
        step = pl.program_id(0)
        for d in range(1, len(grid)):
            step = step * grid[d] + pl.program_id(d)
        for j, job in enumerate(jobs):
            pl.when(step == 0)(functools.partial(job.start, jins[j], jouts[j], jsems[j]))
            for fwd, frac in zip(job.forwards, job.forward_at):
                at = min(int(total * frac), total - 1)
                pl.when(step == at)(functools.partial(fwd, jins[j], jouts[j], jsems[j]))
        body(*ins, *outs, *scr)
        for j, job in enumerate(jobs):
            pl.when(step == total - 1)(functools.partial(job.finish, jins[j], jouts[j], jsems[j]))

    grid_spec = pltpu.PrefetchScalarGridSpec(
        num_scalar_prefetch=len(prefetch), grid=grid,
        in_specs=in_specs + [any_spec] * sum(j_in),
        out_specs=out_specs + [any_spec] * sum(j_out),
        scratch_shapes=scratch_shapes + [s for j in jobs for s in j.sems])
    res = pl.pallas_call(
        wrapped, name=name, grid_spec=grid_spec,
        out_shape=out_shape + [o for j in jobs for o in j.outs],
        input_output_aliases={len(prefetch) + i: o for i, o in (aliases or {}).items()},
        compiler_params=_params(("arbitrary",) * len(grid)),
    )(*prefetch, *args, *[a for j in jobs for a in j.ins])
    res = list(res)
    own, rest = res[:n_out], res[n_out:]
    per_job = []
    for k in j_out:
        per_job.append(rest[:k])
        rest = rest[k:]
    return own, per_job


def _exchange_sibling(arrs, name):
    return _run_job(_sibling_job(arrs), name)


def _sibling_job(arrs):
    n = len(arrs)

    def copies(ins, outs, sems):
        send_sems, recv_sems = sems
        x, y, c = _my_place()
        return [pltpu.make_async_remote_copy(
            src_ref=ins[a].at[2 * k + 1 - c], dst_ref=outs[a].at[k],
            send_sem=send_sems.at[a, k], recv_sem=recv_sems.at[a, k],
            device_id=(x, y, 1 - c), device_id_type=MESH) for a in range(n) for k in range(4)]

    def start(ins, outs, sems):
        for cp in copies(ins, outs, sems):
            cp.start()

    def finish(ins, outs, sems):
        for cp in copies(ins, outs, sems):
            cp.wait()

    return _Job(arrs, [jax.ShapeDtypeStruct((4,) + a.shape[1:], a.dtype) for a in arrs],
                [pltpu.SemaphoreType.DMA((n, 4)), pltpu.SemaphoreType.DMA((n, 4))], start, finish)


def _chips_job(arrs):
    n = len(arrs)

    def copies(ins, outs, sems):
        send_sems, recv_sems = sems
        x, y, c = _my_place()
        chips = [(1 - x, y), (x, 1 - y), (1 - x, 1 - y)]
        return [pltpu.make_async_remote_copy(
            src_ref=ins[a].at[2 * px + py], dst_ref=outs[a].at[j],
            send_sem=send_sems.at[a, j], recv_sem=recv_sems.at[a, j],
            device_id=(px, py, c), device_id_type=MESH) for a in range(n) for j, (px, py) in enumerate(chips)]

    def start(ins, outs, sems):
        for cp in copies(ins, outs, sems):
            cp.start()

    def finish(ins, outs, sems):
        for cp in copies(ins, outs, sems):
            cp.wait()

    return _Job(arrs, [jax.ShapeDtypeStruct((3,) + a.shape[1:], a.dtype) for a in arrs],
                [pltpu.SemaphoreType.DMA((n, 3)), pltpu.SemaphoreType.DMA((n, 3))], start, finish)


def _pair_add(stack, from_sibling, c_idx, name, halves=False):
    _, r, cc = stack.shape
    tr = _tile(r, 512)
    nout = 2 if halves else 1
    wo = cc // nout

    def body(c_ref, a_ref, b_ref, *o_refs):
        res = a_ref[...].astype(F32) + b_ref[...].astype(F32)
        for q, o_ref in enumerate(o_refs):
            o_ref[...] = res[:, q * wo:(q + 1) * wo].astype(o_ref.dtype)

    grid_spec = pltpu.PrefetchScalarGridSpec(
        num_scalar_prefetch=1, grid=(4, r // tr),
        in_specs=[pl.BlockSpec((None, tr, cc), lambda k, i, c_ref: (2 * k + c_ref[0], i, 0)),
                  pl.BlockSpec((None, tr, cc), lambda k, i, c_ref: (k, i, 0))],
        out_specs=[pl.BlockSpec((None, tr, wo), lambda k, i, c_ref: (k, i, 0))] * nout)
    res = pl.pallas_call(
        body, name=name, grid_spec=grid_spec,
        out_shape=[jax.ShapeDtypeStruct((4, r, wo), BF16)] * nout,
        compiler_params=_params(("parallel", "parallel")),
    )(c_idx, stack, from_sibling)
    return res if halves else res[0]


_DIMS = {"nn": ((1,), (0,)), "nt": ((1,), (1,)), "tn": ((0,), (0,))}


def _mm(a, b, mode, *, tm, tn, tk=None, outs, epi=None, tiles=(), rows=(), name, jobs=(), chunks=None, into=None,
        n_total=None, rows_part=None):
    if mode == "tn":
        kk, m = a.shape
    else:
        m, kk = a.shape
    n_all = b.shape[0] if mode == "nt" else b.shape[1]
    nn = n_all if chunks is None else chunks[0].shape[0] * chunks[1]
    n_all = n_all if n_total is None else n_total
    tk = kk if tk is None else tk
    assert m % tm == 0 and nn % tn == 0 and kk % tk == 0, (name, a.shape, b.shape, tm, tn, tk)
    nk = kk // tk
    nt, nr, no = len(tiles), len(rows), len(outs)
    n_into = 0 if into is None else 1

    def body(*refs):
        a_ref, b_ref = refs[:2]
        tile_refs = refs[2:2 + nt]
        row_refs = refs[2 + nt:2 + nt + nr]
        out_refs = refs[2 + nt + nr + n_into:2 + nt + nr + n_into + no]
        part = lax.dot_general(a_ref[...], b_ref[...], (_DIMS[mode], ((), ())), preferred_element_type=F32)

        def finish(acc):
            if epi is None:
                res = (acc,)
            else:
                res = epi(acc, *[t[...] for t in tile_refs], *[r[...] for r in row_refs])
            for o_ref, val in zip(out_refs, res):
                o_ref[...] = val.astype(o_ref.dtype)

        if nk == 1:
            finish(part)
        else:
            acc_ref = refs[-1]
            k = pl.program_id(2)

            @pl.when(k == 0)
            def _():
                acc_ref[...] = part

            @pl.when(k > 0)
            def _():
                acc_ref[...] += part

            @pl.when(k == nk - 1)
            def _():
                finish(acc_ref[...])

    if chunks is None:
        col = b_row = lambda j, *_: j
        prefetch = []
    else:
        assert mode == "nt" and not tiles and not rows and chunks[1] % tn == 0
        per = chunks[1] // tn
        col = lambda j, ids, b_ids: ids[j // per] * per + j % per
        b_row = lambda j, ids, b_ids: b_ids[j // per] * per + j % per
        prefetch = [chunks[0], chunks[2]]
    row0, n_row_blocks = (0, m // tm) if rows_part is None else rows_part
    assert rows_part is None or (mode != "tn" and not tiles)
    if mode == "tn":
        a_spec = pl.BlockSpec((tk, tm), lambda i, j, k, *_: (k, i))
    else:
        a_spec = pl.BlockSpec((tm, tk), lambda i, j, k, *_: (i + row0, k))
    if mode == "nt":
        b_spec = pl.BlockSpec((tn, tk), lambda i, j, k, *s: (b_row(j, *s), k))
    else:
        b_spec = pl.BlockSpec((tk, tn), lambda i, j, k, *_: (k, j))
    tile_specs = [pl.BlockSpec((tm, tn), functools.partial(lambda i, j, k, off: (i, j + off), off=off))
                  for _, off in tiles]
    row_specs = [pl.BlockSpec((r.shape[0], tn), lambda i, j, k: (0, j)) for r in rows]
    out_spec = pl.BlockSpec((tm, tn), lambda i, j, k, *s: (i + row0, col(j, *s)))
    args = [a, b, *[t for t, _ in tiles], *rows]
    in_specs = [a_spec, b_spec] + tile_specs + row_specs
    aliases = None
    if into is not None:
        aliases = {len(args): 0}
        args.append(into)
        in_specs.append(pl.BlockSpec(memory_space=pl.ANY))
    res, jres = _call(
        body, name=name, grid=(n_row_blocks, nn // tn, nk),
        in_specs=in_specs, out_specs=[out_spec] * no,
        out_shape=[jax.ShapeDtypeStruct((m, n_all), dt) for dt in outs],
        scratch_shapes=[pltpu.VMEM((tm, tn), F32)] if nk > 1 else [],
        sem=("parallel", "parallel", "arbitrary"),
        args=args, jobs=jobs, prefetch=prefetch, aliases=aliases)
    return (res, jres) if jobs else res


def _ada_fwd(c_all, w_ada, b_ada_cols, name):
    nb, dd = c_all.shape
    ncol = w_ada.shape[1]
    tn = _tile(ncol, 512, LANE)

    def body(c_ref, w_ref, b_ref, o_ref):
        cv = c_ref[...]
        act = (cv * _sigmoid(cv)).astype(BF16)
        o_ref[...] = jnp.dot(act, w_ref[...].astype(BF16), preferred_element_type=F32) + b_ref[...]

    return pl.pallas_call(
        body, name=name, grid=(ncol // tn,),
        in_specs=[pl.BlockSpec((nb, dd), lambda j: (0, 0)), pl.BlockSpec((dd, tn), lambda j: (0, j)),
                  pl.BlockSpec((1, tn), lambda j: (0, j))],
        out_specs=pl.BlockSpec((nb, tn), lambda j: (0, j)),
        out_shape=jax.ShapeDtypeStruct((nb, ncol), F32),
        compiler_params=_params(("parallel",)),
    )(c_all, w_ada, b_ada_cols)


def _ada_bwd(c_all_t, dmod_cols, name):
    dd, nb = c_all_t.shape
    ncol = dmod_cols.shape[1]
    tr = _tile(dd, 256)

    def body(c_ref, d_ref, o_ref):
        cv = c_ref[...]
        act = (cv * _sigmoid(cv)).astype(BF16).astype(F32)
        dm = d_ref[...].astype(BF16).astype(F32)
        acc = act[:, 0:1] * dm[0:1, :]
        for bi in range(1, nb):
            acc = acc + act[:, bi:bi + 1] * dm[bi:bi + 1, :]
        o_ref[...] = acc

    return pl.pallas_call(
        body, name=name, grid=(dd // tr,),
        in_specs=[pl.BlockSpec((tr, nb), lambda i: (i, 0)), pl.BlockSpec((nb, ncol), lambda i: (0, 0))],
        out_specs=pl.BlockSpec((tr, ncol), lambda i: (i, 0)),
        out_shape=jax.ShapeDtypeStruct((dd, ncol), F32),
        compiler_params=_params(("parallel",)),
    )(c_all_t, dmod_cols)


def _norm_mod(x, gain, mod, row0, name):
    t, dd = x.shape
    tt = _tile(t, 512)

    def body(x_ref, g_ref, m_ref, h_ref):
        xv = x_ref[...]
        r = lax.rsqrt(jnp.mean(xv * xv, axis=-1, keepdims=True) + EPS)
        shift, scale = m_ref[row0:row0 + 1, :], m_ref[row0 + 1:row0 + 2, :]
        h_ref[...] = ((xv * r * g_ref[...]) * (1.0 + scale) + shift).astype(BF16)

    return pl.pallas_call(
        body, name=name, grid=(t // tt,),
        in_specs=[pl.BlockSpec((tt, dd), lambda i: (i, 0)), pl.BlockSpec((1, dd), lambda i: (0, 0)),
                  pl.BlockSpec((8, dd), lambda i: (0, 0))],
        out_specs=pl.BlockSpec((tt, dd), lambda i: (i, 0)),
        out_shape=jax.ShapeDtypeStruct((t, dd), BF16),
        compiler_params=_params(("parallel",)),
    )(x, gain, mod)


def _norm_mod_bwd(x, gain, mod, row0, dh, dres, branch, name, jobs=()):
    t, dd = x.shape
    tt = _tile(t, 256)
    has_branch = branch is not None

    def body(*refs):
        if has_branch:
            x_ref, g_ref, m_ref, dh_ref, dres_ref, o_ref, dx_ref, st_ref, do_ref = refs
        else:
            x_ref, g_ref, m_ref, dh_ref, dres_ref, dx_ref, st_ref = refs

        @pl.when(pl.program_id(0) == 0)
        def _():
            st_ref[...] = jnp.zeros_like(st_ref)

        xv = x_ref[...]
        g = g_ref[...]
        scale = m_ref[row0 + 1:row0 + 2, :]
        r = lax.rsqrt(jnp.mean(xv * xv, axis=-1, keepdims=True) + EPS)
        nv = xv * r
        dhv = dh_ref[...].astype(F32)
        st_ref[0:1, :] += jnp.sum(dhv, axis=0, keepdims=True)
        st_ref[1:2, :] += jnp.sum(dhv * (nv * g), axis=0, keepdims=True)
        dng = dhv * (1.0 + scale)
        st_ref[2:3, :] += jnp.sum(dng * nv, axis=0, keepdims=True)
        dn = dng * g
        dx = dres_ref[...] + r * (dn - nv * jnp.mean(dn * nv, axis=-1, keepdims=True))
        dx_ref[...] = dx
        if has_branch:
            ov = o_ref[...].astype(F32)
            gate = m_ref[branch[1]:branch[1] + 1, :]
            st_ref[3:4, :] += jnp.sum(dx * ov, axis=0, keepdims=True)
            do_ref[...] = (dx * gate).astype(BF16)

    row = pl.BlockSpec((tt, dd), lambda i: (i, 0))
    in_specs = [row, pl.BlockSpec((1, dd), lambda i: (0, 0)), pl.BlockSpec((8, dd), lambda i: (0, 0)), row, row]
    args = [x, gain, mod, dh, dres]
    out_specs = [row, pl.BlockSpec((8, dd), lambda i: (0, 0))]
    out_shape = [jax.ShapeDtypeStruct((t, dd), F32), jax.ShapeDtypeStruct((8, dd), F32)]
    if has_branch:
        in_specs.append(row)
        args.append(branch[0])
        out_specs.append(row)
        out_shape.append(jax.ShapeDtypeStruct((t, dd), BF16))
    res, jres = _call(body, name=name, grid=(t // tt,), in_specs=in_specs, out_specs=out_specs, out_shape=out_shape,
                      sem=("arbitrary",), args=args, jobs=jobs)
    return (res, jres) if jobs else res


def _final_loss(x2, o2, final_g, target, mod, name):
    t, dd = x2.shape
    tt = _tile(t, 256)
    nsteps = t // tt

    def body(x_ref, o_ref, g_ref, t_ref, m_ref, loss_ref, dx_ref, do_ref, st_ref, lacc):
        i = pl.program_id(0)

        @pl.when(i == 0)
        def _():
            st_ref[...] = jnp.zeros_like(st_ref)
            lacc[...] = jnp.zeros_like(lacc)

        xv = x_ref[...]
        g = g_ref[...]
        r = lax.rsqrt(jnp.mean(xv * xv, axis=-1, keepdims=True) + EPS)
        nv = xv * r
        err = nv * g - t_ref[...]
        lacc[...] += jnp.sum(err * err, axis=0, keepdims=True)
        dy = err * (1.0 / dd)
        st_ref[0:1, :] += jnp.sum(dy * nv, axis=0, keepdims=True)
        dn = dy * g
        dx = r * (dn - nv * jnp.mean(dn * nv, axis=-1, keepdims=True))
        dx_ref[...] = dx
        st_ref[1:2, :] += jnp.sum(dx * o_ref[...].astype(F32), axis=0, keepdims=True)
        do_ref[...] = (dx * m_ref[5:6, :]).astype(BF16)

        @pl.when(i == nsteps - 1)
        def _():
            total = jnp.sum(lacc[...], axis=-1, keepdims=True) * (0.5 / dd)
            loss_ref[...] = jnp.broadcast_to(total, loss_ref.shape)

    row = pl.BlockSpec((tt, dd), lambda i: (i, 0))
    return pl.pallas_call(
        body, name=name, grid=(nsteps,),
        in_specs=[row, row, pl.BlockSpec((1, dd), lambda i: (0, 0)), row, pl.BlockSpec((8, dd), lambda i: (0, 0))],
        out_specs=[pl.BlockSpec((8, LANE), lambda i: (0, 0)), row, row, pl.BlockSpec((8, dd), lambda i: (0, 0))],
        out_shape=[jax.ShapeDtypeStruct((8, LANE), F32), jax.ShapeDtypeStruct((t, dd), F32),
                   jax.ShapeDtypeStruct((t, dd), BF16), jax.ShapeDtypeStruct((8, dd), F32)],
        scratch_shapes=[pltpu.VMEM((1, dd), F32)],
        compiler_params=_params(("arbitrary",)),
    )(x2, o2, final_g, target, mod)


def _lru_gates(xt, prev8, lp, wa_ref, wx_ref, first_tile, cw):
    tt = xt.shape[0]
    ext = jnp.concatenate([prev8, xt], axis=0)
    xs = [pltpu.roll(ext, s, 0)[8:, :] if s else xt for s in range(CONV_WIDTH)]
    xc = lp[0:1, :] + lp[7:8, :] * xs[0] + lp[6:7, :] * xs[1] + lp[5:6, :] * xs[2] + lp[4:5, :] * xs[3]
    xcb = xc.astype(BF16)
    za, zi = [], []
    for jb in range(cw // LRU_BLOCK_W):
        sl = slice(jb * LRU_BLOCK_W, (jb + 1) * LRU_BLOCK_W)
        za.append(jnp.dot(xcb[:, sl], wa_ref[jb].astype(BF16), preferred_element_type=F32))
        zi.append(jnp.dot(xcb[:, sl], wx_ref[jb].astype(BF16), preferred_element_type=F32))
    ra = _sigmoid(jnp.concatenate(za, axis=1) + lp[1:2, :])
    ri = _sigmoid(jnp.concatenate(zi, axis=1) + lp[2:3, :])
    sp = _softplus(-lp[3:4, :])
    log_a = -LRU_C * ra * sp
    av = jnp.exp(log_a)
    seq_start = jnp.logical_and(first_tile, lax.broadcasted_iota(jnp.int32, (tt, 1), 0) == 0)
    mult = jnp.where(seq_start, 1.0, jnp.sqrt(_neg_expm1(2.0 * log_a, av * av)))
    return xs, xc, xcb, ra, ri, sp, av, mult, seq_start


def _scan_groups(a, u, reverse):
    tt, cw = a.shape
    a3, u3 = a.reshape(tt // 8, 8, cw), u.reshape(tt // 8, 8, cw)
    r = lax.broadcasted_iota(jnp.int32, (1, 8, 1), 1)
    for s in (1, 2, 4):
        shift = 8 - s if reverse else s
        take = r < 8 - s if reverse else r >= s
        u3 = jnp.where(take, a3 * pltpu.roll(u3, shift, 1) + u3, u3)
        a3 = jnp.where(take, a3 * pltpu.roll(a3, shift, 1), a3)
    return a3, u3


def _scan_carry(a_s, u_s, h_s, carry, reverse):
    ng = a_s.shape[0]

    def step(i, h):
        g = ng - 1 - i if reverse else i
        hg = u_s[g] + a_s[g] * h
        h_s[g] = hg
        return hg[0:1, :] if reverse else hg[7:8, :]

    return lax.fori_loop(0, ng, step, carry, unroll=4)


def _lru_fwd(proj, lp, wa, wx, name, jobs=()):
    t = proj.shape[0]
    w = D_MODEL
    cw = LRU_CW
    tt = _tile(t, 256)
    nb = cw // LRU_BLOCK_W

    def body(lx_ref, lg_ref, lp_ref, wa_ref, wx_ref, rec_ref, ya_ref, a_s, u_s, h_s, halo, carry):
        ti = pl.program_id(1)

        @pl.when(ti == 0)
        def _():
            halo[...] = jnp.zeros_like(halo)
            carry[...] = jnp.zeros_like(carry)

        xt = lx_ref[...].astype(F32)
        lp_v = lp_ref[...]
        _, xc, _, _, ri, _, av, mult, _ = _lru_gates(xt, halo[...], lp_v, wa_ref, wx_ref, ti == 0, cw)
        halo[...] = xt[tt - 8:, :]
        a_s[...], u_s[...] = _scan_groups(av, mult * (ri * xc), False)
        carry[...] = _scan_carry(a_s, u_s, h_s, carry[...], False)
        rec = h_s[...].reshape(tt, cw)
        rec_ref[...] = rec.astype(BF16)
        gelu, _ = _gelu_tanh(lg_ref[...].astype(F32))
        ya_ref[...] = (rec * gelu).astype(BF16)

    off_g = OFF_LG // cw
    res, jres = _call(
        body, name=name, grid=(w // cw, t // tt),
        in_specs=[pl.BlockSpec((tt, cw), lambda ci, ti: (ti, ci)),
                  pl.BlockSpec((tt, cw), lambda ci, ti: (ti, off_g + ci)),
                  pl.BlockSpec((8, cw), lambda ci, ti: (0, ci)),
                  pl.BlockSpec((nb, LRU_BLOCK_W, LRU_BLOCK_W), lambda ci, ti: (ci, 0, 0)),
                  pl.BlockSpec((nb, LRU_BLOCK_W, LRU_BLOCK_W), lambda ci, ti: (ci, 0, 0))],
        out_specs=[pl.BlockSpec((tt, cw), lambda ci, ti: (ti, ci))] * 2,
        out_shape=[jax.ShapeDtypeStruct((t, w), BF16)] * 2,
        scratch_shapes=[pltpu.VMEM((tt // 8, 8, cw), F32)] * 3 + [pltpu.VMEM((8, cw), F32), pltpu.VMEM((1, cw), F32)],
        sem=("parallel", "arbitrary"), args=[proj, proj, lp, wa, wx], jobs=jobs)
    return (res, jres) if jobs else res


def _lru_bwd(proj, rec, dya, lp, wa, wx, name, jobs=()):
    t = proj.shape[0]
    w = D_MODEL
    cw = LRU_CW
    tt = _tile(t, 256)
    nt = t // tt
    nb = cw // LRU_BLOCK_W
    r8 = tt // 8

    def body(lx_ref, lxp_ref, lg_ref, rec_ref, recp_ref, dya_ref, lp_ref, wa_ref, wx_ref,
             dlx_ref, dlg_ref, st_ref, dwa_ref, dwx_ref, a_s, d_s, dh_s, nhalo, carry):
        step_i = pl.program_id(1)
        ti = nt - 1 - step_i

        @pl.when(step_i == 0)
        def _():
            st_ref[...] = jnp.zeros_like(st_ref)
            dwa_ref[...] = jnp.zeros_like(dwa_ref)
            dwx_ref[...] = jnp.zeros_like(dwx_ref)
            nhalo[...] = jnp.zeros_like(nhalo)
            carry[...] = jnp.zeros_like(carry)

        first = ti == 0
        keep = jnp.where(first, 0.0, 1.0)
        xt = lx_ref[...].astype(F32)
        prev8 = lxp_ref[...].astype(F32) * keep
        lp_v = lp_ref[...]
        xs, xc, xcb, ra, ri, sp, av, mult, seq_start = _lru_gates(xt, prev8, lp_v, wa_ref, wx_ref, first, cw)

        lg = lg_ref[...].astype(F32)
        gelu, th = _gelu_tanh(lg)
        dyav = dya_ref[...].astype(F32)
        recv = rec_ref[...].astype(F32)
        dlg_ref[...] = (dyav * recv * _gelu_tanh_grad(lg, th)).astype(BF16)

        drec = dyav * gelu
        e_in = carry[...]
        a_s[...], d_s[...] = _scan_groups(av, av * drec, True)
        carry[...] = _scan_carry(a_s, d_s, dh_s, e_in, True)
        e_next = jnp.concatenate([dh_s[...].reshape(tt, cw), jnp.broadcast_to(e_in, (8, cw))], axis=0)
        dh = drec + pltpu.roll(e_next, tt + 7, 0)[:tt, :]
        rprev8 = recp_ref[...].astype(F32) * keep
        hprev = pltpu.roll(jnp.concatenate([rprev8, recv], axis=0), 1, 0)[8:, :]
        da = dh * hprev
        dmult = jnp.where(seq_start, 0.0, dh * ri * xc)
        dri = dh * mult * xc
        dxc = dh * mult * ri
        dlog_a = da * av - dmult * (av * av) / mult
        dra = dlog_a * (-LRU_C * sp)
        lam = lp_v[3:4, :]
        st_ref[3:4, :] += jnp.sum(dlog_a * ra, axis=0, keepdims=True) * (LRU_C * _sigmoid(-lam))
        dza = dra * ra * (1.0 - ra)
        dzi = dri * ri * (1.0 - ri)
        st_ref[1:2, :] += jnp.sum(dza, axis=0, keepdims=True)
        st_ref[2:3, :] += jnp.sum(dzi, axis=0, keepdims=True)
        dzab, dzib = dza.astype(BF16), dzi.astype(BF16)
        back = []
        for jb in range(nb):
            sl = slice(jb * LRU_BLOCK_W, (jb + 1) * LRU_BLOCK_W)
            dwa_ref[jb] += lax.dot_general(xcb[:, sl], dzab[:, sl], (_DIMS["tn"], ((), ())), preferred_element_type=F32)
            dwx_ref[jb] += lax.dot_general(xcb[:, sl], dzib[:, sl], (_DIMS["tn"], ((), ())), preferred_element_type=F32)
            back.append(
                lax.dot_general(dzab[:, sl], wa_ref[jb].astype(BF16), (_DIMS["nt"], ((), ())), preferred_element_type=F32)
                + lax.dot_general(dzib[:, sl], wx_ref[jb].astype(BF16), (_DIMS["nt"], ((), ())), preferred_element_type=F32))
        dxc = dxc + jnp.concatenate(back, axis=1)
        st_ref[0:1, :] += jnp.sum(dxc, axis=0, keepdims=True)
        for k in range(CONV_WIDTH):
            st_ref[4 + k:5 + k, :] += jnp.sum(dxc * xs[CONV_WIDTH - 1 - k], axis=0, keepdims=True)
        ext = jnp.concatenate([dxc, nhalo[...]], axis=0)
        dlx = lp_v[7:8, :] * dxc
        for s in range(1, CONV_WIDTH):
            dlx = dlx + lp_v[7 - s:8 - s, :] * pltpu.roll(ext, tt + 8 - s, 0)[:tt, :]
        dlx_ref[...] = dlx.astype(BF16)
        nhalo[...] = dxc[0:8, :]

    off_g = OFF_LG // cw
    tile = lambda off: pl.BlockSpec((tt, cw), lambda ci, si: (nt - 1 - si, off + ci))
    prev = lambda off: pl.BlockSpec((8, cw), lambda ci, si: (jnp.maximum((nt - 1 - si) * r8 - 1, 0), off + ci))
    wspec = pl.BlockSpec((nb, LRU_BLOCK_W, LRU_BLOCK_W), lambda ci, si: (ci, 0, 0))
    st_spec = pl.BlockSpec((8, cw), lambda ci, si: (0, ci))
    res, jres = _call(
        body, name=name, grid=(w // cw, nt),
        in_specs=[tile(0), prev(0), tile(off_g), tile(0), prev(0), tile(0), st_spec, wspec, wspec],
        out_specs=[tile(0), tile(0), st_spec, wspec, wspec],
        out_shape=[jax.ShapeDtypeStruct((t, w), BF16), jax.ShapeDtypeStruct((t, w), BF16),
                   jax.ShapeDtypeStruct((8, w), F32),
                   jax.ShapeDtypeStruct((LRU_BLOCKS, LRU_BLOCK_W, LRU_BLOCK_W), F32),
                   jax.ShapeDtypeStruct((LRU_BLOCKS, LRU_BLOCK_W, LRU_BLOCK_W), F32)],
        scratch_shapes=[pltpu.VMEM((tt // 8, 8, cw), F32)] * 3 + [pltpu.VMEM((8, cw), F32), pltpu.VMEM((1, cw), F32)],
        sem=("parallel", "arbitrary"), args=[proj, proj, proj, rec, rec, dya, lp, wa, wx], jobs=jobs)
    return (res, jres) if jobs else res


def _band_valid(first_block):
    qi = lax.broadcasted_iota(jnp.int32, (BLOCK, 2 * BLOCK), 0)
    ki = lax.broadcasted_iota(jnp.int32, (BLOCK, 2 * BLOCK), 1)
    rel = qi + BLOCK - ki
    valid = jnp.logical_and(rel >= 0, rel < BLOCK)
    return jnp.logical_and(valid, jnp.logical_or(ki >= BLOCK, jnp.logical_not(first_block)))


ATTN_STACK_BWD = 8


def _low_half():
    return lax.broadcasted_iota(jnp.int32, (1, LANE), 1) < HEAD_DIM


def _stack_heads(x, h0, ns):
    low = _low_half()
    parts = []
    for g in range(ns):
        h = h0 + g
        slab = x[:, (h // 2) * LANE:(h // 2 + 1) * LANE]
        parts.append(jnp.where(low if h % 2 == 0 else jnp.logical_not(low), slab, jnp.zeros_like(slab)))
    return jnp.concatenate(parts, axis=0)


def _unstack_heads(y, ns):
    low = _low_half()
    pairs = [jnp.where(low, y[(2 * j) * BLOCK:(2 * j + 1) * BLOCK], y[(2 * j + 1) * BLOCK:(2 * j + 2) * BLOCK])
             for j in range(ns // 2)]
    return pairs[0] if len(pairs) == 1 else jnp.concatenate(pairs, axis=1)


def _dup_kv(xall, kv):
    part = xall[:, kv * HEAD_DIM:(kv + 1) * HEAD_DIM]
    return jnp.concatenate([part, part], axis=1)


def _fold_halves(a):
    return a + pltpu.roll(a, HEAD_DIM, 1)


def _group_probs(qg, k2, bias_g, sink_g, lse_g, valid):
    s = lax.dot_general(qg, k2, (_DIMS["nt"], ((), ())), preferred_element_type=F32) * (HEAD_DIM ** -0.5)
    s = jnp.where(valid[None], s.reshape(bias_g.shape) + bias_g, NEG_INF)
    return jnp.exp(s - lse_g), jnp.exp(sink_g - lse_g)


def _group_sinks(s_ref, h0, ns):
    return jnp.concatenate([jnp.full((1, BLOCK, 1), s_ref[h0 + g], F32) for g in range(ns)], axis=0)


def _attn_specs(t):
    nblk = t // BLOCK
    q_spec = pl.BlockSpec((BLOCK, D_MODEL), lambda n: (n, OFF_Q // D_MODEL))
    kc = pl.BlockSpec((BLOCK, KV_WIDTH), lambda n: (n, OFF_K // KV_WIDTH))
    kp = pl.BlockSpec((BLOCK, KV_WIDTH), lambda n: (jnp.maximum(n - 1, 0), OFF_K // KV_WIDTH))
    vc = pl.BlockSpec((BLOCK, KV_WIDTH), lambda n: (n, OFF_V // KV_WIDTH))
    vp = pl.BlockSpec((BLOCK, KV_WIDTH), lambda n: (jnp.maximum(n - 1, 0), OFF_V // KV_WIDTH))
    bias_spec = pl.BlockSpec((N_Q_HEADS, BLOCK, 2 * BLOCK), lambda n: (0, 0, 0))
    sink_spec = pl.BlockSpec(memory_space=pltpu.SMEM)
    return nblk, [q_spec, kp, kc, vp, vc, bias_spec, sink_spec]


def _attn_fwd(proj, bias, sinks, name, jobs=()):
    t = proj.shape[0]
    nblk, in_specs = _attn_specs(t)

    def body(q_ref, kp_ref, kc_ref, vp_ref, vc_ref, b_ref, s_ref, o_ref, lse_ref):
        valid = _band_valid(pl.program_id(0) == 0)
        lane = lax.broadcasted_iota(jnp.int32, (1, LANE), 1)
        q = q_ref[...]
        kall = jnp.concatenate([kp_ref[...], kc_ref[...]], axis=0)
        vall = jnp.concatenate([vp_ref[...], vc_ref[...]], axis=0)
        outs = []
        lse = jnp.zeros((BLOCK, LANE), F32)
        for h in range(N_Q_HEADS):
            kv = h // GROUP
            kk = kall[:, kv * HEAD_DIM:(kv + 1) * HEAD_DIM]
            vv = vall[:, kv * HEAD_DIM:(kv + 1) * HEAD_DIM]
            s = lax.dot_general(q[:, h * HEAD_DIM:(h + 1) * HEAD_DIM], kk, (_DIMS["nt"], ((), ())),
                                preferred_element_type=F32) * (HEAD_DIM ** -0.5)
            s = jnp.where(valid, s + b_ref[h], NEG_INF)
            sink = s_ref[h]
            m = jnp.maximum(jnp.max(s, axis=-1, keepdims=True), sink)
            e = jnp.exp(s - m)
            den = jnp.sum(e, axis=-1, keepdims=True) + jnp.exp(sink - m)
            lse = jnp.where(lane == h, m + jnp.log(den), lse)
            outs.append(jnp.dot((e * (1.0 / den)).astype(BF16), vv, preferred_element_type=F32))
        o_ref[...] = jnp.concatenate(outs, axis=1).astype(BF16)
        lse_ref[...] = lse

    res, jres = _call(
        body, name=name, grid=(nblk,), in_specs=in_specs,
        out_specs=[pl.BlockSpec((BLOCK, D_MODEL), lambda n: (n, 0)), pl.BlockSpec((BLOCK, LANE), lambda n: (n, 0))],
        out_shape=[jax.ShapeDtypeStruct((t, D_MODEL), BF16), jax.ShapeDtypeStruct((t, LANE), F32)],
        sem=("parallel",), args=[proj, proj, proj, proj, proj, bias, sinks], jobs=jobs)
    return (res, jres) if jobs else res


def _attn_bwd(proj, datt, lse, bias, sinks, name, jobs=()):
    t = proj.shape[0]
    nblk, in_specs = _attn_specs(t)
    in_specs = in_specs + [pl.BlockSpec((BLOCK, D_MODEL), lambda n: (n, 0)), pl.BlockSpec((BLOCK, LANE), lambda n: (n, 0))]
    scale = HEAD_DIM ** -0.5

    def body(q_ref, kp_ref, kc_ref, vp_ref, vc_ref, b_ref, s_ref, do_ref, lse_ref,
             dq_ref, dk_ref, dv_ref, db_ref, ds_ref):
        n = pl.program_id(0)

        @pl.when(n == 0)
        def _():
            dk_ref[...] = jnp.zeros_like(dk_ref)
            dv_ref[...] = jnp.zeros_like(dv_ref)
            db_ref[...] = jnp.zeros_like(db_ref)
            ds_ref[...] = jnp.zeros_like(ds_ref)

        valid = _band_valid(n == 0)
        lane = lax.broadcasted_iota(jnp.int32, (1, LANE), 1)
        q = q_ref[...]
        do = do_ref[...]
        lse_all = lse_ref[...]
        kall = jnp.concatenate([kp_ref[...], kc_ref[...]], axis=0)
        vall = jnp.concatenate([vp_ref[...], vc_ref[...]], axis=0)
        low = _low_half()
        dqs, dks, dvs = [], [], []
        dsink_row = jnp.zeros((1, LANE), F32)
        ns = ATTN_STACK_BWD
        for kv in range(N_KV_HEADS):
            k2, v2 = _dup_kv(kall, kv), _dup_kv(vall, kv)
            dk_acc = jnp.zeros((2 * BLOCK, LANE), F32)
            dv_acc = jnp.zeros((2 * BLOCK, LANE), F32)
            for h0 in range(kv * GROUP, (kv + 1) * GROUP, ns):
                qg, dog = _stack_heads(q, h0, ns), _stack_heads(do, h0, ns)
                lse_g = jnp.concatenate([lse_all[:, h0 + g:h0 + g + 1][None] for g in range(ns)], axis=0)
                p, psink = _group_probs(qg, k2, b_ref[h0:h0 + ns], _group_sinks(s_ref, h0, ns), lse_g, valid)
                dp = lax.dot_general(dog, v2, (_DIMS["nt"], ((), ())), preferred_element_type=F32)
                dp = dp.reshape(ns, BLOCK, 2 * BLOCK)
                delta = jnp.sum(p * dp, axis=-1, keepdims=True)
                ds = p * (dp - delta)
                db_ref[h0:h0 + ns] += ds
                dsink = -jnp.sum(psink * delta, axis=1, keepdims=True)
                for g in range(ns):
                    dsink_row = dsink_row + jnp.where(lane == h0 + g, dsink[g], 0.0)
                dsb = ds.reshape(ns * BLOCK, 2 * BLOCK).astype(BF16)
                pb = p.reshape(ns * BLOCK, 2 * BLOCK).astype(BF16)
                dqs.append(_unstack_heads(jnp.dot(dsb, k2, preferred_element_type=F32) * scale, ns))
                dk_acc = dk_acc + lax.dot_general(dsb, qg, (_DIMS["tn"], ((), ())), preferred_element_type=F32)
                dv_acc = dv_acc + lax.dot_general(pb, dog, (_DIMS["tn"], ((), ())), preferred_element_type=F32)
            dks.append(_fold_halves(dk_acc) * scale)
            dvs.append(_fold_halves(dv_acc))
        dq_ref[...] = jnp.concatenate(dqs, axis=1).astype(BF16)
        rows = pl.ds(pl.multiple_of(n * BLOCK, BLOCK), 2 * BLOCK)
        dk_ref[rows, :] += jnp.concatenate([jnp.where(low, dks[0], dks[1]), jnp.where(low, dks[2], dks[3])], axis=1)
        dv_ref[rows, :] += jnp.concatenate([jnp.where(low, dvs[0], dvs[1]), jnp.where(low, dvs[2], dvs[3])], axis=1)
        ds_ref[0:1, :] += dsink_row

    res, jres = _call(
        body, name=name, grid=(nblk,), in_specs=in_specs,
        out_specs=[pl.BlockSpec((BLOCK, D_MODEL), lambda n: (n, 0)),
                   pl.BlockSpec((t + BLOCK, KV_WIDTH), lambda n: (0, 0)),
                   pl.BlockSpec((t + BLOCK, KV_WIDTH), lambda n: (0, 0)),
                   pl.BlockSpec((N_Q_HEADS, BLOCK, 2 * BLOCK), lambda n: (0, 0, 0)),
                   pl.BlockSpec((8, LANE), lambda n: (0, 0))],
        out_shape=[jax.ShapeDtypeStruct((t, D_MODEL), BF16),
                   jax.ShapeDtypeStruct((t + BLOCK, KV_WIDTH), F32),
                   jax.ShapeDtypeStruct((t + BLOCK, KV_WIDTH), F32),
                   jax.ShapeDtypeStruct((N_Q_HEADS, BLOCK, 2 * BLOCK), F32),
                   jax.ShapeDtypeStruct((8, LANE), F32)],
        sem=("arbitrary",), args=[proj, proj, proj, proj, proj, bias, sinks, datt, lse], jobs=jobs)
    return (res, jres) if jobs else res


def _bucket_table():
    qi = np.arange(BLOCK)[:, None]
    ki = np.arange(2 * BLOCK)[None, :]
    rel = np.maximum(qi + BLOCK - ki, 0)
    max_exact = N_BUCKETS // 2
    relf = np.maximum(rel, 1).astype(np.float32)
    large = max_exact + (np.log(relf / max_exact) / math.log(MAX_DISTANCE / max_exact)
                         * (N_BUCKETS - max_exact)).astype(np.int32)
    large = np.minimum(large, N_BUCKETS - 1)
    return np.where(rel < max_exact, rel, large).astype(np.int32)


def _rel_bias_fwd(rel_bias, onehot, name):
    n = onehot.shape[1]
    tn = _tile(n, 8192, LANE)

    def body(r_ref, o_ref, out_ref):
        out_ref[...] = lax.dot_general(r_ref[...], o_ref[...], (_DIMS["tn"], ((), ())),
                                       precision=lax.Precision.HIGHEST, preferred_element_type=F32)

    return pl.pallas_call(
        body, name=name, grid=(n // tn,),
        in_specs=[pl.BlockSpec(rel_bias.shape, lambda j: (0, 0)), pl.BlockSpec((N_BUCKETS, tn), lambda j: (0, j))],
        out_specs=pl.BlockSpec((N_Q_HEADS, tn), lambda j: (0, j)),
        out_shape=jax.ShapeDtypeStruct((N_Q_HEADS, n), F32),
        compiler_params=_params(("parallel",)),
    )(rel_bias, onehot)


def _rel_bias_bwd(dbias, onehot, name):
    def body(d_ref, o_ref, out_ref):
        out_ref[...] = lax.dot_general(d_ref[...], o_ref[...], (_DIMS["nt"], ((), ())),
                                       precision=lax.Precision.HIGHEST, preferred_element_type=F32)

    full = pl.BlockSpec(dbias.shape, lambda: (0, 0))
    return pl.pallas_call(
        body, name=name, in_specs=[full, full],
        out_specs=pl.BlockSpec((N_Q_HEADS, N_BUCKETS), lambda: (0, 0)),
        out_shape=jax.ShapeDtypeStruct((N_Q_HEADS, N_BUCKETS), F32),
        compiler_params=_params(()),
    )(dbias, onehot)


def _part_specs(gparts, part_index, tr, cc):
    blk = pl.BlockSpec((tr, cc), lambda i, *_: (i, 0))
    specs = []
    for i, _ in enumerate(gparts):
        if i in part_index:
            ix = part_index[i]
            if isinstance(ix, int):
                specs.append(pl.BlockSpec((None, tr, cc), functools.partial(lambda i, *_, ix: (ix, i, 0), ix=ix)))
            else:
                specs.append(pl.BlockSpec((None, tr, cc), lambda i, ix_ref: (ix_ref[0], i, 0)))
        else:
            specs.append(blk)
    return blk, specs


def _sum_parts(gparts, part_index, name):
    r, cc = gparts[0].shape[-2:]
    tr = _tile(r, 256)
    dyn = [i for i, ix in part_index.items() if not isinstance(ix, int)]
    assert len(dyn) <= 1

    def body(*refs):
        refs = refs[len(dyn):]
        g = refs[0][...].astype(F32)
        for gr in refs[1:-1]:
            g = g + gr[...].astype(F32)
        refs[-1][...] = g

    blk, specs = _part_specs(gparts, part_index, tr, cc)
    grid_spec = pltpu.PrefetchScalarGridSpec(num_scalar_prefetch=len(dyn), grid=(r // tr,), in_specs=specs, out_specs=blk)
    return pl.pallas_call(
        body, name=name, grid_spec=grid_spec, out_shape=jax.ShapeDtypeStruct((r, cc), F32),
        compiler_params=_params(("parallel",)),
    )(*[part_index[i] for i in dyn], *gparts)


def _adamw(w, m, v, gparts, name, part_index=None):
    r, cc = w.shape
    tr = _tile(r, 128)
    np_ = len(gparts)
    part_index = part_index or {}
    dyn = [i for i, ix in part_index.items() if not isinstance(ix, int)]
    assert len(dyn) <= 1
    bc1 = 1.0 / (1.0 - ADAM_B1 ** ADAM_STEP)
    bc2 = 1.0 / (1.0 - ADAM_B2 ** ADAM_STEP)

    def body(*refs):
        refs = refs[len(dyn):]
        w_ref, m_ref, v_ref = refs[:3]
        g_refs = refs[3:3 + np_]
        g_out, d_out, m_out, v_out = refs[3 + np_:]
        g = g_refs[0][...].astype(F32)
        for gr in g_refs[1:]:
            g = g + gr[...].astype(F32)
        mn = ADAM_B1 * m_ref[...] + (1.0 - ADAM_B1) * g
        vn = ADAM_B2 * v_ref[...] + (1.0 - ADAM_B2) * (g * g)
        g_out[...] = g
        m_out[...] = mn
        v_out[...] = vn
        d_out[...] = -ADAM_LR * ((mn * bc1) / (jnp.sqrt(vn * bc2) + ADAM_EPS) + ADAM_WD * w_ref[...])

    blk, g_specs = _part_specs(gparts, part_index, tr, cc)
    grid_spec = pltpu.PrefetchScalarGridSpec(
        num_scalar_prefetch=len(dyn), grid=(r // tr,),
        in_specs=[blk, blk, blk] + g_specs, out_specs=[blk] * 4)
    return pl.pallas_call(
        body, name=name, grid_spec=grid_spec,
        out_shape=[jax.ShapeDtypeStruct((r, cc), F32)] * 4,
        compiler_params=_params(("parallel",)),
    )(*[part_index[i] for i in dyn], w, m, v, *gparts)


def _row(vec):
    return vec.reshape(1, -1)


def kernel(x, c, w_ada, b_ada, norm1_g, w_in, conv_w, conv_b, lru_wa, lru_ba, lru_wx, lru_bx, lru_lambda, w_lru_out, w_attn_out, attn_sinks, rel_bias, w_out, norm2_g, w_ff1, w_ff2, final_g, loss_target, m_w_ada, m_b_ada, m_norm1_g, m_w_in, m_conv_w, m_conv_b, m_lru_wa, m_lru_ba, m_lru_wx, m_lru_bx, m_lru_lambda, m_w_lru_out, m_w_attn_out, m_attn_sinks, m_rel_bias, m_w_out, m_norm2_g, m_w_ff1, m_w_ff2, m_final_g, v_w_ada, v_b_ada, v_norm1_g, v_w_in, v_conv_w, v_conv_b, v_lru_wa, v_lru_ba, v_lru_wx, v_lru_bx, v_lru_lambda, v_w_lru_out, v_w_attn_out, v_attn_sinks, v_rel_bias, v_w_out, v_norm2_g, v_w_ff1, v_w_ff2, v_final_g):
    dd = D_MODEL
    t = x.shape[1]
    ix, iy, ic = lax.axis_index("x"), lax.axis_index("y"), lax.axis_index("c")
    me = 4 * ix + 2 * iy + ic
    chip = 2 * ix + iy
    c_idx = jnp.reshape(ic, (1,)).astype(jnp.int32)
    chip_idx = jnp.reshape(chip, (1,)).astype(jnp.int32)

    xs = x[0]
    target = loss_target[0]
    ncol_ada = w_ada.shape[2]
    cshard = conv_w.shape[2]

    shards = [jnp.transpose(w_in[0]).astype(BF16), w_lru_out[0].astype(BF16), w_attn_out[0].astype(BF16),
              w_out[0].astype(BF16), jnp.transpose(w_ff1[0]).astype(BF16), w_ff2[0].astype(BF16)]
    s_in, s_lo, s_ao, s_o, s_f1, s_f2 = shards

    def full(g):
        return g.reshape(N_DEV * g.shape[1], dd)

    def stack(g):
        return g.reshape(N_DEV, g.shape[0] // N_DEV, dd)

    (w_in_pair,) = _run_job(_pair_job([s_in]), "ag_w_in_pair")
    w_in_pair = lax.dynamic_update_slice_in_dim(w_in_pair, s_in[None], ic, axis=0)

    pack0 = jnp.zeros((8, dd), F32).at[0:1, :].set(c).at[1:1 + CONV_WIDTH, 0:cshard].set(conv_w[0])
    (g0,) = _all_gather([pack0], "ag_cond")
    c_all = g0[:, 0, :]
    conv_w_full = jnp.transpose(g0[:, 1:1 + CONV_WIDTH, 0:cshard], (1, 0, 2)).reshape(CONV_WIDTH, dd)
    b_cols = lax.dynamic_slice(b_ada, (0, me * ncol_ada), (1, ncol_ada))
    mod_cols = _ada_fwd(c_all, w_ada[0], b_cols, "ada_fwd")
    (g1,) = _all_gather([mod_cols], "ag_mod")
    mod_mine = lax.dynamic_index_in_dim(g1, me, axis=1, keepdims=False).reshape(6, dd)
    mod = jnp.concatenate([mod_mine, jnp.zeros((2, dd), F32)], axis=0)

    bucket = _bucket_table()
    onehot = jnp.asarray((bucket.reshape(1, -1) == np.arange(N_BUCKETS)[:, None]).astype(np.float32))
    bias = _rel_bias_fwd(rel_bias, onehot, "rel_bias_fwd").reshape(N_Q_HEADS, BLOCK, 2 * BLOCK)
    sinks = attn_sinks.reshape(N_Q_HEADS)
    lp = jnp.concatenate([conv_b, lru_ba, lru_bx, lru_lambda, conv_w_full], axis=0)
    wa, wx = lru_wa[0], lru_wx[0]

    tm = _tile(t, 1024)
    tn = 512
    h = _norm_mod(xs, norm1_g, mod, 0, "norm1")
    tn2 = 1024
    chunk_w = 2 * s_in.shape[0]

    def ids(*ks):
        return jnp.stack([jnp.asarray(k, jnp.int32) for k in ks])

    tn_in = _tile(chunk_w, 1024, LANE)
    (proj,), ((g_near,),) = _mm(
        h, w_in_pair.reshape(chunk_w, dd), "nt", tm=tm, tn=tn_in, outs=[BF16], name="in_proj_own",
        chunks=(ids(chip), chunk_w, ids(0)), n_total=IN_WIDTH, jobs=[_gather_job([w_in_pair], (0.7,), "near")])
    near_ids = ids(2 * (1 - ix) + iy, 2 * ix + 1 - iy)
    (proj,), ((g_far,),) = _mm(
        h, full(g_near), "nt", tm=tm, tn=tn_in, outs=[BF16], name="in_proj_near",
        chunks=(near_ids, chunk_w, near_ids), into=proj, jobs=[_gather_job([g_near], (0.6,), "far")])
    far_chip = 2 * (1 - ix) + 1 - iy
    (proj,), ((g_o,),) = _mm(
        h, full(g_far), "nt", tm=tm, tn=tn_in, outs=[BF16], name="in_proj_far",
        chunks=(ids(far_chip), chunk_w, ids(far_chip)), into=proj, jobs=[_gather_job([s_o], (0.5, 0.8))])
    far_slots = lax.dynamic_slice_in_dim(g_far, 2 * far_chip, 2, axis=0)
    win_t = lax.dynamic_update_slice_in_dim(g_near, far_slots, 2 * far_chip, axis=0)
    win_t = full(lax.dynamic_update_slice_in_dim(win_t, w_in_pair, 2 * chip, axis=0))
    (g_o,) = _fill_own([g_o], [s_o])
    (rec, ya), ((g_f1,),) = _lru_fwd(proj, lp, wa, wx, "lru_fwd", jobs=[_gather_job([s_f1], (0.6, 0.9))])
    (att, lse), ((g_lo,), (g_ao,)) = _attn_fwd(proj, bias, sinks, "attn_fwd",
                                        jobs=[_gather_job([s_lo], (0.3, 0.45)), _gather_job([s_ao], (0.75, 0.9))])
    g_lo, g_ao, g_f1 = _fill_own([g_lo, g_ao, g_f1], [s_lo, s_ao, s_f1])
    wlo, wao, wo, wf1_t = full(g_lo), full(g_ao), full(g_o), full(g_f1)
    (y_a,) = _mm(ya, wlo, "nn", tm=tm, tn=tn2, outs=[BF16], name="lru_out")

    def merge_epi(acc, ya_t, ga_t, gb_t):
        return acc, _sigmoid(ga_t.astype(F32)) * ya_t.astype(F32) + _sigmoid(gb_t.astype(F32)) * acc

    y_b, merged = _mm(att, wao, "nn", tm=tm, tn=tn, outs=[BF16, BF16], epi=merge_epi,
                      tiles=[(y_a, 0), (proj, OFF_GA // tn), (proj, OFF_GB // tn)], name="attn_out_merge")

    def resid_epi(row):
        def epi(acc, x_t, mod_t):
            return acc, x_t + mod_t[row:row + 1, :] * acc
        return epi

    o1, x1 = _mm(merged, wo, "nn", tm=tm, tn=tn2, outs=[BF16, F32], epi=resid_epi(2),
                 tiles=[(xs, 0)], rows=[mod], name="out_proj")
    h2 = _norm_mod(x1, norm2_g, mod, 3, "norm2")

    def relu2_epi(acc):
        rl = jnp.maximum(acc, 0.0)
        return acc, rl * rl

    (f_pre, ff), ((g_f2,),) = _mm(h2, wf1_t, "nt", tm=tm, tn=1024, outs=[BF16, BF16], epi=relu2_epi, name="ff1",
                                  jobs=[_gather_job([s_f2], (0.6, 0.9))])
    wf2 = full(_fill_own([g_f2], [s_f2])[0])
    tmh = _tile(t, 512)
    o2, x2 = _mm(ff, wf2, "nn", tm=tmh, tn=tn, outs=[BF16, F32], epi=resid_epi(5),
                 tiles=[(x1, 0)], rows=[mod], name="ff2")

    loss_blk, dx2, do2, st_f = _final_loss(x2, o2, _row(final_g), target, mod, "final_loss")
    loss = lax.psum(loss_blk[0, 0], ("x", "y", "c"))

    def drelu2_epi(acc, f_t):
        return (acc * (2.0 * jnp.maximum(f_t.astype(F32), 0.0)),)

    tmw, tnw, tkw = 512, dd, _tile(t, 2048)
    (df,) = _mm(do2, wf2, "nt", tm=tm, tn=1024, outs=[BF16], epi=drelu2_epi, tiles=[(f_pre, 0)], name="ff2_dx")
    (dwf2,) = _mm(ff, do2, "tn", tm=tmw, tn=tnw, tk=tkw, outs=[BF16], name="ff2_dw")
    (dh2,), ((sib_f2,),) = _mm(df, wf1_t, "nn", tm=tmh, tn=tn, outs=[BF16], name="ff1_dx",
                              jobs=[_sibling_job([stack(dwf2)])])
    pair_f2 = _pair_add(stack(dwf2), sib_f2, c_idx, "rs_pair_add_w_ff2")
    (dwf1_t,), ((chips_f2,),) = _mm(df, h2, "tn", tm=tmw, tn=tnw, tk=tkw, outs=[BF16], name="ff1_dw",
                                    jobs=[_chips_job([pair_f2])])
    (dx1, st_2, do1), ((sib_f1,),) = _norm_mod_bwd(x1, norm2_g, mod, 3, dh2, dx2, (o1, 2), "norm2_bwd",
                                                   jobs=[_sibling_job([stack(dwf1_t)])])
    pair_f1 = _pair_add(stack(dwf1_t), sib_f1, c_idx, "rs_pair_add_w_ff1")

    def dmerge_epi(acc, ga_t, gb_t, ya_t, yb_t):
        sa, sb = _sigmoid(ga_t.astype(F32)), _sigmoid(gb_t.astype(F32))
        return (acc * sa, acc * sb, acc * ya_t.astype(F32) * sa * (1.0 - sa), acc * yb_t.astype(F32) * sb * (1.0 - sb))

    tmh = _tile(t, 512)
    dy_a, dy_b, dga, dgb = _mm(do1, wo, "nt", tm=tmh, tn=tn, outs=[BF16] * 4, epi=dmerge_epi,
                               tiles=[(proj, OFF_GA // tn), (proj, OFF_GB // tn), (y_a, 0), (y_b, 0)], name="out_proj_dx")
    (dwo,) = _mm(merged, do1, "tn", tm=tmw, tn=tnw, tk=tkw, outs=[BF16], name="out_proj_dw")
    (datt,) = _mm(dy_b, wao, "nt", tm=tm, tn=tn2, outs=[BF16], name="attn_out_dx")
    (dwao,) = _mm(att, dy_b, "tn", tm=tmw, tn=tnw, tk=tkw, outs=[BF16], name="attn_out_dw")
    (dya,) = _mm(dy_a, wlo, "nt", tm=tm, tn=tn2, outs=[BF16], name="lru_out_dx")
    (dwlo,) = _mm(ya, dy_a, "tn", tm=tmw, tn=tnw, tk=tkw, outs=[BF16], name="lru_out_dw")
    (dq, dk_pad, dv_pad, dbias, dsinks), (sib_3, (chips_f1,)) = _attn_bwd(
        proj, datt, lse, bias, sinks, "attn_bwd",
        jobs=[_sibling_job([stack(dwlo), stack(dwao), stack(dwo)]), _chips_job([pair_f1])])
    pair_lo, pair_ao, pair_o = [_pair_add(stack(g_), s_, c_idx, "rs_pair_add_" + nm_)
                                for g_, s_, nm_ in zip((dwlo, dwao, dwo), sib_3, ("w_lru_out", "w_attn_out", "w_out"))]
    (dlx, dlg, st_l, dwa, dwx), ((chips_lo, chips_ao, chips_o),) = _lru_bwd(
        proj, rec, dya, lp, wa, wx, "lru_bwd", jobs=[_chips_job([pair_lo, pair_ao, pair_o])])
    dproj = jnp.concatenate([dlx, dlg, dq, dk_pad[BLOCK:].astype(BF16), dv_pad[BLOCK:].astype(BF16), dga, dgb], axis=1)
    lru_mats = jnp.concatenate([dwa.reshape(-1, dd), dwx.reshape(-1, dd)], axis=0)
    (dwin_t,), ((mats_all,),) = _mm(dproj, h, "tn", tm=384, tn=tnw, tk=tkw, outs=[BF16], name="in_proj_dw",
                                   jobs=[_gather_job([lru_mats], (0.3, 0.6))])
    (mats_all,) = _fill_own([mats_all], [lru_mats])
    (sib_in,) = _exchange_sibling([stack(dwin_t)], "rs_sibling_w_in")
    pair_in = _pair_add(stack(dwin_t), sib_in, c_idx, "rs_pair_add_w_in", halves=True)
    n_rb = t // tmh
    if n_rb >= 2:
        (dh,), ((chips_in_l,),) = _mm(dproj, win_t, "nn", tm=tmh, tn=tn, tk=IN_WIDTH // 2, outs=[BF16],
                                     name="in_proj_dx_top", rows_part=(0, n_rb // 2), jobs=[_chips_job(pair_in[:1])])
        (dh,), ((chips_in_r,),) = _mm(dproj, win_t, "nn", tm=tmh, tn=tn, tk=IN_WIDTH // 2, outs=[BF16],
                                     name="in_proj_dx_bottom", rows_part=(n_rb // 2, n_rb - n_rb // 2), into=dh,
                                     jobs=[_chips_job(pair_in[1:])])
    else:
        (dh,), ((chips_in_l,), (chips_in_r,)) = _mm(
            dproj, win_t, "nn", tm=tmh, tn=tn, tk=IN_WIDTH // 2, outs=[BF16], name="in_proj_dx",
            jobs=[_chips_job(pair_in[:1]), _chips_job(pair_in[1:])])
    chips_in = [chips_in_l, chips_in_r]
    grad_x, st_1 = _norm_mod_bwd(xs, norm1_g, mod, 0, dh, dx1, None, "norm1_bwd")

    drel = jnp.transpose(_rel_bias_bwd(dbias.reshape(N_Q_HEADS, -1), onehot, "rel_bias_bwd"))

    misc = jnp.concatenate([drel.reshape(1, -1), dsinks[0:1, 0:N_Q_HEADS],
                            jnp.zeros((1, dd - N_BUCKETS * N_Q_HEADS - N_Q_HEADS), F32)], axis=1)
    zero_row = jnp.zeros((1, dd), F32)
    small = jnp.concatenate([
        st_1[0:2], st_2[3:4], st_2[0:2], st_f[1:2], zero_row, zero_row,
        st_1[2:3], st_l[0:1], st_l[1:2], st_l[2:3], st_l[3:4], st_2[2:3], st_f[0:1], misc,
        st_l[4:8], jnp.zeros((4, dd), F32)], axis=0)
    (small_all,) = _all_gather([small], "ag_small")

    def pack_small(b_, n1, cb, ba, bx, lam, n2, fg, rb, sk):
        misc_ = jnp.concatenate([rb.reshape(1, -1), sk.reshape(1, -1),
                                 jnp.zeros((1, dd - N_BUCKETS * N_Q_HEADS - N_Q_HEADS), F32)], axis=1)
        return jnp.concatenate([b_.reshape(6, dd), jnp.zeros((2, dd), F32), n1, cb, ba, bx, lam, n2, _row(fg), misc_,
                                jnp.zeros((8, dd), F32)], axis=0)

    def pack_mats(wa_, wx_):
        return jnp.concatenate([wa_.reshape(-1, dd), wx_.reshape(-1, dd)], axis=0)

    every = {i: i for i in range(N_DEV)}
    w_s = pack_small(b_ada, norm1_g, conv_b, lru_ba, lru_bx, lru_lambda, norm2_g, final_g, rel_bias, attn_sinks)
    m_s = pack_small(m_b_ada, m_norm1_g, m_conv_b, m_lru_ba, m_lru_bx, m_lru_lambda, m_norm2_g, m_final_g, m_rel_bias, m_attn_sinks)
    v_s = pack_small(v_b_ada, v_norm1_g, v_conv_b, v_lru_ba, v_lru_bx, v_lru_lambda, v_norm2_g, v_final_g, v_rel_bias, v_attn_sinks)
    small_res = _adamw(w_s, m_s, v_s, [small_all] * N_DEV, "adamw_small", part_index=every)
    mats_res = _adamw(pack_mats(lru_wa, lru_wx), pack_mats(m_lru_wa, m_lru_wx), pack_mats(v_lru_wa, v_lru_wx),
                      [mats_all] * N_DEV, "adamw_lru_mats", part_index=every)

    def unpack_small(s, mt):
        nb_ = N_BUCKETS * N_Q_HEADS
        half = mt.shape[0] // 2
        return dict(
            b_ada=s[0:6].reshape(1, 6 * dd), norm1_g=s[8:9], conv_b=s[9:10], lru_ba=s[10:11], lru_bx=s[11:12],
            lru_lambda=s[12:13], norm2_g=s[13:14], final_g=s[14], rel_bias=s[15, 0:nb_].reshape(N_BUCKETS, N_Q_HEADS),
            attn_sinks=s[15:16, nb_:nb_ + N_Q_HEADS],
            lru_wa=mt[:half].reshape(1, LRU_BLOCKS, LRU_BLOCK_W, LRU_BLOCK_W),
            lru_wx=mt[half:].reshape(1, LRU_BLOCKS, LRU_BLOCK_W, LRU_BLOCK_W))

    res = {k: [None] * 4 for k in ("w_ada", "w_in", "conv_w", "w_lru_out", "w_attn_out", "w_out", "w_ff1", "w_ff2")}
    for q_, (s_, mt_) in enumerate(zip(small_res, mats_res)):
        for k_, val in unpack_small(s_, mt_).items():
            res.setdefault(k_, [None] * 4)[q_] = val

    g_conv = lax.dynamic_slice(small_res[0][16:16 + CONV_WIDTH], (0, me * cshard), (CONV_WIDTH, cshard))
    conv_res = _adamw(conv_w[0], m_conv_w[0], v_conv_w[0], [g_conv], "adamw_conv")
    res["conv_w"] = [r_[None] for r_ in conv_res]

    dmod_all = small_all[:, 0:6, :].reshape(N_DEV, 6 * dd)
    dmod_cols = lax.dynamic_slice(dmod_all, (0, me * ncol_ada), (N_DEV, ncol_ada))
    g_ada = _ada_bwd(jnp.transpose(c_all), dmod_cols, "ada_bwd")
    ada_res = _adamw(w_ada[0], m_w_ada[0], v_w_ada[0], [g_ada], "adamw_ada")
    res["w_ada"] = [r_[None] for r_ in ada_res]

    pair = [None, pair_lo, pair_ao, pair_o, pair_f1, pair_f2]
    from_chips = [None, chips_lo, chips_ao, chips_o, chips_f1, chips_f2]
    own_and_chips = {0: chip_idx, 1: 0, 2: 1, 3: 2}

    def summed(i):
        return [pair[i], from_chips[i], from_chips[i], from_chips[i]], {0: chip_idx, 1: 0, 2: 1, 3: 2}

    def sum_only(i, name):
        parts, index = summed(i)
        return _sum_parts(parts, index, name)

    g_in = jnp.transpose(jnp.concatenate(
        [_sum_parts([p_, c_, c_, c_], own_and_chips, "sum_w_in_%d" % q_)
         for q_, (p_, c_) in enumerate(zip(pair_in, chips_in))], axis=1))
    res["w_in"] = [r_[None] for r_ in _adamw(w_in[0], m_w_in[0], v_w_in[0], [g_in], "adamw_w_in")]
    g_ff1 = jnp.transpose(sum_only(4, "sum_w_ff1"))
    res["w_ff1"] = [r_[None] for r_ in _adamw(w_ff1[0], m_w_ff1[0], v_w_ff1[0], [g_ff1], "adamw_w_ff1")]
    for i, (nm, w_, m_, v_) in {1: ("w_lru_out", w_lru_out, m_w_lru_out, v_w_lru_out),
                                2: ("w_attn_out", w_attn_out, m_w_attn_out, v_w_attn_out),
                                3: ("w_out", w_out, m_w_out, v_w_out),
                                5: ("w_ff2", w_ff2, m_w_ff2, v_w_ff2)}.items():
        parts, index = summed(i)
        res[nm] = [r_[None] for r_ in _adamw(w_[0], m_[0], v_[0], parts, "adamw_" + nm, part_index=index)]

    order = ["w_ada", "b_ada", "norm1_g", "w_in", "conv_w", "conv_b", "lru_wa", "lru_ba", "lru_wx", "lru_bx",
             "lru_lambda", "w_lru_out", "w_attn_out", "attn_sinks", "rel_bias", "w_out", "norm2_g", "w_ff1",
             "w_ff2", "final_g"]
    out = [loss, grad_x[None]]
    for q_ in range(4):
        out += [res[k_][q_] for k_ in order]
    return tuple(out)
```

```python
import functools
import math

import numpy as np
import jax
import jax.numpy as jnp
from jax import lax
from jax.experimental import pallas as pl
from jax.experimental.pallas import tpu as pltpu

F32 = jnp.float32
BF16 = jnp.bfloat16
MESH = pl.DeviceIdType.MESH

D_MODEL = 2048
N_Q_HEADS = 32
N_KV_HEADS = 4
GROUP = N_Q_HEADS // N_KV_HEADS
HEAD_DIM = 64
KV_WIDTH = N_KV_HEADS * HEAD_DIM
BLOCK = 128
NEG_INF = -1e30
N_BUCKETS = 32
MAX_DISTANCE = 128
LRU_BLOCKS = 16
LRU_BLOCK_W = 128
CONV_WIDTH = 4
LRU_C = 8.0
D_FF = 4 * D_MODEL
EPS = 1e-6
IN_WIDTH = 5 * D_MODEL + 2 * KV_WIDTH
OFF_LX, OFF_LG, OFF_Q, OFF_K, OFF_V, OFF_GA, OFF_GB = 0, 2048, 4096, 6144, 6400, 6656, 8704

ADAM_LR, ADAM_B1, ADAM_B2, ADAM_EPS, ADAM_WD, ADAM_STEP = 0.001, 0.9, 0.999, 1e-08, 0.01, 10

N_DEV = 8
VMEM_LIMIT_BYTES = 48 * 1024 * 1024
LANE = 128
LRU_CW = 512


def _params(sem, **kw):
    return pltpu.CompilerParams(dimension_semantics=sem, vmem_limit_bytes=VMEM_LIMIT_BYTES, **kw)


def _tile(n, pref, mult=8):
    if n <= pref:
        return n
    t = (pref // mult) * mult
    while t >= mult:
        if n % t == 0:
            return t
        t -= mult
    return n


def _sigmoid(x):
    return 0.5 * jnp.tanh(0.5 * x) + 0.5


def _gelu_tanh(x):
    k = math.sqrt(2.0 / math.pi)
    t = jnp.tanh(k * (x + 0.044715 * x * x * x))
    return 0.5 * x * (1.0 + t), t


def _gelu_tanh_grad(x, t):
    k = math.sqrt(2.0 / math.pi)
    return 0.5 * (1.0 + t) + 0.5 * x * (1.0 - t * t) * k * (1.0 + 3.0 * 0.044715 * x * x)


def _log1p(e):
    u = 1.0 + e
    return jnp.where(u == 1.0, e, jnp.log(u) * (e / jnp.where(u == 1.0, 1.0, u - 1.0)))


def _softplus(x):
    return jnp.maximum(x, 0.0) + _log1p(jnp.exp(-jnp.abs(x)))


def _neg_expm1(x, exp_x):
    series = -x * (1.0 + x * (0.5 + x * (1.0 / 6.0 + x * (1.0 / 24.0))))
    return jnp.where(x > -0.0625, series, 1.0 - exp_x)


def _my_place():
    return lax.axis_index("x"), lax.axis_index("y"), lax.axis_index("c")


def _all_gather(arrs, name):
    return _fill_own(_run_job(_gather_job(arrs), name), arrs)


def _fill_own(stacks, shards):
    me = 4 * lax.axis_index("x") + 2 * lax.axis_index("y") + lax.axis_index("c")
    out = []
    for g, s in zip(stacks, shards):
        if g.size * g.dtype.itemsize <= (1 << 20):
            slot = lax.broadcasted_iota(jnp.int32, g.shape, 0)
            out.append(jnp.where(slot == me, s[None], g))
        else:
            out.append(lax.dynamic_update_slice_in_dim(g, s[None], me, axis=0))
    return out


class _Job:
    def __init__(self, ins, outs, sems, start, finish, forwards=(), forward_at=()):
        self.ins, self.outs, self.sems = list(ins), list(outs), list(sems)
        self.start, self.finish, self.forwards, self.forward_at = start, finish, list(forwards), list(forward_at)
        assert len(self.forwards) <= len(self.forward_at)


def _gather_job(arrs, forward_at=(0.6, 0.85), part="all"):
    n = len(arrs)
    whole, near, far = part == "all", part == "near", part == "far"

    def copies(ins, outs, sems):
        send_sems, recv_sems, local_sems = sems
        x, y, c = _my_place()
        sib = (x, y, 1 - c)
        xn, yn, dg = (1 - x, y), (x, 1 - y), (1 - x, 1 - y)

        def slot(px, py, pc):
            return 4 * px + 2 * py + pc

        def cp(a, k, src, dst_slot, to):
            return pltpu.make_async_remote_copy(
                src_ref=src, dst_ref=outs[a].at[dst_slot], send_sem=send_sems.at[a, k], recv_sem=recv_sems.at[a, k],
                device_id=to, device_id_type=MESH)

        def arrival(a, k, dst_slot):
            return cp(a, k, outs[a].at[dst_slot], dst_slot, sib)

        me_slot = slot(x, y, c)
        sends, local, relay, passes, arrive = {}, [], {}, {}, {}
        for a in range(n):
            if whole:
                own = ins[a]
                sends[a, 0] = cp(a, 0, own, me_slot, sib)
            if near:
                own = ins[a].at[c]
            if whole or near:
                sends[a, 1] = cp(a, 1, own, me_slot, (*xn, c))
                sends[a, 2] = cp(a, 2, own, me_slot, (*yn, c))
                passes[a, 4] = cp(a, 4, outs[a].at[slot(*xn, c)], slot(*xn, c), sib)
                passes[a, 5] = cp(a, 5, outs[a].at[slot(*yn, c)], slot(*yn, c), sib)
            if whole or far:
                source = outs[a] if whole else ins[a]
                relayed = c * slot(*yn, c) + (1 - c) * slot(*xn, c)
                relay[a, 3] = cp(a, 3, source.at[relayed], relayed, (x + c * (1 - 2 * x), y + (1 - c) * (1 - 2 * y), c))
                passes[a, 6] = cp(a, 6, outs[a].at[slot(*dg, c)], slot(*dg, c), sib)
            arrive[a, 0] = arrival(a, 0, slot(x, y, 1 - c))
            arrive[a, 1] = arrival(a, 1, slot(*xn, c))
            arrive[a, 2] = arrival(a, 2, slot(*yn, c))
            arrive[a, 3] = arrival(a, 3, slot(*dg, c))
            arrive[a, 4] = arrival(a, 4, slot(*xn, 1 - c))
            arrive[a, 5] = arrival(a, 5, slot(*yn, 1 - c))
            arrive[a, 6] = arrival(a, 6, slot(*dg, 1 - c))
        return sends, local, relay, passes, arrive

    def pick(d, ks):
        return [d[a, k] for a in range(n) for k in ks if (a, k) in d]

    def start(ins, outs, sems):
        sends, local, relay, _, _ = copies(ins, outs, sems)
        for cp in local + pick(sends, (0, 1, 2)) + (pick(relay, (3,)) if far else []):
            cp.start()

    def forward_neighbours(ins, outs, sems):
        _, _, relay, passes, arrive = copies(ins, outs, sems)
        for cp in pick(arrive, (1, 2)):
            cp.wait_recv()
        for cp in pick(relay, (3,)) + pick(passes, (4, 5)):
            cp.start()

    def forward_diagonal(ins, outs, sems):
        _, _, _, passes, arrive = copies(ins, outs, sems)
        for cp in pick(arrive, (3,)):
            cp.wait_recv()
        for cp in pick(passes, (6,)):
            cp.start()

    def finish(ins, outs, sems):
        sends, local, relay, passes, arrive = copies(ins, outs, sems)
        for cp in pick(arrive, (6,) if far else (0, 4, 5, 6) if whole else (4, 5)):
            cp.wait_recv()
        for cp in pick(sends, (0, 1, 2)) + pick(relay, (3,)) + pick(passes, (4, 5, 6)):
            cp.wait_send()
        for cp in local:
            cp.wait()

    shapes = [a.shape[-2:] for a in arrs]
    forwards = [forward_diagonal] if far else [forward_neighbours] if near else [forward_neighbours, forward_diagonal]
    return _Job(arrs, [jax.ShapeDtypeStruct((N_DEV,) + s, a.dtype) for s, a in zip(shapes, arrs)],
                [pltpu.SemaphoreType.DMA((n, 7)), pltpu.SemaphoreType.DMA((n, 7)), pltpu.SemaphoreType.DMA((n,))],
                start, finish, forwards, forward_at)


def _pair_job(arrs):
    n = len(arrs)

    def copies(ins, outs, sems):
        send_sems, recv_sems, local_sems = sems
        x, y, c = _my_place()
        remote = [pltpu.make_async_remote_copy(
            src_ref=ins[a], dst_ref=outs[a].at[c], send_sem=send_sems.at[a], recv_sem=recv_sems.at[a],
            device_id=(x, y, 1 - c), device_id_type=MESH) for a in range(n)]
        local = []
        arrive = [pltpu.make_async_remote_copy(
            src_ref=ins[a], dst_ref=outs[a].at[1 - c], send_sem=send_sems.at[a], recv_sem=recv_sems.at[a],
            device_id=(x, y, 1 - c), device_id_type=MESH) for a in range(n)]
        return remote, local, arrive

    def start(ins, outs, sems):
        remote, local, _ = copies(ins, outs, sems)
        for cp in remote + local:
            cp.start()

    def finish(ins, outs, sems):
        remote, local, arrive = copies(ins, outs, sems)
        for cp in arrive:
            cp.wait_recv()
        for cp in remote:
            cp.wait_send()
        for cp in local:
            cp.wait()

    return _Job(arrs, [jax.ShapeDtypeStruct((2,) + a.shape, a.dtype) for a in arrs],
                [pltpu.SemaphoreType.DMA((n,)), pltpu.SemaphoreType.DMA((n,)), pltpu.SemaphoreType.DMA((n,))],
                start, finish)


def _run_job(job, name):
    ni, no = len(job.ins), len(job.outs)

    def body(*refs):
        ins, outs, sems = refs[:ni], refs[ni:ni + no], refs[ni + no:]
        job.start(ins, outs, sems)
        for fwd in job.forwards:
            fwd(ins, outs, sems)
        job.finish(ins, outs, sems)

    any_spec = pl.BlockSpec(memory_space=pl.ANY)
    return pl.pallas_call(
        body, name=name, out_shape=job.outs, in_specs=[any_spec] * ni, out_specs=[any_spec] * no,
        scratch_shapes=job.sems,
    )(*job.ins)


def _call(body, *, name, grid, in_specs, out_specs, out_shape, scratch_shapes=(), sem, args, jobs=(),
          prefetch=(), aliases=None):
    in_specs, out_specs, out_shape, scratch_shapes = list(in_specs), list(out_specs), list(out_shape), list(scratch_shapes)
    prefetch = list(prefetch)
    if not jobs and not prefetch and not aliases:
        res = pl.pallas_call(body, name=name, grid=grid, in_specs=in_specs, out_specs=out_specs, out_shape=out_shape,
                             scratch_shapes=scratch_shapes, compiler_params=_params(sem))(*args)
        return list(res), []
    n_in, n_out, n_scr = len(in_specs), len(out_specs), len(scratch_shapes)
    j_in = [len(j.ins) for j in jobs]
    j_out = [len(j.outs) for j in jobs]
    j_sem = [len(j.sems) for j in jobs]
    total = int(np.prod(grid))
    any_spec = pl.BlockSpec(memory_space=pl.ANY)

    def wrapped(*refs):
        pos = [0]

        def take(k):
            part = refs[pos[0]:pos[0] + k]
            pos[0] += k
            return part

        take(len(prefetch))
        ins = take(n_in)
        jins = [take(k) for k in j_in]
        outs = take(n_out)
        jouts = [take(k) for k in j_out]
        scr = take(n_scr)
        jsems = [take(k) for k in j_sem]
        step = pl.program_id(0)
        for d in range(1, len(grid)):
            step = step * grid[d] + pl.program_id(d)
        for j, job in enumerate(jobs):
            pl.when(step == 0)(functools.partial(job.start, jins[j], jouts[j], jsems[j]))
            for fwd, frac in zip(job.forwards, job.forward_at):
                at = min(int(total * frac), total - 1)
                pl.when(step == at)(functools.partial(fwd, jins[j], jouts[j], jsems[j]))
        body(*ins, *outs, *scr)
        for j, job in enumerate(jobs):
            pl.when(step == total - 1)(functools.partial(job.finish, jins[j], jouts[j], jsems[j]))

    grid_spec = pltpu.PrefetchScalarGridSpec(
        num_scalar_prefetch=len(prefetch), grid=grid,
        in_specs=in_specs + [any_spec] * sum(j_in),
        out_specs=out_specs + [any_spec] * sum(j_out),
        scratch_shapes=scratch_shapes + [s for j in jobs for s in j.sems])
    res = pl.pallas_call(
        wrapped, name=name, grid_spec=grid_spec,
        out_shape=out_shape + [o for j in jobs for o in j.outs],
        input_output_aliases={len(prefetch) + i: o for i, o in (aliases or {}).items()},
        compiler_params=_params(("arbitrary",) * len(grid)),
    )(*prefetch, *args, *[a for j in jobs for a in j.ins])
    res = list(res)
    own, rest = res[:n_out], res[n_out:]
    per_job = []
    for k in j_out:
        per_job.append(rest[:k])
        rest = rest[k:]
    return own, per_job


def _exchange_sibling(arrs, name):
    return _run_job(_sibling_job(arrs), name)


def _sibling_job(arrs):
    n = len(arrs)

    def copies(ins, outs, sems):
        send_sems, recv_sems = sems
        x, y, c = _my_place()
        return [pltpu.make_async_remote_copy(
            src_ref=ins[a].at[2 * k + 1 - c], dst_ref=outs[a].at[k],
            send_sem=send_sems.at[a, k], recv_sem=recv_sems.at[a, k],
            device_id=(x, y, 1 - c), device_id_type=MESH) for a in range(n) for k in range(4)]

    def start(ins, outs, sems):
        for cp in copies(ins, outs, sems):
            cp.start()

    def finish(ins, outs, sems):
        for cp in copies(ins, outs, sems):
            cp.wait()

    return _Job(arrs, [jax.ShapeDtypeStruct((4,) + a.shape[1:], a.dtype) for a in arrs],
                [pltpu.SemaphoreType.DMA((n, 4)), pltpu.SemaphoreType.DMA((n, 4))], start, finish)


def _chips_job(arrs):
    n = len(arrs)

    def copies(ins, outs, sems):
        send_sems, recv_sems = sems
        x, y, c = _my_place()
        chips = [(1 - x, y), (x, 1 - y), (1 - x, 1 - y)]
        return [pltpu.make_async_remote_copy(
            src_ref=ins[a].at[2 * px + py], dst_ref=outs[a].at[j],
            send_sem=send_sems.at[a, j], recv_sem=recv_sems.at[a, j],
            device_id=(px, py, c), device_id_type=MESH) for a in range(n) for j, (px, py) in enumerate(chips)]

    def start(ins, outs, sems):
        for cp in copies(ins, outs, sems):
            cp.start()

    def finish(ins, outs, sems):
        for cp in copies(ins, outs, sems):
            cp.wait()

    return _Job(arrs, [jax.ShapeDtypeStruct((3,) + a.shape[1:], a.dtype) for a in arrs],
                [pltpu.SemaphoreType.DMA((n, 3)), pltpu.SemaphoreType.DMA((n, 3))], start, finish)


def _pair_add(stack, from_sibling, c_idx, name, widths=None):
    _, r, cc = stack.shape
    tr = _tile(r, 512)
    ws = [cc] if widths is None else list(widths)
    assert sum(ws) == cc
    starts = [sum(ws[:q]) for q in range(len(ws))]

    def body(c_ref, a_ref, b_ref, *o_refs):
        res = a_ref[...].astype(F32) + b_ref[...].astype(F32)
        for o_ref, s0, w_ in zip(o_refs, starts, ws):
            o_ref[...] = res[:, s0:s0 + w_].astype(o_ref.dtype)

    grid_spec = pltpu.PrefetchScalarGridSpec(
        num_scalar_prefetch=1, grid=(4, r // tr),
        in_specs=[pl.BlockSpec((None, tr, cc), lambda k, i, c_ref: (2 * k + c_ref[0], i, 0)),
                  pl.BlockSpec((None, tr, cc), lambda k, i, c_ref: (k, i, 0))],
        out_specs=[pl.BlockSpec((None, tr, w_), lambda k, i, c_ref: (k, i, 0)) for w_ in ws])
    res = pl.pallas_call(
        body, name=name, grid_spec=grid_spec,
        out_shape=[jax.ShapeDtypeStruct((4, r, w_), BF16) for w_ in ws],
        compiler_params=_params(("parallel", "parallel")),
    )(c_idx, stack, from_sibling)
    return res[0] if widths is None else list(res)


_DIMS = {"nn": ((1,), (0,)), "nt": ((1,), (1,)), "tn": ((0,), (0,))}


def _mm(a, b, mode, *, tm, tn, tk=None, outs, epi=None, tiles=(), rows=(), name, jobs=(), chunks=None, into=None,
        n_total=None, rows_part=None):
    if mode == "tn":
        kk, m = a.shape
    else:
        m, kk = a.shape
    n_all = b.shape[0] if mode == "nt" else b.shape[1]
    nn = n_all if chunks is None else chunks[0].shape[0] * chunks[1]
    n_all = n_all if n_total is None else n_total
    tk = kk if tk is None else tk
    assert m % tm == 0 and nn % tn == 0 and kk % tk == 0, (name, a.shape, b.shape, tm, tn, tk)
    nk = kk // tk
    nt, nr, no = len(tiles), len(rows), len(outs)
    n_into = 0 if into is None else 1

    def body(*refs):
        a_ref, b_ref = refs[:2]
        tile_refs = refs[2:2 + nt]
        row_refs = refs[2 + nt:2 + nt + nr]
        out_refs = refs[2 + nt + nr + n_into:2 + nt + nr + n_into + no]
        part = lax.dot_general(a_ref[...], b_ref[...], (_DIMS[mode], ((), ())), preferred_element_type=F32)

        def finish(acc):
            if epi is None:
                res = (acc,)
            else:
                res = epi(acc, *[t[...] for t in tile_refs], *[r[...] for r in row_refs])
            for o_ref, val in zip(out_refs, res):
                o_ref[...] = val.astype(o_ref.dtype)

        if nk == 1:
            finish(part)
        else:
            acc_ref = refs[-1]
            k = pl.program_id(2)

            @pl.when(k == 0)
            def _():
                acc_ref[...] = part

            @pl.when(k > 0)
            def _():
                acc_ref[...] += part

            @pl.when(k == nk - 1)
            def _():
                finish(acc_ref[...])

    if chunks is None:
        col = b_row = lambda j, *_: j
        prefetch = []
    else:
        assert mode == "nt" and not tiles and not rows and chunks[1] % tn == 0
        per = chunks[1] // tn
        col = lambda j, ids, b_ids: ids[j // per] * per + j % per
        b_row = lambda j, ids, b_ids: b_ids[j // per] * per + j % per
        prefetch = [chunks[0], chunks[2]]
    row0, n_row_blocks = (0, m // tm) if rows_part is None else rows_part
    assert rows_part is None or (mode != "tn" and not tiles)
    if mode == "tn":
        a_spec = pl.BlockSpec((tk, tm), lambda i, j, k, *_: (k, i))
    else:
        a_spec = pl.BlockSpec((tm, tk), lambda i, j, k, *_: (i + row0, k))
    if mode == "nt":
        b_spec = pl.BlockSpec((tn, tk), lambda i, j, k, *s: (b_row(j, *s), k))
    else:
        b_spec = pl.BlockSpec((tk, tn), lambda i, j, k, *_: (k, j))
    tile_specs = [pl.BlockSpec((tm, tn), functools.partial(lambda i, j, k, off: (i, j + off), off=off))
                  for _, off in tiles]
    row_specs = [pl.BlockSpec((r.shape[0], tn), lambda i, j, k: (0, j)) for r in rows]
    out_spec = pl.BlockSpec((tm, tn), lambda i, j, k, *s: (i + row0, col(j, *s)))
    args = [a, b, *[t for t, _ in tiles], *rows]
    in_specs = [a_spec, b_spec] + tile_specs + row_specs
    aliases = None
    if into is not None:
        aliases = {len(args): 0}
        args.append(into)
        in_specs.append(pl.BlockSpec(memory_space=pl.ANY))
    res, jres = _call(
        body, name=name, grid=(n_row_blocks, nn // tn, nk),
        in_specs=in_specs, out_specs=[out_spec] * no,
        out_shape=[jax.ShapeDtypeStruct((m, n_all), dt) for dt in outs],
        scratch_shapes=[pltpu.VMEM((tm, tn), F32)] if nk > 1 else [],
        sem=("parallel", "parallel", "arbitrary"),
        args=args, jobs=jobs, prefetch=prefetch, aliases=aliases)
    return (res, jres) if jobs else res


def _ada_fwd(c_all, w_ada, b_ada_cols, name):
    nb, dd = c_all.shape
    ncol = w_ada.shape[1]
    tn = _tile(ncol, 512, LANE)

    def body(c_ref, w_ref, b_ref, o_ref):
        cv = c_ref[...]
        act = (cv * _sigmoid(cv)).astype(BF16)
        o_ref[...] = jnp.dot(act, w_ref[...].astype(BF16), preferred_element_type=F32) + b_ref[...]

    return pl.pallas_call(
        body, name=name, grid=(ncol // tn,),
        in_specs=[pl.BlockSpec((nb, dd), lambda j: (0, 0)), pl.BlockSpec((dd, tn), lambda j: (0, j)),
                  pl.BlockSpec((1, tn), lambda j: (0, j))],
        out_specs=pl.BlockSpec((nb, tn), lambda j: (0, j)),
        out_shape=jax.ShapeDtypeStruct((nb, ncol), F32),
        compiler_params=_params(("parallel",)),
    )(c_all, w_ada, b_ada_cols)


def _ada_bwd(c_all_t, dmod_cols, name):
    dd, nb = c_all_t.shape
    ncol = dmod_cols.shape[1]
    tr = _tile(dd, 256)

    def body(c_ref, d_ref, o_ref):
        cv = c_ref[...]
        act = (cv * _sigmoid(cv)).astype(BF16).astype(F32)
        dm = d_ref[...].astype(BF16).astype(F32)
        acc = act[:, 0:1] * dm[0:1, :]
        for bi in range(1, nb):
            acc = acc + act[:, bi:bi + 1] * dm[bi:bi + 1, :]
        o_ref[...] = acc

    return pl.pallas_call(
        body, name=name, grid=(dd // tr,),
        in_specs=[pl.BlockSpec((tr, nb), lambda i: (i, 0)), pl.BlockSpec((nb, ncol), lambda i: (0, 0))],
        out_specs=pl.BlockSpec((tr, ncol), lambda i: (i, 0)),
        out_shape=jax.ShapeDtypeStruct((dd, ncol), F32),
        compiler_params=_params(("parallel",)),
    )(c_all_t, dmod_cols)


def _norm_mod(x, gain, mod, row0, name):
    t, dd = x.shape
    tt = _tile(t, 512)

    def body(x_ref, g_ref, m_ref, h_ref):
        xv = x_ref[...]
        r = lax.rsqrt(jnp.mean(xv * xv, axis=-1, keepdims=True) + EPS)
        shift, scale = m_ref[row0:row0 + 1, :], m_ref[row0 + 1:row0 + 2, :]
        h_ref[...] = ((xv * r * g_ref[...]) * (1.0 + scale) + shift).astype(BF16)

    return pl.pallas_call(
        body, name=name, grid=(t // tt,),
        in_specs=[pl.BlockSpec((tt, dd), lambda i: (i, 0)), pl.BlockSpec((1, dd), lambda i: (0, 0)),
                  pl.BlockSpec((8, dd), lambda i: (0, 0))],
        out_specs=pl.BlockSpec((tt, dd), lambda i: (i, 0)),
        out_shape=jax.ShapeDtypeStruct((t, dd), BF16),
        compiler_params=_params(("parallel",)),
    )(x, gain, mod)


def _norm_mod_bwd(x, gain, mod, row0, dh, dres, branch, name, jobs=()):
    t, dd = x.shape
    tt = _tile(t, 256)
    has_branch = branch is not None

    def body(*refs):
        if has_branch:
            x_ref, g_ref, m_ref, dh_ref, dres_ref, o_ref, dx_ref, st_ref, do_ref = refs
        else:
            x_ref, g_ref, m_ref, dh_ref, dres_ref, dx_ref, st_ref = refs

        @pl.when(pl.program_id(0) == 0)
        def _():
            st_ref[...] = jnp.zeros_like(st_ref)

        xv = x_ref[...]
        g = g_ref[...]
        scale = m_ref[row0 + 1:row0 + 2, :]
        r = lax.rsqrt(jnp.mean(xv * xv, axis=-1, keepdims=True) + EPS)
        nv = xv * r
        dhv = dh_ref[...].astype(F32)
        st_ref[0:1, :] += jnp.sum(dhv, axis=0, keepdims=True)
        st_ref[1:2, :] += jnp.sum(dhv * (nv * g), axis=0, keepdims=True)
        dng = dhv * (1.0 + scale)
        st_ref[2:3, :] += jnp.sum(dng * nv, axis=0, keepdims=True)
        dn = dng * g
        dx = dres_ref[...] + r * (dn - nv * jnp.mean(dn * nv, axis=-1, keepdims=True))
        dx_ref[...] = dx
        if has_branch:
            ov = o_ref[...].astype(F32)
            gate = m_ref[branch[1]:branch[1] + 1, :]
            st_ref[3:4, :] += jnp.sum(dx * ov, axis=0, keepdims=True)
            do_ref[...] = (dx * gate).astype(BF16)

    row = pl.BlockSpec((tt, dd), lambda i: (i, 0))
    in_specs = [row, pl.BlockSpec((1, dd), lambda i: (0, 0)), pl.BlockSpec((8, dd), lambda i: (0, 0)), row, row]
    args = [x, gain, mod, dh, dres]
    out_specs = [row, pl.BlockSpec((8, dd), lambda i: (0, 0))]
    out_shape = [jax.ShapeDtypeStruct((t, dd), F32), jax.ShapeDtypeStruct((8, dd), F32)]
    if has_branch:
        in_specs.append(row)
        args.append(branch[0])
        out_specs.append(row)
        out_shape.append(jax.ShapeDtypeStruct((t, dd), BF16))
    res, jres = _call(body, name=name, grid=(t // tt,), in_specs=in_specs, out_specs=out_specs, out_shape=out_shape,
                      sem=("arbitrary",), args=args, jobs=jobs)
    return (res, jres) if jobs else res


def _final_loss(x2, o2, final_g, target, mod, name):
    t, dd = x2.shape
    tt = _tile(t, 256)
    nsteps = t // tt

    def body(x_ref, o_ref, g_ref, t_ref, m_ref, loss_ref, dx_ref, do_ref, st_ref, lacc):
        i = pl.program_id(0)

        @pl.when(i == 0)
        def _():
            st_ref[...] = jnp.zeros_like(st_ref)
            lacc[...] = jnp.zeros_like(lacc)

        xv = x_ref[...]
        g = g_ref[...]
        r = lax.rsqrt(jnp.mean(xv * xv, axis=-1, keepdims=True) + EPS)
        nv = xv * r
        err = nv * g - t_ref[...]
        lacc[...] += jnp.sum(err * err, axis=0, keepdims=True)
        dy = err * (1.0 / dd)
        st_ref[0:1, :] += jnp.sum(dy * nv, axis=0, keepdims=True)
        dn = dy * g
        dx = r * (dn - nv * jnp.mean(dn * nv, axis=-1, keepdims=True))
        dx_ref[...] = dx
        st_ref[1:2, :] += jnp.sum(dx * o_ref[...].astype(F32), axis=0, keepdims=True)
        do_ref[...] = (dx * m_ref[5:6, :]).astype(BF16)

        @pl.when(i == nsteps - 1)
        def _():
            total = jnp.sum(lacc[...], axis=-1, keepdims=True) * (0.5 / dd)
            loss_ref[...] = jnp.broadcast_to(total, loss_ref.shape)

    row = pl.BlockSpec((tt, dd), lambda i: (i, 0))
    return pl.pallas_call(
        body, name=name, grid=(nsteps,),
        in_specs=[row, row, pl.BlockSpec((1, dd), lambda i: (0, 0)), row, pl.BlockSpec((8, dd), lambda i: (0, 0))],
        out_specs=[pl.BlockSpec((8, LANE), lambda i: (0, 0)), row, row, pl.BlockSpec((8, dd), lambda i: (0, 0))],
        out_shape=[jax.ShapeDtypeStruct((8, LANE), F32), jax.ShapeDtypeStruct((t, dd), F32),
                   jax.ShapeDtypeStruct((t, dd), BF16), jax.ShapeDtypeStruct((8, dd), F32)],
        scratch_shapes=[pltpu.VMEM((1, dd), F32)],
        compiler_params=_params(("arbitrary",)),
    )(x2, o2, final_g, target, mod)


def _lru_gates(xt, prev8, lp, wa_ref, wx_ref, first_tile, cw):
    tt = xt.shape[0]
    ext = jnp.concatenate([prev8, xt], axis=0)
    xs = [pltpu.roll(ext, s, 0)[8:, :] if s else xt for s in range(CONV_WIDTH)]
    xc = lp[0:1, :] + lp[7:8, :] * xs[0] + lp[6:7, :] * xs[1] + lp[5:6, :] * xs[2] + lp[4:5, :] * xs[3]
    xcb = xc.astype(BF16)
    za, zi = [], []
    for jb in range(cw // LRU_BLOCK_W):
        sl = slice(jb * LRU_BLOCK_W, (jb + 1) * LRU_BLOCK_W)
        za.append(jnp.dot(xcb[:, sl], wa_ref[jb].astype(BF16), preferred_element_type=F32))
        zi.append(jnp.dot(xcb[:, sl], wx_ref[jb].astype(BF16), preferred_element_type=F32))
    ra = _sigmoid(jnp.concatenate(za, axis=1) + lp[1:2, :])
    ri = _sigmoid(jnp.concatenate(zi, axis=1) + lp[2:3, :])
    sp = _softplus(-lp[3:4, :])
    log_a = -LRU_C * ra * sp
    av = jnp.exp(log_a)
    seq_start = jnp.logical_and(first_tile, lax.broadcasted_iota(jnp.int32, (tt, 1), 0) == 0)
    mult = jnp.where(seq_start, 1.0, jnp.sqrt(_neg_expm1(2.0 * log_a, av * av)))
    return xs, xc, xcb, ra, ri, sp, av, mult, seq_start


def _scan_groups(a, u, reverse):
    tt, cw = a.shape
    a3, u3 = a.reshape(tt // 8, 8, cw), u.reshape(tt // 8, 8, cw)
    r = lax.broadcasted_iota(jnp.int32, (1, 8, 1), 1)
    for s in (1, 2, 4):
        shift = 8 - s if reverse else s
        take = r < 8 - s if reverse else r >= s
        u3 = jnp.where(take, a3 * pltpu.roll(u3, shift, 1) + u3, u3)
        a3 = jnp.where(take, a3 * pltpu.roll(a3, shift, 1), a3)
    return a3, u3


def _scan_carry(a_s, u_s, h_s, carry, reverse):
    ng = a_s.shape[0]

    def step(i, h):
        g = ng - 1 - i if reverse else i
        hg = u_s[g] + a_s[g] * h
        h_s[g] = hg
        return hg[0:1, :] if reverse else hg[7:8, :]

    return lax.fori_loop(0, ng, step, carry, unroll=4)


def _lru_fwd(proj, lp, wa, wx, name, jobs=()):
    t = proj.shape[0]
    w = D_MODEL
    cw = LRU_CW
    tt = _tile(t, 256)
    nb = cw // LRU_BLOCK_W

    def body(lx_ref, lg_ref, lp_ref, wa_ref, wx_ref, rec_ref, ya_ref, a_s, u_s, h_s, halo, carry):
        ti = pl.program_id(1)

        @pl.when(ti == 0)
        def _():
            halo[...] = jnp.zeros_like(halo)
            carry[...] = jnp.zeros_like(carry)

        xt = lx_ref[...].astype(F32)
        lp_v = lp_ref[...]
        _, xc, _, _, ri, _, av, mult, _ = _lru_gates(xt, halo[...], lp_v, wa_ref, wx_ref, ti == 0, cw)
        halo[...] = xt[tt - 8:, :]
        a_s[...], u_s[...] = _scan_groups(av, mult * (ri * xc), False)
        carry[...] = _scan_carry(a_s, u_s, h_s, carry[...], False)
        rec = h_s[...].reshape(tt, cw)
        rec_ref[...] = rec.astype(BF16)
        gelu, _ = _gelu_tanh(lg_ref[...].astype(F32))
        ya_ref[...] = (rec * gelu).astype(BF16)

    off_g = OFF_LG // cw
    res, jres = _call(
        body, name=name, grid=(w // cw, t // tt),
        in_specs=[pl.BlockSpec((tt, cw), lambda ci, ti: (ti, ci)),
                  pl.BlockSpec((tt, cw), lambda ci, ti: (ti, off_g + ci)),
                  pl.BlockSpec((8, cw), lambda ci, ti: (0, ci)),
                  pl.BlockSpec((nb, LRU_BLOCK_W, LRU_BLOCK_W), lambda ci, ti: (ci, 0, 0)),
                  pl.BlockSpec((nb, LRU_BLOCK_W, LRU_BLOCK_W), lambda ci, ti: (ci, 0, 0))],
        out_specs=[pl.BlockSpec((tt, cw), lambda ci, ti: (ti, ci))] * 2,
        out_shape=[jax.ShapeDtypeStruct((t, w), BF16)] * 2,
        scratch_shapes=[pltpu.VMEM((tt // 8, 8, cw), F32)] * 3 + [pltpu.VMEM((8, cw), F32), pltpu.VMEM((1, cw), F32)],
        sem=("parallel", "arbitrary"), args=[proj, proj, lp, wa, wx], jobs=jobs)
    return (res, jres) if jobs else res


def _lru_bwd(proj, rec, dya, lp, wa, wx, name, jobs=()):
    t = proj.shape[0]
    w = D_MODEL
    cw = LRU_CW
    tt = _tile(t, 256)
    nt = t // tt
    nb = cw // LRU_BLOCK_W
    r8 = tt // 8

    def body(lx_ref, lxp_ref, lg_ref, rec_ref, recp_ref, dya_ref, lp_ref, wa_ref, wx_ref,
             dlx_ref, dlg_ref, st_ref, dwa_ref, dwx_ref, a_s, d_s, dh_s, nhalo, carry):
        step_i = pl.program_id(1)
        ti = nt - 1 - step_i

        @pl.when(step_i == 0)
        def _():
            st_ref[...] = jnp.zeros_like(st_ref)
            dwa_ref[...] = jnp.zeros_like(dwa_ref)
            dwx_ref[...] = jnp.zeros_like(dwx_ref)
            nhalo[...] = jnp.zeros_like(nhalo)
            carry[...] = jnp.zeros_like(carry)

        first = ti == 0
        keep = jnp.where(first, 0.0, 1.0)
        xt = lx_ref[...].astype(F32)
        prev8 = lxp_ref[...].astype(F32) * keep
        lp_v = lp_ref[...]
        xs, xc, xcb, ra, ri, sp, av, mult, seq_start = _lru_gates(xt, prev8, lp_v, wa_ref, wx_ref, first, cw)

        lg = lg_ref[...].astype(F32)
        gelu, th = _gelu_tanh(lg)
        dyav = dya_ref[...].astype(F32)
        recv = rec_ref[...].astype(F32)
        dlg_ref[...] = (dyav * recv * _gelu_tanh_grad(lg, th)).astype(BF16)

        drec = dyav * gelu
        e_in = carry[...]
        a_s[...], d_s[...] = _scan_groups(av, av * drec, True)
        carry[...] = _scan_carry(a_s, d_s, dh_s, e_in, True)
        e_next = jnp.concatenate([dh_s[...].reshape(tt, cw), jnp.broadcast_to(e_in, (8, cw))], axis=0)
        dh = drec + pltpu.roll(e_next, tt + 7, 0)[:tt, :]
        rprev8 = recp_ref[...].astype(F32) * keep
        hprev = pltpu.roll(jnp.concatenate([rprev8, recv], axis=0), 1, 0)[8:, :]
        da = dh * hprev
        dmult = jnp.where(seq_start, 0.0, dh * ri * xc)
        dri = dh * mult * xc
        dxc = dh * mult * ri
        dlog_a = da * av - dmult * (av * av) / mult
        dra = dlog_a * (-LRU_C * sp)
        lam = lp_v[3:4, :]
        st_ref[3:4, :] += jnp.sum(dlog_a * ra, axis=0, keepdims=True) * (LRU_C * _sigmoid(-lam))
        dza = dra * ra * (1.0 - ra)
        dzi = dri * ri * (1.0 - ri)
        st_ref[1:2, :] += jnp.sum(dza, axis=0, keepdims=True)
        st_ref[2:3, :] += jnp.sum(dzi, axis=0, keepdims=True)
        dzab, dzib = dza.astype(BF16), dzi.astype(BF16)
        back = []
        for jb in range(nb):
            sl = slice(jb * LRU_BLOCK_W, (jb + 1) * LRU_BLOCK_W)
            dwa_ref[jb] += lax.dot_general(xcb[:, sl], dzab[:, sl], (_DIMS["tn"], ((), ())), preferred_element_type=F32)
            dwx_ref[jb] += lax.dot_general(xcb[:, sl], dzib[:, sl], (_DIMS["tn"], ((), ())), preferred_element_type=F32)
            back.append(
                lax.dot_general(dzab[:, sl], wa_ref[jb].astype(BF16), (_DIMS["nt"], ((), ())), preferred_element_type=F32)
                + lax.dot_general(dzib[:, sl], wx_ref[jb].astype(BF16), (_DIMS["nt"], ((), ())), preferred_element_type=F32))
        dxc = dxc + jnp.concatenate(back, axis=1)
        st_ref[0:1, :] += jnp.sum(dxc, axis=0, keepdims=True)
        for k in range(CONV_WIDTH):
            st_ref[4 + k:5 + k, :] += jnp.sum(dxc * xs[CONV_WIDTH - 1 - k], axis=0, keepdims=True)
        ext = jnp.concatenate([dxc, nhalo[...]], axis=0)
        dlx = lp_v[7:8, :] * dxc
        for s in range(1, CONV_WIDTH):
            dlx = dlx + lp_v[7 - s:8 - s, :] * pltpu.roll(ext, tt + 8 - s, 0)[:tt, :]
        dlx_ref[...] = dlx.astype(BF16)
        nhalo[...] = dxc[0:8, :]

    off_g = OFF_LG // cw
    tile = lambda off: pl.BlockSpec((tt, cw), lambda ci, si: (nt - 1 - si, off + ci))
    prev = lambda off: pl.BlockSpec((8, cw), lambda ci, si: (jnp.maximum((nt - 1 - si) * r8 - 1, 0), off + ci))
    wspec = pl.BlockSpec((nb, LRU_BLOCK_W, LRU_BLOCK_W), lambda ci, si: (ci, 0, 0))
    st_spec = pl.BlockSpec((8, cw), lambda ci, si: (0, ci))
    res, jres = _call(
        body, name=name, grid=(w // cw, nt),
        in_specs=[tile(0), prev(0), tile(off_g), tile(0), prev(0), tile(0), st_spec, wspec, wspec],
        out_specs=[tile(0), tile(0), st_spec, wspec, wspec],
        out_shape=[jax.ShapeDtypeStruct((t, w), BF16), jax.ShapeDtypeStruct((t, w), BF16),
                   jax.ShapeDtypeStruct((8, w), F32),
                   jax.ShapeDtypeStruct((LRU_BLOCKS, LRU_BLOCK_W, LRU_BLOCK_W), F32),
                   jax.ShapeDtypeStruct((LRU_BLOCKS, LRU_BLOCK_W, LRU_BLOCK_W), F32)],
        scratch_shapes=[pltpu.VMEM((tt // 8, 8, cw), F32)] * 3 + [pltpu.VMEM((8, cw), F32), pltpu.VMEM((1, cw), F32)],
        sem=("parallel", "arbitrary"), args=[proj, proj, proj, rec, rec, dya, lp, wa, wx], jobs=jobs)
    return (res, jres) if jobs else res


def _band_valid(first_block):
    qi = lax.broadcasted_iota(jnp.int32, (BLOCK, 2 * BLOCK), 0)
    ki = lax.broadcasted_iota(jnp.int32, (BLOCK, 2 * BLOCK), 1)
    rel = qi + BLOCK - ki
    valid = jnp.logical_and(rel >= 0, rel < BLOCK)
    return jnp.logical_and(valid, jnp.logical_or(ki >= BLOCK, jnp.logical_not(first_block)))


ATTN_STACK_BWD = 8


def _low_half():
    return lax.broadcasted_iota(jnp.int32, (1, LANE), 1) < HEAD_DIM


def _stack_heads(x, h0, ns):
    low = _low_half()
    parts = []
    for g in range(ns):
        h = h0 + g
        slab = x[:, (h // 2) * LANE:(h // 2 + 1) * LANE]
        parts.append(jnp.where(low if h % 2 == 0 else jnp.logical_not(low), slab, jnp.zeros_like(slab)))
    return jnp.concatenate(parts, axis=0)


def _unstack_heads(y, ns):
    low = _low_half()
    pairs = [jnp.where(low, y[(2 * j) * BLOCK:(2 * j + 1) * BLOCK], y[(2 * j + 1) * BLOCK:(2 * j + 2) * BLOCK])
             for j in range(ns // 2)]
    return pairs[0] if len(pairs) == 1 else jnp.concatenate(pairs, axis=1)


def _dup_kv(xall, kv):
    part = xall[:, kv * HEAD_DIM:(kv + 1) * HEAD_DIM]
    return jnp.concatenate([part, part], axis=1)


def _fold_halves(a):
    return a + pltpu.roll(a, HEAD_DIM, 1)


def _group_probs(qg, k2, bias_g, sink_g, lse_g, valid):
    s = lax.dot_general(qg, k2, (_DIMS["nt"], ((), ())), preferred_element_type=F32) * (HEAD_DIM ** -0.5)
    s = jnp.where(valid[None], s.reshape(bias_g.shape) + bias_g, NEG_INF)
    return jnp.exp(s - lse_g), jnp.exp(sink_g - lse_g)


def _group_sinks(s_ref, h0, ns):
    return jnp.concatenate([jnp.full((1, BLOCK, 1), s_ref[h0 + g], F32) for g in range(ns)], axis=0)


def _attn_specs(t):
    nblk = t // BLOCK
    q_spec = pl.BlockSpec((BLOCK, D_MODEL), lambda n: (n, OFF_Q // D_MODEL))
    kc = pl.BlockSpec((BLOCK, KV_WIDTH), lambda n: (n, OFF_K // KV_WIDTH))
    kp = pl.BlockSpec((BLOCK, KV_WIDTH), lambda n: (jnp.maximum(n - 1, 0), OFF_K // KV_WIDTH))
    vc = pl.BlockSpec((BLOCK, KV_WIDTH), lambda n: (n, OFF_V // KV_WIDTH))
    vp = pl.BlockSpec((BLOCK, KV_WIDTH), lambda n: (jnp.maximum(n - 1, 0), OFF_V // KV_WIDTH))
    bias_spec = pl.BlockSpec((N_Q_HEADS, BLOCK, 2 * BLOCK), lambda n: (0, 0, 0))
    sink_spec = pl.BlockSpec(memory_space=pltpu.SMEM)
    return nblk, [q_spec, kp, kc, vp, vc, bias_spec, sink_spec]


def _attn_fwd(proj, bias, sinks, name, jobs=()):
    t = proj.shape[0]
    nblk, in_specs = _attn_specs(t)

    def body(q_ref, kp_ref, kc_ref, vp_ref, vc_ref, b_ref, s_ref, o_ref, lse_ref):
        valid = _band_valid(pl.program_id(0) == 0)
        lane = lax.broadcasted_iota(jnp.int32, (1, LANE), 1)
        q = q_ref[...]
        kall = jnp.concatenate([kp_ref[...], kc_ref[...]], axis=0)
        vall = jnp.concatenate([vp_ref[...], vc_ref[...]], axis=0)
        outs = []
        lse = jnp.zeros((BLOCK, LANE), F32)
        for h in range(N_Q_HEADS):
            kv = h // GROUP
            kk = kall[:, kv * HEAD_DIM:(kv + 1) * HEAD_DIM]
            vv = vall[:, kv * HEAD_DIM:(kv + 1) * HEAD_DIM]
            s = lax.dot_general(q[:, h * HEAD_DIM:(h + 1) * HEAD_DIM], kk, (_DIMS["nt"], ((), ())),
                                preferred_element_type=F32) * (HEAD_DIM ** -0.5)
            s = jnp.where(valid, s + b_ref[h], NEG_INF)
            sink = s_ref[h]
            m = jnp.maximum(jnp.max(s, axis=-1, keepdims=True), sink)
            e = jnp.exp(s - m)
            den = jnp.sum(e, axis=-1, keepdims=True) + jnp.exp(sink - m)
            lse = jnp.where(lane == h, m + jnp.log(den), lse)
            outs.append(jnp.dot((e * (1.0 / den)).astype(BF16), vv, preferred_element_type=F32))
        o_ref[...] = jnp.concatenate(outs, axis=1).astype(BF16)
        lse_ref[...] = lse

    res, jres = _call(
        body, name=name, grid=(nblk,), in_specs=in_specs,
        out_specs=[pl.BlockSpec((BLOCK, D_MODEL), lambda n: (n, 0)), pl.BlockSpec((BLOCK, LANE), lambda n: (n, 0))],
        out_shape=[jax.ShapeDtypeStruct((t, D_MODEL), BF16), jax.ShapeDtypeStruct((t, LANE), F32)],
        sem=("parallel",), args=[proj, proj, proj, proj, proj, bias, sinks], jobs=jobs)
    return (res, jres) if jobs else res


def _attn_bwd(proj, datt, lse, bias, sinks, name, jobs=()):
    t = proj.shape[0]
    nblk, in_specs = _attn_specs(t)
    in_specs = in_specs + [pl.BlockSpec((BLOCK, D_MODEL), lambda n: (n, 0)), pl.BlockSpec((BLOCK, LANE), lambda n: (n, 0))]
    scale = HEAD_DIM ** -0.5

    def body(q_ref, kp_ref, kc_ref, vp_ref, vc_ref, b_ref, s_ref, do_ref, lse_ref,
             dq_ref, dk_ref, dv_ref, db_ref, ds_ref):
        n = pl.program_id(0)

        @pl.when(n == 0)
        def _():
            dk_ref[...] = jnp.zeros_like(dk_ref)
            dv_ref[...] = jnp.zeros_like(dv_ref)
            db_ref[...] = jnp.zeros_like(db_ref)
            ds_ref[...] = jnp.zeros_like(ds_ref)

        valid = _band_valid(n == 0)
        lane = lax.broadcasted_iota(jnp.int32, (1, LANE), 1)
        q = q_ref[...]
        do = do_ref[...]
        lse_all = lse_ref[...]
        kall = jnp.concatenate([kp_ref[...], kc_ref[...]], axis=0)
        vall = jnp.concatenate([vp_ref[...], vc_ref[...]], axis=0)
        low = _low_half()
        dqs, dks, dvs = [], [], []
        dsink_row = jnp.zeros((1, LANE), F32)
        ns = ATTN_STACK_BWD
        for kv in range(N_KV_HEADS):
            k2, v2 = _dup_kv(kall, kv), _dup_kv(vall, kv)
            dk_acc = jnp.zeros((2 * BLOCK, LANE), F32)
            dv_acc = jnp.zeros((2 * BLOCK, LANE), F32)
            for h0 in range(kv * GROUP, (kv + 1) * GROUP, ns):
                qg, dog = _stack_heads(q, h0, ns), _stack_heads(do, h0, ns)
                lse_g = jnp.concatenate([lse_all[:, h0 + g:h0 + g + 1][None] for g in range(ns)], axis=0)
                p, psink = _group_probs(qg, k2, b_ref[h0:h0 + ns], _group_sinks(s_ref, h0, ns), lse_g, valid)
                dp = lax.dot_general(dog, v2, (_DIMS["nt"], ((), ())), preferred_element_type=F32)
                dp = dp.reshape(ns, BLOCK, 2 * BLOCK)
                delta = jnp.sum(p * dp, axis=-1, keepdims=True)
                ds = p * (dp - delta)
                db_ref[h0:h0 + ns] += ds
                dsink = -jnp.sum(psink * delta, axis=1, keepdims=True)
                for g in range(ns):
                    dsink_row = dsink_row + jnp.where(lane == h0 + g, dsink[g], 0.0)
                dsb = ds.reshape(ns * BLOCK, 2 * BLOCK).astype(BF16)
                pb = p.reshape(ns * BLOCK, 2 * BLOCK).astype(BF16)
                dqs.append(_unstack_heads(jnp.dot(dsb, k2, preferred_element_type=F32) * scale, ns))
                dk_acc = dk_acc + lax.dot_general(dsb, qg, (_DIMS["tn"], ((), ())), preferred_element_type=F32)
                dv_acc = dv_acc + lax.dot_general(pb, dog, (_DIMS["tn"], ((), ())), preferred_element_type=F32)
            dks.append(_fold_halves(dk_acc) * scale)
            dvs.append(_fold_halves(dv_acc))
        dq_ref[...] = jnp.concatenate(dqs, axis=1).astype(BF16)
        rows = pl.ds(pl.multiple_of(n * BLOCK, BLOCK), 2 * BLOCK)
        dk_ref[rows, :] += jnp.concatenate([jnp.where(low, dks[0], dks[1]), jnp.where(low, dks[2], dks[3])], axis=1)
        dv_ref[rows, :] += jnp.concatenate([jnp.where(low, dvs[0], dvs[1]), jnp.where(low, dvs[2], dvs[3])], axis=1)
        ds_ref[0:1, :] += dsink_row

    res, jres = _call(
        body, name=name, grid=(nblk,), in_specs=in_specs,
        out_specs=[pl.BlockSpec((BLOCK, D_MODEL), lambda n: (n, 0)),
                   pl.BlockSpec((t + BLOCK, KV_WIDTH), lambda n: (0, 0)),
                   pl.BlockSpec((t + BLOCK, KV_WIDTH), lambda n: (0, 0)),
                   pl.BlockSpec((N_Q_HEADS, BLOCK, 2 * BLOCK), lambda n: (0, 0, 0)),
                   pl.BlockSpec((8, LANE), lambda n: (0, 0))],
        out_shape=[jax.ShapeDtypeStruct((t, D_MODEL), BF16),
                   jax.ShapeDtypeStruct((t + BLOCK, KV_WIDTH), F32),
                   jax.ShapeDtypeStruct((t + BLOCK, KV_WIDTH), F32),
                   jax.ShapeDtypeStruct((N_Q_HEADS, BLOCK, 2 * BLOCK), F32),
                   jax.ShapeDtypeStruct((8, LANE), F32)],
        sem=("arbitrary",), args=[proj, proj, proj, proj, proj, bias, sinks, datt, lse], jobs=jobs)
    return (res, jres) if jobs else res


def _bucket_table():
    qi = np.arange(BLOCK)[:, None]
    ki = np.arange(2 * BLOCK)[None, :]
    rel = np.maximum(qi + BLOCK - ki, 0)
    max_exact = N_BUCKETS // 2
    relf = np.maximum(rel, 1).astype(np.float32)
    large = max_exact + (np.log(relf / max_exact) / math.log(MAX_DISTANCE / max_exact)
                         * (N_BUCKETS - max_exact)).astype(np.int32)
    large = np.minimum(large, N_BUCKETS - 1)
    return np.where(rel < max_exact, rel, large).astype(np.int32)


def _rel_bias_fwd(rel_bias, onehot, name):
    n = onehot.shape[1]
    tn = _tile(n, 8192, LANE)

    def body(r_ref, o_ref, out_ref):
        out_ref[...] = lax.dot_general(r_ref[...], o_ref[...], (_DIMS["tn"], ((), ())),
                                       precision=lax.Precision.HIGHEST, preferred_element_type=F32)

    return pl.pallas_call(
        body, name=name, grid=(n // tn,),
        in_specs=[pl.BlockSpec(rel_bias.shape, lambda j: (0, 0)), pl.BlockSpec((N_BUCKETS, tn), lambda j: (0, j))],
        out_specs=pl.BlockSpec((N_Q_HEADS, tn), lambda j: (0, j)),
        out_shape=jax.ShapeDtypeStruct((N_Q_HEADS, n), F32),
        compiler_params=_params(("parallel",)),
    )(rel_bias, onehot)


def _rel_bias_bwd(dbias, onehot, name):
    def body(d_ref, o_ref, out_ref):
        out_ref[...] = lax.dot_general(d_ref[...], o_ref[...], (_DIMS["nt"], ((), ())),
                                       precision=lax.Precision.HIGHEST, preferred_element_type=F32)

    full = pl.BlockSpec(dbias.shape, lambda: (0, 0))
    return pl.pallas_call(
        body, name=name, in_specs=[full, full],
        out_specs=pl.BlockSpec((N_Q_HEADS, N_BUCKETS), lambda: (0, 0)),
        out_shape=jax.ShapeDtypeStruct((N_Q_HEADS, N_BUCKETS), F32),
        compiler_params=_params(()),
    )(dbias, onehot)


def _part_specs(gparts, part_index, tr, cc):
    blk = pl.BlockSpec((tr, cc), lambda i, *_: (i, 0))
    specs = []
    for i, _ in enumerate(gparts):
        if i in part_index:
            ix = part_index[i]
            if isinstance(ix, int):
                specs.append(pl.BlockSpec((None, tr, cc), functools.partial(lambda i, *_, ix: (ix, i, 0), ix=ix)))
            else:
                specs.append(pl.BlockSpec((None, tr, cc), lambda i, ix_ref: (ix_ref[0], i, 0)))
        else:
            specs.append(blk)
    return blk, specs


def _sum_parts(gparts, part_index, name):
    r, cc = gparts[0].shape[-2:]
    tr = _tile(r, 256)
    dyn = [i for i, ix in part_index.items() if not isinstance(ix, int)]
    assert len(dyn) <= 1

    def body(*refs):
        refs = refs[len(dyn):]
        g = refs[0][...].astype(F32)
        for gr in refs[1:-1]:
            g = g + gr[...].astype(F32)
        refs[-1][...] = g

    blk, specs = _part_specs(gparts, part_index, tr, cc)
    grid_spec = pltpu.PrefetchScalarGridSpec(num_scalar_prefetch=len(dyn), grid=(r // tr,), in_specs=specs, out_specs=blk)
    return pl.pallas_call(
        body, name=name, grid_spec=grid_spec, out_shape=jax.ShapeDtypeStruct((r, cc), F32),
        compiler_params=_params(("parallel",)),
    )(*[part_index[i] for i in dyn], *gparts)


def _adamw(w, m, v, gparts, name, part_index=None):
    r, cc = w.shape
    tr = _tile(r, 128)
    np_ = len(gparts)
    part_index = part_index or {}
    dyn = [i for i, ix in part_index.items() if not isinstance(ix, int)]
    assert len(dyn) <= 1
    bc1 = 1.0 / (1.0 - ADAM_B1 ** ADAM_STEP)
    bc2 = 1.0 / (1.0 - ADAM_B2 ** ADAM_STEP)

    def body(*refs):
        refs = refs[len(dyn):]
        w_ref, m_ref, v_ref = refs[:3]
        g_refs = refs[3:3 + np_]
        g_out, d_out, m_out, v_out = refs[3 + np_:]
        g = g_refs[0][...].astype(F32)
        for gr in g_refs[1:]:
            g = g + gr[...].astype(F32)
        mn = ADAM_B1 * m_ref[...] + (1.0 - ADAM_B1) * g
        vn = ADAM_B2 * v_ref[...] + (1.0 - ADAM_B2) * (g * g)
        g_out[...] = g
        m_out[...] = mn
        v_out[...] = vn
        d_out[...] = -ADAM_LR * ((mn * bc1) / (jnp.sqrt(vn * bc2) + ADAM_EPS) + ADAM_WD * w_ref[...])

    blk, g_specs = _part_specs(gparts, part_index, tr, cc)
    grid_spec = pltpu.PrefetchScalarGridSpec(
        num_scalar_prefetch=len(dyn), grid=(r // tr,),
        in_specs=[blk, blk, blk] + g_specs, out_specs=[blk] * 4)
    return pl.pallas_call(
        body, name=name, grid_spec=grid_spec,
        out_shape=[jax.ShapeDtypeStruct((r, cc), F32)] * 4,
        compiler_params=_params(("parallel",)),
    )(*[part_index[i] for i in dyn], w, m, v, *gparts)


def _row(vec):
    return vec.reshape(1, -1)


def kernel(x, c, w_ada, b_ada, norm1_g, w_in, conv_w, conv_b, lru_wa, lru_ba, lru_wx, lru_bx, lru_lambda, w_lru_out, w_attn_out, attn_sinks, rel_bias, w_out, norm2_g, w_ff1, w_ff2, final_g, loss_target, m_w_ada, m_b_ada, m_norm1_g, m_w_in, m_conv_w, m_conv_b, m_lru_wa, m_lru_ba, m_lru_wx, m_lru_bx, m_lru_lambda, m_w_lru_out, m_w_attn_out, m_attn_sinks, m_rel_bias, m_w_out, m_norm2_g, m_w_ff1, m_w_ff2, m_final_g, v_w_ada, v_b_ada, v_norm1_g, v_w_in, v_conv_w, v_conv_b, v_lru_wa, v_lru_ba, v_lru_wx, v_lru_bx, v_lru_lambda, v_w_lru_out, v_w_attn_out, v_attn_sinks, v_rel_bias, v_w_out, v_norm2_g, v_w_ff1, v_w_ff2, v_final_g):
    dd = D_MODEL
    t = x.shape[1]
    ix, iy, ic = lax.axis_index("x"), lax.axis_index("y"), lax.axis_index("c")
    me = 4 * ix + 2 * iy + ic
    chip = 2 * ix + iy
    c_idx = jnp.reshape(ic, (1,)).astype(jnp.int32)
    chip_idx = jnp.reshape(chip, (1,)).astype(jnp.int32)

    xs = x[0]
    target = loss_target[0]
    ncol_ada = w_ada.shape[2]
    cshard = conv_w.shape[2]

    shards = [jnp.transpose(w_in[0]).astype(BF16), w_lru_out[0].astype(BF16), w_attn_out[0].astype(BF16),
              w_out[0].astype(BF16), jnp.transpose(w_ff1[0]).astype(BF16), w_ff2[0].astype(BF16)]
    s_in, s_lo, s_ao, s_o, s_f1, s_f2 = shards

    def full(g):
        return g.reshape(N_DEV * g.shape[1], dd)

    def stack(g):
        return g.reshape(N_DEV, g.shape[0] // N_DEV, dd)

    (w_in_pair,) = _run_job(_pair_job([s_in]), "ag_w_in_pair")
    w_in_pair = lax.dynamic_update_slice_in_dim(w_in_pair, s_in[None], ic, axis=0)

    pack0 = jnp.zeros((8, dd), F32).at[0:1, :].set(c).at[1:1 + CONV_WIDTH, 0:cshard].set(conv_w[0])
    (g0,) = _all_gather([pack0], "ag_cond")
    c_all = g0[:, 0, :]
    conv_w_full = jnp.transpose(g0[:, 1:1 + CONV_WIDTH, 0:cshard], (1, 0, 2)).reshape(CONV_WIDTH, dd)
    b_cols = lax.dynamic_slice(b_ada, (0, me * ncol_ada), (1, ncol_ada))
    mod_cols = _ada_fwd(c_all, w_ada[0], b_cols, "ada_fwd")
    (g1,) = _all_gather([mod_cols], "ag_mod")
    mod_mine = lax.dynamic_index_in_dim(g1, me, axis=1, keepdims=False).reshape(6, dd)
    mod = jnp.concatenate([mod_mine, jnp.zeros((2, dd), F32)], axis=0)

    bucket = _bucket_table()
    onehot = jnp.asarray((bucket.reshape(1, -1) == np.arange(N_BUCKETS)[:, None]).astype(np.float32))
    bias = _rel_bias_fwd(rel_bias, onehot, "rel_bias_fwd").reshape(N_Q_HEADS, BLOCK, 2 * BLOCK)
    sinks = attn_sinks.reshape(N_Q_HEADS)
    lp = jnp.concatenate([conv_b, lru_ba, lru_bx, lru_lambda, conv_w_full], axis=0)
    wa, wx = lru_wa[0], lru_wx[0]

    tm = _tile(t, 1024)
    tn = 512
    h = _norm_mod(xs, norm1_g, mod, 0, "norm1")
    tn2 = 1024
    chunk_w = 2 * s_in.shape[0]

    def ids(*ks):
        return jnp.stack([jnp.asarray(k, jnp.int32) for k in ks])

    tn_in = _tile(chunk_w, 1024, LANE)
    (proj,), ((g_near,),) = _mm(
        h, w_in_pair.reshape(chunk_w, dd), "nt", tm=tm, tn=tn_in, outs=[BF16], name="in_proj_own",
        chunks=(ids(chip), chunk_w, ids(0)), n_total=IN_WIDTH, jobs=[_gather_job([w_in_pair], (0.7,), "near")])
    near_ids = ids(2 * (1 - ix) + iy, 2 * ix + 1 - iy)
    (proj,), ((g_far,),) = _mm(
        h, full(g_near), "nt", tm=tm, tn=tn_in, outs=[BF16], name="in_proj_near",
        chunks=(near_ids, chunk_w, near_ids), into=proj, jobs=[_gather_job([g_near], (0.6,), "far")])
    far_chip = 2 * (1 - ix) + 1 - iy
    (proj,), ((g_o,),) = _mm(
        h, full(g_far), "nt", tm=tm, tn=tn_in, outs=[BF16], name="in_proj_far",
        chunks=(ids(far_chip), chunk_w, ids(far_chip)), into=proj, jobs=[_gather_job([s_o], (0.5, 0.8))])
    far_slots = lax.dynamic_slice_in_dim(g_far, 2 * far_chip, 2, axis=0)
    win_t = lax.dynamic_update_slice_in_dim(g_near, far_slots, 2 * far_chip, axis=0)
    win_t = full(lax.dynamic_update_slice_in_dim(win_t, w_in_pair, 2 * chip, axis=0))
    (g_o,) = _fill_own([g_o], [s_o])
    (rec, ya), ((g_f1,),) = _lru_fwd(proj, lp, wa, wx, "lru_fwd", jobs=[_gather_job([s_f1], (0.6, 0.9))])
    (att, lse), ((g_lo,), (g_ao,)) = _attn_fwd(proj, bias, sinks, "attn_fwd",
                                        jobs=[_gather_job([s_lo], (0.3, 0.45)), _gather_job([s_ao], (0.75, 0.9))])
    g_lo, g_ao, g_f1 = _fill_own([g_lo, g_ao, g_f1], [s_lo, s_ao, s_f1])
    wlo, wao, wo, wf1_t = full(g_lo), full(g_ao), full(g_o), full(g_f1)
    (y_a,) = _mm(ya, wlo, "nn", tm=tm, tn=tn2, outs=[BF16], name="lru_out")

    def merge_epi(acc, ya_t, ga_t, gb_t):
        return acc, _sigmoid(ga_t.astype(F32)) * ya_t.astype(F32) + _sigmoid(gb_t.astype(F32)) * acc

    y_b, merged = _mm(att, wao, "nn", tm=tm, tn=tn, outs=[BF16, BF16], epi=merge_epi,
                      tiles=[(y_a, 0), (proj, OFF_GA // tn), (proj, OFF_GB // tn)], name="attn_out_merge")

    def resid_epi(row):
        def epi(acc, x_t, mod_t):
            return acc, x_t + mod_t[row:row + 1, :] * acc
        return epi

    o1, x1 = _mm(merged, wo, "nn", tm=tm, tn=tn2, outs=[BF16, F32], epi=resid_epi(2),
                 tiles=[(xs, 0)], rows=[mod], name="out_proj")
    h2 = _norm_mod(x1, norm2_g, mod, 3, "norm2")

    def relu2_epi(acc):
        rl = jnp.maximum(acc, 0.0)
        return acc, rl * rl

    (f_pre, ff), ((g_f2,),) = _mm(h2, wf1_t, "nt", tm=tm, tn=1024, outs=[BF16, BF16], epi=relu2_epi, name="ff1",
                                  jobs=[_gather_job([s_f2], (0.6, 0.9))])
    wf2 = full(_fill_own([g_f2], [s_f2])[0])
    tmh = _tile(t, 512)
    o2, x2 = _mm(ff, wf2, "nn", tm=tmh, tn=tn, outs=[BF16, F32], epi=resid_epi(5),
                 tiles=[(x1, 0)], rows=[mod], name="ff2")

    loss_blk, dx2, do2, st_f = _final_loss(x2, o2, _row(final_g), target, mod, "final_loss")
    loss = lax.psum(loss_blk[0, 0], ("x", "y", "c"))

    def drelu2_epi(acc, f_t):
        return (acc * (2.0 * jnp.maximum(f_t.astype(F32), 0.0)),)

    tmw, tnw, tkw = 512, dd, _tile(t, 2048)
    (df,) = _mm(do2, wf2, "nt", tm=tm, tn=1024, outs=[BF16], epi=drelu2_epi, tiles=[(f_pre, 0)], name="ff2_dx")
    (dwf2,) = _mm(ff, do2, "tn", tm=tmw, tn=tnw, tk=tkw, outs=[BF16], name="ff2_dw")
    (dh2,), ((sib_f2,),) = _mm(df, wf1_t, "nn", tm=tmh, tn=tn, outs=[BF16], name="ff1_dx",
                              jobs=[_sibling_job([stack(dwf2)])])
    pair_f2 = _pair_add(stack(dwf2), sib_f2, c_idx, "rs_pair_add_w_ff2")
    (dwf1_t,), ((chips_f2,),) = _mm(df, h2, "tn", tm=tmw, tn=tnw, tk=tkw, outs=[BF16], name="ff1_dw",
                                    jobs=[_chips_job([pair_f2])])
    (dx1, st_2, do1), ((sib_f1,),) = _norm_mod_bwd(x1, norm2_g, mod, 3, dh2, dx2, (o1, 2), "norm2_bwd",
                                                   jobs=[_sibling_job([stack(dwf1_t)])])
    pair_f1 = _pair_add(stack(dwf1_t), sib_f1, c_idx, "rs_pair_add_w_ff1")

    def dmerge_epi(acc, ga_t, gb_t, ya_t, yb_t):
        sa, sb = _sigmoid(ga_t.astype(F32)), _sigmoid(gb_t.astype(F32))
        return (acc * sa, acc * sb, acc * ya_t.astype(F32) * sa * (1.0 - sa), acc * yb_t.astype(F32) * sb * (1.0 - sb))

    tmh = _tile(t, 512)
    dy_a, dy_b, dga, dgb = _mm(do1, wo, "nt", tm=tmh, tn=tn, outs=[BF16] * 4, epi=dmerge_epi,
                               tiles=[(proj, OFF_GA // tn), (proj, OFF_GB // tn), (y_a, 0), (y_b, 0)], name="out_proj_dx")
    (dwo,) = _mm(merged, do1, "tn", tm=tmw, tn=tnw, tk=tkw, outs=[BF16], name="out_proj_dw")
    (datt,) = _mm(dy_b, wao, "nt", tm=tm, tn=tn2, outs=[BF16], name="attn_out_dx")
    (dwao,) = _mm(att, dy_b, "tn", tm=tmw, tn=tnw, tk=tkw, outs=[BF16], name="attn_out_dw")
    (dya,) = _mm(dy_a, wlo, "nt", tm=tm, tn=tn2, outs=[BF16], name="lru_out_dx")
    (dwlo,) = _mm(ya, dy_a, "tn", tm=tmw, tn=tnw, tk=tkw, outs=[BF16], name="lru_out_dw")
    (dq, dk_pad, dv_pad, dbias, dsinks), (sib_3, (chips_f1,)) = _attn_bwd(
        proj, datt, lse, bias, sinks, "attn_bwd",
        jobs=[_sibling_job([stack(dwlo), stack(dwao), stack(dwo)]), _chips_job([pair_f1])])
    pair_lo, pair_ao, pair_o = [_pair_add(stack(g_), s_, c_idx, "rs_pair_add_" + nm_)
                                for g_, s_, nm_ in zip((dwlo, dwao, dwo), sib_3, ("w_lru_out", "w_attn_out", "w_out"))]
    (dlx, dlg, st_l, dwa, dwx), ((chips_lo, chips_ao, chips_o),) = _lru_bwd(
        proj, rec, dya, lp, wa, wx, "lru_bwd", jobs=[_chips_job([pair_lo, pair_ao, pair_o])])
    dproj = jnp.concatenate([dlx, dlg, dq, dk_pad[BLOCK:].astype(BF16), dv_pad[BLOCK:].astype(BF16), dga, dgb], axis=1)
    lru_mats = jnp.concatenate([dwa.reshape(-1, dd), dwx.reshape(-1, dd)], axis=0)
    (dwin_t,), ((mats_all,),) = _mm(dproj, h, "tn", tm=384, tn=tnw, tk=tkw, outs=[BF16], name="in_proj_dw",
                                   jobs=[_gather_job([lru_mats], (0.3, 0.6))])
    (mats_all,) = _fill_own([mats_all], [lru_mats])
    (sib_in,) = _exchange_sibling([stack(dwin_t)], "rs_sibling_w_in")
    pair_in = _pair_add(stack(dwin_t), sib_in, c_idx, "rs_pair_add_w_in", widths=(768, 768, 512))
    n_rb = t // tmh
    if n_rb >= 2:
        (dh,), ((chips_in_0,),) = _mm(dproj, win_t, "nn", tm=tmh, tn=tn, tk=IN_WIDTH // 2, outs=[BF16],
                                     name="in_proj_dx_top", rows_part=(0, n_rb // 2), jobs=[_chips_job(pair_in[0:1])])
        (dh,), ((chips_in_1,),) = _mm(dproj, win_t, "nn", tm=tmh, tn=tn, tk=IN_WIDTH // 2, outs=[BF16],
                                     name="in_proj_dx_bottom", rows_part=(n_rb // 2, n_rb - n_rb // 2), into=dh,
                                     jobs=[_chips_job(pair_in[1:2])])
    else:
        (dh,), ((chips_in_0,), (chips_in_1,)) = _mm(
            dproj, win_t, "nn", tm=tmh, tn=tn, tk=IN_WIDTH // 2, outs=[BF16], name="in_proj_dx",
            jobs=[_chips_job(pair_in[0:1]), _chips_job(pair_in[1:2])])
    (grad_x, st_1), ((chips_in_2,),) = _norm_mod_bwd(xs, norm1_g, mod, 0, dh, dx1, None, "norm1_bwd",
                                                     jobs=[_chips_job(pair_in[2:3])])
    chips_in = [chips_in_0, chips_in_1, chips_in_2]

    drel = jnp.transpose(_rel_bias_bwd(dbias.reshape(N_Q_HEADS, -1), onehot, "rel_bias_bwd"))

    misc = jnp.concatenate([drel.reshape(1, -1), dsinks[0:1, 0:N_Q_HEADS],
                            jnp.zeros((1, dd - N_BUCKETS * N_Q_HEADS - N_Q_HEADS), F32)], axis=1)
    zero_row = jnp.zeros((1, dd), F32)
    small = jnp.concatenate([
        st_1[0:2], st_2[3:4], st_2[0:2], st_f[1:2], zero_row, zero_row,
        st_1[2:3], st_l[0:1], st_l[1:2], st_l[2:3], st_l[3:4], st_2[2:3], st_f[0:1], misc,
        st_l[4:8], jnp.zeros((4, dd), F32)], axis=0)
    (small_all,) = _all_gather([small], "ag_small")

    def pack_small(b_, n1, cb, ba, bx, lam, n2, fg, rb, sk):
        misc_ = jnp.concatenate([rb.reshape(1, -1), sk.reshape(1, -1),
                                 jnp.zeros((1, dd - N_BUCKETS * N_Q_HEADS - N_Q_HEADS), F32)], axis=1)
        return jnp.concatenate([b_.reshape(6, dd), jnp.zeros((2, dd), F32), n1, cb, ba, bx, lam, n2, _row(fg), misc_,
                                jnp.zeros((8, dd), F32)], axis=0)

    def pack_mats(wa_, wx_):
        return jnp.concatenate([wa_.reshape(-1, dd), wx_.reshape(-1, dd)], axis=0)

    every = {i: i for i in range(N_DEV)}
    w_s = pack_small(b_ada, norm1_g, conv_b, lru_ba, lru_bx, lru_lambda, norm2_g, final_g, rel_bias, attn_sinks)
    m_s = pack_small(m_b_ada, m_norm1_g, m_conv_b, m_lru_ba, m_lru_bx, m_lru_lambda, m_norm2_g, m_final_g, m_rel_bias, m_attn_sinks)
    v_s = pack_small(v_b_ada, v_norm1_g, v_conv_b, v_lru_ba, v_lru_bx, v_lru_lambda, v_norm2_g, v_final_g, v_rel_bias, v_attn_sinks)
    small_res = _adamw(w_s, m_s, v_s, [small_all] * N_DEV, "adamw_small", part_index=every)
    mats_res = _adamw(pack_mats(lru_wa, lru_wx), pack_mats(m_lru_wa, m_lru_wx), pack_mats(v_lru_wa, v_lru_wx),
                      [mats_all] * N_DEV, "adamw_lru_mats", part_index=every)

    def unpack_small(s, mt):
        nb_ = N_BUCKETS * N_Q_HEADS
        half = mt.shape[0] // 2
        return dict(
            b_ada=s[0:6].reshape(1, 6 * dd), norm1_g=s[8:9], conv_b=s[9:10], lru_ba=s[10:11], lru_bx=s[11:12],
            lru_lambda=s[12:13], norm2_g=s[13:14], final_g=s[14], rel_bias=s[15, 0:nb_].reshape(N_BUCKETS, N_Q_HEADS),
            attn_sinks=s[15:16, nb_:nb_ + N_Q_HEADS],
            lru_wa=mt[:half].reshape(1, LRU_BLOCKS, LRU_BLOCK_W, LRU_BLOCK_W),
            lru_wx=mt[half:].reshape(1, LRU_BLOCKS, LRU_BLOCK_W, LRU_BLOCK_W))

    res = {k: [None] * 4 for k in ("w_ada", "w_in", "conv_w", "w_lru_out", "w_attn_out", "w_out", "w_ff1", "w_ff2")}
    for q_, (s_, mt_) in enumerate(zip(small_res, mats_res)):
        for k_, val in unpack_small(s_, mt_).items():
            res.setdefault(k_, [None] * 4)[q_] = val

    g_conv = lax.dynamic_slice(small_res[0][16:16 + CONV_WIDTH], (0, me * cshard), (CONV_WIDTH, cshard))
    conv_res = _adamw(conv_w[0], m_conv_w[0], v_conv_w[0], [g_conv], "adamw_conv")
    res["conv_w"] = [r_[None] for r_ in conv_res]

    dmod_all = small_all[:, 0:6, :].reshape(N_DEV, 6 * dd)
    dmod_cols = lax.dynamic_slice(dmod_all, (0, me * ncol_ada), (N_DEV, ncol_ada))
    g_ada = _ada_bwd(jnp.transpose(c_all), dmod_cols, "ada_bwd")
    ada_res = _adamw(w_ada[0], m_w_ada[0], v_w_ada[0], [g_ada], "adamw_ada")
    res["w_ada"] = [r_[None] for r_ in ada_res]

    pair = [None, pair_lo, pair_ao, pair_o, pair_f1, pair_f2]
    from_chips = [None, chips_lo, chips_ao, chips_o, chips_f1, chips_f2]
    own_and_chips = {0: chip_idx, 1: 0, 2: 1, 3: 2}

    def summed(i):
        return [pair[i], from_chips[i], from_chips[i], from_chips[i]], {0: chip_idx, 1: 0, 2: 1, 3: 2}

    def sum_only(i, name):
        parts, index = summed(i)
        return _sum_parts(parts, index, name)

    g_in = jnp.transpose(jnp.concatenate(
        [_sum_parts([p_, c_, c_, c_], own_and_chips, "sum_w_in_%d" % q_)
         for q_, (p_, c_) in enumerate(zip(pair_in, chips_in))], axis=1))
    res["w_in"] = [r_[None] for r_ in _adamw(w_in[0], m_w_in[0], v_w_in[0], [g_in], "adamw_w_in")]
    g_ff1 = jnp.transpose(sum_only(4, "sum_w_ff1"))
    res["w_ff1"] = [r_[None] for r_ in _adamw(w_ff1[0], m_w_ff1[0], v_w_ff1[0], [g_ff1], "adamw_w_ff1")]
    for i, (nm, w_, m_, v_) in {1: ("w_lru_out", w_lru_out, m_w_lru_out, v_w_lru_out),
                                2: ("w_attn_out", w_attn_out, m_w_attn_out, v_w_attn_out),
                                3: ("w_out", w_out, m_w_out, v_w_out),
                                5: ("w_ff2", w_ff2, m_w_ff2, v_w_ff2)}.items():
        parts, index = summed(i)
        res[nm] = [r_[None] for r_ in _adamw(w_[0], m_[0], v_[0], parts, "adamw_" + nm, part_index=index)]

    order = ["w_ada", "b_ada", "norm1_g", "w_in", "conv_w", "conv_b", "lru_wa", "lru_ba", "lru_wx", "lru_bx",
             "lru_lambda", "w_lru_out", "w_attn_out", "attn_sinks", "rel_bias", "w_out", "norm2_g", "w_ff1",
             "w_ff2", "final_g"]
    out = [loss, grad_x[None]]
    for q_ in range(4):
        out += [res[k_][q_] for k_ in order]
    return tuple(out)
```

```python
import functools
import math

import numpy as np
import jax
import jax.numpy as jnp
from jax import lax
from jax.experimental import pallas as pl
from jax.experimental.pallas import tpu as pltpu

F32 = jnp.float32
BF16 = jnp.bfloat16
MESH = pl.DeviceIdType.MESH

D_MODEL = 2048
N_Q_HEADS = 32
N_KV_HEADS = 4
GROUP = N_Q_HEADS // N_KV_HEADS
HEAD_DIM = 64
KV_WIDTH = N_KV_HEADS * HEAD_DIM
BLOCK = 128
NEG_INF = -1e30
N_BUCKETS = 32
MAX_DISTANCE = 128
LRU_BLOCKS = 16
LRU_BLOCK_W = 128
CONV_WIDTH = 4
LRU_C = 8.0
D_FF = 4 * D_MODEL
EPS = 1e-6
IN_WIDTH = 5 * D_MODEL + 2 * KV_WIDTH
OFF_LX, OFF_LG, OFF_Q, OFF_K, OFF_V, OFF_GA, OFF_GB = 0, 2048, 4096, 6144, 6400, 6656, 8704

ADAM_LR, ADAM_B1, ADAM_B2, ADAM_EPS, ADAM_WD, ADAM_STEP = 0.001, 0.9, 0.999, 1e-08, 0.01, 10

N_DEV = 8
VMEM_LIMIT_BYTES = 48 * 1024 * 1024
LANE = 128
LRU_CW = 512


def _params(sem, **kw):
    return pltpu.CompilerParams(dimension_semantics=sem, vmem_limit_bytes=VMEM_LIMIT_BYTES, **kw)


def _tile(n, pref, mult=8):
    if n <= pref:
        return n
    t = (pref // mult) * mult
    while t >= mult:
        if n % t == 0:
            return t
        t -= mult
    return n


def _sigmoid(x):
    return 0.5 * jnp.tanh(0.5 * x) + 0.5


def _gelu_tanh(x):
    k = math.sqrt(2.0 / math.pi)
    t = jnp.tanh(k * (x + 0.044715 * x * x * x))
    return 0.5 * x * (1.0 + t), t


def _gelu_tanh_grad(x, t):
    k = math.sqrt(2.0 / math.pi)
    return 0.5 * (1.0 + t) + 0.5 * x * (1.0 - t * t) * k * (1.0 + 3.0 * 0.044715 * x * x)


def _log1p(e):
    u = 1.0 + e
    return jnp.where(u == 1.0, e, jnp.log(u) * (e / jnp.where(u == 1.0, 1.0, u - 1.0)))


def _softplus(x):
    return jnp.maximum(x, 0.0) + _log1p(jnp.exp(-jnp.abs(x)))


def _neg_expm1(x, exp_x):
    series = -x * (1.0 + x * (0.5 + x * (1.0 / 6.0 + x * (1.0 / 24.0))))
    return jnp.where(x > -0.0625, series, 1.0 - exp_x)


def _my_place():
    return lax.axis_index("x"), lax.axis_index("y"), lax.axis_index("c")


def _all_gather(arrs, name):
    return _fill_own(_run_job(_gather_job(arrs), name), arrs)


def _fill_own(stacks, shards):
    me = 4 * lax.axis_index("x") + 2 * lax.axis_index("y") + lax.axis_index("c")
    out = []
    for g, s in zip(stacks, shards):
        if g.size * g.dtype.itemsize <= (1 << 20):
            slot = lax.broadcasted_iota(jnp.int32, g.shape, 0)
            out.append(jnp.where(slot == me, s[None], g))
        else:
            out.append(lax.dynamic_update_slice_in_dim(g, s[None], me, axis=0))
    return out


class _Job:
    def __init__(self, ins, outs, sems, start, finish, forwards=(), forward_at=()):
        self.ins, self.outs, self.sems = list(ins), list(outs), list(sems)
        self.start, self.finish, self.forwards, self.forward_at = start, finish, list(forwards), list(forward_at)
        assert len(self.forwards) <= len(self.forward_at)


def _gather_job(arrs, forward_at=(0.6, 0.85), part="all"):
    n = len(arrs)
    whole, near, far = part == "all", part == "near", part == "far"

    def copies(ins, outs, sems):
        send_sems, recv_sems, local_sems = sems
        x, y, c = _my_place()
        sib = (x, y, 1 - c)
        xn, yn, dg = (1 - x, y), (x, 1 - y), (1 - x, 1 - y)

        def slot(px, py, pc):
            return 4 * px + 2 * py + pc

        def cp(a, k, src, dst_slot, to):
            return pltpu.make_async_remote_copy(
                src_ref=src, dst_ref=outs[a].at[dst_slot], send_sem=send_sems.at[a, k], recv_sem=recv_sems.at[a, k],
                device_id=to, device_id_type=MESH)

        def arrival(a, k, dst_slot):
            return cp(a, k, outs[a].at[dst_slot], dst_slot, sib)

        me_slot = slot(x, y, c)
        sends, local, relay, passes, arrive = {}, [], {}, {}, {}
        for a in range(n):
            if whole:
                own = ins[a]
                sends[a, 0] = cp(a, 0, own, me_slot, sib)
            if near:
                own = ins[a].at[c]
            if whole or near:
                sends[a, 1] = cp(a, 1, own, me_slot, (*xn, c))
                sends[a, 2] = cp(a, 2, own, me_slot, (*yn, c))
                passes[a, 4] = cp(a, 4, outs[a].at[slot(*xn, c)], slot(*xn, c), sib)
                passes[a, 5] = cp(a, 5, outs[a].at[slot(*yn, c)], slot(*yn, c), sib)
            if whole or far:
                source = outs[a] if whole else ins[a]
                relayed = c * slot(*yn, c) + (1 - c) * slot(*xn, c)
                relay[a, 3] = cp(a, 3, source.at[relayed], relayed, (x + c * (1 - 2 * x), y + (1 - c) * (1 - 2 * y), c))
                passes[a, 6] = cp(a, 6, outs[a].at[slot(*dg, c)], slot(*dg, c), sib)
            arrive[a, 0] = arrival(a, 0, slot(x, y, 1 - c))
            arrive[a, 1] = arrival(a, 1, slot(*xn, c))
            arrive[a, 2] = arrival(a, 2, slot(*yn, c))
            arrive[a, 3] = arrival(a, 3, slot(*dg, c))
            arrive[a, 4] = arrival(a, 4, slot(*xn, 1 - c))
            arrive[a, 5] = arrival(a, 5, slot(*yn, 1 - c))
            arrive[a, 6] = arrival(a, 6, slot(*dg, 1 - c))
        return sends, local, relay, passes, arrive

    def pick(d, ks):
        return [d[a, k] for a in range(n) for k in ks if (a, k) in d]

    def start(ins, outs, sems):
        sends, local, relay, _, _ = copies(ins, outs, sems)
        for cp in local + pick(sends, (0, 1, 2)) + (pick(relay, (3,)) if far else []):
            cp.start()

    def forward_neighbours(ins, outs, sems):
        _, _, relay, passes, arrive = copies(ins, outs, sems)
        for cp in pick(arrive, (1, 2)):
            cp.wait_recv()
        for cp in pick(relay, (3,)) + pick(passes, (4, 5)):
            cp.start()

    def forward_diagonal(ins, outs, sems):
        _, _, _, passes, arrive = copies(ins, outs, sems)
        for cp in pick(arrive, (3,)):
            cp.wait_recv()
        for cp in pick(passes, (6,)):
            cp.start()

    def finish(ins, outs, sems):
        sends, local, relay, passes, arrive = copies(ins, outs, sems)
        for cp in pick(arrive, (6,) if far else (0, 4, 5, 6) if whole else (4, 5)):
            cp.wait_recv()
        for cp in pick(sends, (0, 1, 2)) + pick(relay, (3,)) + pick(passes, (4, 5, 6)):
            cp.wait_send()
        for cp in local:
            cp.wait()

    shapes = [a.shape[-2:] for a in arrs]
    forwards = [forward_diagonal] if far else [forward_neighbours] if near else [forward_neighbours, forward_diagonal]
    return _Job(arrs, [jax.ShapeDtypeStruct((N_DEV,) + s, a.dtype) for s, a in zip(shapes, arrs)],
                [pltpu.SemaphoreType.DMA((n, 7)), pltpu.SemaphoreType.DMA((n, 7)), pltpu.SemaphoreType.DMA((n,))],
                start, finish, forwards, forward_at)


def _pair_job(arrs):
    n = len(arrs)

    def copies(ins, outs, sems):
        send_sems, recv_sems, local_sems = sems
        x, y, c = _my_place()
        remote = [pltpu.make_async_remote_copy(
            src_ref=ins[a], dst_ref=outs[a].at[c], send_sem=send_sems.at[a], recv_sem=recv_sems.at[a],
            device_id=(x, y, 1 - c), device_id_type=MESH) for a in range(n)]
        local = []
        arrive = [pltpu.make_async_remote_copy(
            src_ref=ins[a], dst_ref=outs[a].at[1 - c], send_sem=send_sems.at[a], recv_sem=recv_sems.at[a],
            device_id=(x, y, 1 - c), device_id_type=MESH) for a in range(n)]
        return remote, local, arrive

    def start(ins, outs, sems):
        remote, local, _ = copies(ins, outs, sems)
        for cp in remote + local:
            cp.start()

    def finish(ins, outs, sems):
        remote, local, arrive = copies(ins, outs, sems)
        for cp in arrive:
            cp.wait_recv()
        for cp in remote:
            cp.wait_send()
        for cp in local:
            cp.wait()

    return _Job(arrs, [jax.ShapeDtypeStruct((2,) + a.shape, a.dtype) for a in arrs],
                [pltpu.SemaphoreType.DMA((n,)), pltpu.SemaphoreType.DMA((n,)), pltpu.SemaphoreType.DMA((n,))],
                start, finish)


def _run_job(job, name):
    ni, no = len(job.ins), len(job.outs)

    def body(*refs):
        ins, outs, sems = refs[:ni], refs[ni:ni + no], refs[ni + no:]
        job.start(ins, outs, sems)
        for fwd in job.forwards:
            fwd(ins, outs, sems)
        job.finish(ins, outs, sems)

    any_spec = pl.BlockSpec(memory_space=pl.ANY)
    return pl.pallas_call(
        body, name=name, out_shape=job.outs, in_specs=[any_spec] * ni, out_specs=[any_spec] * no,
        scratch_shapes=job.sems,
    )(*job.ins)


def _call(body, *, name, grid, in_specs, out_specs, out_shape, scratch_shapes=(), sem, args, jobs=(),
          prefetch=(), aliases=None):
    in_specs, out_specs, out_shape, scratch_shapes = list(in_specs), list(out_specs), list(out_shape), list(scratch_shapes)
    prefetch = list(prefetch)
    if not jobs and not prefetch and not aliases:
        res = pl.pallas_call(body, name=name, grid=grid, in_specs=in_specs, out_specs=out_specs, out_shape=out_shape,
                             scratch_shapes=scratch_shapes, compiler_params=_params(sem))(*args)
        return list(res), []
    n_in, n_out, n_scr = len(in_specs), len(out_specs), len(scratch_shapes)
    j_in = [len(j.ins) for j in jobs]
    j_out = [len(j.outs) for j in jobs]
    j_sem = [len(j.sems) for j in jobs]
    total = int(np.prod(grid))
    any_spec = pl.BlockSpec(memory_space=pl.ANY)

    def wrapped(*refs):
        pos = [0]

        def take(k):
            part = refs[pos[0]:pos[0] + k]
            pos[0] += k
            return part

        take(len(prefetch))
        ins = take(n_in)
        jins = [take(k) for k in j_in]
        outs = take(n_out)
        jouts = [take(k) for k in j_out]
        scr = take(n_scr)
        jsems = [take(k) for k in j_sem]
        step = pl.program_id(0)
        for d in range(1, len(grid)):
            step = step * grid[d] + pl.program_id(d)
        for j, job in enumerate(jobs):
            pl.when(step == 0)(functools.partial(job.start, jins[j], jouts[j], jsems[j]))
            for fwd, frac in zip(job.forwards, job.forward_at):
                at = min(int(total * frac), total - 1)
                pl.when(step == at)(functools.partial(fwd, jins[j], jouts[j], jsems[j]))
        body(*ins, *outs, *scr)
        for j, job in enumerate(jobs):
            pl.when(step == total - 1)(functools.partial(job.finish, jins[j], jouts[j], jsems[j]))

    grid_spec = pltpu.PrefetchScalarGridSpec(
        num_scalar_prefetch=len(prefetch), grid=grid,
        in_specs=in_specs + [any_spec] * sum(j_in),
        out_specs=out_specs + [any_spec] * sum(j_out),
        scratch_shapes=scratch_shapes + [s for j in jobs for s in j.sems])
    res = pl.pallas_call(
        wrapped, name=name, grid_spec=grid_spec,
        out_shape=out_shape + [o for j in jobs for o in j.outs],
        input_output_aliases={len(prefetch) + i: o for i, o in (aliases or {}).items()},
        compiler_params=_params(("arbitrary",) * len(grid)),
    )(*prefetch, *args, *[a for j in jobs for a in j.ins])
    res = list(res)
    own, rest = res[:n_out], res[n_out:]
    per_job = []
    for k in j_out:
        per_job.append(rest[:k])
        rest = rest[k:]
    return own, per_job


def _exchange_sibling(arrs, name):
    return _run_job(_sibling_job(arrs), name)


def _sibling_job(arrs):
    n = len(arrs)

    def copies(ins, outs, sems):
        send_sems, recv_sems = sems
        x, y, c = _my_place()
        return [pltpu.make_async_remote_copy(
            src_ref=ins[a].at[2 * k + 1 - c], dst_ref=outs[a].at[k],
            send_sem=send_sems.at[a, k], recv_sem=recv_sems.at[a, k],
            device_id=(x, y, 1 - c), device_id_type=MESH) for a in range(n) for k in range(4)]

    def start(ins, outs, sems):
        for cp in copies(ins, outs, sems):
            cp.start()

    def finish(ins, outs, sems):
        for cp in copies(ins, outs, sems):
            cp.wait()

    return _Job(arrs, [jax.ShapeDtypeStruct((4,) + a.shape[1:], a.dtype) for a in arrs],
                [pltpu.SemaphoreType.DMA((n, 4)), pltpu.SemaphoreType.DMA((n, 4))], start, finish)


def _chips_job(arrs):
    n = len(arrs)

    def copies(ins, outs, sems):
        send_sems, recv_sems = sems
        x, y, c = _my_place()
        chips = [(1 - x, y), (x, 1 - y), (1 - x, 1 - y)]
        return [pltpu.make_async_remote_copy(
            src_ref=ins[a].at[2 * px + py], dst_ref=outs[a].at[j],
            send_sem=send_sems.at[a, j], recv_sem=recv_sems.at[a, j],
            device_id=(px, py, c), device_id_type=MESH) for a in range(n) for j, (px, py) in enumerate(chips)]

    def start(ins, outs, sems):
        for cp in copies(ins, outs, sems):
            cp.start()

    def finish(ins, outs, sems):
        for cp in copies(ins, outs, sems):
            cp.wait()

    return _Job(arrs, [jax.ShapeDtypeStruct((3,) + a.shape[1:], a.dtype) for a in arrs],
                [pltpu.SemaphoreType.DMA((n, 3)), pltpu.SemaphoreType.DMA((n, 3))], start, finish)


def _pair_add(stack, from_sibling, c_idx, name, widths=None):
    _, r, cc = stack.shape
    tr = _tile(r, 512)
    ws = [cc] if widths is None else list(widths)
    assert sum(ws) == cc
    starts = [sum(ws[:q]) for q in range(len(ws))]

    def body(c_ref, a_ref, b_ref, *o_refs):
        res = a_ref[...].astype(F32) + b_ref[...].astype(F32)
        for o_ref, s0, w_ in zip(o_refs, starts, ws):
            o_ref[...] = res[:, s0:s0 + w_].astype(o_ref.dtype)

    grid_spec = pltpu.PrefetchScalarGridSpec(
        num_scalar_prefetch=1, grid=(4, r // tr),
        in_specs=[pl.BlockSpec((None, tr, cc), lambda k, i, c_ref: (2 * k + c_ref[0], i, 0)),
                  pl.BlockSpec((None, tr, cc), lambda k, i, c_ref: (k, i, 0))],
        out_specs=[pl.BlockSpec((None, tr, w_), lambda k, i, c_ref: (k, i, 0)) for w_ in ws])
    res = pl.pallas_call(
        body, name=name, grid_spec=grid_spec,
        out_shape=[jax.ShapeDtypeStruct((4, r, w_), BF16) for w_ in ws],
        compiler_params=_params(("parallel", "parallel")),
    )(c_idx, stack, from_sibling)
    return res[0] if widths is None else list(res)


_DIMS = {"nn": ((1,), (0,)), "nt": ((1,), (1,)), "tn": ((0,), (0,))}


def _mm(a, b, mode, *, tm, tn, tk=None, outs, epi=None, tiles=(), rows=(), name, jobs=(), chunks=None, into=None,
        n_total=None, rows_part=None):
    if mode == "tn":
        kk, m = a.shape
    else:
        m, kk = a.shape
    n_all = b.shape[0] if mode == "nt" else b.shape[1]
    nn = n_all if chunks is None else chunks[0].shape[0] * chunks[1]
    n_all = n_all if n_total is None else n_total
    tk = kk if tk is None else tk
    assert m % tm == 0 and nn % tn == 0 and kk % tk == 0, (name, a.shape, b.shape, tm, tn, tk)
    nk = kk // tk
    nt, nr, no = len(tiles), len(rows), len(outs)
    n_into = 0 if into is None else 1

    def body(*refs):
        a_ref, b_ref = refs[:2]
        tile_refs = refs[2:2 + nt]
        row_refs = refs[2 + nt:2 + nt + nr]
        out_refs = refs[2 + nt + nr + n_into:2 + nt + nr + n_into + no]
        part = lax.dot_general(a_ref[...], b_ref[...], (_DIMS[mode], ((), ())), preferred_element_type=F32)

        def finish(acc):
            if epi is None:
                res = (acc,)
            else:
                res = epi(acc, *[t[...] for t in tile_refs], *[r[...] for r in row_refs])
            for o_ref, val in zip(out_refs, res):
                o_ref[...] = val.astype(o_ref.dtype)

        if nk == 1:
            finish(part)
        else:
            acc_ref = refs[-1]
            k = pl.program_id(2)

            @pl.when(k == 0)
            def _():
                acc_ref[...] = part

            @pl.when(k > 0)
            def _():
                acc_ref[...] += part

            @pl.when(k == nk - 1)
            def _():
                finish(acc_ref[...])

    if chunks is None:
        col = b_row = lambda j, *_: j
        prefetch = []
    else:
        assert mode == "nt" and not tiles and not rows and chunks[1] % tn == 0
        per = chunks[1] // tn
        col = lambda j, ids, b_ids: ids[j // per] * per + j % per
        b_row = lambda j, ids, b_ids: b_ids[j // per] * per + j % per
        prefetch = [chunks[0], chunks[2]]
    row0, n_row_blocks = (0, m // tm) if rows_part is None else rows_part
    assert rows_part is None or (mode != "tn" and not tiles)
    if mode == "tn":
        a_spec = pl.BlockSpec((tk, tm), lambda i, j, k, *_: (k, i))
    else:
        a_spec = pl.BlockSpec((tm, tk), lambda i, j, k, *_: (i + row0, k))
    if mode == "nt":
        b_spec = pl.BlockSpec((tn, tk), lambda i, j, k, *s: (b_row(j, *s), k))
    else:
        b_spec = pl.BlockSpec((tk, tn), lambda i, j, k, *_: (k, j))
    tile_specs = [pl.BlockSpec((tm, tn), functools.partial(lambda i, j, k, off: (i, j + off), off=off))
                  for _, off in tiles]
    row_specs = [pl.BlockSpec((r.shape[0], tn), lambda i, j, k: (0, j)) for r in rows]
    out_spec = pl.BlockSpec((tm, tn), lambda i, j, k, *s: (i + row0, col(j, *s)))
    args = [a, b, *[t for t, _ in tiles], *rows]
    in_specs = [a_spec, b_spec] + tile_specs + row_specs
    aliases = None
    if into is not None:
        aliases = {len(args): 0}
        args.append(into)
        in_specs.append(pl.BlockSpec(memory_space=pl.ANY))
    res, jres = _call(
        body, name=name, grid=(n_row_blocks, nn // tn, nk),
        in_specs=in_specs, out_specs=[out_spec] * no,
        out_shape=[jax.ShapeDtypeStruct((m, n_all), dt) for dt in outs],
        scratch_shapes=[pltpu.VMEM((tm, tn), F32)] if nk > 1 else [],
        sem=("parallel", "parallel", "arbitrary"),
        args=args, jobs=jobs, prefetch=prefetch, aliases=aliases)
    return (res, jres) if jobs else res


def _ada_fwd(c_all, w_ada, b_ada_cols, name):
    nb, dd = c_all.shape
    ncol = w_ada.shape[1]
    tn = _tile(ncol, 512, LANE)

    def body(c_ref, w_ref, b_ref, o_ref):
        cv = c_ref[...]
        act = (cv * _sigmoid(cv)).astype(BF16)
        o_ref[...] = jnp.dot(act, w_ref[...].astype(BF16), preferred_element_type=F32) + b_ref[...]

    return pl.pallas_call(
        body, name=name, grid=(ncol // tn,),
        in_specs=[pl.BlockSpec((nb, dd), lambda j: (0, 0)), pl.BlockSpec((dd, tn), lambda j: (0, j)),
                  pl.BlockSpec((1, tn), lambda j: (0, j))],
        out_specs=pl.BlockSpec((nb, tn), lambda j: (0, j)),
        out_shape=jax.ShapeDtypeStruct((nb, ncol), F32),
        compiler_params=_params(("parallel",)),
    )(c_all, w_ada, b_ada_cols)


def _ada_bwd(c_all_t, dmod_cols, name):
    dd, nb = c_all_t.shape
    ncol = dmod_cols.shape[1]
    tr = _tile(dd, 256)

    def body(c_ref, d_ref, o_ref):
        cv = c_ref[...]
        act = (cv * _sigmoid(cv)).astype(BF16).astype(F32)
        dm = d_ref[...].astype(BF16).astype(F32)
        acc = act[:, 0:1] * dm[0:1, :]
        for bi in range(1, nb):
            acc = acc + act[:, bi:bi + 1] * dm[bi:bi + 1, :]
        o_ref[...] = acc

    return pl.pallas_call(
        body, name=name, grid=(dd // tr,),
        in_specs=[pl.BlockSpec((tr, nb), lambda i: (i, 0)), pl.BlockSpec((nb, ncol), lambda i: (0, 0))],
        out_specs=pl.BlockSpec((tr, ncol), lambda i: (i, 0)),
        out_shape=jax.ShapeDtypeStruct((dd, ncol), F32),
        compiler_params=_params(("parallel",)),
    )(c_all_t, dmod_cols)


def _norm_mod(x, gain, mod, row0, name):
    t, dd = x.shape
    tt = _tile(t, 512)

    def body(x_ref, g_ref, m_ref, h_ref):
        xv = x_ref[...]
        r = lax.rsqrt(jnp.mean(xv * xv, axis=-1, keepdims=True) + EPS)
        shift, scale = m_ref[row0:row0 + 1, :], m_ref[row0 + 1:row0 + 2, :]
        h_ref[...] = ((xv * r * g_ref[...]) * (1.0 + scale) + shift).astype(BF16)

    return pl.pallas_call(
        body, name=name, grid=(t // tt,),
        in_specs=[pl.BlockSpec((tt, dd), lambda i: (i, 0)), pl.BlockSpec((1, dd), lambda i: (0, 0)),
                  pl.BlockSpec((8, dd), lambda i: (0, 0))],
        out_specs=pl.BlockSpec((tt, dd), lambda i: (i, 0)),
        out_shape=jax.ShapeDtypeStruct((t, dd), BF16),
        compiler_params=_params(("parallel",)),
    )(x, gain, mod)


def _norm_mod_bwd(x, gain, mod, row0, dh, dres, branch, name, jobs=()):
    t, dd = x.shape
    tt = _tile(t, 256)
    has_branch = branch is not None

    def body(*refs):
        if has_branch:
            x_ref, g_ref, m_ref, dh_ref, dres_ref, o_ref, dx_ref, st_ref, do_ref = refs
        else:
            x_ref, g_ref, m_ref, dh_ref, dres_ref, dx_ref, st_ref = refs

        @pl.when(pl.program_id(0) == 0)
        def _():
            st_ref[...] = jnp.zeros_like(st_ref)

        xv = x_ref[...]
        g = g_ref[...]
        scale = m_ref[row0 + 1:row0 + 2, :]
        r = lax.rsqrt(jnp.mean(xv * xv, axis=-1, keepdims=True) + EPS)
        nv = xv * r
        dhv = dh_ref[...].astype(F32)
        st_ref[0:1, :] += jnp.sum(dhv, axis=0, keepdims=True)
        st_ref[1:2, :] += jnp.sum(dhv * (nv * g), axis=0, keepdims=True)
        dng = dhv * (1.0 + scale)
        st_ref[2:3, :] += jnp.sum(dng * nv, axis=0, keepdims=True)
        dn = dng * g
        dx = dres_ref[...] + r * (dn - nv * jnp.mean(dn * nv, axis=-1, keepdims=True))
        dx_ref[...] = dx
        if has_branch:
            ov = o_ref[...].astype(F32)
            gate = m_ref[branch[1]:branch[1] + 1, :]
            st_ref[3:4, :] += jnp.sum(dx * ov, axis=0, keepdims=True)
            do_ref[...] = (dx * gate).astype(BF16)

    row = pl.BlockSpec((tt, dd), lambda i: (i, 0))
    in_specs = [row, pl.BlockSpec((1, dd), lambda i: (0, 0)), pl.BlockSpec((8, dd), lambda i: (0, 0)), row, row]
    args = [x, gain, mod, dh, dres]
    out_specs = [row, pl.BlockSpec((8, dd), lambda i: (0, 0))]
    out_shape = [jax.ShapeDtypeStruct((t, dd), F32), jax.ShapeDtypeStruct((8, dd), F32)]
    if has_branch:
        in_specs.append(row)
        args.append(branch[0])
        out_specs.append(row)
        out_shape.append(jax.ShapeDtypeStruct((t, dd), BF16))
    res, jres = _call(body, name=name, grid=(t // tt,), in_specs=in_specs, out_specs=out_specs, out_shape=out_shape,
                      sem=("arbitrary",), args=args, jobs=jobs)
    return (res, jres) if jobs else res


def _final_loss(x2, o2, final_g, target, mod, name):
    t, dd = x2.shape
    tt = _tile(t, 256)
    nsteps = t // tt

    def body(x_ref, o_ref, g_ref, t_ref, m_ref, loss_ref, dx_ref, do_ref, st_ref, lacc):
        i = pl.program_id(0)

        @pl.when(i == 0)
        def _():
            st_ref[...] = jnp.zeros_like(st_ref)
            lacc[...] = jnp.zeros_like(lacc)

        xv = x_ref[...]
        g = g_ref[...]
        r = lax.rsqrt(jnp.mean(xv * xv, axis=-1, keepdims=True) + EPS)
        nv = xv * r
        err = nv * g - t_ref[...]
        lacc[...] += jnp.sum(err * err, axis=0, keepdims=True)
        dy = err * (1.0 / dd)
        st_ref[0:1, :] += jnp.sum(dy * nv, axis=0, keepdims=True)
        dn = dy * g
        dx = r * (dn - nv * jnp.mean(dn * nv, axis=-1, keepdims=True))
        dx_ref[...] = dx
        st_ref[1:2, :] += jnp.sum(dx * o_ref[...].astype(F32), axis=0, keepdims=True)
        do_ref[...] = (dx * m_ref[5:6, :]).astype(BF16)

        @pl.when(i == nsteps - 1)
        def _():
            total = jnp.sum(lacc[...], axis=-1, keepdims=True) * (0.5 / dd)
            loss_ref[...] = jnp.broadcast_to(total, loss_ref.shape)

    row = pl.BlockSpec((tt, dd), lambda i: (i, 0))
    return pl.pallas_call(
        body, name=name, grid=(nsteps,),
        in_specs=[row, row, pl.BlockSpec((1, dd), lambda i: (0, 0)), row, pl.BlockSpec((8, dd), lambda i: (0, 0))],
        out_specs=[pl.BlockSpec((8, LANE), lambda i: (0, 0)), row, row, pl.BlockSpec((8, dd), lambda i: (0, 0))],
        out_shape=[jax.ShapeDtypeStruct((8, LANE), F32), jax.ShapeDtypeStruct((t, dd), F32),
                   jax.ShapeDtypeStruct((t, dd), BF16), jax.ShapeDtypeStruct((8, dd), F32)],
        scratch_shapes=[pltpu.VMEM((1, dd), F32)],
        compiler_params=_params(("arbitrary",)),
    )(x2, o2, final_g, target, mod)


def _lru_gates(xt, prev8, lp, wa_ref, wx_ref, first_tile, cw):
    tt = xt.shape[0]
    ext = jnp.concatenate([prev8, xt], axis=0)
    xs = [pltpu.roll(ext, s, 0)[8:, :] if s else xt for s in range(CONV_WIDTH)]
    xc = lp[0:1, :] + lp[7:8, :] * xs[0] + lp[6:7, :] * xs[1] + lp[5:6, :] * xs[2] + lp[4:5, :] * xs[3]
    xcb = xc.astype(BF16)
    za, zi = [], []
    for jb in range(cw // LRU_BLOCK_W):
        sl = slice(jb * LRU_BLOCK_W, (jb + 1) * LRU_BLOCK_W)
        za.append(jnp.dot(xcb[:, sl], wa_ref[jb].astype(BF16), preferred_element_type=F32))
        zi.append(jnp.dot(xcb[:, sl], wx_ref[jb].astype(BF16), preferred_element_type=F32))
    ra = _sigmoid(jnp.concatenate(za, axis=1) + lp[1:2, :])
    ri = _sigmoid(jnp.concatenate(zi, axis=1) + lp[2:3, :])
    sp = _softplus(-lp[3:4, :])
    log_a = -LRU_C * ra * sp
    av = jnp.exp(log_a)
    seq_start = jnp.logical_and(first_tile, lax.broadcasted_iota(jnp.int32, (tt, 1), 0) == 0)
    mult = jnp.where(seq_start, 1.0, jnp.sqrt(_neg_expm1(2.0 * log_a, av * av)))
    return xs, xc, xcb, ra, ri, sp, av, mult, seq_start


def _scan_groups(a, u, reverse):
    tt, cw = a.shape
    a3, u3 = a.reshape(tt // 8, 8, cw), u.reshape(tt // 8, 8, cw)
    r = lax.broadcasted_iota(jnp.int32, (1, 8, 1), 1)
    for s in (1, 2, 4):
        shift = 8 - s if reverse else s
        take = r < 8 - s if reverse else r >= s
        u3 = jnp.where(take, a3 * pltpu.roll(u3, shift, 1) + u3, u3)
        a3 = jnp.where(take, a3 * pltpu.roll(a3, shift, 1), a3)
    return a3, u3


def _scan_carry(a_s, u_s, h_s, carry, reverse):
    ng = a_s.shape[0]

    def step(i, h):
        g = ng - 1 - i if reverse else i
        hg = u_s[g] + a_s[g] * h
        h_s[g] = hg
        return hg[0:1, :] if reverse else hg[7:8, :]

    return lax.fori_loop(0, ng, step, carry, unroll=4)


def _lru_fwd(proj, lp, wa, wx, name, jobs=()):
    t = proj.shape[0]
    w = D_MODEL
    cw = LRU_CW
    tt = _tile(t, 256)
    nb = cw // LRU_BLOCK_W

    def body(lx_ref, lg_ref, lp_ref, wa_ref, wx_ref, rec_ref, ya_ref, a_s, u_s, h_s, halo, carry):
        ti = pl.program_id(1)

        @pl.when(ti == 0)
        def _():
            halo[...] = jnp.zeros_like(halo)
            carry[...] = jnp.zeros_like(carry)

        xt = lx_ref[...].astype(F32)
        lp_v = lp_ref[...]
        _, xc, _, _, ri, _, av, mult, _ = _lru_gates(xt, halo[...], lp_v, wa_ref, wx_ref, ti == 0, cw)
        halo[...] = xt[tt - 8:, :]
        a_s[...], u_s[...] = _scan_groups(av, mult * (ri * xc), False)
        carry[...] = _scan_carry(a_s, u_s, h_s, carry[...], False)
        rec = h_s[...].reshape(tt, cw)
        rec_ref[...] = rec.astype(BF16)
        gelu, _ = _gelu_tanh(lg_ref[...].astype(F32))
        ya_ref[...] = (rec * gelu).astype(BF16)

    off_g = OFF_LG // cw
    res, jres = _call(
        body, name=name, grid=(w // cw, t // tt),
        in_specs=[pl.BlockSpec((tt, cw), lambda ci, ti: (ti, ci)),
                  pl.BlockSpec((tt, cw), lambda ci, ti: (ti, off_g + ci)),
                  pl.BlockSpec((8, cw), lambda ci, ti: (0, ci)),
                  pl.BlockSpec((nb, LRU_BLOCK_W, LRU_BLOCK_W), lambda ci, ti: (ci, 0, 0)),
                  pl.BlockSpec((nb, LRU_BLOCK_W, LRU_BLOCK_W), lambda ci, ti: (ci, 0, 0))],
        out_specs=[pl.BlockSpec((tt, cw), lambda ci, ti: (ti, ci))] * 2,
        out_shape=[jax.ShapeDtypeStruct((t, w), BF16)] * 2,
        scratch_shapes=[pltpu.VMEM((tt // 8, 8, cw), F32)] * 3 + [pltpu.VMEM((8, cw), F32), pltpu.VMEM((1, cw), F32)],
        sem=("parallel", "arbitrary"), args=[proj, proj, lp, wa, wx], jobs=jobs)
    return (res, jres) if jobs else res


def _lru_bwd(proj, rec, dya, lp, wa, wx, name, jobs=()):
    t = proj.shape[0]
    w = D_MODEL
    cw = LRU_CW
    tt = _tile(t, 256)
    nt = t // tt
    nb = cw // LRU_BLOCK_W
    r8 = tt // 8

    def body(lx_ref, lxp_ref, lg_ref, rec_ref, recp_ref, dya_ref, lp_ref, wa_ref, wx_ref,
             dlx_ref, dlg_ref, st_ref, dwa_ref, dwx_ref, a_s, d_s, dh_s, nhalo, carry):
        step_i = pl.program_id(1)
        ti = nt - 1 - step_i

        @pl.when(step_i == 0)
        def _():
            st_ref[...] = jnp.zeros_like(st_ref)
            dwa_ref[...] = jnp.zeros_like(dwa_ref)
            dwx_ref[...] = jnp.zeros_like(dwx_ref)
            nhalo[...] = jnp.zeros_like(nhalo)
            carry[...] = jnp.zeros_like(carry)

        first = ti == 0
        keep = jnp.where(first, 0.0, 1.0)
        xt = lx_ref[...].astype(F32)
        prev8 = lxp_ref[...].astype(F32) * keep
        lp_v = lp_ref[...]
        xs, xc, xcb, ra, ri, sp, av, mult, seq_start = _lru_gates(xt, prev8, lp_v, wa_ref, wx_ref, first, cw)

        lg = lg_ref[...].astype(F32)
        gelu, th = _gelu_tanh(lg)
        dyav = dya_ref[...].astype(F32)
        recv = rec_ref[...].astype(F32)
        dlg_ref[...] = (dyav * recv * _gelu_tanh_grad(lg, th)).astype(BF16)

        drec = dyav * gelu
        e_in = carry[...]
        a_s[...], d_s[...] = _scan_groups(av, av * drec, True)
        carry[...] = _scan_carry(a_s, d_s, dh_s, e_in, True)
        e_next = jnp.concatenate([dh_s[...].reshape(tt, cw), jnp.broadcast_to(e_in, (8, cw))], axis=0)
        dh = drec + pltpu.roll(e_next, tt + 7, 0)[:tt, :]
        rprev8 = recp_ref[...].astype(F32) * keep
        hprev = pltpu.roll(jnp.concatenate([rprev8, recv], axis=0), 1, 0)[8:, :]
        da = dh * hprev
        dmult = jnp.where(seq_start, 0.0, dh * ri * xc)
        dri = dh * mult * xc
        dxc = dh * mult * ri
        dlog_a = da * av - dmult * (av * av) / mult
        dra = dlog_a * (-LRU_C * sp)
        lam = lp_v[3:4, :]
        st_ref[3:4, :] += jnp.sum(dlog_a * ra, axis=0, keepdims=True) * (LRU_C * _sigmoid(-lam))
        dza = dra * ra * (1.0 - ra)
        dzi = dri * ri * (1.0 - ri)
        st_ref[1:2, :] += jnp.sum(dza, axis=0, keepdims=True)
        st_ref[2:3, :] += jnp.sum(dzi, axis=0, keepdims=True)
        dzab, dzib = dza.astype(BF16), dzi.astype(BF16)
        back = []
        for jb in range(nb):
            sl = slice(jb * LRU_BLOCK_W, (jb + 1) * LRU_BLOCK_W)
            dwa_ref[jb] += lax.dot_general(xcb[:, sl], dzab[:, sl], (_DIMS["tn"], ((), ())), preferred_element_type=F32)
            dwx_ref[jb] += lax.dot_general(xcb[:, sl], dzib[:, sl], (_DIMS["tn"], ((), ())), preferred_element_type=F32)
            back.append(
                lax.dot_general(dzab[:, sl], wa_ref[jb].astype(BF16), (_DIMS["nt"], ((), ())), preferred_element_type=F32)
                + lax.dot_general(dzib[:, sl], wx_ref[jb].astype(BF16), (_DIMS["nt"], ((), ())), preferred_element_type=F32))
        dxc = dxc + jnp.concatenate(back, axis=1)
        st_ref[0:1, :] += jnp.sum(dxc, axis=0, keepdims=True)
        for k in range(CONV_WIDTH):
            st_ref[4 + k:5 + k, :] += jnp.sum(dxc * xs[CONV_WIDTH - 1 - k], axis=0, keepdims=True)
        ext = jnp.concatenate([dxc, nhalo[...]], axis=0)
        dlx = lp_v[7:8, :] * dxc
        for s in range(1, CONV_WIDTH):
            dlx = dlx + lp_v[7 - s:8 - s, :] * pltpu.roll(ext, tt + 8 - s, 0)[:tt, :]
        dlx_ref[...] = dlx.astype(BF16)
        nhalo[...] = dxc[0:8, :]

    off_g = OFF_LG // cw
    tile = lambda off: pl.BlockSpec((tt, cw), lambda ci, si: (nt - 1 - si, off + ci))
    prev = lambda off: pl.BlockSpec((8, cw), lambda ci, si: (jnp.maximum((nt - 1 - si) * r8 - 1, 0), off + ci))
    wspec = pl.BlockSpec((nb, LRU_BLOCK_W, LRU_BLOCK_W), lambda ci, si: (ci, 0, 0))
    st_spec = pl.BlockSpec((8, cw), lambda ci, si: (0, ci))
    res, jres = _call(
        body, name=name, grid=(w // cw, nt),
        in_specs=[tile(0), prev(0), tile(off_g), tile(0), prev(0), tile(0), st_spec, wspec, wspec],
        out_specs=[tile(0), tile(0), st_spec, wspec, wspec],
        out_shape=[jax.ShapeDtypeStruct((t, w), BF16), jax.ShapeDtypeStruct((t, w), BF16),
                   jax.ShapeDtypeStruct((8, w), F32),
                   jax.ShapeDtypeStruct((LRU_BLOCKS, LRU_BLOCK_W, LRU_BLOCK_W), F32),
                   jax.ShapeDtypeStruct((LRU_BLOCKS, LRU_BLOCK_W, LRU_BLOCK_W), F32)],
        scratch_shapes=[pltpu.VMEM((tt // 8, 8, cw), F32)] * 3 + [pltpu.VMEM((8, cw), F32), pltpu.VMEM((1, cw), F32)],
        sem=("parallel", "arbitrary"), args=[proj, proj, proj, rec, rec, dya, lp, wa, wx], jobs=jobs)
    return (res, jres) if jobs else res


def _band_valid(first_block):
    qi = lax.broadcasted_iota(jnp.int32, (BLOCK, 2 * BLOCK), 0)
    ki = lax.broadcasted_iota(jnp.int32, (BLOCK, 2 * BLOCK), 1)
    rel = qi + BLOCK - ki
    valid = jnp.logical_and(rel >= 0, rel < BLOCK)
    return jnp.logical_and(valid, jnp.logical_or(ki >= BLOCK, jnp.logical_not(first_block)))


ATTN_STACK_BWD = 8


def _low_half():
    return lax.broadcasted_iota(jnp.int32, (1, LANE), 1) < HEAD_DIM


def _stack_heads(x, h0, ns):
    low = _low_half()
    parts = []
    for g in range(ns):
        h = h0 + g
        slab = x[:, (h // 2) * LANE:(h // 2 + 1) * LANE]
        parts.append(jnp.where(low if h % 2 == 0 else jnp.logical_not(low), slab, jnp.zeros_like(slab)))
    return jnp.concatenate(parts, axis=0)


def _unstack_heads(y, ns):
    low = _low_half()
    pairs = [jnp.where(low, y[(2 * j) * BLOCK:(2 * j + 1) * BLOCK], y[(2 * j + 1) * BLOCK:(2 * j + 2) * BLOCK])
             for j in range(ns // 2)]
    return pairs[0] if len(pairs) == 1 else jnp.concatenate(pairs, axis=1)


def _dup_kv(xall, kv):
    part = xall[:, kv * HEAD_DIM:(kv + 1) * HEAD_DIM]
    return jnp.concatenate([part, part], axis=1)


def _fold_halves(a):
    return a + pltpu.roll(a, HEAD_DIM, 1)


def _group_probs(qg, k2, bias_g, sink_g, lse_g, valid):
    s = lax.dot_general(qg, k2, (_DIMS["nt"], ((), ())), preferred_element_type=F32) * (HEAD_DIM ** -0.5)
    s = jnp.where(valid[None], s.reshape(bias_g.shape) + bias_g, NEG_INF)
    return jnp.exp(s - lse_g), jnp.exp(sink_g - lse_g)


def _group_sinks(s_ref, h0, ns):
    return jnp.concatenate([jnp.full((1, BLOCK, 1), s_ref[h0 + g], F32) for g in range(ns)], axis=0)


def _attn_specs(t):
    nblk = t // BLOCK
    q_spec = pl.BlockSpec((BLOCK, D_MODEL), lambda n: (n, OFF_Q // D_MODEL))
    kc = pl.BlockSpec((BLOCK, KV_WIDTH), lambda n: (n, OFF_K // KV_WIDTH))
    kp = pl.BlockSpec((BLOCK, KV_WIDTH), lambda n: (jnp.maximum(n - 1, 0), OFF_K // KV_WIDTH))
    vc = pl.BlockSpec((BLOCK, KV_WIDTH), lambda n: (n, OFF_V // KV_WIDTH))
    vp = pl.BlockSpec((BLOCK, KV_WIDTH), lambda n: (jnp.maximum(n - 1, 0), OFF_V // KV_WIDTH))
    bias_spec = pl.BlockSpec((N_Q_HEADS, BLOCK, 2 * BLOCK), lambda n: (0, 0, 0))
    sink_spec = pl.BlockSpec(memory_space=pltpu.SMEM)
    return nblk, [q_spec, kp, kc, vp, vc, bias_spec, sink_spec]


def _attn_fwd(proj, bias, sinks, name, jobs=()):
    t = proj.shape[0]
    nblk, in_specs = _attn_specs(t)

    def body(q_ref, kp_ref, kc_ref, vp_ref, vc_ref, b_ref, s_ref, o_ref, lse_ref):
        valid = _band_valid(pl.program_id(0) == 0)
        lane = lax.broadcasted_iota(jnp.int32, (1, LANE), 1)
        q = q_ref[...]
        kall = jnp.concatenate([kp_ref[...], kc_ref[...]], axis=0)
        vall = jnp.concatenate([vp_ref[...], vc_ref[...]], axis=0)
        outs = []
        lse = jnp.zeros((BLOCK, LANE), F32)
        for h in range(N_Q_HEADS):
            kv = h // GROUP
            kk = kall[:, kv * HEAD_DIM:(kv + 1) * HEAD_DIM]
            vv = vall[:, kv * HEAD_DIM:(kv + 1) * HEAD_DIM]
            s = lax.dot_general(q[:, h * HEAD_DIM:(h + 1) * HEAD_DIM], kk, (_DIMS["nt"], ((), ())),
                                preferred_element_type=F32) * (HEAD_DIM ** -0.5)
            s = jnp.where(valid, s + b_ref[h], NEG_INF)
            sink = s_ref[h]
            m = jnp.maximum(jnp.max(s, axis=-1, keepdims=True), sink)
            e = jnp.exp(s - m)
            den = jnp.sum(e, axis=-1, keepdims=True) + jnp.exp(sink - m)
            lse = jnp.where(lane == h, m + jnp.log(den), lse)
            outs.append(jnp.dot((e * (1.0 / den)).astype(BF16), vv, preferred_element_type=F32))
        o_ref[...] = jnp.concatenate(outs, axis=1).astype(BF16)
        lse_ref[...] = lse

    res, jres = _call(
        body, name=name, grid=(nblk,), in_specs=in_specs,
        out_specs=[pl.BlockSpec((BLOCK, D_MODEL), lambda n: (n, 0)), pl.BlockSpec((BLOCK, LANE), lambda n: (n, 0))],
        out_shape=[jax.ShapeDtypeStruct((t, D_MODEL), BF16), jax.ShapeDtypeStruct((t, LANE), F32)],
        sem=("parallel",), args=[proj, proj, proj, proj, proj, bias, sinks], jobs=jobs)
    return (res, jres) if jobs else res


def _attn_bwd(proj, datt, lse, bias, sinks, name, jobs=()):
    t = proj.shape[0]
    nblk, in_specs = _attn_specs(t)
    in_specs = in_specs + [pl.BlockSpec((BLOCK, D_MODEL), lambda n: (n, 0)), pl.BlockSpec((BLOCK, LANE), lambda n: (n, 0))]
    scale = HEAD_DIM ** -0.5

    def body(q_ref, kp_ref, kc_ref, vp_ref, vc_ref, b_ref, s_ref, do_ref, lse_ref,
             dq_ref, dk_ref, dv_ref, db_ref, ds_ref):
        n = pl.program_id(0)

        @pl.when(n == 0)
        def _():
            dk_ref[...] = jnp.zeros_like(dk_ref)
            dv_ref[...] = jnp.zeros_like(dv_ref)
            db_ref[...] = jnp.zeros_like(db_ref)
            ds_ref[...] = jnp.zeros_like(ds_ref)

        valid = _band_valid(n == 0)
        lane = lax.broadcasted_iota(jnp.int32, (1, LANE), 1)
        q = q_ref[...]
        do = do_ref[...]
        lse_all = lse_ref[...]
        kall = jnp.concatenate([kp_ref[...], kc_ref[...]], axis=0)
        vall = jnp.concatenate([vp_ref[...], vc_ref[...]], axis=0)
        low = _low_half()
        dqs, dks, dvs = [], [], []
        dsink_row = jnp.zeros((1, LANE), F32)
        ns = ATTN_STACK_BWD
        for kv in range(N_KV_HEADS):
            k2, v2 = _dup_kv(kall, kv), _dup_kv(vall, kv)
            dk_acc = jnp.zeros((2 * BLOCK, LANE), F32)
            dv_acc = jnp.zeros((2 * BLOCK, LANE), F32)
            for h0 in range(kv * GROUP, (kv + 1) * GROUP, ns):
                qg, dog = _stack_heads(q, h0, ns), _stack_heads(do, h0, ns)
                lse_g = jnp.concatenate([lse_all[:, h0 + g:h0 + g + 1][None] for g in range(ns)], axis=0)
                p, psink = _group_probs(qg, k2, b_ref[h0:h0 + ns], _group_sinks(s_ref, h0, ns), lse_g, valid)
                dp = lax.dot_general(dog, v2, (_DIMS["nt"], ((), ())), preferred_element_type=F32)
                dp = dp.reshape(ns, BLOCK, 2 * BLOCK)
                delta = jnp.sum(p * dp, axis=-1, keepdims=True)
                ds = p * (dp - delta)
                db_ref[h0:h0 + ns] += ds
                dsink = -jnp.sum(psink * delta, axis=1, keepdims=True)
                for g in range(ns):
                    dsink_row = dsink_row + jnp.where(lane == h0 + g, dsink[g], 0.0)
                dsb = ds.reshape(ns * BLOCK, 2 * BLOCK).astype(BF16)
                pb = p.reshape(ns * BLOCK, 2 * BLOCK).astype(BF16)
                dqs.append(_unstack_heads(jnp.dot(dsb, k2, preferred_element_type=F32) * scale, ns))
                dk_acc = dk_acc + lax.dot_general(dsb, qg, (_DIMS["tn"], ((), ())), preferred_element_type=F32)
                dv_acc = dv_acc + lax.dot_general(pb, dog, (_DIMS["tn"], ((), ())), preferred_element_type=F32)
            dks.append(_fold_halves(dk_acc) * scale)
            dvs.append(_fold_halves(dv_acc))
        dq_ref[...] = jnp.concatenate(dqs, axis=1).astype(BF16)
        rows = pl.ds(pl.multiple_of(n * BLOCK, BLOCK), 2 * BLOCK)
        dk_ref[rows, :] += jnp.concatenate([jnp.where(low, dks[0], dks[1]), jnp.where(low, dks[2], dks[3])], axis=1)
        dv_ref[rows, :] += jnp.concatenate([jnp.where(low, dvs[0], dvs[1]), jnp.where(low, dvs[2], dvs[3])], axis=1)
        ds_ref[0:1, :] += dsink_row

    res, jres = _call(
        body, name=name, grid=(nblk,), in_specs=in_specs,
        out_specs=[pl.BlockSpec((BLOCK, D_MODEL), lambda n: (n, 0)),
                   pl.BlockSpec((t + BLOCK, KV_WIDTH), lambda n: (0, 0)),
                   pl.BlockSpec((t + BLOCK, KV_WIDTH), lambda n: (0, 0)),
                   pl.BlockSpec((N_Q_HEADS, BLOCK, 2 * BLOCK), lambda n: (0, 0, 0)),
                   pl.BlockSpec((8, LANE), lambda n: (0, 0))],
        out_shape=[jax.ShapeDtypeStruct((t, D_MODEL), BF16),
                   jax.ShapeDtypeStruct((t + BLOCK, KV_WIDTH), F32),
                   jax.ShapeDtypeStruct((t + BLOCK, KV_WIDTH), F32),
                   jax.ShapeDtypeStruct((N_Q_HEADS, BLOCK, 2 * BLOCK), F32),
                   jax.ShapeDtypeStruct((8, LANE), F32)],
        sem=("arbitrary",), args=[proj, proj, proj, proj, proj, bias, sinks, datt, lse], jobs=jobs)
    return (res, jres) if jobs else res


def _bucket_table():
    qi = np.arange(BLOCK)[:, None]
    ki = np.arange(2 * BLOCK)[None, :]
    rel = np.maximum(qi + BLOCK - ki, 0)
    max_exact = N_BUCKETS // 2
    relf = np.maximum(rel, 1).astype(np.float32)
    large = max_exact + (np.log(relf / max_exact) / math.log(MAX_DISTANCE / max_exact)
                         * (N_BUCKETS - max_exact)).astype(np.int32)
    large = np.minimum(large, N_BUCKETS - 1)
    return np.where(rel < max_exact, rel, large).astype(np.int32)


def _rel_bias_fwd(rel_bias, onehot, name):
    n = onehot.shape[1]
    tn = _tile(n, 8192, LANE)

    def body(r_ref, o_ref, out_ref):
        out_ref[...] = lax.dot_general(r_ref[...], o_ref[...], (_DIMS["tn"], ((), ())),
                                       precision=lax.Precision.HIGHEST, preferred_element_type=F32)

    return pl.pallas_call(
        body, name=name, grid=(n // tn,),
        in_specs=[pl.BlockSpec(rel_bias.shape, lambda j: (0, 0)), pl.BlockSpec((N_BUCKETS, tn), lambda j: (0, j))],
        out_specs=pl.BlockSpec((N_Q_HEADS, tn), lambda j: (0, j)),
        out_shape=jax.ShapeDtypeStruct((N_Q_HEADS, n), F32),
        compiler_params=_params(("parallel",)),
    )(rel_bias, onehot)


def _rel_bias_bwd(dbias, onehot, name):
    def body(d_ref, o_ref, out_ref):
        out_ref[...] = lax.dot_general(d_ref[...], o_ref[...], (_DIMS["nt"], ((), ())),
                                       precision=lax.Precision.HIGHEST, preferred_element_type=F32)

    full = pl.BlockSpec(dbias.shape, lambda: (0, 0))
    return pl.pallas_call(
        body, name=name, in_specs=[full, full],
        out_specs=pl.BlockSpec((N_Q_HEADS, N_BUCKETS), lambda: (0, 0)),
        out_shape=jax.ShapeDtypeStruct((N_Q_HEADS, N_BUCKETS), F32),
        compiler_params=_params(()),
    )(dbias, onehot)


def _part_specs(gparts, part_index, tr, cc):
    blk = pl.BlockSpec((tr, cc), lambda i, *_: (i, 0))
    specs = []
    for i, _ in enumerate(gparts):
        if i in part_index:
            ix = part_index[i]
            if isinstance(ix, int):
                specs.append(pl.BlockSpec((None, tr, cc), functools.partial(lambda i, *_, ix: (ix, i, 0), ix=ix)))
            else:
                specs.append(pl.BlockSpec((None, tr, cc), lambda i, ix_ref: (ix_ref[0], i, 0)))
        else:
            specs.append(blk)
    return blk, specs


def _sum_parts(gparts, part_index, name):
    r, cc = gparts[0].shape[-2:]
    tr = _tile(r, 256)
    dyn = [i for i, ix in part_index.items() if not isinstance(ix, int)]
    assert len(dyn) <= 1

    def body(*refs):
        refs = refs[len(dyn):]
        g = refs[0][...].astype(F32)
        for gr in refs[1:-1]:
            g = g + gr[...].astype(F32)
        refs[-1][...] = g

    blk, specs = _part_specs(gparts, part_index, tr, cc)
    grid_spec = pltpu.PrefetchScalarGridSpec(num_scalar_prefetch=len(dyn), grid=(r // tr,), in_specs=specs, out_specs=blk)
    return pl.pallas_call(
        body, name=name, grid_spec=grid_spec, out_shape=jax.ShapeDtypeStruct((r, cc), F32),
        compiler_params=_params(("parallel",)),
    )(*[part_index[i] for i in dyn], *gparts)


def _adamw(w, m, v, gparts, name, part_index=None):
    r, cc = w.shape
    tr = _tile(r, 128)
    np_ = len(gparts)
    part_index = part_index or {}
    dyn = [i for i, ix in part_index.items() if not isinstance(ix, int)]
    assert len(dyn) <= 1
    bc1 = 1.0 / (1.0 - ADAM_B1 ** ADAM_STEP)
    bc2 = 1.0 / (1.0 - ADAM_B2 ** ADAM_STEP)

    def body(*refs):
        refs = refs[len(dyn):]
        w_ref, m_ref, v_ref = refs[:3]
        g_refs = refs[3:3 + np_]
        g_out, d_out, m_out, v_out = refs[3 + np_:]
        g = g_refs[0][...].astype(F32)
        for gr in g_refs[1:]:
            g = g + gr[...].astype(F32)
        mn = ADAM_B1 * m_ref[...] + (1.0 - ADAM_B1) * g
        vn = ADAM_B2 * v_ref[...] + (1.0 - ADAM_B2) * (g * g)
        g_out[...] = g
        m_out[...] = mn
        v_out[...] = vn
        d_out[...] = -ADAM_LR * ((mn * bc1) / (jnp.sqrt(vn * bc2) + ADAM_EPS) + ADAM_WD * w_ref[...])

    blk, g_specs = _part_specs(gparts, part_index, tr, cc)
    grid_spec = pltpu.PrefetchScalarGridSpec(
        num_scalar_prefetch=len(dyn), grid=(r // tr,),
        in_specs=[blk, blk, blk] + g_specs, out_specs=[blk] * 4)
    return pl.pallas_call(
        body, name=name, grid_spec=grid_spec,
        out_shape=[jax.ShapeDtypeStruct((r, cc), F32)] * 4,
        compiler_params=_params(("parallel",)),
    )(*[part_index[i] for i in dyn], w, m, v, *gparts)


def _row(vec):
    return vec.reshape(1, -1)


def kernel(x, c, w_ada, b_ada, norm1_g, w_in, conv_w, conv_b, lru_wa, lru_ba, lru_wx, lru_bx, lru_lambda, w_lru_out, w_attn_out, attn_sinks, rel_bias, w_out, norm2_g, w_ff1, w_ff2, final_g, loss_target, m_w_ada, m_b_ada, m_norm1_g, m_w_in, m_conv_w, m_conv_b, m_lru_wa, m_lru_ba, m_lru_wx, m_lru_bx, m_lru_lambda, m_w_lru_out, m_w_attn_out, m_attn_sinks, m_rel_bias, m_w_out, m_norm2_g, m_w_ff1, m_w_ff2, m_final_g, v_w_ada, v_b_ada, v_norm1_g, v_w_in, v_conv_w, v_conv_b, v_lru_wa, v_lru_ba, v_lru_wx, v_lru_bx, v_lru_lambda, v_w_lru_out, v_w_attn_out, v_attn_sinks, v_rel_bias, v_w_out, v_norm2_g, v_w_ff1, v_w_ff2, v_final_g):
    dd = D_MODEL
    t = x.shape[1]
    ix, iy, ic = lax.axis_index("x"), lax.axis_index("y"), lax.axis_index("c")
    me = 4 * ix + 2 * iy + ic
    chip = 2 * ix + iy
    c_idx = jnp.reshape(ic, (1,)).astype(jnp.int32)
    chip_idx = jnp.reshape(chip, (1,)).astype(jnp.int32)

    xs = x[0]
    target = loss_target[0]
    ncol_ada = w_ada.shape[2]
    cshard = conv_w.shape[2]

    shards = [jnp.transpose(w_in[0]).astype(BF16), w_lru_out[0].astype(BF16), w_attn_out[0].astype(BF16),
              w_out[0].astype(BF16), jnp.transpose(w_ff1[0]).astype(BF16), w_ff2[0].astype(BF16)]
    s_in, s_lo, s_ao, s_o, s_f1, s_f2 = shards

    def full(g):
        return g.reshape(N_DEV * g.shape[1], dd)

    def stack(g):
        return g.reshape(N_DEV, g.shape[0] // N_DEV, dd)

    (w_in_pair,) = _run_job(_pair_job([s_in]), "ag_w_in_pair")
    w_in_pair = lax.dynamic_update_slice_in_dim(w_in_pair, s_in[None], ic, axis=0)

    pack0 = jnp.zeros((8, dd), F32).at[0:1, :].set(c).at[1:1 + CONV_WIDTH, 0:cshard].set(conv_w[0])
    (g0,) = _all_gather([pack0], "ag_cond")
    c_all = g0[:, 0, :]
    conv_w_full = jnp.transpose(g0[:, 1:1 + CONV_WIDTH, 0:cshard], (1, 0, 2)).reshape(CONV_WIDTH, dd)
    b_cols = lax.dynamic_slice(b_ada, (0, me * ncol_ada), (1, ncol_ada))
    mod_cols = _ada_fwd(c_all, w_ada[0], b_cols, "ada_fwd")
    (g1,) = _all_gather([mod_cols], "ag_mod")
    mod_mine = lax.dynamic_index_in_dim(g1, me, axis=1, keepdims=False).reshape(6, dd)
    mod = jnp.concatenate([mod_mine, jnp.zeros((2, dd), F32)], axis=0)

    bucket = _bucket_table()
    onehot = jnp.asarray((bucket.reshape(1, -1) == np.arange(N_BUCKETS)[:, None]).astype(np.float32))
    bias = _rel_bias_fwd(rel_bias, onehot, "rel_bias_fwd").reshape(N_Q_HEADS, BLOCK, 2 * BLOCK)
    sinks = attn_sinks.reshape(N_Q_HEADS)
    lp = jnp.concatenate([conv_b, lru_ba, lru_bx, lru_lambda, conv_w_full], axis=0)
    wa, wx = lru_wa[0], lru_wx[0]

    tm = _tile(t, 1024)
    tn = 512
    h = _norm_mod(xs, norm1_g, mod, 0, "norm1")
    tn2 = 1024
    chunk_w = 2 * s_in.shape[0]

    def ids(*ks):
        return jnp.stack([jnp.asarray(k, jnp.int32) for k in ks])

    tn_in = _tile(chunk_w, 1024, LANE)
    (proj,), ((g_near,),) = _mm(
        h, w_in_pair.reshape(chunk_w, dd), "nt", tm=tm, tn=tn_in, outs=[BF16], name="in_proj_own",
        chunks=(ids(chip), chunk_w, ids(0)), n_total=IN_WIDTH, jobs=[_gather_job([w_in_pair], (0.7,), "near")])
    near_ids = ids(2 * (1 - ix) + iy, 2 * ix + 1 - iy)
    (proj,), ((g_far,),) = _mm(
        h, full(g_near), "nt", tm=tm, tn=tn_in, outs=[BF16], name="in_proj_near",
        chunks=(near_ids, chunk_w, near_ids), into=proj, jobs=[_gather_job([g_near], (0.6,), "far")])
    far_chip = 2 * (1 - ix) + 1 - iy
    (proj,), ((g_o,),) = _mm(
        h, full(g_far), "nt", tm=tm, tn=tn_in, outs=[BF16], name="in_proj_far",
        chunks=(ids(far_chip), chunk_w, ids(far_chip)), into=proj, jobs=[_gather_job([s_o], (0.5, 0.8))])
    far_slots = lax.dynamic_slice_in_dim(g_far, 2 * far_chip, 2, axis=0)
    win_t = lax.dynamic_update_slice_in_dim(g_near, far_slots, 2 * far_chip, axis=0)
    win_t = full(lax.dynamic_update_slice_in_dim(win_t, w_in_pair, 2 * chip, axis=0))
    (g_o,) = _fill_own([g_o], [s_o])
    (rec, ya), ((g_f1,),) = _lru_fwd(proj, lp, wa, wx, "lru_fwd", jobs=[_gather_job([s_f1], (0.6, 0.9))])
    (att, lse), ((g_lo,), (g_ao,)) = _attn_fwd(proj, bias, sinks, "attn_fwd",
                                        jobs=[_gather_job([s_lo], (0.3, 0.45)), _gather_job([s_ao], (0.75, 0.9))])
    g_lo, g_ao, g_f1 = _fill_own([g_lo, g_ao, g_f1], [s_lo, s_ao, s_f1])
    wlo, wao, wo, wf1_t = full(g_lo), full(g_ao), full(g_o), full(g_f1)
    (y_a,) = _mm(ya, wlo, "nn", tm=tm, tn=tn2, outs=[BF16], name="lru_out")

    def merge_epi(acc, ya_t, ga_t, gb_t):
        return acc, _sigmoid(ga_t.astype(F32)) * ya_t.astype(F32) + _sigmoid(gb_t.astype(F32)) * acc

    y_b, merged = _mm(att, wao, "nn", tm=tm, tn=tn, outs=[BF16, BF16], epi=merge_epi,
                      tiles=[(y_a, 0), (proj, OFF_GA // tn), (proj, OFF_GB // tn)], name="attn_out_merge")

    def resid_epi(row):
        def epi(acc, x_t, mod_t):
            return acc, x_t + mod_t[row:row + 1, :] * acc
        return epi

    o1, x1 = _mm(merged, wo, "nn", tm=tm, tn=tn2, outs=[BF16, F32], epi=resid_epi(2),
                 tiles=[(xs, 0)], rows=[mod], name="out_proj")
    h2 = _norm_mod(x1, norm2_g, mod, 3, "norm2")

    def relu2_epi(acc):
        rl = jnp.maximum(acc, 0.0)
        return acc, rl * rl

    (f_pre, ff), ((g_f2,),) = _mm(h2, wf1_t, "nt", tm=tm, tn=1024, outs=[BF16, BF16], epi=relu2_epi, name="ff1",
                                  jobs=[_gather_job([s_f2], (0.6, 0.9))])
    wf2 = full(_fill_own([g_f2], [s_f2])[0])
    tmh = _tile(t, 512)
    o2, x2 = _mm(ff, wf2, "nn", tm=tmh, tn=tn, outs=[BF16, F32], epi=resid_epi(5),
                 tiles=[(x1, 0)], rows=[mod], name="ff2")

    loss_blk, dx2, do2, st_f = _final_loss(x2, o2, _row(final_g), target, mod, "final_loss")
    loss = lax.psum(loss_blk[0, 0], ("x", "y", "c"))

    def drelu2_epi(acc, f_t):
        return (acc * (2.0 * jnp.maximum(f_t.astype(F32), 0.0)),)

    tmw, tnw, tkw = 512, dd, _tile(t, 2048)
    (df,) = _mm(do2, wf2, "nt", tm=tm, tn=1024, outs=[BF16], epi=drelu2_epi, tiles=[(f_pre, 0)], name="ff2_dx")
    (dwf2,) = _mm(ff, do2, "tn", tm=tmw, tn=tnw, tk=tkw, outs=[BF16], name="ff2_dw")
    (dh2,), ((sib_f2,),) = _mm(df, wf1_t, "nn", tm=tmh, tn=tn, outs=[BF16], name="ff1_dx",
                              jobs=[_sibling_job([stack(dwf2)])])
    pair_f2 = _pair_add(stack(dwf2), sib_f2, c_idx, "rs_pair_add_w_ff2")
    (dwf1_t,), ((chips_f2,),) = _mm(df, h2, "tn", tm=tmw, tn=tnw, tk=tkw, outs=[BF16], name="ff1_dw",
                                    jobs=[_chips_job([pair_f2])])
    (dx1, st_2, do1), ((sib_f1,),) = _norm_mod_bwd(x1, norm2_g, mod, 3, dh2, dx2, (o1, 2), "norm2_bwd",
                                                   jobs=[_sibling_job([stack(dwf1_t)])])
    pair_f1 = _pair_add(stack(dwf1_t), sib_f1, c_idx, "rs_pair_add_w_ff1")

    def dmerge_epi(acc, ga_t, gb_t, ya_t, yb_t):
        sa, sb = _sigmoid(ga_t.astype(F32)), _sigmoid(gb_t.astype(F32))
        return (acc * sa, acc * sb, acc * ya_t.astype(F32) * sa * (1.0 - sa), acc * yb_t.astype(F32) * sb * (1.0 - sb))

    tmh = _tile(t, 512)
    dy_a, dy_b, dga, dgb = _mm(do1, wo, "nt", tm=tmh, tn=tn, outs=[BF16] * 4, epi=dmerge_epi,
                               tiles=[(proj, OFF_GA // tn), (proj, OFF_GB // tn), (y_a, 0), (y_b, 0)], name="out_proj_dx")
    (dwo,) = _mm(merged, do1, "tn", tm=tmw, tn=tnw, tk=tkw, outs=[BF16], name="out_proj_dw")
    (datt,) = _mm(dy_b, wao, "nt", tm=tm, tn=tn2, outs=[BF16], name="attn_out_dx")
    (dwao,) = _mm(att, dy_b, "tn", tm=tmw, tn=tnw, tk=tkw, outs=[BF16], name="attn_out_dw")
    (dya,) = _mm(dy_a, wlo, "nt", tm=tm, tn=tn2, outs=[BF16], name="lru_out_dx")
    (dwlo,) = _mm(ya, dy_a, "tn", tm=tmw, tn=tnw, tk=tkw, outs=[BF16], name="lru_out_dw")
    (dq, dk_pad, dv_pad, dbias, dsinks), (sib_3, (chips_f1,)) = _attn_bwd(
        proj, datt, lse, bias, sinks, "attn_bwd",
        jobs=[_sibling_job([stack(dwlo), stack(dwao), stack(dwo)]), _chips_job([pair_f1])])
    pair_lo, pair_ao, pair_o = [_pair_add(stack(g_), s_, c_idx, "rs_pair_add_" + nm_)
                                for g_, s_, nm_ in zip((dwlo, dwao, dwo), sib_3, ("w_lru_out", "w_attn_out", "w_out"))]
    (dlx, dlg, st_l, dwa, dwx), ((chips_lo, chips_ao, chips_o),) = _lru_bwd(
        proj, rec, dya, lp, wa, wx, "lru_bwd", jobs=[_chips_job([pair_lo, pair_ao, pair_o])])
    dproj = jnp.concatenate([dlx, dlg, dq, dk_pad[BLOCK:].astype(BF16), dv_pad[BLOCK:].astype(BF16), dga, dgb], axis=1)
    lru_mats = jnp.concatenate([dwa.reshape(-1, dd), dwx.reshape(-1, dd)], axis=0)
    (dwin_t,), ((mats_all,),) = _mm(dproj, h, "tn", tm=768, tn=tnw, tk=tkw, outs=[BF16], name="in_proj_dw",
                                   jobs=[_gather_job([lru_mats], (0.3, 0.6))])
    (mats_all,) = _fill_own([mats_all], [lru_mats])
    (sib_in,) = _exchange_sibling([stack(dwin_t)], "rs_sibling_w_in")
    pair_in = _pair_add(stack(dwin_t), sib_in, c_idx, "rs_pair_add_w_in", widths=(dd // 2, dd // 2))
    n_rb = t // tm
    dx_tiles = dict(tm=tm, tn=tn2, tk=IN_WIDTH // 4, outs=[BF16])
    if n_rb >= 2:
        (dh,), ((chips_in_0,),) = _mm(dproj, win_t, "nn", name="in_proj_dx_top", rows_part=(0, n_rb // 2),
                                     jobs=[_chips_job(pair_in[0:1])], **dx_tiles)
        (dh,), ((chips_in_1,),) = _mm(dproj, win_t, "nn", name="in_proj_dx_bottom", into=dh,
                                     rows_part=(n_rb // 2, n_rb - n_rb // 2), jobs=[_chips_job(pair_in[1:2])], **dx_tiles)
    else:
        (dh,), ((chips_in_0,), (chips_in_1,)) = _mm(
            dproj, win_t, "nn", name="in_proj_dx", jobs=[_chips_job(pair_in[0:1]), _chips_job(pair_in[1:2])], **dx_tiles)
    grad_x, st_1 = _norm_mod_bwd(xs, norm1_g, mod, 0, dh, dx1, None, "norm1_bwd")
    chips_in = [chips_in_0, chips_in_1]

    drel = jnp.transpose(_rel_bias_bwd(dbias.reshape(N_Q_HEADS, -1), onehot, "rel_bias_bwd"))

    misc = jnp.concatenate([drel.reshape(1, -1), dsinks[0:1, 0:N_Q_HEADS],
                            jnp.zeros((1, dd - N_BUCKETS * N_Q_HEADS - N_Q_HEADS), F32)], axis=1)
    zero_row = jnp.zeros((1, dd), F32)
    small = jnp.concatenate([
        st_1[0:2], st_2[3:4], st_2[0:2], st_f[1:2], zero_row, zero_row,
        st_1[2:3], st_l[0:1], st_l[1:2], st_l[2:3], st_l[3:4], st_2[2:3], st_f[0:1], misc,
        st_l[4:8], jnp.zeros((4, dd), F32)], axis=0)
    (small_all,) = _all_gather([small], "ag_small")

    def pack_small(b_, n1, cb, ba, bx, lam, n2, fg, rb, sk):
        misc_ = jnp.concatenate([rb.reshape(1, -1), sk.reshape(1, -1),
                                 jnp.zeros((1, dd - N_BUCKETS * N_Q_HEADS - N_Q_HEADS), F32)], axis=1)
        return jnp.concatenate([b_.reshape(6, dd), jnp.zeros((2, dd), F32), n1, cb, ba, bx, lam, n2, _row(fg), misc_,
                                jnp.zeros((8, dd), F32)], axis=0)

    def pack_mats(wa_, wx_):
        return jnp.concatenate([wa_.reshape(-1, dd), wx_.reshape(-1, dd)], axis=0)

    every = {i: i for i in range(N_DEV)}
    w_s = pack_small(b_ada, norm1_g, conv_b, lru_ba, lru_bx, lru_lambda, norm2_g, final_g, rel_bias, attn_sinks)
    m_s = pack_small(m_b_ada, m_norm1_g, m_conv_b, m_lru_ba, m_lru_bx, m_lru_lambda, m_norm2_g, m_final_g, m_rel_bias, m_attn_sinks)
    v_s = pack_small(v_b_ada, v_norm1_g, v_conv_b, v_lru_ba, v_lru_bx, v_lru_lambda, v_norm2_g, v_final_g, v_rel_bias, v_attn_sinks)
    small_res = _adamw(w_s, m_s, v_s, [small_all] * N_DEV, "adamw_small", part_index=every)
    mats_res = _adamw(pack_mats(lru_wa, lru_wx), pack_mats(m_lru_wa, m_lru_wx), pack_mats(v_lru_wa, v_lru_wx),
                      [mats_all] * N_DEV, "adamw_lru_mats", part_index=every)

    def unpack_small(s, mt):
        nb_ = N_BUCKETS * N_Q_HEADS
        half = mt.shape[0] // 2
        return dict(
            b_ada=s[0:6].reshape(1, 6 * dd), norm1_g=s[8:9], conv_b=s[9:10], lru_ba=s[10:11], lru_bx=s[11:12],
            lru_lambda=s[12:13], norm2_g=s[13:14], final_g=s[14], rel_bias=s[15, 0:nb_].reshape(N_BUCKETS, N_Q_HEADS),
            attn_sinks=s[15:16, nb_:nb_ + N_Q_HEADS],
            lru_wa=mt[:half].reshape(1, LRU_BLOCKS, LRU_BLOCK_W, LRU_BLOCK_W),
            lru_wx=mt[half:].reshape(1, LRU_BLOCKS, LRU_BLOCK_W, LRU_BLOCK_W))

    res = {k: [None] * 4 for k in ("w_ada", "w_in", "conv_w", "w_lru_out", "w_attn_out", "w_out", "w_ff1", "w_ff2")}
    for q_, (s_, mt_) in enumerate(zip(small_res, mats_res)):
        for k_, val in unpack_small(s_, mt_).items():
            res.setdefault(k_, [None] * 4)[q_] = val

    g_conv = lax.dynamic_slice(small_res[0][16:16 + CONV_WIDTH], (0, me * cshard), (CONV_WIDTH, cshard))
    conv_res = _adamw(conv_w[0], m_conv_w[0], v_conv_w[0], [g_conv], "adamw_conv")
    res["conv_w"] = [r_[None] for r_ in conv_res]

    dmod_all = small_all[:, 0:6, :].reshape(N_DEV, 6 * dd)
    dmod_cols = lax.dynamic_slice(dmod_all, (0, me * ncol_ada), (N_DEV, ncol_ada))
    g_ada = _ada_bwd(jnp.transpose(c_all), dmod_cols, "ada_bwd")
    ada_res = _adamw(w_ada[0], m_w_ada[0], v_w_ada[0], [g_ada], "adamw_ada")
    res["w_ada"] = [r_[None] for r_ in ada_res]

    pair = [None, pair_lo, pair_ao, pair_o, pair_f1, pair_f2]
    from_chips = [None, chips_lo, chips_ao, chips_o, chips_f1, chips_f2]
    own_and_chips = {0: chip_idx, 1: 0, 2: 1, 3: 2}

    def summed(i):
        return [pair[i], from_chips[i], from_chips[i], from_chips[i]], {0: chip_idx, 1: 0, 2: 1, 3: 2}

    def sum_only(i, name):
        parts, index = summed(i)
        return _sum_parts(parts, index, name)

    g_in = jnp.transpose(jnp.concatenate(
        [_sum_parts([p_, c_, c_, c_], own_and_chips, "sum_w_in_%d" % q_)
         for q_, (p_, c_) in enumerate(zip(pair_in, chips_in))], axis=1))
    res["w_in"] = [r_[None] for r_ in _adamw(w_in[0], m_w_in[0], v_w_in[0], [g_in], "adamw_w_in")]
    g_ff1 = jnp.transpose(sum_only(4, "sum_w_ff1"))
    res["w_ff1"] = [r_[None] for r_ in _adamw(w_ff1[0], m_w_ff1[0], v_w_ff1[0], [g_ff1], "adamw_w_ff1")]
    for i, (nm, w_, m_, v_) in {1: ("w_lru_out", w_lru_out, m_w_lru_out, v_w_lru_out),
                                2: ("w_attn_out", w_attn_out, m_w_attn_out, v_w_attn_out),
                                3: ("w_out", w_out, m_w_out, v_w_out),
                                5: ("w_ff2", w_ff2, m_w_ff2, v_w_ff2)}.items():
        parts, index = summed(i)
        res[nm] = [r_[None] for r_ in _adamw(w_[0], m_[0], v_[0], parts, "adamw_" + nm, part_index=index)]

    order = ["w_ada", "b_ada", "norm1_g", "w_in", "conv_w", "conv_b", "lru_wa", "lru_ba", "lru_wx", "lru_bx",
             "lru_lambda", "w_lru_out", "w_attn_out", "attn_sinks", "rel_bias", "w_out", "norm2_g", "w_ff1",
             "w_ff2", "final_g"]
    out = [loss, grad_x[None]]
    for q_ in range(4):
        out += [res[k_][q_] for k_ in order]
    return tuple(out)
```

```python
import functools
import math

import numpy as np
import jax
import jax.numpy as jnp
from jax import lax
from jax.experimental import pallas as pl
from jax.experimental.pallas import tpu as pltpu

F32 = jnp.float32
BF16 = jnp.bfloat16
MESH = pl.DeviceIdType.MESH

D_MODEL = 2048
N_Q_HEADS = 32
N_KV_HEADS = 4
GROUP = N_Q_HEADS // N_KV_HEADS
HEAD_DIM = 64
KV_WIDTH = N_KV_HEADS * HEAD_DIM
BLOCK = 128
NEG_INF = -1e30
N_BUCKETS = 32
MAX_DISTANCE = 128
LRU_BLOCKS = 16
LRU_BLOCK_W = 128
CONV_WIDTH = 4
LRU_C = 8.0
D_FF = 4 * D_MODEL
EPS = 1e-6
IN_WIDTH = 5 * D_MODEL + 2 * KV_WIDTH
OFF_LX, OFF_LG, OFF_Q, OFF_K, OFF_V, OFF_GA, OFF_GB = 0, 2048, 4096, 6144, 6400, 6656, 8704

ADAM_LR, ADAM_B1, ADAM_B2, ADAM_EPS, ADAM_WD, ADAM_STEP = 0.001, 0.9, 0.999, 1e-08, 0.01, 10

N_DEV = 8
VMEM_LIMIT_BYTES = 48 * 1024 * 1024
LANE = 128
LRU_CW = 512


def _params(sem, **kw):
    return pltpu.CompilerParams(dimension_semantics=sem, vmem_limit_bytes=VMEM_LIMIT_BYTES, **kw)


def _tile(n, pref, mult=8):
    if n <= pref:
        return n
    t = (pref // mult) * mult
    while t >= mult:
        if n % t == 0:
            return t
        t -= mult
    return n


def _sigmoid(x):
    return 0.5 * jnp.tanh(0.5 * x) + 0.5


def _gelu_tanh(x):
    k = math.sqrt(2.0 / math.pi)
    t = jnp.tanh(k * (x + 0.044715 * x * x * x))
    return 0.5 * x * (1.0 + t), t


def _gelu_tanh_grad(x, t):
    k = math.sqrt(2.0 / math.pi)
    return 0.5 * (1.0 + t) + 0.5 * x * (1.0 - t * t) * k * (1.0 + 3.0 * 0.044715 * x * x)


def _log1p(e):
    u = 1.0 + e
    return jnp.where(u == 1.0, e, jnp.log(u) * (e / jnp.where(u == 1.0, 1.0, u - 1.0)))


def _softplus(x):
    return jnp.maximum(x, 0.0) + _log1p(jnp.exp(-jnp.abs(x)))


def _neg_expm1(x, exp_x):
    series = -x * (1.0 + x * (0.5 + x * (1.0 / 6.0 + x * (1.0 / 24.0))))
    return jnp.where(x > -0.0625, series, 1.0 - exp_x)


def _my_place():
    return lax.axis_index("x"), lax.axis_index("y"), lax.axis_index("c")


def _all_gather(arrs, name):
    return _fill_own(_run_job(_gather_job(arrs), name), arrs)


def _fill_own(stacks, shards):
    me = 4 * lax.axis_index("x") + 2 * lax.axis_index("y") + lax.axis_index("c")
    out = []
    for g, s in zip(stacks, shards):
        if g.size * g.dtype.itemsize <= (1 << 20):
            slot = lax.broadcasted_iota(jnp.int32, g.shape, 0)
            out.append(jnp.where(slot == me, s[None], g))
        else:
            out.append(lax.dynamic_update_slice_in_dim(g, s[None], me, axis=0))
    return out


class _Job:
    def __init__(self, ins, outs, sems, start, finish, forwards=(), forward_at=()):
        self.ins, self.outs, self.sems = list(ins), list(outs), list(sems)
        self.start, self.finish, self.forwards, self.forward_at = start, finish, list(forwards), list(forward_at)
        assert len(self.forwards) <= len(self.forward_at)


def _gather_job(arrs, forward_at=(0.6, 0.85), part="all"):
    n = len(arrs)
    whole, near, far = part == "all", part == "near", part == "far"

    def copies(ins, outs, sems):
        send_sems, recv_sems, local_sems = sems
        x, y, c = _my_place()
        sib = (x, y, 1 - c)
        xn, yn, dg = (1 - x, y), (x, 1 - y), (1 - x, 1 - y)

        def slot(px, py, pc):
            return 4 * px + 2 * py + pc

        def cp(a, k, src, dst_slot, to):
            return pltpu.make_async_remote_copy(
                src_ref=src, dst_ref=outs[a].at[dst_slot], send_sem=send_sems.at[a, k], recv_sem=recv_sems.at[a, k],
                device_id=to, device_id_type=MESH)

        def arrival(a, k, dst_slot):
            return cp(a, k, outs[a].at[dst_slot], dst_slot, sib)

        me_slot = slot(x, y, c)
        sends, local, relay, passes, arrive = {}, [], {}, {}, {}
        for a in range(n):
            if whole:
                own = ins[a]
                sends[a, 0] = cp(a, 0, own, me_slot, sib)
            if near:
                own = ins[a].at[c]
            if whole or near:
                sends[a, 1] = cp(a, 1, own, me_slot, (*xn, c))
                sends[a, 2] = cp(a, 2, own, me_slot, (*yn, c))
                passes[a, 4] = cp(a, 4, outs[a].at[slot(*xn, c)], slot(*xn, c), sib)
                passes[a, 5] = cp(a, 5, outs[a].at[slot(*yn, c)], slot(*yn, c), sib)
            if whole or far:
                source = outs[a] if whole else ins[a]
                relayed = c * slot(*yn, c) + (1 - c) * slot(*xn, c)
                relay[a, 3] = cp(a, 3, source.at[relayed], relayed, (x + c * (1 - 2 * x), y + (1 - c) * (1 - 2 * y), c))
                passes[a, 6] = cp(a, 6, outs[a].at[slot(*dg, c)], slot(*dg, c), sib)
            arrive[a, 0] = arrival(a, 0, slot(x, y, 1 - c))
            arrive[a, 1] = arrival(a, 1, slot(*xn, c))
            arrive[a, 2] = arrival(a, 2, slot(*yn, c))
            arrive[a, 3] = arrival(a, 3, slot(*dg, c))
            arrive[a, 4] = arrival(a, 4, slot(*xn, 1 - c))
            arrive[a, 5] = arrival(a, 5, slot(*yn, 1 - c))
            arrive[a, 6] = arrival(a, 6, slot(*dg, 1 - c))
        return sends, local, relay, passes, arrive

    def pick(d, ks):
        return [d[a, k] for a in range(n) for k in ks if (a, k) in d]

    def start(ins, outs, sems):
        sends, local, relay, _, _ = copies(ins, outs, sems)
        for cp in local + pick(sends, (0, 1, 2)) + (pick(relay, (3,)) if far else []):
            cp.start()

    def forward_neighbours(ins, outs, sems):
        _, _, relay, passes, arrive = copies(ins, outs, sems)
        for cp in pick(arrive, (1, 2)):
            cp.wait_recv()
        for cp in pick(relay, (3,)) + pick(passes, (4, 5)):
            cp.start()

    def forward_diagonal(ins, outs, sems):
        _, _, _, passes, arrive = copies(ins, outs, sems)
        for cp in pick(arrive, (3,)):
            cp.wait_recv()
        for cp in pick(passes, (6,)):
            cp.start()

    def finish(ins, outs, sems):
        sends, local, relay, passes, arrive = copies(ins, outs, sems)
        for cp in pick(arrive, (6,) if far else (0, 4, 5, 6) if whole else (4, 5)):
            cp.wait_recv()
        for cp in pick(sends, (0, 1, 2)) + pick(relay, (3,)) + pick(passes, (4, 5, 6)):
            cp.wait_send()
        for cp in local:
            cp.wait()

    shapes = [a.shape[-2:] for a in arrs]
    forwards = [forward_diagonal] if far else [forward_neighbours] if near else [forward_neighbours, forward_diagonal]
    return _Job(arrs, [jax.ShapeDtypeStruct((N_DEV,) + s, a.dtype) for s, a in zip(shapes, arrs)],
                [pltpu.SemaphoreType.DMA((n, 7)), pltpu.SemaphoreType.DMA((n, 7)), pltpu.SemaphoreType.DMA((n,))],
                start, finish, forwards, forward_at)


def _pair_job(arrs):
    n = len(arrs)

    def copies(ins, outs, sems):
        send_sems, recv_sems, local_sems = sems
        x, y, c = _my_place()
        remote = [pltpu.make_async_remote_copy(
            src_ref=ins[a], dst_ref=outs[a].at[c], send_sem=send_sems.at[a], recv_sem=recv_sems.at[a],
            device_id=(x, y, 1 - c), device_id_type=MESH) for a in range(n)]
        local = []
        arrive = [pltpu.make_async_remote_copy(
            src_ref=ins[a], dst_ref=outs[a].at[1 - c], send_sem=send_sems.at[a], recv_sem=recv_sems.at[a],
            device_id=(x, y, 1 - c), device_id_type=MESH) for a in range(n)]
        return remote, local, arrive

    def start(ins, outs, sems):
        remote, local, _ = copies(ins, outs, sems)
        for cp in remote + local:
            cp.start()

    def finish(ins, outs, sems):
        remote, local, arrive = copies(ins, outs, sems)
        for cp in arrive:
            cp.wait_recv()
        for cp in remote:
            cp.wait_send()
        for cp in local:
            cp.wait()

    return _Job(arrs, [jax.ShapeDtypeStruct((2,) + a.shape, a.dtype) for a in arrs],
                [pltpu.SemaphoreType.DMA((n,)), pltpu.SemaphoreType.DMA((n,)), pltpu.SemaphoreType.DMA((n,))],
                start, finish)


def _run_job(job, name):
    ni, no = len(job.ins), len(job.outs)

    def body(*refs):
        ins, outs, sems = refs[:ni], refs[ni:ni + no], refs[ni + no:]
        job.start(ins, outs, sems)
        for fwd in job.forwards:
            fwd(ins, outs, sems)
        job.finish(ins, outs, sems)

    any_spec = pl.BlockSpec(memory_space=pl.ANY)
    return pl.pallas_call(
        body, name=name, out_shape=job.outs, in_specs=[any_spec] * ni, out_specs=[any_spec] * no,
        scratch_shapes=job.sems,
    )(*job.ins)


def _call(body, *, name, grid, in_specs, out_specs, out_shape, scratch_shapes=(), sem, args, jobs=(),
          prefetch=(), aliases=None):
    in_specs, out_specs, out_shape, scratch_shapes = list(in_specs), list(out_specs), list(out_shape), list(scratch_shapes)
    prefetch = list(prefetch)
    if not jobs and not prefetch and not aliases:
        res = pl.pallas_call(body, name=name, grid=grid, in_specs=in_specs, out_specs=out_specs, out_shape=out_shape,
                             scratch_shapes=scratch_shapes, compiler_params=_params(sem))(*args)
        return list(res), []
    n_in, n_out, n_scr = len(in_specs), len(out_specs), len(scratch_shapes)
    j_in = [len(j.ins) for j in jobs]
    j_out = [len(j.outs) for j in jobs]
    j_sem = [len(j.sems) for j in jobs]
    total = int(np.prod(grid))
    any_spec = pl.BlockSpec(memory_space=pl.ANY)

    def wrapped(*refs):
        pos = [0]

        def take(k):
            part = refs[pos[0]:pos[0] + k]
            pos[0] += k
            return part

        take(len(prefetch))
        ins = take(n_in)
        jins = [take(k) for k in j_in]
        outs = take(n_out)
        jouts = [take(k) for k in j_out]
        scr = take(n_scr)
        jsems = [take(k) for k in j_sem]
        step = pl.program_id(0)
        for d in range(1, len(grid)):
            step = step * grid[d] + pl.program_id(d)
        for j, job in enumerate(jobs):
            pl.when(step == 0)(functools.partial(job.start, jins[j], jouts[j], jsems[j]))
            for fwd, frac in zip(job.forwards, job.forward_at):
                at = min(int(total * frac), total - 1)
                pl.when(step == at)(functools.partial(fwd, jins[j], jouts[j], jsems[j]))
        body(*ins, *outs, *scr)
        for j, job in enumerate(jobs):
            pl.when(step == total - 1)(functools.partial(job.finish, jins[j], jouts[j], jsems[j]))

    grid_spec = pltpu.PrefetchScalarGridSpec(
        num_scalar_prefetch=len(prefetch), grid=grid,
        in_specs=in_specs + [any_spec] * sum(j_in),
        out_specs=out_specs + [any_spec] * sum(j_out),
        scratch_shapes=scratch_shapes + [s for j in jobs for s in j.sems])
    res = pl.pallas_call(
        wrapped, name=name, grid_spec=grid_spec,
        out_shape=out_shape + [o for j in jobs for o in j.outs],
        input_output_aliases={len(prefetch) + i: o for i, o in (aliases or {}).items()},
        compiler_params=_params(("arbitrary",) * len(grid)),
    )(*prefetch, *args, *[a for j in jobs for a in j.ins])
    res = list(res)
    own, rest = res[:n_out], res[n_out:]
    per_job = []
    for k in j_out:
        per_job.append(rest[:k])
        rest = rest[k:]
    return own, per_job


def _exchange_sibling(arrs, name):
    return _run_job(_sibling_job(arrs), name)


def _sibling_job(arrs):
    n = len(arrs)

    def copies(ins, outs, sems):
        send_sems, recv_sems = sems
        x, y, c = _my_place()
        return [pltpu.make_async_remote_copy(
            src_ref=ins[a].at[2 * k + 1 - c], dst_ref=outs[a].at[k],
            send_sem=send_sems.at[a, k], recv_sem=recv_sems.at[a, k],
            device_id=(x, y, 1 - c), device_id_type=MESH) for a in range(n) for k in range(4)]

    def start(ins, outs, sems):
        for cp in copies(ins, outs, sems):
            cp.start()

    def finish(ins, outs, sems):
        for cp in copies(ins, outs, sems):
            cp.wait()

    return _Job(arrs, [jax.ShapeDtypeStruct((4,) + a.shape[1:], a.dtype) for a in arrs],
                [pltpu.SemaphoreType.DMA((n, 4)), pltpu.SemaphoreType.DMA((n, 4))], start, finish)


def _chips_job(arrs):
    n = len(arrs)

    def copies(ins, outs, sems):
        send_sems, recv_sems = sems
        x, y, c = _my_place()
        chips = [(1 - x, y), (x, 1 - y), (1 - x, 1 - y)]
        return [pltpu.make_async_remote_copy(
            src_ref=ins[a].at[2 * px + py], dst_ref=outs[a].at[j],
            send_sem=send_sems.at[a, j], recv_sem=recv_sems.at[a, j],
            device_id=(px, py, c), device_id_type=MESH) for a in range(n) for j, (px, py) in enumerate(chips)]

    def start(ins, outs, sems):
        for cp in copies(ins, outs, sems):
            cp.start()

    def finish(ins, outs, sems):
        for cp in copies(ins, outs, sems):
            cp.wait()

    return _Job(arrs, [jax.ShapeDtypeStruct((3,) + a.shape[1:], a.dtype) for a in arrs],
                [pltpu.SemaphoreType.DMA((n, 3)), pltpu.SemaphoreType.DMA((n, 3))], start, finish)


def _pair_add(stack, from_sibling, c_idx, name, widths=None):
    _, r, cc = stack.shape
    tr = _tile(r, 512)
    ws = [cc] if widths is None else list(widths)
    assert sum(ws) == cc
    starts = [sum(ws[:q]) for q in range(len(ws))]

    def body(c_ref, a_ref, b_ref, *o_refs):
        res = a_ref[...].astype(F32) + b_ref[...].astype(F32)
        for o_ref, s0, w_ in zip(o_refs, starts, ws):
            o_ref[...] = res[:, s0:s0 + w_].astype(o_ref.dtype)

    grid_spec = pltpu.PrefetchScalarGridSpec(
        num_scalar_prefetch=1, grid=(4, r // tr),
        in_specs=[pl.BlockSpec((None, tr, cc), lambda k, i, c_ref: (2 * k + c_ref[0], i, 0)),
                  pl.BlockSpec((None, tr, cc), lambda k, i, c_ref: (k, i, 0))],
        out_specs=[pl.BlockSpec((None, tr, w_), lambda k, i, c_ref: (k, i, 0)) for w_ in ws])
    res = pl.pallas_call(
        body, name=name, grid_spec=grid_spec,
        out_shape=[jax.ShapeDtypeStruct((4, r, w_), BF16) for w_ in ws],
        compiler_params=_params(("parallel", "parallel")),
    )(c_idx, stack, from_sibling)
    return res[0] if widths is None else list(res)


_DIMS = {"nn": ((1,), (0,)), "nt": ((1,), (1,)), "tn": ((0,), (0,))}


def _mm(a, b, mode, *, tm, tn, tk=None, outs, epi=None, tiles=(), rows=(), name, jobs=(), chunks=None, into=None,
        n_total=None, rows_part=None):
    if mode == "tn":
        kk, m = a.shape
    else:
        m, kk = a.shape
    n_all = b.shape[0] if mode == "nt" else b.shape[1]
    nn = n_all if chunks is None else chunks[0].shape[0] * chunks[1]
    n_all = n_all if n_total is None else n_total
    tk = kk if tk is None else tk
    assert m % tm == 0 and nn % tn == 0 and kk % tk == 0, (name, a.shape, b.shape, tm, tn, tk)
    nk = kk // tk
    nt, nr, no = len(tiles), len(rows), len(outs)
    n_into = 0 if into is None else 1

    def body(*refs):
        a_ref, b_ref = refs[:2]
        tile_refs = refs[2:2 + nt]
        row_refs = refs[2 + nt:2 + nt + nr]
        out_refs = refs[2 + nt + nr + n_into:2 + nt + nr + n_into + no]
        part = lax.dot_general(a_ref[...], b_ref[...], (_DIMS[mode], ((), ())), preferred_element_type=F32)

        def finish(acc):
            if epi is None:
                res = (acc,)
            else:
                res = epi(acc, *[t[...] for t in tile_refs], *[r[...] for r in row_refs])
            for o_ref, val in zip(out_refs, res):
                o_ref[...] = val.astype(o_ref.dtype)

        if nk == 1:
            finish(part)
        else:
            acc_ref = refs[-1]
            k = pl.program_id(2)

            @pl.when(k == 0)
            def _():
                acc_ref[...] = part

            @pl.when(k > 0)
            def _():
                acc_ref[...] += part

            @pl.when(k == nk - 1)
            def _():
                finish(acc_ref[...])

    if chunks is None:
        col = b_row = lambda j, *_: j
        prefetch = []
    else:
        assert mode == "nt" and not tiles and not rows and chunks[1] % tn == 0
        per = chunks[1] // tn
        col = lambda j, ids, b_ids: ids[j // per] * per + j % per
        b_row = lambda j, ids, b_ids: b_ids[j // per] * per + j % per
        prefetch = [chunks[0], chunks[2]]
    row0, n_row_blocks = (0, m // tm) if rows_part is None else rows_part
    assert rows_part is None or (mode != "tn" and not tiles)
    if mode == "tn":
        a_spec = pl.BlockSpec((tk, tm), lambda i, j, k, *_: (k, i))
    else:
        a_spec = pl.BlockSpec((tm, tk), lambda i, j, k, *_: (i + row0, k))
    if mode == "nt":
        b_spec = pl.BlockSpec((tn, tk), lambda i, j, k, *s: (b_row(j, *s), k))
    else:
        b_spec = pl.BlockSpec((tk, tn), lambda i, j, k, *_: (k, j))
    tile_specs = [pl.BlockSpec((tm, tn), functools.partial(lambda i, j, k, off: (i, j + off), off=off))
                  for _, off in tiles]
    row_specs = [pl.BlockSpec((r.shape[0], tn), lambda i, j, k: (0, j)) for r in rows]
    out_spec = pl.BlockSpec((tm, tn), lambda i, j, k, *s: (i + row0, col(j, *s)))
    args = [a, b, *[t for t, _ in tiles], *rows]
    in_specs = [a_spec, b_spec] + tile_specs + row_specs
    aliases = None
    if into is not None:
        aliases = {len(args): 0}
        args.append(into)
        in_specs.append(pl.BlockSpec(memory_space=pl.ANY))
    res, jres = _call(
        body, name=name, grid=(n_row_blocks, nn // tn, nk),
        in_specs=in_specs, out_specs=[out_spec] * no,
        out_shape=[jax.ShapeDtypeStruct((m, n_all), dt) for dt in outs],
        scratch_shapes=[pltpu.VMEM((tm, tn), F32)] if nk > 1 else [],
        sem=("parallel", "parallel", "arbitrary"),
        args=args, jobs=jobs, prefetch=prefetch, aliases=aliases)
    return (res, jres) if jobs else res


def _ada_fwd(c_all, w_ada, b_ada_cols, name):
    nb, dd = c_all.shape
    ncol = w_ada.shape[1]
    tn = _tile(ncol, 512, LANE)

    def body(c_ref, w_ref, b_ref, o_ref):
        cv = c_ref[...]
        act = (cv * _sigmoid(cv)).astype(BF16)
        o_ref[...] = jnp.dot(act, w_ref[...].astype(BF16), preferred_element_type=F32) + b_ref[...]

    return pl.pallas_call(
        body, name=name, grid=(ncol // tn,),
        in_specs=[pl.BlockSpec((nb, dd), lambda j: (0, 0)), pl.BlockSpec((dd, tn), lambda j: (0, j)),
                  pl.BlockSpec((1, tn), lambda j: (0, j))],
        out_specs=pl.BlockSpec((nb, tn), lambda j: (0, j)),
        out_shape=jax.ShapeDtypeStruct((nb, ncol), F32),
        compiler_params=_params(("parallel",)),
    )(c_all, w_ada, b_ada_cols)


def _ada_bwd(c_all_t, dmod_cols, name):
    dd, nb = c_all_t.shape
    ncol = dmod_cols.shape[1]
    tr = _tile(dd, 256)

    def body(c_ref, d_ref, o_ref):
        cv = c_ref[...]
        act = (cv * _sigmoid(cv)).astype(BF16).astype(F32)
        dm = d_ref[...].astype(BF16).astype(F32)
        acc = act[:, 0:1] * dm[0:1, :]
        for bi in range(1, nb):
            acc = acc + act[:, bi:bi + 1] * dm[bi:bi + 1, :]
        o_ref[...] = acc

    return pl.pallas_call(
        body, name=name, grid=(dd // tr,),
        in_specs=[pl.BlockSpec((tr, nb), lambda i: (i, 0)), pl.BlockSpec((nb, ncol), lambda i: (0, 0))],
        out_specs=pl.BlockSpec((tr, ncol), lambda i: (i, 0)),
        out_shape=jax.ShapeDtypeStruct((dd, ncol), F32),
        compiler_params=_params(("parallel",)),
    )(c_all_t, dmod_cols)


def _norm_mod(x, gain, mod, row0, name):
    t, dd = x.shape
    tt = _tile(t, 512)

    def body(x_ref, g_ref, m_ref, h_ref):
        xv = x_ref[...]
        r = lax.rsqrt(jnp.mean(xv * xv, axis=-1, keepdims=True) + EPS)
        shift, scale = m_ref[row0:row0 + 1, :], m_ref[row0 + 1:row0 + 2, :]
        h_ref[...] = ((xv * r * g_ref[...]) * (1.0 + scale) + shift).astype(BF16)

    return pl.pallas_call(
        body, name=name, grid=(t // tt,),
        in_specs=[pl.BlockSpec((tt, dd), lambda i: (i, 0)), pl.BlockSpec((1, dd), lambda i: (0, 0)),
                  pl.BlockSpec((8, dd), lambda i: (0, 0))],
        out_specs=pl.BlockSpec((tt, dd), lambda i: (i, 0)),
        out_shape=jax.ShapeDtypeStruct((t, dd), BF16),
        compiler_params=_params(("parallel",)),
    )(x, gain, mod)


def _norm_mod_bwd(x, gain, mod, row0, dh, dres, branch, name, jobs=()):
    t, dd = x.shape
    tt = _tile(t, 256)
    has_branch = branch is not None

    def body(*refs):
        if has_branch:
            x_ref, g_ref, m_ref, dh_ref, dres_ref, o_ref, dx_ref, st_ref, do_ref = refs
        else:
            x_ref, g_ref, m_ref, dh_ref, dres_ref, dx_ref, st_ref = refs

        @pl.when(pl.program_id(0) == 0)
        def _():
            st_ref[...] = jnp.zeros_like(st_ref)

        xv = x_ref[...]
        g = g_ref[...]
        scale = m_ref[row0 + 1:row0 + 2, :]
        r = lax.rsqrt(jnp.mean(xv * xv, axis=-1, keepdims=True) + EPS)
        nv = xv * r
        dhv = dh_ref[...].astype(F32)
        st_ref[0:1, :] += jnp.sum(dhv, axis=0, keepdims=True)
        st_ref[1:2, :] += jnp.sum(dhv * (nv * g), axis=0, keepdims=True)
        dng = dhv * (1.0 + scale)
        st_ref[2:3, :] += jnp.sum(dng * nv, axis=0, keepdims=True)
        dn = dng * g
        dx = dres_ref[...] + r * (dn - nv * jnp.mean(dn * nv, axis=-1, keepdims=True))
        dx_ref[...] = dx
        if has_branch:
            ov = o_ref[...].astype(F32)
            gate = m_ref[branch[1]:branch[1] + 1, :]
            st_ref[3:4, :] += jnp.sum(dx * ov, axis=0, keepdims=True)
            do_ref[...] = (dx * gate).astype(BF16)

    row = pl.BlockSpec((tt, dd), lambda i: (i, 0))
    in_specs = [row, pl.BlockSpec((1, dd), lambda i: (0, 0)), pl.BlockSpec((8, dd), lambda i: (0, 0)), row, row]
    args = [x, gain, mod, dh, dres]
    out_specs = [row, pl.BlockSpec((8, dd), lambda i: (0, 0))]
    out_shape = [jax.ShapeDtypeStruct((t, dd), F32), jax.ShapeDtypeStruct((8, dd), F32)]
    if has_branch:
        in_specs.append(row)
        args.append(branch[0])
        out_specs.append(row)
        out_shape.append(jax.ShapeDtypeStruct((t, dd), BF16))
    res, jres = _call(body, name=name, grid=(t // tt,), in_specs=in_specs, out_specs=out_specs, out_shape=out_shape,
                      sem=("arbitrary",), args=args, jobs=jobs)
    return (res, jres) if jobs else res


def _final_loss(x2, o2, final_g, target, mod, name):
    t, dd = x2.shape
    tt = _tile(t, 256)
    nsteps = t // tt

    def body(x_ref, o_ref, g_ref, t_ref, m_ref, loss_ref, dx_ref, do_ref, st_ref, lacc):
        i = pl.program_id(0)

        @pl.when(i == 0)
        def _():
            st_ref[...] = jnp.zeros_like(st_ref)
            lacc[...] = jnp.zeros_like(lacc)

        xv = x_ref[...]
        g = g_ref[...]
        r = lax.rsqrt(jnp.mean(xv * xv, axis=-1, keepdims=True) + EPS)
        nv = xv * r
        err = nv * g - t_ref[...]
        lacc[...] += jnp.sum(err * err, axis=0, keepdims=True)
        dy = err * (1.0 / dd)
        st_ref[0:1, :] += jnp.sum(dy * nv, axis=0, keepdims=True)
        dn = dy * g
        dx = r * (dn - nv * jnp.mean(dn * nv, axis=-1, keepdims=True))
        dx_ref[...] = dx
        st_ref[1:2, :] += jnp.sum(dx * o_ref[...].astype(F32), axis=0, keepdims=True)
        do_ref[...] = (dx * m_ref[5:6, :]).astype(BF16)

        @pl.when(i == nsteps - 1)
        def _():
            total = jnp.sum(lacc[...], axis=-1, keepdims=True) * (0.5 / dd)
            loss_ref[...] = jnp.broadcast_to(total, loss_ref.shape)

    row = pl.BlockSpec((tt, dd), lambda i: (i, 0))
    return pl.pallas_call(
        body, name=name, grid=(nsteps,),
        in_specs=[row, row, pl.BlockSpec((1, dd), lambda i: (0, 0)), row, pl.BlockSpec((8, dd), lambda i: (0, 0))],
        out_specs=[pl.BlockSpec((8, LANE), lambda i: (0, 0)), row, row, pl.BlockSpec((8, dd), lambda i: (0, 0))],
        out_shape=[jax.ShapeDtypeStruct((8, LANE), F32), jax.ShapeDtypeStruct((t, dd), F32),
                   jax.ShapeDtypeStruct((t, dd), BF16), jax.ShapeDtypeStruct((8, dd), F32)],
        scratch_shapes=[pltpu.VMEM((1, dd), F32)],
        compiler_params=_params(("arbitrary",)),
    )(x2, o2, final_g, target, mod)


def _lru_gates(xt, prev8, lp, wa_ref, wx_ref, first_tile, cw):
    tt = xt.shape[0]
    ext = jnp.concatenate([prev8, xt], axis=0)
    xs = [pltpu.roll(ext, s, 0)[8:, :] if s else xt for s in range(CONV_WIDTH)]
    xc = lp[0:1, :] + lp[7:8, :] * xs[0] + lp[6:7, :] * xs[1] + lp[5:6, :] * xs[2] + lp[4:5, :] * xs[3]
    xcb = xc.astype(BF16)
    za, zi = [], []
    for jb in range(cw // LRU_BLOCK_W):
        sl = slice(jb * LRU_BLOCK_W, (jb + 1) * LRU_BLOCK_W)
        za.append(jnp.dot(xcb[:, sl], wa_ref[jb].astype(BF16), preferred_element_type=F32))
        zi.append(jnp.dot(xcb[:, sl], wx_ref[jb].astype(BF16), preferred_element_type=F32))
    ra = _sigmoid(jnp.concatenate(za, axis=1) + lp[1:2, :])
    ri = _sigmoid(jnp.concatenate(zi, axis=1) + lp[2:3, :])
    sp = _softplus(-lp[3:4, :])
    log_a = -LRU_C * ra * sp
    av = jnp.exp(log_a)
    seq_start = jnp.logical_and(first_tile, lax.broadcasted_iota(jnp.int32, (tt, 1), 0) == 0)
    mult = jnp.where(seq_start, 1.0, jnp.sqrt(_neg_expm1(2.0 * log_a, av * av)))
    return xs, xc, xcb, ra, ri, sp, av, mult, seq_start


def _scan_groups(a, u, reverse):
    tt, cw = a.shape
    a3, u3 = a.reshape(tt // 8, 8, cw), u.reshape(tt // 8, 8, cw)
    r = lax.broadcasted_iota(jnp.int32, (1, 8, 1), 1)
    for s in (1, 2, 4):
        shift = 8 - s if reverse else s
        take = r < 8 - s if reverse else r >= s
        u3 = jnp.where(take, a3 * pltpu.roll(u3, shift, 1) + u3, u3)
        a3 = jnp.where(take, a3 * pltpu.roll(a3, shift, 1), a3)
    return a3, u3


def _scan_carry(a_s, u_s, h_s, carry, reverse):
    ng = a_s.shape[0]

    def step(i, h):
        g = ng - 1 - i if reverse else i
        hg = u_s[g] + a_s[g] * h
        h_s[g] = hg
        return hg[0:1, :] if reverse else hg[7:8, :]

    return lax.fori_loop(0, ng, step, carry, unroll=4)


def _lru_fwd(proj, lp, wa, wx, name, jobs=()):
    t = proj.shape[0]
    w = D_MODEL
    cw = LRU_CW
    tt = _tile(t, 256)
    nb = cw // LRU_BLOCK_W

    def body(lx_ref, lg_ref, lp_ref, wa_ref, wx_ref, rec_ref, ya_ref, a_s, u_s, h_s, halo, carry):
        ti = pl.program_id(1)

        @pl.when(ti == 0)
        def _():
            halo[...] = jnp.zeros_like(halo)
            carry[...] = jnp.zeros_like(carry)

        xt = lx_ref[...].astype(F32)
        lp_v = lp_ref[...]
        _, xc, _, _, ri, _, av, mult, _ = _lru_gates(xt, halo[...], lp_v, wa_ref, wx_ref, ti == 0, cw)
        halo[...] = xt[tt - 8:, :]
        a_s[...], u_s[...] = _scan_groups(av, mult * (ri * xc), False)
        carry[...] = _scan_carry(a_s, u_s, h_s, carry[...], False)
        rec = h_s[...].reshape(tt, cw)
        rec_ref[...] = rec.astype(BF16)
        gelu, _ = _gelu_tanh(lg_ref[...].astype(F32))
        ya_ref[...] = (rec * gelu).astype(BF16)

    off_g = OFF_LG // cw
    res, jres = _call(
        body, name=name, grid=(w // cw, t // tt),
        in_specs=[pl.BlockSpec((tt, cw), lambda ci, ti: (ti, ci)),
                  pl.BlockSpec((tt, cw), lambda ci, ti: (ti, off_g + ci)),
                  pl.BlockSpec((8, cw), lambda ci, ti: (0, ci)),
                  pl.BlockSpec((nb, LRU_BLOCK_W, LRU_BLOCK_W), lambda ci, ti: (ci, 0, 0)),
                  pl.BlockSpec((nb, LRU_BLOCK_W, LRU_BLOCK_W), lambda ci, ti: (ci, 0, 0))],
        out_specs=[pl.BlockSpec((tt, cw), lambda ci, ti: (ti, ci))] * 2,
        out_shape=[jax.ShapeDtypeStruct((t, w), BF16)] * 2,
        scratch_shapes=[pltpu.VMEM((tt // 8, 8, cw), F32)] * 3 + [pltpu.VMEM((8, cw), F32), pltpu.VMEM((1, cw), F32)],
        sem=("parallel", "arbitrary"), args=[proj, proj, lp, wa, wx], jobs=jobs)
    return (res, jres) if jobs else res


def _lru_bwd(proj, rec, dya, lp, wa, wx, name, jobs=()):
    t = proj.shape[0]
    w = D_MODEL
    cw = LRU_CW
    tt = _tile(t, 256)
    nt = t // tt
    nb = cw // LRU_BLOCK_W
    r8 = tt // 8

    def body(lx_ref, lxp_ref, lg_ref, rec_ref, recp_ref, dya_ref, lp_ref, wa_ref, wx_ref,
             dlx_ref, dlg_ref, st_ref, dwa_ref, dwx_ref, a_s, d_s, dh_s, nhalo, carry):
        step_i = pl.program_id(1)
        ti = nt - 1 - step_i

        @pl.when(step_i == 0)
        def _():
            st_ref[...] = jnp.zeros_like(st_ref)
            dwa_ref[...] = jnp.zeros_like(dwa_ref)
            dwx_ref[...] = jnp.zeros_like(dwx_ref)
            nhalo[...] = jnp.zeros_like(nhalo)
            carry[...] = jnp.zeros_like(carry)

        first = ti == 0
        keep = jnp.where(first, 0.0, 1.0)
        xt = lx_ref[...].astype(F32)
        prev8 = lxp_ref[...].astype(F32) * keep
        lp_v = lp_ref[...]
        xs, xc, xcb, ra, ri, sp, av, mult, seq_start = _lru_gates(xt, prev8, lp_v, wa_ref, wx_ref, first, cw)

        lg = lg_ref[...].astype(F32)
        gelu, th = _gelu_tanh(lg)
        dyav = dya_ref[...].astype(F32)
        recv = rec_ref[...].astype(F32)
        dlg_ref[...] = (dyav * recv * _gelu_tanh_grad(lg, th)).astype(BF16)

        drec = dyav * gelu
        e_in = carry[...]
        a_s[...], d_s[...] = _scan_groups(av, av * drec, True)
        carry[...] = _scan_carry(a_s, d_s, dh_s, e_in, True)
        e_next = jnp.concatenate([dh_s[...].reshape(tt, cw), jnp.broadcast_to(e_in, (8, cw))], axis=0)
        dh = drec + pltpu.roll(e_next, tt + 7, 0)[:tt, :]
        rprev8 = recp_ref[...].astype(F32) * keep
        hprev = pltpu.roll(jnp.concatenate([rprev8, recv], axis=0), 1, 0)[8:, :]
        da = dh * hprev
        dmult = jnp.where(seq_start, 0.0, dh * ri * xc)
        dri = dh * mult * xc
        dxc = dh * mult * ri
        dlog_a = da * av - dmult * (av * av) / mult
        dra = dlog_a * (-LRU_C * sp)
        lam = lp_v[3:4, :]
        st_ref[3:4, :] += jnp.sum(dlog_a * ra, axis=0, keepdims=True) * (LRU_C * _sigmoid(-lam))
        dza = dra * ra * (1.0 - ra)
        dzi = dri * ri * (1.0 - ri)
        st_ref[1:2, :] += jnp.sum(dza, axis=0, keepdims=True)
        st_ref[2:3, :] += jnp.sum(dzi, axis=0, keepdims=True)
        dzab, dzib = dza.astype(BF16), dzi.astype(BF16)
        back = []
        for jb in range(nb):
            sl = slice(jb * LRU_BLOCK_W, (jb + 1) * LRU_BLOCK_W)
            dwa_ref[jb] += lax.dot_general(xcb[:, sl], dzab[:, sl], (_DIMS["tn"], ((), ())), preferred_element_type=F32)
            dwx_ref[jb] += lax.dot_general(xcb[:, sl], dzib[:, sl], (_DIMS["tn"], ((), ())), preferred_element_type=F32)
            back.append(
                lax.dot_general(dzab[:, sl], wa_ref[jb].astype(BF16), (_DIMS["nt"], ((), ())), preferred_element_type=F32)
                + lax.dot_general(dzib[:, sl], wx_ref[jb].astype(BF16), (_DIMS["nt"], ((), ())), preferred_element_type=F32))
        dxc = dxc + jnp.concatenate(back, axis=1)
        st_ref[0:1, :] += jnp.sum(dxc, axis=0, keepdims=True)
        for k in range(CONV_WIDTH):
            st_ref[4 + k:5 + k, :] += jnp.sum(dxc * xs[CONV_WIDTH - 1 - k], axis=0, keepdims=True)
        ext = jnp.concatenate([dxc, nhalo[...]], axis=0)
        dlx = lp_v[7:8, :] * dxc
        for s in range(1, CONV_WIDTH):
            dlx = dlx + lp_v[7 - s:8 - s, :] * pltpu.roll(ext, tt + 8 - s, 0)[:tt, :]
        dlx_ref[...] = dlx.astype(BF16)
        nhalo[...] = dxc[0:8, :]

    off_g = OFF_LG // cw
    tile = lambda off: pl.BlockSpec((tt, cw), lambda ci, si: (nt - 1 - si, off + ci))
    prev = lambda off: pl.BlockSpec((8, cw), lambda ci, si: (jnp.maximum((nt - 1 - si) * r8 - 1, 0), off + ci))
    wspec = pl.BlockSpec((nb, LRU_BLOCK_W, LRU_BLOCK_W), lambda ci, si: (ci, 0, 0))
    st_spec = pl.BlockSpec((8, cw), lambda ci, si: (0, ci))
    res, jres = _call(
        body, name=name, grid=(w // cw, nt),
        in_specs=[tile(0), prev(0), tile(off_g), tile(0), prev(0), tile(0), st_spec, wspec, wspec],
        out_specs=[tile(0), tile(0), st_spec, wspec, wspec],
        out_shape=[jax.ShapeDtypeStruct((t, w), BF16), jax.ShapeDtypeStruct((t, w), BF16),
                   jax.ShapeDtypeStruct((8, w), F32),
                   jax.ShapeDtypeStruct((LRU_BLOCKS, LRU_BLOCK_W, LRU_BLOCK_W), F32),
                   jax.ShapeDtypeStruct((LRU_BLOCKS, LRU_BLOCK_W, LRU_BLOCK_W), F32)],
        scratch_shapes=[pltpu.VMEM((tt // 8, 8, cw), F32)] * 3 + [pltpu.VMEM((8, cw), F32), pltpu.VMEM((1, cw), F32)],
        sem=("parallel", "arbitrary"), args=[proj, proj, proj, rec, rec, dya, lp, wa, wx], jobs=jobs)
    return (res, jres) if jobs else res


def _band_valid(first_block):
    qi = lax.broadcasted_iota(jnp.int32, (BLOCK, 2 * BLOCK), 0)
    ki = lax.broadcasted_iota(jnp.int32, (BLOCK, 2 * BLOCK), 1)
    rel = qi + BLOCK - ki
    valid = jnp.logical_and(rel >= 0, rel < BLOCK)
    return jnp.logical_and(valid, jnp.logical_or(ki >= BLOCK, jnp.logical_not(first_block)))


ATTN_STACK_BWD = 8


def _low_half():
    return lax.broadcasted_iota(jnp.int32, (1, LANE), 1) < HEAD_DIM


def _stack_heads(x, h0, ns):
    low = _low_half()
    parts = []
    for g in range(ns):
        h = h0 + g
        slab = x[:, (h // 2) * LANE:(h // 2 + 1) * LANE]
        parts.append(jnp.where(low if h % 2 == 0 else jnp.logical_not(low), slab, jnp.zeros_like(slab)))
    return jnp.concatenate(parts, axis=0)


def _unstack_heads(y, ns):
    low = _low_half()
    pairs = [jnp.where(low, y[(2 * j) * BLOCK:(2 * j + 1) * BLOCK], y[(2 * j + 1) * BLOCK:(2 * j + 2) * BLOCK])
             for j in range(ns // 2)]
    return pairs[0] if len(pairs) == 1 else jnp.concatenate(pairs, axis=1)


def _dup_kv(xall, kv):
    part = xall[:, kv * HEAD_DIM:(kv + 1) * HEAD_DIM]
    return jnp.concatenate([part, part], axis=1)


def _fold_halves(a):
    return a + pltpu.roll(a, HEAD_DIM, 1)


def _group_probs(qg, k2, bias_g, sink_g, lse_g, valid):
    s = lax.dot_general(qg, k2, (_DIMS["nt"], ((), ())), preferred_element_type=F32) * (HEAD_DIM ** -0.5)
    s = jnp.where(valid[None], s.reshape(bias_g.shape) + bias_g, NEG_INF)
    return jnp.exp(s - lse_g), jnp.exp(sink_g - lse_g)


def _group_sinks(s_ref, h0, ns):
    return jnp.concatenate([jnp.full((1, BLOCK, 1), s_ref[h0 + g], F32) for g in range(ns)], axis=0)


def _attn_specs(t):
    nblk = t // BLOCK
    q_spec = pl.BlockSpec((BLOCK, D_MODEL), lambda n: (n, OFF_Q // D_MODEL))
    kc = pl.BlockSpec((BLOCK, KV_WIDTH), lambda n: (n, OFF_K // KV_WIDTH))
    kp = pl.BlockSpec((BLOCK, KV_WIDTH), lambda n: (jnp.maximum(n - 1, 0), OFF_K // KV_WIDTH))
    vc = pl.BlockSpec((BLOCK, KV_WIDTH), lambda n: (n, OFF_V // KV_WIDTH))
    vp = pl.BlockSpec((BLOCK, KV_WIDTH), lambda n: (jnp.maximum(n - 1, 0), OFF_V // KV_WIDTH))
    bias_spec = pl.BlockSpec((N_Q_HEADS, BLOCK, 2 * BLOCK), lambda n: (0, 0, 0))
    sink_spec = pl.BlockSpec(memory_space=pltpu.SMEM)
    return nblk, [q_spec, kp, kc, vp, vc, bias_spec, sink_spec]


def _attn_fwd(proj, bias, sinks, name, jobs=()):
    t = proj.shape[0]
    nblk, in_specs = _attn_specs(t)

    def body(q_ref, kp_ref, kc_ref, vp_ref, vc_ref, b_ref, s_ref, o_ref, lse_ref):
        valid = _band_valid(pl.program_id(0) == 0)
        lane = lax.broadcasted_iota(jnp.int32, (1, LANE), 1)
        q = q_ref[...]
        kall = jnp.concatenate([kp_ref[...], kc_ref[...]], axis=0)
        vall = jnp.concatenate([vp_ref[...], vc_ref[...]], axis=0)
        outs = []
        lse = jnp.zeros((BLOCK, LANE), F32)
        for h in range(N_Q_HEADS):
            kv = h // GROUP
            kk = kall[:, kv * HEAD_DIM:(kv + 1) * HEAD_DIM]
            vv = vall[:, kv * HEAD_DIM:(kv + 1) * HEAD_DIM]
            s = lax.dot_general(q[:, h * HEAD_DIM:(h + 1) * HEAD_DIM], kk, (_DIMS["nt"], ((), ())),
                                preferred_element_type=F32) * (HEAD_DIM ** -0.5)
            s = jnp.where(valid, s + b_ref[h], NEG_INF)
            sink = s_ref[h]
            m = jnp.maximum(jnp.max(s, axis=-1, keepdims=True), sink)
            e = jnp.exp(s - m)
            den = jnp.sum(e, axis=-1, keepdims=True) + jnp.exp(sink - m)
            lse = jnp.where(lane == h, m + jnp.log(den), lse)
            outs.append(jnp.dot((e * (1.0 / den)).astype(BF16), vv, preferred_element_type=F32))
        o_ref[...] = jnp.concatenate(outs, axis=1).astype(BF16)
        lse_ref[...] = lse

    res, jres = _call(
        body, name=name, grid=(nblk,), in_specs=in_specs,
        out_specs=[pl.BlockSpec((BLOCK, D_MODEL), lambda n: (n, 0)), pl.BlockSpec((BLOCK, LANE), lambda n: (n, 0))],
        out_shape=[jax.ShapeDtypeStruct((t, D_MODEL), BF16), jax.ShapeDtypeStruct((t, LANE), F32)],
        sem=("parallel",), args=[proj, proj, proj, proj, proj, bias, sinks], jobs=jobs)
    return (res, jres) if jobs else res


def _attn_bwd(proj, datt, lse, bias, sinks, name, jobs=()):
    t = proj.shape[0]
    nblk, in_specs = _attn_specs(t)
    in_specs = in_specs + [pl.BlockSpec((BLOCK, D_MODEL), lambda n: (n, 0)), pl.BlockSpec((BLOCK, LANE), lambda n: (n, 0))]
    scale = HEAD_DIM ** -0.5

    def body(q_ref, kp_ref, kc_ref, vp_ref, vc_ref, b_ref, s_ref, do_ref, lse_ref,
             dq_ref, dk_ref, dv_ref, db_ref, ds_ref):
        n = pl.program_id(0)

        @pl.when(n == 0)
        def _():
            dk_ref[...] = jnp.zeros_like(dk_ref)
            dv_ref[...] = jnp.zeros_like(dv_ref)
            db_ref[...] = jnp.zeros_like(db_ref)
            ds_ref[...] = jnp.zeros_like(ds_ref)

        valid = _band_valid(n == 0)
        lane = lax.broadcasted_iota(jnp.int32, (1, LANE), 1)
        q = q_ref[...]
        do = do_ref[...]
        lse_all = lse_ref[...]
        kall = jnp.concatenate([kp_ref[...], kc_ref[...]], axis=0)
        vall = jnp.concatenate([vp_ref[...], vc_ref[...]], axis=0)
        low = _low_half()
        dqs, dks, dvs = [], [], []
        dsink_row = jnp.zeros((1, LANE), F32)
        ns = ATTN_STACK_BWD
        for kv in range(N_KV_HEADS):
            k2, v2 = _dup_kv(kall, kv), _dup_kv(vall, kv)
            dk_acc = jnp.zeros((2 * BLOCK, LANE), F32)
            dv_acc = jnp.zeros((2 * BLOCK, LANE), F32)
            for h0 in range(kv * GROUP, (kv + 1) * GROUP, ns):
                qg, dog = _stack_heads(q, h0, ns), _stack_heads(do, h0, ns)
                lse_g = jnp.concatenate([lse_all[:, h0 + g:h0 + g + 1][None] for g in range(ns)], axis=0)
                p, psink = _group_probs(qg, k2, b_ref[h0:h0 + ns], _group_sinks(s_ref, h0, ns), lse_g, valid)
                dp = lax.dot_general(dog, v2, (_DIMS["nt"], ((), ())), preferred_element_type=F32)
                dp = dp.reshape(ns, BLOCK, 2 * BLOCK)
                delta = jnp.sum(p * dp, axis=-1, keepdims=True)
                ds = p * (dp - delta)
                db_ref[h0:h0 + ns] += ds
                dsink = -jnp.sum(psink * delta, axis=1, keepdims=True)
                for g in range(ns):
                    dsink_row = dsink_row + jnp.where(lane == h0 + g, dsink[g], 0.0)
                dsb = ds.reshape(ns * BLOCK, 2 * BLOCK).astype(BF16)
                pb = p.reshape(ns * BLOCK, 2 * BLOCK).astype(BF16)
                dqs.append(_unstack_heads(jnp.dot(dsb, k2, preferred_element_type=F32) * scale, ns))
                dk_acc = dk_acc + lax.dot_general(dsb, qg, (_DIMS["tn"], ((), ())), preferred_element_type=F32)
                dv_acc = dv_acc + lax.dot_general(pb, dog, (_DIMS["tn"], ((), ())), preferred_element_type=F32)
            dks.append(_fold_halves(dk_acc) * scale)
            dvs.append(_fold_halves(dv_acc))
        dq_ref[...] = jnp.concatenate(dqs, axis=1).astype(BF16)
        rows = pl.ds(pl.multiple_of(n * BLOCK, BLOCK), 2 * BLOCK)
        dk_ref[rows, :] += jnp.concatenate([jnp.where(low, dks[0], dks[1]), jnp.where(low, dks[2], dks[3])], axis=1)
        dv_ref[rows, :] += jnp.concatenate([jnp.where(low, dvs[0], dvs[1]), jnp.where(low, dvs[2], dvs[3])], axis=1)
        ds_ref[0:1, :] += dsink_row

    res, jres = _call(
        body, name=name, grid=(nblk,), in_specs=in_specs,
        out_specs=[pl.BlockSpec((BLOCK, D_MODEL), lambda n: (n, 0)),
                   pl.BlockSpec((t + BLOCK, KV_WIDTH), lambda n: (0, 0)),
                   pl.BlockSpec((t + BLOCK, KV_WIDTH), lambda n: (0, 0)),
                   pl.BlockSpec((N_Q_HEADS, BLOCK, 2 * BLOCK), lambda n: (0, 0, 0)),
                   pl.BlockSpec((8, LANE), lambda n: (0, 0))],
        out_shape=[jax.ShapeDtypeStruct((t, D_MODEL), BF16),
                   jax.ShapeDtypeStruct((t + BLOCK, KV_WIDTH), F32),
                   jax.ShapeDtypeStruct((t + BLOCK, KV_WIDTH), F32),
                   jax.ShapeDtypeStruct((N_Q_HEADS, BLOCK, 2 * BLOCK), F32),
                   jax.ShapeDtypeStruct((8, LANE), F32)],
        sem=("arbitrary",), args=[proj, proj, proj, proj, proj, bias, sinks, datt, lse], jobs=jobs)
    return (res, jres) if jobs else res


def _bucket_table():
    qi = np.arange(BLOCK)[:, None]
    ki = np.arange(2 * BLOCK)[None, :]
    rel = np.maximum(qi + BLOCK - ki, 0)
    max_exact = N_BUCKETS // 2
    relf = np.maximum(rel, 1).astype(np.float32)
    large = max_exact + (np.log(relf / max_exact) / math.log(MAX_DISTANCE / max_exact)
                         * (N_BUCKETS - max_exact)).astype(np.int32)
    large = np.minimum(large, N_BUCKETS - 1)
    return np.where(rel < max_exact, rel, large).astype(np.int32)


def _rel_bias_fwd(rel_bias, onehot, name):
    n = onehot.shape[1]
    tn = _tile(n, 8192, LANE)

    def body(r_ref, o_ref, out_ref):
        out_ref[...] = lax.dot_general(r_ref[...], o_ref[...], (_DIMS["tn"], ((), ())),
                                       precision=lax.Precision.HIGHEST, preferred_element_type=F32)

    return pl.pallas_call(
        body, name=name, grid=(n // tn,),
        in_specs=[pl.BlockSpec(rel_bias.shape, lambda j: (0, 0)), pl.BlockSpec((N_BUCKETS, tn), lambda j: (0, j))],
        out_specs=pl.BlockSpec((N_Q_HEADS, tn), lambda j: (0, j)),
        out_shape=jax.ShapeDtypeStruct((N_Q_HEADS, n), F32),
        compiler_params=_params(("parallel",)),
    )(rel_bias, onehot)


def _rel_bias_bwd(dbias, onehot, name):
    def body(d_ref, o_ref, out_ref):
        out_ref[...] = lax.dot_general(d_ref[...], o_ref[...], (_DIMS["nt"], ((), ())),
                                       precision=lax.Precision.HIGHEST, preferred_element_type=F32)

    full = pl.BlockSpec(dbias.shape, lambda: (0, 0))
    return pl.pallas_call(
        body, name=name, in_specs=[full, full],
        out_specs=pl.BlockSpec((N_Q_HEADS, N_BUCKETS), lambda: (0, 0)),
        out_shape=jax.ShapeDtypeStruct((N_Q_HEADS, N_BUCKETS), F32),
        compiler_params=_params(()),
    )(dbias, onehot)


def _part_specs(gparts, part_index, tr, cc):
    blk = pl.BlockSpec((tr, cc), lambda i, *_: (i, 0))
    specs = []
    for i, _ in enumerate(gparts):
        if i in part_index:
            ix = part_index[i]
            if isinstance(ix, int):
                specs.append(pl.BlockSpec((None, tr, cc), functools.partial(lambda i, *_, ix: (ix, i, 0), ix=ix)))
            else:
                specs.append(pl.BlockSpec((None, tr, cc), lambda i, ix_ref: (ix_ref[0], i, 0)))
        else:
            specs.append(blk)
    return blk, specs


def _sum_parts(gparts, part_index, name):
    r, cc = gparts[0].shape[-2:]
    tr = _tile(r, 256)
    dyn = [i for i, ix in part_index.items() if not isinstance(ix, int)]
    assert len(dyn) <= 1

    def body(*refs):
        refs = refs[len(dyn):]
        g = refs[0][...].astype(F32)
        for gr in refs[1:-1]:
            g = g + gr[...].astype(F32)
        refs[-1][...] = g

    blk, specs = _part_specs(gparts, part_index, tr, cc)
    grid_spec = pltpu.PrefetchScalarGridSpec(num_scalar_prefetch=len(dyn), grid=(r // tr,), in_specs=specs, out_specs=blk)
    return pl.pallas_call(
        body, name=name, grid_spec=grid_spec, out_shape=jax.ShapeDtypeStruct((r, cc), F32),
        compiler_params=_params(("parallel",)),
    )(*[part_index[i] for i in dyn], *gparts)


def _adamw(w, m, v, gparts, name, part_index=None):
    r, cc = w.shape
    tr = _tile(r, 128)
    np_ = len(gparts)
    part_index = part_index or {}
    dyn = [i for i, ix in part_index.items() if not isinstance(ix, int)]
    assert len(dyn) <= 1
    bc1 = 1.0 / (1.0 - ADAM_B1 ** ADAM_STEP)
    bc2 = 1.0 / (1.0 - ADAM_B2 ** ADAM_STEP)

    def body(*refs):
        refs = refs[len(dyn):]
        w_ref, m_ref, v_ref = refs[:3]
        g_refs = refs[3:3 + np_]
        g_out, d_out, m_out, v_out = refs[3 + np_:]
        g = g_refs[0][...].astype(F32)
        for gr in g_refs[1:]:
            g = g + gr[...].astype(F32)
        mn = ADAM_B1 * m_ref[...] + (1.0 - ADAM_B1) * g
        vn = ADAM_B2 * v_ref[...] + (1.0 - ADAM_B2) * (g * g)
        g_out[...] = g
        m_out[...] = mn
        v_out[...] = vn
        d_out[...] = -ADAM_LR * ((mn * bc1) / (jnp.sqrt(vn * bc2) + ADAM_EPS) + ADAM_WD * w_ref[...])

    blk, g_specs = _part_specs(gparts, part_index, tr, cc)
    grid_spec = pltpu.PrefetchScalarGridSpec(
        num_scalar_prefetch=len(dyn), grid=(r // tr,),
        in_specs=[blk, blk, blk] + g_specs, out_specs=[blk] * 4)
    return pl.pallas_call(
        body, name=name, grid_spec=grid_spec,
        out_shape=[jax.ShapeDtypeStruct((r, cc), F32)] * 4,
        compiler_params=_params(("parallel",)),
    )(*[part_index[i] for i in dyn], w, m, v, *gparts)


def _row(vec):
    return vec.reshape(1, -1)


def kernel(x, c, w_ada, b_ada, norm1_g, w_in, conv_w, conv_b, lru_wa, lru_ba, lru_wx, lru_bx, lru_lambda, w_lru_out, w_attn_out, attn_sinks, rel_bias, w_out, norm2_g, w_ff1, w_ff2, final_g, loss_target, m_w_ada, m_b_ada, m_norm1_g, m_w_in, m_conv_w, m_conv_b, m_lru_wa, m_lru_ba, m_lru_wx, m_lru_bx, m_lru_lambda, m_w_lru_out, m_w_attn_out, m_attn_sinks, m_rel_bias, m_w_out, m_norm2_g, m_w_ff1, m_w_ff2, m_final_g, v_w_ada, v_b_ada, v_norm1_g, v_w_in, v_conv_w, v_conv_b, v_lru_wa, v_lru_ba, v_lru_wx, v_lru_bx, v_lru_lambda, v_w_lru_out, v_w_attn_out, v_attn_sinks, v_rel_bias, v_w_out, v_norm2_g, v_w_ff1, v_w_ff2, v_final_g):
    dd = D_MODEL
    t = x.shape[1]
    ix, iy, ic = lax.axis_index("x"), lax.axis_index("y"), lax.axis_index("c")
    me = 4 * ix + 2 * iy + ic
    chip = 2 * ix + iy
    c_idx = jnp.reshape(ic, (1,)).astype(jnp.int32)
    chip_idx = jnp.reshape(chip, (1,)).astype(jnp.int32)

    xs = x[0]
    target = loss_target[0]
    ncol_ada = w_ada.shape[2]
    cshard = conv_w.shape[2]

    shards = [jnp.transpose(w_in[0]).astype(BF16), w_lru_out[0].astype(BF16), w_attn_out[0].astype(BF16),
              w_out[0].astype(BF16), jnp.transpose(w_ff1[0]).astype(BF16), w_ff2[0].astype(BF16)]
    s_in, s_lo, s_ao, s_o, s_f1, s_f2 = shards

    def full(g):
        return g.reshape(N_DEV * g.shape[1], dd)

    def stack(g):
        return g.reshape(N_DEV, g.shape[0] // N_DEV, dd)

    (w_in_pair,) = _run_job(_pair_job([s_in]), "ag_w_in_pair")
    w_in_pair = lax.dynamic_update_slice_in_dim(w_in_pair, s_in[None], ic, axis=0)

    pack0 = jnp.zeros((8, dd), F32).at[0:1, :].set(c).at[1:1 + CONV_WIDTH, 0:cshard].set(conv_w[0])
    (g0,) = _all_gather([pack0], "ag_cond")
    c_all = g0[:, 0, :]
    conv_w_full = jnp.transpose(g0[:, 1:1 + CONV_WIDTH, 0:cshard], (1, 0, 2)).reshape(CONV_WIDTH, dd)
    b_cols = lax.dynamic_slice(b_ada, (0, me * ncol_ada), (1, ncol_ada))
    mod_cols = _ada_fwd(c_all, w_ada[0], b_cols, "ada_fwd")
    (g1,) = _all_gather([mod_cols], "ag_mod")
    mod_mine = lax.dynamic_index_in_dim(g1, me, axis=1, keepdims=False).reshape(6, dd)
    mod = jnp.concatenate([mod_mine, jnp.zeros((2, dd), F32)], axis=0)

    bucket = _bucket_table()
    onehot = jnp.asarray((bucket.reshape(1, -1) == np.arange(N_BUCKETS)[:, None]).astype(np.float32))
    bias = _rel_bias_fwd(rel_bias, onehot, "rel_bias_fwd").reshape(N_Q_HEADS, BLOCK, 2 * BLOCK)
    sinks = attn_sinks.reshape(N_Q_HEADS)
    lp = jnp.concatenate([conv_b, lru_ba, lru_bx, lru_lambda, conv_w_full], axis=0)
    wa, wx = lru_wa[0], lru_wx[0]

    tm = _tile(t, 1024)
    tn = 512
    h = _norm_mod(xs, norm1_g, mod, 0, "norm1")
    tn2 = 1024
    chunk_w = 2 * s_in.shape[0]

    def ids(*ks):
        return jnp.stack([jnp.asarray(k, jnp.int32) for k in ks])

    tn_in = _tile(chunk_w, 1024, LANE)
    (proj,), ((g_near,),) = _mm(
        h, w_in_pair.reshape(chunk_w, dd), "nt", tm=tm, tn=tn_in, outs=[BF16], name="in_proj_own",
        chunks=(ids(chip), chunk_w, ids(0)), n_total=IN_WIDTH, jobs=[_gather_job([w_in_pair], (0.7,), "near")])
    near_ids = ids(2 * (1 - ix) + iy, 2 * ix + 1 - iy)
    (proj,), ((g_far,),) = _mm(
        h, full(g_near), "nt", tm=tm, tn=tn_in, outs=[BF16], name="in_proj_near",
        chunks=(near_ids, chunk_w, near_ids), into=proj, jobs=[_gather_job([g_near], (0.6,), "far")])
    far_chip = 2 * (1 - ix) + 1 - iy
    (proj,), ((g_o,),) = _mm(
        h, full(g_far), "nt", tm=tm, tn=tn_in, outs=[BF16], name="in_proj_far",
        chunks=(ids(far_chip), chunk_w, ids(far_chip)), into=proj, jobs=[_gather_job([s_o], (0.5, 0.8))])
    far_slots = lax.dynamic_slice_in_dim(g_far, 2 * far_chip, 2, axis=0)
    win_t = lax.dynamic_update_slice_in_dim(g_near, far_slots, 2 * far_chip, axis=0)
    win_t = full(lax.dynamic_update_slice_in_dim(win_t, w_in_pair, 2 * chip, axis=0))
    (g_o,) = _fill_own([g_o], [s_o])
    (att, lse), ((g_f1,),) = _attn_fwd(proj, bias, sinks, "attn_fwd", jobs=[_gather_job([s_f1], (0.6, 0.9))])
    (rec, ya), ((g_lo,), (g_ao,)) = _lru_fwd(proj, lp, wa, wx, "lru_fwd",
                                             jobs=[_gather_job([s_lo], (0.3, 0.45)), _gather_job([s_ao], (0.75, 0.9))])
    g_lo, g_ao, g_f1 = _fill_own([g_lo, g_ao, g_f1], [s_lo, s_ao, s_f1])
    wlo, wao, wo, wf1_t = full(g_lo), full(g_ao), full(g_o), full(g_f1)
    (y_a,) = _mm(ya, wlo, "nn", tm=tm, tn=tn2, outs=[BF16], name="lru_out")

    def merge_epi(acc, ya_t, ga_t, gb_t):
        return acc, _sigmoid(ga_t.astype(F32)) * ya_t.astype(F32) + _sigmoid(gb_t.astype(F32)) * acc

    y_b, merged = _mm(att, wao, "nn", tm=tm, tn=tn, outs=[BF16, BF16], epi=merge_epi,
                      tiles=[(y_a, 0), (proj, OFF_GA // tn), (proj, OFF_GB // tn)], name="attn_out_merge")

    def resid_epi(row):
        def epi(acc, x_t, mod_t):
            return acc, x_t + mod_t[row:row + 1, :] * acc
        return epi

    o1, x1 = _mm(merged, wo, "nn", tm=tm, tn=tn2, outs=[BF16, F32], epi=resid_epi(2),
                 tiles=[(xs, 0)], rows=[mod], name="out_proj")
    h2 = _norm_mod(x1, norm2_g, mod, 3, "norm2")

    def relu2_epi(acc):
        rl = jnp.maximum(acc, 0.0)
        return acc, rl * rl

    (f_pre, ff), ((g_f2,),) = _mm(h2, wf1_t, "nt", tm=tm, tn=1024, outs=[BF16, BF16], epi=relu2_epi, name="ff1",
                                  jobs=[_gather_job([s_f2], (0.6, 0.9))])
    wf2 = full(_fill_own([g_f2], [s_f2])[0])
    tmh = _tile(t, 512)
    o2, x2 = _mm(ff, wf2, "nn", tm=tmh, tn=tn, outs=[BF16, F32], epi=resid_epi(5),
                 tiles=[(x1, 0)], rows=[mod], name="ff2")

    loss_blk, dx2, do2, st_f = _final_loss(x2, o2, _row(final_g), target, mod, "final_loss")
    loss = lax.psum(loss_blk[0, 0], ("x", "y", "c"))

    def drelu2_epi(acc, f_t):
        return (acc * (2.0 * jnp.maximum(f_t.astype(F32), 0.0)),)

    tmw, tnw, tkw = 512, dd, _tile(t, 2048)
    (df,) = _mm(do2, wf2, "nt", tm=tm, tn=1024, outs=[BF16], epi=drelu2_epi, tiles=[(f_pre, 0)], name="ff2_dx")
    (dwf2,) = _mm(ff, do2, "tn", tm=tmw, tn=tnw, tk=tkw, outs=[BF16], name="ff2_dw")
    (dh2,), ((sib_f2,),) = _mm(df, wf1_t, "nn", tm=tmh, tn=tn, outs=[BF16], name="ff1_dx",
                              jobs=[_sibling_job([stack(dwf2)])])
    pair_f2 = _pair_add(stack(dwf2), sib_f2, c_idx, "rs_pair_add_w_ff2")
    (dwf1_t,), ((chips_f2,),) = _mm(df, h2, "tn", tm=tmw, tn=tnw, tk=tkw, outs=[BF16], name="ff1_dw",
                                    jobs=[_chips_job([pair_f2])])
    (dx1, st_2, do1), ((sib_f1,),) = _norm_mod_bwd(x1, norm2_g, mod, 3, dh2, dx2, (o1, 2), "norm2_bwd",
                                                   jobs=[_sibling_job([stack(dwf1_t)])])
    pair_f1 = _pair_add(stack(dwf1_t), sib_f1, c_idx, "rs_pair_add_w_ff1")

    def dmerge_epi(acc, ga_t, gb_t, ya_t, yb_t):
        sa, sb = _sigmoid(ga_t.astype(F32)), _sigmoid(gb_t.astype(F32))
        return (acc * sa, acc * sb, acc * ya_t.astype(F32) * sa * (1.0 - sa), acc * yb_t.astype(F32) * sb * (1.0 - sb))

    tmh = _tile(t, 512)
    dy_a, dy_b, dga, dgb = _mm(do1, wo, "nt", tm=tm, tn=tn, outs=[BF16] * 4, epi=dmerge_epi,
                               tiles=[(proj, OFF_GA // tn), (proj, OFF_GB // tn), (y_a, 0), (y_b, 0)], name="out_proj_dx")
    (dwo,) = _mm(merged, do1, "tn", tm=tmw, tn=tnw, tk=tkw, outs=[BF16], name="out_proj_dw")
    (datt,) = _mm(dy_b, wao, "nt", tm=tm, tn=tn2, outs=[BF16], name="attn_out_dx")
    (dwao,) = _mm(att, dy_b, "tn", tm=tmw, tn=tnw, tk=tkw, outs=[BF16], name="attn_out_dw")
    (dya,) = _mm(dy_a, wlo, "nt", tm=tm, tn=tn2, outs=[BF16], name="lru_out_dx")
    (dwlo,) = _mm(ya, dy_a, "tn", tm=tmw, tn=tnw, tk=tkw, outs=[BF16], name="lru_out_dw")
    (dq, dk_pad, dv_pad, dbias, dsinks), (sib_3, (chips_f1,)) = _attn_bwd(
        proj, datt, lse, bias, sinks, "attn_bwd",
        jobs=[_sibling_job([stack(dwlo), stack(dwao), stack(dwo)]), _chips_job([pair_f1])])
    pair_lo, pair_ao, pair_o = [_pair_add(stack(g_), s_, c_idx, "rs_pair_add_" + nm_)
                                for g_, s_, nm_ in zip((dwlo, dwao, dwo), sib_3, ("w_lru_out", "w_attn_out", "w_out"))]
    (dlx, dlg, st_l, dwa, dwx), ((chips_lo, chips_ao, chips_o),) = _lru_bwd(
        proj, rec, dya, lp, wa, wx, "lru_bwd", jobs=[_chips_job([pair_lo, pair_ao, pair_o])])
    dproj = jnp.concatenate([dlx, dlg, dq, dk_pad[BLOCK:].astype(BF16), dv_pad[BLOCK:].astype(BF16), dga, dgb], axis=1)
    lru_mats = jnp.concatenate([dwa.reshape(-1, dd), dwx.reshape(-1, dd)], axis=0)
    (dwin_t,), ((mats_all,),) = _mm(dproj, h, "tn", tm=768, tn=tnw, tk=tkw, outs=[BF16], name="in_proj_dw",
                                   jobs=[_gather_job([lru_mats], (0.3, 0.6))])
    (mats_all,) = _fill_own([mats_all], [lru_mats])
    (sib_in,) = _exchange_sibling([stack(dwin_t)], "rs_sibling_w_in")
    pair_in = _pair_add(stack(dwin_t), sib_in, c_idx, "rs_pair_add_w_in", widths=(dd // 2, dd // 2))
    n_rb = t // tm
    dx_tiles = dict(tm=tm, tn=tn2, tk=IN_WIDTH // 4, outs=[BF16])
    if n_rb >= 2:
        (dh,), ((chips_in_0,),) = _mm(dproj, win_t, "nn", name="in_proj_dx_top", rows_part=(0, n_rb // 2),
                                     jobs=[_chips_job(pair_in[0:1])], **dx_tiles)
        (dh,), ((chips_in_1,),) = _mm(dproj, win_t, "nn", name="in_proj_dx_bottom", into=dh,
                                     rows_part=(n_rb // 2, n_rb - n_rb // 2), jobs=[_chips_job(pair_in[1:2])], **dx_tiles)
    else:
        (dh,), ((chips_in_0,), (chips_in_1,)) = _mm(
            dproj, win_t, "nn", name="in_proj_dx", jobs=[_chips_job(pair_in[0:1]), _chips_job(pair_in[1:2])], **dx_tiles)
    grad_x, st_1 = _norm_mod_bwd(xs, norm1_g, mod, 0, dh, dx1, None, "norm1_bwd")
    chips_in = [chips_in_0, chips_in_1]

    drel = jnp.transpose(_rel_bias_bwd(dbias.reshape(N_Q_HEADS, -1), onehot, "rel_bias_bwd"))

    misc = jnp.concatenate([drel.reshape(1, -1), dsinks[0:1, 0:N_Q_HEADS],
                            jnp.zeros((1, dd - N_BUCKETS * N_Q_HEADS - N_Q_HEADS), F32)], axis=1)
    zero_row = jnp.zeros((1, dd), F32)
    small = jnp.concatenate([
        st_1[0:2], st_2[3:4], st_2[0:2], st_f[1:2], zero_row, zero_row,
        st_1[2:3], st_l[0:1], st_l[1:2], st_l[2:3], st_l[3:4], st_2[2:3], st_f[0:1], misc,
        st_l[4:8], jnp.zeros((4, dd), F32)], axis=0)
    (small_all,) = _all_gather([small], "ag_small")

    def pack_small(b_, n1, cb, ba, bx, lam, n2, fg, rb, sk):
        misc_ = jnp.concatenate([rb.reshape(1, -1), sk.reshape(1, -1),
                                 jnp.zeros((1, dd - N_BUCKETS * N_Q_HEADS - N_Q_HEADS), F32)], axis=1)
        return jnp.concatenate([b_.reshape(6, dd), jnp.zeros((2, dd), F32), n1, cb, ba, bx, lam, n2, _row(fg), misc_,
                                jnp.zeros((8, dd), F32)], axis=0)

    def pack_mats(wa_, wx_):
        return jnp.concatenate([wa_.reshape(-1, dd), wx_.reshape(-1, dd)], axis=0)

    every = {i: i for i in range(N_DEV)}
    w_s = pack_small(b_ada, norm1_g, conv_b, lru_ba, lru_bx, lru_lambda, norm2_g, final_g, rel_bias, attn_sinks)
    m_s = pack_small(m_b_ada, m_norm1_g, m_conv_b, m_lru_ba, m_lru_bx, m_lru_lambda, m_norm2_g, m_final_g, m_rel_bias, m_attn_sinks)
    v_s = pack_small(v_b_ada, v_norm1_g, v_conv_b, v_lru_ba, v_lru_bx, v_lru_lambda, v_norm2_g, v_final_g, v_rel_bias, v_attn_sinks)
    small_res = _adamw(w_s, m_s, v_s, [small_all] * N_DEV, "adamw_small", part_index=every)
    mats_res = _adamw(pack_mats(lru_wa, lru_wx), pack_mats(m_lru_wa, m_lru_wx), pack_mats(v_lru_wa, v_lru_wx),
                      [mats_all] * N_DEV, "adamw_lru_mats", part_index=every)

    def unpack_small(s, mt):
        nb_ = N_BUCKETS * N_Q_HEADS
        half = mt.shape[0] // 2
        return dict(
            b_ada=s[0:6].reshape(1, 6 * dd), norm1_g=s[8:9], conv_b=s[9:10], lru_ba=s[10:11], lru_bx=s[11:12],
            lru_lambda=s[12:13], norm2_g=s[13:14], final_g=s[14], rel_bias=s[15, 0:nb_].reshape(N_BUCKETS, N_Q_HEADS),
            attn_sinks=s[15:16, nb_:nb_ + N_Q_HEADS],
            lru_wa=mt[:half].reshape(1, LRU_BLOCKS, LRU_BLOCK_W, LRU_BLOCK_W),
            lru_wx=mt[half:].reshape(1, LRU_BLOCKS, LRU_BLOCK_W, LRU_BLOCK_W))

    res = {k: [None] * 4 for k in ("w_ada", "w_in", "conv_w", "w_lru_out", "w_attn_out", "w_out", "w_ff1", "w_ff2")}
    for q_, (s_, mt_) in enumerate(zip(small_res, mats_res)):
        for k_, val in unpack_small(s_, mt_).items():
            res.setdefault(k_, [None] * 4)[q_] = val

    g_conv = lax.dynamic_slice(small_res[0][16:16 + CONV_WIDTH], (0, me * cshard), (CONV_WIDTH, cshard))
    conv_res = _adamw(conv_w[0], m_conv_w[0], v_conv_w[0], [g_conv], "adamw_conv")
    res["conv_w"] = [r_[None] for r_ in conv_res]

    dmod_all = small_all[:, 0:6, :].reshape(N_DEV, 6 * dd)
    dmod_cols = lax.dynamic_slice(dmod_all, (0, me * ncol_ada), (N_DEV, ncol_ada))
    g_ada = _ada_bwd(jnp.transpose(c_all), dmod_cols, "ada_bwd")
    ada_res = _adamw(w_ada[0], m_w_ada[0], v_w_ada[0], [g_ada], "adamw_ada")
    res["w_ada"] = [r_[None] for r_ in ada_res]

    pair = [None, pair_lo, pair_ao, pair_o, pair_f1, pair_f2]
    from_chips = [None, chips_lo, chips_ao, chips_o, chips_f1, chips_f2]
    own_and_chips = {0: chip_idx, 1: 0, 2: 1, 3: 2}

    def summed(i):
        return [pair[i], from_chips[i], from_chips[i], from_chips[i]], {0: chip_idx, 1: 0, 2: 1, 3: 2}

    def sum_only(i, name):
        parts, index = summed(i)
        return _sum_parts(parts, index, name)

    g_in = jnp.transpose(jnp.concatenate(
        [_sum_parts([p_, c_, c_, c_], own_and_chips, "sum_w_in_%d" % q_)
         for q_, (p_, c_) in enumerate(zip(pair_in, chips_in))], axis=1))
    res["w_in"] = [r_[None] for r_ in _adamw(w_in[0], m_w_in[0], v_w_in[0], [g_in], "adamw_w_in")]
    g_ff1 = jnp.transpose(sum_only(4, "sum_w_ff1"))
    res["w_ff1"] = [r_[None] for r_ in _adamw(w_ff1[0], m_w_ff1[0], v_w_ff1[0], [g_ff1], "adamw_w_ff1")]
    for i, (nm, w_, m_, v_) in {1: ("w_lru_out", w_lru_out, m_w_lru_out, v_w_lru_out),
                                2: ("w_attn_out", w_attn_out, m_w_attn_out, v_w_attn_out),
                                3: ("w_out", w_out, m_w_out, v_w_out),
                                5: ("w_ff2", w_ff2, m_w_ff2, v_w_ff2)}.items():
        parts, index = summed(i)
        res[nm] = [r_[None] for r_ in _adamw(w_[0], m_[0], v_[0], parts, "adamw_" + nm, part_index=index)]

    order = ["w_ada", "b_ada", "norm1_g", "w_in", "conv_w", "conv_b", "lru_wa", "lru_ba", "lru_wx", "lru_bx",
             "lru_lambda", "w_lru_out", "w_attn_out", "attn_sinks", "rel_bias", "w_out", "norm2_g", "w_ff1",
             "w_ff2", "final_g"]
    out = [loss, grad_x[None]]
    for q_ in range(4):
        out += [res[k_][q_] for k_ in order]
    return tuple(out)
```

```python
import functools
import math

import numpy as np
import jax
import jax.numpy as jnp
from jax import lax
from jax.experimental import pallas as pl
from jax.experimental.pallas import tpu as pltpu

F32 = jnp.float32
BF16 = jnp.bfloat16
MESH = pl.DeviceIdType.MESH

D_MODEL = 2048
N_Q_HEADS = 32
N_KV_HEADS = 4
GROUP = N_Q_HEADS // N_KV_HEADS
HEAD_DIM = 64
KV_WIDTH = N_KV_HEADS * HEAD_DIM
BLOCK = 128
NEG_INF = -1e30
N_BUCKETS = 32
MAX_DISTANCE = 128
LRU_BLOCKS = 16
LRU_BLOCK_W = 128
CONV_WIDTH = 4
LRU_C = 8.0
D_FF = 4 * D_MODEL
EPS = 1e-6
IN_WIDTH = 5 * D_MODEL + 2 * KV_WIDTH
OFF_LX, OFF_LG, OFF_Q, OFF_K, OFF_V, OFF_GA, OFF_GB = 0, 2048, 4096, 6144, 6400, 6656, 8704

ADAM_LR, ADAM_B1, ADAM_B2, ADAM_EPS, ADAM_WD, ADAM_STEP = 0.001, 0.9, 0.999, 1e-08, 0.01, 10

N_DEV = 8
VMEM_LIMIT_BYTES = 48 * 1024 * 1024
LANE = 128
LRU_CW = 512


def _params(sem, **kw):
    return pltpu.CompilerParams(dimension_semantics=sem, vmem_limit_bytes=VMEM_LIMIT_BYTES, **kw)


def _tile(n, pref, mult=8):
    if n <= pref:
        return n
    t = (pref // mult) * mult
    while t >= mult:
        if n % t == 0:
            return t
        t -= mult
    return n


def _sigmoid(x):
    return 0.5 * jnp.tanh(0.5 * x) + 0.5


def _gelu_tanh(x):
    k = math.sqrt(2.0 / math.pi)
    t = jnp.tanh(k * (x + 0.044715 * x * x * x))
    return 0.5 * x * (1.0 + t), t


def _gelu_tanh_grad(x, t):
    k = math.sqrt(2.0 / math.pi)
    return 0.5 * (1.0 + t) + 0.5 * x * (1.0 - t * t) * k * (1.0 + 3.0 * 0.044715 * x * x)


def _log1p(e):
    u = 1.0 + e
    return jnp.where(u == 1.0, e, jnp.log(u) * (e / jnp.where(u == 1.0, 1.0, u - 1.0)))


def _softplus(x):
    return jnp.maximum(x, 0.0) + _log1p(jnp.exp(-jnp.abs(x)))


def _neg_expm1(x, exp_x):
    series = -x * (1.0 + x * (0.5 + x * (1.0 / 6.0 + x * (1.0 / 24.0))))
    return jnp.where(x > -0.0625, series, 1.0 - exp_x)


def _my_place():
    return lax.axis_index("x"), lax.axis_index("y"), lax.axis_index("c")


def _all_gather(arrs, name):
    return _fill_own(_run_job(_gather_job(arrs), name), arrs)


def _fill_own(stacks, shards):
    me = 4 * lax.axis_index("x") + 2 * lax.axis_index("y") + lax.axis_index("c")
    out = []
    for g, s in zip(stacks, shards):
        if g.size * g.dtype.itemsize <= (1 << 20):
            slot = lax.broadcasted_iota(jnp.int32, g.shape, 0)
            out.append(jnp.where(slot == me, s[None], g))
        else:
            out.append(lax.dynamic_update_slice_in_dim(g, s[None], me, axis=0))
    return out


class _Job:
    def __init__(self, ins, outs, sems, start, finish, forwards=(), forward_at=()):
        self.ins, self.outs, self.sems = list(ins), list(outs), list(sems)
        self.start, self.finish, self.forwards, self.forward_at = start, finish, list(forwards), list(forward_at)
        assert len(self.forwards) <= len(self.forward_at)


def _gather_job(arrs, forward_at=(0.6, 0.85), part="all"):
    n = len(arrs)
    whole, near, far = part == "all", part == "near", part == "far"

    def copies(ins, outs, sems):
        send_sems, recv_sems, local_sems = sems
        x, y, c = _my_place()
        sib = (x, y, 1 - c)
        xn, yn, dg = (1 - x, y), (x, 1 - y), (1 - x, 1 - y)

        def slot(px, py, pc):
            return 4 * px + 2 * py + pc

        def cp(a, k, src, dst_slot, to):
            return pltpu.make_async_remote_copy(
                src_ref=src, dst_ref=outs[a].at[dst_slot], send_sem=send_sems.at[a, k], recv_sem=recv_sems.at[a, k],
                device_id=to, device_id_type=MESH)

        def arrival(a, k, dst_slot):
            return cp(a, k, outs[a].at[dst_slot], dst_slot, sib)

        me_slot = slot(x, y, c)
        sends, local, relay, passes, arrive = {}, [], {}, {}, {}
        for a in range(n):
            if whole:
                own = ins[a]
                sends[a, 0] = cp(a, 0, own, me_slot, sib)
            if near:
                own = ins[a].at[c]
            if whole or near:
                sends[a, 1] = cp(a, 1, own, me_slot, (*xn, c))
                sends[a, 2] = cp(a, 2, own, me_slot, (*yn, c))
                passes[a, 4] = cp(a, 4, outs[a].at[slot(*xn, c)], slot(*xn, c), sib)
                passes[a, 5] = cp(a, 5, outs[a].at[slot(*yn, c)], slot(*yn, c), sib)
            if whole or far:
                source = outs[a] if whole else ins[a]
                relayed = c * slot(*yn, c) + (1 - c) * slot(*xn, c)
                relay[a, 3] = cp(a, 3, source.at[relayed], relayed, (x + c * (1 - 2 * x), y + (1 - c) * (1 - 2 * y), c))
                passes[a, 6] = cp(a, 6, outs[a].at[slot(*dg, c)], slot(*dg, c), sib)
            arrive[a, 0] = arrival(a, 0, slot(x, y, 1 - c))
            arrive[a, 1] = arrival(a, 1, slot(*xn, c))
            arrive[a, 2] = arrival(a, 2, slot(*yn, c))
            arrive[a, 3] = arrival(a, 3, slot(*dg, c))
            arrive[a, 4] = arrival(a, 4, slot(*xn, 1 - c))
            arrive[a, 5] = arrival(a, 5, slot(*yn, 1 - c))
            arrive[a, 6] = arrival(a, 6, slot(*dg, 1 - c))
        return sends, local, relay, passes, arrive

    def pick(d, ks):
        return [d[a, k] for a in range(n) for k in ks if (a, k) in d]

    def start(ins, outs, sems):
        sends, local, relay, _, _ = copies(ins, outs, sems)
        for cp in local + pick(sends, (0, 1, 2)) + (pick(relay, (3,)) if far else []):
            cp.start()

    def forward_neighbours(ins, outs, sems):
        _, _, relay, passes, arrive = copies(ins, outs, sems)
        for cp in pick(arrive, (1, 2)):
            cp.wait_recv()
        for cp in pick(relay, (3,)) + pick(passes, (4, 5)):
            cp.start()

    def forward_diagonal(ins, outs, sems):
        _, _, _, passes, arrive = copies(ins, outs, sems)
        for cp in pick(arrive, (3,)):
            cp.wait_recv()
        for cp in pick(passes, (6,)):
            cp.start()

    def finish(ins, outs, sems):
        sends, local, relay, passes, arrive = copies(ins, outs, sems)
        for cp in pick(arrive, (6,) if far else (0, 4, 5, 6) if whole else (4, 5)):
            cp.wait_recv()
        for cp in pick(sends, (0, 1, 2)) + pick(relay, (3,)) + pick(passes, (4, 5, 6)):
            cp.wait_send()
        for cp in local:
            cp.wait()

    shapes = [a.shape[-2:] for a in arrs]
    forwards = [forward_diagonal] if far else [forward_neighbours] if near else [forward_neighbours, forward_diagonal]
    return _Job(arrs, [jax.ShapeDtypeStruct((N_DEV,) + s, a.dtype) for s, a in zip(shapes, arrs)],
                [pltpu.SemaphoreType.DMA((n, 7)), pltpu.SemaphoreType.DMA((n, 7)), pltpu.SemaphoreType.DMA((n,))],
                start, finish, forwards, forward_at)


def _pair_job(arrs):
    n = len(arrs)

    def copies(ins, outs, sems):
        send_sems, recv_sems, local_sems = sems
        x, y, c = _my_place()
        remote = [pltpu.make_async_remote_copy(
            src_ref=ins[a], dst_ref=outs[a].at[c], send_sem=send_sems.at[a], recv_sem=recv_sems.at[a],
            device_id=(x, y, 1 - c), device_id_type=MESH) for a in range(n)]
        local = []
        arrive = [pltpu.make_async_remote_copy(
            src_ref=ins[a], dst_ref=outs[a].at[1 - c], send_sem=send_sems.at[a], recv_sem=recv_sems.at[a],
            device_id=(x, y, 1 - c), device_id_type=MESH) for a in range(n)]
        return remote, local, arrive

    def start(ins, outs, sems):
        remote, local, _ = copies(ins, outs, sems)
        for cp in remote + local:
            cp.start()

    def finish(ins, outs, sems):
        remote, local, arrive = copies(ins, outs, sems)
        for cp in arrive:
            cp.wait_recv()
        for cp in remote:
            cp.wait_send()
        for cp in local:
            cp.wait()

    return _Job(arrs, [jax.ShapeDtypeStruct((2,) + a.shape, a.dtype) for a in arrs],
                [pltpu.SemaphoreType.DMA((n,)), pltpu.SemaphoreType.DMA((n,)), pltpu.SemaphoreType.DMA((n,))],
                start, finish)


def _run_job(job, name):
    ni, no = len(job.ins), len(job.outs)

    def body(*refs):
        ins, outs, sems = refs[:ni], refs[ni:ni + no], refs[ni + no:]
        job.start(ins, outs, sems)
        for fwd in job.forwards:
            fwd(ins, outs, sems)
        job.finish(ins, outs, sems)

    any_spec = pl.BlockSpec(memory_space=pl.ANY)
    return pl.pallas_call(
        body, name=name, out_shape=job.outs, in_specs=[any_spec] * ni, out_specs=[any_spec] * no,
        scratch_shapes=job.sems,
    )(*job.ins)


def _call(body, *, name, grid, in_specs, out_specs, out_shape, scratch_shapes=(), sem, args, jobs=(),
          prefetch=(), aliases=None):
    in_specs, out_specs, out_shape, scratch_shapes = list(in_specs), list(out_specs), list(out_shape), list(scratch_shapes)
    prefetch = list(prefetch)
    if not jobs and not prefetch and not aliases:
        res = pl.pallas_call(body, name=name, grid=grid, in_specs=in_specs, out_specs=out_specs, out_shape=out_shape,
                             scratch_shapes=scratch_shapes, compiler_params=_params(sem))(*args)
        return list(res), []
    n_in, n_out, n_scr = len(in_specs), len(out_specs), len(scratch_shapes)
    j_in = [len(j.ins) for j in jobs]
    j_out = [len(j.outs) for j in jobs]
    j_sem = [len(j.sems) for j in jobs]
    total = int(np.prod(grid))
    any_spec = pl.BlockSpec(memory_space=pl.ANY)

    def wrapped(*refs):
        pos = [0]

        def take(k):
            part = refs[pos[0]:pos[0] + k]
            pos[0] += k
            return part

        take(len(prefetch))
        ins = take(n_in)
        jins = [take(k) for k in j_in]
        outs = take(n_out)
        jouts = [take(k) for k in j_out]
        scr = take(n_scr)
        jsems = [take(k) for k in j_sem]
        step = pl.program_id(0)
        for d in range(1, len(grid)):
            step = step * grid[d] + pl.program_id(d)
        for j, job in enumerate(jobs):
            pl.when(step == 0)(functools.partial(job.start, jins[j], jouts[j], jsems[j]))
            for fwd, frac in zip(job.forwards, job.forward_at):
                at = min(int(total * frac), total - 1)
                pl.when(step == at)(functools.partial(fwd, jins[j], jouts[j], jsems[j]))
        body(*ins, *outs, *scr)
        for j, job in enumerate(jobs):
            pl.when(step == total - 1)(functools.partial(job.finish, jins[j], jouts[j], jsems[j]))

    grid_spec = pltpu.PrefetchScalarGridSpec(
        num_scalar_prefetch=len(prefetch), grid=grid,
        in_specs=in_specs + [any_spec] * sum(j_in),
        out_specs=out_specs + [any_spec] * sum(j_out),
        scratch_shapes=scratch_shapes + [s for j in jobs for s in j.sems])
    res = pl.pallas_call(
        wrapped, name=name, grid_spec=grid_spec,
        out_shape=out_shape + [o for j in jobs for o in j.outs],
        input_output_aliases={len(prefetch) + i: o for i, o in (aliases or {}).items()},
        compiler_params=_params(("arbitrary",) * len(grid)),
    )(*prefetch, *args, *[a for j in jobs for a in j.ins])
    res = list(res)
    own, rest = res[:n_out], res[n_out:]
    per_job = []
    for k in j_out:
        per_job.append(rest[:k])
        rest = rest[k:]
    return own, per_job


def _exchange_sibling(arrs, name):
    return _run_job(_sibling_job(arrs), name)


def _sibling_job(arrs):
    n = len(arrs)

    def copies(ins, outs, sems):
        send_sems, recv_sems = sems
        x, y, c = _my_place()
        return [pltpu.make_async_remote_copy(
            src_ref=ins[a].at[2 * k + 1 - c], dst_ref=outs[a].at[k],
            send_sem=send_sems.at[a, k], recv_sem=recv_sems.at[a, k],
            device_id=(x, y, 1 - c), device_id_type=MESH) for a in range(n) for k in range(4)]

    def start(ins, outs, sems):
        for cp in copies(ins, outs, sems):
            cp.start()

    def finish(ins, outs, sems):
        for cp in copies(ins, outs, sems):
            cp.wait()

    return _Job(arrs, [jax.ShapeDtypeStruct((4,) + a.shape[1:], a.dtype) for a in arrs],
                [pltpu.SemaphoreType.DMA((n, 4)), pltpu.SemaphoreType.DMA((n, 4))], start, finish)


def _chips_job(arrs):
    n = len(arrs)

    def copies(ins, outs, sems):
        send_sems, recv_sems = sems
        x, y, c = _my_place()
        chips = [(1 - x, y), (x, 1 - y), (1 - x, 1 - y)]
        return [pltpu.make_async_remote_copy(
            src_ref=ins[a].at[2 * px + py], dst_ref=outs[a].at[j],
            send_sem=send_sems.at[a, j], recv_sem=recv_sems.at[a, j],
            device_id=(px, py, c), device_id_type=MESH) for a in range(n) for j, (px, py) in enumerate(chips)]

    def start(ins, outs, sems):
        for cp in copies(ins, outs, sems):
            cp.start()

    def finish(ins, outs, sems):
        for cp in copies(ins, outs, sems):
            cp.wait()

    return _Job(arrs, [jax.ShapeDtypeStruct((3,) + a.shape[1:], a.dtype) for a in arrs],
                [pltpu.SemaphoreType.DMA((n, 3)), pltpu.SemaphoreType.DMA((n, 3))], start, finish)


def _pair_add(stack, from_sibling, c_idx, name, widths=None):
    _, r, cc = stack.shape
    tr = _tile(r, 512)
    ws = [cc] if widths is None else list(widths)
    assert sum(ws) == cc
    starts = [sum(ws[:q]) for q in range(len(ws))]

    def body(c_ref, a_ref, b_ref, *o_refs):
        res = a_ref[...].astype(F32) + b_ref[...].astype(F32)
        for o_ref, s0, w_ in zip(o_refs, starts, ws):
            o_ref[...] = res[:, s0:s0 + w_].astype(o_ref.dtype)

    grid_spec = pltpu.PrefetchScalarGridSpec(
        num_scalar_prefetch=1, grid=(4, r // tr),
        in_specs=[pl.BlockSpec((None, tr, cc), lambda k, i, c_ref: (2 * k + c_ref[0], i, 0)),
                  pl.BlockSpec((None, tr, cc), lambda k, i, c_ref: (k, i, 0))],
        out_specs=[pl.BlockSpec((None, tr, w_), lambda k, i, c_ref: (k, i, 0)) for w_ in ws])
    res = pl.pallas_call(
        body, name=name, grid_spec=grid_spec,
        out_shape=[jax.ShapeDtypeStruct((4, r, w_), BF16) for w_ in ws],
        compiler_params=_params(("parallel", "parallel")),
    )(c_idx, stack, from_sibling)
    return res[0] if widths is None else list(res)


_DIMS = {"nn": ((1,), (0,)), "nt": ((1,), (1,)), "tn": ((0,), (0,))}


def _mm(a, b, mode, *, tm, tn, tk=None, outs, epi=None, tiles=(), rows=(), name, jobs=(), chunks=None, into=None,
        n_total=None, rows_part=None):
    if mode == "tn":
        kk, m = a.shape
    else:
        m, kk = a.shape
    n_all = b.shape[0] if mode == "nt" else b.shape[1]
    nn = n_all if chunks is None else chunks[0].shape[0] * chunks[1]
    n_all = n_all if n_total is None else n_total
    tk = kk if tk is None else tk
    assert m % tm == 0 and nn % tn == 0 and kk % tk == 0, (name, a.shape, b.shape, tm, tn, tk)
    nk = kk // tk
    nt, nr, no = len(tiles), len(rows), len(outs)
    n_into = 0 if into is None else 1

    def body(*refs):
        a_ref, b_ref = refs[:2]
        tile_refs = refs[2:2 + nt]
        row_refs = refs[2 + nt:2 + nt + nr]
        out_refs = refs[2 + nt + nr + n_into:2 + nt + nr + n_into + no]
        part = lax.dot_general(a_ref[...], b_ref[...], (_DIMS[mode], ((), ())), preferred_element_type=F32)

        def finish(acc):
            if epi is None:
                res = (acc,)
            else:
                res = epi(acc, *[t[...] for t in tile_refs], *[r[...] for r in row_refs])
            for o_ref, val in zip(out_refs, res):
                o_ref[...] = val.astype(o_ref.dtype)

        if nk == 1:
            finish(part)
        else:
            acc_ref = refs[-1]
            k = pl.program_id(2)

            @pl.when(k == 0)
            def _():
                acc_ref[...] = part

            @pl.when(k > 0)
            def _():
                acc_ref[...] += part

            @pl.when(k == nk - 1)
            def _():
                finish(acc_ref[...])

    if chunks is None:
        col = b_row = lambda j, *_: j
        prefetch = []
    else:
        assert mode == "nt" and not tiles and not rows and chunks[1] % tn == 0
        per = chunks[1] // tn
        col = lambda j, ids, b_ids: ids[j // per] * per + j % per
        b_row = lambda j, ids, b_ids: b_ids[j // per] * per + j % per
        prefetch = [chunks[0], chunks[2]]
    row0, n_row_blocks = (0, m // tm) if rows_part is None else rows_part
    assert rows_part is None or (mode != "tn" and not tiles)
    if mode == "tn":
        a_spec = pl.BlockSpec((tk, tm), lambda i, j, k, *_: (k, i))
    else:
        a_spec = pl.BlockSpec((tm, tk), lambda i, j, k, *_: (i + row0, k))
    if mode == "nt":
        b_spec = pl.BlockSpec((tn, tk), lambda i, j, k, *s: (b_row(j, *s), k))
    else:
        b_spec = pl.BlockSpec((tk, tn), lambda i, j, k, *_: (k, j))
    tile_specs = [pl.BlockSpec((tm, tn), functools.partial(lambda i, j, k, off: (i, j + off), off=off))
                  for _, off in tiles]
    row_specs = [pl.BlockSpec((r.shape[0], tn), lambda i, j, k: (0, j)) for r in rows]
    out_spec = pl.BlockSpec((tm, tn), lambda i, j, k, *s: (i + row0, col(j, *s)))
    args = [a, b, *[t for t, _ in tiles], *rows]
    in_specs = [a_spec, b_spec] + tile_specs + row_specs
    aliases = None
    if into is not None:
        aliases = {len(args): 0}
        args.append(into)
        in_specs.append(pl.BlockSpec(memory_space=pl.ANY))
    res, jres = _call(
        body, name=name, grid=(n_row_blocks, nn // tn, nk),
        in_specs=in_specs, out_specs=[out_spec] * no,
        out_shape=[jax.ShapeDtypeStruct((m, n_all), dt) for dt in outs],
        scratch_shapes=[pltpu.VMEM((tm, tn), F32)] if nk > 1 else [],
        sem=("parallel", "parallel", "arbitrary"),
        args=args, jobs=jobs, prefetch=prefetch, aliases=aliases)
    return (res, jres) if jobs else res


def _ada_fwd(c_all, w_ada, b_ada_cols, name):
    nb, dd = c_all.shape
    ncol = w_ada.shape[1]
    tn = _tile(ncol, 512, LANE)

    def body(c_ref, w_ref, b_ref, o_ref):
        cv = c_ref[...]
        act = (cv * _sigmoid(cv)).astype(BF16)
        o_ref[...] = jnp.dot(act, w_ref[...].astype(BF16), preferred_element_type=F32) + b_ref[...]

    return pl.pallas_call(
        body, name=name, grid=(ncol // tn,),
        in_specs=[pl.BlockSpec((nb, dd), lambda j: (0, 0)), pl.BlockSpec((dd, tn), lambda j: (0, j)),
                  pl.BlockSpec((1, tn), lambda j: (0, j))],
        out_specs=pl.BlockSpec((nb, tn), lambda j: (0, j)),
        out_shape=jax.ShapeDtypeStruct((nb, ncol), F32),
        compiler_params=_params(("parallel",)),
    )(c_all, w_ada, b_ada_cols)


def _ada_bwd(c_all_t, dmod_cols, name):
    dd, nb = c_all_t.shape
    ncol = dmod_cols.shape[1]
    tr = _tile(dd, 256)

    def body(c_ref, d_ref, o_ref):
        cv = c_ref[...]
        act = (cv * _sigmoid(cv)).astype(BF16).astype(F32)
        dm = d_ref[...].astype(BF16).astype(F32)
        acc = act[:, 0:1] * dm[0:1, :]
        for bi in range(1, nb):
            acc = acc + act[:, bi:bi + 1] * dm[bi:bi + 1, :]
        o_ref[...] = acc

    return pl.pallas_call(
        body, name=name, grid=(dd // tr,),
        in_specs=[pl.BlockSpec((tr, nb), lambda i: (i, 0)), pl.BlockSpec((nb, ncol), lambda i: (0, 0))],
        out_specs=pl.BlockSpec((tr, ncol), lambda i: (i, 0)),
        out_shape=jax.ShapeDtypeStruct((dd, ncol), F32),
        compiler_params=_params(("parallel",)),
    )(c_all_t, dmod_cols)


def _norm_mod(x, gain, mod, row0, name):
    t, dd = x.shape
    tt = _tile(t, 512)

    def body(x_ref, g_ref, m_ref, h_ref):
        xv = x_ref[...]
        r = lax.rsqrt(jnp.mean(xv * xv, axis=-1, keepdims=True) + EPS)
        shift, scale = m_ref[row0:row0 + 1, :], m_ref[row0 + 1:row0 + 2, :]
        h_ref[...] = ((xv * r * g_ref[...]) * (1.0 + scale) + shift).astype(BF16)

    return pl.pallas_call(
        body, name=name, grid=(t // tt,),
        in_specs=[pl.BlockSpec((tt, dd), lambda i: (i, 0)), pl.BlockSpec((1, dd), lambda i: (0, 0)),
                  pl.BlockSpec((8, dd), lambda i: (0, 0))],
        out_specs=pl.BlockSpec((tt, dd), lambda i: (i, 0)),
        out_shape=jax.ShapeDtypeStruct((t, dd), BF16),
        compiler_params=_params(("parallel",)),
    )(x, gain, mod)


def _norm_mod_bwd(x, gain, mod, row0, dh, dres, branch, name, jobs=()):
    t, dd = x.shape
    tt = _tile(t, 512)
    has_branch = branch is not None

    def body(*refs):
        if has_branch:
            x_ref, g_ref, m_ref, dh_ref, dres_ref, o_ref, dx_ref, st_ref, do_ref = refs[:-1]
        else:
            x_ref, g_ref, m_ref, dh_ref, dres_ref, dx_ref, st_ref = refs[:-1]

        acc_ref = refs[-1]
        step_i = pl.program_id(0)

        @pl.when(step_i == 0)
        def _():
            acc_ref[...] = jnp.zeros_like(acc_ref)

        def fold(v):
            out = v[0:8]
            for q in range(1, rc // 8):
                out = out + v[8 * q:8 * q + 8]
            return out

        def chunk(ci, carry):
            rows = pl.ds(pl.multiple_of(ci * rc, rc), rc)
            xv = x_ref[rows, :]
            g = g_ref[...]
            r = lax.rsqrt(jnp.mean(xv * xv, axis=-1, keepdims=True) + EPS)
            nv = xv * r
            dhv = dh_ref[rows, :].astype(F32)
            acc_ref[0] += fold(dhv)
            acc_ref[1] += fold(dhv * (nv * g))
            dng = dhv * (1.0 + m_ref[row0 + 1:row0 + 2, :])
            acc_ref[2] += fold(dng * nv)
            dn = dng * g
            dx = dres_ref[rows, :] + r * (dn - nv * jnp.mean(dn * nv, axis=-1, keepdims=True))
            dx_ref[rows, :] = dx
            if has_branch:
                acc_ref[3] += fold(dx * o_ref[rows, :].astype(F32))
                do_ref[rows, :] = (dx * m_ref[branch[1]:branch[1] + 1, :]).astype(BF16)
            return carry

        lax.fori_loop(0, tt // rc, chunk, 0, unroll=4)

        @pl.when(step_i == nsteps - 1)
        def _():
            st_ref[...] = jnp.zeros_like(st_ref)
            for k in range(4):
                st_ref[k:k + 1, :] = jnp.sum(acc_ref[k], axis=0, keepdims=True)

    rc = 16 if tt % 16 == 0 else 8
    nsteps = t // tt
    row = pl.BlockSpec((tt, dd), lambda i: (i, 0))
    in_specs = [row, pl.BlockSpec((1, dd), lambda i: (0, 0)), pl.BlockSpec((8, dd), lambda i: (0, 0)), row, row]
    args = [x, gain, mod, dh, dres]
    out_specs = [row, pl.BlockSpec((8, dd), lambda i: (0, 0))]
    out_shape = [jax.ShapeDtypeStruct((t, dd), F32), jax.ShapeDtypeStruct((8, dd), F32)]
    if has_branch:
        in_specs.append(row)
        args.append(branch[0])
        out_specs.append(row)
        out_shape.append(jax.ShapeDtypeStruct((t, dd), BF16))
    res, jres = _call(body, name=name, grid=(t // tt,), in_specs=in_specs, out_specs=out_specs, out_shape=out_shape,
                      scratch_shapes=[pltpu.VMEM((4, 8, dd), F32)], sem=("arbitrary",), args=args, jobs=jobs)
    return (res, jres) if jobs else res


def _final_loss(x2, o2, final_g, target, mod, name):
    t, dd = x2.shape
    tt = _tile(t, 256)
    nsteps = t // tt

    def body(x_ref, o_ref, g_ref, t_ref, m_ref, loss_ref, dx_ref, do_ref, st_ref, lacc):
        i = pl.program_id(0)

        @pl.when(i == 0)
        def _():
            st_ref[...] = jnp.zeros_like(st_ref)
            lacc[...] = jnp.zeros_like(lacc)

        xv = x_ref[...]
        g = g_ref[...]
        r = lax.rsqrt(jnp.mean(xv * xv, axis=-1, keepdims=True) + EPS)
        nv = xv * r
        err = nv * g - t_ref[...]
        lacc[...] += jnp.sum(err * err, axis=0, keepdims=True)
        dy = err * (1.0 / dd)
        st_ref[0:1, :] += jnp.sum(dy * nv, axis=0, keepdims=True)
        dn = dy * g
        dx = r * (dn - nv * jnp.mean(dn * nv, axis=-1, keepdims=True))
        dx_ref[...] = dx
        st_ref[1:2, :] += jnp.sum(dx * o_ref[...].astype(F32), axis=0, keepdims=True)
        do_ref[...] = (dx * m_ref[5:6, :]).astype(BF16)

        @pl.when(i == nsteps - 1)
        def _():
            total = jnp.sum(lacc[...], axis=-1, keepdims=True) * (0.5 / dd)
            loss_ref[...] = jnp.broadcast_to(total, loss_ref.shape)

    row = pl.BlockSpec((tt, dd), lambda i: (i, 0))
    return pl.pallas_call(
        body, name=name, grid=(nsteps,),
        in_specs=[row, row, pl.BlockSpec((1, dd), lambda i: (0, 0)), row, pl.BlockSpec((8, dd), lambda i: (0, 0))],
        out_specs=[pl.BlockSpec((8, LANE), lambda i: (0, 0)), row, row, pl.BlockSpec((8, dd), lambda i: (0, 0))],
        out_shape=[jax.ShapeDtypeStruct((8, LANE), F32), jax.ShapeDtypeStruct((t, dd), F32),
                   jax.ShapeDtypeStruct((t, dd), BF16), jax.ShapeDtypeStruct((8, dd), F32)],
        scratch_shapes=[pltpu.VMEM((1, dd), F32)],
        compiler_params=_params(("arbitrary",)),
    )(x2, o2, final_g, target, mod)


def _lru_gates(xt, prev8, lp, wa_ref, wx_ref, first_tile, cw):
    tt = xt.shape[0]
    ext = jnp.concatenate([prev8, xt], axis=0)
    xs = [pltpu.roll(ext, s, 0)[8:, :] if s else xt for s in range(CONV_WIDTH)]
    xc = lp[0:1, :] + lp[7:8, :] * xs[0] + lp[6:7, :] * xs[1] + lp[5:6, :] * xs[2] + lp[4:5, :] * xs[3]
    xcb = xc.astype(BF16)
    za, zi = [], []
    for jb in range(cw // LRU_BLOCK_W):
        sl = slice(jb * LRU_BLOCK_W, (jb + 1) * LRU_BLOCK_W)
        za.append(jnp.dot(xcb[:, sl], wa_ref[jb].astype(BF16), preferred_element_type=F32))
        zi.append(jnp.dot(xcb[:, sl], wx_ref[jb].astype(BF16), preferred_element_type=F32))
    ra = _sigmoid(jnp.concatenate(za, axis=1) + lp[1:2, :])
    ri = _sigmoid(jnp.concatenate(zi, axis=1) + lp[2:3, :])
    sp = _softplus(-lp[3:4, :])
    log_a = -LRU_C * ra * sp
    av = jnp.exp(log_a)
    seq_start = jnp.logical_and(first_tile, lax.broadcasted_iota(jnp.int32, (tt, 1), 0) == 0)
    m2 = _neg_expm1(2.0 * log_a, av * av)
    inv_mult = jnp.where(seq_start, 1.0, lax.rsqrt(jnp.maximum(m2, 1e-30)))
    mult = jnp.where(seq_start, 1.0, m2 * inv_mult)
    return xs, xc, xcb, ra, ri, sp, av, mult, inv_mult, seq_start


def _scan_groups(a, u, reverse):
    tt, cw = a.shape
    a3, u3 = a.reshape(tt // 8, 8, cw), u.reshape(tt // 8, 8, cw)
    r = lax.broadcasted_iota(jnp.int32, (1, 8, 1), 1)
    for s in (1, 2, 4):
        shift = 8 - s if reverse else s
        take = r < 8 - s if reverse else r >= s
        u3 = jnp.where(take, a3 * pltpu.roll(u3, shift, 1) + u3, u3)
        a3 = jnp.where(take, a3 * pltpu.roll(a3, shift, 1), a3)
    return a3, u3


def _scan_carry(a_s, u_s, h_s, carry, reverse):
    ng = a_s.shape[0]

    def step(i, h):
        g = ng - 1 - i if reverse else i
        hg = u_s[g] + a_s[g] * h
        h_s[g] = hg
        return hg[0:1, :] if reverse else hg[7:8, :]

    return lax.fori_loop(0, ng, step, carry, unroll=4)


def _lru_fwd(proj, lp, wa, wx, name, jobs=()):
    t = proj.shape[0]
    w = D_MODEL
    cw = LRU_CW
    tt = _tile(t, 256)
    nb = cw // LRU_BLOCK_W

    def body(lx_ref, lg_ref, lp_ref, wa_ref, wx_ref, rec_ref, ya_ref, a_s, u_s, h_s, halo, carry):
        ti = pl.program_id(1)

        @pl.when(ti == 0)
        def _():
            halo[...] = jnp.zeros_like(halo)
            carry[...] = jnp.zeros_like(carry)

        xt = lx_ref[...].astype(F32)
        lp_v = lp_ref[...]
        _, xc, _, _, ri, _, av, mult, _, _ = _lru_gates(xt, halo[...], lp_v, wa_ref, wx_ref, ti == 0, cw)
        halo[...] = xt[tt - 8:, :]
        a_s[...], u_s[...] = _scan_groups(av, mult * (ri * xc), False)
        carry[...] = _scan_carry(a_s, u_s, h_s, carry[...], False)
        rec = h_s[...].reshape(tt, cw)
        rec_ref[...] = rec.astype(BF16)
        gelu, _ = _gelu_tanh(lg_ref[...].astype(F32))
        ya_ref[...] = (rec * gelu).astype(BF16)

    off_g = OFF_LG // cw
    res, jres = _call(
        body, name=name, grid=(w // cw, t // tt),
        in_specs=[pl.BlockSpec((tt, cw), lambda ci, ti: (ti, ci)),
                  pl.BlockSpec((tt, cw), lambda ci, ti: (ti, off_g + ci)),
                  pl.BlockSpec((8, cw), lambda ci, ti: (0, ci)),
                  pl.BlockSpec((nb, LRU_BLOCK_W, LRU_BLOCK_W), lambda ci, ti: (ci, 0, 0)),
                  pl.BlockSpec((nb, LRU_BLOCK_W, LRU_BLOCK_W), lambda ci, ti: (ci, 0, 0))],
        out_specs=[pl.BlockSpec((tt, cw), lambda ci, ti: (ti, ci))] * 2,
        out_shape=[jax.ShapeDtypeStruct((t, w), BF16)] * 2,
        scratch_shapes=[pltpu.VMEM((tt // 8, 8, cw), F32)] * 3 + [pltpu.VMEM((8, cw), F32), pltpu.VMEM((1, cw), F32)],
        sem=("parallel", "arbitrary"), args=[proj, proj, lp, wa, wx], jobs=jobs)
    return (res, jres) if jobs else res


def _lru_bwd(proj, rec, dya, lp, wa, wx, name, jobs=()):
    t = proj.shape[0]
    w = D_MODEL
    cw = LRU_CW
    tt = _tile(t, 256)
    nt = t // tt
    nb = cw // LRU_BLOCK_W
    r8 = tt // 8

    def body(lx_ref, lxp_ref, lg_ref, rec_ref, recp_ref, dya_ref, lp_ref, wa_ref, wx_ref,
             dlx_ref, dlg_ref, st_ref, dwa_ref, dwx_ref, a_s, d_s, dh_s, nhalo, carry):
        step_i = pl.program_id(1)
        ti = nt - 1 - step_i

        @pl.when(step_i == 0)
        def _():
            st_ref[...] = jnp.zeros_like(st_ref)
            dwa_ref[...] = jnp.zeros_like(dwa_ref)
            dwx_ref[...] = jnp.zeros_like(dwx_ref)
            nhalo[...] = jnp.zeros_like(nhalo)
            carry[...] = jnp.zeros_like(carry)

        first = ti == 0
        keep = jnp.where(first, 0.0, 1.0)
        xt = lx_ref[...].astype(F32)
        prev8 = lxp_ref[...].astype(F32) * keep
        lp_v = lp_ref[...]
        xs, xc, xcb, ra, ri, sp, av, mult, inv_mult, seq_start = _lru_gates(xt, prev8, lp_v, wa_ref, wx_ref, first, cw)

        lg = lg_ref[...].astype(F32)
        gelu, th = _gelu_tanh(lg)
        dyav = dya_ref[...].astype(F32)
        recv = rec_ref[...].astype(F32)
        dlg_ref[...] = (dyav * recv * _gelu_tanh_grad(lg, th)).astype(BF16)

        drec = dyav * gelu
        e_in = carry[...]
        a_s[...], d_s[...] = _scan_groups(av, av * drec, True)
        carry[...] = _scan_carry(a_s, d_s, dh_s, e_in, True)
        e_next = jnp.concatenate([dh_s[...].reshape(tt, cw), jnp.broadcast_to(e_in, (8, cw))], axis=0)
        dh = drec + pltpu.roll(e_next, tt + 7, 0)[:tt, :]
        rprev8 = recp_ref[...].astype(F32) * keep
        hprev = pltpu.roll(jnp.concatenate([rprev8, recv], axis=0), 1, 0)[8:, :]
        da = dh * hprev
        dmult = jnp.where(seq_start, 0.0, dh * ri * xc)
        dri = dh * mult * xc
        dxc = dh * mult * ri
        dlog_a = da * av - dmult * (av * av) * inv_mult
        dra = dlog_a * (-LRU_C * sp)
        lam = lp_v[3:4, :]
        st_ref[3:4, :] += jnp.sum(dlog_a * ra, axis=0, keepdims=True) * (LRU_C * _sigmoid(-lam))
        dza = dra * ra * (1.0 - ra)
        dzi = dri * ri * (1.0 - ri)
        st_ref[1:2, :] += jnp.sum(dza, axis=0, keepdims=True)
        st_ref[2:3, :] += jnp.sum(dzi, axis=0, keepdims=True)
        dzab, dzib = dza.astype(BF16), dzi.astype(BF16)
        back = []
        for jb in range(nb):
            sl = slice(jb * LRU_BLOCK_W, (jb + 1) * LRU_BLOCK_W)
            dwa_ref[jb] += lax.dot_general(xcb[:, sl], dzab[:, sl], (_DIMS["tn"], ((), ())), preferred_element_type=F32)
            dwx_ref[jb] += lax.dot_general(xcb[:, sl], dzib[:, sl], (_DIMS["tn"], ((), ())), preferred_element_type=F32)
            back.append(
                lax.dot_general(dzab[:, sl], wa_ref[jb].astype(BF16), (_DIMS["nt"], ((), ())), preferred_element_type=F32)
                + lax.dot_general(dzib[:, sl], wx_ref[jb].astype(BF16), (_DIMS["nt"], ((), ())), preferred_element_type=F32))
        dxc = dxc + jnp.concatenate(back, axis=1)
        st_ref[0:1, :] += jnp.sum(dxc, axis=0, keepdims=True)
        for k in range(CONV_WIDTH):
            st_ref[4 + k:5 + k, :] += jnp.sum(dxc * xs[CONV_WIDTH - 1 - k], axis=0, keepdims=True)
        ext = jnp.concatenate([dxc, nhalo[...]], axis=0)
        dlx = lp_v[7:8, :] * dxc
        for s in range(1, CONV_WIDTH):
            dlx = dlx + lp_v[7 - s:8 - s, :] * pltpu.roll(ext, tt + 8 - s, 0)[:tt, :]
        dlx_ref[...] = dlx.astype(BF16)
        nhalo[...] = dxc[0:8, :]

    off_g = OFF_LG // cw
    tile = lambda off: pl.BlockSpec((tt, cw), lambda ci, si: (nt - 1 - si, off + ci))
    prev = lambda off: pl.BlockSpec((8, cw), lambda ci, si: (jnp.maximum((nt - 1 - si) * r8 - 1, 0), off + ci))
    wspec = pl.BlockSpec((nb, LRU_BLOCK_W, LRU_BLOCK_W), lambda ci, si: (ci, 0, 0))
    st_spec = pl.BlockSpec((8, cw), lambda ci, si: (0, ci))
    res, jres = _call(
        body, name=name, grid=(w // cw, nt),
        in_specs=[tile(0), prev(0), tile(off_g), tile(0), prev(0), tile(0), st_spec, wspec, wspec],
        out_specs=[tile(0), tile(0), st_spec, wspec, wspec],
        out_shape=[jax.ShapeDtypeStruct((t, w), BF16), jax.ShapeDtypeStruct((t, w), BF16),
                   jax.ShapeDtypeStruct((8, w), F32),
                   jax.ShapeDtypeStruct((LRU_BLOCKS, LRU_BLOCK_W, LRU_BLOCK_W), F32),
                   jax.ShapeDtypeStruct((LRU_BLOCKS, LRU_BLOCK_W, LRU_BLOCK_W), F32)],
        scratch_shapes=[pltpu.VMEM((tt // 8, 8, cw), F32)] * 3 + [pltpu.VMEM((8, cw), F32), pltpu.VMEM((1, cw), F32)],
        sem=("parallel", "arbitrary"), args=[proj, proj, proj, rec, rec, dya, lp, wa, wx], jobs=jobs)
    return (res, jres) if jobs else res


def _band_valid(first_block):
    qi = lax.broadcasted_iota(jnp.int32, (BLOCK, 2 * BLOCK), 0)
    ki = lax.broadcasted_iota(jnp.int32, (BLOCK, 2 * BLOCK), 1)
    rel = qi + BLOCK - ki
    valid = jnp.logical_and(rel >= 0, rel < BLOCK)
    return jnp.logical_and(valid, jnp.logical_or(ki >= BLOCK, jnp.logical_not(first_block)))


ATTN_STACK_BWD = 8


def _low_half():
    return lax.broadcasted_iota(jnp.int32, (1, LANE), 1) < HEAD_DIM


def _stack_heads(x, h0, ns):
    low = _low_half()
    parts = []
    for g in range(ns):
        h = h0 + g
        slab = x[:, (h // 2) * LANE:(h // 2 + 1) * LANE]
        parts.append(jnp.where(low if h % 2 == 0 else jnp.logical_not(low), slab, jnp.zeros_like(slab)))
    return jnp.concatenate(parts, axis=0)


def _unstack_heads(y, ns):
    low = _low_half()
    pairs = [jnp.where(low, y[(2 * j) * BLOCK:(2 * j + 1) * BLOCK], y[(2 * j + 1) * BLOCK:(2 * j + 2) * BLOCK])
             for j in range(ns // 2)]
    return pairs[0] if len(pairs) == 1 else jnp.concatenate(pairs, axis=1)


def _dup_kv(xall, kv):
    part = xall[:, kv * HEAD_DIM:(kv + 1) * HEAD_DIM]
    return jnp.concatenate([part, part], axis=1)


def _fold_halves(a):
    return a + pltpu.roll(a, HEAD_DIM, 1)


def _group_probs(qg, k2, bias_g, sink_g, lse_g, valid):
    s = lax.dot_general(qg, k2, (_DIMS["nt"], ((), ())), preferred_element_type=F32) * (HEAD_DIM ** -0.5)
    s = jnp.where(valid[None], s.reshape(bias_g.shape) + bias_g, NEG_INF)
    return jnp.exp(s - lse_g), jnp.exp(sink_g - lse_g)


def _group_sinks(s_ref, h0, ns):
    return jnp.concatenate([jnp.full((1, BLOCK, 1), s_ref[h0 + g], F32) for g in range(ns)], axis=0)


def _attn_specs(t):
    nblk = t // BLOCK
    q_spec = pl.BlockSpec((BLOCK, D_MODEL), lambda n: (n, OFF_Q // D_MODEL))
    kc = pl.BlockSpec((BLOCK, KV_WIDTH), lambda n: (n, OFF_K // KV_WIDTH))
    kp = pl.BlockSpec((BLOCK, KV_WIDTH), lambda n: (jnp.maximum(n - 1, 0), OFF_K // KV_WIDTH))
    vc = pl.BlockSpec((BLOCK, KV_WIDTH), lambda n: (n, OFF_V // KV_WIDTH))
    vp = pl.BlockSpec((BLOCK, KV_WIDTH), lambda n: (jnp.maximum(n - 1, 0), OFF_V // KV_WIDTH))
    bias_spec = pl.BlockSpec((N_Q_HEADS, BLOCK, 2 * BLOCK), lambda n: (0, 0, 0))
    sink_spec = pl.BlockSpec(memory_space=pltpu.SMEM)
    return nblk, [q_spec, kp, kc, vp, vc, bias_spec, sink_spec]


def _attn_fwd(proj, bias, sinks, name, jobs=()):
    t = proj.shape[0]
    nblk, in_specs = _attn_specs(t)

    def body(q_ref, kp_ref, kc_ref, vp_ref, vc_ref, b_ref, s_ref, o_ref, lse_ref):
        valid = _band_valid(pl.program_id(0) == 0)
        lane = lax.broadcasted_iota(jnp.int32, (1, LANE), 1)
        q = q_ref[...]
        kall = jnp.concatenate([kp_ref[...], kc_ref[...]], axis=0)
        vall = jnp.concatenate([vp_ref[...], vc_ref[...]], axis=0)
        outs = []
        lse = jnp.zeros((BLOCK, LANE), F32)
        for h in range(N_Q_HEADS):
            kv = h // GROUP
            kk = kall[:, kv * HEAD_DIM:(kv + 1) * HEAD_DIM]
            vv = vall[:, kv * HEAD_DIM:(kv + 1) * HEAD_DIM]
            s = lax.dot_general(q[:, h * HEAD_DIM:(h + 1) * HEAD_DIM], kk, (_DIMS["nt"], ((), ())),
                                preferred_element_type=F32) * (HEAD_DIM ** -0.5)
            s = jnp.where(valid, s + b_ref[h], NEG_INF)
            sink = s_ref[h]
            m = jnp.maximum(jnp.max(s, axis=-1, keepdims=True), sink)
            e = jnp.exp(s - m)
            den = jnp.sum(e, axis=-1, keepdims=True) + jnp.exp(sink - m)
            lse = jnp.where(lane == h, m + jnp.log(den), lse)
            outs.append(jnp.dot((e * (1.0 / den)).astype(BF16), vv, preferred_element_type=F32))
        o_ref[...] = jnp.concatenate(outs, axis=1).astype(BF16)
        lse_ref[...] = lse

    res, jres = _call(
        body, name=name, grid=(nblk,), in_specs=in_specs,
        out_specs=[pl.BlockSpec((BLOCK, D_MODEL), lambda n: (n, 0)), pl.BlockSpec((BLOCK, LANE), lambda n: (n, 0))],
        out_shape=[jax.ShapeDtypeStruct((t, D_MODEL), BF16), jax.ShapeDtypeStruct((t, LANE), F32)],
        sem=("parallel",), args=[proj, proj, proj, proj, proj, bias, sinks], jobs=jobs)
    return (res, jres) if jobs else res


def _attn_bwd(proj, datt, lse, bias, sinks, name, jobs=()):
    t = proj.shape[0]
    nblk, in_specs = _attn_specs(t)
    in_specs = in_specs + [pl.BlockSpec((BLOCK, D_MODEL), lambda n: (n, 0)), pl.BlockSpec((BLOCK, LANE), lambda n: (n, 0))]
    scale = HEAD_DIM ** -0.5

    def body(q_ref, kp_ref, kc_ref, vp_ref, vc_ref, b_ref, s_ref, do_ref, lse_ref,
             dq_ref, dk_ref, dv_ref, db_ref, ds_ref):
        n = pl.program_id(0)

        @pl.when(n == 0)
        def _():
            dk_ref[...] = jnp.zeros_like(dk_ref)
            dv_ref[...] = jnp.zeros_like(dv_ref)
            db_ref[...] = jnp.zeros_like(db_ref)
            ds_ref[...] = jnp.zeros_like(ds_ref)

        valid = _band_valid(n == 0)
        lane = lax.broadcasted_iota(jnp.int32, (1, LANE), 1)
        q = q_ref[...]
        do = do_ref[...]
        lse_all = lse_ref[...]
        kall = jnp.concatenate([kp_ref[...], kc_ref[...]], axis=0)
        vall = jnp.concatenate([vp_ref[...], vc_ref[...]], axis=0)
        low = _low_half()
        dqs, dks, dvs = [], [], []
        dsink_row = jnp.zeros((1, LANE), F32)
        ns = ATTN_STACK_BWD
        for kv in range(N_KV_HEADS):
            k2, v2 = _dup_kv(kall, kv), _dup_kv(vall, kv)
            dk_acc = jnp.zeros((2 * BLOCK, LANE), F32)
            dv_acc = jnp.zeros((2 * BLOCK, LANE), F32)
            for h0 in range(kv * GROUP, (kv + 1) * GROUP, ns):
                qg, dog = _stack_heads(q, h0, ns), _stack_heads(do, h0, ns)
                lse_g = jnp.concatenate([lse_all[:, h0 + g:h0 + g + 1][None] for g in range(ns)], axis=0)
                p, psink = _group_probs(qg, k2, b_ref[h0:h0 + ns], _group_sinks(s_ref, h0, ns), lse_g, valid)
                dp = lax.dot_general(dog, v2, (_DIMS["nt"], ((), ())), preferred_element_type=F32)
                dp = dp.reshape(ns, BLOCK, 2 * BLOCK)
                delta = jnp.sum(p * dp, axis=-1, keepdims=True)
                ds = p * (dp - delta)
                db_ref[h0:h0 + ns] += ds
                dsink = -jnp.sum(psink * delta, axis=1, keepdims=True)
                for g in range(ns):
                    dsink_row = dsink_row + jnp.where(lane == h0 + g, dsink[g], 0.0)
                dsb = ds.reshape(ns * BLOCK, 2 * BLOCK).astype(BF16)
                pb = p.reshape(ns * BLOCK, 2 * BLOCK).astype(BF16)
                dqs.append(_unstack_heads(jnp.dot(dsb, k2, preferred_element_type=F32) * scale, ns))
                dk_acc = dk_acc + lax.dot_general(dsb, qg, (_DIMS["tn"], ((), ())), preferred_element_type=F32)
                dv_acc = dv_acc + lax.dot_general(pb, dog, (_DIMS["tn"], ((), ())), preferred_element_type=F32)
            dks.append(_fold_halves(dk_acc) * scale)
            dvs.append(_fold_halves(dv_acc))
        dq_ref[...] = jnp.concatenate(dqs, axis=1).astype(BF16)
        rows = pl.ds(pl.multiple_of(n * BLOCK, BLOCK), 2 * BLOCK)
        dk_ref[rows, :] += jnp.concatenate([jnp.where(low, dks[0], dks[1]), jnp.where(low, dks[2], dks[3])], axis=1)
        dv_ref[rows, :] += jnp.concatenate([jnp.where(low, dvs[0], dvs[1]), jnp.where(low, dvs[2], dvs[3])], axis=1)
        ds_ref[0:1, :] += dsink_row

    res, jres = _call(
        body, name=name, grid=(nblk,), in_specs=in_specs,
        out_specs=[pl.BlockSpec((BLOCK, D_MODEL), lambda n: (n, 0)),
                   pl.BlockSpec((t + BLOCK, KV_WIDTH), lambda n: (0, 0)),
                   pl.BlockSpec((t + BLOCK, KV_WIDTH), lambda n: (0, 0)),
                   pl.BlockSpec((N_Q_HEADS, BLOCK, 2 * BLOCK), lambda n: (0, 0, 0)),
                   pl.BlockSpec((8, LANE), lambda n: (0, 0))],
        out_shape=[jax.ShapeDtypeStruct((t, D_MODEL), BF16),
                   jax.ShapeDtypeStruct((t + BLOCK, KV_WIDTH), F32),
                   jax.ShapeDtypeStruct((t + BLOCK, KV_WIDTH), F32),
                   jax.ShapeDtypeStruct((N_Q_HEADS, BLOCK, 2 * BLOCK), F32),
                   jax.ShapeDtypeStruct((8, LANE), F32)],
        sem=("arbitrary",), args=[proj, proj, proj, proj, proj, bias, sinks, datt, lse], jobs=jobs)
    return (res, jres) if jobs else res


def _bucket_table():
    qi = np.arange(BLOCK)[:, None]
    ki = np.arange(2 * BLOCK)[None, :]
    rel = np.maximum(qi + BLOCK - ki, 0)
    max_exact = N_BUCKETS // 2
    relf = np.maximum(rel, 1).astype(np.float32)
    large = max_exact + (np.log(relf / max_exact) / math.log(MAX_DISTANCE / max_exact)
                         * (N_BUCKETS - max_exact)).astype(np.int32)
    large = np.minimum(large, N_BUCKETS - 1)
    return np.where(rel < max_exact, rel, large).astype(np.int32)


def _rel_bias_fwd(rel_bias, onehot, name):
    n = onehot.shape[1]
    tn = _tile(n, 8192, LANE)

    def body(r_ref, o_ref, out_ref):
        out_ref[...] = lax.dot_general(r_ref[...], o_ref[...], (_DIMS["tn"], ((), ())),
                                       precision=lax.Precision.HIGHEST, preferred_element_type=F32)

    return pl.pallas_call(
        body, name=name, grid=(n // tn,),
        in_specs=[pl.BlockSpec(rel_bias.shape, lambda j: (0, 0)), pl.BlockSpec((N_BUCKETS, tn), lambda j: (0, j))],
        out_specs=pl.BlockSpec((N_Q_HEADS, tn), lambda j: (0, j)),
        out_shape=jax.ShapeDtypeStruct((N_Q_HEADS, n), F32),
        compiler_params=_params(("parallel",)),
    )(rel_bias, onehot)


def _rel_bias_bwd(dbias, onehot, name):
    def body(d_ref, o_ref, out_ref):
        out_ref[...] = lax.dot_general(d_ref[...], o_ref[...], (_DIMS["nt"], ((), ())),
                                       precision=lax.Precision.HIGHEST, preferred_element_type=F32)

    full = pl.BlockSpec(dbias.shape, lambda: (0, 0))
    return pl.pallas_call(
        body, name=name, in_specs=[full, full],
        out_specs=pl.BlockSpec((N_Q_HEADS, N_BUCKETS), lambda: (0, 0)),
        out_shape=jax.ShapeDtypeStruct((N_Q_HEADS, N_BUCKETS), F32),
        compiler_params=_params(()),
    )(dbias, onehot)


def _part_specs(gparts, part_index, tr, cc):
    blk = pl.BlockSpec((tr, cc), lambda i, *_: (i, 0))
    specs = []
    for i, _ in enumerate(gparts):
        if i in part_index:
            ix = part_index[i]
            if isinstance(ix, int):
                specs.append(pl.BlockSpec((None, tr, cc), functools.partial(lambda i, *_, ix: (ix, i, 0), ix=ix)))
            else:
                specs.append(pl.BlockSpec((None, tr, cc), lambda i, ix_ref: (ix_ref[0], i, 0)))
        else:
            specs.append(blk)
    return blk, specs


def _sum_parts(gparts, part_index, name):
    r, cc = gparts[0].shape[-2:]
    tr = _tile(r, 256)
    dyn = [i for i, ix in part_index.items() if not isinstance(ix, int)]
    assert len(dyn) <= 1

    def body(*refs):
        refs = refs[len(dyn):]
        g = refs[0][...].astype(F32)
        for gr in refs[1:-1]:
            g = g + gr[...].astype(F32)
        refs[-1][...] = g

    blk, specs = _part_specs(gparts, part_index, tr, cc)
    grid_spec = pltpu.PrefetchScalarGridSpec(num_scalar_prefetch=len(dyn), grid=(r // tr,), in_specs=specs, out_specs=blk)
    return pl.pallas_call(
        body, name=name, grid_spec=grid_spec, out_shape=jax.ShapeDtypeStruct((r, cc), F32),
        compiler_params=_params(("parallel",)),
    )(*[part_index[i] for i in dyn], *gparts)


def _adamw(w, m, v, gparts, name, part_index=None):
    r, cc = w.shape
    tr = _tile(r, 128)
    np_ = len(gparts)
    part_index = part_index or {}
    dyn = [i for i, ix in part_index.items() if not isinstance(ix, int)]
    assert len(dyn) <= 1
    bc1 = 1.0 / (1.0 - ADAM_B1 ** ADAM_STEP)
    bc2 = 1.0 / (1.0 - ADAM_B2 ** ADAM_STEP)

    def body(*refs):
        refs = refs[len(dyn):]
        w_ref, m_ref, v_ref = refs[:3]
        g_refs = refs[3:3 + np_]
        g_out, d_out, m_out, v_out = refs[3 + np_:]
        g = g_refs[0][...].astype(F32)
        for gr in g_refs[1:]:
            g = g + gr[...].astype(F32)
        mn = ADAM_B1 * m_ref[...] + (1.0 - ADAM_B1) * g
        vn = ADAM_B2 * v_ref[...] + (1.0 - ADAM_B2) * (g * g)
        g_out[...] = g
        m_out[...] = mn
        v_out[...] = vn
        d_out[...] = -ADAM_LR * ((mn * bc1) / (jnp.sqrt(vn * bc2) + ADAM_EPS) + ADAM_WD * w_ref[...])

    blk, g_specs = _part_specs(gparts, part_index, tr, cc)
    grid_spec = pltpu.PrefetchScalarGridSpec(
        num_scalar_prefetch=len(dyn), grid=(r // tr,),
        in_specs=[blk, blk, blk] + g_specs, out_specs=[blk] * 4)
    return pl.pallas_call(
        body, name=name, grid_spec=grid_spec,
        out_shape=[jax.ShapeDtypeStruct((r, cc), F32)] * 4,
        compiler_params=_params(("parallel",)),
    )(*[part_index[i] for i in dyn], w, m, v, *gparts)


def _row(vec):
    return vec.reshape(1, -1)


def kernel(x, c, w_ada, b_ada, norm1_g, w_in, conv_w, conv_b, lru_wa, lru_ba, lru_wx, lru_bx, lru_lambda, w_lru_out, w_attn_out, attn_sinks, rel_bias, w_out, norm2_g, w_ff1, w_ff2, final_g, loss_target, m_w_ada, m_b_ada, m_norm1_g, m_w_in, m_conv_w, m_conv_b, m_lru_wa, m_lru_ba, m_lru_wx, m_lru_bx, m_lru_lambda, m_w_lru_out, m_w_attn_out, m_attn_sinks, m_rel_bias, m_w_out, m_norm2_g, m_w_ff1, m_w_ff2, m_final_g, v_w_ada, v_b_ada, v_norm1_g, v_w_in, v_conv_w, v_conv_b, v_lru_wa, v_lru_ba, v_lru_wx, v_lru_bx, v_lru_lambda, v_w_lru_out, v_w_attn_out, v_attn_sinks, v_rel_bias, v_w_out, v_norm2_g, v_w_ff1, v_w_ff2, v_final_g):
    dd = D_MODEL
    t = x.shape[1]
    ix, iy, ic = lax.axis_index("x"), lax.axis_index("y"), lax.axis_index("c")
    me = 4 * ix + 2 * iy + ic
    chip = 2 * ix + iy
    c_idx = jnp.reshape(ic, (1,)).astype(jnp.int32)
    chip_idx = jnp.reshape(chip, (1,)).astype(jnp.int32)

    xs = x[0]
    target = loss_target[0]
    ncol_ada = w_ada.shape[2]
    cshard = conv_w.shape[2]

    shards = [jnp.transpose(w_in[0]).astype(BF16), w_lru_out[0].astype(BF16), w_attn_out[0].astype(BF16),
              w_out[0].astype(BF16), jnp.transpose(w_ff1[0]).astype(BF16), w_ff2[0].astype(BF16)]
    s_in, s_lo, s_ao, s_o, s_f1, s_f2 = shards

    def full(g):
        return g.reshape(N_DEV * g.shape[1], dd)

    def stack(g):
        return g.reshape(N_DEV, g.shape[0] // N_DEV, dd)

    (w_in_pair,) = _run_job(_pair_job([s_in]), "ag_w_in_pair")
    w_in_pair = lax.dynamic_update_slice_in_dim(w_in_pair, s_in[None], ic, axis=0)

    pack0 = jnp.zeros((8, dd), F32).at[0:1, :].set(c).at[1:1 + CONV_WIDTH, 0:cshard].set(conv_w[0])
    (g0,) = _all_gather([pack0], "ag_cond")
    c_all = g0[:, 0, :]
    conv_w_full = jnp.transpose(g0[:, 1:1 + CONV_WIDTH, 0:cshard], (1, 0, 2)).reshape(CONV_WIDTH, dd)
    b_cols = lax.dynamic_slice(b_ada, (0, me * ncol_ada), (1, ncol_ada))
    mod_cols = _ada_fwd(c_all, w_ada[0], b_cols, "ada_fwd")
    (g1,) = _all_gather([mod_cols], "ag_mod")
    mod_mine = lax.dynamic_index_in_dim(g1, me, axis=1, keepdims=False).reshape(6, dd)
    mod = jnp.concatenate([mod_mine, jnp.zeros((2, dd), F32)], axis=0)

    bucket = _bucket_table()
    onehot = jnp.asarray((bucket.reshape(1, -1) == np.arange(N_BUCKETS)[:, None]).astype(np.float32))
    bias = _rel_bias_fwd(rel_bias, onehot, "rel_bias_fwd").reshape(N_Q_HEADS, BLOCK, 2 * BLOCK)
    sinks = attn_sinks.reshape(N_Q_HEADS)
    lp = jnp.concatenate([conv_b, lru_ba, lru_bx, lru_lambda, conv_w_full], axis=0)
    wa, wx = lru_wa[0], lru_wx[0]

    tm = _tile(t, 1024)
    tn = 512
    h = _norm_mod(xs, norm1_g, mod, 0, "norm1")
    tn2 = 1024
    chunk_w = 2 * s_in.shape[0]

    def ids(*ks):
        return jnp.stack([jnp.asarray(k, jnp.int32) for k in ks])

    tn_in = _tile(chunk_w, 1024, LANE)
    (proj,), ((g_near,),) = _mm(
        h, w_in_pair.reshape(chunk_w, dd), "nt", tm=tm, tn=tn_in, outs=[BF16], name="in_proj_own",
        chunks=(ids(chip), chunk_w, ids(0)), n_total=IN_WIDTH, jobs=[_gather_job([w_in_pair], (0.7,), "near")])
    near_ids = ids(2 * (1 - ix) + iy, 2 * ix + 1 - iy)
    (proj,), ((g_far,),) = _mm(
        h, full(g_near), "nt", tm=tm, tn=tn_in, outs=[BF16], name="in_proj_near",
        chunks=(near_ids, chunk_w, near_ids), into=proj, jobs=[_gather_job([g_near], (0.6,), "far")])
    far_chip = 2 * (1 - ix) + 1 - iy
    (proj,), ((g_o,),) = _mm(
        h, full(g_far), "nt", tm=tm, tn=tn_in, outs=[BF16], name="in_proj_far",
        chunks=(ids(far_chip), chunk_w, ids(far_chip)), into=proj, jobs=[_gather_job([s_o], (0.5, 0.8))])
    far_slots = lax.dynamic_slice_in_dim(g_far, 2 * far_chip, 2, axis=0)
    win_t = lax.dynamic_update_slice_in_dim(g_near, far_slots, 2 * far_chip, axis=0)
    win_t = full(lax.dynamic_update_slice_in_dim(win_t, w_in_pair, 2 * chip, axis=0))
    (g_o,) = _fill_own([g_o], [s_o])
    (att, lse), ((g_f1,),) = _attn_fwd(proj, bias, sinks, "attn_fwd", jobs=[_gather_job([s_f1], (0.6, 0.9))])
    (rec, ya), ((g_lo,), (g_ao,)) = _lru_fwd(proj, lp, wa, wx, "lru_fwd",
                                             jobs=[_gather_job([s_lo], (0.3, 0.45)), _gather_job([s_ao], (0.75, 0.9))])
    g_lo, g_ao, g_f1 = _fill_own([g_lo, g_ao, g_f1], [s_lo, s_ao, s_f1])
    wlo, wao, wo, wf1_t = full(g_lo), full(g_ao), full(g_o), full(g_f1)
    (y_a,) = _mm(ya, wlo, "nn", tm=tm, tn=tn2, outs=[BF16], name="lru_out")

    def merge_epi(acc, ya_t, ga_t, gb_t):
        return acc, _sigmoid(ga_t.astype(F32)) * ya_t.astype(F32) + _sigmoid(gb_t.astype(F32)) * acc

    y_b, merged = _mm(att, wao, "nn", tm=tm, tn=tn, outs=[BF16, BF16], epi=merge_epi,
                      tiles=[(y_a, 0), (proj, OFF_GA // tn), (proj, OFF_GB // tn)], name="attn_out_merge")

    def resid_epi(row):
        def epi(acc, x_t, mod_t):
            return acc, x_t + mod_t[row:row + 1, :] * acc
        return epi

    o1, x1 = _mm(merged, wo, "nn", tm=tm, tn=tn2, outs=[BF16, F32], epi=resid_epi(2),
                 tiles=[(xs, 0)], rows=[mod], name="out_proj")
    h2 = _norm_mod(x1, norm2_g, mod, 3, "norm2")

    def relu2_epi(acc):
        rl = jnp.maximum(acc, 0.0)
        return acc, rl * rl

    (f_pre, ff), ((g_f2,),) = _mm(h2, wf1_t, "nt", tm=tm, tn=1024, outs=[BF16, BF16], epi=relu2_epi, name="ff1",
                                  jobs=[_gather_job([s_f2], (0.6, 0.9))])
    wf2 = full(_fill_own([g_f2], [s_f2])[0])
    tmh = _tile(t, 512)
    o2, x2 = _mm(ff, wf2, "nn", tm=tmh, tn=tn, outs=[BF16, F32], epi=resid_epi(5),
                 tiles=[(x1, 0)], rows=[mod], name="ff2")

    loss_blk, dx2, do2, st_f = _final_loss(x2, o2, _row(final_g), target, mod, "final_loss")
    loss = lax.psum(loss_blk[0, 0], ("x", "y", "c"))

    def drelu2_epi(acc, f_t):
        return (acc * (2.0 * jnp.maximum(f_t.astype(F32), 0.0)),)

    tmw, tnw, tkw = 512, dd, _tile(t, 2048)
    (df,) = _mm(do2, wf2, "nt", tm=tm, tn=1024, outs=[BF16], epi=drelu2_epi, tiles=[(f_pre, 0)], name="ff2_dx")
    (dwf2,) = _mm(ff, do2, "tn", tm=tmw, tn=tnw, tk=tkw, outs=[BF16], name="ff2_dw")
    (dh2,), ((sib_f2,),) = _mm(df, wf1_t, "nn", tm=tmh, tn=tn, outs=[BF16], name="ff1_dx",
                              jobs=[_sibling_job([stack(dwf2)])])
    pair_f2 = _pair_add(stack(dwf2), sib_f2, c_idx, "rs_pair_add_w_ff2")
    (dwf1_t,), ((chips_f2,),) = _mm(df, h2, "tn", tm=tmw, tn=tnw, tk=tkw, outs=[BF16], name="ff1_dw",
                                    jobs=[_chips_job([pair_f2])])
    (dx1, st_2, do1), ((sib_f1,),) = _norm_mod_bwd(x1, norm2_g, mod, 3, dh2, dx2, (o1, 2), "norm2_bwd",
                                                   jobs=[_sibling_job([stack(dwf1_t)])])
    pair_f1 = _pair_add(stack(dwf1_t), sib_f1, c_idx, "rs_pair_add_w_ff1")

    def dmerge_epi(acc, ga_t, gb_t, ya_t, yb_t):
        sa, sb = _sigmoid(ga_t.astype(F32)), _sigmoid(gb_t.astype(F32))
        return (acc * sa, acc * sb, acc * ya_t.astype(F32) * sa * (1.0 - sa), acc * yb_t.astype(F32) * sb * (1.0 - sb))

    tmh = _tile(t, 512)
    dy_a, dy_b, dga, dgb = _mm(do1, wo, "nt", tm=tm, tn=tn, outs=[BF16] * 4, epi=dmerge_epi,
                               tiles=[(proj, OFF_GA // tn), (proj, OFF_GB // tn), (y_a, 0), (y_b, 0)], name="out_proj_dx")
    (dwo,) = _mm(merged, do1, "tn", tm=tmw, tn=tnw, tk=tkw, outs=[BF16], name="out_proj_dw")
    (datt,) = _mm(dy_b, wao, "nt", tm=tm, tn=tn2, outs=[BF16], name="attn_out_dx")
    (dwao,) = _mm(att, dy_b, "tn", tm=tmw, tn=tnw, tk=tkw, outs=[BF16], name="attn_out_dw")
    (dya,) = _mm(dy_a, wlo, "nt", tm=tm, tn=tn2, outs=[BF16], name="lru_out_dx")
    (dwlo,) = _mm(ya, dy_a, "tn", tm=tmw, tn=tnw, tk=tkw, outs=[BF16], name="lru_out_dw")
    (dq, dk_pad, dv_pad, dbias, dsinks), (sib_3, (chips_f1,)) = _attn_bwd(
        proj, datt, lse, bias, sinks, "attn_bwd",
        jobs=[_sibling_job([stack(dwlo), stack(dwao), stack(dwo)]), _chips_job([pair_f1])])
    pair_lo, pair_ao, pair_o = [_pair_add(stack(g_), s_, c_idx, "rs_pair_add_" + nm_)
                                for g_, s_, nm_ in zip((dwlo, dwao, dwo), sib_3, ("w_lru_out", "w_attn_out", "w_out"))]
    (dlx, dlg, st_l, dwa, dwx), ((chips_lo, chips_ao, chips_o),) = _lru_bwd(
        proj, rec, dya, lp, wa, wx, "lru_bwd", jobs=[_chips_job([pair_lo, pair_ao, pair_o])])
    dproj = jnp.concatenate([dlx, dlg, dq, dk_pad[BLOCK:].astype(BF16), dv_pad[BLOCK:].astype(BF16), dga, dgb], axis=1)
    lru_mats = jnp.concatenate([dwa.reshape(-1, dd), dwx.reshape(-1, dd)], axis=0)
    (dwin_t,), ((mats_all,),) = _mm(dproj, h, "tn", tm=768, tn=tnw, tk=tkw, outs=[BF16], name="in_proj_dw",
                                   jobs=[_gather_job([lru_mats], (0.3, 0.6))])
    (mats_all,) = _fill_own([mats_all], [lru_mats])
    (sib_in,) = _exchange_sibling([stack(dwin_t)], "rs_sibling_w_in")
    pair_in = _pair_add(stack(dwin_t), sib_in, c_idx, "rs_pair_add_w_in", widths=(dd // 2, dd // 2))
    n_rb = t // tm
    dx_tiles = dict(tm=tm, tn=tn2, tk=IN_WIDTH // 4, outs=[BF16])
    if n_rb >= 2:
        (dh,), ((chips_in_0,),) = _mm(dproj, win_t, "nn", name="in_proj_dx_top", rows_part=(0, n_rb // 2),
                                     jobs=[_chips_job(pair_in[0:1])], **dx_tiles)
        (dh,), ((chips_in_1,),) = _mm(dproj, win_t, "nn", name="in_proj_dx_bottom", into=dh,
                                     rows_part=(n_rb // 2, n_rb - n_rb // 2), jobs=[_chips_job(pair_in[1:2])], **dx_tiles)
    else:
        (dh,), ((chips_in_0,), (chips_in_1,)) = _mm(
            dproj, win_t, "nn", name="in_proj_dx", jobs=[_chips_job(pair_in[0:1]), _chips_job(pair_in[1:2])], **dx_tiles)
    grad_x, st_1 = _norm_mod_bwd(xs, norm1_g, mod, 0, dh, dx1, None, "norm1_bwd")
    chips_in = [chips_in_0, chips_in_1]

    drel = jnp.transpose(_rel_bias_bwd(dbias.reshape(N_Q_HEADS, -1), onehot, "rel_bias_bwd"))

    misc = jnp.concatenate([drel.reshape(1, -1), dsinks[0:1, 0:N_Q_HEADS],
                            jnp.zeros((1, dd - N_BUCKETS * N_Q_HEADS - N_Q_HEADS), F32)], axis=1)
    zero_row = jnp.zeros((1, dd), F32)
    small = jnp.concatenate([
        st_1[0:2], st_2[3:4], st_2[0:2], st_f[1:2], zero_row, zero_row,
        st_1[2:3], st_l[0:1], st_l[1:2], st_l[2:3], st_l[3:4], st_2[2:3], st_f[0:1], misc,
        st_l[4:8], jnp.zeros((4, dd), F32)], axis=0)
    (small_all,) = _all_gather([small], "ag_small")

    def pack_small(b_, n1, cb, ba, bx, lam, n2, fg, rb, sk):
        misc_ = jnp.concatenate([rb.reshape(1, -1), sk.reshape(1, -1),
                                 jnp.zeros((1, dd - N_BUCKETS * N_Q_HEADS - N_Q_HEADS), F32)], axis=1)
        return jnp.concatenate([b_.reshape(6, dd), jnp.zeros((2, dd), F32), n1, cb, ba, bx, lam, n2, _row(fg), misc_,
                                jnp.zeros((8, dd), F32)], axis=0)

    def pack_mats(wa_, wx_):
        return jnp.concatenate([wa_.reshape(-1, dd), wx_.reshape(-1, dd)], axis=0)

    every = {i: i for i in range(N_DEV)}
    w_s = pack_small(b_ada, norm1_g, conv_b, lru_ba, lru_bx, lru_lambda, norm2_g, final_g, rel_bias, attn_sinks)
    m_s = pack_small(m_b_ada, m_norm1_g, m_conv_b, m_lru_ba, m_lru_bx, m_lru_lambda, m_norm2_g, m_final_g, m_rel_bias, m_attn_sinks)
    v_s = pack_small(v_b_ada, v_norm1_g, v_conv_b, v_lru_ba, v_lru_bx, v_lru_lambda, v_norm2_g, v_final_g, v_rel_bias, v_attn_sinks)
    small_res = _adamw(w_s, m_s, v_s, [small_all] * N_DEV, "adamw_small", part_index=every)
    mats_res = _adamw(pack_mats(lru_wa, lru_wx), pack_mats(m_lru_wa, m_lru_wx), pack_mats(v_lru_wa, v_lru_wx),
                      [mats_all] * N_DEV, "adamw_lru_mats", part_index=every)

    def unpack_small(s, mt):
        nb_ = N_BUCKETS * N_Q_HEADS
        half = mt.shape[0] // 2
        return dict(
            b_ada=s[0:6].reshape(1, 6 * dd), norm1_g=s[8:9], conv_b=s[9:10], lru_ba=s[10:11], lru_bx=s[11:12],
            lru_lambda=s[12:13], norm2_g=s[13:14], final_g=s[14], rel_bias=s[15, 0:nb_].reshape(N_BUCKETS, N_Q_HEADS),
            attn_sinks=s[15:16, nb_:nb_ + N_Q_HEADS],
            lru_wa=mt[:half].reshape(1, LRU_BLOCKS, LRU_BLOCK_W, LRU_BLOCK_W),
            lru_wx=mt[half:].reshape(1, LRU_BLOCKS, LRU_BLOCK_W, LRU_BLOCK_W))

    res = {k: [None] * 4 for k in ("w_ada", "w_in", "conv_w", "w_lru_out", "w_attn_out", "w_out", "w_ff1", "w_ff2")}
    for q_, (s_, mt_) in enumerate(zip(small_res, mats_res)):
        for k_, val in unpack_small(s_, mt_).items():
            res.setdefault(k_, [None] * 4)[q_] = val

    g_conv = lax.dynamic_slice(small_res[0][16:16 + CONV_WIDTH], (0, me * cshard), (CONV_WIDTH, cshard))
    conv_res = _adamw(conv_w[0], m_conv_w[0], v_conv_w[0], [g_conv], "adamw_conv")
    res["conv_w"] = [r_[None] for r_ in conv_res]

    dmod_all = small_all[:, 0:6, :].reshape(N_DEV, 6 * dd)
    dmod_cols = lax.dynamic_slice(dmod_all, (0, me * ncol_ada), (N_DEV, ncol_ada))
    g_ada = _ada_bwd(jnp.transpose(c_all), dmod_cols, "ada_bwd")
    ada_res = _adamw(w_ada[0], m_w_ada[0], v_w_ada[0], [g_ada], "adamw_ada")
    res["w_ada"] = [r_[None] for r_ in ada_res]

    pair = [None, pair_lo, pair_ao, pair_o, pair_f1, pair_f2]
    from_chips = [None, chips_lo, chips_ao, chips_o, chips_f1, chips_f2]
    own_and_chips = {0: chip_idx, 1: 0, 2: 1, 3: 2}

    def summed(i):
        return [pair[i], from_chips[i], from_chips[i], from_chips[i]], {0: chip_idx, 1: 0, 2: 1, 3: 2}

    def sum_only(i, name):
        parts, index = summed(i)
        return _sum_parts(parts, index, name)

    g_in = jnp.transpose(jnp.concatenate(
        [_sum_parts([p_, c_, c_, c_], own_and_chips, "sum_w_in_%d" % q_)
         for q_, (p_, c_) in enumerate(zip(pair_in, chips_in))], axis=1))
    res["w_in"] = [r_[None] for r_ in _adamw(w_in[0], m_w_in[0], v_w_in[0], [g_in], "adamw_w_in")]
    g_ff1 = jnp.transpose(sum_only(4, "sum_w_ff1"))
    res["w_ff1"] = [r_[None] for r_ in _adamw(w_ff1[0], m_w_ff1[0], v_w_ff1[0], [g_ff1], "adamw_w_ff1")]
    for i, (nm, w_, m_, v_) in {1: ("w_lru_out", w_lru_out, m_w_lru_out, v_w_lru_out),
                                2: ("w_attn_out", w_attn_out, m_w_attn_out, v_w_attn_out),
                                3: ("w_out", w_out, m_w_out, v_w_out),
                                5: ("w_ff2", w_ff2, m_w_ff2, v_w_ff2)}.items():
        parts, index = summed(i)
        res[nm] = [r_[None] for r_ in _adamw(w_[0], m_[0], v_[0], parts, "adamw_" + nm, part_index=index)]

    order = ["w_ada", "b_ada", "norm1_g", "w_in", "conv_w", "conv_b", "lru_wa", "lru_ba", "lru_wx", "lru_bx",
             "lru_lambda", "w_lru_out", "w_attn_out", "attn_sinks", "rel_bias", "w_out", "norm2_g", "w_ff1",
             "w_ff2", "final_g"]
    out = [loss, grad_x[None]]
    for q_ in range(4):
        out += [res[k_][q_] for k_ in order]
    return tuple(out)
```

```python
import functools
import math

import numpy as np
import jax
import jax.numpy as jnp
from jax import lax
from jax.experimental import pallas as pl
from jax.experimental.pallas import tpu as pltpu

F32 = jnp.float32
BF16 = jnp.bfloat16
MESH = pl.DeviceIdType.MESH

D_MODEL = 2048
N_Q_HEADS = 32
N_KV_HEADS = 4
GROUP = N_Q_HEADS // N_KV_HEADS
HEAD_DIM = 64
KV_WIDTH = N_KV_HEADS * HEAD_DIM
BLOCK = 128
NEG_INF = -1e30
N_BUCKETS = 32
MAX_DISTANCE = 128
LRU_BLOCKS = 16
LRU_BLOCK_W = 128
CONV_WIDTH = 4
LRU_C = 8.0
D_FF = 4 * D_MODEL
EPS = 1e-6
IN_WIDTH = 5 * D_MODEL + 2 * KV_WIDTH
OFF_LX, OFF_LG, OFF_Q, OFF_K, OFF_V, OFF_GA, OFF_GB = 0, 2048, 4096, 6144, 6400, 6656, 8704

ADAM_LR, ADAM_B1, ADAM_B2, ADAM_EPS, ADAM_WD, ADAM_STEP = 0.001, 0.9, 0.999, 1e-08, 0.01, 10

N_DEV = 8
VMEM_LIMIT_BYTES = 48 * 1024 * 1024
LANE = 128
LRU_CW = 512


def _params(sem, **kw):
    return pltpu.CompilerParams(dimension_semantics=sem, vmem_limit_bytes=VMEM_LIMIT_BYTES, **kw)


def _tile(n, pref, mult=8):
    if n <= pref:
        return n
    t = (pref // mult) * mult
    while t >= mult:
        if n % t == 0:
            return t
        t -= mult
    return n


def _sigmoid(x):
    return 0.5 * jnp.tanh(0.5 * x) + 0.5


def _gelu_tanh(x):
    k = math.sqrt(2.0 / math.pi)
    t = jnp.tanh(k * (x + 0.044715 * x * x * x))
    return 0.5 * x * (1.0 + t), t


def _gelu_tanh_grad(x, t):
    k = math.sqrt(2.0 / math.pi)
    return 0.5 * (1.0 + t) + 0.5 * x * (1.0 - t * t) * k * (1.0 + 3.0 * 0.044715 * x * x)


def _log1p(e):
    u = 1.0 + e
    return jnp.where(u == 1.0, e, jnp.log(u) * (e / jnp.where(u == 1.0, 1.0, u - 1.0)))


def _softplus(x):
    return jnp.maximum(x, 0.0) + _log1p(jnp.exp(-jnp.abs(x)))


def _neg_expm1(x, exp_x):
    series = -x * (1.0 + x * (0.5 + x * (1.0 / 6.0 + x * (1.0 / 24.0))))
    return jnp.where(x > -0.0625, series, 1.0 - exp_x)


def _my_place():
    return lax.axis_index("x"), lax.axis_index("y"), lax.axis_index("c")


def _all_gather(arrs, name):
    return _fill_own(_run_job(_gather_job(arrs), name), arrs)


def _fill_own(stacks, shards):
    me = 4 * lax.axis_index("x") + 2 * lax.axis_index("y") + lax.axis_index("c")
    out = []
    for g, s in zip(stacks, shards):
        if g.size * g.dtype.itemsize <= (1 << 20):
            slot = lax.broadcasted_iota(jnp.int32, g.shape, 0)
            out.append(jnp.where(slot == me, s[None], g))
        else:
            out.append(lax.dynamic_update_slice_in_dim(g, s[None], me, axis=0))
    return out


class _Job:
    def __init__(self, ins, outs, sems, start, finish, forwards=(), forward_at=()):
        self.ins, self.outs, self.sems = list(ins), list(outs), list(sems)
        self.start, self.finish, self.forwards, self.forward_at = start, finish, list(forwards), list(forward_at)
        assert len(self.forwards) <= len(self.forward_at)


def _gather_job(arrs, forward_at=(0.6, 0.85), part="all"):
    n = len(arrs)
    whole, near, far = part == "all", part == "near", part == "far"

    def copies(ins, outs, sems):
        send_sems, recv_sems, local_sems = sems
        x, y, c = _my_place()
        sib = (x, y, 1 - c)
        xn, yn, dg = (1 - x, y), (x, 1 - y), (1 - x, 1 - y)

        def slot(px, py, pc):
            return 4 * px + 2 * py + pc

        def cp(a, k, src, dst_slot, to):
            return pltpu.make_async_remote_copy(
                src_ref=src, dst_ref=outs[a].at[dst_slot], send_sem=send_sems.at[a, k], recv_sem=recv_sems.at[a, k],
                device_id=to, device_id_type=MESH)

        def arrival(a, k, dst_slot):
            return cp(a, k, outs[a].at[dst_slot], dst_slot, sib)

        me_slot = slot(x, y, c)
        sends, local, relay, passes, arrive = {}, [], {}, {}, {}
        for a in range(n):
            if whole:
                own = ins[a]
                sends[a, 0] = cp(a, 0, own, me_slot, sib)
            if near:
                own = ins[a].at[c]
            if whole or near:
                sends[a, 1] = cp(a, 1, own, me_slot, (*xn, c))
                sends[a, 2] = cp(a, 2, own, me_slot, (*yn, c))
                passes[a, 4] = cp(a, 4, outs[a].at[slot(*xn, c)], slot(*xn, c), sib)
                passes[a, 5] = cp(a, 5, outs[a].at[slot(*yn, c)], slot(*yn, c), sib)
            if whole or far:
                source = outs[a] if whole else ins[a]
                relayed = c * slot(*yn, c) + (1 - c) * slot(*xn, c)
                relay[a, 3] = cp(a, 3, source.at[relayed], relayed, (x + c * (1 - 2 * x), y + (1 - c) * (1 - 2 * y), c))
                passes[a, 6] = cp(a, 6, outs[a].at[slot(*dg, c)], slot(*dg, c), sib)
            arrive[a, 0] = arrival(a, 0, slot(x, y, 1 - c))
            arrive[a, 1] = arrival(a, 1, slot(*xn, c))
            arrive[a, 2] = arrival(a, 2, slot(*yn, c))
            arrive[a, 3] = arrival(a, 3, slot(*dg, c))
            arrive[a, 4] = arrival(a, 4, slot(*xn, 1 - c))
            arrive[a, 5] = arrival(a, 5, slot(*yn, 1 - c))
            arrive[a, 6] = arrival(a, 6, slot(*dg, 1 - c))
        return sends, local, relay, passes, arrive

    def pick(d, ks):
        return [d[a, k] for a in range(n) for k in ks if (a, k) in d]

    def start(ins, outs, sems):
        sends, local, relay, _, _ = copies(ins, outs, sems)
        for cp in local + pick(sends, (0, 1, 2)) + (pick(relay, (3,)) if far else []):
            cp.start()

    def forward_neighbours(ins, outs, sems):
        _, _, relay, passes, arrive = copies(ins, outs, sems)
        for cp in pick(arrive, (1, 2)):
            cp.wait_recv()
        for cp in pick(relay, (3,)) + pick(passes, (4, 5)):
            cp.start()

    def forward_diagonal(ins, outs, sems):
        _, _, _, passes, arrive = copies(ins, outs, sems)
        for cp in pick(arrive, (3,)):
            cp.wait_recv()
        for cp in pick(passes, (6,)):
            cp.start()

    def finish(ins, outs, sems):
        sends, local, relay, passes, arrive = copies(ins, outs, sems)
        for cp in pick(arrive, (6,) if far else (0, 4, 5, 6) if whole else (4, 5)):
            cp.wait_recv()
        for cp in pick(sends, (0, 1, 2)) + pick(relay, (3,)) + pick(passes, (4, 5, 6)):
            cp.wait_send()
        for cp in local:
            cp.wait()

    shapes = [a.shape[-2:] for a in arrs]
    forwards = [forward_diagonal] if far else [forward_neighbours] if near else [forward_neighbours, forward_diagonal]
    return _Job(arrs, [jax.ShapeDtypeStruct((N_DEV,) + s, a.dtype) for s, a in zip(shapes, arrs)],
                [pltpu.SemaphoreType.DMA((n, 7)), pltpu.SemaphoreType.DMA((n, 7)), pltpu.SemaphoreType.DMA((n,))],
                start, finish, forwards, forward_at)


def _pair_job(arrs):
    n = len(arrs)

    def copies(ins, outs, sems):
        send_sems, recv_sems, local_sems = sems
        x, y, c = _my_place()
        remote = [pltpu.make_async_remote_copy(
            src_ref=ins[a], dst_ref=outs[a].at[c], send_sem=send_sems.at[a], recv_sem=recv_sems.at[a],
            device_id=(x, y, 1 - c), device_id_type=MESH) for a in range(n)]
        local = []
        arrive = [pltpu.make_async_remote_copy(
            src_ref=ins[a], dst_ref=outs[a].at[1 - c], send_sem=send_sems.at[a], recv_sem=recv_sems.at[a],
            device_id=(x, y, 1 - c), device_id_type=MESH) for a in range(n)]
        return remote, local, arrive

    def start(ins, outs, sems):
        remote, local, _ = copies(ins, outs, sems)
        for cp in remote + local:
            cp.start()

    def finish(ins, outs, sems):
        remote, local, arrive = copies(ins, outs, sems)
        for cp in arrive:
            cp.wait_recv()
        for cp in remote:
            cp.wait_send()
        for cp in local:
            cp.wait()

    return _Job(arrs, [jax.ShapeDtypeStruct((2,) + a.shape, a.dtype) for a in arrs],
                [pltpu.SemaphoreType.DMA((n,)), pltpu.SemaphoreType.DMA((n,)), pltpu.SemaphoreType.DMA((n,))],
                start, finish)


def _run_job(job, name):
    ni, no = len(job.ins), len(job.outs)

    def body(*refs):
        ins, outs, sems = refs[:ni], refs[ni:ni + no], refs[ni + no:]
        job.start(ins, outs, sems)
        for fwd in job.forwards:
            fwd(ins, outs, sems)
        job.finish(ins, outs, sems)

    any_spec = pl.BlockSpec(memory_space=pl.ANY)
    return pl.pallas_call(
        body, name=name, out_shape=job.outs, in_specs=[any_spec] * ni, out_specs=[any_spec] * no,
        scratch_shapes=job.sems,
    )(*job.ins)


def _call(body, *, name, grid, in_specs, out_specs, out_shape, scratch_shapes=(), sem, args, jobs=(),
          prefetch=(), aliases=None):
    in_specs, out_specs, out_shape, scratch_shapes = list(in_specs), list(out_specs), list(out_shape), list(scratch_shapes)
    prefetch = list(prefetch)
    if not jobs and not prefetch and not aliases:
        res = pl.pallas_call(body, name=name, grid=grid, in_specs=in_specs, out_specs=out_specs, out_shape=out_shape,
                             scratch_shapes=scratch_shapes, compiler_params=_params(sem))(*args)
        return list(res), []
    n_in, n_out, n_scr = len(in_specs), len(out_specs), len(scratch_shapes)
    j_in = [len(j.ins) for j in jobs]
    j_out = [len(j.outs) for j in jobs]
    j_sem = [len(j.sems) for j in jobs]
    total = int(np.prod(grid))
    any_spec = pl.BlockSpec(memory_space=pl.ANY)

    def wrapped(*refs):
        pos = [0]

        def take(k):
            part = refs[pos[0]:pos[0] + k]
            pos[0] += k
            return part

        take(len(prefetch))
        ins = take(n_in)
        jins = [take(k) for k in j_in]
        outs = take(n_out)
        jouts = [take(k) for k in j_out]
        scr = take(n_scr)
        jsems = [take(k) for k in j_sem]
        step = pl.program_id(0)
        for d in range(1, len(grid)):
            step = step * grid[d] + pl.program_id(d)
        for j, job in enumerate(jobs):
            pl.when(step == 0)(functools.partial(job.start, jins[j], jouts[j], jsems[j]))
            for fwd, frac in zip(job.forwards, job.forward_at):
                at = min(int(total * frac), total - 1)
                pl.when(step == at)(functools.partial(fwd, jins[j], jouts[j], jsems[j]))
        body(*ins, *outs, *scr)
        for j, job in enumerate(jobs):
            pl.when(step == total - 1)(functools.partial(job.finish, jins[j], jouts[j], jsems[j]))

    grid_spec = pltpu.PrefetchScalarGridSpec(
        num_scalar_prefetch=len(prefetch), grid=grid,
        in_specs=in_specs + [any_spec] * sum(j_in),
        out_specs=out_specs + [any_spec] * sum(j_out),
        scratch_shapes=scratch_shapes + [s for j in jobs for s in j.sems])
    res = pl.pallas_call(
        wrapped, name=name, grid_spec=grid_spec,
        out_shape=out_shape + [o for j in jobs for o in j.outs],
        input_output_aliases={len(prefetch) + i: o for i, o in (aliases or {}).items()},
        compiler_params=_params(("arbitrary",) * len(grid)),
    )(*prefetch, *args, *[a for j in jobs for a in j.ins])
    res = list(res)
    own, rest = res[:n_out], res[n_out:]
    per_job = []
    for k in j_out:
        per_job.append(rest[:k])
        rest = rest[k:]
    return own, per_job


def _exchange_sibling(arrs, name):
    return _run_job(_sibling_job(arrs), name)


def _sibling_job(arrs):
    n = len(arrs)

    def copies(ins, outs, sems):
        send_sems, recv_sems = sems
        x, y, c = _my_place()
        return [pltpu.make_async_remote_copy(
            src_ref=ins[a].at[2 * k + 1 - c], dst_ref=outs[a].at[k],
            send_sem=send_sems.at[a, k], recv_sem=recv_sems.at[a, k],
            device_id=(x, y, 1 - c), device_id_type=MESH) for a in range(n) for k in range(4)]

    def start(ins, outs, sems):
        for cp in copies(ins, outs, sems):
            cp.start()

    def finish(ins, outs, sems):
        for cp in copies(ins, outs, sems):
            cp.wait()

    return _Job(arrs, [jax.ShapeDtypeStruct((4,) + a.shape[1:], a.dtype) for a in arrs],
                [pltpu.SemaphoreType.DMA((n, 4)), pltpu.SemaphoreType.DMA((n, 4))], start, finish)


def _chips_job(arrs):
    n = len(arrs)

    def copies(ins, outs, sems):
        send_sems, recv_sems = sems
        x, y, c = _my_place()
        chips = [(1 - x, y), (x, 1 - y), (1 - x, 1 - y)]
        return [pltpu.make_async_remote_copy(
            src_ref=ins[a].at[2 * px + py], dst_ref=outs[a].at[j],
            send_sem=send_sems.at[a, j], recv_sem=recv_sems.at[a, j],
            device_id=(px, py, c), device_id_type=MESH) for a in range(n) for j, (px, py) in enumerate(chips)]

    def start(ins, outs, sems):
        for cp in copies(ins, outs, sems):
            cp.start()

    def finish(ins, outs, sems):
        for cp in copies(ins, outs, sems):
            cp.wait()

    return _Job(arrs, [jax.ShapeDtypeStruct((3,) + a.shape[1:], a.dtype) for a in arrs],
                [pltpu.SemaphoreType.DMA((n, 3)), pltpu.SemaphoreType.DMA((n, 3))], start, finish)


def _pair_add(stack, from_sibling, c_idx, name, widths=None):
    _, r, cc = stack.shape
    tr = _tile(r, 512)
    ws = [cc] if widths is None else list(widths)
    assert sum(ws) == cc
    starts = [sum(ws[:q]) for q in range(len(ws))]

    def body(c_ref, a_ref, b_ref, *o_refs):
        res = a_ref[...].astype(F32) + b_ref[...].astype(F32)
        for o_ref, s0, w_ in zip(o_refs, starts, ws):
            o_ref[...] = res[:, s0:s0 + w_].astype(o_ref.dtype)

    grid_spec = pltpu.PrefetchScalarGridSpec(
        num_scalar_prefetch=1, grid=(4, r // tr),
        in_specs=[pl.BlockSpec((None, tr, cc), lambda k, i, c_ref: (2 * k + c_ref[0], i, 0)),
                  pl.BlockSpec((None, tr, cc), lambda k, i, c_ref: (k, i, 0))],
        out_specs=[pl.BlockSpec((None, tr, w_), lambda k, i, c_ref: (k, i, 0)) for w_ in ws])
    res = pl.pallas_call(
        body, name=name, grid_spec=grid_spec,
        out_shape=[jax.ShapeDtypeStruct((4, r, w_), BF16) for w_ in ws],
        compiler_params=_params(("parallel", "parallel")),
    )(c_idx, stack, from_sibling)
    return res[0] if widths is None else list(res)


_DIMS = {"nn": ((1,), (0,)), "nt": ((1,), (1,)), "tn": ((0,), (0,))}


def _mm(a, b, mode, *, tm, tn, tk=None, outs, epi=None, tiles=(), rows=(), name, jobs=(), chunks=None, into=None,
        n_total=None, rows_part=None):
    if mode == "tn":
        kk, m = a.shape
    else:
        m, kk = a.shape
    n_all = b.shape[0] if mode == "nt" else b.shape[1]
    nn = n_all if chunks is None else chunks[0].shape[0] * chunks[1]
    n_all = n_all if n_total is None else n_total
    tk = kk if tk is None else tk
    assert m % tm == 0 and nn % tn == 0 and kk % tk == 0, (name, a.shape, b.shape, tm, tn, tk)
    nk = kk // tk
    nt, nr, no = len(tiles), len(rows), len(outs)
    n_into = 0 if into is None else 1

    def body(*refs):
        a_ref, b_ref = refs[:2]
        tile_refs = refs[2:2 + nt]
        row_refs = refs[2 + nt:2 + nt + nr]
        out_refs = refs[2 + nt + nr + n_into:2 + nt + nr + n_into + no]
        part = lax.dot_general(a_ref[...], b_ref[...], (_DIMS[mode], ((), ())), preferred_element_type=F32)

        def finish(acc):
            if epi is None:
                res = (acc,)
            else:
                res = epi(acc, *[t[...] for t in tile_refs], *[r[...] for r in row_refs])
            for o_ref, val in zip(out_refs, res):
                o_ref[...] = val.astype(o_ref.dtype)

        if nk == 1:
            finish(part)
        else:
            acc_ref = refs[-1]
            k = pl.program_id(2)

            @pl.when(k == 0)
            def _():
                acc_ref[...] = part

            @pl.when(k > 0)
            def _():
                acc_ref[...] += part

            @pl.when(k == nk - 1)
            def _():
                finish(acc_ref[...])

    if chunks is None:
        col = b_row = lambda j, *_: j
        prefetch = []
    else:
        assert mode == "nt" and not tiles and not rows and chunks[1] % tn == 0
        per = chunks[1] // tn
        col = lambda j, ids, b_ids: ids[j // per] * per + j % per
        b_row = lambda j, ids, b_ids: b_ids[j // per] * per + j % per
        prefetch = [chunks[0], chunks[2]]
    row0, n_row_blocks = (0, m // tm) if rows_part is None else rows_part
    assert rows_part is None or (mode != "tn" and not tiles)
    if mode == "tn":
        a_spec = pl.BlockSpec((tk, tm), lambda i, j, k, *_: (k, i))
    else:
        a_spec = pl.BlockSpec((tm, tk), lambda i, j, k, *_: (i + row0, k))
    if mode == "nt":
        b_spec = pl.BlockSpec((tn, tk), lambda i, j, k, *s: (b_row(j, *s), k))
    else:
        b_spec = pl.BlockSpec((tk, tn), lambda i, j, k, *_: (k, j))
    tile_specs = [pl.BlockSpec((tm, tn), functools.partial(lambda i, j, k, off: (i, j + off), off=off))
                  for _, off in tiles]
    row_specs = [pl.BlockSpec((r.shape[0], tn), lambda i, j, k: (0, j)) for r in rows]
    out_spec = pl.BlockSpec((tm, tn), lambda i, j, k, *s: (i + row0, col(j, *s)))
    args = [a, b, *[t for t, _ in tiles], *rows]
    in_specs = [a_spec, b_spec] + tile_specs + row_specs
    aliases = None
    if into is not None:
        aliases = {len(args): 0}
        args.append(into)
        in_specs.append(pl.BlockSpec(memory_space=pl.ANY))
    res, jres = _call(
        body, name=name, grid=(n_row_blocks, nn // tn, nk),
        in_specs=in_specs, out_specs=[out_spec] * no,
        out_shape=[jax.ShapeDtypeStruct((m, n_all), dt) for dt in outs],
        scratch_shapes=[pltpu.VMEM((tm, tn), F32)] if nk > 1 else [],
        sem=("parallel", "parallel", "arbitrary"),
        args=args, jobs=jobs, prefetch=prefetch, aliases=aliases)
    return (res, jres) if jobs else res


def _ada_fwd(c_all, w_ada, b_ada_cols, name):
    nb, dd = c_all.shape
    ncol = w_ada.shape[1]
    tn = _tile(ncol, 512, LANE)

    def body(c_ref, w_ref, b_ref, o_ref):
        cv = c_ref[...]
        act = (cv * _sigmoid(cv)).astype(BF16)
        o_ref[...] = jnp.dot(act, w_ref[...].astype(BF16), preferred_element_type=F32) + b_ref[...]

    return pl.pallas_call(
        body, name=name, grid=(ncol // tn,),
        in_specs=[pl.BlockSpec((nb, dd), lambda j: (0, 0)), pl.BlockSpec((dd, tn), lambda j: (0, j)),
                  pl.BlockSpec((1, tn), lambda j: (0, j))],
        out_specs=pl.BlockSpec((nb, tn), lambda j: (0, j)),
        out_shape=jax.ShapeDtypeStruct((nb, ncol), F32),
        compiler_params=_params(("parallel",)),
    )(c_all, w_ada, b_ada_cols)


def _ada_bwd(c_all_t, dmod_cols, name):
    dd, nb = c_all_t.shape
    ncol = dmod_cols.shape[1]
    tr = _tile(dd, 256)

    def body(c_ref, d_ref, o_ref):
        cv = c_ref[...]
        act = (cv * _sigmoid(cv)).astype(BF16).astype(F32)
        dm = d_ref[...].astype(BF16).astype(F32)
        acc = act[:, 0:1] * dm[0:1, :]
        for bi in range(1, nb):
            acc = acc + act[:, bi:bi + 1] * dm[bi:bi + 1, :]
        o_ref[...] = acc

    return pl.pallas_call(
        body, name=name, grid=(dd // tr,),
        in_specs=[pl.BlockSpec((tr, nb), lambda i: (i, 0)), pl.BlockSpec((nb, ncol), lambda i: (0, 0))],
        out_specs=pl.BlockSpec((tr, ncol), lambda i: (i, 0)),
        out_shape=jax.ShapeDtypeStruct((dd, ncol), F32),
        compiler_params=_params(("parallel",)),
    )(c_all_t, dmod_cols)


def _norm_mod(x, gain, mod, row0, name):
    t, dd = x.shape
    tt = _tile(t, 512)

    def body(x_ref, g_ref, m_ref, h_ref):
        xv = x_ref[...]
        r = lax.rsqrt(jnp.mean(xv * xv, axis=-1, keepdims=True) + EPS)
        shift, scale = m_ref[row0:row0 + 1, :], m_ref[row0 + 1:row0 + 2, :]
        h_ref[...] = ((xv * r * g_ref[...]) * (1.0 + scale) + shift).astype(BF16)

    return pl.pallas_call(
        body, name=name, grid=(t // tt,),
        in_specs=[pl.BlockSpec((tt, dd), lambda i: (i, 0)), pl.BlockSpec((1, dd), lambda i: (0, 0)),
                  pl.BlockSpec((8, dd), lambda i: (0, 0))],
        out_specs=pl.BlockSpec((tt, dd), lambda i: (i, 0)),
        out_shape=jax.ShapeDtypeStruct((t, dd), BF16),
        compiler_params=_params(("parallel",)),
    )(x, gain, mod)


def _norm_mod_bwd(x, gain, mod, row0, dh, dres, branch, name, jobs=()):
    t, dd = x.shape
    tt = _tile(t, 512)
    has_branch = branch is not None

    def body(*refs):
        if has_branch:
            x_ref, g_ref, m_ref, dh_ref, dres_ref, o_ref, dx_ref, st_ref, do_ref = refs[:-1]
        else:
            x_ref, g_ref, m_ref, dh_ref, dres_ref, dx_ref, st_ref = refs[:-1]

        acc_ref = refs[-1]
        step_i = pl.program_id(0)

        @pl.when(step_i == 0)
        def _():
            acc_ref[...] = jnp.zeros_like(acc_ref)

        def fold(v):
            out = v[0:8]
            for q in range(1, rc // 8):
                out = out + v[8 * q:8 * q + 8]
            return out

        def chunk(ci, carry):
            rows = pl.ds(pl.multiple_of(ci * rc, rc), rc)
            xv = x_ref[rows, :]
            g = g_ref[...]
            r = lax.rsqrt(jnp.mean(xv * xv, axis=-1, keepdims=True) + EPS)
            nv = xv * r
            dhv = dh_ref[rows, :].astype(F32)
            acc_ref[0] += fold(dhv)
            acc_ref[1] += fold(dhv * (nv * g))
            dng = dhv * (1.0 + m_ref[row0 + 1:row0 + 2, :])
            acc_ref[2] += fold(dng * nv)
            dn = dng * g
            dx = dres_ref[rows, :] + r * (dn - nv * jnp.mean(dn * nv, axis=-1, keepdims=True))
            dx_ref[rows, :] = dx
            if has_branch:
                acc_ref[3] += fold(dx * o_ref[rows, :].astype(F32))
                do_ref[rows, :] = (dx * m_ref[branch[1]:branch[1] + 1, :]).astype(BF16)
            return carry

        lax.fori_loop(0, tt // rc, chunk, 0, unroll=4)

        @pl.when(step_i == nsteps - 1)
        def _():
            st_ref[...] = jnp.zeros_like(st_ref)
            for k in range(4):
                st_ref[k:k + 1, :] = jnp.sum(acc_ref[k], axis=0, keepdims=True)

    rc = 16 if tt % 16 == 0 else 8
    nsteps = t // tt
    row = pl.BlockSpec((tt, dd), lambda i: (i, 0))
    in_specs = [row, pl.BlockSpec((1, dd), lambda i: (0, 0)), pl.BlockSpec((8, dd), lambda i: (0, 0)), row, row]
    args = [x, gain, mod, dh, dres]
    out_specs = [row, pl.BlockSpec((8, dd), lambda i: (0, 0))]
    out_shape = [jax.ShapeDtypeStruct((t, dd), F32), jax.ShapeDtypeStruct((8, dd), F32)]
    if has_branch:
        in_specs.append(row)
        args.append(branch[0])
        out_specs.append(row)
        out_shape.append(jax.ShapeDtypeStruct((t, dd), BF16))
    res, jres = _call(body, name=name, grid=(t // tt,), in_specs=in_specs, out_specs=out_specs, out_shape=out_shape,
                      scratch_shapes=[pltpu.VMEM((4, 8, dd), F32)], sem=("arbitrary",), args=args, jobs=jobs)
    return (res, jres) if jobs else res


def _final_loss(x2, o2, final_g, target, mod, name):
    t, dd = x2.shape
    tt = _tile(t, 256)
    nsteps = t // tt

    def body(x_ref, o_ref, g_ref, t_ref, m_ref, loss_ref, dx_ref, do_ref, st_ref, lacc):
        i = pl.program_id(0)

        @pl.when(i == 0)
        def _():
            st_ref[...] = jnp.zeros_like(st_ref)
            lacc[...] = jnp.zeros_like(lacc)

        xv = x_ref[...]
        g = g_ref[...]
        r = lax.rsqrt(jnp.mean(xv * xv, axis=-1, keepdims=True) + EPS)
        nv = xv * r
        err = nv * g - t_ref[...]
        lacc[...] += jnp.sum(err * err, axis=0, keepdims=True)
        dy = err * (1.0 / dd)
        st_ref[0:1, :] += jnp.sum(dy * nv, axis=0, keepdims=True)
        dn = dy * g
        dx = r * (dn - nv * jnp.mean(dn * nv, axis=-1, keepdims=True))
        dx_ref[...] = dx
        st_ref[1:2, :] += jnp.sum(dx * o_ref[...].astype(F32), axis=0, keepdims=True)
        do_ref[...] = (dx * m_ref[5:6, :]).astype(BF16)

        @pl.when(i == nsteps - 1)
        def _():
            total = jnp.sum(lacc[...], axis=-1, keepdims=True) * (0.5 / dd)
            loss_ref[...] = jnp.broadcast_to(total, loss_ref.shape)

    row = pl.BlockSpec((tt, dd), lambda i: (i, 0))
    return pl.pallas_call(
        body, name=name, grid=(nsteps,),
        in_specs=[row, row, pl.BlockSpec((1, dd), lambda i: (0, 0)), row, pl.BlockSpec((8, dd), lambda i: (0, 0))],
        out_specs=[pl.BlockSpec((8, LANE), lambda i: (0, 0)), row, row, pl.BlockSpec((8, dd), lambda i: (0, 0))],
        out_shape=[jax.ShapeDtypeStruct((8, LANE), F32), jax.ShapeDtypeStruct((t, dd), F32),
                   jax.ShapeDtypeStruct((t, dd), BF16), jax.ShapeDtypeStruct((8, dd), F32)],
        scratch_shapes=[pltpu.VMEM((1, dd), F32)],
        compiler_params=_params(("arbitrary",)),
    )(x2, o2, final_g, target, mod)


def _lru_gates(xt, prev8, lp, wa_ref, wx_ref, first_tile, cw):
    tt = xt.shape[0]
    ext = jnp.concatenate([prev8, xt], axis=0)
    xs = [pltpu.roll(ext, s, 0)[8:, :] if s else xt for s in range(CONV_WIDTH)]
    xc = lp[0:1, :] + lp[7:8, :] * xs[0] + lp[6:7, :] * xs[1] + lp[5:6, :] * xs[2] + lp[4:5, :] * xs[3]
    xcb = xc.astype(BF16)
    za, zi = [], []
    for jb in range(cw // LRU_BLOCK_W):
        sl = slice(jb * LRU_BLOCK_W, (jb + 1) * LRU_BLOCK_W)
        za.append(jnp.dot(xcb[:, sl], wa_ref[jb].astype(BF16), preferred_element_type=F32))
        zi.append(jnp.dot(xcb[:, sl], wx_ref[jb].astype(BF16), preferred_element_type=F32))
    ra = _sigmoid(jnp.concatenate(za, axis=1) + lp[1:2, :])
    ri = _sigmoid(jnp.concatenate(zi, axis=1) + lp[2:3, :])
    sp = _softplus(-lp[3:4, :])
    log_a = -LRU_C * ra * sp
    av = jnp.exp(log_a)
    seq_start = jnp.logical_and(first_tile, lax.broadcasted_iota(jnp.int32, (tt, 1), 0) == 0)
    m2 = _neg_expm1(2.0 * log_a, av * av)
    inv_mult = jnp.where(seq_start, 1.0, lax.rsqrt(jnp.maximum(m2, 1e-30)))
    mult = jnp.where(seq_start, 1.0, m2 * inv_mult)
    return xs, xc, xcb, ra, ri, sp, av, mult, inv_mult, seq_start


def _scan_groups(a, u, reverse):
    tt, cw = a.shape
    a3, u3 = a.reshape(tt // 8, 8, cw), u.reshape(tt // 8, 8, cw)
    r = lax.broadcasted_iota(jnp.int32, (1, 8, 1), 1)
    for s in (1, 2, 4):
        shift = 8 - s if reverse else s
        take = r < 8 - s if reverse else r >= s
        u3 = jnp.where(take, a3 * pltpu.roll(u3, shift, 1) + u3, u3)
        a3 = jnp.where(take, a3 * pltpu.roll(a3, shift, 1), a3)
    return a3, u3


def _scan_carry(a_s, u_s, h_s, carry, reverse):
    ng = a_s.shape[0]

    def step(i, h):
        g = ng - 1 - i if reverse else i
        hg = u_s[g] + a_s[g] * h
        h_s[g] = hg
        return hg[0:1, :] if reverse else hg[7:8, :]

    return lax.fori_loop(0, ng, step, carry, unroll=4)


def _lru_fwd(proj, lp, wa, wx, name, jobs=()):
    t = proj.shape[0]
    w = D_MODEL
    cw = LRU_CW
    tt = _tile(t, 256)
    nb = cw // LRU_BLOCK_W

    def body(lx_ref, lg_ref, lp_ref, wa_ref, wx_ref, rec_ref, ya_ref, a_s, u_s, h_s, halo, carry):
        ti = pl.program_id(1)

        @pl.when(ti == 0)
        def _():
            halo[...] = jnp.zeros_like(halo)
            carry[...] = jnp.zeros_like(carry)

        xt = lx_ref[...].astype(F32)
        lp_v = lp_ref[...]
        _, xc, _, _, ri, _, av, mult, _, _ = _lru_gates(xt, halo[...], lp_v, wa_ref, wx_ref, ti == 0, cw)
        halo[...] = xt[tt - 8:, :]
        a_s[...], u_s[...] = _scan_groups(av, mult * (ri * xc), False)
        carry[...] = _scan_carry(a_s, u_s, h_s, carry[...], False)
        rec = h_s[...].reshape(tt, cw)
        rec_ref[...] = rec.astype(BF16)
        gelu, _ = _gelu_tanh(lg_ref[...].astype(F32))
        ya_ref[...] = (rec * gelu).astype(BF16)

    off_g = OFF_LG // cw
    res, jres = _call(
        body, name=name, grid=(w // cw, t // tt),
        in_specs=[pl.BlockSpec((tt, cw), lambda ci, ti: (ti, ci)),
                  pl.BlockSpec((tt, cw), lambda ci, ti: (ti, off_g + ci)),
                  pl.BlockSpec((8, cw), lambda ci, ti: (0, ci)),
                  pl.BlockSpec((nb, LRU_BLOCK_W, LRU_BLOCK_W), lambda ci, ti: (ci, 0, 0)),
                  pl.BlockSpec((nb, LRU_BLOCK_W, LRU_BLOCK_W), lambda ci, ti: (ci, 0, 0))],
        out_specs=[pl.BlockSpec((tt, cw), lambda ci, ti: (ti, ci))] * 2,
        out_shape=[jax.ShapeDtypeStruct((t, w), BF16)] * 2,
        scratch_shapes=[pltpu.VMEM((tt // 8, 8, cw), F32)] * 3 + [pltpu.VMEM((8, cw), F32), pltpu.VMEM((1, cw), F32)],
        sem=("parallel", "arbitrary"), args=[proj, proj, lp, wa, wx], jobs=jobs)
    return (res, jres) if jobs else res


def _lru_bwd(proj, rec, dya, lp, wa, wx, name, jobs=()):
    t = proj.shape[0]
    w = D_MODEL
    cw = LRU_CW
    tt = _tile(t, 256)
    nt = t // tt
    nb = cw // LRU_BLOCK_W
    r8 = tt // 8

    def body(lx_ref, lxp_ref, lg_ref, rec_ref, recp_ref, dya_ref, lp_ref, wa_ref, wx_ref,
             dlx_ref, dlg_ref, st_ref, dwa_ref, dwx_ref, a_s, d_s, dh_s, nhalo, carry):
        step_i = pl.program_id(1)
        ti = nt - 1 - step_i

        @pl.when(step_i == 0)
        def _():
            st_ref[...] = jnp.zeros_like(st_ref)
            dwa_ref[...] = jnp.zeros_like(dwa_ref)
            dwx_ref[...] = jnp.zeros_like(dwx_ref)
            nhalo[...] = jnp.zeros_like(nhalo)
            carry[...] = jnp.zeros_like(carry)

        first = ti == 0
        keep = jnp.where(first, 0.0, 1.0)
        xt = lx_ref[...].astype(F32)
        prev8 = lxp_ref[...].astype(F32) * keep
        lp_v = lp_ref[...]
        xs, xc, xcb, ra, ri, sp, av, mult, inv_mult, seq_start = _lru_gates(xt, prev8, lp_v, wa_ref, wx_ref, first, cw)

        lg = lg_ref[...].astype(F32)
        gelu, th = _gelu_tanh(lg)
        dyav = dya_ref[...].astype(F32)
        recv = rec_ref[...].astype(F32)
        dlg_ref[...] = (dyav * recv * _gelu_tanh_grad(lg, th)).astype(BF16)

        drec = dyav * gelu
        e_in = carry[...]
        a_s[...], d_s[...] = _scan_groups(av, av * drec, True)
        carry[...] = _scan_carry(a_s, d_s, dh_s, e_in, True)
        e_next = jnp.concatenate([dh_s[...].reshape(tt, cw), jnp.broadcast_to(e_in, (8, cw))], axis=0)
        dh = drec + pltpu.roll(e_next, tt + 7, 0)[:tt, :]
        rprev8 = recp_ref[...].astype(F32) * keep
        hprev = pltpu.roll(jnp.concatenate([rprev8, recv], axis=0), 1, 0)[8:, :]
        da = dh * hprev
        dmult = jnp.where(seq_start, 0.0, dh * ri * xc)
        dri = dh * mult * xc
        dxc = dh * mult * ri
        dlog_a = da * av - dmult * (av * av) * inv_mult
        dra = dlog_a * (-LRU_C * sp)
        lam = lp_v[3:4, :]
        st_ref[3:4, :] += jnp.sum(dlog_a * ra, axis=0, keepdims=True) * (LRU_C * _sigmoid(-lam))
        dza = dra * ra * (1.0 - ra)
        dzi = dri * ri * (1.0 - ri)
        st_ref[1:2, :] += jnp.sum(dza, axis=0, keepdims=True)
        st_ref[2:3, :] += jnp.sum(dzi, axis=0, keepdims=True)
        dzab, dzib = dza.astype(BF16), dzi.astype(BF16)
        back = []
        for jb in range(nb):
            sl = slice(jb * LRU_BLOCK_W, (jb + 1) * LRU_BLOCK_W)
            dwa_ref[jb] += lax.dot_general(xcb[:, sl], dzab[:, sl], (_DIMS["tn"], ((), ())), preferred_element_type=F32)
            dwx_ref[jb] += lax.dot_general(xcb[:, sl], dzib[:, sl], (_DIMS["tn"], ((), ())), preferred_element_type=F32)
            back.append(
                lax.dot_general(dzab[:, sl], wa_ref[jb].astype(BF16), (_DIMS["nt"], ((), ())), preferred_element_type=F32)
                + lax.dot_general(dzib[:, sl], wx_ref[jb].astype(BF16), (_DIMS["nt"], ((), ())), preferred_element_type=F32))
        dxc = dxc + jnp.concatenate(back, axis=1)
        st_ref[0:1, :] += jnp.sum(dxc, axis=0, keepdims=True)
        for k in range(CONV_WIDTH):
            st_ref[4 + k:5 + k, :] += jnp.sum(dxc * xs[CONV_WIDTH - 1 - k], axis=0, keepdims=True)
        ext = jnp.concatenate([dxc, nhalo[...]], axis=0)
        dlx = lp_v[7:8, :] * dxc
        for s in range(1, CONV_WIDTH):
            dlx = dlx + lp_v[7 - s:8 - s, :] * pltpu.roll(ext, tt + 8 - s, 0)[:tt, :]
        dlx_ref[...] = dlx.astype(BF16)
        nhalo[...] = dxc[0:8, :]

    off_g = OFF_LG // cw
    tile = lambda off: pl.BlockSpec((tt, cw), lambda ci, si: (nt - 1 - si, off + ci))
    prev = lambda off: pl.BlockSpec((8, cw), lambda ci, si: (jnp.maximum((nt - 1 - si) * r8 - 1, 0), off + ci))
    wspec = pl.BlockSpec((nb, LRU_BLOCK_W, LRU_BLOCK_W), lambda ci, si: (ci, 0, 0))
    st_spec = pl.BlockSpec((8, cw), lambda ci, si: (0, ci))
    res, jres = _call(
        body, name=name, grid=(w // cw, nt),
        in_specs=[tile(0), prev(0), tile(off_g), tile(0), prev(0), tile(0), st_spec, wspec, wspec],
        out_specs=[tile(0), tile(0), st_spec, wspec, wspec],
        out_shape=[jax.ShapeDtypeStruct((t, w), BF16), jax.ShapeDtypeStruct((t, w), BF16),
                   jax.ShapeDtypeStruct((8, w), F32),
                   jax.ShapeDtypeStruct((LRU_BLOCKS, LRU_BLOCK_W, LRU_BLOCK_W), F32),
                   jax.ShapeDtypeStruct((LRU_BLOCKS, LRU_BLOCK_W, LRU_BLOCK_W), F32)],
        scratch_shapes=[pltpu.VMEM((tt // 8, 8, cw), F32)] * 3 + [pltpu.VMEM((8, cw), F32), pltpu.VMEM((1, cw), F32)],
        sem=("parallel", "arbitrary"), args=[proj, proj, proj, rec, rec, dya, lp, wa, wx], jobs=jobs)
    return (res, jres) if jobs else res


def _band_valid(first_block):
    qi = lax.broadcasted_iota(jnp.int32, (BLOCK, 2 * BLOCK), 0)
    ki = lax.broadcasted_iota(jnp.int32, (BLOCK, 2 * BLOCK), 1)
    rel = qi + BLOCK - ki
    valid = jnp.logical_and(rel >= 0, rel < BLOCK)
    return jnp.logical_and(valid, jnp.logical_or(ki >= BLOCK, jnp.logical_not(first_block)))


ATTN_STACK_BWD = 8


def _low_half():
    return lax.broadcasted_iota(jnp.int32, (1, LANE), 1) < HEAD_DIM


def _stack_heads(x, h0, ns):
    low = _low_half()
    parts = []
    for g in range(ns):
        h = h0 + g
        slab = x[:, (h // 2) * LANE:(h // 2 + 1) * LANE]
        parts.append(jnp.where(low if h % 2 == 0 else jnp.logical_not(low), slab, jnp.zeros_like(slab)))
    return jnp.concatenate(parts, axis=0)


def _unstack_heads(y, ns):
    low = _low_half()
    pairs = [jnp.where(low, y[(2 * j) * BLOCK:(2 * j + 1) * BLOCK], y[(2 * j + 1) * BLOCK:(2 * j + 2) * BLOCK])
             for j in range(ns // 2)]
    return pairs[0] if len(pairs) == 1 else jnp.concatenate(pairs, axis=1)


def _dup_kv(xall, kv):
    part = xall[:, kv * HEAD_DIM:(kv + 1) * HEAD_DIM]
    return jnp.concatenate([part, part], axis=1)


def _fold_halves(a):
    return a + pltpu.roll(a, HEAD_DIM, 1)


def _group_probs(qg, k2, bias_g, sink_g, lse_g, valid):
    s = lax.dot_general(qg, k2, (_DIMS["nt"], ((), ())), preferred_element_type=F32) * (HEAD_DIM ** -0.5)
    s = jnp.where(valid[None], s.reshape(bias_g.shape) + bias_g, NEG_INF)
    return jnp.exp(s - lse_g), jnp.exp(sink_g - lse_g)


def _group_sinks(s_ref, h0, ns):
    return jnp.concatenate([jnp.full((1, BLOCK, 1), s_ref[h0 + g], F32) for g in range(ns)], axis=0)


def _attn_specs(t):
    nblk = t // BLOCK
    q_spec = pl.BlockSpec((BLOCK, D_MODEL), lambda n: (n, OFF_Q // D_MODEL))
    kc = pl.BlockSpec((BLOCK, KV_WIDTH), lambda n: (n, OFF_K // KV_WIDTH))
    kp = pl.BlockSpec((BLOCK, KV_WIDTH), lambda n: (jnp.maximum(n - 1, 0), OFF_K // KV_WIDTH))
    vc = pl.BlockSpec((BLOCK, KV_WIDTH), lambda n: (n, OFF_V // KV_WIDTH))
    vp = pl.BlockSpec((BLOCK, KV_WIDTH), lambda n: (jnp.maximum(n - 1, 0), OFF_V // KV_WIDTH))
    bias_spec = pl.BlockSpec((N_Q_HEADS, BLOCK, 2 * BLOCK), lambda n: (0, 0, 0))
    sink_spec = pl.BlockSpec(memory_space=pltpu.SMEM)
    return nblk, [q_spec, kp, kc, vp, vc, bias_spec, sink_spec]


def _attn_fwd(proj, bias, sinks, name, jobs=()):
    t = proj.shape[0]
    nblk, in_specs = _attn_specs(t)

    def body(q_ref, kp_ref, kc_ref, vp_ref, vc_ref, b_ref, s_ref, o_ref, lse_ref):
        valid = _band_valid(pl.program_id(0) == 0)
        lane = lax.broadcasted_iota(jnp.int32, (1, LANE), 1)
        q = q_ref[...]
        kall = jnp.concatenate([kp_ref[...], kc_ref[...]], axis=0)
        vall = jnp.concatenate([vp_ref[...], vc_ref[...]], axis=0)
        outs = []
        lse = jnp.zeros((BLOCK, LANE), F32)
        for h in range(N_Q_HEADS):
            kv = h // GROUP
            kk = kall[:, kv * HEAD_DIM:(kv + 1) * HEAD_DIM]
            vv = vall[:, kv * HEAD_DIM:(kv + 1) * HEAD_DIM]
            s = lax.dot_general(q[:, h * HEAD_DIM:(h + 1) * HEAD_DIM], kk, (_DIMS["nt"], ((), ())),
                                preferred_element_type=F32) * (HEAD_DIM ** -0.5)
            s = jnp.where(valid, s + b_ref[h], NEG_INF)
            sink = s_ref[h]
            m = jnp.maximum(jnp.max(s, axis=-1, keepdims=True), sink)
            e = jnp.exp(s - m)
            den = jnp.sum(e, axis=-1, keepdims=True) + jnp.exp(sink - m)
            lse = jnp.where(lane == h, m + jnp.log(den), lse)
            outs.append(jnp.dot((e * (1.0 / den)).astype(BF16), vv, preferred_element_type=F32))
        o_ref[...] = jnp.concatenate(outs, axis=1).astype(BF16)
        lse_ref[...] = lse

    res, jres = _call(
        body, name=name, grid=(nblk,), in_specs=in_specs,
        out_specs=[pl.BlockSpec((BLOCK, D_MODEL), lambda n: (n, 0)), pl.BlockSpec((BLOCK, LANE), lambda n: (n, 0))],
        out_shape=[jax.ShapeDtypeStruct((t, D_MODEL), BF16), jax.ShapeDtypeStruct((t, LANE), F32)],
        sem=("parallel",), args=[proj, proj, proj, proj, proj, bias, sinks], jobs=jobs)
    return (res, jres) if jobs else res


def _attn_bwd(proj, datt, lse, bias, sinks, name, jobs=()):
    t = proj.shape[0]
    nblk, in_specs = _attn_specs(t)
    in_specs = in_specs + [pl.BlockSpec((BLOCK, D_MODEL), lambda n: (n, 0)), pl.BlockSpec((BLOCK, LANE), lambda n: (n, 0))]
    scale = HEAD_DIM ** -0.5

    def body(q_ref, kp_ref, kc_ref, vp_ref, vc_ref, b_ref, s_ref, do_ref, lse_ref,
             dq_ref, dk_ref, dv_ref, db_ref, ds_ref):
        n = pl.program_id(0)

        @pl.when(n == 0)
        def _():
            dk_ref[...] = jnp.zeros_like(dk_ref)
            dv_ref[...] = jnp.zeros_like(dv_ref)
            db_ref[...] = jnp.zeros_like(db_ref)
            ds_ref[...] = jnp.zeros_like(ds_ref)

        valid = _band_valid(n == 0)
        lane = lax.broadcasted_iota(jnp.int32, (1, LANE), 1)
        q = q_ref[...]
        do = do_ref[...]
        lse_all = lse_ref[...]
        kall = jnp.concatenate([kp_ref[...], kc_ref[...]], axis=0)
        vall = jnp.concatenate([vp_ref[...], vc_ref[...]], axis=0)
        low = _low_half()
        dqs, dks, dvs = [], [], []
        dsink_row = jnp.zeros((1, LANE), F32)
        ns = ATTN_STACK_BWD
        for kv in range(N_KV_HEADS):
            k2, v2 = _dup_kv(kall, kv), _dup_kv(vall, kv)
            dk_acc = jnp.zeros((2 * BLOCK, LANE), F32)
            dv_acc = jnp.zeros((2 * BLOCK, LANE), F32)
            for h0 in range(kv * GROUP, (kv + 1) * GROUP, ns):
                qg, dog = _stack_heads(q, h0, ns), _stack_heads(do, h0, ns)
                lse_g = jnp.concatenate([lse_all[:, h0 + g:h0 + g + 1][None] for g in range(ns)], axis=0)
                p, psink = _group_probs(qg, k2, b_ref[h0:h0 + ns], _group_sinks(s_ref, h0, ns), lse_g, valid)
                dp = lax.dot_general(dog, v2, (_DIMS["nt"], ((), ())), preferred_element_type=F32)
                dp = dp.reshape(ns, BLOCK, 2 * BLOCK)
                delta = jnp.sum(p * dp, axis=-1, keepdims=True)
                ds = p * (dp - delta)
                db_ref[h0:h0 + ns] += ds
                dsink = -jnp.sum(psink * delta, axis=1, keepdims=True)
                for g in range(ns):
                    dsink_row = dsink_row + jnp.where(lane == h0 + g, dsink[g], 0.0)
                dsb = ds.reshape(ns * BLOCK, 2 * BLOCK).astype(BF16)
                pb = p.reshape(ns * BLOCK, 2 * BLOCK).astype(BF16)
                dqs.append(_unstack_heads(jnp.dot(dsb, k2, preferred_element_type=F32) * scale, ns))
                dk_acc = dk_acc + lax.dot_general(dsb, qg, (_DIMS["tn"], ((), ())), preferred_element_type=F32)
                dv_acc = dv_acc + lax.dot_general(pb, dog, (_DIMS["tn"], ((), ())), preferred_element_type=F32)
            dks.append(_fold_halves(dk_acc) * scale)
            dvs.append(_fold_halves(dv_acc))
        dq_ref[...] = jnp.concatenate(dqs, axis=1).astype(BF16)
        rows = pl.ds(pl.multiple_of(n * BLOCK, BLOCK), 2 * BLOCK)
        dk_ref[rows, :] += jnp.concatenate([jnp.where(low, dks[0], dks[1]), jnp.where(low, dks[2], dks[3])], axis=1)
        dv_ref[rows, :] += jnp.concatenate([jnp.where(low, dvs[0], dvs[1]), jnp.where(low, dvs[2], dvs[3])], axis=1)
        ds_ref[0:1, :] += dsink_row

    res, jres = _call(
        body, name=name, grid=(nblk,), in_specs=in_specs,
        out_specs=[pl.BlockSpec((BLOCK, D_MODEL), lambda n: (n, 0)),
                   pl.BlockSpec((t + BLOCK, KV_WIDTH), lambda n: (0, 0)),
                   pl.BlockSpec((t + BLOCK, KV_WIDTH), lambda n: (0, 0)),
                   pl.BlockSpec((N_Q_HEADS, BLOCK, 2 * BLOCK), lambda n: (0, 0, 0)),
                   pl.BlockSpec((8, LANE), lambda n: (0, 0))],
        out_shape=[jax.ShapeDtypeStruct((t, D_MODEL), BF16),
                   jax.ShapeDtypeStruct((t + BLOCK, KV_WIDTH), F32),
                   jax.ShapeDtypeStruct((t + BLOCK, KV_WIDTH), F32),
                   jax.ShapeDtypeStruct((N_Q_HEADS, BLOCK, 2 * BLOCK), F32),
                   jax.ShapeDtypeStruct((8, LANE), F32)],
        sem=("arbitrary",), args=[proj, proj, proj, proj, proj, bias, sinks, datt, lse], jobs=jobs)
    return (res, jres) if jobs else res


def _bucket_table():
    qi = np.arange(BLOCK)[:, None]
    ki = np.arange(2 * BLOCK)[None, :]
    rel = np.maximum(qi + BLOCK - ki, 0)
    max_exact = N_BUCKETS // 2
    relf = np.maximum(rel, 1).astype(np.float32)
    large = max_exact + (np.log(relf / max_exact) / math.log(MAX_DISTANCE / max_exact)
                         * (N_BUCKETS - max_exact)).astype(np.int32)
    large = np.minimum(large, N_BUCKETS - 1)
    return np.where(rel < max_exact, rel, large).astype(np.int32)


def _rel_bias_fwd(rel_bias, onehot, name):
    n = onehot.shape[1]
    tn = _tile(n, 8192, LANE)

    def body(r_ref, o_ref, out_ref):
        out_ref[...] = lax.dot_general(r_ref[...], o_ref[...], (_DIMS["tn"], ((), ())),
                                       precision=lax.Precision.HIGHEST, preferred_element_type=F32)

    return pl.pallas_call(
        body, name=name, grid=(n // tn,),
        in_specs=[pl.BlockSpec(rel_bias.shape, lambda j: (0, 0)), pl.BlockSpec((N_BUCKETS, tn), lambda j: (0, j))],
        out_specs=pl.BlockSpec((N_Q_HEADS, tn), lambda j: (0, j)),
        out_shape=jax.ShapeDtypeStruct((N_Q_HEADS, n), F32),
        compiler_params=_params(("parallel",)),
    )(rel_bias, onehot)


def _rel_bias_bwd(dbias, onehot, name):
    def body(d_ref, o_ref, out_ref):
        out_ref[...] = lax.dot_general(d_ref[...], o_ref[...], (_DIMS["nt"], ((), ())),
                                       precision=lax.Precision.HIGHEST, preferred_element_type=F32)

    full = pl.BlockSpec(dbias.shape, lambda: (0, 0))
    return pl.pallas_call(
        body, name=name, in_specs=[full, full],
        out_specs=pl.BlockSpec((N_Q_HEADS, N_BUCKETS), lambda: (0, 0)),
        out_shape=jax.ShapeDtypeStruct((N_Q_HEADS, N_BUCKETS), F32),
        compiler_params=_params(()),
    )(dbias, onehot)


def _part_specs(gparts, part_index, tr, cc):
    blk = pl.BlockSpec((tr, cc), lambda i, *_: (i, 0))
    specs = []
    for i, _ in enumerate(gparts):
        if i in part_index:
            ix = part_index[i]
            if isinstance(ix, int):
                specs.append(pl.BlockSpec((None, tr, cc), functools.partial(lambda i, *_, ix: (ix, i, 0), ix=ix)))
            else:
                specs.append(pl.BlockSpec((None, tr, cc), lambda i, ix_ref: (ix_ref[0], i, 0)))
        else:
            specs.append(blk)
    return blk, specs


def _sum_parts(gparts, part_index, name):
    r, cc = gparts[0].shape[-2:]
    tr = _tile(r, 256)
    dyn = [i for i, ix in part_index.items() if not isinstance(ix, int)]
    assert len(dyn) <= 1

    def body(*refs):
        refs = refs[len(dyn):]
        g = refs[0][...].astype(F32)
        for gr in refs[1:-1]:
            g = g + gr[...].astype(F32)
        refs[-1][...] = g

    blk, specs = _part_specs(gparts, part_index, tr, cc)
    grid_spec = pltpu.PrefetchScalarGridSpec(num_scalar_prefetch=len(dyn), grid=(r // tr,), in_specs=specs, out_specs=blk)
    return pl.pallas_call(
        body, name=name, grid_spec=grid_spec, out_shape=jax.ShapeDtypeStruct((r, cc), F32),
        compiler_params=_params(("parallel",)),
    )(*[part_index[i] for i in dyn], *gparts)


def _adamw(w, m, v, gparts, name, part_index=None):
    r, cc = w.shape
    tr = _tile(r, 128)
    np_ = len(gparts)
    part_index = part_index or {}
    dyn = [i for i, ix in part_index.items() if not isinstance(ix, int)]
    assert len(dyn) <= 1
    bc1 = 1.0 / (1.0 - ADAM_B1 ** ADAM_STEP)
    bc2 = 1.0 / (1.0 - ADAM_B2 ** ADAM_STEP)

    def body(*refs):
        refs = refs[len(dyn):]
        w_ref, m_ref, v_ref = refs[:3]
        g_refs = refs[3:3 + np_]
        g_out, d_out, m_out, v_out = refs[3 + np_:]
        g = g_refs[0][...].astype(F32)
        for gr in g_refs[1:]:
            g = g + gr[...].astype(F32)
        mn = ADAM_B1 * m_ref[...] + (1.0 - ADAM_B1) * g
        vn = ADAM_B2 * v_ref[...] + (1.0 - ADAM_B2) * (g * g)
        g_out[...] = g
        m_out[...] = mn
        v_out[...] = vn
        d_out[...] = -ADAM_LR * ((mn * bc1) / (jnp.sqrt(vn * bc2) + ADAM_EPS) + ADAM_WD * w_ref[...])

    blk, g_specs = _part_specs(gparts, part_index, tr, cc)
    grid_spec = pltpu.PrefetchScalarGridSpec(
        num_scalar_prefetch=len(dyn), grid=(r // tr,),
        in_specs=[blk, blk, blk] + g_specs, out_specs=[blk] * 4)
    return pl.pallas_call(
        body, name=name, grid_spec=grid_spec,
        out_shape=[jax.ShapeDtypeStruct((r, cc), F32)] * 4,
        compiler_params=_params(("parallel",)),
    )(*[part_index[i] for i in dyn], w, m, v, *gparts)


def _row(vec):
    return vec.reshape(1, -1)


def kernel(x, c, w_ada, b_ada, norm1_g, w_in, conv_w, conv_b, lru_wa, lru_ba, lru_wx, lru_bx, lru_lambda, w_lru_out, w_attn_out, attn_sinks, rel_bias, w_out, norm2_g, w_ff1, w_ff2, final_g, loss_target, m_w_ada, m_b_ada, m_norm1_g, m_w_in, m_conv_w, m_conv_b, m_lru_wa, m_lru_ba, m_lru_wx, m_lru_bx, m_lru_lambda, m_w_lru_out, m_w_attn_out, m_attn_sinks, m_rel_bias, m_w_out, m_norm2_g, m_w_ff1, m_w_ff2, m_final_g, v_w_ada, v_b_ada, v_norm1_g, v_w_in, v_conv_w, v_conv_b, v_lru_wa, v_lru_ba, v_lru_wx, v_lru_bx, v_lru_lambda, v_w_lru_out, v_w_attn_out, v_attn_sinks, v_rel_bias, v_w_out, v_norm2_g, v_w_ff1, v_w_ff2, v_final_g):
    dd = D_MODEL
    t = x.shape[1]
    ix, iy, ic = lax.axis_index("x"), lax.axis_index("y"), lax.axis_index("c")
    me = 4 * ix + 2 * iy + ic
    chip = 2 * ix + iy
    c_idx = jnp.reshape(ic, (1,)).astype(jnp.int32)
    chip_idx = jnp.reshape(chip, (1,)).astype(jnp.int32)

    xs = x[0]
    target = loss_target[0]
    ncol_ada = w_ada.shape[2]
    cshard = conv_w.shape[2]

    shards = [jnp.transpose(w_in[0]).astype(BF16), w_lru_out[0].astype(BF16), w_attn_out[0].astype(BF16),
              w_out[0].astype(BF16), jnp.transpose(w_ff1[0]).astype(BF16), w_ff2[0].astype(BF16)]
    s_in, s_lo, s_ao, s_o, s_f1, s_f2 = shards

    def full(g):
        return g.reshape(N_DEV * g.shape[1], dd)

    def stack(g):
        return g.reshape(N_DEV, g.shape[0] // N_DEV, dd)

    (w_in_pair,) = _run_job(_pair_job([s_in]), "ag_w_in_pair")
    w_in_pair = lax.dynamic_update_slice_in_dim(w_in_pair, s_in[None], ic, axis=0)

    pack0 = jnp.zeros((8, dd), F32).at[0:1, :].set(c).at[1:1 + CONV_WIDTH, 0:cshard].set(conv_w[0])
    (g0,) = _all_gather([pack0], "ag_cond")
    c_all = g0[:, 0, :]
    conv_w_full = jnp.transpose(g0[:, 1:1 + CONV_WIDTH, 0:cshard], (1, 0, 2)).reshape(CONV_WIDTH, dd)
    b_cols = lax.dynamic_slice(b_ada, (0, me * ncol_ada), (1, ncol_ada))
    mod_cols = _ada_fwd(c_all, w_ada[0], b_cols, "ada_fwd")
    (g1,) = _all_gather([mod_cols], "ag_mod")
    mod_mine = lax.dynamic_index_in_dim(g1, me, axis=1, keepdims=False).reshape(6, dd)
    mod = jnp.concatenate([mod_mine, jnp.zeros((2, dd), F32)], axis=0)

    bucket = _bucket_table()
    onehot = jnp.asarray((bucket.reshape(1, -1) == np.arange(N_BUCKETS)[:, None]).astype(np.float32))
    bias = _rel_bias_fwd(rel_bias, onehot, "rel_bias_fwd").reshape(N_Q_HEADS, BLOCK, 2 * BLOCK)
    sinks = attn_sinks.reshape(N_Q_HEADS)
    lp = jnp.concatenate([conv_b, lru_ba, lru_bx, lru_lambda, conv_w_full], axis=0)
    wa, wx = lru_wa[0], lru_wx[0]

    tm = _tile(t, 1024)
    tn = 512
    h = _norm_mod(xs, norm1_g, mod, 0, "norm1")
    tn2 = 1024
    chunk_w = 2 * s_in.shape[0]

    def ids(*ks):
        return jnp.stack([jnp.asarray(k, jnp.int32) for k in ks])

    tn_in = _tile(chunk_w, 1024, LANE)
    (proj,), ((g_near,),) = _mm(
        h, w_in_pair.reshape(chunk_w, dd), "nt", tm=tm, tn=tn_in, outs=[BF16], name="in_proj_own",
        chunks=(ids(chip), chunk_w, ids(0)), n_total=IN_WIDTH, jobs=[_gather_job([w_in_pair], (0.7,), "near")])
    near_ids = ids(2 * (1 - ix) + iy, 2 * ix + 1 - iy)
    (proj,), ((g_far,), (g_o,)) = _mm(
        h, full(g_near), "nt", tm=tm, tn=tn_in, outs=[BF16], name="in_proj_near",
        chunks=(near_ids, chunk_w, near_ids), into=proj,
        jobs=[_gather_job([g_near], (0.6,), "far"), _gather_job([s_o], (0.6, 0.85))])
    far_chip = 2 * (1 - ix) + 1 - iy
    (proj,), ((g_lo,),) = _mm(
        h, full(g_far), "nt", tm=tm, tn=tn_in, outs=[BF16], name="in_proj_far",
        chunks=(ids(far_chip), chunk_w, ids(far_chip)), into=proj, jobs=[_gather_job([s_lo], (0.5, 0.8))])
    far_slots = lax.dynamic_slice_in_dim(g_far, 2 * far_chip, 2, axis=0)
    win_t = lax.dynamic_update_slice_in_dim(g_near, far_slots, 2 * far_chip, axis=0)
    win_t = full(lax.dynamic_update_slice_in_dim(win_t, w_in_pair, 2 * chip, axis=0))
    (g_o,) = _fill_own([g_o], [s_o])
    (att, lse), ((g_f1,),) = _attn_fwd(proj, bias, sinks, "attn_fwd", jobs=[_gather_job([s_f1], (0.6, 0.9))])
    (rec, ya), ((g_ao,),) = _lru_fwd(proj, lp, wa, wx, "lru_fwd", jobs=[_gather_job([s_ao], (0.4, 0.7))])
    g_lo, g_ao, g_f1 = _fill_own([g_lo, g_ao, g_f1], [s_lo, s_ao, s_f1])
    wlo, wao, wo, wf1_t = full(g_lo), full(g_ao), full(g_o), full(g_f1)
    (y_a,) = _mm(ya, wlo, "nn", tm=tm, tn=tn2, outs=[BF16], name="lru_out")

    def merge_epi(acc, ya_t, ga_t, gb_t):
        return acc, _sigmoid(ga_t.astype(F32)) * ya_t.astype(F32) + _sigmoid(gb_t.astype(F32)) * acc

    y_b, merged = _mm(att, wao, "nn", tm=tm, tn=tn, outs=[BF16, BF16], epi=merge_epi,
                      tiles=[(y_a, 0), (proj, OFF_GA // tn), (proj, OFF_GB // tn)], name="attn_out_merge")

    def resid_epi(row):
        def epi(acc, x_t, mod_t):
            return acc, x_t + mod_t[row:row + 1, :] * acc
        return epi

    o1, x1 = _mm(merged, wo, "nn", tm=tm, tn=tn2, outs=[BF16, F32], epi=resid_epi(2),
                 tiles=[(xs, 0)], rows=[mod], name="out_proj")
    h2 = _norm_mod(x1, norm2_g, mod, 3, "norm2")

    def relu2_epi(acc):
        rl = jnp.maximum(acc, 0.0)
        return acc, rl * rl

    (f_pre, ff), ((g_f2,),) = _mm(h2, wf1_t, "nt", tm=tm, tn=1024, outs=[BF16, BF16], epi=relu2_epi, name="ff1",
                                  jobs=[_gather_job([s_f2], (0.6, 0.9))])
    wf2 = full(_fill_own([g_f2], [s_f2])[0])
    tmh = _tile(t, 512)
    o2, x2 = _mm(ff, wf2, "nn", tm=tmh, tn=tn, outs=[BF16, F32], epi=resid_epi(5),
                 tiles=[(x1, 0)], rows=[mod], name="ff2")

    loss_blk, dx2, do2, st_f = _final_loss(x2, o2, _row(final_g), target, mod, "final_loss")
    loss = lax.psum(loss_blk[0, 0], ("x", "y", "c"))

    def drelu2_epi(acc, f_t):
        return (acc * (2.0 * jnp.maximum(f_t.astype(F32), 0.0)),)

    tmw, tnw, tkw = 512, dd, _tile(t, 2048)
    (df,) = _mm(do2, wf2, "nt", tm=tm, tn=1024, outs=[BF16], epi=drelu2_epi, tiles=[(f_pre, 0)], name="ff2_dx")
    (dwf2,) = _mm(ff, do2, "tn", tm=tmw, tn=tnw, tk=tkw, outs=[BF16], name="ff2_dw")
    (dh2,), ((sib_f2,),) = _mm(df, wf1_t, "nn", tm=tmh, tn=tn, outs=[BF16], name="ff1_dx",
                              jobs=[_sibling_job([stack(dwf2)])])
    pair_f2 = _pair_add(stack(dwf2), sib_f2, c_idx, "rs_pair_add_w_ff2")
    (dwf1_t,), ((chips_f2,),) = _mm(df, h2, "tn", tm=tmw, tn=tnw, tk=tkw, outs=[BF16], name="ff1_dw",
                                    jobs=[_chips_job([pair_f2])])
    (dx1, st_2, do1), ((sib_f1,),) = _norm_mod_bwd(x1, norm2_g, mod, 3, dh2, dx2, (o1, 2), "norm2_bwd",
                                                   jobs=[_sibling_job([stack(dwf1_t)])])
    pair_f1 = _pair_add(stack(dwf1_t), sib_f1, c_idx, "rs_pair_add_w_ff1")

    def dmerge_epi(acc, ga_t, gb_t, ya_t, yb_t):
        sa, sb = _sigmoid(ga_t.astype(F32)), _sigmoid(gb_t.astype(F32))
        return (acc * sa, acc * sb, acc * ya_t.astype(F32) * sa * (1.0 - sa), acc * yb_t.astype(F32) * sb * (1.0 - sb))

    tmh = _tile(t, 512)
    dy_a, dy_b, dga, dgb = _mm(do1, wo, "nt", tm=tm, tn=tn, outs=[BF16] * 4, epi=dmerge_epi,
                               tiles=[(proj, OFF_GA // tn), (proj, OFF_GB // tn), (y_a, 0), (y_b, 0)], name="out_proj_dx")
    (dwo,) = _mm(merged, do1, "tn", tm=tmw, tn=tnw, tk=tkw, outs=[BF16], name="out_proj_dw")
    (datt,) = _mm(dy_b, wao, "nt", tm=tm, tn=tn2, outs=[BF16], name="attn_out_dx")
    (dwao,) = _mm(att, dy_b, "tn", tm=tmw, tn=tnw, tk=tkw, outs=[BF16], name="attn_out_dw")
    (dya,) = _mm(dy_a, wlo, "nt", tm=tm, tn=tn2, outs=[BF16], name="lru_out_dx")
    (dwlo,) = _mm(ya, dy_a, "tn", tm=tmw, tn=tnw, tk=tkw, outs=[BF16], name="lru_out_dw")
    (dq, dk_pad, dv_pad, dbias, dsinks), (sib_3, (chips_f1,)) = _attn_bwd(
        proj, datt, lse, bias, sinks, "attn_bwd",
        jobs=[_sibling_job([stack(dwlo), stack(dwao), stack(dwo)]), _chips_job([pair_f1])])
    pair_lo, pair_ao, pair_o = [_pair_add(stack(g_), s_, c_idx, "rs_pair_add_" + nm_)
                                for g_, s_, nm_ in zip((dwlo, dwao, dwo), sib_3, ("w_lru_out", "w_attn_out", "w_out"))]
    (dlx, dlg, st_l, dwa, dwx), ((chips_lo, chips_ao, chips_o),) = _lru_bwd(
        proj, rec, dya, lp, wa, wx, "lru_bwd", jobs=[_chips_job([pair_lo, pair_ao, pair_o])])
    dproj = jnp.concatenate([dlx, dlg, dq, dk_pad[BLOCK:].astype(BF16), dv_pad[BLOCK:].astype(BF16), dga, dgb], axis=1)
    lru_mats = jnp.concatenate([dwa.reshape(-1, dd), dwx.reshape(-1, dd)], axis=0)
    (dwin_t,), ((mats_all,),) = _mm(dproj, h, "tn", tm=768, tn=tnw, tk=tkw, outs=[BF16], name="in_proj_dw",
                                   jobs=[_gather_job([lru_mats], (0.3, 0.6))])
    (mats_all,) = _fill_own([mats_all], [lru_mats])
    (sib_in,) = _exchange_sibling([stack(dwin_t)], "rs_sibling_w_in")
    pair_in = _pair_add(stack(dwin_t), sib_in, c_idx, "rs_pair_add_w_in", widths=(dd // 2, dd // 2))
    n_rb = t // tm
    dx_tiles = dict(tm=tm, tn=tn2, tk=IN_WIDTH // 4, outs=[BF16])
    if n_rb >= 2:
        (dh,), ((chips_in_0,),) = _mm(dproj, win_t, "nn", name="in_proj_dx_top", rows_part=(0, n_rb // 2),
                                     jobs=[_chips_job(pair_in[0:1])], **dx_tiles)
        (dh,), ((chips_in_1,),) = _mm(dproj, win_t, "nn", name="in_proj_dx_bottom", into=dh,
                                     rows_part=(n_rb // 2, n_rb - n_rb // 2), jobs=[_chips_job(pair_in[1:2])], **dx_tiles)
    else:
        (dh,), ((chips_in_0,), (chips_in_1,)) = _mm(
            dproj, win_t, "nn", name="in_proj_dx", jobs=[_chips_job(pair_in[0:1]), _chips_job(pair_in[1:2])], **dx_tiles)
    grad_x, st_1 = _norm_mod_bwd(xs, norm1_g, mod, 0, dh, dx1, None, "norm1_bwd")
    chips_in = [chips_in_0, chips_in_1]

    drel = jnp.transpose(_rel_bias_bwd(dbias.reshape(N_Q_HEADS, -1), onehot, "rel_bias_bwd"))

    misc = jnp.concatenate([drel.reshape(1, -1), dsinks[0:1, 0:N_Q_HEADS],
                            jnp.zeros((1, dd - N_BUCKETS * N_Q_HEADS - N_Q_HEADS), F32)], axis=1)
    zero_row = jnp.zeros((1, dd), F32)
    small = jnp.concatenate([
        st_1[0:2], st_2[3:4], st_2[0:2], st_f[1:2], zero_row, zero_row,
        st_1[2:3], st_l[0:1], st_l[1:2], st_l[2:3], st_l[3:4], st_2[2:3], st_f[0:1], misc,
        st_l[4:8], jnp.zeros((4, dd), F32)], axis=0)
    (small_all,) = _all_gather([small], "ag_small")

    def pack_small(b_, n1, cb, ba, bx, lam, n2, fg, rb, sk):
        misc_ = jnp.concatenate([rb.reshape(1, -1), sk.reshape(1, -1),
                                 jnp.zeros((1, dd - N_BUCKETS * N_Q_HEADS - N_Q_HEADS), F32)], axis=1)
        return jnp.concatenate([b_.reshape(6, dd), jnp.zeros((2, dd), F32), n1, cb, ba, bx, lam, n2, _row(fg), misc_,
                                jnp.zeros((8, dd), F32)], axis=0)

    def pack_mats(wa_, wx_):
        return jnp.concatenate([wa_.reshape(-1, dd), wx_.reshape(-1, dd)], axis=0)

    every = {i: i for i in range(N_DEV)}
    w_s = pack_small(b_ada, norm1_g, conv_b, lru_ba, lru_bx, lru_lambda, norm2_g, final_g, rel_bias, attn_sinks)
    m_s = pack_small(m_b_ada, m_norm1_g, m_conv_b, m_lru_ba, m_lru_bx, m_lru_lambda, m_norm2_g, m_final_g, m_rel_bias, m_attn_sinks)
    v_s = pack_small(v_b_ada, v_norm1_g, v_conv_b, v_lru_ba, v_lru_bx, v_lru_lambda, v_norm2_g, v_final_g, v_rel_bias, v_attn_sinks)
    small_res = _adamw(w_s, m_s, v_s, [small_all] * N_DEV, "adamw_small", part_index=every)
    mats_res = _adamw(pack_mats(lru_wa, lru_wx), pack_mats(m_lru_wa, m_lru_wx), pack_mats(v_lru_wa, v_lru_wx),
                      [mats_all] * N_DEV, "adamw_lru_mats", part_index=every)

    def unpack_small(s, mt):
        nb_ = N_BUCKETS * N_Q_HEADS
        half = mt.shape[0] // 2
        return dict(
            b_ada=s[0:6].reshape(1, 6 * dd), norm1_g=s[8:9], conv_b=s[9:10], lru_ba=s[10:11], lru_bx=s[11:12],
            lru_lambda=s[12:13], norm2_g=s[13:14], final_g=s[14], rel_bias=s[15, 0:nb_].reshape(N_BUCKETS, N_Q_HEADS),
            attn_sinks=s[15:16, nb_:nb_ + N_Q_HEADS],
            lru_wa=mt[:half].reshape(1, LRU_BLOCKS, LRU_BLOCK_W, LRU_BLOCK_W),
            lru_wx=mt[half:].reshape(1, LRU_BLOCKS, LRU_BLOCK_W, LRU_BLOCK_W))

    res = {k: [None] * 4 for k in ("w_ada", "w_in", "conv_w", "w_lru_out", "w_attn_out", "w_out", "w_ff1", "w_ff2")}
    for q_, (s_, mt_) in enumerate(zip(small_res, mats_res)):
        for k_, val in unpack_small(s_, mt_).items():
            res.setdefault(k_, [None] * 4)[q_] = val

    g_conv = lax.dynamic_slice(small_res[0][16:16 + CONV_WIDTH], (0, me * cshard), (CONV_WIDTH, cshard))
    conv_res = _adamw(conv_w[0], m_conv_w[0], v_conv_w[0], [g_conv], "adamw_conv")
    res["conv_w"] = [r_[None] for r_ in conv_res]

    dmod_all = small_all[:, 0:6, :].reshape(N_DEV, 6 * dd)
    dmod_cols = lax.dynamic_slice(dmod_all, (0, me * ncol_ada), (N_DEV, ncol_ada))
    g_ada = _ada_bwd(jnp.transpose(c_all), dmod_cols, "ada_bwd")
    ada_res = _adamw(w_ada[0], m_w_ada[0], v_w_ada[0], [g_ada], "adamw_ada")
    res["w_ada"] = [r_[None] for r_ in ada_res]

    pair = [None, pair_lo, pair_ao, pair_o, pair_f1, pair_f2]
    from_chips = [None, chips_lo, chips_ao, chips_o, chips_f1, chips_f2]
    own_and_chips = {0: chip_idx, 1: 0, 2: 1, 3: 2}

    def summed(i):
        return [pair[i], from_chips[i], from_chips[i], from_chips[i]], {0: chip_idx, 1: 0, 2: 1, 3: 2}

    def sum_only(i, name):
        parts, index = summed(i)
        return _sum_parts(parts, index, name)

    g_in = jnp.transpose(jnp.concatenate(
        [_sum_parts([p_, c_, c_, c_], own_and_chips, "sum_w_in_%d" % q_)
         for q_, (p_, c_) in enumerate(zip(pair_in, chips_in))], axis=1))
    res["w_in"] = [r_[None] for r_ in _adamw(w_in[0], m_w_in[0], v_w_in[0], [g_in], "adamw_w_in")]
    g_ff1 = jnp.transpose(sum_only(4, "sum_w_ff1"))
    res["w_ff1"] = [r_[None] for r_ in _adamw(w_ff1[0], m_w_ff1[0], v_w_ff1[0], [g_ff1], "adamw_w_ff1")]
    for i, (nm, w_, m_, v_) in {1: ("w_lru_out", w_lru_out, m_w_lru_out, v_w_lru_out),
                                2: ("w_attn_out", w_attn_out, m_w_attn_out, v_w_attn_out),
                                3: ("w_out", w_out, m_w_out, v_w_out),
                                5: ("w_ff2", w_ff2, m_w_ff2, v_w_ff2)}.items():
        parts, index = summed(i)
        res[nm] = [r_[None] for r_ in _adamw(w_[0], m_[0], v_[0], parts, "adamw_" + nm, part_index=index)]

    order = ["w_ada", "b_ada", "norm1_g", "w_in", "conv_w", "conv_b", "lru_wa", "lru_ba", "lru_wx", "lru_bx",
             "lru_lambda", "w_lru_out", "w_attn_out", "attn_sinks", "rel_bias", "w_out", "norm2_g", "w_ff1",
             "w_ff2", "final_g"]
    out = [loss, grad_x[None]]
    for q_ in range(4):
        out += [res[k_][q_] for k_ in order]
    return tuple(out)
```

```python
import functools
import math

import numpy as np
import jax
import jax.numpy as jnp
from jax import lax
from jax.experimental import pallas as pl
from jax.experimental.pallas import tpu as pltpu

F32 = jnp.float32
BF16 = jnp.bfloat16
MESH = pl.DeviceIdType.MESH

D_MODEL = 2048
N_Q_HEADS = 32
N_KV_HEADS = 4
GROUP = N_Q_HEADS // N_KV_HEADS
HEAD_DIM = 64
KV_WIDTH = N_KV_HEADS * HEAD_DIM
BLOCK = 128
NEG_INF = -1e30
N_BUCKETS = 32
MAX_DISTANCE = 128
LRU_BLOCKS = 16
LRU_BLOCK_W = 128
CONV_WIDTH = 4
LRU_C = 8.0
D_FF = 4 * D_MODEL
EPS = 1e-6
IN_WIDTH = 5 * D_MODEL + 2 * KV_WIDTH
OFF_LX, OFF_LG, OFF_Q, OFF_K, OFF_V, OFF_GA, OFF_GB = 0, 2048, 4096, 6144, 6400, 6656, 8704

ADAM_LR, ADAM_B1, ADAM_B2, ADAM_EPS, ADAM_WD, ADAM_STEP = 0.001, 0.9, 0.999, 1e-08, 0.01, 10

N_DEV = 8
VMEM_LIMIT_BYTES = 48 * 1024 * 1024
LANE = 128
LRU_CW = 512


def _params(sem, **kw):
    return pltpu.CompilerParams(dimension_semantics=sem, vmem_limit_bytes=VMEM_LIMIT_BYTES, **kw)


def _tile(n, pref, mult=8):
    if n <= pref:
        return n
    t = (pref // mult) * mult
    while t >= mult:
        if n % t == 0:
            return t
        t -= mult
    return n


def _sigmoid(x):
    return 0.5 * jnp.tanh(0.5 * x) + 0.5


def _gelu_tanh(x):
    k = math.sqrt(2.0 / math.pi)
    t = jnp.tanh(k * (x + 0.044715 * x * x * x))
    return 0.5 * x * (1.0 + t), t


def _gelu_tanh_grad(x, t):
    k = math.sqrt(2.0 / math.pi)
    return 0.5 * (1.0 + t) + 0.5 * x * (1.0 - t * t) * k * (1.0 + 3.0 * 0.044715 * x * x)


def _log1p(e):
    u = 1.0 + e
    return jnp.where(u == 1.0, e, jnp.log(u) * (e / jnp.where(u == 1.0, 1.0, u - 1.0)))


def _softplus(x):
    return jnp.maximum(x, 0.0) + _log1p(jnp.exp(-jnp.abs(x)))


def _neg_expm1(x, exp_x):
    series = -x * (1.0 + x * (0.5 + x * (1.0 / 6.0 + x * (1.0 / 24.0))))
    return jnp.where(x > -0.0625, series, 1.0 - exp_x)


def _my_place():
    return lax.axis_index("x"), lax.axis_index("y"), lax.axis_index("c")


def _all_gather(arrs, name):
    return _fill_own(_run_job(_gather_job(arrs), name), arrs)


def _fill_own(stacks, shards):
    me = 4 * lax.axis_index("x") + 2 * lax.axis_index("y") + lax.axis_index("c")
    out = []
    for g, s in zip(stacks, shards):
        if g.size * g.dtype.itemsize <= (1 << 20):
            slot = lax.broadcasted_iota(jnp.int32, g.shape, 0)
            out.append(jnp.where(slot == me, s[None], g))
        else:
            out.append(lax.dynamic_update_slice_in_dim(g, s[None], me, axis=0))
    return out


class _Job:
    def __init__(self, ins, outs, sems, start, finish, forwards=(), forward_at=()):
        self.ins, self.outs, self.sems = list(ins), list(outs), list(sems)
        self.start, self.finish, self.forwards, self.forward_at = start, finish, list(forwards), list(forward_at)
        assert len(self.forwards) <= len(self.forward_at)


def _gather_job(arrs, forward_at=(0.6, 0.85), part="all"):
    n = len(arrs)
    whole, near, far = part == "all", part == "near", part == "far"

    def copies(ins, outs, sems):
        send_sems, recv_sems = sems
        x, y, c = _my_place()
        sib = (x, y, 1 - c)
        xn, yn, dg = (1 - x, y), (x, 1 - y), (1 - x, 1 - y)

        def slot(px, py, pc):
            return 4 * px + 2 * py + pc

        def cp(a, k, src, dst_slot, to):
            return pltpu.make_async_remote_copy(
                src_ref=src, dst_ref=outs[a].at[dst_slot], send_sem=send_sems.at[a, k], recv_sem=recv_sems.at[a, k],
                device_id=to, device_id_type=MESH)

        def arrival(a, k, dst_slot):
            return cp(a, k, outs[a].at[dst_slot], dst_slot, sib)

        me_slot = slot(x, y, c)
        sends, relay, passes, arrive = {}, {}, {}, {}
        for a in range(n):
            if whole:
                own = ins[a]
                sends[a, 0] = cp(a, 0, own, me_slot, sib)
            if near:
                own = ins[a].at[c]
            if whole or near:
                sends[a, 1] = cp(a, 1, own, me_slot, (*xn, c))
                sends[a, 2] = cp(a, 2, own, me_slot, (*yn, c))
                passes[a, 4] = cp(a, 4, outs[a].at[slot(*xn, c)], slot(*xn, c), sib)
                passes[a, 5] = cp(a, 5, outs[a].at[slot(*yn, c)], slot(*yn, c), sib)
            if whole or far:
                source = outs[a] if whole else ins[a]
                relayed = c * slot(*yn, c) + (1 - c) * slot(*xn, c)
                relay[a, 3] = cp(a, 3, source.at[relayed], relayed, (x + c * (1 - 2 * x), y + (1 - c) * (1 - 2 * y), c))
                passes[a, 6] = cp(a, 6, outs[a].at[slot(*dg, c)], slot(*dg, c), sib)
            arrive[a, 0] = arrival(a, 0, slot(x, y, 1 - c))
            arrive[a, 1] = arrival(a, 1, slot(*xn, c))
            arrive[a, 2] = arrival(a, 2, slot(*yn, c))
            arrive[a, 3] = arrival(a, 3, slot(*dg, c))
            arrive[a, 4] = arrival(a, 4, slot(*xn, 1 - c))
            arrive[a, 5] = arrival(a, 5, slot(*yn, 1 - c))
            arrive[a, 6] = arrival(a, 6, slot(*dg, 1 - c))
        return sends, relay, passes, arrive

    def pick(d, ks):
        return [d[a, k] for a in range(n) for k in ks if (a, k) in d]

    def start(ins, outs, sems):
        sends, relay, _, _ = copies(ins, outs, sems)
        for cp in pick(sends, (0, 1, 2)) + (pick(relay, (3,)) if far else []):
            cp.start()

    def forward_neighbours(ins, outs, sems):
        _, relay, passes, arrive = copies(ins, outs, sems)
        for cp in pick(arrive, (1, 2)):
            cp.wait_recv()
        for cp in pick(relay, (3,)) + pick(passes, (4, 5)):
            cp.start()

    def forward_diagonal(ins, outs, sems):
        _, _, passes, arrive = copies(ins, outs, sems)
        for cp in pick(arrive, (3,)):
            cp.wait_recv()
        for cp in pick(passes, (6,)):
            cp.start()

    def finish(ins, outs, sems):
        sends, relay, passes, arrive = copies(ins, outs, sems)
        for cp in pick(arrive, (6,) if far else (0, 4, 5, 6) if whole else (4, 5)):
            cp.wait_recv()
        for cp in pick(sends, (0, 1, 2)) + pick(relay, (3,)) + pick(passes, (4, 5, 6)):
            cp.wait_send()

    shapes = [a.shape[-2:] for a in arrs]
    forwards = [forward_diagonal] if far else [forward_neighbours] if near else [forward_neighbours, forward_diagonal]
    return _Job(arrs, [jax.ShapeDtypeStruct((N_DEV,) + s, a.dtype) for s, a in zip(shapes, arrs)],
                [pltpu.SemaphoreType.DMA((n, 7)), pltpu.SemaphoreType.DMA((n, 7))],
                start, finish, forwards, forward_at)


def _pair_job(arrs):
    n = len(arrs)

    def copies(ins, outs, sems):
        send_sems, recv_sems = sems
        x, y, c = _my_place()
        remote = [pltpu.make_async_remote_copy(
            src_ref=ins[a], dst_ref=outs[a].at[c], send_sem=send_sems.at[a], recv_sem=recv_sems.at[a],
            device_id=(x, y, 1 - c), device_id_type=MESH) for a in range(n)]
        arrive = [pltpu.make_async_remote_copy(
            src_ref=ins[a], dst_ref=outs[a].at[1 - c], send_sem=send_sems.at[a], recv_sem=recv_sems.at[a],
            device_id=(x, y, 1 - c), device_id_type=MESH) for a in range(n)]
        return remote, arrive

    def start(ins, outs, sems):
        for cp in copies(ins, outs, sems)[0]:
            cp.start()

    def finish(ins, outs, sems):
        remote, arrive = copies(ins, outs, sems)
        for cp in arrive:
            cp.wait_recv()
        for cp in remote:
            cp.wait_send()

    return _Job(arrs, [jax.ShapeDtypeStruct((2,) + a.shape, a.dtype) for a in arrs],
                [pltpu.SemaphoreType.DMA((n,)), pltpu.SemaphoreType.DMA((n,))], start, finish)


def _run_job(job, name):
    ni, no = len(job.ins), len(job.outs)

    def body(*refs):
        ins, outs, sems = refs[:ni], refs[ni:ni + no], refs[ni + no:]
        job.start(ins, outs, sems)
        for fwd in job.forwards:
            fwd(ins, outs, sems)
        job.finish(ins, outs, sems)

    any_spec = pl.BlockSpec(memory_space=pl.ANY)
    return pl.pallas_call(
        body, name=name, out_shape=job.outs, in_specs=[any_spec] * ni, out_specs=[any_spec] * no,
        scratch_shapes=job.sems,
    )(*job.ins)


def _call(body, *, name, grid, in_specs, out_specs, out_shape, scratch_shapes=(), sem, args, jobs=(),
          prefetch=(), aliases=None):
    in_specs, out_specs, out_shape, scratch_shapes = list(in_specs), list(out_specs), list(out_shape), list(scratch_shapes)
    prefetch = list(prefetch)
    if not jobs and not prefetch and not aliases:
        res = pl.pallas_call(body, name=name, grid=grid, in_specs=in_specs, out_specs=out_specs, out_shape=out_shape,
                             scratch_shapes=scratch_shapes, compiler_params=_params(sem))(*args)
        return list(res), []
    n_in, n_out, n_scr = len(in_specs), len(out_specs), len(scratch_shapes)
    j_in = [len(j.ins) for j in jobs]
    j_out = [len(j.outs) for j in jobs]
    j_sem = [len(j.sems) for j in jobs]
    total = int(np.prod(grid))
    any_spec = pl.BlockSpec(memory_space=pl.ANY)

    def wrapped(*refs):
        pos = [0]

        def take(k):
            part = refs[pos[0]:pos[0] + k]
            pos[0] += k
            return part

        take(len(prefetch))
        ins = take(n_in)
        jins = [take(k) for k in j_in]
        outs = take(n_out)
        jouts = [take(k) for k in j_out]
        scr = take(n_scr)
        jsems = [take(k) for k in j_sem]
        step = pl.program_id(0)
        for d in range(1, len(grid)):
            step = step * grid[d] + pl.program_id(d)
        for j, job in enumerate(jobs):
            pl.when(step == 0)(functools.partial(job.start, jins[j], jouts[j], jsems[j]))
            for fwd, frac in zip(job.forwards, job.forward_at):
                at = min(int(total * frac), total - 1)
                pl.when(step == at)(functools.partial(fwd, jins[j], jouts[j], jsems[j]))
        body(*ins, *outs, *scr)
        for j, job in enumerate(jobs):
            pl.when(step == total - 1)(functools.partial(job.finish, jins[j], jouts[j], jsems[j]))

    grid_spec = pltpu.PrefetchScalarGridSpec(
        num_scalar_prefetch=len(prefetch), grid=grid,
        in_specs=in_specs + [any_spec] * sum(j_in),
        out_specs=out_specs + [any_spec] * sum(j_out),
        scratch_shapes=scratch_shapes + [s for j in jobs for s in j.sems])
    res = pl.pallas_call(
        wrapped, name=name, grid_spec=grid_spec,
        out_shape=out_shape + [o for j in jobs for o in j.outs],
        input_output_aliases={len(prefetch) + i: o for i, o in (aliases or {}).items()},
        compiler_params=_params(("arbitrary",) * len(grid)),
    )(*prefetch, *args, *[a for j in jobs for a in j.ins])
    res = list(res)
    own, rest = res[:n_out], res[n_out:]
    per_job = []
    for k in j_out:
        per_job.append(rest[:k])
        rest = rest[k:]
    return own, per_job


def _exchange_sibling(arrs, name):
    return _run_job(_sibling_job(arrs), name)


def _sibling_job(arrs):
    n = len(arrs)

    def copies(ins, outs, sems):
        send_sems, recv_sems = sems
        x, y, c = _my_place()
        return [pltpu.make_async_remote_copy(
            src_ref=ins[a].at[2 * k + 1 - c], dst_ref=outs[a].at[k],
            send_sem=send_sems.at[a, k], recv_sem=recv_sems.at[a, k],
            device_id=(x, y, 1 - c), device_id_type=MESH) for a in range(n) for k in range(4)]

    def start(ins, outs, sems):
        for cp in copies(ins, outs, sems):
            cp.start()

    def finish(ins, outs, sems):
        for cp in copies(ins, outs, sems):
            cp.wait()

    return _Job(arrs, [jax.ShapeDtypeStruct((4,) + a.shape[1:], a.dtype) for a in arrs],
                [pltpu.SemaphoreType.DMA((n, 4)), pltpu.SemaphoreType.DMA((n, 4))], start, finish)


def _chips_job(arrs):
    n = len(arrs)

    def copies(ins, outs, sems):
        send_sems, recv_sems = sems
        x, y, c = _my_place()
        chips = [(1 - x, y), (x, 1 - y), (1 - x, 1 - y)]
        return [pltpu.make_async_remote_copy(
            src_ref=ins[a].at[2 * px + py], dst_ref=outs[a].at[j],
            send_sem=send_sems.at[a, j], recv_sem=recv_sems.at[a, j],
            device_id=(px, py, c), device_id_type=MESH) for a in range(n) for j, (px, py) in enumerate(chips)]

    def start(ins, outs, sems):
        for cp in copies(ins, outs, sems):
            cp.start()

    def finish(ins, outs, sems):
        for cp in copies(ins, outs, sems):
            cp.wait()

    return _Job(arrs, [jax.ShapeDtypeStruct((3,) + a.shape[1:], a.dtype) for a in arrs],
                [pltpu.SemaphoreType.DMA((n, 3)), pltpu.SemaphoreType.DMA((n, 3))], start, finish)


def _pair_add(stack, from_sibling, c_idx, name, widths=None):
    _, r, cc = stack.shape
    tr = _tile(r, 512)
    ws = [cc] if widths is None else list(widths)
    assert sum(ws) == cc
    starts = [sum(ws[:q]) for q in range(len(ws))]

    def body(c_ref, a_ref, b_ref, *o_refs):
        res = a_ref[...].astype(F32) + b_ref[...].astype(F32)
        for o_ref, s0, w_ in zip(o_refs, starts, ws):
            o_ref[...] = res[:, s0:s0 + w_].astype(o_ref.dtype)

    grid_spec = pltpu.PrefetchScalarGridSpec(
        num_scalar_prefetch=1, grid=(4, r // tr),
        in_specs=[pl.BlockSpec((None, tr, cc), lambda k, i, c_ref: (2 * k + c_ref[0], i, 0)),
                  pl.BlockSpec((None, tr, cc), lambda k, i, c_ref: (k, i, 0))],
        out_specs=[pl.BlockSpec((None, tr, w_), lambda k, i, c_ref: (k, i, 0)) for w_ in ws])
    res = pl.pallas_call(
        body, name=name, grid_spec=grid_spec,
        out_shape=[jax.ShapeDtypeStruct((4, r, w_), BF16) for w_ in ws],
        compiler_params=_params(("parallel", "parallel")),
    )(c_idx, stack, from_sibling)
    return res[0] if widths is None else list(res)


_DIMS = {"nn": ((1,), (0,)), "nt": ((1,), (1,)), "tn": ((0,), (0,))}


def _mm(a, b, mode, *, tm, tn, tk=None, outs, epi=None, tiles=(), rows=(), name, jobs=(), chunks=None, into=None,
        n_total=None, rows_part=None, wide=None):
    wide = wide or {}
    if mode == "tn":
        kk, m = a.shape
    else:
        m, kk = a.shape
    n_all = b.shape[0] if mode == "nt" else b.shape[1]
    nn = n_all if chunks is None else chunks[0].shape[0] * chunks[1]
    n_all = n_all if n_total is None else n_total
    tk = kk if tk is None else tk
    assert m % tm == 0 and nn % tn == 0 and kk % tk == 0, (name, a.shape, b.shape, tm, tn, tk)
    nk = kk // tk
    nt, nr, no = len(tiles), len(rows), len(outs)
    n_into = 0 if into is None else 1

    def body(*refs):
        a_ref, b_ref = refs[:2]
        tile_refs = refs[2:2 + nt]
        row_refs = refs[2 + nt:2 + nt + nr]
        out_refs = refs[2 + nt + nr + n_into:2 + nt + nr + n_into + no]
        part = lax.dot_general(a_ref[...], b_ref[...], (_DIMS[mode], ((), ())), preferred_element_type=F32)

        def finish(acc):
            if epi is None:
                res = (acc,)
            else:
                res = epi(acc, *[t[...] for t in tile_refs], *[r[...] for r in row_refs])
            for o_ref, val in zip(out_refs, res):
                o_ref[...] = val.astype(o_ref.dtype)

        if nk == 1:
            finish(part)
        else:
            acc_ref = refs[-1]
            k = pl.program_id(2)

            @pl.when(k == 0)
            def _():
                acc_ref[...] = part

            @pl.when(k > 0)
            def _():
                acc_ref[...] += part

            @pl.when(k == nk - 1)
            def _():
                finish(acc_ref[...])

    if chunks is None:
        col = b_row = lambda j, *_: j
        prefetch = []
    else:
        assert mode == "nt" and not tiles and not rows and chunks[1] % tn == 0
        per = chunks[1] // tn
        col = lambda j, ids, b_ids: ids[j // per] * per + j % per
        b_row = lambda j, ids, b_ids: b_ids[j // per] * per + j % per
        prefetch = [chunks[0], chunks[2]]
    row0, n_row_blocks = (0, m // tm) if rows_part is None else rows_part
    assert rows_part is None or (mode != "tn" and not tiles)
    if mode == "tn":
        a_spec = pl.BlockSpec((tk, tm), lambda i, j, k, *_: (k, i))
    else:
        a_spec = pl.BlockSpec((tm, tk), lambda i, j, k, *_: (i + row0, k))
    if mode == "nt":
        b_spec = pl.BlockSpec((tn, tk), lambda i, j, k, *s: (b_row(j, *s), k))
    else:
        b_spec = pl.BlockSpec((tk, tn), lambda i, j, k, *_: (k, j))
    tile_specs = [pl.BlockSpec((tm, tn), functools.partial(lambda i, j, k, off: (i, j + off), off=off))
                  for _, off in tiles]
    row_specs = [pl.BlockSpec((r.shape[0], tn), lambda i, j, k: (0, j)) for r in rows]
    out_spec = pl.BlockSpec((tm, tn), lambda i, j, k, *s: (i + row0, col(j, *s)))
    args = [a, b, *[t for t, _ in tiles], *rows]
    in_specs = [a_spec, b_spec] + tile_specs + row_specs
    aliases = None
    if into is not None:
        aliases = {len(args): 0}
        args.append(into)
        in_specs.append(pl.BlockSpec(memory_space=pl.ANY))
    res, jres = _call(
        body, name=name, grid=(n_row_blocks, nn // tn, nk),
        in_specs=in_specs,
        out_specs=[out_spec if q not in wide else
                   pl.BlockSpec((tm, tn), functools.partial(lambda i, j, k, off: (i, j + off), off=wide[q][1]))
                   for q in range(no)],
        out_shape=[jax.ShapeDtypeStruct((m, wide[q][0] if q in wide else n_all), dt) for q, dt in enumerate(outs)],
        scratch_shapes=[pltpu.VMEM((tm, tn), F32)] if nk > 1 else [],
        sem=("parallel", "parallel", "arbitrary"),
        args=args, jobs=jobs, prefetch=prefetch, aliases=aliases)
    return (res, jres) if jobs else res


def _ada_fwd(c_all, w_ada, b_ada_cols, name):
    nb, dd = c_all.shape
    ncol = w_ada.shape[1]
    tn = _tile(ncol, 512, LANE)

    def body(c_ref, w_ref, b_ref, o_ref):
        cv = c_ref[...]
        act = (cv * _sigmoid(cv)).astype(BF16)
        o_ref[...] = jnp.dot(act, w_ref[...].astype(BF16), preferred_element_type=F32) + b_ref[...]

    return pl.pallas_call(
        body, name=name, grid=(ncol // tn,),
        in_specs=[pl.BlockSpec((nb, dd), lambda j: (0, 0)), pl.BlockSpec((dd, tn), lambda j: (0, j)),
                  pl.BlockSpec((1, tn), lambda j: (0, j))],
        out_specs=pl.BlockSpec((nb, tn), lambda j: (0, j)),
        out_shape=jax.ShapeDtypeStruct((nb, ncol), F32),
        compiler_params=_params(("parallel",)),
    )(c_all, w_ada, b_ada_cols)


def _ada_bwd(c_all_t, dmod_cols, name):
    dd, nb = c_all_t.shape
    ncol = dmod_cols.shape[1]
    tr = _tile(dd, 256)

    def body(c_ref, d_ref, o_ref):
        cv = c_ref[...]
        act = (cv * _sigmoid(cv)).astype(BF16).astype(F32)
        dm = d_ref[...].astype(BF16).astype(F32)
        acc = act[:, 0:1] * dm[0:1, :]
        for bi in range(1, nb):
            acc = acc + act[:, bi:bi + 1] * dm[bi:bi + 1, :]
        o_ref[...] = acc

    return pl.pallas_call(
        body, name=name, grid=(dd // tr,),
        in_specs=[pl.BlockSpec((tr, nb), lambda i: (i, 0)), pl.BlockSpec((nb, ncol), lambda i: (0, 0))],
        out_specs=pl.BlockSpec((tr, ncol), lambda i: (i, 0)),
        out_shape=jax.ShapeDtypeStruct((dd, ncol), F32),
        compiler_params=_params(("parallel",)),
    )(c_all_t, dmod_cols)


def _norm_mod(x, gain, mod, row0, name):
    t, dd = x.shape
    tt = _tile(t, 512)

    def body(x_ref, g_ref, m_ref, h_ref):
        xv = x_ref[...]
        r = lax.rsqrt(jnp.mean(xv * xv, axis=-1, keepdims=True) + EPS)
        shift, scale = m_ref[row0:row0 + 1, :], m_ref[row0 + 1:row0 + 2, :]
        h_ref[...] = ((xv * r * g_ref[...]) * (1.0 + scale) + shift).astype(BF16)

    return pl.pallas_call(
        body, name=name, grid=(t // tt,),
        in_specs=[pl.BlockSpec((tt, dd), lambda i: (i, 0)), pl.BlockSpec((1, dd), lambda i: (0, 0)),
                  pl.BlockSpec((8, dd), lambda i: (0, 0))],
        out_specs=pl.BlockSpec((tt, dd), lambda i: (i, 0)),
        out_shape=jax.ShapeDtypeStruct((t, dd), BF16),
        compiler_params=_params(("parallel",)),
    )(x, gain, mod)


def _norm_mod_bwd(x, gain, mod, row0, dh, dres, branch, name, jobs=()):
    t, dd = x.shape
    tt = _tile(t, 512)
    has_branch = branch is not None

    def body(*refs):
        if has_branch:
            x_ref, g_ref, m_ref, dh_ref, dres_ref, o_ref, dx_ref, st_ref, do_ref = refs[:-1]
        else:
            x_ref, g_ref, m_ref, dh_ref, dres_ref, dx_ref, st_ref = refs[:-1]

        acc_ref = refs[-1]
        step_i = pl.program_id(0)

        @pl.when(step_i == 0)
        def _():
            acc_ref[...] = jnp.zeros_like(acc_ref)

        def fold(v):
            out = v[0:8]
            for q in range(1, rc // 8):
                out = out + v[8 * q:8 * q + 8]
            return out

        def chunk(ci, carry):
            rows = pl.ds(pl.multiple_of(ci * rc, rc), rc)
            xv = x_ref[rows, :]
            g = g_ref[...]
            r = lax.rsqrt(jnp.mean(xv * xv, axis=-1, keepdims=True) + EPS)
            nv = xv * r
            dhv = dh_ref[rows, :].astype(F32)
            acc_ref[0] += fold(dhv)
            acc_ref[1] += fold(dhv * (nv * g))
            dng = dhv * (1.0 + m_ref[row0 + 1:row0 + 2, :])
            acc_ref[2] += fold(dng * nv)
            dn = dng * g
            dx = dres_ref[rows, :] + r * (dn - nv * jnp.mean(dn * nv, axis=-1, keepdims=True))
            dx_ref[rows, :] = dx
            if has_branch:
                acc_ref[3] += fold(dx * o_ref[rows, :].astype(F32))
                do_ref[rows, :] = (dx * m_ref[branch[1]:branch[1] + 1, :]).astype(BF16)
            return carry

        lax.fori_loop(0, tt // rc, chunk, 0, unroll=4)

        @pl.when(step_i == nsteps - 1)
        def _():
            st_ref[...] = jnp.zeros_like(st_ref)
            for k in range(4):
                st_ref[k:k + 1, :] = jnp.sum(acc_ref[k], axis=0, keepdims=True)

    rc = 16 if tt % 16 == 0 else 8
    nsteps = t // tt
    row = pl.BlockSpec((tt, dd), lambda i: (i, 0))
    in_specs = [row, pl.BlockSpec((1, dd), lambda i: (0, 0)), pl.BlockSpec((8, dd), lambda i: (0, 0)), row, row]
    args = [x, gain, mod, dh, dres]
    out_specs = [row, pl.BlockSpec((8, dd), lambda i: (0, 0))]
    out_shape = [jax.ShapeDtypeStruct((t, dd), F32), jax.ShapeDtypeStruct((8, dd), F32)]
    if has_branch:
        in_specs.append(row)
        args.append(branch[0])
        out_specs.append(row)
        out_shape.append(jax.ShapeDtypeStruct((t, dd), BF16))
    res, jres = _call(body, name=name, grid=(t // tt,), in_specs=in_specs, out_specs=out_specs, out_shape=out_shape,
                      scratch_shapes=[pltpu.VMEM((4, 8, dd), F32)], sem=("arbitrary",), args=args, jobs=jobs)
    return (res, jres) if jobs else res


def _final_loss(x2, o2, final_g, target, mod, name):
    t, dd = x2.shape
    tt = _tile(t, 256)
    nsteps = t // tt

    def body(x_ref, o_ref, g_ref, t_ref, m_ref, loss_ref, dx_ref, do_ref, st_ref, lacc):
        i = pl.program_id(0)

        @pl.when(i == 0)
        def _():
            st_ref[...] = jnp.zeros_like(st_ref)
            lacc[...] = jnp.zeros_like(lacc)

        xv = x_ref[...]
        g = g_ref[...]
        r = lax.rsqrt(jnp.mean(xv * xv, axis=-1, keepdims=True) + EPS)
        nv = xv * r
        err = nv * g - t_ref[...]
        lacc[...] += jnp.sum(err * err, axis=0, keepdims=True)
        dy = err * (1.0 / dd)
        st_ref[0:1, :] += jnp.sum(dy * nv, axis=0, keepdims=True)
        dn = dy * g
        dx = r * (dn - nv * jnp.mean(dn * nv, axis=-1, keepdims=True))
        dx_ref[...] = dx
        st_ref[1:2, :] += jnp.sum(dx * o_ref[...].astype(F32), axis=0, keepdims=True)
        do_ref[...] = (dx * m_ref[5:6, :]).astype(BF16)

        @pl.when(i == nsteps - 1)
        def _():
            total = jnp.sum(lacc[...], axis=-1, keepdims=True) * (0.5 / dd)
            loss_ref[...] = jnp.broadcast_to(total, loss_ref.shape)

    row = pl.BlockSpec((tt, dd), lambda i: (i, 0))
    return pl.pallas_call(
        body, name=name, grid=(nsteps,),
        in_specs=[row, row, pl.BlockSpec((1, dd), lambda i: (0, 0)), row, pl.BlockSpec((8, dd), lambda i: (0, 0))],
        out_specs=[pl.BlockSpec((8, LANE), lambda i: (0, 0)), row, row, pl.BlockSpec((8, dd), lambda i: (0, 0))],
        out_shape=[jax.ShapeDtypeStruct((8, LANE), F32), jax.ShapeDtypeStruct((t, dd), F32),
                   jax.ShapeDtypeStruct((t, dd), BF16), jax.ShapeDtypeStruct((8, dd), F32)],
        scratch_shapes=[pltpu.VMEM((1, dd), F32)],
        compiler_params=_params(("arbitrary",)),
    )(x2, o2, final_g, target, mod)


def _lru_gates(xt, prev8, lp, wa_ref, wx_ref, first_tile, cw):
    tt = xt.shape[0]
    ext = jnp.concatenate([prev8, xt], axis=0)
    xs = [pltpu.roll(ext, s, 0)[8:, :] if s else xt for s in range(CONV_WIDTH)]
    xc = lp[0:1, :] + lp[7:8, :] * xs[0] + lp[6:7, :] * xs[1] + lp[5:6, :] * xs[2] + lp[4:5, :] * xs[3]
    xcb = xc.astype(BF16)
    za, zi = [], []
    for jb in range(cw // LRU_BLOCK_W):
        sl = slice(jb * LRU_BLOCK_W, (jb + 1) * LRU_BLOCK_W)
        za.append(jnp.dot(xcb[:, sl], wa_ref[jb].astype(BF16), preferred_element_type=F32))
        zi.append(jnp.dot(xcb[:, sl], wx_ref[jb].astype(BF16), preferred_element_type=F32))
    ra = _sigmoid(jnp.concatenate(za, axis=1) + lp[1:2, :])
    ri = _sigmoid(jnp.concatenate(zi, axis=1) + lp[2:3, :])
    sp = _softplus(-lp[3:4, :])
    log_a = -LRU_C * ra * sp
    av = jnp.exp(log_a)
    seq_start = jnp.logical_and(first_tile, lax.broadcasted_iota(jnp.int32, (tt, 1), 0) == 0)
    m2 = _neg_expm1(2.0 * log_a, av * av)
    inv_mult = jnp.where(seq_start, 1.0, lax.rsqrt(jnp.maximum(m2, 1e-30)))
    mult = jnp.where(seq_start, 1.0, m2 * inv_mult)
    return xs, xc, xcb, ra, ri, sp, av, mult, inv_mult, seq_start


def _scan_groups(a, u, reverse):
    tt, cw = a.shape
    a3, u3 = a.reshape(tt // 8, 8, cw), u.reshape(tt // 8, 8, cw)
    r = lax.broadcasted_iota(jnp.int32, (1, 8, 1), 1)
    for s in (1, 2, 4):
        shift = 8 - s if reverse else s
        take = r < 8 - s if reverse else r >= s
        u3 = jnp.where(take, a3 * pltpu.roll(u3, shift, 1) + u3, u3)
        a3 = jnp.where(take, a3 * pltpu.roll(a3, shift, 1), a3)
    return a3, u3


def _scan_carry(a_s, u_s, h_s, carry, reverse):
    ng = a_s.shape[0]

    def step(i, h):
        g = ng - 1 - i if reverse else i
        hg = u_s[g] + a_s[g] * h
        h_s[g] = hg
        return hg[0:1, :] if reverse else hg[7:8, :]

    return lax.fori_loop(0, ng, step, carry, unroll=4)


def _lru_fwd(proj, lp, wa, wx, name, jobs=()):
    t = proj.shape[0]
    w = D_MODEL
    cw = LRU_CW
    tt = _tile(t, 256)
    nb = cw // LRU_BLOCK_W

    def body(lx_ref, lg_ref, lp_ref, wa_ref, wx_ref, rec_ref, ya_ref, a_s, u_s, h_s, halo, carry):
        ti = pl.program_id(1)

        @pl.when(ti == 0)
        def _():
            halo[...] = jnp.zeros_like(halo)
            carry[...] = jnp.zeros_like(carry)

        xt = lx_ref[...].astype(F32)
        lp_v = lp_ref[...]
        _, xc, _, _, ri, _, av, mult, _, _ = _lru_gates(xt, halo[...], lp_v, wa_ref, wx_ref, ti == 0, cw)
        halo[...] = xt[tt - 8:, :]
        a_s[...], u_s[...] = _scan_groups(av, mult * (ri * xc), False)
        carry[...] = _scan_carry(a_s, u_s, h_s, carry[...], False)
        rec = h_s[...].reshape(tt, cw)
        rec_ref[...] = rec.astype(BF16)
        gelu, _ = _gelu_tanh(lg_ref[...].astype(F32))
        ya_ref[...] = (rec * gelu).astype(BF16)

    off_g = OFF_LG // cw
    res, jres = _call(
        body, name=name, grid=(w // cw, t // tt),
        in_specs=[pl.BlockSpec((tt, cw), lambda ci, ti: (ti, ci)),
                  pl.BlockSpec((tt, cw), lambda ci, ti: (ti, off_g + ci)),
                  pl.BlockSpec((8, cw), lambda ci, ti: (0, ci)),
                  pl.BlockSpec((nb, LRU_BLOCK_W, LRU_BLOCK_W), lambda ci, ti: (ci, 0, 0)),
                  pl.BlockSpec((nb, LRU_BLOCK_W, LRU_BLOCK_W), lambda ci, ti: (ci, 0, 0))],
        out_specs=[pl.BlockSpec((tt, cw), lambda ci, ti: (ti, ci))] * 2,
        out_shape=[jax.ShapeDtypeStruct((t, w), BF16)] * 2,
        scratch_shapes=[pltpu.VMEM((tt // 8, 8, cw), F32)] * 3 + [pltpu.VMEM((8, cw), F32), pltpu.VMEM((1, cw), F32)],
        sem=("parallel", "arbitrary"), args=[proj, proj, lp, wa, wx], jobs=jobs)
    return (res, jres) if jobs else res


def _lru_bwd(proj, rec, dya, lp, wa, wx, name, jobs=(), into=None):
    t = proj.shape[0]
    w = D_MODEL
    cw = LRU_CW if into is None else w
    tt = _tile(t, 256 if into is None else 128)
    nt = t // tt
    nb = cw // LRU_BLOCK_W
    r8 = tt // 8

    def body(lx_ref, lxp_ref, lg_ref, rec_ref, recp_ref, dya_ref, lp_ref, wa_ref, wx_ref, *rest):
        if into is None:
            dlx_ref, dlg_ref = rest[0], rest[1]
        else:
            dlx_ref, dlg_ref = rest[1].at[:, 0:cw], rest[1].at[:, cw:2 * cw]
        st_ref, dwa_ref, dwx_ref, a_s, d_s, dh_s, nhalo, carry = rest[-8:]
        step_i = pl.program_id(1)
        ti = nt - 1 - step_i

        @pl.when(step_i == 0)
        def _():
            st_ref[...] = jnp.zeros_like(st_ref)
            dwa_ref[...] = jnp.zeros_like(dwa_ref)
            dwx_ref[...] = jnp.zeros_like(dwx_ref)
            nhalo[...] = jnp.zeros_like(nhalo)
            carry[...] = jnp.zeros_like(carry)

        first = ti == 0
        keep = jnp.where(first, 0.0, 1.0)
        xt = lx_ref[...].astype(F32)
        prev8 = lxp_ref[...].astype(F32) * keep
        lp_v = lp_ref[...]
        xs, xc, xcb, ra, ri, sp, av, mult, inv_mult, seq_start = _lru_gates(xt, prev8, lp_v, wa_ref, wx_ref, first, cw)

        lg = lg_ref[...].astype(F32)
        gelu, th = _gelu_tanh(lg)
        dyav = dya_ref[...].astype(F32)
        recv = rec_ref[...].astype(F32)
        dlg_ref[...] = (dyav * recv * _gelu_tanh_grad(lg, th)).astype(BF16)

        drec = dyav * gelu
        e_in = carry[...]
        a_s[...], d_s[...] = _scan_groups(av, av * drec, True)
        carry[...] = _scan_carry(a_s, d_s, dh_s, e_in, True)
        e_next = jnp.concatenate([dh_s[...].reshape(tt, cw), jnp.broadcast_to(e_in, (8, cw))], axis=0)
        dh = drec + pltpu.roll(e_next, tt + 7, 0)[:tt, :]
        rprev8 = recp_ref[...].astype(F32) * keep
        hprev = pltpu.roll(jnp.concatenate([rprev8, recv], axis=0), 1, 0)[8:, :]
        da = dh * hprev
        dmult = jnp.where(seq_start, 0.0, dh * ri * xc)
        dri = dh * mult * xc
        dxc = dh * mult * ri
        dlog_a = da * av - dmult * (av * av) * inv_mult
        dra = dlog_a * (-LRU_C * sp)
        lam = lp_v[3:4, :]
        st_ref[3:4, :] += jnp.sum(dlog_a * ra, axis=0, keepdims=True) * (LRU_C * _sigmoid(-lam))
        dza = dra * ra * (1.0 - ra)
        dzi = dri * ri * (1.0 - ri)
        st_ref[1:2, :] += jnp.sum(dza, axis=0, keepdims=True)
        st_ref[2:3, :] += jnp.sum(dzi, axis=0, keepdims=True)
        dzab, dzib = dza.astype(BF16), dzi.astype(BF16)
        back = []
        for jb in range(nb):
            sl = slice(jb * LRU_BLOCK_W, (jb + 1) * LRU_BLOCK_W)
            dwa_ref[jb] += lax.dot_general(xcb[:, sl], dzab[:, sl], (_DIMS["tn"], ((), ())), preferred_element_type=F32)
            dwx_ref[jb] += lax.dot_general(xcb[:, sl], dzib[:, sl], (_DIMS["tn"], ((), ())), preferred_element_type=F32)
            back.append(
                lax.dot_general(dzab[:, sl], wa_ref[jb].astype(BF16), (_DIMS["nt"], ((), ())), preferred_element_type=F32)
                + lax.dot_general(dzib[:, sl], wx_ref[jb].astype(BF16), (_DIMS["nt"], ((), ())), preferred_element_type=F32))
        dxc = dxc + jnp.concatenate(back, axis=1)
        st_ref[0:1, :] += jnp.sum(dxc, axis=0, keepdims=True)
        for k in range(CONV_WIDTH):
            st_ref[4 + k:5 + k, :] += jnp.sum(dxc * xs[CONV_WIDTH - 1 - k], axis=0, keepdims=True)
        ext = jnp.concatenate([dxc, nhalo[...]], axis=0)
        dlx = lp_v[7:8, :] * dxc
        for s in range(1, CONV_WIDTH):
            dlx = dlx + lp_v[7 - s:8 - s, :] * pltpu.roll(ext, tt + 8 - s, 0)[:tt, :]
        dlx_ref[...] = dlx.astype(BF16)
        nhalo[...] = dxc[0:8, :]

    off_g = OFF_LG // cw
    tile = lambda off: pl.BlockSpec((tt, cw), lambda ci, si: (nt - 1 - si, off + ci))
    prev = lambda off: pl.BlockSpec((8, cw), lambda ci, si: (jnp.maximum((nt - 1 - si) * r8 - 1, 0), off + ci))
    wspec = pl.BlockSpec((nb, LRU_BLOCK_W, LRU_BLOCK_W), lambda ci, si: (ci, 0, 0))
    st_spec = pl.BlockSpec((8, cw), lambda ci, si: (0, ci))
    in_specs = [tile(0), prev(0), tile(off_g), tile(0), prev(0), tile(0), st_spec, wspec, wspec]
    args = [proj, proj, proj, rec, rec, dya, lp, wa, wx]
    if into is None:
        d_specs = [tile(0), tile(0)]
        d_shapes = [jax.ShapeDtypeStruct((t, w), BF16), jax.ShapeDtypeStruct((t, w), BF16)]
        aliases = None
    else:
        d_specs = [pl.BlockSpec((tt, 2 * cw), lambda ci, si: (nt - 1 - si, 0))]
        d_shapes = [jax.ShapeDtypeStruct(into.shape, BF16)]
        aliases = {len(args): 0}
        args.append(into)
        in_specs.append(pl.BlockSpec(memory_space=pl.ANY))
    res, jres = _call(
        body, name=name, grid=(w // cw, nt), in_specs=in_specs,
        out_specs=d_specs + [st_spec, wspec, wspec],
        out_shape=d_shapes + [jax.ShapeDtypeStruct((8, w), F32),
                              jax.ShapeDtypeStruct((LRU_BLOCKS, LRU_BLOCK_W, LRU_BLOCK_W), F32),
                              jax.ShapeDtypeStruct((LRU_BLOCKS, LRU_BLOCK_W, LRU_BLOCK_W), F32)],
        scratch_shapes=[pltpu.VMEM((tt // 8, 8, cw), F32)] * 3 + [pltpu.VMEM((8, cw), F32), pltpu.VMEM((1, cw), F32)],
        sem=("parallel", "arbitrary"), args=args, jobs=jobs, aliases=aliases)
    return (res, jres) if jobs else res


def _band_valid(first_block):
    qi = lax.broadcasted_iota(jnp.int32, (BLOCK, 2 * BLOCK), 0)
    ki = lax.broadcasted_iota(jnp.int32, (BLOCK, 2 * BLOCK), 1)
    rel = qi + BLOCK - ki
    valid = jnp.logical_and(rel >= 0, rel < BLOCK)
    return jnp.logical_and(valid, jnp.logical_or(ki >= BLOCK, jnp.logical_not(first_block)))


ATTN_STACK_BWD = 8


def _low_half():
    return lax.broadcasted_iota(jnp.int32, (1, LANE), 1) < HEAD_DIM


def _stack_heads(x, h0, ns):
    low = _low_half()
    parts = []
    for g in range(ns):
        h = h0 + g
        slab = x[:, (h // 2) * LANE:(h // 2 + 1) * LANE]
        parts.append(jnp.where(low if h % 2 == 0 else jnp.logical_not(low), slab, jnp.zeros_like(slab)))
    return jnp.concatenate(parts, axis=0)


def _unstack_heads(y, ns):
    low = _low_half()
    pairs = [jnp.where(low, y[(2 * j) * BLOCK:(2 * j + 1) * BLOCK], y[(2 * j + 1) * BLOCK:(2 * j + 2) * BLOCK])
             for j in range(ns // 2)]
    return pairs[0] if len(pairs) == 1 else jnp.concatenate(pairs, axis=1)


def _dup_kv(xall, kv):
    part = xall[:, kv * HEAD_DIM:(kv + 1) * HEAD_DIM]
    return jnp.concatenate([part, part], axis=1)


def _fold_halves(a):
    return a + pltpu.roll(a, HEAD_DIM, 1)


def _group_probs(qg, k2, bias_g, sink_g, lse_g, valid):
    s = lax.dot_general(qg, k2, (_DIMS["nt"], ((), ())), preferred_element_type=F32) * (HEAD_DIM ** -0.5)
    s = jnp.where(valid[None], s.reshape(bias_g.shape) + bias_g, NEG_INF)
    return jnp.exp(s - lse_g), jnp.exp(sink_g - lse_g)


def _group_sinks(s_ref, h0, ns):
    return jnp.concatenate([jnp.full((1, BLOCK, 1), s_ref[h0 + g], F32) for g in range(ns)], axis=0)


def _attn_specs(t):
    nblk = t // BLOCK
    q_spec = pl.BlockSpec((BLOCK, D_MODEL), lambda n: (n, OFF_Q // D_MODEL))
    kc = pl.BlockSpec((BLOCK, KV_WIDTH), lambda n: (n, OFF_K // KV_WIDTH))
    kp = pl.BlockSpec((BLOCK, KV_WIDTH), lambda n: (jnp.maximum(n - 1, 0), OFF_K // KV_WIDTH))
    vc = pl.BlockSpec((BLOCK, KV_WIDTH), lambda n: (n, OFF_V // KV_WIDTH))
    vp = pl.BlockSpec((BLOCK, KV_WIDTH), lambda n: (jnp.maximum(n - 1, 0), OFF_V // KV_WIDTH))
    bias_spec = pl.BlockSpec((N_Q_HEADS, BLOCK, 2 * BLOCK), lambda n: (0, 0, 0))
    sink_spec = pl.BlockSpec(memory_space=pltpu.SMEM)
    return nblk, [q_spec, kp, kc, vp, vc, bias_spec, sink_spec]


def _attn_fwd(proj, bias, sinks, name, jobs=()):
    t = proj.shape[0]
    nblk, in_specs = _attn_specs(t)

    def body(q_ref, kp_ref, kc_ref, vp_ref, vc_ref, b_ref, s_ref, o_ref, lse_ref):
        valid = _band_valid(pl.program_id(0) == 0)
        lane = lax.broadcasted_iota(jnp.int32, (1, LANE), 1)
        q = q_ref[...]
        kall = jnp.concatenate([kp_ref[...], kc_ref[...]], axis=0)
        vall = jnp.concatenate([vp_ref[...], vc_ref[...]], axis=0)
        outs = []
        lse = jnp.zeros((BLOCK, LANE), F32)
        for h in range(N_Q_HEADS):
            kv = h // GROUP
            kk = kall[:, kv * HEAD_DIM:(kv + 1) * HEAD_DIM]
            vv = vall[:, kv * HEAD_DIM:(kv + 1) * HEAD_DIM]
            s = lax.dot_general(q[:, h * HEAD_DIM:(h + 1) * HEAD_DIM], kk, (_DIMS["nt"], ((), ())),
                                preferred_element_type=F32) * (HEAD_DIM ** -0.5)
            s = jnp.where(valid, s + b_ref[h], NEG_INF)
            sink = s_ref[h]
            m = jnp.maximum(jnp.max(s, axis=-1, keepdims=True), sink)
            e = jnp.exp(s - m)
            den = jnp.sum(e, axis=-1, keepdims=True) + jnp.exp(sink - m)
            lse = jnp.where(lane == h, m + jnp.log(den), lse)
            outs.append(jnp.dot((e * (1.0 / den)).astype(BF16), vv, preferred_element_type=F32))
        o_ref[...] = jnp.concatenate(outs, axis=1).astype(BF16)
        lse_ref[...] = lse

    res, jres = _call(
        body, name=name, grid=(nblk,), in_specs=in_specs,
        out_specs=[pl.BlockSpec((BLOCK, D_MODEL), lambda n: (n, 0)), pl.BlockSpec((BLOCK, LANE), lambda n: (n, 0))],
        out_shape=[jax.ShapeDtypeStruct((t, D_MODEL), BF16), jax.ShapeDtypeStruct((t, LANE), F32)],
        sem=("parallel",), args=[proj, proj, proj, proj, proj, bias, sinks], jobs=jobs)
    return (res, jres) if jobs else res


def _attn_bwd(proj, datt, lse, bias, sinks, name, jobs=(), into=None):
    t = proj.shape[0]
    nblk, in_specs = _attn_specs(t)
    in_specs = in_specs + [pl.BlockSpec((BLOCK, D_MODEL), lambda n: (n, 0)), pl.BlockSpec((BLOCK, LANE), lambda n: (n, 0))]
    scale = HEAD_DIM ** -0.5
    args = [proj, proj, proj, proj, proj, bias, sinks, datt, lse]
    dq_spec = pl.BlockSpec((BLOCK, D_MODEL), lambda n: (n, 0))
    dq_shape = jax.ShapeDtypeStruct((t, D_MODEL), BF16)
    aliases = None
    if into is not None:
        aliases = {len(args): 0}
        args.append(into)
        in_specs.append(pl.BlockSpec(memory_space=pl.ANY))
        dq_spec = pl.BlockSpec((BLOCK, D_MODEL), lambda n: (n, OFF_Q // D_MODEL))
        dq_shape = jax.ShapeDtypeStruct(into.shape, BF16)

    def body(q_ref, kp_ref, kc_ref, vp_ref, vc_ref, b_ref, s_ref, do_ref, lse_ref, *rest):
        dq_ref, dk_ref, dv_ref, db_ref, ds_ref = rest[-5:]
        n = pl.program_id(0)

        @pl.when(n == 0)
        def _():
            dk_ref[...] = jnp.zeros_like(dk_ref)
            dv_ref[...] = jnp.zeros_like(dv_ref)
            db_ref[...] = jnp.zeros_like(db_ref)
            ds_ref[...] = jnp.zeros_like(ds_ref)

        valid = _band_valid(n == 0)
        lane = lax.broadcasted_iota(jnp.int32, (1, LANE), 1)
        q = q_ref[...]
        do = do_ref[...]
        lse_all = lse_ref[...]
        kall = jnp.concatenate([kp_ref[...], kc_ref[...]], axis=0)
        vall = jnp.concatenate([vp_ref[...], vc_ref[...]], axis=0)
        low = _low_half()
        dqs, dks, dvs = [], [], []
        dsink_row = jnp.zeros((1, LANE), F32)
        ns = ATTN_STACK_BWD
        for kv in range(N_KV_HEADS):
            k2, v2 = _dup_kv(kall, kv), _dup_kv(vall, kv)
            dk_acc = jnp.zeros((2 * BLOCK, LANE), F32)
            dv_acc = jnp.zeros((2 * BLOCK, LANE), F32)
            for h0 in range(kv * GROUP, (kv + 1) * GROUP, ns):
                qg, dog = _stack_heads(q, h0, ns), _stack_heads(do, h0, ns)
                lse_g = jnp.concatenate([lse_all[:, h0 + g:h0 + g + 1][None] for g in range(ns)], axis=0)
                p, psink = _group_probs(qg, k2, b_ref[h0:h0 + ns], _group_sinks(s_ref, h0, ns), lse_g, valid)
                dp = lax.dot_general(dog, v2, (_DIMS["nt"], ((), ())), preferred_element_type=F32)
                dp = dp.reshape(ns, BLOCK, 2 * BLOCK)
                delta = jnp.sum(p * dp, axis=-1, keepdims=True)
                ds = p * (dp - delta)
                db_ref[h0:h0 + ns] += ds
                dsink = -jnp.sum(psink * delta, axis=1, keepdims=True)
                for g in range(ns):
                    dsink_row = dsink_row + jnp.where(lane == h0 + g, dsink[g], 0.0)
                dsb = ds.reshape(ns * BLOCK, 2 * BLOCK).astype(BF16)
                pb = p.reshape(ns * BLOCK, 2 * BLOCK).astype(BF16)
                dqs.append(_unstack_heads(jnp.dot(dsb, k2, preferred_element_type=F32) * scale, ns))
                dk_acc = dk_acc + lax.dot_general(dsb, qg, (_DIMS["tn"], ((), ())), preferred_element_type=F32)
                dv_acc = dv_acc + lax.dot_general(pb, dog, (_DIMS["tn"], ((), ())), preferred_element_type=F32)
            dks.append(_fold_halves(dk_acc) * scale)
            dvs.append(_fold_halves(dv_acc))
        dq_ref[...] = jnp.concatenate(dqs, axis=1).astype(BF16)
        rows = pl.ds(pl.multiple_of(n * BLOCK, BLOCK), 2 * BLOCK)
        dk_ref[rows, :] += jnp.concatenate([jnp.where(low, dks[0], dks[1]), jnp.where(low, dks[2], dks[3])], axis=1)
        dv_ref[rows, :] += jnp.concatenate([jnp.where(low, dvs[0], dvs[1]), jnp.where(low, dvs[2], dvs[3])], axis=1)
        ds_ref[0:1, :] += dsink_row

    res, jres = _call(
        body, name=name, grid=(nblk,), in_specs=in_specs,
        out_specs=[dq_spec,
                   pl.BlockSpec((t + BLOCK, KV_WIDTH), lambda n: (0, 0)),
                   pl.BlockSpec((t + BLOCK, KV_WIDTH), lambda n: (0, 0)),
                   pl.BlockSpec((N_Q_HEADS, BLOCK, 2 * BLOCK), lambda n: (0, 0, 0)),
                   pl.BlockSpec((8, LANE), lambda n: (0, 0))],
        out_shape=[dq_shape,
                   jax.ShapeDtypeStruct((t + BLOCK, KV_WIDTH), F32),
                   jax.ShapeDtypeStruct((t + BLOCK, KV_WIDTH), F32),
                   jax.ShapeDtypeStruct((N_Q_HEADS, BLOCK, 2 * BLOCK), F32),
                   jax.ShapeDtypeStruct((8, LANE), F32)],
        sem=("arbitrary",), args=args, jobs=jobs, aliases=aliases)
    return (res, jres) if jobs else res


def _bucket_table():
    qi = np.arange(BLOCK)[:, None]
    ki = np.arange(2 * BLOCK)[None, :]
    rel = np.maximum(qi + BLOCK - ki, 0)
    max_exact = N_BUCKETS // 2
    relf = np.maximum(rel, 1).astype(np.float32)
    large = max_exact + (np.log(relf / max_exact) / math.log(MAX_DISTANCE / max_exact)
                         * (N_BUCKETS - max_exact)).astype(np.int32)
    large = np.minimum(large, N_BUCKETS - 1)
    return np.where(rel < max_exact, rel, large).astype(np.int32)


def _rel_bias_fwd(rel_bias, onehot, name):
    n = onehot.shape[1]
    tn = _tile(n, 8192, LANE)

    def body(r_ref, o_ref, out_ref):
        out_ref[...] = lax.dot_general(r_ref[...], o_ref[...], (_DIMS["tn"], ((), ())),
                                       precision=lax.Precision.HIGHEST, preferred_element_type=F32)

    return pl.pallas_call(
        body, name=name, grid=(n // tn,),
        in_specs=[pl.BlockSpec(rel_bias.shape, lambda j: (0, 0)), pl.BlockSpec((N_BUCKETS, tn), lambda j: (0, j))],
        out_specs=pl.BlockSpec((N_Q_HEADS, tn), lambda j: (0, j)),
        out_shape=jax.ShapeDtypeStruct((N_Q_HEADS, n), F32),
        compiler_params=_params(("parallel",)),
    )(rel_bias, onehot)


def _rel_bias_bwd(dbias, onehot, name):
    def body(d_ref, o_ref, out_ref):
        out_ref[...] = lax.dot_general(d_ref[...], o_ref[...], (_DIMS["nt"], ((), ())),
                                       precision=lax.Precision.HIGHEST, preferred_element_type=F32)

    full = pl.BlockSpec(dbias.shape, lambda: (0, 0))
    return pl.pallas_call(
        body, name=name, in_specs=[full, full],
        out_specs=pl.BlockSpec((N_Q_HEADS, N_BUCKETS), lambda: (0, 0)),
        out_shape=jax.ShapeDtypeStruct((N_Q_HEADS, N_BUCKETS), F32),
        compiler_params=_params(()),
    )(dbias, onehot)


def _part_specs(gparts, part_index, tr, cc):
    blk = pl.BlockSpec((tr, cc), lambda i, *_: (i, 0))
    specs = []
    for i, _ in enumerate(gparts):
        if i in part_index:
            ix = part_index[i]
            if isinstance(ix, int):
                specs.append(pl.BlockSpec((None, tr, cc), functools.partial(lambda i, *_, ix: (ix, i, 0), ix=ix)))
            else:
                specs.append(pl.BlockSpec((None, tr, cc), lambda i, ix_ref: (ix_ref[0], i, 0)))
        else:
            specs.append(blk)
    return blk, specs


def _sum_parts(gparts, part_index, name):
    r, cc = gparts[0].shape[-2:]
    tr = _tile(r, 256)
    dyn = [i for i, ix in part_index.items() if not isinstance(ix, int)]
    assert len(dyn) <= 1

    def body(*refs):
        refs = refs[len(dyn):]
        g = refs[0][...].astype(F32)
        for gr in refs[1:-1]:
            g = g + gr[...].astype(F32)
        refs[-1][...] = g

    blk, specs = _part_specs(gparts, part_index, tr, cc)
    grid_spec = pltpu.PrefetchScalarGridSpec(num_scalar_prefetch=len(dyn), grid=(r // tr,), in_specs=specs, out_specs=blk)
    return pl.pallas_call(
        body, name=name, grid_spec=grid_spec, out_shape=jax.ShapeDtypeStruct((r, cc), F32),
        compiler_params=_params(("parallel",)),
    )(*[part_index[i] for i in dyn], *gparts)


def _adamw(w, m, v, gparts, name, part_index=None):
    r, cc = w.shape
    tr = _tile(r, 128)
    np_ = len(gparts)
    part_index = part_index or {}
    dyn = [i for i, ix in part_index.items() if not isinstance(ix, int)]
    assert len(dyn) <= 1
    bc1 = 1.0 / (1.0 - ADAM_B1 ** ADAM_STEP)
    bc2 = 1.0 / (1.0 - ADAM_B2 ** ADAM_STEP)

    def body(*refs):
        refs = refs[len(dyn):]
        w_ref, m_ref, v_ref = refs[:3]
        g_refs = refs[3:3 + np_]
        g_out, d_out, m_out, v_out = refs[3 + np_:]
        g = g_refs[0][...].astype(F32)
        for gr in g_refs[1:]:
            g = g + gr[...].astype(F32)
        mn = ADAM_B1 * m_ref[...] + (1.0 - ADAM_B1) * g
        vn = ADAM_B2 * v_ref[...] + (1.0 - ADAM_B2) * (g * g)
        g_out[...] = g
        m_out[...] = mn
        v_out[...] = vn
        d_out[...] = -ADAM_LR * ((mn * bc1) / (jnp.sqrt(vn * bc2) + ADAM_EPS) + ADAM_WD * w_ref[...])

    blk, g_specs = _part_specs(gparts, part_index, tr, cc)
    grid_spec = pltpu.PrefetchScalarGridSpec(
        num_scalar_prefetch=len(dyn), grid=(r // tr,),
        in_specs=[blk, blk, blk] + g_specs, out_specs=[blk] * 4)
    return pl.pallas_call(
        body, name=name, grid_spec=grid_spec,
        out_shape=[jax.ShapeDtypeStruct((r, cc), F32)] * 4,
        compiler_params=_params(("parallel",)),
    )(*[part_index[i] for i in dyn], w, m, v, *gparts)


def _row(vec):
    return vec.reshape(1, -1)


def kernel(x, c, w_ada, b_ada, norm1_g, w_in, conv_w, conv_b, lru_wa, lru_ba, lru_wx, lru_bx, lru_lambda, w_lru_out, w_attn_out, attn_sinks, rel_bias, w_out, norm2_g, w_ff1, w_ff2, final_g, loss_target, m_w_ada, m_b_ada, m_norm1_g, m_w_in, m_conv_w, m_conv_b, m_lru_wa, m_lru_ba, m_lru_wx, m_lru_bx, m_lru_lambda, m_w_lru_out, m_w_attn_out, m_attn_sinks, m_rel_bias, m_w_out, m_norm2_g, m_w_ff1, m_w_ff2, m_final_g, v_w_ada, v_b_ada, v_norm1_g, v_w_in, v_conv_w, v_conv_b, v_lru_wa, v_lru_ba, v_lru_wx, v_lru_bx, v_lru_lambda, v_w_lru_out, v_w_attn_out, v_attn_sinks, v_rel_bias, v_w_out, v_norm2_g, v_w_ff1, v_w_ff2, v_final_g):
    dd = D_MODEL
    t = x.shape[1]
    ix, iy, ic = lax.axis_index("x"), lax.axis_index("y"), lax.axis_index("c")
    me = 4 * ix + 2 * iy + ic
    chip = 2 * ix + iy
    c_idx = jnp.reshape(ic, (1,)).astype(jnp.int32)
    chip_idx = jnp.reshape(chip, (1,)).astype(jnp.int32)

    xs = x[0]
    target = loss_target[0]
    ncol_ada = w_ada.shape[2]
    cshard = conv_w.shape[2]

    shards = [jnp.transpose(w_in[0]).astype(BF16), w_lru_out[0].astype(BF16), w_attn_out[0].astype(BF16),
              w_out[0].astype(BF16), jnp.transpose(w_ff1[0]).astype(BF16), w_ff2[0].astype(BF16)]
    s_in, s_lo, s_ao, s_o, s_f1, s_f2 = shards

    def full(g):
        return g.reshape(N_DEV * g.shape[1], dd)

    def stack(g):
        return g.reshape(N_DEV, g.shape[0] // N_DEV, dd)

    (w_in_pair,) = _run_job(_pair_job([s_in]), "ag_w_in_pair")
    w_in_pair = lax.dynamic_update_slice_in_dim(w_in_pair, s_in[None], ic, axis=0)

    pack0 = jnp.zeros((8, dd), F32).at[0:1, :].set(c).at[1:1 + CONV_WIDTH, 0:cshard].set(conv_w[0])
    (g0,) = _all_gather([pack0], "ag_cond")
    c_all = g0[:, 0, :]
    conv_w_full = jnp.transpose(g0[:, 1:1 + CONV_WIDTH, 0:cshard], (1, 0, 2)).reshape(CONV_WIDTH, dd)
    b_cols = lax.dynamic_slice(b_ada, (0, me * ncol_ada), (1, ncol_ada))
    mod_cols = _ada_fwd(c_all, w_ada[0], b_cols, "ada_fwd")
    (g1,) = _all_gather([mod_cols], "ag_mod")
    mod_mine = lax.dynamic_index_in_dim(g1, me, axis=1, keepdims=False).reshape(6, dd)
    mod = jnp.concatenate([mod_mine, jnp.zeros((2, dd), F32)], axis=0)

    bucket = _bucket_table()
    onehot = jnp.asarray((bucket.reshape(1, -1) == np.arange(N_BUCKETS)[:, None]).astype(np.float32))
    bias = _rel_bias_fwd(rel_bias, onehot, "rel_bias_fwd").reshape(N_Q_HEADS, BLOCK, 2 * BLOCK)
    sinks = attn_sinks.reshape(N_Q_HEADS)
    lp = jnp.concatenate([conv_b, lru_ba, lru_bx, lru_lambda, conv_w_full], axis=0)
    wa, wx = lru_wa[0], lru_wx[0]

    tm = _tile(t, 1024)
    tn = 512
    h = _norm_mod(xs, norm1_g, mod, 0, "norm1")
    tn2 = 1024
    chunk_w = 2 * s_in.shape[0]

    def ids(*ks):
        return jnp.stack([jnp.asarray(k, jnp.int32) for k in ks])

    tn_in = _tile(chunk_w, 1024, LANE)
    (proj,), ((g_near,),) = _mm(
        h, w_in_pair.reshape(chunk_w, dd), "nt", tm=tm, tn=tn_in, outs=[BF16], name="in_proj_own",
        chunks=(ids(chip), chunk_w, ids(0)), n_total=IN_WIDTH, jobs=[_gather_job([w_in_pair], (0.7,), "near")])
    near_ids = ids(2 * (1 - ix) + iy, 2 * ix + 1 - iy)
    (proj,), ((g_far,), (g_o,)) = _mm(
        h, full(g_near), "nt", tm=tm, tn=tn_in, outs=[BF16], name="in_proj_near",
        chunks=(near_ids, chunk_w, near_ids), into=proj,
        jobs=[_gather_job([g_near], (0.6,), "far"), _gather_job([s_o], (0.6, 0.85))])
    far_chip = 2 * (1 - ix) + 1 - iy
    (proj,), ((g_lo,),) = _mm(
        h, full(g_far), "nt", tm=tm, tn=tn_in, outs=[BF16], name="in_proj_far",
        chunks=(ids(far_chip), chunk_w, ids(far_chip)), into=proj, jobs=[_gather_job([s_lo], (0.5, 0.8))])
    far_slots = lax.dynamic_slice_in_dim(g_far, 2 * far_chip, 2, axis=0)
    win_t = lax.dynamic_update_slice_in_dim(g_near, far_slots, 2 * far_chip, axis=0)
    win_t = full(lax.dynamic_update_slice_in_dim(win_t, w_in_pair, 2 * chip, axis=0))
    (g_o,) = _fill_own([g_o], [s_o])
    (att, lse), ((g_f1,),) = _attn_fwd(proj, bias, sinks, "attn_fwd", jobs=[_gather_job([s_f1], (0.6, 0.9))])
    (rec, ya), ((g_ao,),) = _lru_fwd(proj, lp, wa, wx, "lru_fwd", jobs=[_gather_job([s_ao], (0.4, 0.7))])
    g_lo, g_ao, g_f1 = _fill_own([g_lo, g_ao, g_f1], [s_lo, s_ao, s_f1])
    wlo, wao, wo, wf1_t = full(g_lo), full(g_ao), full(g_o), full(g_f1)
    (y_a,) = _mm(ya, wlo, "nn", tm=tm, tn=tn2, outs=[BF16], name="lru_out")

    def merge_epi(acc, ya_t, ga_t, gb_t):
        return acc, _sigmoid(ga_t.astype(F32)) * ya_t.astype(F32) + _sigmoid(gb_t.astype(F32)) * acc

    y_b, merged = _mm(att, wao, "nn", tm=tm, tn=tn, outs=[BF16, BF16], epi=merge_epi,
                      tiles=[(y_a, 0), (proj, OFF_GA // tn), (proj, OFF_GB // tn)], name="attn_out_merge")

    def resid_epi(row):
        def epi(acc, x_t, mod_t):
            return acc, x_t + mod_t[row:row + 1, :] * acc
        return epi

    o1, x1 = _mm(merged, wo, "nn", tm=tm, tn=tn2, outs=[BF16, F32], epi=resid_epi(2),
                 tiles=[(xs, 0)], rows=[mod], name="out_proj")
    h2 = _norm_mod(x1, norm2_g, mod, 3, "norm2")

    def relu2_epi(acc):
        rl = jnp.maximum(acc, 0.0)
        return acc, rl * rl

    (f_pre, ff), ((g_f2,),) = _mm(h2, wf1_t, "nt", tm=tm, tn=1024, outs=[BF16, BF16], epi=relu2_epi, name="ff1",
                                  jobs=[_gather_job([s_f2], (0.6, 0.9))])
    wf2 = full(_fill_own([g_f2], [s_f2])[0])
    tmh = _tile(t, 512)
    o2, x2 = _mm(ff, wf2, "nn", tm=tmh, tn=tn, outs=[BF16, F32], epi=resid_epi(5),
                 tiles=[(x1, 0)], rows=[mod], name="ff2")

    loss_blk, dx2, do2, st_f = _final_loss(x2, o2, _row(final_g), target, mod, "final_loss")
    loss = lax.psum(loss_blk[0, 0], ("x", "y", "c"))

    def drelu2_epi(acc, f_t):
        return (acc * (2.0 * jnp.maximum(f_t.astype(F32), 0.0)),)

    tmw, tnw, tkw = 512, dd, _tile(t, 2048)
    (df,) = _mm(do2, wf2, "nt", tm=tm, tn=1024, outs=[BF16], epi=drelu2_epi, tiles=[(f_pre, 0)], name="ff2_dx")
    (dwf2,) = _mm(ff, do2, "tn", tm=tmw, tn=tnw, tk=tkw, outs=[BF16], name="ff2_dw")
    (dh2,), ((sib_f2,),) = _mm(df, wf1_t, "nn", tm=tmh, tn=tn, outs=[BF16], name="ff1_dx",
                              jobs=[_sibling_job([stack(dwf2)])])
    pair_f2 = _pair_add(stack(dwf2), sib_f2, c_idx, "rs_pair_add_w_ff2")
    (dwf1_t,), ((chips_f2,),) = _mm(df, h2, "tn", tm=tmw, tn=tnw, tk=tkw, outs=[BF16], name="ff1_dw",
                                    jobs=[_chips_job([pair_f2])])
    (dx1, st_2, do1), ((sib_f1,),) = _norm_mod_bwd(x1, norm2_g, mod, 3, dh2, dx2, (o1, 2), "norm2_bwd",
                                                   jobs=[_sibling_job([stack(dwf1_t)])])
    pair_f1 = _pair_add(stack(dwf1_t), sib_f1, c_idx, "rs_pair_add_w_ff1")

    def dmerge_epi(acc, ga_t, gb_t, ya_t, yb_t):
        sa, sb = _sigmoid(ga_t.astype(F32)), _sigmoid(gb_t.astype(F32))
        return (acc * sa, acc * sb, acc * ya_t.astype(F32) * sa * (1.0 - sa), acc * yb_t.astype(F32) * sb * (1.0 - sb))

    tmh = _tile(t, 512)
    dy_a, dy_b, dproj, dgb = _mm(do1, wo, "nt", tm=tm, tn=tn, outs=[BF16] * 4, epi=dmerge_epi,
                                 tiles=[(proj, OFF_GA // tn), (proj, OFF_GB // tn), (y_a, 0), (y_b, 0)],
                                 wide={2: (IN_WIDTH, OFF_GA // tn)}, name="out_proj_dx")
    (dwo,) = _mm(merged, do1, "tn", tm=tmw, tn=tnw, tk=tkw, outs=[BF16], name="out_proj_dw")
    (datt,) = _mm(dy_b, wao, "nt", tm=tm, tn=tn2, outs=[BF16], name="attn_out_dx")
    (dwao,) = _mm(att, dy_b, "tn", tm=tmw, tn=tnw, tk=tkw, outs=[BF16], name="attn_out_dw")
    (dya,) = _mm(dy_a, wlo, "nt", tm=tm, tn=tn2, outs=[BF16], name="lru_out_dx")
    (dwlo,) = _mm(ya, dy_a, "tn", tm=tmw, tn=tnw, tk=tkw, outs=[BF16], name="lru_out_dw")
    (dproj, dk_pad, dv_pad, dbias, dsinks), (sib_3, (chips_f1,)) = _attn_bwd(
        proj, datt, lse, bias, sinks, "attn_bwd", into=dproj,
        jobs=[_sibling_job([stack(dwlo), stack(dwao), stack(dwo)]), _chips_job([pair_f1])])
    pair_lo, pair_ao, pair_o = [_pair_add(stack(g_), s_, c_idx, "rs_pair_add_" + nm_)
                                for g_, s_, nm_ in zip((dwlo, dwao, dwo), sib_3, ("w_lru_out", "w_attn_out", "w_out"))]
    (dproj, st_l, dwa, dwx), ((chips_lo, chips_ao, chips_o),) = _lru_bwd(
        proj, rec, dya, lp, wa, wx, "lru_bwd", into=dproj, jobs=[_chips_job([pair_lo, pair_ao, pair_o])])
    dkv = jnp.concatenate([dk_pad[BLOCK:].astype(BF16), dv_pad[BLOCK:].astype(BF16)], axis=1)
    dproj = lax.dynamic_update_slice_in_dim(dproj, dkv, OFF_K, axis=1)
    dproj = lax.dynamic_update_slice_in_dim(dproj, dgb, OFF_GB, axis=1)
    lru_mats = jnp.concatenate([dwa.reshape(-1, dd), dwx.reshape(-1, dd)], axis=0)
    (dwin_t,), ((mats_all,),) = _mm(dproj, h, "tn", tm=768, tn=tnw, tk=tkw, outs=[BF16], name="in_proj_dw",
                                   jobs=[_gather_job([lru_mats], (0.3, 0.6))])
    (mats_all,) = _fill_own([mats_all], [lru_mats])
    (sib_in,) = _exchange_sibling([stack(dwin_t)], "rs_sibling_w_in")
    pair_in = _pair_add(stack(dwin_t), sib_in, c_idx, "rs_pair_add_w_in", widths=(dd // 2, dd // 2))
    n_rb = t // tm
    dx_tiles = dict(tm=tm, tn=tn2, tk=IN_WIDTH // 4, outs=[BF16])
    if n_rb >= 2:
        (dh,), ((chips_in_0,),) = _mm(dproj, win_t, "nn", name="in_proj_dx_top", rows_part=(0, n_rb // 2),
                                     jobs=[_chips_job(pair_in[0:1])], **dx_tiles)
        (dh,), ((chips_in_1,),) = _mm(dproj, win_t, "nn", name="in_proj_dx_bottom", into=dh,
                                     rows_part=(n_rb // 2, n_rb - n_rb // 2), jobs=[_chips_job(pair_in[1:2])], **dx_tiles)
    else:
        (dh,), ((chips_in_0,), (chips_in_1,)) = _mm(
            dproj, win_t, "nn", name="in_proj_dx", jobs=[_chips_job(pair_in[0:1]), _chips_job(pair_in[1:2])], **dx_tiles)
    grad_x, st_1 = _norm_mod_bwd(xs, norm1_g, mod, 0, dh, dx1, None, "norm1_bwd")
    chips_in = [chips_in_0, chips_in_1]

    drel = jnp.transpose(_rel_bias_bwd(dbias.reshape(N_Q_HEADS, -1), onehot, "rel_bias_bwd"))

    misc = jnp.concatenate([drel.reshape(1, -1), dsinks[0:1, 0:N_Q_HEADS],
                            jnp.zeros((1, dd - N_BUCKETS * N_Q_HEADS - N_Q_HEADS), F32)], axis=1)
    zero_row = jnp.zeros((1, dd), F32)
    small = jnp.concatenate([
        st_1[0:2], st_2[3:4], st_2[0:2], st_f[1:2], zero_row, zero_row,
        st_1[2:3], st_l[0:1], st_l[1:2], st_l[2:3], st_l[3:4], st_2[2:3], st_f[0:1], misc,
        st_l[4:8], jnp.zeros((4, dd), F32)], axis=0)
    (small_all,) = _all_gather([small], "ag_small")

    def pack_small(b_, n1, cb, ba, bx, lam, n2, fg, rb, sk):
        misc_ = jnp.concatenate([rb.reshape(1, -1), sk.reshape(1, -1),
                                 jnp.zeros((1, dd - N_BUCKETS * N_Q_HEADS - N_Q_HEADS), F32)], axis=1)
        return jnp.concatenate([b_.reshape(6, dd), jnp.zeros((2, dd), F32), n1, cb, ba, bx, lam, n2, _row(fg), misc_,
                                jnp.zeros((8, dd), F32)], axis=0)

    def pack_mats(wa_, wx_):
        return jnp.concatenate([wa_.reshape(-1, dd), wx_.reshape(-1, dd)], axis=0)

    every = {i: i for i in range(N_DEV)}
    w_s = pack_small(b_ada, norm1_g, conv_b, lru_ba, lru_bx, lru_lambda, norm2_g, final_g, rel_bias, attn_sinks)
    m_s = pack_small(m_b_ada, m_norm1_g, m_conv_b, m_lru_ba, m_lru_bx, m_lru_lambda, m_norm2_g, m_final_g, m_rel_bias, m_attn_sinks)
    v_s = pack_small(v_b_ada, v_norm1_g, v_conv_b, v_lru_ba, v_lru_bx, v_lru_lambda, v_norm2_g, v_final_g, v_rel_bias, v_attn_sinks)
    small_res = _adamw(w_s, m_s, v_s, [small_all] * N_DEV, "adamw_small", part_index=every)
    mats_res = _adamw(pack_mats(lru_wa, lru_wx), pack_mats(m_lru_wa, m_lru_wx), pack_mats(v_lru_wa, v_lru_wx),
                      [mats_all] * N_DEV, "adamw_lru_mats", part_index=every)

    def unpack_small(s, mt):
        nb_ = N_BUCKETS * N_Q_HEADS
        half = mt.shape[0] // 2
        return dict(
            b_ada=s[0:6].reshape(1, 6 * dd), norm1_g=s[8:9], conv_b=s[9:10], lru_ba=s[10:11], lru_bx=s[11:12],
            lru_lambda=s[12:13], norm2_g=s[13:14], final_g=s[14], rel_bias=s[15, 0:nb_].reshape(N_BUCKETS, N_Q_HEADS),
            attn_sinks=s[15:16, nb_:nb_ + N_Q_HEADS],
            lru_wa=mt[:half].reshape(1, LRU_BLOCKS, LRU_BLOCK_W, LRU_BLOCK_W),
            lru_wx=mt[half:].reshape(1, LRU_BLOCKS, LRU_BLOCK_W, LRU_BLOCK_W))

    res = {k: [None] * 4 for k in ("w_ada", "w_in", "conv_w", "w_lru_out", "w_attn_out", "w_out", "w_ff1", "w_ff2")}
    for q_, (s_, mt_) in enumerate(zip(small_res, mats_res)):
        for k_, val in unpack_small(s_, mt_).items():
            res.setdefault(k_, [None] * 4)[q_] = val

    g_conv = lax.dynamic_slice(small_res[0][16:16 + CONV_WIDTH], (0, me * cshard), (CONV_WIDTH, cshard))
    conv_res = _adamw(conv_w[0], m_conv_w[0], v_conv_w[0], [g_conv], "adamw_conv")
    res["conv_w"] = [r_[None] for r_ in conv_res]

    dmod_all = small_all[:, 0:6, :].reshape(N_DEV, 6 * dd)
    dmod_cols = lax.dynamic_slice(dmod_all, (0, me * ncol_ada), (N_DEV, ncol_ada))
    g_ada = _ada_bwd(jnp.transpose(c_all), dmod_cols, "ada_bwd")
    ada_res = _adamw(w_ada[0], m_w_ada[0], v_w_ada[0], [g_ada], "adamw_ada")
    res["w_ada"] = [r_[None] for r_ in ada_res]

    pair = [None, pair_lo, pair_ao, pair_o, pair_f1, pair_f2]
    from_chips = [None, chips_lo, chips_ao, chips_o, chips_f1, chips_f2]
    own_and_chips = {0: chip_idx, 1: 0, 2: 1, 3: 2}

    def summed(i):
        return [pair[i], from_chips[i], from_chips[i], from_chips[i]], {0: chip_idx, 1: 0, 2: 1, 3: 2}

    def sum_only(i, name):
        parts, index = summed(i)
        return _sum_parts(parts, index, name)

    g_in = jnp.transpose(jnp.concatenate(
        [_sum_parts([p_, c_, c_, c_], own_and_chips, "sum_w_in_%d" % q_)
         for q_, (p_, c_) in enumerate(zip(pair_in, chips_in))], axis=1))
    res["w_in"] = [r_[None] for r_ in _adamw(w_in[0], m_w_in[0], v_w_in[0], [g_in], "adamw_w_in")]
    g_ff1 = jnp.transpose(sum_only(4, "sum_w_ff1"))
    res["w_ff1"] = [r_[None] for r_ in _adamw(w_ff1[0], m_w_ff1[0], v_w_ff1[0], [g_ff1], "adamw_w_ff1")]
    for i, (nm, w_, m_, v_) in {1: ("w_lru_out", w_lru_out, m_w_lru_out, v_w_lru_out),
                                2: ("w_attn_out", w_attn_out, m_w_attn_out, v_w_attn_out),
                                3: ("w_out", w_out, m_w_out, v_w_out),
                                5: ("w_ff2", w_ff2, m_w_ff2, v_w_ff2)}.items():
        parts, index = summed(i)
        res[nm] = [r_[None] for r_ in _adamw(w_[0], m_[0], v_[0], parts, "adamw_" + nm, part_index=index)]

    order = ["w_ada", "b_ada", "norm1_g", "w_in", "conv_w", "conv_b", "lru_wa", "lru_ba", "lru_wx", "lru_bx",
             "lru_lambda", "w_lru_out", "w_attn_out", "attn_sinks", "rel_bias", "w_out", "norm2_g", "w_ff1",
             "w_ff2", "final_g"]
    out = [loss, grad_x[None]]
    for q_ in range(4):
        out += [res[k_][q_] for k_ in order]
    return tuple(out)
```

```python
import functools
import math

import numpy as np
import jax
import jax.numpy as jnp
from jax import lax
from jax.experimental import pallas as pl
from jax.experimental.pallas import tpu as pltpu

F32 = jnp.float32
BF16 = jnp.bfloat16
MESH = pl.DeviceIdType.MESH

D_MODEL = 2048
N_Q_HEADS = 32
N_KV_HEADS = 4
GROUP = N_Q_HEADS // N_KV_HEADS
HEAD_DIM = 64
KV_WIDTH = N_KV_HEADS * HEAD_DIM
BLOCK = 128
NEG_INF = -1e30
N_BUCKETS = 32
MAX_DISTANCE = 128
LRU_BLOCKS = 16
LRU_BLOCK_W = 128
CONV_WIDTH = 4
LRU_C = 8.0
D_FF = 4 * D_MODEL
EPS = 1e-6
IN_WIDTH = 5 * D_MODEL + 2 * KV_WIDTH
OFF_LX, OFF_LG, OFF_Q, OFF_K, OFF_V, OFF_GA, OFF_GB = 0, 2048, 4096, 6144, 6400, 6656, 8704

ADAM_LR, ADAM_B1, ADAM_B2, ADAM_EPS, ADAM_WD, ADAM_STEP = 0.001, 0.9, 0.999, 1e-08, 0.01, 10

N_DEV = 8
VMEM_LIMIT_BYTES = 48 * 1024 * 1024
LANE = 128
LRU_CW = 512


def _params(sem, **kw):
    return pltpu.CompilerParams(dimension_semantics=sem, vmem_limit_bytes=VMEM_LIMIT_BYTES, **kw)


def _tile(n, pref, mult=8):
    if n <= pref:
        return n
    t = (pref // mult) * mult
    while t >= mult:
        if n % t == 0:
            return t
        t -= mult
    return n


def _sigmoid(x):
    return 0.5 * jnp.tanh(0.5 * x) + 0.5


def _gelu_tanh(x):
    k = math.sqrt(2.0 / math.pi)
    t = jnp.tanh(k * (x + 0.044715 * x * x * x))
    return 0.5 * x * (1.0 + t), t


def _gelu_tanh_grad(x, t):
    k = math.sqrt(2.0 / math.pi)
    return 0.5 * (1.0 + t) + 0.5 * x * (1.0 - t * t) * k * (1.0 + 3.0 * 0.044715 * x * x)


def _log1p(e):
    u = 1.0 + e
    return jnp.where(u == 1.0, e, jnp.log(u) * (e / jnp.where(u == 1.0, 1.0, u - 1.0)))


def _softplus(x):
    return jnp.maximum(x, 0.0) + _log1p(jnp.exp(-jnp.abs(x)))


def _neg_expm1(x, exp_x):
    series = -x * (1.0 + x * (0.5 + x * (1.0 / 6.0 + x * (1.0 / 24.0))))
    return jnp.where(x > -0.0625, series, 1.0 - exp_x)


def _my_place():
    return lax.axis_index("x"), lax.axis_index("y"), lax.axis_index("c")


def _all_gather(arrs, name):
    return _fill_own(_run_job(_gather_job(arrs), name), arrs)


def _fill_own(stacks, shards):
    me = 4 * lax.axis_index("x") + 2 * lax.axis_index("y") + lax.axis_index("c")
    out = []
    for g, s in zip(stacks, shards):
        if g.size * g.dtype.itemsize <= (1 << 20):
            slot = lax.broadcasted_iota(jnp.int32, g.shape, 0)
            out.append(jnp.where(slot == me, s[None], g))
        else:
            out.append(lax.dynamic_update_slice_in_dim(g, s[None], me, axis=0))
    return out


class _Job:
    def __init__(self, ins, outs, sems, start, finish, forwards=(), forward_at=()):
        self.ins, self.outs, self.sems = list(ins), list(outs), list(sems)
        self.start, self.finish, self.forwards, self.forward_at = start, finish, list(forwards), list(forward_at)
        assert len(self.forwards) <= len(self.forward_at)


def _gather_job(arrs, forward_at=(0.6, 0.85), part="all"):
    n = len(arrs)
    whole, near, far = part == "all", part == "near", part == "far"

    def copies(ins, outs, sems):
        send_sems, recv_sems = sems
        x, y, c = _my_place()
        sib = (x, y, 1 - c)
        xn, yn, dg = (1 - x, y), (x, 1 - y), (1 - x, 1 - y)

        def slot(px, py, pc):
            return 4 * px + 2 * py + pc

        def cp(a, k, src, dst_slot, to):
            return pltpu.make_async_remote_copy(
                src_ref=src, dst_ref=outs[a].at[dst_slot], send_sem=send_sems.at[a, k], recv_sem=recv_sems.at[a, k],
                device_id=to, device_id_type=MESH)

        def arrival(a, k, dst_slot):
            return cp(a, k, outs[a].at[dst_slot], dst_slot, sib)

        me_slot = slot(x, y, c)
        sends, relay, passes, arrive = {}, {}, {}, {}
        for a in range(n):
            if whole:
                own = ins[a]
                sends[a, 0] = cp(a, 0, own, me_slot, sib)
            if near:
                own = ins[a].at[c]
            if whole or near:
                sends[a, 1] = cp(a, 1, own, me_slot, (*xn, c))
                sends[a, 2] = cp(a, 2, own, me_slot, (*yn, c))
                passes[a, 4] = cp(a, 4, outs[a].at[slot(*xn, c)], slot(*xn, c), sib)
                passes[a, 5] = cp(a, 5, outs[a].at[slot(*yn, c)], slot(*yn, c), sib)
            if whole or far:
                source = outs[a] if whole else ins[a]
                relayed = c * slot(*yn, c) + (1 - c) * slot(*xn, c)
                relay[a, 3] = cp(a, 3, source.at[relayed], relayed, (x + c * (1 - 2 * x), y + (1 - c) * (1 - 2 * y), c))
                passes[a, 6] = cp(a, 6, outs[a].at[slot(*dg, c)], slot(*dg, c), sib)
            arrive[a, 0] = arrival(a, 0, slot(x, y, 1 - c))
            arrive[a, 1] = arrival(a, 1, slot(*xn, c))
            arrive[a, 2] = arrival(a, 2, slot(*yn, c))
            arrive[a, 3] = arrival(a, 3, slot(*dg, c))
            arrive[a, 4] = arrival(a, 4, slot(*xn, 1 - c))
            arrive[a, 5] = arrival(a, 5, slot(*yn, 1 - c))
            arrive[a, 6] = arrival(a, 6, slot(*dg, 1 - c))
        return sends, relay, passes, arrive

    def pick(d, ks):
        return [d[a, k] for a in range(n) for k in ks if (a, k) in d]

    def start(ins, outs, sems):
        sends, relay, _, _ = copies(ins, outs, sems)
        for cp in pick(sends, (0, 1, 2)) + (pick(relay, (3,)) if far else []):
            cp.start()

    def forward_neighbours(ins, outs, sems):
        _, relay, passes, arrive = copies(ins, outs, sems)
        for cp in pick(arrive, (1, 2)):
            cp.wait_recv()
        for cp in pick(relay, (3,)) + pick(passes, (4, 5)):
            cp.start()

    def forward_diagonal(ins, outs, sems):
        _, _, passes, arrive = copies(ins, outs, sems)
        for cp in pick(arrive, (3,)):
            cp.wait_recv()
        for cp in pick(passes, (6,)):
            cp.start()

    def finish(ins, outs, sems):
        sends, relay, passes, arrive = copies(ins, outs, sems)
        for cp in pick(arrive, (6,) if far else (0, 4, 5, 6) if whole else (4, 5)):
            cp.wait_recv()
        for cp in pick(sends, (0, 1, 2)) + pick(relay, (3,)) + pick(passes, (4, 5, 6)):
            cp.wait_send()

    shapes = [a.shape[-2:] for a in arrs]
    forwards = [forward_diagonal] if far else [forward_neighbours] if near else [forward_neighbours, forward_diagonal]
    return _Job(arrs, [jax.ShapeDtypeStruct((N_DEV,) + s, a.dtype) for s, a in zip(shapes, arrs)],
                [pltpu.SemaphoreType.DMA((n, 7)), pltpu.SemaphoreType.DMA((n, 7))],
                start, finish, forwards, forward_at)


def _pair_job(arrs):
    n = len(arrs)

    def copies(ins, outs, sems):
        send_sems, recv_sems = sems
        x, y, c = _my_place()
        remote = [pltpu.make_async_remote_copy(
            src_ref=ins[a], dst_ref=outs[a].at[c], send_sem=send_sems.at[a], recv_sem=recv_sems.at[a],
            device_id=(x, y, 1 - c), device_id_type=MESH) for a in range(n)]
        arrive = [pltpu.make_async_remote_copy(
            src_ref=ins[a], dst_ref=outs[a].at[1 - c], send_sem=send_sems.at[a], recv_sem=recv_sems.at[a],
            device_id=(x, y, 1 - c), device_id_type=MESH) for a in range(n)]
        return remote, arrive

    def start(ins, outs, sems):
        for cp in copies(ins, outs, sems)[0]:
            cp.start()

    def finish(ins, outs, sems):
        remote, arrive = copies(ins, outs, sems)
        for cp in arrive:
            cp.wait_recv()
        for cp in remote:
            cp.wait_send()

    return _Job(arrs, [jax.ShapeDtypeStruct((2,) + a.shape, a.dtype) for a in arrs],
                [pltpu.SemaphoreType.DMA((n,)), pltpu.SemaphoreType.DMA((n,))], start, finish)


def _run_job(job, name):
    ni, no = len(job.ins), len(job.outs)

    def body(*refs):
        ins, outs, sems = refs[:ni], refs[ni:ni + no], refs[ni + no:]
        job.start(ins, outs, sems)
        for fwd in job.forwards:
            fwd(ins, outs, sems)
        job.finish(ins, outs, sems)

    any_spec = pl.BlockSpec(memory_space=pl.ANY)
    return pl.pallas_call(
        body, name=name, out_shape=job.outs, in_specs=[any_spec] * ni, out_specs=[any_spec] * no,
        scratch_shapes=job.sems,
    )(*job.ins)


def _call(body, *, name, grid, in_specs, out_specs, out_shape, scratch_shapes=(), sem, args, jobs=(),
          prefetch=(), aliases=None):
    in_specs, out_specs, out_shape, scratch_shapes = list(in_specs), list(out_specs), list(out_shape), list(scratch_shapes)
    prefetch = list(prefetch)
    if not jobs and not prefetch and not aliases:
        res = pl.pallas_call(body, name=name, grid=grid, in_specs=in_specs, out_specs=out_specs, out_shape=out_shape,
                             scratch_shapes=scratch_shapes, compiler_params=_params(sem))(*args)
        return list(res), []
    n_in, n_out, n_scr = len(in_specs), len(out_specs), len(scratch_shapes)
    j_in = [len(j.ins) for j in jobs]
    j_out = [len(j.outs) for j in jobs]
    j_sem = [len(j.sems) for j in jobs]
    total = int(np.prod(grid))
    any_spec = pl.BlockSpec(memory_space=pl.ANY)

    def wrapped(*refs):
        pos = [0]

        def take(k):
            part = refs[pos[0]:pos[0] + k]
            pos[0] += k
            return part

        take(len(prefetch))
        ins = take(n_in)
        jins = [take(k) for k in j_in]
        outs = take(n_out)
        jouts = [take(k) for k in j_out]
        scr = take(n_scr)
        jsems = [take(k) for k in j_sem]
        step = pl.program_id(0)
        for d in range(1, len(grid)):
            step = step * grid[d] + pl.program_id(d)
        for j, job in enumerate(jobs):
            pl.when(step == 0)(functools.partial(job.start, jins[j], jouts[j], jsems[j]))
            for fwd, frac in zip(job.forwards, job.forward_at):
                at = min(int(total * frac), total - 1)
                pl.when(step == at)(functools.partial(fwd, jins[j], jouts[j], jsems[j]))
        body(*ins, *outs, *scr)
        for j, job in enumerate(jobs):
            pl.when(step == total - 1)(functools.partial(job.finish, jins[j], jouts[j], jsems[j]))

    grid_spec = pltpu.PrefetchScalarGridSpec(
        num_scalar_prefetch=len(prefetch), grid=grid,
        in_specs=in_specs + [any_spec] * sum(j_in),
        out_specs=out_specs + [any_spec] * sum(j_out),
        scratch_shapes=scratch_shapes + [s for j in jobs for s in j.sems])
    res = pl.pallas_call(
        wrapped, name=name, grid_spec=grid_spec,
        out_shape=out_shape + [o for j in jobs for o in j.outs],
        input_output_aliases={len(prefetch) + i: o for i, o in (aliases or {}).items()},
        compiler_params=_params(("arbitrary",) * len(grid)),
    )(*prefetch, *args, *[a for j in jobs for a in j.ins])
    res = list(res)
    own, rest = res[:n_out], res[n_out:]
    per_job = []
    for k in j_out:
        per_job.append(rest[:k])
        rest = rest[k:]
    return own, per_job


def _exchange_sibling(arrs, name):
    return _run_job(_sibling_job(arrs), name)


def _sibling_job(arrs):
    n = len(arrs)

    def copies(ins, outs, sems):
        send_sems, recv_sems = sems
        x, y, c = _my_place()
        return [pltpu.make_async_remote_copy(
            src_ref=ins[a].at[2 * k + 1 - c], dst_ref=outs[a].at[k],
            send_sem=send_sems.at[a, k], recv_sem=recv_sems.at[a, k],
            device_id=(x, y, 1 - c), device_id_type=MESH) for a in range(n) for k in range(4)]

    def start(ins, outs, sems):
        for cp in copies(ins, outs, sems):
            cp.start()

    def finish(ins, outs, sems):
        for cp in copies(ins, outs, sems):
            cp.wait()

    return _Job(arrs, [jax.ShapeDtypeStruct((4,) + a.shape[1:], a.dtype) for a in arrs],
                [pltpu.SemaphoreType.DMA((n, 4)), pltpu.SemaphoreType.DMA((n, 4))], start, finish)


def _chips_job(arrs):
    n = len(arrs)

    def copies(ins, outs, sems):
        send_sems, recv_sems = sems
        x, y, c = _my_place()
        chips = [(1 - x, y), (x, 1 - y), (1 - x, 1 - y)]
        return [pltpu.make_async_remote_copy(
            src_ref=ins[a].at[2 * px + py], dst_ref=outs[a].at[j],
            send_sem=send_sems.at[a, j], recv_sem=recv_sems.at[a, j],
            device_id=(px, py, c), device_id_type=MESH) for a in range(n) for j, (px, py) in enumerate(chips)]

    def start(ins, outs, sems):
        for cp in copies(ins, outs, sems):
            cp.start()

    def finish(ins, outs, sems):
        for cp in copies(ins, outs, sems):
            cp.wait()

    return _Job(arrs, [jax.ShapeDtypeStruct((3,) + a.shape[1:], a.dtype) for a in arrs],
                [pltpu.SemaphoreType.DMA((n, 3)), pltpu.SemaphoreType.DMA((n, 3))], start, finish)


def _pair_add(stack, from_sibling, c_idx, name, widths=None):
    _, r, cc = stack.shape
    tr = _tile(r, 512)
    ws = [cc] if widths is None else list(widths)
    assert sum(ws) == cc
    starts = [sum(ws[:q]) for q in range(len(ws))]

    def body(c_ref, a_ref, b_ref, *o_refs):
        res = a_ref[...].astype(F32) + b_ref[...].astype(F32)
        for o_ref, s0, w_ in zip(o_refs, starts, ws):
            o_ref[...] = res[:, s0:s0 + w_].astype(o_ref.dtype)

    grid_spec = pltpu.PrefetchScalarGridSpec(
        num_scalar_prefetch=1, grid=(4, r // tr),
        in_specs=[pl.BlockSpec((None, tr, cc), lambda k, i, c_ref: (2 * k + c_ref[0], i, 0)),
                  pl.BlockSpec((None, tr, cc), lambda k, i, c_ref: (k, i, 0))],
        out_specs=[pl.BlockSpec((None, tr, w_), lambda k, i, c_ref: (k, i, 0)) for w_ in ws])
    res = pl.pallas_call(
        body, name=name, grid_spec=grid_spec,
        out_shape=[jax.ShapeDtypeStruct((4, r, w_), BF16) for w_ in ws],
        compiler_params=_params(("parallel", "parallel")),
    )(c_idx, stack, from_sibling)
    return res[0] if widths is None else list(res)


_DIMS = {"nn": ((1,), (0,)), "nt": ((1,), (1,)), "tn": ((0,), (0,))}


def _mm(a, b, mode, *, tm, tn, tk=None, outs, epi=None, tiles=(), rows=(), name, jobs=(), chunks=None, into=None,
        n_total=None, rows_part=None, wide=None):
    wide = wide or {}
    if mode == "tn":
        kk, m = a.shape
    else:
        m, kk = a.shape
    n_all = b.shape[0] if mode == "nt" else b.shape[1]
    nn = n_all if chunks is None else chunks[0].shape[0] * chunks[1]
    n_all = n_all if n_total is None else n_total
    tk = kk if tk is None else tk
    assert m % tm == 0 and nn % tn == 0 and kk % tk == 0, (name, a.shape, b.shape, tm, tn, tk)
    nk = kk // tk
    nt, nr, no = len(tiles), len(rows), len(outs)
    n_into = 0 if into is None else 1

    def body(*refs):
        a_ref, b_ref = refs[:2]
        tile_refs = refs[2:2 + nt]
        row_refs = refs[2 + nt:2 + nt + nr]
        out_refs = refs[2 + nt + nr + n_into:2 + nt + nr + n_into + no]
        part = lax.dot_general(a_ref[...], b_ref[...], (_DIMS[mode], ((), ())), preferred_element_type=F32)

        def finish(acc):
            if epi is None:
                res = (acc,)
            else:
                res = epi(acc, *[t[...] for t in tile_refs], *[r[...] for r in row_refs])
            for o_ref, val in zip(out_refs, res):
                o_ref[...] = val.astype(o_ref.dtype)

        if nk == 1:
            finish(part)
        else:
            acc_ref = refs[-1]
            k = pl.program_id(2)

            @pl.when(k == 0)
            def _():
                acc_ref[...] = part

            @pl.when(k > 0)
            def _():
                acc_ref[...] += part

            @pl.when(k == nk - 1)
            def _():
                finish(acc_ref[...])

    if chunks is None:
        col = b_row = lambda j, *_: j
        prefetch = []
    else:
        assert mode == "nt" and not tiles and not rows and chunks[1] % tn == 0
        per = chunks[1] // tn
        col = lambda j, ids, b_ids: ids[j // per] * per + j % per
        b_row = lambda j, ids, b_ids: b_ids[j // per] * per + j % per
        prefetch = [chunks[0], chunks[2]]
    row0, n_row_blocks = (0, m // tm) if rows_part is None else rows_part
    assert rows_part is None or (mode != "tn" and not tiles)
    if mode == "tn":
        a_spec = pl.BlockSpec((tk, tm), lambda i, j, k, *_: (k, i))
    else:
        a_spec = pl.BlockSpec((tm, tk), lambda i, j, k, *_: (i + row0, k))
    if mode == "nt":
        b_spec = pl.BlockSpec((tn, tk), lambda i, j, k, *s: (b_row(j, *s), k))
    else:
        b_spec = pl.BlockSpec((tk, tn), lambda i, j, k, *_: (k, j))
    tile_specs = [pl.BlockSpec((tm, tn), functools.partial(lambda i, j, k, off: (i, j + off), off=off))
                  for _, off in tiles]
    row_specs = [pl.BlockSpec((r.shape[0], tn), lambda i, j, k: (0, j)) for r in rows]
    out_spec = pl.BlockSpec((tm, tn), lambda i, j, k, *s: (i + row0, col(j, *s)))
    args = [a, b, *[t for t, _ in tiles], *rows]
    in_specs = [a_spec, b_spec] + tile_specs + row_specs
    aliases = None
    if into is not None:
        aliases = {len(args): 0}
        args.append(into)
        in_specs.append(pl.BlockSpec(memory_space=pl.ANY))
    res, jres = _call(
        body, name=name, grid=(n_row_blocks, nn // tn, nk),
        in_specs=in_specs,
        out_specs=[out_spec if q not in wide else
                   pl.BlockSpec((tm, tn), functools.partial(lambda i, j, k, off: (i, j + off), off=wide[q][1]))
                   for q in range(no)],
        out_shape=[jax.ShapeDtypeStruct((m, wide[q][0] if q in wide else n_all), dt) for q, dt in enumerate(outs)],
        scratch_shapes=[pltpu.VMEM((tm, tn), F32)] if nk > 1 else [],
        sem=("parallel", "parallel", "arbitrary"),
        args=args, jobs=jobs, prefetch=prefetch, aliases=aliases)
    return (res, jres) if jobs else res


def _ada_fwd(c_all, w_ada, b_ada_cols, name):
    nb, dd = c_all.shape
    ncol = w_ada.shape[1]
    tn = _tile(ncol, 512, LANE)

    def body(c_ref, w_ref, b_ref, o_ref):
        cv = c_ref[...]
        act = (cv * _sigmoid(cv)).astype(BF16)
        o_ref[...] = jnp.dot(act, w_ref[...].astype(BF16), preferred_element_type=F32) + b_ref[...]

    return pl.pallas_call(
        body, name=name, grid=(ncol // tn,),
        in_specs=[pl.BlockSpec((nb, dd), lambda j: (0, 0)), pl.BlockSpec((dd, tn), lambda j: (0, j)),
                  pl.BlockSpec((1, tn), lambda j: (0, j))],
        out_specs=pl.BlockSpec((nb, tn), lambda j: (0, j)),
        out_shape=jax.ShapeDtypeStruct((nb, ncol), F32),
        compiler_params=_params(("parallel",)),
    )(c_all, w_ada, b_ada_cols)


def _ada_bwd(c_all_t, dmod_cols, name):
    dd, nb = c_all_t.shape
    ncol = dmod_cols.shape[1]
    tr = _tile(dd, 256)

    def body(c_ref, d_ref, o_ref):
        cv = c_ref[...]
        act = (cv * _sigmoid(cv)).astype(BF16).astype(F32)
        dm = d_ref[...].astype(BF16).astype(F32)
        acc = act[:, 0:1] * dm[0:1, :]
        for bi in range(1, nb):
            acc = acc + act[:, bi:bi + 1] * dm[bi:bi + 1, :]
        o_ref[...] = acc

    return pl.pallas_call(
        body, name=name, grid=(dd // tr,),
        in_specs=[pl.BlockSpec((tr, nb), lambda i: (i, 0)), pl.BlockSpec((nb, ncol), lambda i: (0, 0))],
        out_specs=pl.BlockSpec((tr, ncol), lambda i: (i, 0)),
        out_shape=jax.ShapeDtypeStruct((dd, ncol), F32),
        compiler_params=_params(("parallel",)),
    )(c_all_t, dmod_cols)


def _norm_mod(x, gain, mod, row0, name):
    t, dd = x.shape
    tt = _tile(t, 512)

    def body(x_ref, g_ref, m_ref, h_ref):
        xv = x_ref[...]
        r = lax.rsqrt(jnp.mean(xv * xv, axis=-1, keepdims=True) + EPS)
        shift, scale = m_ref[row0:row0 + 1, :], m_ref[row0 + 1:row0 + 2, :]
        h_ref[...] = ((xv * r * g_ref[...]) * (1.0 + scale) + shift).astype(BF16)

    return pl.pallas_call(
        body, name=name, grid=(t // tt,),
        in_specs=[pl.BlockSpec((tt, dd), lambda i: (i, 0)), pl.BlockSpec((1, dd), lambda i: (0, 0)),
                  pl.BlockSpec((8, dd), lambda i: (0, 0))],
        out_specs=pl.BlockSpec((tt, dd), lambda i: (i, 0)),
        out_shape=jax.ShapeDtypeStruct((t, dd), BF16),
        compiler_params=_params(("parallel",)),
    )(x, gain, mod)


def _norm_mod_bwd(x, gain, mod, row0, dh, dres, branch, name, jobs=()):
    t, dd = x.shape
    tt = _tile(t, 512)
    has_branch = branch is not None

    def body(*refs):
        if has_branch:
            x_ref, g_ref, m_ref, dh_ref, dres_ref, o_ref, dx_ref, st_ref, do_ref = refs[:-1]
        else:
            x_ref, g_ref, m_ref, dh_ref, dres_ref, dx_ref, st_ref = refs[:-1]

        acc_ref = refs[-1]
        step_i = pl.program_id(0)

        @pl.when(step_i == 0)
        def _():
            acc_ref[...] = jnp.zeros_like(acc_ref)

        def fold(v):
            out = v[0:8]
            for q in range(1, rc // 8):
                out = out + v[8 * q:8 * q + 8]
            return out

        def chunk(ci, carry):
            rows = pl.ds(pl.multiple_of(ci * rc, rc), rc)
            xv = x_ref[rows, :]
            g = g_ref[...]
            r = lax.rsqrt(jnp.mean(xv * xv, axis=-1, keepdims=True) + EPS)
            nv = xv * r
            dhv = dh_ref[rows, :].astype(F32)
            acc_ref[0] += fold(dhv)
            acc_ref[1] += fold(dhv * (nv * g))
            dng = dhv * (1.0 + m_ref[row0 + 1:row0 + 2, :])
            acc_ref[2] += fold(dng * nv)
            dn = dng * g
            dx = dres_ref[rows, :] + r * (dn - nv * jnp.mean(dn * nv, axis=-1, keepdims=True))
            dx_ref[rows, :] = dx
            if has_branch:
                acc_ref[3] += fold(dx * o_ref[rows, :].astype(F32))
                do_ref[rows, :] = (dx * m_ref[branch[1]:branch[1] + 1, :]).astype(BF16)
            return carry

        lax.fori_loop(0, tt // rc, chunk, 0, unroll=4)

        @pl.when(step_i == nsteps - 1)
        def _():
            st_ref[...] = jnp.zeros_like(st_ref)
            for k in range(4):
                st_ref[k:k + 1, :] = jnp.sum(acc_ref[k], axis=0, keepdims=True)

    rc = 16 if tt % 16 == 0 else 8
    nsteps = t // tt
    row = pl.BlockSpec((tt, dd), lambda i: (i, 0))
    in_specs = [row, pl.BlockSpec((1, dd), lambda i: (0, 0)), pl.BlockSpec((8, dd), lambda i: (0, 0)), row, row]
    args = [x, gain, mod, dh, dres]
    out_specs = [row, pl.BlockSpec((8, dd), lambda i: (0, 0))]
    out_shape = [jax.ShapeDtypeStruct((t, dd), F32), jax.ShapeDtypeStruct((8, dd), F32)]
    if has_branch:
        in_specs.append(row)
        args.append(branch[0])
        out_specs.append(row)
        out_shape.append(jax.ShapeDtypeStruct((t, dd), BF16))
    res, jres = _call(body, name=name, grid=(t // tt,), in_specs=in_specs, out_specs=out_specs, out_shape=out_shape,
                      scratch_shapes=[pltpu.VMEM((4, 8, dd), F32)], sem=("arbitrary",), args=args, jobs=jobs)
    return (res, jres) if jobs else res


def _final_loss(x2, o2, final_g, target, mod, name):
    t, dd = x2.shape
    tt = _tile(t, 256)
    nsteps = t // tt

    def body(x_ref, o_ref, g_ref, t_ref, m_ref, loss_ref, dx_ref, do_ref, st_ref, lacc):
        i = pl.program_id(0)

        @pl.when(i == 0)
        def _():
            st_ref[...] = jnp.zeros_like(st_ref)
            lacc[...] = jnp.zeros_like(lacc)

        xv = x_ref[...]
        g = g_ref[...]
        r = lax.rsqrt(jnp.mean(xv * xv, axis=-1, keepdims=True) + EPS)
        nv = xv * r
        err = nv * g - t_ref[...]
        lacc[...] += jnp.sum(err * err, axis=0, keepdims=True)
        dy = err * (1.0 / dd)
        st_ref[0:1, :] += jnp.sum(dy * nv, axis=0, keepdims=True)
        dn = dy * g
        dx = r * (dn - nv * jnp.mean(dn * nv, axis=-1, keepdims=True))
        dx_ref[...] = dx
        st_ref[1:2, :] += jnp.sum(dx * o_ref[...].astype(F32), axis=0, keepdims=True)
        do_ref[...] = (dx * m_ref[5:6, :]).astype(BF16)

        @pl.when(i == nsteps - 1)
        def _():
            total = jnp.sum(lacc[...], axis=-1, keepdims=True) * (0.5 / dd)
            loss_ref[...] = jnp.broadcast_to(total, loss_ref.shape)

    row = pl.BlockSpec((tt, dd), lambda i: (i, 0))
    return pl.pallas_call(
        body, name=name, grid=(nsteps,),
        in_specs=[row, row, pl.BlockSpec((1, dd), lambda i: (0, 0)), row, pl.BlockSpec((8, dd), lambda i: (0, 0))],
        out_specs=[pl.BlockSpec((8, LANE), lambda i: (0, 0)), row, row, pl.BlockSpec((8, dd), lambda i: (0, 0))],
        out_shape=[jax.ShapeDtypeStruct((8, LANE), F32), jax.ShapeDtypeStruct((t, dd), F32),
                   jax.ShapeDtypeStruct((t, dd), BF16), jax.ShapeDtypeStruct((8, dd), F32)],
        scratch_shapes=[pltpu.VMEM((1, dd), F32)],
        compiler_params=_params(("arbitrary",)),
    )(x2, o2, final_g, target, mod)


def _lru_gates(xt, prev8, lp, wa_ref, wx_ref, first_tile, cw):
    tt = xt.shape[0]
    ext = jnp.concatenate([prev8, xt], axis=0)
    xs = [pltpu.roll(ext, s, 0)[8:, :] if s else xt for s in range(CONV_WIDTH)]
    xc = lp[0:1, :] + lp[7:8, :] * xs[0] + lp[6:7, :] * xs[1] + lp[5:6, :] * xs[2] + lp[4:5, :] * xs[3]
    xcb = xc.astype(BF16)
    za, zi = [], []
    for jb in range(cw // LRU_BLOCK_W):
        sl = slice(jb * LRU_BLOCK_W, (jb + 1) * LRU_BLOCK_W)
        za.append(jnp.dot(xcb[:, sl], wa_ref[jb].astype(BF16), preferred_element_type=F32))
        zi.append(jnp.dot(xcb[:, sl], wx_ref[jb].astype(BF16), preferred_element_type=F32))
    ra = _sigmoid(jnp.concatenate(za, axis=1) + lp[1:2, :])
    ri = _sigmoid(jnp.concatenate(zi, axis=1) + lp[2:3, :])
    sp = _softplus(-lp[3:4, :])
    log_a = -LRU_C * ra * sp
    av = jnp.exp(log_a)
    seq_start = jnp.logical_and(first_tile, lax.broadcasted_iota(jnp.int32, (tt, 1), 0) == 0)
    m2 = _neg_expm1(2.0 * log_a, av * av)
    inv_mult = jnp.where(seq_start, 1.0, lax.rsqrt(jnp.maximum(m2, 1e-30)))
    mult = jnp.where(seq_start, 1.0, m2 * inv_mult)
    return xs, xc, xcb, ra, ri, sp, av, mult, inv_mult, seq_start


def _scan_groups(a, u, reverse):
    tt, cw = a.shape
    a3, u3 = a.reshape(tt // 8, 8, cw), u.reshape(tt // 8, 8, cw)
    r = lax.broadcasted_iota(jnp.int32, (1, 8, 1), 1)
    for s in (1, 2, 4):
        shift = 8 - s if reverse else s
        take = r < 8 - s if reverse else r >= s
        u3 = jnp.where(take, a3 * pltpu.roll(u3, shift, 1) + u3, u3)
        a3 = jnp.where(take, a3 * pltpu.roll(a3, shift, 1), a3)
    return a3, u3


def _scan_carry(a_s, u_s, h_s, carry, reverse):
    ng = a_s.shape[0]

    def step(i, h):
        g = ng - 1 - i if reverse else i
        hg = u_s[g] + a_s[g] * h
        h_s[g] = hg
        return hg[0:1, :] if reverse else hg[7:8, :]

    return lax.fori_loop(0, ng, step, carry, unroll=4)


def _lru_fwd(proj, lp, wa, wx, name, jobs=()):
    t = proj.shape[0]
    w = D_MODEL
    cw = LRU_CW
    tt = _tile(t, 256)
    nb = cw // LRU_BLOCK_W

    def body(lx_ref, lg_ref, lp_ref, wa_ref, wx_ref, rec_ref, ya_ref, a_s, u_s, h_s, halo, carry):
        ti = pl.program_id(1)

        @pl.when(ti == 0)
        def _():
            halo[...] = jnp.zeros_like(halo)
            carry[...] = jnp.zeros_like(carry)

        xt = lx_ref[...].astype(F32)
        lp_v = lp_ref[...]
        _, xc, _, _, ri, _, av, mult, _, _ = _lru_gates(xt, halo[...], lp_v, wa_ref, wx_ref, ti == 0, cw)
        halo[...] = xt[tt - 8:, :]
        a_s[...], u_s[...] = _scan_groups(av, mult * (ri * xc), False)
        carry[...] = _scan_carry(a_s, u_s, h_s, carry[...], False)
        rec = h_s[...].reshape(tt, cw)
        rec_ref[...] = rec.astype(BF16)
        gelu, _ = _gelu_tanh(lg_ref[...].astype(F32))
        ya_ref[...] = (rec * gelu).astype(BF16)

    off_g = OFF_LG // cw
    res, jres = _call(
        body, name=name, grid=(w // cw, t // tt),
        in_specs=[pl.BlockSpec((tt, cw), lambda ci, ti: (ti, ci)),
                  pl.BlockSpec((tt, cw), lambda ci, ti: (ti, off_g + ci)),
                  pl.BlockSpec((8, cw), lambda ci, ti: (0, ci)),
                  pl.BlockSpec((nb, LRU_BLOCK_W, LRU_BLOCK_W), lambda ci, ti: (ci, 0, 0)),
                  pl.BlockSpec((nb, LRU_BLOCK_W, LRU_BLOCK_W), lambda ci, ti: (ci, 0, 0))],
        out_specs=[pl.BlockSpec((tt, cw), lambda ci, ti: (ti, ci))] * 2,
        out_shape=[jax.ShapeDtypeStruct((t, w), BF16)] * 2,
        scratch_shapes=[pltpu.VMEM((tt // 8, 8, cw), F32)] * 3 + [pltpu.VMEM((8, cw), F32), pltpu.VMEM((1, cw), F32)],
        sem=("parallel", "arbitrary"), args=[proj, proj, lp, wa, wx], jobs=jobs)
    return (res, jres) if jobs else res


def _lru_bwd(proj, rec, dya, lp, wa, wx, name, jobs=(), into=None):
    t = proj.shape[0]
    w = D_MODEL
    cw = LRU_CW if into is None else w
    tt = _tile(t, 256 if into is None else 128)
    nt = t // tt
    nb = cw // LRU_BLOCK_W
    r8 = tt // 8

    def body(lx_ref, lxp_ref, lg_ref, rec_ref, recp_ref, dya_ref, lp_ref, wa_ref, wx_ref, *rest):
        if into is None:
            dlx_ref, dlg_ref = rest[0], rest[1]
        else:
            dlx_ref, dlg_ref = rest[1].at[:, 0:cw], rest[1].at[:, cw:2 * cw]
        st_ref, dwa_ref, dwx_ref, a_s, d_s, dh_s, nhalo, carry = rest[-8:]
        step_i = pl.program_id(1)
        ti = nt - 1 - step_i

        @pl.when(step_i == 0)
        def _():
            st_ref[...] = jnp.zeros_like(st_ref)
            dwa_ref[...] = jnp.zeros_like(dwa_ref)
            dwx_ref[...] = jnp.zeros_like(dwx_ref)
            nhalo[...] = jnp.zeros_like(nhalo)
            carry[...] = jnp.zeros_like(carry)

        first = ti == 0
        keep = jnp.where(first, 0.0, 1.0)
        xt = lx_ref[...].astype(F32)
        prev8 = lxp_ref[...].astype(F32) * keep
        lp_v = lp_ref[...]
        xs, xc, xcb, ra, ri, sp, av, mult, inv_mult, seq_start = _lru_gates(xt, prev8, lp_v, wa_ref, wx_ref, first, cw)

        lg = lg_ref[...].astype(F32)
        gelu, th = _gelu_tanh(lg)
        dyav = dya_ref[...].astype(F32)
        recv = rec_ref[...].astype(F32)
        dlg_ref[...] = (dyav * recv * _gelu_tanh_grad(lg, th)).astype(BF16)

        drec = dyav * gelu
        e_in = carry[...]
        a_s[...], d_s[...] = _scan_groups(av, av * drec, True)
        carry[...] = _scan_carry(a_s, d_s, dh_s, e_in, True)
        e_next = jnp.concatenate([dh_s[...].reshape(tt, cw), jnp.broadcast_to(e_in, (8, cw))], axis=0)
        dh = drec + pltpu.roll(e_next, tt + 7, 0)[:tt, :]
        rprev8 = recp_ref[...].astype(F32) * keep
        hprev = pltpu.roll(jnp.concatenate([rprev8, recv], axis=0), 1, 0)[8:, :]
        da = dh * hprev
        dmult = jnp.where(seq_start, 0.0, dh * ri * xc)
        dri = dh * mult * xc
        dxc = dh * mult * ri
        dlog_a = da * av - dmult * (av * av) * inv_mult
        dra = dlog_a * (-LRU_C * sp)
        lam = lp_v[3:4, :]
        st_ref[3:4, :] += jnp.sum(dlog_a * ra, axis=0, keepdims=True) * (LRU_C * _sigmoid(-lam))
        dza = dra * ra * (1.0 - ra)
        dzi = dri * ri * (1.0 - ri)
        st_ref[1:2, :] += jnp.sum(dza, axis=0, keepdims=True)
        st_ref[2:3, :] += jnp.sum(dzi, axis=0, keepdims=True)
        dzab, dzib = dza.astype(BF16), dzi.astype(BF16)
        back = []
        for jb in range(nb):
            sl = slice(jb * LRU_BLOCK_W, (jb + 1) * LRU_BLOCK_W)
            dwa_ref[jb] += lax.dot_general(xcb[:, sl], dzab[:, sl], (_DIMS["tn"], ((), ())), preferred_element_type=F32)
            dwx_ref[jb] += lax.dot_general(xcb[:, sl], dzib[:, sl], (_DIMS["tn"], ((), ())), preferred_element_type=F32)
            back.append(
                lax.dot_general(dzab[:, sl], wa_ref[jb].astype(BF16), (_DIMS["nt"], ((), ())), preferred_element_type=F32)
                + lax.dot_general(dzib[:, sl], wx_ref[jb].astype(BF16), (_DIMS["nt"], ((), ())), preferred_element_type=F32))
        dxc = dxc + jnp.concatenate(back, axis=1)
        st_ref[0:1, :] += jnp.sum(dxc, axis=0, keepdims=True)
        for k in range(CONV_WIDTH):
            st_ref[4 + k:5 + k, :] += jnp.sum(dxc * xs[CONV_WIDTH - 1 - k], axis=0, keepdims=True)
        ext = jnp.concatenate([dxc, nhalo[...]], axis=0)
        dlx = lp_v[7:8, :] * dxc
        for s in range(1, CONV_WIDTH):
            dlx = dlx + lp_v[7 - s:8 - s, :] * pltpu.roll(ext, tt + 8 - s, 0)[:tt, :]
        dlx_ref[...] = dlx.astype(BF16)
        nhalo[...] = dxc[0:8, :]

    off_g = OFF_LG // cw
    tile = lambda off: pl.BlockSpec((tt, cw), lambda ci, si: (nt - 1 - si, off + ci))
    prev = lambda off: pl.BlockSpec((8, cw), lambda ci, si: (jnp.maximum((nt - 1 - si) * r8 - 1, 0), off + ci))
    wspec = pl.BlockSpec((nb, LRU_BLOCK_W, LRU_BLOCK_W), lambda ci, si: (ci, 0, 0))
    st_spec = pl.BlockSpec((8, cw), lambda ci, si: (0, ci))
    in_specs = [tile(0), prev(0), tile(off_g), tile(0), prev(0), tile(0), st_spec, wspec, wspec]
    args = [proj, proj, proj, rec, rec, dya, lp, wa, wx]
    if into is None:
        d_specs = [tile(0), tile(0)]
        d_shapes = [jax.ShapeDtypeStruct((t, w), BF16), jax.ShapeDtypeStruct((t, w), BF16)]
        aliases = None
    else:
        d_specs = [pl.BlockSpec((tt, 2 * cw), lambda ci, si: (nt - 1 - si, 0))]
        d_shapes = [jax.ShapeDtypeStruct(into.shape, BF16)]
        aliases = {len(args): 0}
        args.append(into)
        in_specs.append(pl.BlockSpec(memory_space=pl.ANY))
    res, jres = _call(
        body, name=name, grid=(w // cw, nt), in_specs=in_specs,
        out_specs=d_specs + [st_spec, wspec, wspec],
        out_shape=d_shapes + [jax.ShapeDtypeStruct((8, w), F32),
                              jax.ShapeDtypeStruct((LRU_BLOCKS, LRU_BLOCK_W, LRU_BLOCK_W), F32),
                              jax.ShapeDtypeStruct((LRU_BLOCKS, LRU_BLOCK_W, LRU_BLOCK_W), F32)],
        scratch_shapes=[pltpu.VMEM((tt // 8, 8, cw), F32)] * 3 + [pltpu.VMEM((8, cw), F32), pltpu.VMEM((1, cw), F32)],
        sem=("parallel", "arbitrary"), args=args, jobs=jobs, aliases=aliases)
    return (res, jres) if jobs else res


def _band_valid(first_block):
    qi = lax.broadcasted_iota(jnp.int32, (BLOCK, 2 * BLOCK), 0)
    ki = lax.broadcasted_iota(jnp.int32, (BLOCK, 2 * BLOCK), 1)
    rel = qi + BLOCK - ki
    valid = jnp.logical_and(rel >= 0, rel < BLOCK)
    return jnp.logical_and(valid, jnp.logical_or(ki >= BLOCK, jnp.logical_not(first_block)))


ATTN_STACK_BWD = 8


def _low_half():
    return lax.broadcasted_iota(jnp.int32, (1, LANE), 1) < HEAD_DIM


def _stack_heads(x, h0, ns):
    low = _low_half()
    parts = []
    for g in range(ns):
        h = h0 + g
        slab = x[:, (h // 2) * LANE:(h // 2 + 1) * LANE]
        parts.append(jnp.where(low if h % 2 == 0 else jnp.logical_not(low), slab, jnp.zeros_like(slab)))
    return jnp.concatenate(parts, axis=0)


def _unstack_heads(y, ns):
    low = _low_half()
    pairs = [jnp.where(low, y[(2 * j) * BLOCK:(2 * j + 1) * BLOCK], y[(2 * j + 1) * BLOCK:(2 * j + 2) * BLOCK])
             for j in range(ns // 2)]
    return pairs[0] if len(pairs) == 1 else jnp.concatenate(pairs, axis=1)


def _dup_kv(xall, kv):
    part = xall[:, kv * HEAD_DIM:(kv + 1) * HEAD_DIM]
    return jnp.concatenate([part, part], axis=1)


def _fold_halves(a):
    return a + pltpu.roll(a, HEAD_DIM, 1)


def _group_probs(qg, k2, bias_g, sink_g, lse_g, valid):
    s = lax.dot_general(qg, k2, (_DIMS["nt"], ((), ())), preferred_element_type=F32) * (HEAD_DIM ** -0.5)
    s = jnp.where(valid[None], s.reshape(bias_g.shape) + bias_g, NEG_INF)
    return jnp.exp(s - lse_g), jnp.exp(sink_g - lse_g)


def _group_sinks(s_ref, h0, ns):
    return jnp.concatenate([jnp.full((1, BLOCK, 1), s_ref[h0 + g], F32) for g in range(ns)], axis=0)


def _attn_specs(t):
    nblk = t // BLOCK
    q_spec = pl.BlockSpec((BLOCK, D_MODEL), lambda n: (n, OFF_Q // D_MODEL))
    kc = pl.BlockSpec((BLOCK, KV_WIDTH), lambda n: (n, OFF_K // KV_WIDTH))
    kp = pl.BlockSpec((BLOCK, KV_WIDTH), lambda n: (jnp.maximum(n - 1, 0), OFF_K // KV_WIDTH))
    vc = pl.BlockSpec((BLOCK, KV_WIDTH), lambda n: (n, OFF_V // KV_WIDTH))
    vp = pl.BlockSpec((BLOCK, KV_WIDTH), lambda n: (jnp.maximum(n - 1, 0), OFF_V // KV_WIDTH))
    bias_spec = pl.BlockSpec((N_Q_HEADS, BLOCK, 2 * BLOCK), lambda n: (0, 0, 0))
    sink_spec = pl.BlockSpec(memory_space=pltpu.SMEM)
    return nblk, [q_spec, kp, kc, vp, vc, bias_spec, sink_spec]


def _attn_fwd(proj, bias, sinks, name, jobs=()):
    t = proj.shape[0]
    nblk, in_specs = _attn_specs(t)

    def body(q_ref, kp_ref, kc_ref, vp_ref, vc_ref, b_ref, s_ref, o_ref, lse_ref):
        valid = _band_valid(pl.program_id(0) == 0)
        lane = lax.broadcasted_iota(jnp.int32, (1, LANE), 1)
        q = q_ref[...]
        kall = jnp.concatenate([kp_ref[...], kc_ref[...]], axis=0)
        vall = jnp.concatenate([vp_ref[...], vc_ref[...]], axis=0)
        outs = []
        lse = jnp.zeros((BLOCK, LANE), F32)
        for h in range(N_Q_HEADS):
            kv = h // GROUP
            kk = kall[:, kv * HEAD_DIM:(kv + 1) * HEAD_DIM]
            vv = vall[:, kv * HEAD_DIM:(kv + 1) * HEAD_DIM]
            s = lax.dot_general(q[:, h * HEAD_DIM:(h + 1) * HEAD_DIM], kk, (_DIMS["nt"], ((), ())),
                                preferred_element_type=F32) * (HEAD_DIM ** -0.5)
            s = jnp.where(valid, s + b_ref[h], NEG_INF)
            sink = s_ref[h]
            m = jnp.maximum(jnp.max(s, axis=-1, keepdims=True), sink)
            e = jnp.exp(s - m)
            den = jnp.sum(e, axis=-1, keepdims=True) + jnp.exp(sink - m)
            lse = jnp.where(lane == h, m + jnp.log(den), lse)
            outs.append(jnp.dot((e * (1.0 / den)).astype(BF16), vv, preferred_element_type=F32))
        o_ref[...] = jnp.concatenate(outs, axis=1).astype(BF16)
        lse_ref[...] = lse

    res, jres = _call(
        body, name=name, grid=(nblk,), in_specs=in_specs,
        out_specs=[pl.BlockSpec((BLOCK, D_MODEL), lambda n: (n, 0)), pl.BlockSpec((BLOCK, LANE), lambda n: (n, 0))],
        out_shape=[jax.ShapeDtypeStruct((t, D_MODEL), BF16), jax.ShapeDtypeStruct((t, LANE), F32)],
        sem=("parallel",), args=[proj, proj, proj, proj, proj, bias, sinks], jobs=jobs)
    return (res, jres) if jobs else res


def _attn_bwd(proj, datt, lse, bias, sinks, name, jobs=(), into=None):
    t = proj.shape[0]
    nblk, in_specs = _attn_specs(t)
    in_specs = in_specs + [pl.BlockSpec((BLOCK, D_MODEL), lambda n: (n, 0)), pl.BlockSpec((BLOCK, LANE), lambda n: (n, 0))]
    scale = HEAD_DIM ** -0.5
    args = [proj, proj, proj, proj, proj, bias, sinks, datt, lse]
    dq_spec = pl.BlockSpec((BLOCK, D_MODEL), lambda n: (n, 0))
    dq_shape = jax.ShapeDtypeStruct((t, D_MODEL), BF16)
    aliases = None
    if into is not None:
        aliases = {len(args): 0}
        args.append(into)
        in_specs.append(pl.BlockSpec(memory_space=pl.ANY))
        dq_spec = pl.BlockSpec((BLOCK, D_MODEL), lambda n: (n, OFF_Q // D_MODEL))
        dq_shape = jax.ShapeDtypeStruct(into.shape, BF16)

    def body(q_ref, kp_ref, kc_ref, vp_ref, vc_ref, b_ref, s_ref, do_ref, lse_ref, *rest):
        dq_ref, dk_ref, dv_ref, db_ref, ds_ref = rest[-5:]
        n = pl.program_id(0)

        @pl.when(n == 0)
        def _():
            dk_ref[...] = jnp.zeros_like(dk_ref)
            dv_ref[...] = jnp.zeros_like(dv_ref)
            db_ref[...] = jnp.zeros_like(db_ref)
            ds_ref[...] = jnp.zeros_like(ds_ref)

        valid = _band_valid(n == 0)
        lane = lax.broadcasted_iota(jnp.int32, (1, LANE), 1)
        q = q_ref[...]
        do = do_ref[...]
        lse_all = lse_ref[...]
        kall = jnp.concatenate([kp_ref[...], kc_ref[...]], axis=0)
        vall = jnp.concatenate([vp_ref[...], vc_ref[...]], axis=0)
        low = _low_half()
        dqs, dks, dvs = [], [], []
        dsink_row = jnp.zeros((1, LANE), F32)
        ns = ATTN_STACK_BWD
        for kv in range(N_KV_HEADS):
            k2, v2 = _dup_kv(kall, kv), _dup_kv(vall, kv)
            dk_acc = jnp.zeros((2 * BLOCK, LANE), F32)
            dv_acc = jnp.zeros((2 * BLOCK, LANE), F32)
            for h0 in range(kv * GROUP, (kv + 1) * GROUP, ns):
                qg, dog = _stack_heads(q, h0, ns), _stack_heads(do, h0, ns)
                lse_g = jnp.concatenate([lse_all[:, h0 + g:h0 + g + 1][None] for g in range(ns)], axis=0)
                p, psink = _group_probs(qg, k2, b_ref[h0:h0 + ns], _group_sinks(s_ref, h0, ns), lse_g, valid)
                dp = lax.dot_general(dog, v2, (_DIMS["nt"], ((), ())), preferred_element_type=F32)
                dp = dp.reshape(ns, BLOCK, 2 * BLOCK)
                delta = jnp.sum(p * dp, axis=-1, keepdims=True)
                ds = p * (dp - delta)
                db_ref[h0:h0 + ns] += ds
                dsink = -jnp.sum(psink * delta, axis=1, keepdims=True)
                for g in range(ns):
                    dsink_row = dsink_row + jnp.where(lane == h0 + g, dsink[g], 0.0)
                dsb = ds.reshape(ns * BLOCK, 2 * BLOCK).astype(BF16)
                pb = p.reshape(ns * BLOCK, 2 * BLOCK).astype(BF16)
                dqs.append(_unstack_heads(jnp.dot(dsb, k2, preferred_element_type=F32) * scale, ns))
                dk_acc = dk_acc + lax.dot_general(dsb, qg, (_DIMS["tn"], ((), ())), preferred_element_type=F32)
                dv_acc = dv_acc + lax.dot_general(pb, dog, (_DIMS["tn"], ((), ())), preferred_element_type=F32)
            dks.append(_fold_halves(dk_acc) * scale)
            dvs.append(_fold_halves(dv_acc))
        dq_ref[...] = jnp.concatenate(dqs, axis=1).astype(BF16)
        rows = pl.ds(pl.multiple_of(n * BLOCK, BLOCK), 2 * BLOCK)
        dk_ref[rows, :] += jnp.concatenate([jnp.where(low, dks[0], dks[1]), jnp.where(low, dks[2], dks[3])], axis=1)
        dv_ref[rows, :] += jnp.concatenate([jnp.where(low, dvs[0], dvs[1]), jnp.where(low, dvs[2], dvs[3])], axis=1)
        ds_ref[0:1, :] += dsink_row

    res, jres = _call(
        body, name=name, grid=(nblk,), in_specs=in_specs,
        out_specs=[dq_spec,
                   pl.BlockSpec((t + BLOCK, KV_WIDTH), lambda n: (0, 0)),
                   pl.BlockSpec((t + BLOCK, KV_WIDTH), lambda n: (0, 0)),
                   pl.BlockSpec((N_Q_HEADS, BLOCK, 2 * BLOCK), lambda n: (0, 0, 0)),
                   pl.BlockSpec((8, LANE), lambda n: (0, 0))],
        out_shape=[dq_shape,
                   jax.ShapeDtypeStruct((t + BLOCK, KV_WIDTH), F32),
                   jax.ShapeDtypeStruct((t + BLOCK, KV_WIDTH), F32),
                   jax.ShapeDtypeStruct((N_Q_HEADS, BLOCK, 2 * BLOCK), F32),
                   jax.ShapeDtypeStruct((8, LANE), F32)],
        sem=("arbitrary",), args=args, jobs=jobs, aliases=aliases)
    return (res, jres) if jobs else res


def _bucket_table():
    qi = np.arange(BLOCK)[:, None]
    ki = np.arange(2 * BLOCK)[None, :]
    rel = np.maximum(qi + BLOCK - ki, 0)
    max_exact = N_BUCKETS // 2
    relf = np.maximum(rel, 1).astype(np.float32)
    large = max_exact + (np.log(relf / max_exact) / math.log(MAX_DISTANCE / max_exact)
                         * (N_BUCKETS - max_exact)).astype(np.int32)
    large = np.minimum(large, N_BUCKETS - 1)
    return np.where(rel < max_exact, rel, large).astype(np.int32)


def _rel_bias_fwd(rel_bias, onehot, name):
    n = onehot.shape[1]
    tn = _tile(n, 8192, LANE)

    def body(r_ref, o_ref, out_ref):
        out_ref[...] = lax.dot_general(r_ref[...], o_ref[...], (_DIMS["tn"], ((), ())),
                                       precision=lax.Precision.HIGHEST, preferred_element_type=F32)

    return pl.pallas_call(
        body, name=name, grid=(n // tn,),
        in_specs=[pl.BlockSpec(rel_bias.shape, lambda j: (0, 0)), pl.BlockSpec((N_BUCKETS, tn), lambda j: (0, j))],
        out_specs=pl.BlockSpec((N_Q_HEADS, tn), lambda j: (0, j)),
        out_shape=jax.ShapeDtypeStruct((N_Q_HEADS, n), F32),
        compiler_params=_params(("parallel",)),
    )(rel_bias, onehot)


def _rel_bias_bwd(dbias, onehot, name):
    def body(d_ref, o_ref, out_ref):
        out_ref[...] = lax.dot_general(d_ref[...], o_ref[...], (_DIMS["nt"], ((), ())),
                                       precision=lax.Precision.HIGHEST, preferred_element_type=F32)

    full = pl.BlockSpec(dbias.shape, lambda: (0, 0))
    return pl.pallas_call(
        body, name=name, in_specs=[full, full],
        out_specs=pl.BlockSpec((N_Q_HEADS, N_BUCKETS), lambda: (0, 0)),
        out_shape=jax.ShapeDtypeStruct((N_Q_HEADS, N_BUCKETS), F32),
        compiler_params=_params(()),
    )(dbias, onehot)


def _part_specs(gparts, part_index, tr, cc):
    blk = pl.BlockSpec((tr, cc), lambda i, *_: (i, 0))
    specs = []
    for i, _ in enumerate(gparts):
        if i in part_index:
            ix = part_index[i]
            if isinstance(ix, int):
                specs.append(pl.BlockSpec((None, tr, cc), functools.partial(lambda i, *_, ix: (ix, i, 0), ix=ix)))
            else:
                specs.append(pl.BlockSpec((None, tr, cc), lambda i, ix_ref: (ix_ref[0], i, 0)))
        else:
            specs.append(blk)
    return blk, specs


def _sum_parts(gparts, part_index, name):
    r, cc = gparts[0].shape[-2:]
    tr = _tile(r, 256)
    dyn = [i for i, ix in part_index.items() if not isinstance(ix, int)]
    assert len(dyn) <= 1

    def body(*refs):
        refs = refs[len(dyn):]
        g = refs[0][...].astype(F32)
        for gr in refs[1:-1]:
            g = g + gr[...].astype(F32)
        refs[-1][...] = g

    blk, specs = _part_specs(gparts, part_index, tr, cc)
    grid_spec = pltpu.PrefetchScalarGridSpec(num_scalar_prefetch=len(dyn), grid=(r // tr,), in_specs=specs, out_specs=blk)
    return pl.pallas_call(
        body, name=name, grid_spec=grid_spec, out_shape=jax.ShapeDtypeStruct((r, cc), F32),
        compiler_params=_params(("parallel",)),
    )(*[part_index[i] for i in dyn], *gparts)


def _adamw(w, m, v, gparts, name, part_index=None, jobs=()):
    r, cc = w.shape
    tr = _tile(r, 128)
    np_ = len(gparts)
    part_index = part_index or {}
    dyn = [i for i, ix in part_index.items() if not isinstance(ix, int)]
    assert len(dyn) <= 1
    bc1 = 1.0 / (1.0 - ADAM_B1 ** ADAM_STEP)
    bc2 = 1.0 / (1.0 - ADAM_B2 ** ADAM_STEP)

    def body(*refs):
        refs = refs[len(dyn):]
        w_ref, m_ref, v_ref = refs[:3]
        g_refs = refs[3:3 + np_]
        g_out, d_out, m_out, v_out = refs[3 + np_:]
        g = g_refs[0][...].astype(F32)
        for gr in g_refs[1:]:
            g = g + gr[...].astype(F32)
        mn = ADAM_B1 * m_ref[...] + (1.0 - ADAM_B1) * g
        vn = ADAM_B2 * v_ref[...] + (1.0 - ADAM_B2) * (g * g)
        g_out[...] = g
        m_out[...] = mn
        v_out[...] = vn
        d_out[...] = -ADAM_LR * ((mn * bc1) / (jnp.sqrt(vn * bc2) + ADAM_EPS) + ADAM_WD * w_ref[...])

    blk, g_specs = _part_specs(gparts, part_index, tr, cc)
    if jobs:
        assert not dyn
        return _call(body, name=name, grid=(r // tr,), in_specs=[blk, blk, blk] + g_specs, out_specs=[blk] * 4,
                     out_shape=[jax.ShapeDtypeStruct((r, cc), F32)] * 4, sem=("arbitrary",),
                     args=[w, m, v, *gparts], jobs=jobs)
    grid_spec = pltpu.PrefetchScalarGridSpec(
        num_scalar_prefetch=len(dyn), grid=(r // tr,),
        in_specs=[blk, blk, blk] + g_specs, out_specs=[blk] * 4)
    return pl.pallas_call(
        body, name=name, grid_spec=grid_spec,
        out_shape=[jax.ShapeDtypeStruct((r, cc), F32)] * 4,
        compiler_params=_params(("parallel",)),
    )(*[part_index[i] for i in dyn], w, m, v, *gparts)


def _row(vec):
    return vec.reshape(1, -1)


def kernel(x, c, w_ada, b_ada, norm1_g, w_in, conv_w, conv_b, lru_wa, lru_ba, lru_wx, lru_bx, lru_lambda, w_lru_out, w_attn_out, attn_sinks, rel_bias, w_out, norm2_g, w_ff1, w_ff2, final_g, loss_target, m_w_ada, m_b_ada, m_norm1_g, m_w_in, m_conv_w, m_conv_b, m_lru_wa, m_lru_ba, m_lru_wx, m_lru_bx, m_lru_lambda, m_w_lru_out, m_w_attn_out, m_attn_sinks, m_rel_bias, m_w_out, m_norm2_g, m_w_ff1, m_w_ff2, m_final_g, v_w_ada, v_b_ada, v_norm1_g, v_w_in, v_conv_w, v_conv_b, v_lru_wa, v_lru_ba, v_lru_wx, v_lru_bx, v_lru_lambda, v_w_lru_out, v_w_attn_out, v_attn_sinks, v_rel_bias, v_w_out, v_norm2_g, v_w_ff1, v_w_ff2, v_final_g):
    dd = D_MODEL
    t = x.shape[1]
    ix, iy, ic = lax.axis_index("x"), lax.axis_index("y"), lax.axis_index("c")
    me = 4 * ix + 2 * iy + ic
    chip = 2 * ix + iy
    c_idx = jnp.reshape(ic, (1,)).astype(jnp.int32)
    chip_idx = jnp.reshape(chip, (1,)).astype(jnp.int32)

    xs = x[0]
    target = loss_target[0]
    ncol_ada = w_ada.shape[2]
    cshard = conv_w.shape[2]

    shards = [jnp.transpose(w_in[0]).astype(BF16), w_lru_out[0].astype(BF16), w_attn_out[0].astype(BF16),
              w_out[0].astype(BF16), jnp.transpose(w_ff1[0]).astype(BF16), w_ff2[0].astype(BF16)]
    s_in, s_lo, s_ao, s_o, s_f1, s_f2 = shards

    def full(g):
        return g.reshape(N_DEV * g.shape[1], dd)

    def stack(g):
        return g.reshape(N_DEV, g.shape[0] // N_DEV, dd)

    (w_in_pair,) = _run_job(_pair_job([s_in]), "ag_w_in_pair")
    w_in_pair = lax.dynamic_update_slice_in_dim(w_in_pair, s_in[None], ic, axis=0)

    pack0 = jnp.zeros((8, dd), F32).at[0:1, :].set(c).at[1:1 + CONV_WIDTH, 0:cshard].set(conv_w[0])
    (g0,) = _all_gather([pack0], "ag_cond")
    c_all = g0[:, 0, :]
    conv_w_full = jnp.transpose(g0[:, 1:1 + CONV_WIDTH, 0:cshard], (1, 0, 2)).reshape(CONV_WIDTH, dd)
    b_cols = lax.dynamic_slice(b_ada, (0, me * ncol_ada), (1, ncol_ada))
    mod_cols = _ada_fwd(c_all, w_ada[0], b_cols, "ada_fwd")
    (g1,) = _all_gather([mod_cols], "ag_mod")
    mod_mine = lax.dynamic_index_in_dim(g1, me, axis=1, keepdims=False).reshape(6, dd)
    mod = jnp.concatenate([mod_mine, jnp.zeros((2, dd), F32)], axis=0)

    bucket = _bucket_table()
    onehot = jnp.asarray((bucket.reshape(1, -1) == np.arange(N_BUCKETS)[:, None]).astype(np.float32))
    bias = _rel_bias_fwd(rel_bias, onehot, "rel_bias_fwd").reshape(N_Q_HEADS, BLOCK, 2 * BLOCK)
    sinks = attn_sinks.reshape(N_Q_HEADS)
    lp = jnp.concatenate([conv_b, lru_ba, lru_bx, lru_lambda, conv_w_full], axis=0)
    wa, wx = lru_wa[0], lru_wx[0]

    tm = _tile(t, 1024)
    tn = 512
    h = _norm_mod(xs, norm1_g, mod, 0, "norm1")
    tn2 = 1024
    chunk_w = 2 * s_in.shape[0]

    def ids(*ks):
        return jnp.stack([jnp.asarray(k, jnp.int32) for k in ks])

    tn_in = _tile(chunk_w, 1024, LANE)
    (proj,), ((g_near,),) = _mm(
        h, w_in_pair.reshape(chunk_w, dd), "nt", tm=tm, tn=tn_in, outs=[BF16], name="in_proj_own",
        chunks=(ids(chip), chunk_w, ids(0)), n_total=IN_WIDTH, jobs=[_gather_job([w_in_pair], (0.7,), "near")])
    near_ids = ids(2 * (1 - ix) + iy, 2 * ix + 1 - iy)
    (proj,), ((g_far,), (g_o,)) = _mm(
        h, full(g_near), "nt", tm=tm, tn=tn_in, outs=[BF16], name="in_proj_near",
        chunks=(near_ids, chunk_w, near_ids), into=proj,
        jobs=[_gather_job([g_near], (0.6,), "far"), _gather_job([s_o], (0.6, 0.85))])
    far_chip = 2 * (1 - ix) + 1 - iy
    (proj,), ((g_lo,),) = _mm(
        h, full(g_far), "nt", tm=tm, tn=tn_in, outs=[BF16], name="in_proj_far",
        chunks=(ids(far_chip), chunk_w, ids(far_chip)), into=proj, jobs=[_gather_job([s_lo], (0.5, 0.8))])
    far_slots = lax.dynamic_slice_in_dim(g_far, 2 * far_chip, 2, axis=0)
    win_t = lax.dynamic_update_slice_in_dim(g_near, far_slots, 2 * far_chip, axis=0)
    win_t = full(lax.dynamic_update_slice_in_dim(win_t, w_in_pair, 2 * chip, axis=0))
    (g_o,) = _fill_own([g_o], [s_o])
    (att, lse), ((g_f1,),) = _attn_fwd(proj, bias, sinks, "attn_fwd", jobs=[_gather_job([s_f1], (0.6, 0.9))])
    (rec, ya), ((g_ao,),) = _lru_fwd(proj, lp, wa, wx, "lru_fwd", jobs=[_gather_job([s_ao], (0.4, 0.7))])
    g_lo, g_ao, g_f1 = _fill_own([g_lo, g_ao, g_f1], [s_lo, s_ao, s_f1])
    wlo, wao, wo, wf1_t = full(g_lo), full(g_ao), full(g_o), full(g_f1)
    (y_a,) = _mm(ya, wlo, "nn", tm=tm, tn=tn2, outs=[BF16], name="lru_out")

    def merge_epi(acc, ya_t, ga_t, gb_t):
        return acc, _sigmoid(ga_t.astype(F32)) * ya_t.astype(F32) + _sigmoid(gb_t.astype(F32)) * acc

    y_b, merged = _mm(att, wao, "nn", tm=tm, tn=tn, outs=[BF16, BF16], epi=merge_epi,
                      tiles=[(y_a, 0), (proj, OFF_GA // tn), (proj, OFF_GB // tn)], name="attn_out_merge")

    def resid_epi(row):
        def epi(acc, x_t, mod_t):
            return acc, x_t + mod_t[row:row + 1, :] * acc
        return epi

    o1, x1 = _mm(merged, wo, "nn", tm=tm, tn=tn2, outs=[BF16, F32], epi=resid_epi(2),
                 tiles=[(xs, 0)], rows=[mod], name="out_proj")
    h2 = _norm_mod(x1, norm2_g, mod, 3, "norm2")

    def relu2_epi(acc):
        rl = jnp.maximum(acc, 0.0)
        return acc, rl * rl

    (f_pre, ff), ((g_f2,),) = _mm(h2, wf1_t, "nt", tm=tm, tn=1024, outs=[BF16, BF16], epi=relu2_epi, name="ff1",
                                  jobs=[_gather_job([s_f2], (0.6, 0.9))])
    wf2 = full(_fill_own([g_f2], [s_f2])[0])
    tmh = _tile(t, 512)
    o2, x2 = _mm(ff, wf2, "nn", tm=tmh, tn=tn, outs=[BF16, F32], epi=resid_epi(5),
                 tiles=[(x1, 0)], rows=[mod], name="ff2")

    loss_blk, dx2, do2, st_f = _final_loss(x2, o2, _row(final_g), target, mod, "final_loss")
    loss = lax.psum(loss_blk[0, 0], ("x", "y", "c"))

    def drelu2_epi(acc, f_t):
        return (acc * (2.0 * jnp.maximum(f_t.astype(F32), 0.0)),)

    tmw, tnw, tkw = 512, dd, _tile(t, 2048)
    (df,) = _mm(do2, wf2, "nt", tm=tm, tn=1024, outs=[BF16], epi=drelu2_epi, tiles=[(f_pre, 0)], name="ff2_dx")
    (dwf2,) = _mm(ff, do2, "tn", tm=tmw, tn=tnw, tk=tkw, outs=[BF16], name="ff2_dw")
    (dh2,), ((sib_f2,),) = _mm(df, wf1_t, "nn", tm=tmh, tn=tn, outs=[BF16], name="ff1_dx",
                              jobs=[_sibling_job([stack(dwf2)])])
    pair_f2 = _pair_add(stack(dwf2), sib_f2, c_idx, "rs_pair_add_w_ff2")
    (dwf1_t,), ((chips_f2,),) = _mm(df, h2, "tn", tm=tmw, tn=tnw, tk=tkw, outs=[BF16], name="ff1_dw",
                                    jobs=[_chips_job([pair_f2])])
    (dx1, st_2, do1), ((sib_f1,),) = _norm_mod_bwd(x1, norm2_g, mod, 3, dh2, dx2, (o1, 2), "norm2_bwd",
                                                   jobs=[_sibling_job([stack(dwf1_t)])])
    pair_f1 = _pair_add(stack(dwf1_t), sib_f1, c_idx, "rs_pair_add_w_ff1")

    def dmerge_epi(acc, ga_t, gb_t, ya_t, yb_t):
        sa, sb = _sigmoid(ga_t.astype(F32)), _sigmoid(gb_t.astype(F32))
        return (acc * sa, acc * sb, acc * ya_t.astype(F32) * sa * (1.0 - sa), acc * yb_t.astype(F32) * sb * (1.0 - sb))

    tmh = _tile(t, 512)
    dy_a, dy_b, dproj, dgb = _mm(do1, wo, "nt", tm=tm, tn=tn, outs=[BF16] * 4, epi=dmerge_epi,
                                 tiles=[(proj, OFF_GA // tn), (proj, OFF_GB // tn), (y_a, 0), (y_b, 0)],
                                 wide={2: (IN_WIDTH, OFF_GA // tn)}, name="out_proj_dx")
    (dwo,) = _mm(merged, do1, "tn", tm=tmw, tn=tnw, tk=tkw, outs=[BF16], name="out_proj_dw")
    (datt,) = _mm(dy_b, wao, "nt", tm=tm, tn=tn2, outs=[BF16], name="attn_out_dx")
    (dwao,) = _mm(att, dy_b, "tn", tm=tmw, tn=tnw, tk=tkw, outs=[BF16], name="attn_out_dw")
    (dya,) = _mm(dy_a, wlo, "nt", tm=tm, tn=tn2, outs=[BF16], name="lru_out_dx")
    (dwlo,) = _mm(ya, dy_a, "tn", tm=tmw, tn=tnw, tk=tkw, outs=[BF16], name="lru_out_dw")
    (dproj, dk_pad, dv_pad, dbias, dsinks), (sib_3, (chips_f1,)) = _attn_bwd(
        proj, datt, lse, bias, sinks, "attn_bwd", into=dproj,
        jobs=[_sibling_job([stack(dwlo), stack(dwao), stack(dwo)]), _chips_job([pair_f1])])
    pair_lo, pair_ao, pair_o = [_pair_add(stack(g_), s_, c_idx, "rs_pair_add_" + nm_)
                                for g_, s_, nm_ in zip((dwlo, dwao, dwo), sib_3, ("w_lru_out", "w_attn_out", "w_out"))]
    (dproj, st_l, dwa, dwx), ((chips_lo, chips_ao, chips_o),) = _lru_bwd(
        proj, rec, dya, lp, wa, wx, "lru_bwd", into=dproj, jobs=[_chips_job([pair_lo, pair_ao, pair_o])])
    dkv = jnp.concatenate([dk_pad[BLOCK:].astype(BF16), dv_pad[BLOCK:].astype(BF16)], axis=1)
    dproj = lax.dynamic_update_slice_in_dim(dproj, dkv, OFF_K, axis=1)
    dproj = lax.dynamic_update_slice_in_dim(dproj, dgb, OFF_GB, axis=1)
    lru_mats = jnp.concatenate([dwa.reshape(-1, dd), dwx.reshape(-1, dd)], axis=0)
    (dwin_t,), ((mats_all,),) = _mm(dproj, h, "tn", tm=768, tn=tnw, tk=tkw, outs=[BF16], name="in_proj_dw",
                                   jobs=[_gather_job([lru_mats], (0.3, 0.6))])
    (mats_all,) = _fill_own([mats_all], [lru_mats])
    (sib_in,) = _exchange_sibling([stack(dwin_t)], "rs_sibling_w_in")
    pair_in = _pair_add(stack(dwin_t), sib_in, c_idx, "rs_pair_add_w_in", widths=(dd // 2, dd // 2))
    n_rb = t // tm
    dx_tiles = dict(tm=tm, tn=tn2, tk=IN_WIDTH // 4, outs=[BF16])
    if n_rb >= 2:
        (dh,), ((chips_in_0,),) = _mm(dproj, win_t, "nn", name="in_proj_dx_top", rows_part=(0, n_rb // 2),
                                     jobs=[_chips_job(pair_in[0:1])], **dx_tiles)
        (dh,), ((chips_in_1,),) = _mm(dproj, win_t, "nn", name="in_proj_dx_bottom", into=dh,
                                     rows_part=(n_rb // 2, n_rb - n_rb // 2), jobs=[_chips_job(pair_in[1:2])], **dx_tiles)
    else:
        (dh,), ((chips_in_0,), (chips_in_1,)) = _mm(
            dproj, win_t, "nn", name="in_proj_dx", jobs=[_chips_job(pair_in[0:1]), _chips_job(pair_in[1:2])], **dx_tiles)
    grad_x, st_1 = _norm_mod_bwd(xs, norm1_g, mod, 0, dh, dx1, None, "norm1_bwd")
    chips_in = [chips_in_0, chips_in_1]

    drel = jnp.transpose(_rel_bias_bwd(dbias.reshape(N_Q_HEADS, -1), onehot, "rel_bias_bwd"))

    misc = jnp.concatenate([drel.reshape(1, -1), dsinks[0:1, 0:N_Q_HEADS],
                            jnp.zeros((1, dd - N_BUCKETS * N_Q_HEADS - N_Q_HEADS), F32)], axis=1)
    zero_row = jnp.zeros((1, dd), F32)
    small = jnp.concatenate([
        st_1[0:2], st_2[3:4], st_2[0:2], st_f[1:2], zero_row, zero_row,
        st_1[2:3], st_l[0:1], st_l[1:2], st_l[2:3], st_l[3:4], st_2[2:3], st_f[0:1], misc,
        st_l[4:8], jnp.zeros((4, dd), F32)], axis=0)
    own_and_chips = {0: chip_idx, 1: 0, 2: 1, 3: 2}
    g_in = jnp.transpose(jnp.concatenate(
        [_sum_parts([p_, c_, c_, c_], own_and_chips, "sum_w_in_%d" % q_)
         for q_, (p_, c_) in enumerate(zip(pair_in, chips_in))], axis=1))
    w_in_res, ((small_all,),) = _adamw(w_in[0], m_w_in[0], v_w_in[0], [g_in], "adamw_w_in",
                                       jobs=[_gather_job([small], (0.3, 0.6))])
    (small_all,) = _fill_own([small_all], [small])

    def pack_small(b_, n1, cb, ba, bx, lam, n2, fg, rb, sk):
        misc_ = jnp.concatenate([rb.reshape(1, -1), sk.reshape(1, -1),
                                 jnp.zeros((1, dd - N_BUCKETS * N_Q_HEADS - N_Q_HEADS), F32)], axis=1)
        return jnp.concatenate([b_.reshape(6, dd), jnp.zeros((2, dd), F32), n1, cb, ba, bx, lam, n2, _row(fg), misc_,
                                jnp.zeros((8, dd), F32)], axis=0)

    def pack_mats(wa_, wx_):
        return jnp.concatenate([wa_.reshape(-1, dd), wx_.reshape(-1, dd)], axis=0)

    every = {i: i for i in range(N_DEV)}
    w_s = pack_small(b_ada, norm1_g, conv_b, lru_ba, lru_bx, lru_lambda, norm2_g, final_g, rel_bias, attn_sinks)
    m_s = pack_small(m_b_ada, m_norm1_g, m_conv_b, m_lru_ba, m_lru_bx, m_lru_lambda, m_norm2_g, m_final_g, m_rel_bias, m_attn_sinks)
    v_s = pack_small(v_b_ada, v_norm1_g, v_conv_b, v_lru_ba, v_lru_bx, v_lru_lambda, v_norm2_g, v_final_g, v_rel_bias, v_attn_sinks)
    small_res = _adamw(w_s, m_s, v_s, [small_all] * N_DEV, "adamw_small", part_index=every)
    mats_res = _adamw(pack_mats(lru_wa, lru_wx), pack_mats(m_lru_wa, m_lru_wx), pack_mats(v_lru_wa, v_lru_wx),
                      [mats_all] * N_DEV, "adamw_lru_mats", part_index=every)

    def unpack_small(s, mt):
        nb_ = N_BUCKETS * N_Q_HEADS
        half = mt.shape[0] // 2
        return dict(
            b_ada=s[0:6].reshape(1, 6 * dd), norm1_g=s[8:9], conv_b=s[9:10], lru_ba=s[10:11], lru_bx=s[11:12],
            lru_lambda=s[12:13], norm2_g=s[13:14], final_g=s[14], rel_bias=s[15, 0:nb_].reshape(N_BUCKETS, N_Q_HEADS),
            attn_sinks=s[15:16, nb_:nb_ + N_Q_HEADS],
            lru_wa=mt[:half].reshape(1, LRU_BLOCKS, LRU_BLOCK_W, LRU_BLOCK_W),
            lru_wx=mt[half:].reshape(1, LRU_BLOCKS, LRU_BLOCK_W, LRU_BLOCK_W))

    res = {k: [None] * 4 for k in ("w_ada", "w_in", "conv_w", "w_lru_out", "w_attn_out", "w_out", "w_ff1", "w_ff2")}
    for q_, (s_, mt_) in enumerate(zip(small_res, mats_res)):
        for k_, val in unpack_small(s_, mt_).items():
            res.setdefault(k_, [None] * 4)[q_] = val

    g_conv = lax.dynamic_slice(small_res[0][16:16 + CONV_WIDTH], (0, me * cshard), (CONV_WIDTH, cshard))
    conv_res = _adamw(conv_w[0], m_conv_w[0], v_conv_w[0], [g_conv], "adamw_conv")
    res["conv_w"] = [r_[None] for r_ in conv_res]

    dmod_all = small_all[:, 0:6, :].reshape(N_DEV, 6 * dd)
    dmod_cols = lax.dynamic_slice(dmod_all, (0, me * ncol_ada), (N_DEV, ncol_ada))
    g_ada = _ada_bwd(jnp.transpose(c_all), dmod_cols, "ada_bwd")
    ada_res = _adamw(w_ada[0], m_w_ada[0], v_w_ada[0], [g_ada], "adamw_ada")
    res["w_ada"] = [r_[None] for r_ in ada_res]

    pair = [None, pair_lo, pair_ao, pair_o, pair_f1, pair_f2]
    from_chips = [None, chips_lo, chips_ao, chips_o, chips_f1, chips_f2]

    def summed(i):
        return [pair[i], from_chips[i], from_chips[i], from_chips[i]], own_and_chips

    def sum_only(i, name):
        parts, index = summed(i)
        return _sum_parts(parts, index, name)

    res["w_in"] = [r_[None] for r_ in w_in_res]
    g_ff1 = jnp.transpose(sum_only(4, "sum_w_ff1"))
    res["w_ff1"] = [r_[None] for r_ in _adamw(w_ff1[0], m_w_ff1[0], v_w_ff1[0], [g_ff1], "adamw_w_ff1")]
    for i, (nm, w_, m_, v_) in {1: ("w_lru_out", w_lru_out, m_w_lru_out, v_w_lru_out),
                                2: ("w_attn_out", w_attn_out, m_w_attn_out, v_w_attn_out),
                                3: ("w_out", w_out, m_w_out, v_w_out),
                                5: ("w_ff2", w_ff2, m_w_ff2, v_w_ff2)}.items():
        parts, index = summed(i)
        res[nm] = [r_[None] for r_ in _adamw(w_[0], m_[0], v_[0], parts, "adamw_" + nm, part_index=index)]

    order = ["w_ada", "b_ada", "norm1_g", "w_in", "conv_w", "conv_b", "lru_wa", "lru_ba", "lru_wx", "lru_bx",
             "lru_lambda", "w_lru_out", "w_attn_out", "attn_sinks", "rel_bias", "w_out", "norm2_g", "w_ff1",
             "w_ff2", "final_g"]
    out = [loss, grad_x[None]]
    for q_ in range(4):
        out += [res[k_][q_] for k_ in order]
    return tuple(out)
```

```python
import functools
import math

import numpy as np
import jax
import jax.numpy as jnp
from jax import lax
from jax.experimental import pallas as pl
from jax.experimental.pallas import tpu as pltpu

F32 = jnp.float32
BF16 = jnp.bfloat16
MESH = pl.DeviceIdType.MESH

D_MODEL = 2048
N_Q_HEADS = 32
N_KV_HEADS = 4
GROUP = N_Q_HEADS // N_KV_HEADS
HEAD_DIM = 64
KV_WIDTH = N_KV_HEADS * HEAD_DIM
BLOCK = 128
NEG_INF = -1e30
N_BUCKETS = 32
MAX_DISTANCE = 128
LRU_BLOCKS = 16
LRU_BLOCK_W = 128
CONV_WIDTH = 4
LRU_C = 8.0
D_FF = 4 * D_MODEL
EPS = 1e-6
IN_WIDTH = 5 * D_MODEL + 2 * KV_WIDTH
OFF_LX, OFF_LG, OFF_Q, OFF_K, OFF_V, OFF_GA, OFF_GB = 0, 2048, 4096, 6144, 6400, 6656, 8704

ADAM_LR, ADAM_B1, ADAM_B2, ADAM_EPS, ADAM_WD, ADAM_STEP = 0.001, 0.9, 0.999, 1e-08, 0.01, 10

N_DEV = 8
VMEM_LIMIT_BYTES = 48 * 1024 * 1024
LANE = 128
LRU_CW = 512


def _params(sem, **kw):
    return pltpu.CompilerParams(dimension_semantics=sem, vmem_limit_bytes=VMEM_LIMIT_BYTES, **kw)


def _tile(n, pref, mult=8):
    if n <= pref:
        return n
    t = (pref // mult) * mult
    while t >= mult:
        if n % t == 0:
            return t
        t -= mult
    return n


def _sigmoid(x):
    return 0.5 * jnp.tanh(0.5 * x) + 0.5


def _gelu_tanh(x):
    k = math.sqrt(2.0 / math.pi)
    t = jnp.tanh(k * (x + 0.044715 * x * x * x))
    return 0.5 * x * (1.0 + t), t


def _gelu_tanh_grad(x, t):
    k = math.sqrt(2.0 / math.pi)
    return 0.5 * (1.0 + t) + 0.5 * x * (1.0 - t * t) * k * (1.0 + 3.0 * 0.044715 * x * x)


def _log1p(e):
    u = 1.0 + e
    return jnp.where(u == 1.0, e, jnp.log(u) * (e / jnp.where(u == 1.0, 1.0, u - 1.0)))


def _softplus(x):
    return jnp.maximum(x, 0.0) + _log1p(jnp.exp(-jnp.abs(x)))


def _neg_expm1(x, exp_x):
    series = -x * (1.0 + x * (0.5 + x * (1.0 / 6.0 + x * (1.0 / 24.0))))
    return jnp.where(x > -0.0625, series, 1.0 - exp_x)


def _my_place():
    return lax.axis_index("x"), lax.axis_index("y"), lax.axis_index("c")


def _all_gather(arrs, name):
    return _fill_own(_run_job(_gather_job(arrs), name), arrs)


def _fill_own(stacks, shards):
    me = 4 * lax.axis_index("x") + 2 * lax.axis_index("y") + lax.axis_index("c")
    out = []
    for g, s in zip(stacks, shards):
        if g.size * g.dtype.itemsize <= (1 << 20):
            slot = lax.broadcasted_iota(jnp.int32, g.shape, 0)
            out.append(jnp.where(slot == me, s[None], g))
        else:
            out.append(lax.dynamic_update_slice_in_dim(g, s[None], me, axis=0))
    return out


class _Job:
    def __init__(self, ins, outs, sems, start, finish, forwards=(), forward_at=()):
        self.ins, self.outs, self.sems = list(ins), list(outs), list(sems)
        self.start, self.finish, self.forwards, self.forward_at = start, finish, list(forwards), list(forward_at)
        assert len(self.forwards) <= len(self.forward_at)


def _gather_job(arrs, forward_at=(0.6, 0.85), part="all"):
    n = len(arrs)
    whole, near, far = part == "all", part == "near", part == "far"

    def copies(ins, outs, sems):
        send_sems, recv_sems = sems
        x, y, c = _my_place()
        sib = (x, y, 1 - c)
        xn, yn, dg = (1 - x, y), (x, 1 - y), (1 - x, 1 - y)

        def slot(px, py, pc):
            return 4 * px + 2 * py + pc

        def cp(a, k, src, dst_slot, to):
            return pltpu.make_async_remote_copy(
                src_ref=src, dst_ref=outs[a].at[dst_slot], send_sem=send_sems.at[a, k], recv_sem=recv_sems.at[a, k],
                device_id=to, device_id_type=MESH)

        def arrival(a, k, dst_slot):
            return cp(a, k, outs[a].at[dst_slot], dst_slot, sib)

        me_slot = slot(x, y, c)
        sends, relay, passes, arrive = {}, {}, {}, {}
        for a in range(n):
            if whole:
                own = ins[a]
                sends[a, 0] = cp(a, 0, own, me_slot, sib)
            if near:
                own = ins[a].at[c]
            if whole or near:
                sends[a, 1] = cp(a, 1, own, me_slot, (*xn, c))
                sends[a, 2] = cp(a, 2, own, me_slot, (*yn, c))
                passes[a, 4] = cp(a, 4, outs[a].at[slot(*xn, c)], slot(*xn, c), sib)
                passes[a, 5] = cp(a, 5, outs[a].at[slot(*yn, c)], slot(*yn, c), sib)
            if whole or far:
                source = outs[a] if whole else ins[a]
                relayed = c * slot(*yn, c) + (1 - c) * slot(*xn, c)
                relay[a, 3] = cp(a, 3, source.at[relayed], relayed, (x + c * (1 - 2 * x), y + (1 - c) * (1 - 2 * y), c))
                passes[a, 6] = cp(a, 6, outs[a].at[slot(*dg, c)], slot(*dg, c), sib)
            arrive[a, 0] = arrival(a, 0, slot(x, y, 1 - c))
            arrive[a, 1] = arrival(a, 1, slot(*xn, c))
            arrive[a, 2] = arrival(a, 2, slot(*yn, c))
            arrive[a, 3] = arrival(a, 3, slot(*dg, c))
            arrive[a, 4] = arrival(a, 4, slot(*xn, 1 - c))
            arrive[a, 5] = arrival(a, 5, slot(*yn, 1 - c))
            arrive[a, 6] = arrival(a, 6, slot(*dg, 1 - c))
        return sends, relay, passes, arrive

    def pick(d, ks):
        return [d[a, k] for a in range(n) for k in ks if (a, k) in d]

    def start(ins, outs, sems):
        sends, relay, _, _ = copies(ins, outs, sems)
        for cp in pick(sends, (0, 1, 2)) + (pick(relay, (3,)) if far else []):
            cp.start()

    def forward_neighbours(ins, outs, sems):
        _, relay, passes, arrive = copies(ins, outs, sems)
        for cp in pick(arrive, (1, 2)):
            cp.wait_recv()
        for cp in pick(relay, (3,)) + pick(passes, (4, 5)):
            cp.start()

    def forward_diagonal(ins, outs, sems):
        _, _, passes, arrive = copies(ins, outs, sems)
        for cp in pick(arrive, (3,)):
            cp.wait_recv()
        for cp in pick(passes, (6,)):
            cp.start()

    def finish(ins, outs, sems):
        sends, relay, passes, arrive = copies(ins, outs, sems)
        for cp in pick(arrive, (6,) if far else (0, 4, 5, 6) if whole else (4, 5)):
            cp.wait_recv()
        for cp in pick(sends, (0, 1, 2)) + pick(relay, (3,)) + pick(passes, (4, 5, 6)):
            cp.wait_send()

    shapes = [a.shape[-2:] for a in arrs]
    forwards = [forward_diagonal] if far else [forward_neighbours] if near else [forward_neighbours, forward_diagonal]
    return _Job(arrs, [jax.ShapeDtypeStruct((N_DEV,) + s, a.dtype) for s, a in zip(shapes, arrs)],
                [pltpu.SemaphoreType.DMA((n, 7)), pltpu.SemaphoreType.DMA((n, 7))],
                start, finish, forwards, forward_at)


def _pair_job(arrs):
    n = len(arrs)

    def copies(ins, outs, sems):
        send_sems, recv_sems = sems
        x, y, c = _my_place()
        remote = [pltpu.make_async_remote_copy(
            src_ref=ins[a], dst_ref=outs[a].at[c], send_sem=send_sems.at[a], recv_sem=recv_sems.at[a],
            device_id=(x, y, 1 - c), device_id_type=MESH) for a in range(n)]
        arrive = [pltpu.make_async_remote_copy(
            src_ref=ins[a], dst_ref=outs[a].at[1 - c], send_sem=send_sems.at[a], recv_sem=recv_sems.at[a],
            device_id=(x, y, 1 - c), device_id_type=MESH) for a in range(n)]
        return remote, arrive

    def start(ins, outs, sems):
        for cp in copies(ins, outs, sems)[0]:
            cp.start()

    def finish(ins, outs, sems):
        remote, arrive = copies(ins, outs, sems)
        for cp in arrive:
            cp.wait_recv()
        for cp in remote:
            cp.wait_send()

    return _Job(arrs, [jax.ShapeDtypeStruct((2,) + a.shape, a.dtype) for a in arrs],
                [pltpu.SemaphoreType.DMA((n,)), pltpu.SemaphoreType.DMA((n,))], start, finish)


def _run_job(job, name):
    ni, no = len(job.ins), len(job.outs)

    def body(*refs):
        ins, outs, sems = refs[:ni], refs[ni:ni + no], refs[ni + no:]
        job.start(ins, outs, sems)
        for fwd in job.forwards:
            fwd(ins, outs, sems)
        job.finish(ins, outs, sems)

    any_spec = pl.BlockSpec(memory_space=pl.ANY)
    return pl.pallas_call(
        body, name=name, out_shape=job.outs, in_specs=[any_spec] * ni, out_specs=[any_spec] * no,
        scratch_shapes=job.sems,
    )(*job.ins)


def _call(body, *, name, grid, in_specs, out_specs, out_shape, scratch_shapes=(), sem, args, jobs=(),
          prefetch=(), aliases=None):
    in_specs, out_specs, out_shape, scratch_shapes = list(in_specs), list(out_specs), list(out_shape), list(scratch_shapes)
    prefetch = list(prefetch)
    if not jobs and not prefetch and not aliases:
        res = pl.pallas_call(body, name=name, grid=grid, in_specs=in_specs, out_specs=out_specs, out_shape=out_shape,
                             scratch_shapes=scratch_shapes, compiler_params=_params(sem))(*args)
        return list(res), []
    n_in, n_out, n_scr = len(in_specs), len(out_specs), len(scratch_shapes)
    j_in = [len(j.ins) for j in jobs]
    j_out = [len(j.outs) for j in jobs]
    j_sem = [len(j.sems) for j in jobs]
    total = int(np.prod(grid))
    any_spec = pl.BlockSpec(memory_space=pl.ANY)

    def wrapped(*refs):
        pos = [0]

        def take(k):
            part = refs[pos[0]:pos[0] + k]
            pos[0] += k
            return part

        take(len(prefetch))
        ins = take(n_in)
        jins = [take(k) for k in j_in]
        outs = take(n_out)
        jouts = [take(k) for k in j_out]
        scr = take(n_scr)
        jsems = [take(k) for k in j_sem]
        step = pl.program_id(0)
        for d in range(1, len(grid)):
            step = step * grid[d] + pl.program_id(d)
        for j, job in enumerate(jobs):
            pl.when(step == 0)(functools.partial(job.start, jins[j], jouts[j], jsems[j]))
            for fwd, frac in zip(job.forwards, job.forward_at):
                at = min(int(total * frac), total - 1)
                pl.when(step == at)(functools.partial(fwd, jins[j], jouts[j], jsems[j]))
        body(*ins, *outs, *scr)
        for j, job in enumerate(jobs):
            pl.when(step == total - 1)(functools.partial(job.finish, jins[j], jouts[j], jsems[j]))

    grid_spec = pltpu.PrefetchScalarGridSpec(
        num_scalar_prefetch=len(prefetch), grid=grid,
        in_specs=in_specs + [any_spec] * sum(j_in),
        out_specs=out_specs + [any_spec] * sum(j_out),
        scratch_shapes=scratch_shapes + [s for j in jobs for s in j.sems])
    res = pl.pallas_call(
        wrapped, name=name, grid_spec=grid_spec,
        out_shape=out_shape + [o for j in jobs for o in j.outs],
        input_output_aliases={len(prefetch) + i: o for i, o in (aliases or {}).items()},
        compiler_params=_params(("arbitrary",) * len(grid)),
    )(*prefetch, *args, *[a for j in jobs for a in j.ins])
    res = list(res)
    own, rest = res[:n_out], res[n_out:]
    per_job = []
    for k in j_out:
        per_job.append(rest[:k])
        rest = rest[k:]
    return own, per_job


def _exchange_sibling(arrs, name):
    return _run_job(_sibling_job(arrs), name)


def _sibling_job(arrs):
    n = len(arrs)

    def copies(ins, outs, sems):
        send_sems, recv_sems = sems
        x, y, c = _my_place()
        return [pltpu.make_async_remote_copy(
            src_ref=ins[a].at[2 * k + 1 - c], dst_ref=outs[a].at[k],
            send_sem=send_sems.at[a, k], recv_sem=recv_sems.at[a, k],
            device_id=(x, y, 1 - c), device_id_type=MESH) for a in range(n) for k in range(4)]

    def start(ins, outs, sems):
        for cp in copies(ins, outs, sems):
            cp.start()

    def finish(ins, outs, sems):
        for cp in copies(ins, outs, sems):
            cp.wait()

    return _Job(arrs, [jax.ShapeDtypeStruct((4,) + a.shape[1:], a.dtype) for a in arrs],
                [pltpu.SemaphoreType.DMA((n, 4)), pltpu.SemaphoreType.DMA((n, 4))], start, finish)


def _chips_job(arrs):
    n = len(arrs)

    def copies(ins, outs, sems):
        send_sems, recv_sems = sems
        x, y, c = _my_place()
        chips = [(1 - x, y), (x, 1 - y), (1 - x, 1 - y)]
        return [pltpu.make_async_remote_copy(
            src_ref=ins[a].at[2 * px + py], dst_ref=outs[a].at[j],
            send_sem=send_sems.at[a, j], recv_sem=recv_sems.at[a, j],
            device_id=(px, py, c), device_id_type=MESH) for a in range(n) for j, (px, py) in enumerate(chips)]

    def start(ins, outs, sems):
        for cp in copies(ins, outs, sems):
            cp.start()

    def finish(ins, outs, sems):
        for cp in copies(ins, outs, sems):
            cp.wait()

    return _Job(arrs, [jax.ShapeDtypeStruct((3,) + a.shape[1:], a.dtype) for a in arrs],
                [pltpu.SemaphoreType.DMA((n, 3)), pltpu.SemaphoreType.DMA((n, 3))], start, finish)


def _pair_add(stack, from_sibling, c_idx, name, widths=None):
    _, r, cc = stack.shape
    tr = _tile(r, 512)
    ws = [cc] if widths is None else list(widths)
    assert sum(ws) == cc
    starts = [sum(ws[:q]) for q in range(len(ws))]

    def body(c_ref, a_ref, b_ref, *o_refs):
        res = a_ref[...].astype(F32) + b_ref[...].astype(F32)
        for o_ref, s0, w_ in zip(o_refs, starts, ws):
            o_ref[...] = res[:, s0:s0 + w_].astype(o_ref.dtype)

    grid_spec = pltpu.PrefetchScalarGridSpec(
        num_scalar_prefetch=1, grid=(4, r // tr),
        in_specs=[pl.BlockSpec((None, tr, cc), lambda k, i, c_ref: (2 * k + c_ref[0], i, 0)),
                  pl.BlockSpec((None, tr, cc), lambda k, i, c_ref: (k, i, 0))],
        out_specs=[pl.BlockSpec((None, tr, w_), lambda k, i, c_ref: (k, i, 0)) for w_ in ws])
    res = pl.pallas_call(
        body, name=name, grid_spec=grid_spec,
        out_shape=[jax.ShapeDtypeStruct((4, r, w_), BF16) for w_ in ws],
        compiler_params=_params(("parallel", "parallel")),
    )(c_idx, stack, from_sibling)
    return res[0] if widths is None else list(res)


_DIMS = {"nn": ((1,), (0,)), "nt": ((1,), (1,)), "tn": ((0,), (0,))}


def _mm(a, b, mode, *, tm, tn, tk=None, outs, epi=None, tiles=(), rows=(), name, jobs=(), chunks=None, into=None,
        n_total=None, rows_part=None, wide=None):
    wide = wide or {}
    if mode == "tn":
        kk, m = a.shape
    else:
        m, kk = a.shape
    n_all = b.shape[0] if mode == "nt" else b.shape[1]
    nn = n_all if chunks is None else chunks[0].shape[0] * chunks[1]
    n_all = n_all if n_total is None else n_total
    tk = kk if tk is None else tk
    assert m % tm == 0 and nn % tn == 0 and kk % tk == 0, (name, a.shape, b.shape, tm, tn, tk)
    nk = kk // tk
    nt, nr, no = len(tiles), len(rows), len(outs)
    n_into = 0 if into is None else 1

    def body(*refs):
        a_ref, b_ref = refs[:2]
        tile_refs = refs[2:2 + nt]
        row_refs = refs[2 + nt:2 + nt + nr]
        out_refs = refs[2 + nt + nr + n_into:2 + nt + nr + n_into + no]
        part = lax.dot_general(a_ref[...], b_ref[...], (_DIMS[mode], ((), ())), preferred_element_type=F32)

        def finish(acc):
            if epi is None:
                res = (acc,)
            else:
                res = epi(acc, *[t[...] for t in tile_refs], *[r[...] for r in row_refs])
            for o_ref, val in zip(out_refs, res):
                o_ref[...] = val.astype(o_ref.dtype)

        if nk == 1:
            finish(part)
        else:
            acc_ref = refs[-1]
            k = pl.program_id(2)

            @pl.when(k == 0)
            def _():
                acc_ref[...] = part

            @pl.when(k > 0)
            def _():
                acc_ref[...] += part

            @pl.when(k == nk - 1)
            def _():
                finish(acc_ref[...])

    if chunks is None:
        col = b_row = lambda j, *_: j
        prefetch = []
    else:
        assert mode == "nt" and not tiles and not rows and chunks[1] % tn == 0
        per = chunks[1] // tn
        col = lambda j, ids, b_ids: ids[j // per] * per + j % per
        b_row = lambda j, ids, b_ids: b_ids[j // per] * per + j % per
        prefetch = [chunks[0], chunks[2]]
    row0, n_row_blocks = (0, m // tm) if rows_part is None else rows_part
    assert rows_part is None or (mode != "tn" and not tiles)
    if mode == "tn":
        a_spec = pl.BlockSpec((tk, tm), lambda i, j, k, *_: (k, i))
    else:
        a_spec = pl.BlockSpec((tm, tk), lambda i, j, k, *_: (i + row0, k))
    if mode == "nt":
        b_spec = pl.BlockSpec((tn, tk), lambda i, j, k, *s: (b_row(j, *s), k))
    else:
        b_spec = pl.BlockSpec((tk, tn), lambda i, j, k, *_: (k, j))
    tile_specs = [pl.BlockSpec((tm, tn), functools.partial(lambda i, j, k, off: (i, j + off), off=off))
                  for _, off in tiles]
    row_specs = [pl.BlockSpec((r.shape[0], tn), lambda i, j, k: (0, j)) for r in rows]
    out_spec = pl.BlockSpec((tm, tn), lambda i, j, k, *s: (i + row0, col(j, *s)))
    args = [a, b, *[t for t, _ in tiles], *rows]
    in_specs = [a_spec, b_spec] + tile_specs + row_specs
    aliases = None
    if into is not None:
        aliases = {len(args): 0}
        args.append(into)
        in_specs.append(pl.BlockSpec(memory_space=pl.ANY))
    res, jres = _call(
        body, name=name, grid=(n_row_blocks, nn // tn, nk),
        in_specs=in_specs,
        out_specs=[out_spec if q not in wide else
                   pl.BlockSpec((tm, tn), functools.partial(lambda i, j, k, off: (i, j + off), off=wide[q][1]))
                   for q in range(no)],
        out_shape=[jax.ShapeDtypeStruct((m, wide[q][0] if q in wide else n_all), dt) for q, dt in enumerate(outs)],
        scratch_shapes=[pltpu.VMEM((tm, tn), F32)] if nk > 1 else [],
        sem=("parallel", "parallel", "arbitrary"),
        args=args, jobs=jobs, prefetch=prefetch, aliases=aliases)
    return (res, jres) if jobs else res


def _ada_fwd(c_all, w_ada, b_ada_cols, name):
    nb, dd = c_all.shape
    ncol = w_ada.shape[1]
    tn = _tile(ncol, 512, LANE)

    def body(c_ref, w_ref, b_ref, o_ref):
        cv = c_ref[...]
        act = (cv * _sigmoid(cv)).astype(BF16)
        o_ref[...] = jnp.dot(act, w_ref[...].astype(BF16), preferred_element_type=F32) + b_ref[...]

    return pl.pallas_call(
        body, name=name, grid=(ncol // tn,),
        in_specs=[pl.BlockSpec((nb, dd), lambda j: (0, 0)), pl.BlockSpec((dd, tn), lambda j: (0, j)),
                  pl.BlockSpec((1, tn), lambda j: (0, j))],
        out_specs=pl.BlockSpec((nb, tn), lambda j: (0, j)),
        out_shape=jax.ShapeDtypeStruct((nb, ncol), F32),
        compiler_params=_params(("parallel",)),
    )(c_all, w_ada, b_ada_cols)


def _ada_bwd(c_all_t, dmod_cols, name):
    dd, nb = c_all_t.shape
    ncol = dmod_cols.shape[1]
    tr = _tile(dd, 256)

    def body(c_ref, d_ref, o_ref):
        cv = c_ref[...]
        act = (cv * _sigmoid(cv)).astype(BF16).astype(F32)
        dm = d_ref[...].astype(BF16).astype(F32)
        acc = act[:, 0:1] * dm[0:1, :]
        for bi in range(1, nb):
            acc = acc + act[:, bi:bi + 1] * dm[bi:bi + 1, :]
        o_ref[...] = acc

    return pl.pallas_call(
        body, name=name, grid=(dd // tr,),
        in_specs=[pl.BlockSpec((tr, nb), lambda i: (i, 0)), pl.BlockSpec((nb, ncol), lambda i: (0, 0))],
        out_specs=pl.BlockSpec((tr, ncol), lambda i: (i, 0)),
        out_shape=jax.ShapeDtypeStruct((dd, ncol), F32),
        compiler_params=_params(("parallel",)),
    )(c_all_t, dmod_cols)


def _norm_mod(x, gain, mod, row0, name):
    t, dd = x.shape
    tt = _tile(t, 512)

    def body(x_ref, g_ref, m_ref, h_ref):
        xv = x_ref[...]
        r = lax.rsqrt(jnp.mean(xv * xv, axis=-1, keepdims=True) + EPS)
        shift, scale = m_ref[row0:row0 + 1, :], m_ref[row0 + 1:row0 + 2, :]
        h_ref[...] = ((xv * r * g_ref[...]) * (1.0 + scale) + shift).astype(BF16)

    return pl.pallas_call(
        body, name=name, grid=(t // tt,),
        in_specs=[pl.BlockSpec((tt, dd), lambda i: (i, 0)), pl.BlockSpec((1, dd), lambda i: (0, 0)),
                  pl.BlockSpec((8, dd), lambda i: (0, 0))],
        out_specs=pl.BlockSpec((tt, dd), lambda i: (i, 0)),
        out_shape=jax.ShapeDtypeStruct((t, dd), BF16),
        compiler_params=_params(("parallel",)),
    )(x, gain, mod)


def _norm_mod_bwd(x, gain, mod, row0, dh, dres, branch, name, jobs=()):
    t, dd = x.shape
    tt = _tile(t, 512)
    has_branch = branch is not None

    def body(*refs):
        if has_branch:
            x_ref, g_ref, m_ref, dh_ref, dres_ref, o_ref, dx_ref, st_ref, do_ref = refs[:-1]
        else:
            x_ref, g_ref, m_ref, dh_ref, dres_ref, dx_ref, st_ref = refs[:-1]

        acc_ref = refs[-1]
        step_i = pl.program_id(0)

        @pl.when(step_i == 0)
        def _():
            acc_ref[...] = jnp.zeros_like(acc_ref)

        def fold(v):
            out = v[0:8]
            for q in range(1, rc // 8):
                out = out + v[8 * q:8 * q + 8]
            return out

        def chunk(ci, carry):
            rows = pl.ds(pl.multiple_of(ci * rc, rc), rc)
            xv = x_ref[rows, :]
            g = g_ref[...]
            r = lax.rsqrt(jnp.mean(xv * xv, axis=-1, keepdims=True) + EPS)
            nv = xv * r
            dhv = dh_ref[rows, :].astype(F32)
            acc_ref[0] += fold(dhv)
            acc_ref[1] += fold(dhv * (nv * g))
            dng = dhv * (1.0 + m_ref[row0 + 1:row0 + 2, :])
            acc_ref[2] += fold(dng * nv)
            dn = dng * g
            dx = dres_ref[rows, :] + r * (dn - nv * jnp.mean(dn * nv, axis=-1, keepdims=True))
            dx_ref[rows, :] = dx
            if has_branch:
                acc_ref[3] += fold(dx * o_ref[rows, :].astype(F32))
                do_ref[rows, :] = (dx * m_ref[branch[1]:branch[1] + 1, :]).astype(BF16)
            return carry

        lax.fori_loop(0, tt // rc, chunk, 0, unroll=4)

        @pl.when(step_i == nsteps - 1)
        def _():
            st_ref[...] = jnp.zeros_like(st_ref)
            for k in range(4):
                st_ref[k:k + 1, :] = jnp.sum(acc_ref[k], axis=0, keepdims=True)

    rc = 16 if tt % 16 == 0 else 8
    nsteps = t // tt
    row = pl.BlockSpec((tt, dd), lambda i: (i, 0))
    in_specs = [row, pl.BlockSpec((1, dd), lambda i: (0, 0)), pl.BlockSpec((8, dd), lambda i: (0, 0)), row, row]
    args = [x, gain, mod, dh, dres]
    out_specs = [row, pl.BlockSpec((8, dd), lambda i: (0, 0))]
    out_shape = [jax.ShapeDtypeStruct((t, dd), F32), jax.ShapeDtypeStruct((8, dd), F32)]
    if has_branch:
        in_specs.append(row)
        args.append(branch[0])
        out_specs.append(row)
        out_shape.append(jax.ShapeDtypeStruct((t, dd), BF16))
    res, jres = _call(body, name=name, grid=(t // tt,), in_specs=in_specs, out_specs=out_specs, out_shape=out_shape,
                      scratch_shapes=[pltpu.VMEM((4, 8, dd), F32)], sem=("arbitrary",), args=args, jobs=jobs)
    return (res, jres) if jobs else res


def _final_loss(x2, o2, final_g, target, mod, name):
    t, dd = x2.shape
    tt = _tile(t, 256)
    nsteps = t // tt

    def body(x_ref, o_ref, g_ref, t_ref, m_ref, loss_ref, dx_ref, do_ref, st_ref, lacc):
        i = pl.program_id(0)

        @pl.when(i == 0)
        def _():
            st_ref[...] = jnp.zeros_like(st_ref)
            lacc[...] = jnp.zeros_like(lacc)

        xv = x_ref[...]
        g = g_ref[...]
        r = lax.rsqrt(jnp.mean(xv * xv, axis=-1, keepdims=True) + EPS)
        nv = xv * r
        err = nv * g - t_ref[...]
        lacc[...] += jnp.sum(err * err, axis=0, keepdims=True)
        dy = err * (1.0 / dd)
        st_ref[0:1, :] += jnp.sum(dy * nv, axis=0, keepdims=True)
        dn = dy * g
        dx = r * (dn - nv * jnp.mean(dn * nv, axis=-1, keepdims=True))
        dx_ref[...] = dx
        st_ref[1:2, :] += jnp.sum(dx * o_ref[...].astype(F32), axis=0, keepdims=True)
        do_ref[...] = (dx * m_ref[5:6, :]).astype(BF16)

        @pl.when(i == nsteps - 1)
        def _():
            total = jnp.sum(lacc[...], axis=-1, keepdims=True) * (0.5 / dd)
            loss_ref[...] = jnp.broadcast_to(total, loss_ref.shape)

    row = pl.BlockSpec((tt, dd), lambda i: (i, 0))
    return pl.pallas_call(
        body, name=name, grid=(nsteps,),
        in_specs=[row, row, pl.BlockSpec((1, dd), lambda i: (0, 0)), row, pl.BlockSpec((8, dd), lambda i: (0, 0))],
        out_specs=[pl.BlockSpec((8, LANE), lambda i: (0, 0)), row, row, pl.BlockSpec((8, dd), lambda i: (0, 0))],
        out_shape=[jax.ShapeDtypeStruct((8, LANE), F32), jax.ShapeDtypeStruct((t, dd), F32),
                   jax.ShapeDtypeStruct((t, dd), BF16), jax.ShapeDtypeStruct((8, dd), F32)],
        scratch_shapes=[pltpu.VMEM((1, dd), F32)],
        compiler_params=_params(("arbitrary",)),
    )(x2, o2, final_g, target, mod)


def _lru_gates(xt, prev8, lp, wa_ref, wx_ref, first_tile, cw):
    tt = xt.shape[0]
    ext = jnp.concatenate([prev8, xt], axis=0)
    xs = [pltpu.roll(ext, s, 0)[8:, :] if s else xt for s in range(CONV_WIDTH)]
    xc = lp[0:1, :] + lp[7:8, :] * xs[0] + lp[6:7, :] * xs[1] + lp[5:6, :] * xs[2] + lp[4:5, :] * xs[3]
    xcb = xc.astype(BF16)
    za, zi = [], []
    for jb in range(cw // LRU_BLOCK_W):
        sl = slice(jb * LRU_BLOCK_W, (jb + 1) * LRU_BLOCK_W)
        za.append(jnp.dot(xcb[:, sl], wa_ref[jb].astype(BF16), preferred_element_type=F32))
        zi.append(jnp.dot(xcb[:, sl], wx_ref[jb].astype(BF16), preferred_element_type=F32))
    ra = _sigmoid(jnp.concatenate(za, axis=1) + lp[1:2, :])
    ri = _sigmoid(jnp.concatenate(zi, axis=1) + lp[2:3, :])
    sp = _softplus(-lp[3:4, :])
    log_a = -LRU_C * ra * sp
    av = jnp.exp(log_a)
    seq_start = jnp.logical_and(first_tile, lax.broadcasted_iota(jnp.int32, (tt, 1), 0) == 0)
    m2 = _neg_expm1(2.0 * log_a, av * av)
    inv_mult = jnp.where(seq_start, 1.0, lax.rsqrt(jnp.maximum(m2, 1e-30)))
    mult = jnp.where(seq_start, 1.0, m2 * inv_mult)
    return xs, xc, xcb, ra, ri, sp, av, mult, inv_mult, seq_start


def _scan_groups(a, u, reverse):
    tt, cw = a.shape
    a3, u3 = a.reshape(tt // 8, 8, cw), u.reshape(tt // 8, 8, cw)
    r = lax.broadcasted_iota(jnp.int32, (1, 8, 1), 1)
    for s in (1, 2, 4):
        shift = 8 - s if reverse else s
        take = r < 8 - s if reverse else r >= s
        u3 = jnp.where(take, a3 * pltpu.roll(u3, shift, 1) + u3, u3)
        a3 = jnp.where(take, a3 * pltpu.roll(a3, shift, 1), a3)
    return a3, u3


def _scan_carry(a_s, u_s, h_s, carry, reverse):
    ng = a_s.shape[0]

    def step(i, h):
        g = ng - 1 - i if reverse else i
        hg = u_s[g] + a_s[g] * h
        h_s[g] = hg
        return hg[0:1, :] if reverse else hg[7:8, :]

    return lax.fori_loop(0, ng, step, carry, unroll=4)


def _lru_fwd(proj, lp, wa, wx, name, jobs=()):
    t = proj.shape[0]
    w = D_MODEL
    cw = LRU_CW
    tt = _tile(t, 256)
    nb = cw // LRU_BLOCK_W

    def body(lx_ref, lg_ref, lp_ref, wa_ref, wx_ref, rec_ref, ya_ref, a_s, u_s, h_s, halo, carry):
        ti = pl.program_id(1)

        @pl.when(ti == 0)
        def _():
            halo[...] = jnp.zeros_like(halo)
            carry[...] = jnp.zeros_like(carry)

        xt = lx_ref[...].astype(F32)
        lp_v = lp_ref[...]
        _, xc, _, _, ri, _, av, mult, _, _ = _lru_gates(xt, halo[...], lp_v, wa_ref, wx_ref, ti == 0, cw)
        halo[...] = xt[tt - 8:, :]
        a_s[...], u_s[...] = _scan_groups(av, mult * (ri * xc), False)
        carry[...] = _scan_carry(a_s, u_s, h_s, carry[...], False)
        rec = h_s[...].reshape(tt, cw)
        rec_ref[...] = rec.astype(BF16)
        gelu, _ = _gelu_tanh(lg_ref[...].astype(F32))
        ya_ref[...] = (rec * gelu).astype(BF16)

    off_g = OFF_LG // cw
    res, jres = _call(
        body, name=name, grid=(w // cw, t // tt),
        in_specs=[pl.BlockSpec((tt, cw), lambda ci, ti: (ti, ci)),
                  pl.BlockSpec((tt, cw), lambda ci, ti: (ti, off_g + ci)),
                  pl.BlockSpec((8, cw), lambda ci, ti: (0, ci)),
                  pl.BlockSpec((nb, LRU_BLOCK_W, LRU_BLOCK_W), lambda ci, ti: (ci, 0, 0)),
                  pl.BlockSpec((nb, LRU_BLOCK_W, LRU_BLOCK_W), lambda ci, ti: (ci, 0, 0))],
        out_specs=[pl.BlockSpec((tt, cw), lambda ci, ti: (ti, ci))] * 2,
        out_shape=[jax.ShapeDtypeStruct((t, w), BF16)] * 2,
        scratch_shapes=[pltpu.VMEM((tt // 8, 8, cw), F32)] * 3 + [pltpu.VMEM((8, cw), F32), pltpu.VMEM((1, cw), F32)],
        sem=("parallel", "arbitrary"), args=[proj, proj, lp, wa, wx], jobs=jobs)
    return (res, jres) if jobs else res


def _lru_bwd(proj, rec, dya, lp, wa, wx, name, jobs=(), into=None):
    t = proj.shape[0]
    w = D_MODEL
    cw = LRU_CW if into is None else w
    tt = _tile(t, 256 if into is None else 128)
    nt = t // tt
    nb = cw // LRU_BLOCK_W
    r8 = tt // 8

    def body(lx_ref, lxp_ref, lg_ref, rec_ref, recp_ref, dya_ref, lp_ref, wa_ref, wx_ref, *rest):
        if into is None:
            dlx_ref, dlg_ref = rest[0], rest[1]
        else:
            dlx_ref, dlg_ref = rest[1].at[:, 0:cw], rest[1].at[:, cw:2 * cw]
        st_ref, dwa_ref, dwx_ref, a_s, d_s, dh_s, nhalo, carry = rest[-8:]
        step_i = pl.program_id(1)
        ti = nt - 1 - step_i

        @pl.when(step_i == 0)
        def _():
            st_ref[...] = jnp.zeros_like(st_ref)
            dwa_ref[...] = jnp.zeros_like(dwa_ref)
            dwx_ref[...] = jnp.zeros_like(dwx_ref)
            nhalo[...] = jnp.zeros_like(nhalo)
            carry[...] = jnp.zeros_like(carry)

        first = ti == 0
        keep = jnp.where(first, 0.0, 1.0)
        xt = lx_ref[...].astype(F32)
        prev8 = lxp_ref[...].astype(F32) * keep
        lp_v = lp_ref[...]
        xs, xc, xcb, ra, ri, sp, av, mult, inv_mult, seq_start = _lru_gates(xt, prev8, lp_v, wa_ref, wx_ref, first, cw)

        lg = lg_ref[...].astype(F32)
        gelu, th = _gelu_tanh(lg)
        dyav = dya_ref[...].astype(F32)
        recv = rec_ref[...].astype(F32)
        dlg_ref[...] = (dyav * recv * _gelu_tanh_grad(lg, th)).astype(BF16)

        drec = dyav * gelu
        e_in = carry[...]
        a_s[...], d_s[...] = _scan_groups(av, av * drec, True)
        carry[...] = _scan_carry(a_s, d_s, dh_s, e_in, True)
        e_next = jnp.concatenate([dh_s[...].reshape(tt, cw), jnp.broadcast_to(e_in, (8, cw))], axis=0)
        dh = drec + pltpu.roll(e_next, tt + 7, 0)[:tt, :]
        rprev8 = recp_ref[...].astype(F32) * keep
        hprev = pltpu.roll(jnp.concatenate([rprev8, recv], axis=0), 1, 0)[8:, :]
        da = dh * hprev
        dmult = jnp.where(seq_start, 0.0, dh * ri * xc)
        dri = dh * mult * xc
        dxc = dh * mult * ri
        dlog_a = da * av - dmult * (av * av) * inv_mult
        dra = dlog_a * (-LRU_C * sp)
        lam = lp_v[3:4, :]
        st_ref[3:4, :] += jnp.sum(dlog_a * ra, axis=0, keepdims=True) * (LRU_C * _sigmoid(-lam))
        dza = dra * ra * (1.0 - ra)
        dzi = dri * ri * (1.0 - ri)
        st_ref[1:2, :] += jnp.sum(dza, axis=0, keepdims=True)
        st_ref[2:3, :] += jnp.sum(dzi, axis=0, keepdims=True)
        dzab, dzib = dza.astype(BF16), dzi.astype(BF16)
        back = []
        for jb in range(nb):
            sl = slice(jb * LRU_BLOCK_W, (jb + 1) * LRU_BLOCK_W)
            dwa_ref[jb] += lax.dot_general(xcb[:, sl], dzab[:, sl], (_DIMS["tn"], ((), ())), preferred_element_type=F32)
            dwx_ref[jb] += lax.dot_general(xcb[:, sl], dzib[:, sl], (_DIMS["tn"], ((), ())), preferred_element_type=F32)
            back.append(
                lax.dot_general(dzab[:, sl], wa_ref[jb].astype(BF16), (_DIMS["nt"], ((), ())), preferred_element_type=F32)
                + lax.dot_general(dzib[:, sl], wx_ref[jb].astype(BF16), (_DIMS["nt"], ((), ())), preferred_element_type=F32))
        dxc = dxc + jnp.concatenate(back, axis=1)
        st_ref[0:1, :] += jnp.sum(dxc, axis=0, keepdims=True)
        for k in range(CONV_WIDTH):
            st_ref[4 + k:5 + k, :] += jnp.sum(dxc * xs[CONV_WIDTH - 1 - k], axis=0, keepdims=True)
        ext = jnp.concatenate([dxc, nhalo[...]], axis=0)
        dlx = lp_v[7:8, :] * dxc
        for s in range(1, CONV_WIDTH):
            dlx = dlx + lp_v[7 - s:8 - s, :] * pltpu.roll(ext, tt + 8 - s, 0)[:tt, :]
        dlx_ref[...] = dlx.astype(BF16)
        nhalo[...] = dxc[0:8, :]

    off_g = OFF_LG // cw
    tile = lambda off: pl.BlockSpec((tt, cw), lambda ci, si: (nt - 1 - si, off + ci))
    prev = lambda off: pl.BlockSpec((8, cw), lambda ci, si: (jnp.maximum((nt - 1 - si) * r8 - 1, 0), off + ci))
    wspec = pl.BlockSpec((nb, LRU_BLOCK_W, LRU_BLOCK_W), lambda ci, si: (ci, 0, 0))
    st_spec = pl.BlockSpec((8, cw), lambda ci, si: (0, ci))
    in_specs = [tile(0), prev(0), tile(off_g), tile(0), prev(0), tile(0), st_spec, wspec, wspec]
    args = [proj, proj, proj, rec, rec, dya, lp, wa, wx]
    if into is None:
        d_specs = [tile(0), tile(0)]
        d_shapes = [jax.ShapeDtypeStruct((t, w), BF16), jax.ShapeDtypeStruct((t, w), BF16)]
        aliases = None
    else:
        d_specs = [pl.BlockSpec((tt, 2 * cw), lambda ci, si: (nt - 1 - si, 0))]
        d_shapes = [jax.ShapeDtypeStruct(into.shape, BF16)]
        aliases = {len(args): 0}
        args.append(into)
        in_specs.append(pl.BlockSpec(memory_space=pl.ANY))
    res, jres = _call(
        body, name=name, grid=(w // cw, nt), in_specs=in_specs,
        out_specs=d_specs + [st_spec, wspec, wspec],
        out_shape=d_shapes + [jax.ShapeDtypeStruct((8, w), F32),
                              jax.ShapeDtypeStruct((LRU_BLOCKS, LRU_BLOCK_W, LRU_BLOCK_W), F32),
                              jax.ShapeDtypeStruct((LRU_BLOCKS, LRU_BLOCK_W, LRU_BLOCK_W), F32)],
        scratch_shapes=[pltpu.VMEM((tt // 8, 8, cw), F32)] * 3 + [pltpu.VMEM((8, cw), F32), pltpu.VMEM((1, cw), F32)],
        sem=("parallel", "arbitrary"), args=args, jobs=jobs, aliases=aliases)
    return (res, jres) if jobs else res


def _band_valid(first_block):
    qi = lax.broadcasted_iota(jnp.int32, (BLOCK, 2 * BLOCK), 0)
    ki = lax.broadcasted_iota(jnp.int32, (BLOCK, 2 * BLOCK), 1)
    rel = qi + BLOCK - ki
    valid = jnp.logical_and(rel >= 0, rel < BLOCK)
    return jnp.logical_and(valid, jnp.logical_or(ki >= BLOCK, jnp.logical_not(first_block)))


ATTN_STACK_BWD = 8


def _low_half():
    return lax.broadcasted_iota(jnp.int32, (1, LANE), 1) < HEAD_DIM


def _stack_heads(x, h0, ns):
    low = _low_half()
    parts = []
    for g in range(ns):
        h = h0 + g
        slab = x[:, (h // 2) * LANE:(h // 2 + 1) * LANE]
        parts.append(jnp.where(low if h % 2 == 0 else jnp.logical_not(low), slab, jnp.zeros_like(slab)))
    return jnp.concatenate(parts, axis=0)


def _unstack_heads(y, ns):
    low = _low_half()
    pairs = [jnp.where(low, y[(2 * j) * BLOCK:(2 * j + 1) * BLOCK], y[(2 * j + 1) * BLOCK:(2 * j + 2) * BLOCK])
             for j in range(ns // 2)]
    return pairs[0] if len(pairs) == 1 else jnp.concatenate(pairs, axis=1)


def _dup_kv(xall, kv):
    part = xall[:, kv * HEAD_DIM:(kv + 1) * HEAD_DIM]
    return jnp.concatenate([part, part], axis=1)


def _fold_halves(a):
    return a + pltpu.roll(a, HEAD_DIM, 1)


def _group_probs(qg, k2, bias_g, sink_g, lse_g, valid):
    s = lax.dot_general(qg, k2, (_DIMS["nt"], ((), ())), preferred_element_type=F32) * (HEAD_DIM ** -0.5)
    s = jnp.where(valid[None], s.reshape(bias_g.shape) + bias_g, NEG_INF)
    return jnp.exp(s - lse_g), jnp.exp(sink_g - lse_g)


def _group_sinks(s_ref, h0, ns):
    return jnp.concatenate([jnp.full((1, BLOCK, 1), s_ref[h0 + g], F32) for g in range(ns)], axis=0)


def _attn_specs(t):
    nblk = t // BLOCK
    q_spec = pl.BlockSpec((BLOCK, D_MODEL), lambda n: (n, OFF_Q // D_MODEL))
    kc = pl.BlockSpec((BLOCK, KV_WIDTH), lambda n: (n, OFF_K // KV_WIDTH))
    kp = pl.BlockSpec((BLOCK, KV_WIDTH), lambda n: (jnp.maximum(n - 1, 0), OFF_K // KV_WIDTH))
    vc = pl.BlockSpec((BLOCK, KV_WIDTH), lambda n: (n, OFF_V // KV_WIDTH))
    vp = pl.BlockSpec((BLOCK, KV_WIDTH), lambda n: (jnp.maximum(n - 1, 0), OFF_V // KV_WIDTH))
    bias_spec = pl.BlockSpec((N_Q_HEADS, BLOCK, 2 * BLOCK), lambda n: (0, 0, 0))
    sink_spec = pl.BlockSpec(memory_space=pltpu.SMEM)
    return nblk, [q_spec, kp, kc, vp, vc, bias_spec, sink_spec]


def _attn_fwd(proj, bias, sinks, name, jobs=()):
    t = proj.shape[0]
    nblk, in_specs = _attn_specs(t)

    def body(q_ref, kp_ref, kc_ref, vp_ref, vc_ref, b_ref, s_ref, o_ref, lse_ref):
        valid = _band_valid(pl.program_id(0) == 0)
        lane = lax.broadcasted_iota(jnp.int32, (1, LANE), 1)
        q = q_ref[...]
        kall = jnp.concatenate([kp_ref[...], kc_ref[...]], axis=0)
        vall = jnp.concatenate([vp_ref[...], vc_ref[...]], axis=0)
        outs = []
        lse = jnp.zeros((BLOCK, LANE), F32)
        for h in range(N_Q_HEADS):
            kv = h // GROUP
            kk = kall[:, kv * HEAD_DIM:(kv + 1) * HEAD_DIM]
            vv = vall[:, kv * HEAD_DIM:(kv + 1) * HEAD_DIM]
            s = lax.dot_general(q[:, h * HEAD_DIM:(h + 1) * HEAD_DIM], kk, (_DIMS["nt"], ((), ())),
                                preferred_element_type=F32) * (HEAD_DIM ** -0.5)
            s = jnp.where(valid, s + b_ref[h], NEG_INF)
            sink = s_ref[h]
            m = jnp.maximum(jnp.max(s, axis=-1, keepdims=True), sink)
            e = jnp.exp(s - m)
            den = jnp.sum(e, axis=-1, keepdims=True) + jnp.exp(sink - m)
            lse = jnp.where(lane == h, m + jnp.log(den), lse)
            outs.append(jnp.dot((e * (1.0 / den)).astype(BF16), vv, preferred_element_type=F32))
        o_ref[...] = jnp.concatenate(outs, axis=1).astype(BF16)
        lse_ref[...] = lse

    res, jres = _call(
        body, name=name, grid=(nblk,), in_specs=in_specs,
        out_specs=[pl.BlockSpec((BLOCK, D_MODEL), lambda n: (n, 0)), pl.BlockSpec((BLOCK, LANE), lambda n: (n, 0))],
        out_shape=[jax.ShapeDtypeStruct((t, D_MODEL), BF16), jax.ShapeDtypeStruct((t, LANE), F32)],
        sem=("parallel",), args=[proj, proj, proj, proj, proj, bias, sinks], jobs=jobs)
    return (res, jres) if jobs else res


def _attn_bwd(proj, datt, lse, bias, sinks, name, jobs=(), into=None):
    t = proj.shape[0]
    nblk, in_specs = _attn_specs(t)
    in_specs = in_specs + [pl.BlockSpec((BLOCK, D_MODEL), lambda n: (n, 0)), pl.BlockSpec((BLOCK, LANE), lambda n: (n, 0))]
    scale = HEAD_DIM ** -0.5
    args = [proj, proj, proj, proj, proj, bias, sinks, datt, lse]
    dq_spec = pl.BlockSpec((BLOCK, D_MODEL), lambda n: (n, 0))
    dq_shape = jax.ShapeDtypeStruct((t, D_MODEL), BF16)
    aliases = None
    if into is not None:
        aliases = {len(args): 0}
        args.append(into)
        in_specs.append(pl.BlockSpec(memory_space=pl.ANY))
        dq_spec = pl.BlockSpec((BLOCK, D_MODEL), lambda n: (n, OFF_Q // D_MODEL))
        dq_shape = jax.ShapeDtypeStruct(into.shape, BF16)

    def body(q_ref, kp_ref, kc_ref, vp_ref, vc_ref, b_ref, s_ref, do_ref, lse_ref, *rest):
        dq_ref, dk_ref, dv_ref, db_ref, ds_ref = rest[-5:]
        n = pl.program_id(0)

        @pl.when(n == 0)
        def _():
            dk_ref[...] = jnp.zeros_like(dk_ref)
            dv_ref[...] = jnp.zeros_like(dv_ref)
            db_ref[...] = jnp.zeros_like(db_ref)
            ds_ref[...] = jnp.zeros_like(ds_ref)

        valid = _band_valid(n == 0)
        lane = lax.broadcasted_iota(jnp.int32, (1, LANE), 1)
        q = q_ref[...]
        do = do_ref[...]
        lse_all = lse_ref[...]
        kall = jnp.concatenate([kp_ref[...], kc_ref[...]], axis=0)
        vall = jnp.concatenate([vp_ref[...], vc_ref[...]], axis=0)
        low = _low_half()
        dqs, dks, dvs = [], [], []
        dsink_row = jnp.zeros((1, LANE), F32)
        ns = ATTN_STACK_BWD
        for kv in range(N_KV_HEADS):
            k2, v2 = _dup_kv(kall, kv), _dup_kv(vall, kv)
            dk_acc = jnp.zeros((2 * BLOCK, LANE), F32)
            dv_acc = jnp.zeros((2 * BLOCK, LANE), F32)
            for h0 in range(kv * GROUP, (kv + 1) * GROUP, ns):
                qg, dog = _stack_heads(q, h0, ns), _stack_heads(do, h0, ns)
                lse_g = jnp.concatenate([lse_all[:, h0 + g:h0 + g + 1][None] for g in range(ns)], axis=0)
                p, psink = _group_probs(qg, k2, b_ref[h0:h0 + ns], _group_sinks(s_ref, h0, ns), lse_g, valid)
                dp = lax.dot_general(dog, v2, (_DIMS["nt"], ((), ())), preferred_element_type=F32)
                dp = dp.reshape(ns, BLOCK, 2 * BLOCK)
                delta = jnp.sum(p * dp, axis=-1, keepdims=True)
                ds = p * (dp - delta)
                db_ref[h0:h0 + ns] += ds
                dsink = -jnp.sum(psink * delta, axis=1, keepdims=True)
                for g in range(ns):
                    dsink_row = dsink_row + jnp.where(lane == h0 + g, dsink[g], 0.0)
                dsb = ds.reshape(ns * BLOCK, 2 * BLOCK).astype(BF16)
                pb = p.reshape(ns * BLOCK, 2 * BLOCK).astype(BF16)
                dqs.append(_unstack_heads(jnp.dot(dsb, k2, preferred_element_type=F32) * scale, ns))
                dk_acc = dk_acc + lax.dot_general(dsb, qg, (_DIMS["tn"], ((), ())), preferred_element_type=F32)
                dv_acc = dv_acc + lax.dot_general(pb, dog, (_DIMS["tn"], ((), ())), preferred_element_type=F32)
            dks.append(_fold_halves(dk_acc) * scale)
            dvs.append(_fold_halves(dv_acc))
        dq_ref[...] = jnp.concatenate(dqs, axis=1).astype(BF16)
        rows = pl.ds(pl.multiple_of(n * BLOCK, BLOCK), 2 * BLOCK)
        dk_ref[rows, :] += jnp.concatenate([jnp.where(low, dks[0], dks[1]), jnp.where(low, dks[2], dks[3])], axis=1)
        dv_ref[rows, :] += jnp.concatenate([jnp.where(low, dvs[0], dvs[1]), jnp.where(low, dvs[2], dvs[3])], axis=1)
        ds_ref[0:1, :] += dsink_row

    res, jres = _call(
        body, name=name, grid=(nblk,), in_specs=in_specs,
        out_specs=[dq_spec,
                   pl.BlockSpec((t + BLOCK, KV_WIDTH), lambda n: (0, 0)),
                   pl.BlockSpec((t + BLOCK, KV_WIDTH), lambda n: (0, 0)),
                   pl.BlockSpec((N_Q_HEADS, BLOCK, 2 * BLOCK), lambda n: (0, 0, 0)),
                   pl.BlockSpec((8, LANE), lambda n: (0, 0))],
        out_shape=[dq_shape,
                   jax.ShapeDtypeStruct((t + BLOCK, KV_WIDTH), F32),
                   jax.ShapeDtypeStruct((t + BLOCK, KV_WIDTH), F32),
                   jax.ShapeDtypeStruct((N_Q_HEADS, BLOCK, 2 * BLOCK), F32),
                   jax.ShapeDtypeStruct((8, LANE), F32)],
        sem=("arbitrary",), args=args, jobs=jobs, aliases=aliases)
    return (res, jres) if jobs else res


def _bucket_table():
    qi = np.arange(BLOCK)[:, None]
    ki = np.arange(2 * BLOCK)[None, :]
    rel = np.maximum(qi + BLOCK - ki, 0)
    max_exact = N_BUCKETS // 2
    relf = np.maximum(rel, 1).astype(np.float32)
    large = max_exact + (np.log(relf / max_exact) / math.log(MAX_DISTANCE / max_exact)
                         * (N_BUCKETS - max_exact)).astype(np.int32)
    large = np.minimum(large, N_BUCKETS - 1)
    return np.where(rel < max_exact, rel, large).astype(np.int32)


def _rel_bias_fwd(rel_bias, onehot, name):
    n = onehot.shape[1]
    tn = _tile(n, 8192, LANE)

    def body(r_ref, o_ref, out_ref):
        out_ref[...] = lax.dot_general(r_ref[...], o_ref[...], (_DIMS["tn"], ((), ())),
                                       precision=lax.Precision.HIGHEST, preferred_element_type=F32)

    return pl.pallas_call(
        body, name=name, grid=(n // tn,),
        in_specs=[pl.BlockSpec(rel_bias.shape, lambda j: (0, 0)), pl.BlockSpec((N_BUCKETS, tn), lambda j: (0, j))],
        out_specs=pl.BlockSpec((N_Q_HEADS, tn), lambda j: (0, j)),
        out_shape=jax.ShapeDtypeStruct((N_Q_HEADS, n), F32),
        compiler_params=_params(("parallel",)),
    )(rel_bias, onehot)


def _rel_bias_bwd(dbias, onehot, name):
    def body(d_ref, o_ref, out_ref):
        out_ref[...] = lax.dot_general(d_ref[...], o_ref[...], (_DIMS["nt"], ((), ())),
                                       precision=lax.Precision.HIGHEST, preferred_element_type=F32)

    full = pl.BlockSpec(dbias.shape, lambda: (0, 0))
    return pl.pallas_call(
        body, name=name, in_specs=[full, full],
        out_specs=pl.BlockSpec((N_Q_HEADS, N_BUCKETS), lambda: (0, 0)),
        out_shape=jax.ShapeDtypeStruct((N_Q_HEADS, N_BUCKETS), F32),
        compiler_params=_params(()),
    )(dbias, onehot)


def _part_specs(gparts, part_index, tr, cc):
    blk = pl.BlockSpec((tr, cc), lambda i, *_: (i, 0))
    specs = []
    for i, _ in enumerate(gparts):
        if i in part_index:
            ix = part_index[i]
            if isinstance(ix, int):
                specs.append(pl.BlockSpec((None, tr, cc), functools.partial(lambda i, *_, ix: (ix, i, 0), ix=ix)))
            else:
                specs.append(pl.BlockSpec((None, tr, cc), lambda i, ix_ref: (ix_ref[0], i, 0)))
        else:
            specs.append(blk)
    return blk, specs


def _sum_parts(gparts, part_index, name):
    r, cc = gparts[0].shape[-2:]
    tr = _tile(r, 256)
    dyn = [i for i, ix in part_index.items() if not isinstance(ix, int)]
    assert len(dyn) <= 1

    def body(*refs):
        refs = refs[len(dyn):]
        g = refs[0][...].astype(F32)
        for gr in refs[1:-1]:
            g = g + gr[...].astype(F32)
        refs[-1][...] = g

    blk, specs = _part_specs(gparts, part_index, tr, cc)
    grid_spec = pltpu.PrefetchScalarGridSpec(num_scalar_prefetch=len(dyn), grid=(r // tr,), in_specs=specs, out_specs=blk)
    return pl.pallas_call(
        body, name=name, grid_spec=grid_spec, out_shape=jax.ShapeDtypeStruct((r, cc), F32),
        compiler_params=_params(("parallel",)),
    )(*[part_index[i] for i in dyn], *gparts)


def _adamw(w, m, v, gparts, name, part_index=None, jobs=()):
    r, cc = w.shape
    tr = _tile(r, 128)
    np_ = len(gparts)
    part_index = part_index or {}
    dyn = [i for i, ix in part_index.items() if not isinstance(ix, int)]
    assert len(dyn) <= 1
    bc1 = 1.0 / (1.0 - ADAM_B1 ** ADAM_STEP)
    bc2 = 1.0 / (1.0 - ADAM_B2 ** ADAM_STEP)

    def body(*refs):
        refs = refs[len(dyn):]
        w_ref, m_ref, v_ref = refs[:3]
        g_refs = refs[3:3 + np_]
        g_out, d_out, m_out, v_out = refs[3 + np_:]
        g = g_refs[0][...].astype(F32)
        for gr in g_refs[1:]:
            g = g + gr[...].astype(F32)
        mn = ADAM_B1 * m_ref[...] + (1.0 - ADAM_B1) * g
        vn = ADAM_B2 * v_ref[...] + (1.0 - ADAM_B2) * (g * g)
        g_out[...] = g
        m_out[...] = mn
        v_out[...] = vn
        d_out[...] = -ADAM_LR * ((mn * bc1) / (jnp.sqrt(vn * bc2) + ADAM_EPS) + ADAM_WD * w_ref[...])

    blk, g_specs = _part_specs(gparts, part_index, tr, cc)
    if jobs:
        assert not dyn
        return _call(body, name=name, grid=(r // tr,), in_specs=[blk, blk, blk] + g_specs, out_specs=[blk] * 4,
                     out_shape=[jax.ShapeDtypeStruct((r, cc), F32)] * 4, sem=("arbitrary",),
                     args=[w, m, v, *gparts], jobs=jobs)
    grid_spec = pltpu.PrefetchScalarGridSpec(
        num_scalar_prefetch=len(dyn), grid=(r // tr,),
        in_specs=[blk, blk, blk] + g_specs, out_specs=[blk] * 4)
    return pl.pallas_call(
        body, name=name, grid_spec=grid_spec,
        out_shape=[jax.ShapeDtypeStruct((r, cc), F32)] * 4,
        compiler_params=_params(("parallel",)),
    )(*[part_index[i] for i in dyn], w, m, v, *gparts)


def _row(vec):
    return vec.reshape(1, -1)


def kernel(x, c, w_ada, b_ada, norm1_g, w_in, conv_w, conv_b, lru_wa, lru_ba, lru_wx, lru_bx, lru_lambda, w_lru_out, w_attn_out, attn_sinks, rel_bias, w_out, norm2_g, w_ff1, w_ff2, final_g, loss_target, m_w_ada, m_b_ada, m_norm1_g, m_w_in, m_conv_w, m_conv_b, m_lru_wa, m_lru_ba, m_lru_wx, m_lru_bx, m_lru_lambda, m_w_lru_out, m_w_attn_out, m_attn_sinks, m_rel_bias, m_w_out, m_norm2_g, m_w_ff1, m_w_ff2, m_final_g, v_w_ada, v_b_ada, v_norm1_g, v_w_in, v_conv_w, v_conv_b, v_lru_wa, v_lru_ba, v_lru_wx, v_lru_bx, v_lru_lambda, v_w_lru_out, v_w_attn_out, v_attn_sinks, v_rel_bias, v_w_out, v_norm2_g, v_w_ff1, v_w_ff2, v_final_g):
    dd = D_MODEL
    t = x.shape[1]
    ix, iy, ic = lax.axis_index("x"), lax.axis_index("y"), lax.axis_index("c")
    me = 4 * ix + 2 * iy + ic
    chip = 2 * ix + iy
    c_idx = jnp.reshape(ic, (1,)).astype(jnp.int32)
    chip_idx = jnp.reshape(chip, (1,)).astype(jnp.int32)

    xs = x[0]
    target = loss_target[0]
    ncol_ada = w_ada.shape[2]
    cshard = conv_w.shape[2]

    shards = [jnp.transpose(w_in[0]).astype(BF16), w_lru_out[0].astype(BF16), w_attn_out[0].astype(BF16),
              w_out[0].astype(BF16), jnp.transpose(w_ff1[0]).astype(BF16), w_ff2[0].astype(BF16)]
    s_in, s_lo, s_ao, s_o, s_f1, s_f2 = shards

    def full(g):
        return g.reshape(N_DEV * g.shape[1], dd)

    def stack(g):
        return g.reshape(N_DEV, g.shape[0] // N_DEV, dd)

    (w_in_pair,) = _run_job(_pair_job([s_in]), "ag_w_in_pair")
    w_in_pair = lax.dynamic_update_slice_in_dim(w_in_pair, s_in[None], ic, axis=0)

    pack0 = jnp.zeros((8, dd), F32).at[0:1, :].set(c).at[1:1 + CONV_WIDTH, 0:cshard].set(conv_w[0])
    (g0,) = _all_gather([pack0], "ag_cond")
    c_all = g0[:, 0, :]
    conv_w_full = jnp.transpose(g0[:, 1:1 + CONV_WIDTH, 0:cshard], (1, 0, 2)).reshape(CONV_WIDTH, dd)
    b_cols = lax.dynamic_slice(b_ada, (0, me * ncol_ada), (1, ncol_ada))
    mod_cols = _ada_fwd(c_all, w_ada[0], b_cols, "ada_fwd")
    (g1,) = _all_gather([mod_cols], "ag_mod")
    mod_mine = lax.dynamic_index_in_dim(g1, me, axis=1, keepdims=False).reshape(6, dd)
    mod = jnp.concatenate([mod_mine, jnp.zeros((2, dd), F32)], axis=0)

    bucket = _bucket_table()
    onehot = jnp.asarray((bucket.reshape(1, -1) == np.arange(N_BUCKETS)[:, None]).astype(np.float32))
    bias = _rel_bias_fwd(rel_bias, onehot, "rel_bias_fwd").reshape(N_Q_HEADS, BLOCK, 2 * BLOCK)
    sinks = attn_sinks.reshape(N_Q_HEADS)
    lp = jnp.concatenate([conv_b, lru_ba, lru_bx, lru_lambda, conv_w_full], axis=0)
    wa, wx = lru_wa[0], lru_wx[0]

    tm = _tile(t, 1024)
    tn = 512
    h = _norm_mod(xs, norm1_g, mod, 0, "norm1")
    tn2 = 1024
    chunk_w = 2 * s_in.shape[0]

    def ids(*ks):
        return jnp.stack([jnp.asarray(k, jnp.int32) for k in ks])

    tn_in = _tile(chunk_w, 1024, LANE)
    (proj,), ((g_near,),) = _mm(
        h, w_in_pair.reshape(chunk_w, dd), "nt", tm=tm, tn=tn_in, outs=[BF16], name="in_proj_own",
        chunks=(ids(chip), chunk_w, ids(0)), n_total=IN_WIDTH, jobs=[_gather_job([w_in_pair], (0.7,), "near")])
    near_ids = ids(2 * (1 - ix) + iy, 2 * ix + 1 - iy)
    (proj,), ((g_far,), (g_o,)) = _mm(
        h, full(g_near), "nt", tm=tm, tn=tn_in, outs=[BF16], name="in_proj_near",
        chunks=(near_ids, chunk_w, near_ids), into=proj,
        jobs=[_gather_job([g_near], (0.6,), "far"), _gather_job([s_o], (0.6, 0.85))])
    far_chip = 2 * (1 - ix) + 1 - iy
    (proj,), ((g_lo,),) = _mm(
        h, full(g_far), "nt", tm=tm, tn=tn_in, outs=[BF16], name="in_proj_far",
        chunks=(ids(far_chip), chunk_w, ids(far_chip)), into=proj, jobs=[_gather_job([s_lo], (0.5, 0.8))])
    far_slots = lax.dynamic_slice_in_dim(g_far, 2 * far_chip, 2, axis=0)
    win_t = lax.dynamic_update_slice_in_dim(g_near, far_slots, 2 * far_chip, axis=0)
    win_t = full(lax.dynamic_update_slice_in_dim(win_t, w_in_pair, 2 * chip, axis=0))
    (g_o,) = _fill_own([g_o], [s_o])
    (rec, ya), ((g_f1,),) = _lru_fwd(proj, lp, wa, wx, "lru_fwd", jobs=[_gather_job([s_f1], (0.6, 0.9))])
    (att, lse), ((g_ao,),) = _attn_fwd(proj, bias, sinks, "attn_fwd", jobs=[_gather_job([s_ao], (0.4, 0.7))])
    g_lo, g_ao, g_f1 = _fill_own([g_lo, g_ao, g_f1], [s_lo, s_ao, s_f1])
    wlo, wao, wo, wf1_t = full(g_lo), full(g_ao), full(g_o), full(g_f1)
    (y_a,) = _mm(ya, wlo, "nn", tm=tm, tn=tn2, outs=[BF16], name="lru_out")

    def merge_epi(acc, ya_t, ga_t, gb_t):
        return acc, _sigmoid(ga_t.astype(F32)) * ya_t.astype(F32) + _sigmoid(gb_t.astype(F32)) * acc

    y_b, merged = _mm(att, wao, "nn", tm=tm, tn=tn, outs=[BF16, BF16], epi=merge_epi,
                      tiles=[(y_a, 0), (proj, OFF_GA // tn), (proj, OFF_GB // tn)], name="attn_out_merge")

    def resid_epi(row):
        def epi(acc, x_t, mod_t):
            return acc, x_t + mod_t[row:row + 1, :] * acc
        return epi

    o1, x1 = _mm(merged, wo, "nn", tm=tm, tn=tn2, outs=[BF16, F32], epi=resid_epi(2),
                 tiles=[(xs, 0)], rows=[mod], name="out_proj")
    h2 = _norm_mod(x1, norm2_g, mod, 3, "norm2")

    def relu2_epi(acc):
        rl = jnp.maximum(acc, 0.0)
        return acc, rl * rl

    (f_pre, ff), ((g_f2,),) = _mm(h2, wf1_t, "nt", tm=tm, tn=1024, outs=[BF16, BF16], epi=relu2_epi, name="ff1",
                                  jobs=[_gather_job([s_f2], (0.6, 0.9))])
    wf2 = full(_fill_own([g_f2], [s_f2])[0])
    tmh = _tile(t, 512)
    o2, x2 = _mm(ff, wf2, "nn", tm=tmh, tn=tn, outs=[BF16, F32], epi=resid_epi(5),
                 tiles=[(x1, 0)], rows=[mod], name="ff2")

    loss_blk, dx2, do2, st_f = _final_loss(x2, o2, _row(final_g), target, mod, "final_loss")
    loss = lax.psum(loss_blk[0, 0], ("x", "y", "c"))

    def drelu2_epi(acc, f_t):
        return (acc * (2.0 * jnp.maximum(f_t.astype(F32), 0.0)),)

    tmw, tnw, tkw = 512, dd, _tile(t, 2048)
    (df,) = _mm(do2, wf2, "nt", tm=tm, tn=1024, outs=[BF16], epi=drelu2_epi, tiles=[(f_pre, 0)], name="ff2_dx")
    (dwf2,) = _mm(ff, do2, "tn", tm=tmw, tn=tnw, tk=tkw, outs=[BF16], name="ff2_dw")
    (dh2,), ((sib_f2,),) = _mm(df, wf1_t, "nn", tm=tmh, tn=tn, outs=[BF16], name="ff1_dx",
                              jobs=[_sibling_job([stack(dwf2)])])
    pair_f2 = _pair_add(stack(dwf2), sib_f2, c_idx, "rs_pair_add_w_ff2")
    (dwf1_t,), ((chips_f2,),) = _mm(df, h2, "tn", tm=tmw, tn=tnw, tk=tkw, outs=[BF16], name="ff1_dw",
                                    jobs=[_chips_job([pair_f2])])
    (dx1, st_2, do1), ((sib_f1,),) = _norm_mod_bwd(x1, norm2_g, mod, 3, dh2, dx2, (o1, 2), "norm2_bwd",
                                                   jobs=[_sibling_job([stack(dwf1_t)])])
    pair_f1 = _pair_add(stack(dwf1_t), sib_f1, c_idx, "rs_pair_add_w_ff1")

    def dmerge_epi(acc, ga_t, gb_t, ya_t, yb_t):
        sa, sb = _sigmoid(ga_t.astype(F32)), _sigmoid(gb_t.astype(F32))
        return (acc * sa, acc * sb, acc * ya_t.astype(F32) * sa * (1.0 - sa), acc * yb_t.astype(F32) * sb * (1.0 - sb))

    tmh = _tile(t, 512)
    dy_a, dy_b, dproj, dgb = _mm(do1, wo, "nt", tm=tm, tn=tn, outs=[BF16] * 4, epi=dmerge_epi,
                                 tiles=[(proj, OFF_GA // tn), (proj, OFF_GB // tn), (y_a, 0), (y_b, 0)],
                                 wide={2: (IN_WIDTH, OFF_GA // tn)}, name="out_proj_dx")
    (dwo,) = _mm(merged, do1, "tn", tm=tmw, tn=tnw, tk=tkw, outs=[BF16], name="out_proj_dw")
    (datt,) = _mm(dy_b, wao, "nt", tm=tm, tn=tn2, outs=[BF16], name="attn_out_dx")
    (dwao,) = _mm(att, dy_b, "tn", tm=tmw, tn=tnw, tk=tkw, outs=[BF16], name="attn_out_dw")
    (dya,) = _mm(dy_a, wlo, "nt", tm=tm, tn=tn2, outs=[BF16], name="lru_out_dx")
    (dwlo,) = _mm(ya, dy_a, "tn", tm=tmw, tn=tnw, tk=tkw, outs=[BF16], name="lru_out_dw")
    (dproj, dk_pad, dv_pad, dbias, dsinks), (sib_3, (chips_f1,)) = _attn_bwd(
        proj, datt, lse, bias, sinks, "attn_bwd", into=dproj,
        jobs=[_sibling_job([stack(dwlo), stack(dwao), stack(dwo)]), _chips_job([pair_f1])])
    pair_lo, pair_ao, pair_o = [_pair_add(stack(g_), s_, c_idx, "rs_pair_add_" + nm_)
                                for g_, s_, nm_ in zip((dwlo, dwao, dwo), sib_3, ("w_lru_out", "w_attn_out", "w_out"))]
    (dproj, st_l, dwa, dwx), ((chips_lo, chips_ao, chips_o),) = _lru_bwd(
        proj, rec, dya, lp, wa, wx, "lru_bwd", into=dproj, jobs=[_chips_job([pair_lo, pair_ao, pair_o])])
    dkv = jnp.concatenate([dk_pad[BLOCK:].astype(BF16), dv_pad[BLOCK:].astype(BF16)], axis=1)
    dproj = lax.dynamic_update_slice_in_dim(dproj, dkv, OFF_K, axis=1)
    dproj = lax.dynamic_update_slice_in_dim(dproj, dgb, OFF_GB, axis=1)
    lru_mats = jnp.concatenate([dwa.reshape(-1, dd), dwx.reshape(-1, dd)], axis=0)
    (dwin_t,), ((mats_all,),) = _mm(dproj, h, "tn", tm=768, tn=tnw, tk=tkw, outs=[BF16], name="in_proj_dw",
                                   jobs=[_gather_job([lru_mats], (0.3, 0.6))])
    (mats_all,) = _fill_own([mats_all], [lru_mats])
    (sib_in,) = _exchange_sibling([stack(dwin_t)], "rs_sibling_w_in")
    pair_in = _pair_add(stack(dwin_t), sib_in, c_idx, "rs_pair_add_w_in", widths=(dd // 2, dd // 2))
    n_rb = t // tm
    dx_tiles = dict(tm=tm, tn=tn2, tk=IN_WIDTH // 4, outs=[BF16])
    if n_rb >= 2:
        (dh,), ((chips_in_0,),) = _mm(dproj, win_t, "nn", name="in_proj_dx_top", rows_part=(0, n_rb // 2),
                                     jobs=[_chips_job(pair_in[0:1])], **dx_tiles)
        (dh,), ((chips_in_1,),) = _mm(dproj, win_t, "nn", name="in_proj_dx_bottom", into=dh,
                                     rows_part=(n_rb // 2, n_rb - n_rb // 2), jobs=[_chips_job(pair_in[1:2])], **dx_tiles)
    else:
        (dh,), ((chips_in_0,), (chips_in_1,)) = _mm(
            dproj, win_t, "nn", name="in_proj_dx", jobs=[_chips_job(pair_in[0:1]), _chips_job(pair_in[1:2])], **dx_tiles)
    grad_x, st_1 = _norm_mod_bwd(xs, norm1_g, mod, 0, dh, dx1, None, "norm1_bwd")
    chips_in = [chips_in_0, chips_in_1]

    drel = jnp.transpose(_rel_bias_bwd(dbias.reshape(N_Q_HEADS, -1), onehot, "rel_bias_bwd"))

    misc = jnp.concatenate([drel.reshape(1, -1), dsinks[0:1, 0:N_Q_HEADS],
                            jnp.zeros((1, dd - N_BUCKETS * N_Q_HEADS - N_Q_HEADS), F32)], axis=1)
    zero_row = jnp.zeros((1, dd), F32)
    small = jnp.concatenate([
        st_1[0:2], st_2[3:4], st_2[0:2], st_f[1:2], zero_row, zero_row,
        st_1[2:3], st_l[0:1], st_l[1:2], st_l[2:3], st_l[3:4], st_2[2:3], st_f[0:1], misc,
        st_l[4:8], jnp.zeros((4, dd), F32)], axis=0)
    own_and_chips = {0: chip_idx, 1: 0, 2: 1, 3: 2}
    g_in = jnp.transpose(jnp.concatenate(
        [_sum_parts([p_, c_, c_, c_], own_and_chips, "sum_w_in_%d" % q_)
         for q_, (p_, c_) in enumerate(zip(pair_in, chips_in))], axis=1))
    w_in_res, ((small_all,),) = _adamw(w_in[0], m_w_in[0], v_w_in[0], [g_in], "adamw_w_in",
                                       jobs=[_gather_job([small], (0.3, 0.6))])
    (small_all,) = _fill_own([small_all], [small])

    def pack_small(b_, n1, cb, ba, bx, lam, n2, fg, rb, sk):
        misc_ = jnp.concatenate([rb.reshape(1, -1), sk.reshape(1, -1),
                                 jnp.zeros((1, dd - N_BUCKETS * N_Q_HEADS - N_Q_HEADS), F32)], axis=1)
        return jnp.concatenate([b_.reshape(6, dd), jnp.zeros((2, dd), F32), n1, cb, ba, bx, lam, n2, _row(fg), misc_,
                                jnp.zeros((8, dd), F32)], axis=0)

    def pack_mats(wa_, wx_):
        return jnp.concatenate([wa_.reshape(-1, dd), wx_.reshape(-1, dd)], axis=0)

    every = {i: i for i in range(N_DEV)}
    w_s = pack_small(b_ada, norm1_g, conv_b, lru_ba, lru_bx, lru_lambda, norm2_g, final_g, rel_bias, attn_sinks)
    m_s = pack_small(m_b_ada, m_norm1_g, m_conv_b, m_lru_ba, m_lru_bx, m_lru_lambda, m_norm2_g, m_final_g, m_rel_bias, m_attn_sinks)
    v_s = pack_small(v_b_ada, v_norm1_g, v_conv_b, v_lru_ba, v_lru_bx, v_lru_lambda, v_norm2_g, v_final_g, v_rel_bias, v_attn_sinks)
    small_res = _adamw(w_s, m_s, v_s, [small_all] * N_DEV, "adamw_small", part_index=every)
    mats_res = _adamw(pack_mats(lru_wa, lru_wx), pack_mats(m_lru_wa, m_lru_wx), pack_mats(v_lru_wa, v_lru_wx),
                      [mats_all] * N_DEV, "adamw_lru_mats", part_index=every)

    def unpack_small(s, mt):
        nb_ = N_BUCKETS * N_Q_HEADS
        half = mt.shape[0] // 2
        return dict(
            b_ada=s[0:6].reshape(1, 6 * dd), norm1_g=s[8:9], conv_b=s[9:10], lru_ba=s[10:11], lru_bx=s[11:12],
            lru_lambda=s[12:13], norm2_g=s[13:14], final_g=s[14], rel_bias=s[15, 0:nb_].reshape(N_BUCKETS, N_Q_HEADS),
            attn_sinks=s[15:16, nb_:nb_ + N_Q_HEADS],
            lru_wa=mt[:half].reshape(1, LRU_BLOCKS, LRU_BLOCK_W, LRU_BLOCK_W),
            lru_wx=mt[half:].reshape(1, LRU_BLOCKS, LRU_BLOCK_W, LRU_BLOCK_W))

    res = {k: [None] * 4 for k in ("w_ada", "w_in", "conv_w", "w_lru_out", "w_attn_out", "w_out", "w_ff1", "w_ff2")}
    for q_, (s_, mt_) in enumerate(zip(small_res, mats_res)):
        for k_, val in unpack_small(s_, mt_).items():
            res.setdefault(k_, [None] * 4)[q_] = val

    g_conv = lax.dynamic_slice(small_res[0][16:16 + CONV_WIDTH], (0, me * cshard), (CONV_WIDTH, cshard))
    conv_res = _adamw(conv_w[0], m_conv_w[0], v_conv_w[0], [g_conv], "adamw_conv")
    res["conv_w"] = [r_[None] for r_ in conv_res]

    dmod_all = small_all[:, 0:6, :].reshape(N_DEV, 6 * dd)
    dmod_cols = lax.dynamic_slice(dmod_all, (0, me * ncol_ada), (N_DEV, ncol_ada))
    g_ada = _ada_bwd(jnp.transpose(c_all), dmod_cols, "ada_bwd")
    ada_res = _adamw(w_ada[0], m_w_ada[0], v_w_ada[0], [g_ada], "adamw_ada")
    res["w_ada"] = [r_[None] for r_ in ada_res]

    pair = [None, pair_lo, pair_ao, pair_o, pair_f1, pair_f2]
    from_chips = [None, chips_lo, chips_ao, chips_o, chips_f1, chips_f2]

    def summed(i):
        return [pair[i], from_chips[i], from_chips[i], from_chips[i]], own_and_chips

    def sum_only(i, name):
        parts, index = summed(i)
        return _sum_parts(parts, index, name)

    res["w_in"] = [r_[None] for r_ in w_in_res]
    g_ff1 = jnp.transpose(sum_only(4, "sum_w_ff1"))
    res["w_ff1"] = [r_[None] for r_ in _adamw(w_ff1[0], m_w_ff1[0], v_w_ff1[0], [g_ff1], "adamw_w_ff1")]
    for i, (nm, w_, m_, v_) in {1: ("w_lru_out", w_lru_out, m_w_lru_out, v_w_lru_out),
                                2: ("w_attn_out", w_attn_out, m_w_attn_out, v_w_attn_out),
                                3: ("w_out", w_out, m_w_out, v_w_out),
                                5: ("w_ff2", w_ff2, m_w_ff2, v_w_ff2)}.items():
        parts, index = summed(i)
        res[nm] = [r_[None] for r_ in _adamw(w_[0], m_[0], v_[0], parts, "adamw_" + nm, part_index=index)]

    order = ["w_ada", "b_ada", "norm1_g", "w_in", "conv_w", "conv_b", "lru_wa", "lru_ba", "lru_wx", "lru_bx",
             "lru_lambda", "w_lru_out", "w_attn_out", "attn_sinks", "rel_bias", "w_out", "norm2_g", "w_ff1",
             "w_ff2", "final_g"]
    out = [loss, grad_x[None]]
    for q_ in range(4):
        out += [res[k_][q_] for k_ in order]
    return tuple(out)
```

```python
import functools
import math

import numpy as np
import jax
import jax.numpy as jnp
from jax import lax
from jax.experimental import pallas as pl
from jax.experimental.pallas import tpu as pltpu

F32 = jnp.float32
BF16 = jnp.bfloat16
MESH = pl.DeviceIdType.MESH

D_MODEL = 2048
N_Q_HEADS = 32
N_KV_HEADS = 4
GROUP = N_Q_HEADS // N_KV_HEADS
HEAD_DIM = 64
KV_WIDTH = N_KV_HEADS * HEAD_DIM
BLOCK = 128
NEG_INF = -1e30
N_BUCKETS = 32
MAX_DISTANCE = 128
LRU_BLOCKS = 16
LRU_BLOCK_W = 128
CONV_WIDTH = 4
LRU_C = 8.0
D_FF = 4 * D_MODEL
EPS = 1e-6
IN_WIDTH = 5 * D_MODEL + 2 * KV_WIDTH
OFF_LX, OFF_LG, OFF_Q, OFF_K, OFF_V, OFF_GA, OFF_GB = 0, 2048, 4096, 6144, 6400, 6656, 8704

ADAM_LR, ADAM_B1, ADAM_B2, ADAM_EPS, ADAM_WD, ADAM_STEP = 0.001, 0.9, 0.999, 1e-08, 0.01, 10

N_DEV = 8
VMEM_LIMIT_BYTES = 48 * 1024 * 1024
LANE = 128
LRU_CW = 512


def _params(sem, **kw):
    return pltpu.CompilerParams(dimension_semantics=sem, vmem_limit_bytes=VMEM_LIMIT_BYTES, **kw)


def _tile(n, pref, mult=8):
    if n <= pref:
        return n
    t = (pref // mult) * mult
    while t >= mult:
        if n % t == 0:
            return t
        t -= mult
    return n


def _sigmoid(x):
    return 0.5 * jnp.tanh(0.5 * x) + 0.5


def _gelu_tanh(x):
    k = math.sqrt(2.0 / math.pi)
    t = jnp.tanh(k * (x + 0.044715 * x * x * x))
    return 0.5 * x * (1.0 + t), t


def _gelu_tanh_grad(x, t):
    k = math.sqrt(2.0 / math.pi)
    return 0.5 * (1.0 + t) + 0.5 * x * (1.0 - t * t) * k * (1.0 + 3.0 * 0.044715 * x * x)


def _log1p(e):
    u = 1.0 + e
    return jnp.where(u == 1.0, e, jnp.log(u) * (e / jnp.where(u == 1.0, 1.0, u - 1.0)))


def _softplus(x):
    return jnp.maximum(x, 0.0) + _log1p(jnp.exp(-jnp.abs(x)))


def _neg_expm1(x, exp_x):
    series = -x * (1.0 + x * (0.5 + x * (1.0 / 6.0 + x * (1.0 / 24.0))))
    return jnp.where(x > -0.0625, series, 1.0 - exp_x)


def _my_place():
    return lax.axis_index("x"), lax.axis_index("y"), lax.axis_index("c")


def _all_gather(arrs, name):
    return _fill_own(_run_job(_gather_job(arrs), name), arrs)


def _fill_own(stacks, shards):
    me = 4 * lax.axis_index("x") + 2 * lax.axis_index("y") + lax.axis_index("c")
    out = []
    for g, s in zip(stacks, shards):
        if g.size * g.dtype.itemsize <= (1 << 20):
            slot = lax.broadcasted_iota(jnp.int32, g.shape, 0)
            out.append(jnp.where(slot == me, s[None], g))
        else:
            out.append(lax.dynamic_update_slice_in_dim(g, s[None], me, axis=0))
    return out


class _Job:
    def __init__(self, ins, outs, sems, start, finish, forwards=(), forward_at=()):
        self.ins, self.outs, self.sems = list(ins), list(outs), list(sems)
        self.start, self.finish, self.forwards, self.forward_at = start, finish, list(forwards), list(forward_at)
        assert len(self.forwards) <= len(self.forward_at)


def _gather_job(arrs, forward_at=(0.6, 0.85), part="all"):
    n = len(arrs)
    whole, near, far = part == "all", part == "near", part == "far"

    def copies(ins, outs, sems):
        send_sems, recv_sems = sems
        x, y, c = _my_place()
        sib = (x, y, 1 - c)
        xn, yn, dg = (1 - x, y), (x, 1 - y), (1 - x, 1 - y)

        def slot(px, py, pc):
            return 4 * px + 2 * py + pc

        def cp(a, k, src, dst_slot, to):
            return pltpu.make_async_remote_copy(
                src_ref=src, dst_ref=outs[a].at[dst_slot], send_sem=send_sems.at[a, k], recv_sem=recv_sems.at[a, k],
                device_id=to, device_id_type=MESH)

        def arrival(a, k, dst_slot):
            return cp(a, k, outs[a].at[dst_slot], dst_slot, sib)

        me_slot = slot(x, y, c)
        sends, relay, passes, arrive = {}, {}, {}, {}
        for a in range(n):
            if whole:
                own = ins[a]
                sends[a, 0] = cp(a, 0, own, me_slot, sib)
            if near:
                own = ins[a].at[c]
            if whole or near:
                sends[a, 1] = cp(a, 1, own, me_slot, (*xn, c))
                sends[a, 2] = cp(a, 2, own, me_slot, (*yn, c))
                passes[a, 4] = cp(a, 4, outs[a].at[slot(*xn, c)], slot(*xn, c), sib)
                passes[a, 5] = cp(a, 5, outs[a].at[slot(*yn, c)], slot(*yn, c), sib)
            if whole or far:
                source = outs[a] if whole else ins[a]
                relayed = c * slot(*yn, c) + (1 - c) * slot(*xn, c)
                relay[a, 3] = cp(a, 3, source.at[relayed], relayed, (x + c * (1 - 2 * x), y + (1 - c) * (1 - 2 * y), c))
                passes[a, 6] = cp(a, 6, outs[a].at[slot(*dg, c)], slot(*dg, c), sib)
            arrive[a, 0] = arrival(a, 0, slot(x, y, 1 - c))
            arrive[a, 1] = arrival(a, 1, slot(*xn, c))
            arrive[a, 2] = arrival(a, 2, slot(*yn, c))
            arrive[a, 3] = arrival(a, 3, slot(*dg, c))
            arrive[a, 4] = arrival(a, 4, slot(*xn, 1 - c))
            arrive[a, 5] = arrival(a, 5, slot(*yn, 1 - c))
            arrive[a, 6] = arrival(a, 6, slot(*dg, 1 - c))
        return sends, relay, passes, arrive

    def pick(d, ks):
        return [d[a, k] for a in range(n) for k in ks if (a, k) in d]

    def start(ins, outs, sems):
        sends, relay, _, _ = copies(ins, outs, sems)
        for cp in pick(sends, (0, 1, 2)) + (pick(relay, (3,)) if far else []):
            cp.start()

    def forward_neighbours(ins, outs, sems):
        _, relay, passes, arrive = copies(ins, outs, sems)
        for cp in pick(arrive, (1, 2)):
            cp.wait_recv()
        for cp in pick(relay, (3,)) + pick(passes, (4, 5)):
            cp.start()

    def forward_diagonal(ins, outs, sems):
        _, _, passes, arrive = copies(ins, outs, sems)
        for cp in pick(arrive, (3,)):
            cp.wait_recv()
        for cp in pick(passes, (6,)):
            cp.start()

    def finish(ins, outs, sems):
        sends, relay, passes, arrive = copies(ins, outs, sems)
        for cp in pick(arrive, (6,) if far else (0, 4, 5, 6) if whole else (4, 5)):
            cp.wait_recv()
        for cp in pick(sends, (0, 1, 2)) + pick(relay, (3,)) + pick(passes, (4, 5, 6)):
            cp.wait_send()

    shapes = [a.shape[-2:] for a in arrs]
    forwards = [forward_diagonal] if far else [forward_neighbours] if near else [forward_neighbours, forward_diagonal]
    return _Job(arrs, [jax.ShapeDtypeStruct((N_DEV,) + s, a.dtype) for s, a in zip(shapes, arrs)],
                [pltpu.SemaphoreType.DMA((n, 7)), pltpu.SemaphoreType.DMA((n, 7))],
                start, finish, forwards, forward_at)


def _pair_job(arrs):
    n = len(arrs)

    def copies(ins, outs, sems):
        send_sems, recv_sems = sems
        x, y, c = _my_place()
        remote = [pltpu.make_async_remote_copy(
            src_ref=ins[a], dst_ref=outs[a].at[c], send_sem=send_sems.at[a], recv_sem=recv_sems.at[a],
            device_id=(x, y, 1 - c), device_id_type=MESH) for a in range(n)]
        arrive = [pltpu.make_async_remote_copy(
            src_ref=ins[a], dst_ref=outs[a].at[1 - c], send_sem=send_sems.at[a], recv_sem=recv_sems.at[a],
            device_id=(x, y, 1 - c), device_id_type=MESH) for a in range(n)]
        return remote, arrive

    def start(ins, outs, sems):
        for cp in copies(ins, outs, sems)[0]:
            cp.start()

    def finish(ins, outs, sems):
        remote, arrive = copies(ins, outs, sems)
        for cp in arrive:
            cp.wait_recv()
        for cp in remote:
            cp.wait_send()

    return _Job(arrs, [jax.ShapeDtypeStruct((2,) + a.shape, a.dtype) for a in arrs],
                [pltpu.SemaphoreType.DMA((n,)), pltpu.SemaphoreType.DMA((n,))], start, finish)


def _run_job(job, name):
    ni, no = len(job.ins), len(job.outs)

    def body(*refs):
        ins, outs, sems = refs[:ni], refs[ni:ni + no], refs[ni + no:]
        job.start(ins, outs, sems)
        for fwd in job.forwards:
            fwd(ins, outs, sems)
        job.finish(ins, outs, sems)

    any_spec = pl.BlockSpec(memory_space=pl.ANY)
    return pl.pallas_call(
        body, name=name, out_shape=job.outs, in_specs=[any_spec] * ni, out_specs=[any_spec] * no,
        scratch_shapes=job.sems,
    )(*job.ins)


def _call(body, *, name, grid, in_specs, out_specs, out_shape, scratch_shapes=(), sem, args, jobs=(),
          prefetch=(), aliases=None):
    in_specs, out_specs, out_shape, scratch_shapes = list(in_specs), list(out_specs), list(out_shape), list(scratch_shapes)
    prefetch = list(prefetch)
    if not jobs and not prefetch and not aliases:
        res = pl.pallas_call(body, name=name, grid=grid, in_specs=in_specs, out_specs=out_specs, out_shape=out_shape,
                             scratch_shapes=scratch_shapes, compiler_params=_params(sem))(*args)
        return list(res), []
    n_in, n_out, n_scr = len(in_specs), len(out_specs), len(scratch_shapes)
    j_in = [len(j.ins) for j in jobs]
    j_out = [len(j.outs) for j in jobs]
    j_sem = [len(j.sems) for j in jobs]
    total = int(np.prod(grid))
    any_spec = pl.BlockSpec(memory_space=pl.ANY)

    def wrapped(*refs):
        pos = [0]

        def take(k):
            part = refs[pos[0]:pos[0] + k]
            pos[0] += k
            return part

        take(len(prefetch))
        ins = take(n_in)
        jins = [take(k) for k in j_in]
        outs = take(n_out)
        jouts = [take(k) for k in j_out]
        scr = take(n_scr)
        jsems = [take(k) for k in j_sem]
        step = pl.program_id(0)
        for d in range(1, len(grid)):
            step = step * grid[d] + pl.program_id(d)
        for j, job in enumerate(jobs):
            pl.when(step == 0)(functools.partial(job.start, jins[j], jouts[j], jsems[j]))
            for fwd, frac in zip(job.forwards, job.forward_at):
                at = min(int(total * frac), total - 1)
                pl.when(step == at)(functools.partial(fwd, jins[j], jouts[j], jsems[j]))
        body(*ins, *outs, *scr)
        for j, job in enumerate(jobs):
            pl.when(step == total - 1)(functools.partial(job.finish, jins[j], jouts[j], jsems[j]))

    grid_spec = pltpu.PrefetchScalarGridSpec(
        num_scalar_prefetch=len(prefetch), grid=grid,
        in_specs=in_specs + [any_spec] * sum(j_in),
        out_specs=out_specs + [any_spec] * sum(j_out),
        scratch_shapes=scratch_shapes + [s for j in jobs for s in j.sems])
    res = pl.pallas_call(
        wrapped, name=name, grid_spec=grid_spec,
        out_shape=out_shape + [o for j in jobs for o in j.outs],
        input_output_aliases={len(prefetch) + i: o for i, o in (aliases or {}).items()},
        compiler_params=_params(("arbitrary",) * len(grid)),
    )(*prefetch, *args, *[a for j in jobs for a in j.ins])
    res = list(res)
    own, rest = res[:n_out], res[n_out:]
    per_job = []
    for k in j_out:
        per_job.append(rest[:k])
        rest = rest[k:]
    return own, per_job


def _exchange_sibling(arrs, name):
    return _run_job(_sibling_job(arrs), name)


def _sibling_job(arrs):
    n = len(arrs)

    def copies(ins, outs, sems):
        send_sems, recv_sems = sems
        x, y, c = _my_place()
        return [pltpu.make_async_remote_copy(
            src_ref=ins[a].at[2 * k + 1 - c], dst_ref=outs[a].at[k],
            send_sem=send_sems.at[a, k], recv_sem=recv_sems.at[a, k],
            device_id=(x, y, 1 - c), device_id_type=MESH) for a in range(n) for k in range(4)]

    def start(ins, outs, sems):
        for cp in copies(ins, outs, sems):
            cp.start()

    def finish(ins, outs, sems):
        for cp in copies(ins, outs, sems):
            cp.wait()

    return _Job(arrs, [jax.ShapeDtypeStruct((4,) + a.shape[1:], a.dtype) for a in arrs],
                [pltpu.SemaphoreType.DMA((n, 4)), pltpu.SemaphoreType.DMA((n, 4))], start, finish)


def _chips_job(arrs):
    n = len(arrs)

    def copies(ins, outs, sems):
        send_sems, recv_sems = sems
        x, y, c = _my_place()
        chips = [(1 - x, y), (x, 1 - y), (1 - x, 1 - y)]
        return [pltpu.make_async_remote_copy(
            src_ref=ins[a].at[2 * px + py], dst_ref=outs[a].at[j],
            send_sem=send_sems.at[a, j], recv_sem=recv_sems.at[a, j],
            device_id=(px, py, c), device_id_type=MESH) for a in range(n) for j, (px, py) in enumerate(chips)]

    def start(ins, outs, sems):
        for cp in copies(ins, outs, sems):
            cp.start()

    def finish(ins, outs, sems):
        for cp in copies(ins, outs, sems):
            cp.wait()

    return _Job(arrs, [jax.ShapeDtypeStruct((3,) + a.shape[1:], a.dtype) for a in arrs],
                [pltpu.SemaphoreType.DMA((n, 3)), pltpu.SemaphoreType.DMA((n, 3))], start, finish)


def _pair_add(stack, from_sibling, c_idx, name, widths=None):
    _, r, cc = stack.shape
    tr = _tile(r, 512)
    ws = [cc] if widths is None else list(widths)
    assert sum(ws) == cc
    starts = [sum(ws[:q]) for q in range(len(ws))]

    def body(c_ref, a_ref, b_ref, *o_refs):
        res = a_ref[...].astype(F32) + b_ref[...].astype(F32)
        for o_ref, s0, w_ in zip(o_refs, starts, ws):
            o_ref[...] = res[:, s0:s0 + w_].astype(o_ref.dtype)

    grid_spec = pltpu.PrefetchScalarGridSpec(
        num_scalar_prefetch=1, grid=(4, r // tr),
        in_specs=[pl.BlockSpec((None, tr, cc), lambda k, i, c_ref: (2 * k + c_ref[0], i, 0)),
                  pl.BlockSpec((None, tr, cc), lambda k, i, c_ref: (k, i, 0))],
        out_specs=[pl.BlockSpec((None, tr, w_), lambda k, i, c_ref: (k, i, 0)) for w_ in ws])
    res = pl.pallas_call(
        body, name=name, grid_spec=grid_spec,
        out_shape=[jax.ShapeDtypeStruct((4, r, w_), BF16) for w_ in ws],
        compiler_params=_params(("parallel", "parallel")),
    )(c_idx, stack, from_sibling)
    return res[0] if widths is None else list(res)


_DIMS = {"nn": ((1,), (0,)), "nt": ((1,), (1,)), "tn": ((0,), (0,))}


def _mm(a, b, mode, *, tm, tn, tk=None, outs, epi=None, tiles=(), rows=(), name, jobs=(), chunks=None, into=None,
        n_total=None, rows_part=None, wide=None):
    wide = wide or {}
    if mode == "tn":
        kk, m = a.shape
    else:
        m, kk = a.shape
    n_all = b.shape[0] if mode == "nt" else b.shape[1]
    nn = n_all if chunks is None else chunks[0].shape[0] * chunks[1]
    n_all = n_all if n_total is None else n_total
    tk = kk if tk is None else tk
    assert m % tm == 0 and nn % tn == 0 and kk % tk == 0, (name, a.shape, b.shape, tm, tn, tk)
    nk = kk // tk
    nt, nr, no = len(tiles), len(rows), len(outs)
    n_into = 0 if into is None else 1

    def body(*refs):
        a_ref, b_ref = refs[:2]
        tile_refs = refs[2:2 + nt]
        row_refs = refs[2 + nt:2 + nt + nr]
        out_refs = refs[2 + nt + nr + n_into:2 + nt + nr + n_into + no]
        part = lax.dot_general(a_ref[...], b_ref[...], (_DIMS[mode], ((), ())), preferred_element_type=F32)

        def finish(acc):
            if epi is None:
                res = (acc,)
            else:
                res = epi(acc, *[t[...] for t in tile_refs], *[r[...] for r in row_refs])
            for o_ref, val in zip(out_refs, res):
                o_ref[...] = val.astype(o_ref.dtype)

        if nk == 1:
            finish(part)
        else:
            acc_ref = refs[-1]
            k = pl.program_id(2)

            @pl.when(k == 0)
            def _():
                acc_ref[...] = part

            @pl.when(k > 0)
            def _():
                acc_ref[...] += part

            @pl.when(k == nk - 1)
            def _():
                finish(acc_ref[...])

    if chunks is None:
        col = b_row = lambda j, *_: j
        prefetch = []
    else:
        assert mode == "nt" and not tiles and not rows and chunks[1] % tn == 0
        per = chunks[1] // tn
        col = lambda j, ids, b_ids: ids[j // per] * per + j % per
        b_row = lambda j, ids, b_ids: b_ids[j // per] * per + j % per
        prefetch = [chunks[0], chunks[2]]
    row0, n_row_blocks = (0, m // tm) if rows_part is None else rows_part
    assert rows_part is None or (mode != "tn" and not tiles)
    if mode == "tn":
        a_spec = pl.BlockSpec((tk, tm), lambda i, j, k, *_: (k, i))
    else:
        a_spec = pl.BlockSpec((tm, tk), lambda i, j, k, *_: (i + row0, k))
    if mode == "nt":
        b_spec = pl.BlockSpec((tn, tk), lambda i, j, k, *s: (b_row(j, *s), k))
    else:
        b_spec = pl.BlockSpec((tk, tn), lambda i, j, k, *_: (k, j))
    tile_specs = [pl.BlockSpec((tm, tn), functools.partial(lambda i, j, k, off: (i, j + off), off=off))
                  for _, off in tiles]
    row_specs = [pl.BlockSpec((r.shape[0], tn), lambda i, j, k: (0, j)) for r in rows]
    out_spec = pl.BlockSpec((tm, tn), lambda i, j, k, *s: (i + row0, col(j, *s)))
    args = [a, b, *[t for t, _ in tiles], *rows]
    in_specs = [a_spec, b_spec] + tile_specs + row_specs
    aliases = None
    if into is not None:
        aliases = {len(args): 0}
        args.append(into)
        in_specs.append(pl.BlockSpec(memory_space=pl.ANY))
    res, jres = _call(
        body, name=name, grid=(n_row_blocks, nn // tn, nk),
        in_specs=in_specs,
        out_specs=[out_spec if q not in wide else
                   pl.BlockSpec((tm, tn), functools.partial(lambda i, j, k, off: (i, j + off), off=wide[q][1]))
                   for q in range(no)],
        out_shape=[jax.ShapeDtypeStruct((m, wide[q][0] if q in wide else n_all), dt) for q, dt in enumerate(outs)],
        scratch_shapes=[pltpu.VMEM((tm, tn), F32)] if nk > 1 else [],
        sem=("parallel", "parallel", "arbitrary"),
        args=args, jobs=jobs, prefetch=prefetch, aliases=aliases)
    return (res, jres) if jobs else res


def _ada_fwd(c_all, w_ada, b_ada_cols, name):
    nb, dd = c_all.shape
    ncol = w_ada.shape[1]
    tn = _tile(ncol, 512, LANE)

    def body(c_ref, w_ref, b_ref, o_ref):
        cv = c_ref[...]
        act = (cv * _sigmoid(cv)).astype(BF16)
        o_ref[...] = jnp.dot(act, w_ref[...].astype(BF16), preferred_element_type=F32) + b_ref[...]

    return pl.pallas_call(
        body, name=name, grid=(ncol // tn,),
        in_specs=[pl.BlockSpec((nb, dd), lambda j: (0, 0)), pl.BlockSpec((dd, tn), lambda j: (0, j)),
                  pl.BlockSpec((1, tn), lambda j: (0, j))],
        out_specs=pl.BlockSpec((nb, tn), lambda j: (0, j)),
        out_shape=jax.ShapeDtypeStruct((nb, ncol), F32),
        compiler_params=_params(("parallel",)),
    )(c_all, w_ada, b_ada_cols)


def _ada_bwd(c_all_t, dmod_cols, name):
    dd, nb = c_all_t.shape
    ncol = dmod_cols.shape[1]
    tr = _tile(dd, 256)

    def body(c_ref, d_ref, o_ref):
        cv = c_ref[...]
        act = (cv * _sigmoid(cv)).astype(BF16).astype(F32)
        dm = d_ref[...].astype(BF16).astype(F32)
        acc = act[:, 0:1] * dm[0:1, :]
        for bi in range(1, nb):
            acc = acc + act[:, bi:bi + 1] * dm[bi:bi + 1, :]
        o_ref[...] = acc

    return pl.pallas_call(
        body, name=name, grid=(dd // tr,),
        in_specs=[pl.BlockSpec((tr, nb), lambda i: (i, 0)), pl.BlockSpec((nb, ncol), lambda i: (0, 0))],
        out_specs=pl.BlockSpec((tr, ncol), lambda i: (i, 0)),
        out_shape=jax.ShapeDtypeStruct((dd, ncol), F32),
        compiler_params=_params(("parallel",)),
    )(c_all_t, dmod_cols)


def _norm_mod(x, gain, mod, row0, name):
    t, dd = x.shape
    tt = _tile(t, 512)

    def body(x_ref, g_ref, m_ref, h_ref):
        xv = x_ref[...]
        r = lax.rsqrt(jnp.mean(xv * xv, axis=-1, keepdims=True) + EPS)
        shift, scale = m_ref[row0:row0 + 1, :], m_ref[row0 + 1:row0 + 2, :]
        h_ref[...] = ((xv * r * g_ref[...]) * (1.0 + scale) + shift).astype(BF16)

    return pl.pallas_call(
        body, name=name, grid=(t // tt,),
        in_specs=[pl.BlockSpec((tt, dd), lambda i: (i, 0)), pl.BlockSpec((1, dd), lambda i: (0, 0)),
                  pl.BlockSpec((8, dd), lambda i: (0, 0))],
        out_specs=pl.BlockSpec((tt, dd), lambda i: (i, 0)),
        out_shape=jax.ShapeDtypeStruct((t, dd), BF16),
        compiler_params=_params(("parallel",)),
    )(x, gain, mod)


def _norm_mod_bwd(x, gain, mod, row0, dh, dres, branch, name, jobs=()):
    t, dd = x.shape
    tt = _tile(t, 512)
    has_branch = branch is not None

    def body(*refs):
        if has_branch:
            x_ref, g_ref, m_ref, dh_ref, dres_ref, o_ref, dx_ref, st_ref, do_ref = refs[:-1]
        else:
            x_ref, g_ref, m_ref, dh_ref, dres_ref, dx_ref, st_ref = refs[:-1]

        acc_ref = refs[-1]
        step_i = pl.program_id(0)

        @pl.when(step_i == 0)
        def _():
            acc_ref[...] = jnp.zeros_like(acc_ref)

        def fold(v):
            out = v[0:8]
            for q in range(1, rc // 8):
                out = out + v[8 * q:8 * q + 8]
            return out

        def chunk(ci, carry):
            rows = pl.ds(pl.multiple_of(ci * rc, rc), rc)
            xv = x_ref[rows, :]
            g = g_ref[...]
            r = lax.rsqrt(jnp.mean(xv * xv, axis=-1, keepdims=True) + EPS)
            nv = xv * r
            dhv = dh_ref[rows, :].astype(F32)
            acc_ref[0] += fold(dhv)
            acc_ref[1] += fold(dhv * (nv * g))
            dng = dhv * (1.0 + m_ref[row0 + 1:row0 + 2, :])
            acc_ref[2] += fold(dng * nv)
            dn = dng * g
            dx = dres_ref[rows, :] + r * (dn - nv * jnp.mean(dn * nv, axis=-1, keepdims=True))
            dx_ref[rows, :] = dx
            if has_branch:
                acc_ref[3] += fold(dx * o_ref[rows, :].astype(F32))
                do_ref[rows, :] = (dx * m_ref[branch[1]:branch[1] + 1, :]).astype(BF16)
            return carry

        lax.fori_loop(0, tt // rc, chunk, 0, unroll=4)

        @pl.when(step_i == nsteps - 1)
        def _():
            st_ref[...] = jnp.zeros_like(st_ref)
            for k in range(4):
                st_ref[k:k + 1, :] = jnp.sum(acc_ref[k], axis=0, keepdims=True)

    rc = 16 if tt % 16 == 0 else 8
    nsteps = t // tt
    row = pl.BlockSpec((tt, dd), lambda i: (i, 0))
    in_specs = [row, pl.BlockSpec((1, dd), lambda i: (0, 0)), pl.BlockSpec((8, dd), lambda i: (0, 0)), row, row]
    args = [x, gain, mod, dh, dres]
    out_specs = [row, pl.BlockSpec((8, dd), lambda i: (0, 0))]
    out_shape = [jax.ShapeDtypeStruct((t, dd), F32), jax.ShapeDtypeStruct((8, dd), F32)]
    if has_branch:
        in_specs.append(row)
        args.append(branch[0])
        out_specs.append(row)
        out_shape.append(jax.ShapeDtypeStruct((t, dd), BF16))
    res, jres = _call(body, name=name, grid=(t // tt,), in_specs=in_specs, out_specs=out_specs, out_shape=out_shape,
                      scratch_shapes=[pltpu.VMEM((4, 8, dd), F32)], sem=("arbitrary",), args=args, jobs=jobs)
    return (res, jres) if jobs else res


def _final_loss(x2, o2, final_g, target, mod, name):
    t, dd = x2.shape
    tt = _tile(t, 256)
    nsteps = t // tt

    def body(x_ref, o_ref, g_ref, t_ref, m_ref, loss_ref, dx_ref, do_ref, st_ref, lacc):
        i = pl.program_id(0)

        @pl.when(i == 0)
        def _():
            st_ref[...] = jnp.zeros_like(st_ref)
            lacc[...] = jnp.zeros_like(lacc)

        xv = x_ref[...]
        g = g_ref[...]
        r = lax.rsqrt(jnp.mean(xv * xv, axis=-1, keepdims=True) + EPS)
        nv = xv * r
        err = nv * g - t_ref[...]
        lacc[...] += jnp.sum(err * err, axis=0, keepdims=True)
        dy = err * (1.0 / dd)
        st_ref[0:1, :] += jnp.sum(dy * nv, axis=0, keepdims=True)
        dn = dy * g
        dx = r * (dn - nv * jnp.mean(dn * nv, axis=-1, keepdims=True))
        dx_ref[...] = dx
        st_ref[1:2, :] += jnp.sum(dx * o_ref[...].astype(F32), axis=0, keepdims=True)
        do_ref[...] = (dx * m_ref[5:6, :]).astype(BF16)

        @pl.when(i == nsteps - 1)
        def _():
            total = jnp.sum(lacc[...], axis=-1, keepdims=True) * (0.5 / dd)
            loss_ref[...] = jnp.broadcast_to(total, loss_ref.shape)

    row = pl.BlockSpec((tt, dd), lambda i: (i, 0))
    return pl.pallas_call(
        body, name=name, grid=(nsteps,),
        in_specs=[row, row, pl.BlockSpec((1, dd), lambda i: (0, 0)), row, pl.BlockSpec((8, dd), lambda i: (0, 0))],
        out_specs=[pl.BlockSpec((8, LANE), lambda i: (0, 0)), row, row, pl.BlockSpec((8, dd), lambda i: (0, 0))],
        out_shape=[jax.ShapeDtypeStruct((8, LANE), F32), jax.ShapeDtypeStruct((t, dd), F32),
                   jax.ShapeDtypeStruct((t, dd), BF16), jax.ShapeDtypeStruct((8, dd), F32)],
        scratch_shapes=[pltpu.VMEM((1, dd), F32)],
        compiler_params=_params(("arbitrary",)),
    )(x2, o2, final_g, target, mod)


def _lru_gates(xt, prev8, lp, wa_ref, wx_ref, first_tile, cw):
    tt = xt.shape[0]
    ext = jnp.concatenate([prev8, xt], axis=0)
    xs = [pltpu.roll(ext, s, 0)[8:, :] if s else xt for s in range(CONV_WIDTH)]
    xc = lp[0:1, :] + lp[7:8, :] * xs[0] + lp[6:7, :] * xs[1] + lp[5:6, :] * xs[2] + lp[4:5, :] * xs[3]
    xcb = xc.astype(BF16)
    za, zi = [], []
    for jb in range(cw // LRU_BLOCK_W):
        sl = slice(jb * LRU_BLOCK_W, (jb + 1) * LRU_BLOCK_W)
        za.append(jnp.dot(xcb[:, sl], wa_ref[jb].astype(BF16), preferred_element_type=F32))
        zi.append(jnp.dot(xcb[:, sl], wx_ref[jb].astype(BF16), preferred_element_type=F32))
    ra = _sigmoid(jnp.concatenate(za, axis=1) + lp[1:2, :])
    ri = _sigmoid(jnp.concatenate(zi, axis=1) + lp[2:3, :])
    sp = _softplus(-lp[3:4, :])
    log_a = -LRU_C * ra * sp
    av = jnp.exp(log_a)
    seq_start = jnp.logical_and(first_tile, lax.broadcasted_iota(jnp.int32, (tt, 1), 0) == 0)
    m2 = _neg_expm1(2.0 * log_a, av * av)
    inv_mult = jnp.where(seq_start, 1.0, lax.rsqrt(jnp.maximum(m2, 1e-30)))
    mult = jnp.where(seq_start, 1.0, m2 * inv_mult)
    return xs, xc, xcb, ra, ri, sp, av, mult, inv_mult, seq_start


def _scan_groups(a, u, reverse):
    tt, cw = a.shape
    a3, u3 = a.reshape(tt // 8, 8, cw), u.reshape(tt // 8, 8, cw)
    r = lax.broadcasted_iota(jnp.int32, (1, 8, 1), 1)
    for s in (1, 2, 4):
        shift = 8 - s if reverse else s
        take = r < 8 - s if reverse else r >= s
        u3 = jnp.where(take, a3 * pltpu.roll(u3, shift, 1) + u3, u3)
        a3 = jnp.where(take, a3 * pltpu.roll(a3, shift, 1), a3)
    return a3, u3


def _scan_carry(a_s, u_s, h_s, carry, reverse):
    ng = a_s.shape[0]

    def step(i, h):
        g = ng - 1 - i if reverse else i
        hg = u_s[g] + a_s[g] * h
        h_s[g] = hg
        return hg[0:1, :] if reverse else hg[7:8, :]

    return lax.fori_loop(0, ng, step, carry, unroll=4)


def _lru_fwd(proj, lp, wa, wx, name, jobs=()):
    t = proj.shape[0]
    w = D_MODEL
    cw = LRU_CW
    tt = _tile(t, 256)
    nb = cw // LRU_BLOCK_W

    def body(lx_ref, lg_ref, lp_ref, wa_ref, wx_ref, rec_ref, ya_ref, a_s, u_s, h_s, halo, carry):
        ti = pl.program_id(1)

        @pl.when(ti == 0)
        def _():
            halo[...] = jnp.zeros_like(halo)
            carry[...] = jnp.zeros_like(carry)

        xt = lx_ref[...].astype(F32)
        lp_v = lp_ref[...]
        _, xc, _, _, ri, _, av, mult, _, _ = _lru_gates(xt, halo[...], lp_v, wa_ref, wx_ref, ti == 0, cw)
        halo[...] = xt[tt - 8:, :]
        a_s[...], u_s[...] = _scan_groups(av, mult * (ri * xc), False)
        carry[...] = _scan_carry(a_s, u_s, h_s, carry[...], False)
        rec = h_s[...].reshape(tt, cw)
        rec_ref[...] = rec.astype(BF16)
        gelu, _ = _gelu_tanh(lg_ref[...].astype(F32))
        ya_ref[...] = (rec * gelu).astype(BF16)

    off_g = OFF_LG // cw
    res, jres = _call(
        body, name=name, grid=(w // cw, t // tt),
        in_specs=[pl.BlockSpec((tt, cw), lambda ci, ti: (ti, ci)),
                  pl.BlockSpec((tt, cw), lambda ci, ti: (ti, off_g + ci)),
                  pl.BlockSpec((8, cw), lambda ci, ti: (0, ci)),
                  pl.BlockSpec((nb, LRU_BLOCK_W, LRU_BLOCK_W), lambda ci, ti: (ci, 0, 0)),
                  pl.BlockSpec((nb, LRU_BLOCK_W, LRU_BLOCK_W), lambda ci, ti: (ci, 0, 0))],
        out_specs=[pl.BlockSpec((tt, cw), lambda ci, ti: (ti, ci))] * 2,
        out_shape=[jax.ShapeDtypeStruct((t, w), BF16)] * 2,
        scratch_shapes=[pltpu.VMEM((tt // 8, 8, cw), F32)] * 3 + [pltpu.VMEM((8, cw), F32), pltpu.VMEM((1, cw), F32)],
        sem=("parallel", "arbitrary"), args=[proj, proj, lp, wa, wx], jobs=jobs)
    return (res, jres) if jobs else res


def _lru_bwd(proj, rec, dya, lp, wa, wx, name, jobs=(), into=None):
    t = proj.shape[0]
    w = D_MODEL
    cw = LRU_CW if into is None else w
    tt = _tile(t, 256 if into is None else 128)
    nt = t // tt
    nb = cw // LRU_BLOCK_W
    r8 = tt // 8

    def body(lx_ref, lxp_ref, lg_ref, rec_ref, recp_ref, dya_ref, lp_ref, wa_ref, wx_ref, *rest):
        if into is None:
            dlx_ref, dlg_ref = rest[0], rest[1]
        else:
            dlx_ref, dlg_ref = rest[1].at[:, 0:cw], rest[1].at[:, cw:2 * cw]
        st_ref, dwa_ref, dwx_ref, a_s, d_s, dh_s, nhalo, carry = rest[-8:]
        step_i = pl.program_id(1)
        ti = nt - 1 - step_i

        @pl.when(step_i == 0)
        def _():
            st_ref[...] = jnp.zeros_like(st_ref)
            dwa_ref[...] = jnp.zeros_like(dwa_ref)
            dwx_ref[...] = jnp.zeros_like(dwx_ref)
            nhalo[...] = jnp.zeros_like(nhalo)
            carry[...] = jnp.zeros_like(carry)

        first = ti == 0
        keep = jnp.where(first, 0.0, 1.0)
        xt = lx_ref[...].astype(F32)
        prev8 = lxp_ref[...].astype(F32) * keep
        lp_v = lp_ref[...]
        xs, xc, xcb, ra, ri, sp, av, mult, inv_mult, seq_start = _lru_gates(xt, prev8, lp_v, wa_ref, wx_ref, first, cw)

        lg = lg_ref[...].astype(F32)
        gelu, th = _gelu_tanh(lg)
        dyav = dya_ref[...].astype(F32)
        recv = rec_ref[...].astype(F32)
        dlg_ref[...] = (dyav * recv * _gelu_tanh_grad(lg, th)).astype(BF16)

        drec = dyav * gelu
        e_in = carry[...]
        a_s[...], d_s[...] = _scan_groups(av, av * drec, True)
        carry[...] = _scan_carry(a_s, d_s, dh_s, e_in, True)
        e_next = jnp.concatenate([dh_s[...].reshape(tt, cw), jnp.broadcast_to(e_in, (8, cw))], axis=0)
        dh = drec + pltpu.roll(e_next, tt + 7, 0)[:tt, :]
        rprev8 = recp_ref[...].astype(F32) * keep
        hprev = pltpu.roll(jnp.concatenate([rprev8, recv], axis=0), 1, 0)[8:, :]
        da = dh * hprev
        dmult = jnp.where(seq_start, 0.0, dh * ri * xc)
        dri = dh * mult * xc
        dxc = dh * mult * ri
        dlog_a = da * av - dmult * (av * av) * inv_mult
        dra = dlog_a * (-LRU_C * sp)
        lam = lp_v[3:4, :]
        st_ref[3:4, :] += jnp.sum(dlog_a * ra, axis=0, keepdims=True) * (LRU_C * _sigmoid(-lam))
        dza = dra * ra * (1.0 - ra)
        dzi = dri * ri * (1.0 - ri)
        st_ref[1:2, :] += jnp.sum(dza, axis=0, keepdims=True)
        st_ref[2:3, :] += jnp.sum(dzi, axis=0, keepdims=True)
        dzab, dzib = dza.astype(BF16), dzi.astype(BF16)
        back = []
        for jb in range(nb):
            sl = slice(jb * LRU_BLOCK_W, (jb + 1) * LRU_BLOCK_W)
            dwa_ref[jb] += lax.dot_general(xcb[:, sl], dzab[:, sl], (_DIMS["tn"], ((), ())), preferred_element_type=F32)
            dwx_ref[jb] += lax.dot_general(xcb[:, sl], dzib[:, sl], (_DIMS["tn"], ((), ())), preferred_element_type=F32)
            back.append(
                lax.dot_general(dzab[:, sl], wa_ref[jb].astype(BF16), (_DIMS["nt"], ((), ())), preferred_element_type=F32)
                + lax.dot_general(dzib[:, sl], wx_ref[jb].astype(BF16), (_DIMS["nt"], ((), ())), preferred_element_type=F32))
        dxc = dxc + jnp.concatenate(back, axis=1)
        st_ref[0:1, :] += jnp.sum(dxc, axis=0, keepdims=True)
        for k in range(CONV_WIDTH):
            st_ref[4 + k:5 + k, :] += jnp.sum(dxc * xs[CONV_WIDTH - 1 - k], axis=0, keepdims=True)
        ext = jnp.concatenate([dxc, nhalo[...]], axis=0)
        dlx = lp_v[7:8, :] * dxc
        for s in range(1, CONV_WIDTH):
            dlx = dlx + lp_v[7 - s:8 - s, :] * pltpu.roll(ext, tt + 8 - s, 0)[:tt, :]
        dlx_ref[...] = dlx.astype(BF16)
        nhalo[...] = dxc[0:8, :]

    off_g = OFF_LG // cw
    tile = lambda off: pl.BlockSpec((tt, cw), lambda ci, si: (nt - 1 - si, off + ci))
    prev = lambda off: pl.BlockSpec((8, cw), lambda ci, si: (jnp.maximum((nt - 1 - si) * r8 - 1, 0), off + ci))
    wspec = pl.BlockSpec((nb, LRU_BLOCK_W, LRU_BLOCK_W), lambda ci, si: (ci, 0, 0))
    st_spec = pl.BlockSpec((8, cw), lambda ci, si: (0, ci))
    in_specs = [tile(0), prev(0), tile(off_g), tile(0), prev(0), tile(0), st_spec, wspec, wspec]
    args = [proj, proj, proj, rec, rec, dya, lp, wa, wx]
    if into is None:
        d_specs = [tile(0), tile(0)]
        d_shapes = [jax.ShapeDtypeStruct((t, w), BF16), jax.ShapeDtypeStruct((t, w), BF16)]
        aliases = None
    else:
        d_specs = [pl.BlockSpec((tt, 2 * cw), lambda ci, si: (nt - 1 - si, 0))]
        d_shapes = [jax.ShapeDtypeStruct(into.shape, BF16)]
        aliases = {len(args): 0}
        args.append(into)
        in_specs.append(pl.BlockSpec(memory_space=pl.ANY))
    res, jres = _call(
        body, name=name, grid=(w // cw, nt), in_specs=in_specs,
        out_specs=d_specs + [st_spec, wspec, wspec],
        out_shape=d_shapes + [jax.ShapeDtypeStruct((8, w), F32),
                              jax.ShapeDtypeStruct((LRU_BLOCKS, LRU_BLOCK_W, LRU_BLOCK_W), F32),
                              jax.ShapeDtypeStruct((LRU_BLOCKS, LRU_BLOCK_W, LRU_BLOCK_W), F32)],
        scratch_shapes=[pltpu.VMEM((tt // 8, 8, cw), F32)] * 3 + [pltpu.VMEM((8, cw), F32), pltpu.VMEM((1, cw), F32)],
        sem=("parallel", "arbitrary"), args=args, jobs=jobs, aliases=aliases)
    return (res, jres) if jobs else res


def _band_valid(first_block):
    qi = lax.broadcasted_iota(jnp.int32, (BLOCK, 2 * BLOCK), 0)
    ki = lax.broadcasted_iota(jnp.int32, (BLOCK, 2 * BLOCK), 1)
    rel = qi + BLOCK - ki
    valid = jnp.logical_and(rel >= 0, rel < BLOCK)
    return jnp.logical_and(valid, jnp.logical_or(ki >= BLOCK, jnp.logical_not(first_block)))


ATTN_STACK_BWD = 8


def _low_half():
    return lax.broadcasted_iota(jnp.int32, (1, LANE), 1) < HEAD_DIM


def _stack_heads(x, h0, ns):
    low = _low_half()
    parts = []
    for g in range(ns):
        h = h0 + g
        slab = x[:, (h // 2) * LANE:(h // 2 + 1) * LANE]
        parts.append(jnp.where(low if h % 2 == 0 else jnp.logical_not(low), slab, jnp.zeros_like(slab)))
    return jnp.concatenate(parts, axis=0)


def _unstack_heads(y, ns):
    low = _low_half()
    pairs = [jnp.where(low, y[(2 * j) * BLOCK:(2 * j + 1) * BLOCK], y[(2 * j + 1) * BLOCK:(2 * j + 2) * BLOCK])
             for j in range(ns // 2)]
    return pairs[0] if len(pairs) == 1 else jnp.concatenate(pairs, axis=1)


def _dup_kv(xall, kv):
    part = xall[:, kv * HEAD_DIM:(kv + 1) * HEAD_DIM]
    return jnp.concatenate([part, part], axis=1)


def _fold_halves(a):
    return a + pltpu.roll(a, HEAD_DIM, 1)


def _group_probs(qg, k2, bias_g, sink_g, lse_g, valid):
    s = lax.dot_general(qg, k2, (_DIMS["nt"], ((), ())), preferred_element_type=F32) * (HEAD_DIM ** -0.5)
    s = jnp.where(valid[None], s.reshape(bias_g.shape) + bias_g, NEG_INF)
    return jnp.exp(s - lse_g), jnp.exp(sink_g - lse_g)


def _group_sinks(s_ref, h0, ns):
    return jnp.concatenate([jnp.full((1, BLOCK, 1), s_ref[h0 + g], F32) for g in range(ns)], axis=0)


def _attn_specs(t):
    nblk = t // BLOCK
    q_spec = pl.BlockSpec((BLOCK, D_MODEL), lambda n: (n, OFF_Q // D_MODEL))
    kc = pl.BlockSpec((BLOCK, KV_WIDTH), lambda n: (n, OFF_K // KV_WIDTH))
    kp = pl.BlockSpec((BLOCK, KV_WIDTH), lambda n: (jnp.maximum(n - 1, 0), OFF_K // KV_WIDTH))
    vc = pl.BlockSpec((BLOCK, KV_WIDTH), lambda n: (n, OFF_V // KV_WIDTH))
    vp = pl.BlockSpec((BLOCK, KV_WIDTH), lambda n: (jnp.maximum(n - 1, 0), OFF_V // KV_WIDTH))
    bias_spec = pl.BlockSpec((N_Q_HEADS, BLOCK, 2 * BLOCK), lambda n: (0, 0, 0))
    sink_spec = pl.BlockSpec(memory_space=pltpu.SMEM)
    return nblk, [q_spec, kp, kc, vp, vc, bias_spec, sink_spec]


def _attn_fwd(proj, bias, sinks, name, jobs=()):
    t = proj.shape[0]
    nblk, in_specs = _attn_specs(t)

    def body(q_ref, kp_ref, kc_ref, vp_ref, vc_ref, b_ref, s_ref, o_ref, lse_ref):
        valid = _band_valid(pl.program_id(0) == 0)
        lane = lax.broadcasted_iota(jnp.int32, (1, LANE), 1)
        q = q_ref[...]
        kall = jnp.concatenate([kp_ref[...], kc_ref[...]], axis=0)
        vall = jnp.concatenate([vp_ref[...], vc_ref[...]], axis=0)
        outs = []
        lse = jnp.zeros((BLOCK, LANE), F32)
        for h in range(N_Q_HEADS):
            kv = h // GROUP
            kk = kall[:, kv * HEAD_DIM:(kv + 1) * HEAD_DIM]
            vv = vall[:, kv * HEAD_DIM:(kv + 1) * HEAD_DIM]
            s = lax.dot_general(q[:, h * HEAD_DIM:(h + 1) * HEAD_DIM], kk, (_DIMS["nt"], ((), ())),
                                preferred_element_type=F32) * (HEAD_DIM ** -0.5)
            s = jnp.where(valid, s + b_ref[h], NEG_INF)
            sink = s_ref[h]
            m = jnp.maximum(jnp.max(s, axis=-1, keepdims=True), sink)
            e = jnp.exp(s - m)
            den = jnp.sum(e, axis=-1, keepdims=True) + jnp.exp(sink - m)
            lse = jnp.where(lane == h, m + jnp.log(den), lse)
            outs.append(jnp.dot((e * (1.0 / den)).astype(BF16), vv, preferred_element_type=F32))
        o_ref[...] = jnp.concatenate(outs, axis=1).astype(BF16)
        lse_ref[...] = lse

    res, jres = _call(
        body, name=name, grid=(nblk,), in_specs=in_specs,
        out_specs=[pl.BlockSpec((BLOCK, D_MODEL), lambda n: (n, 0)), pl.BlockSpec((BLOCK, LANE), lambda n: (n, 0))],
        out_shape=[jax.ShapeDtypeStruct((t, D_MODEL), BF16), jax.ShapeDtypeStruct((t, LANE), F32)],
        sem=("parallel",), args=[proj, proj, proj, proj, proj, bias, sinks], jobs=jobs)
    return (res, jres) if jobs else res


def _attn_bwd(proj, datt, lse, bias, sinks, name, jobs=(), into=None):
    t = proj.shape[0]
    nblk, in_specs = _attn_specs(t)
    in_specs = in_specs + [pl.BlockSpec((BLOCK, D_MODEL), lambda n: (n, 0)), pl.BlockSpec((BLOCK, LANE), lambda n: (n, 0))]
    scale = HEAD_DIM ** -0.5
    args = [proj, proj, proj, proj, proj, bias, sinks, datt, lse]
    dq_spec = pl.BlockSpec((BLOCK, D_MODEL), lambda n: (n, 0))
    dq_shape = jax.ShapeDtypeStruct((t, D_MODEL), BF16)
    aliases = None
    if into is not None:
        aliases = {len(args): 0}
        args.append(into)
        in_specs.append(pl.BlockSpec(memory_space=pl.ANY))
        dq_spec = pl.BlockSpec((BLOCK, D_MODEL), lambda n: (n, OFF_Q // D_MODEL))
        dq_shape = jax.ShapeDtypeStruct(into.shape, BF16)

    def body(q_ref, kp_ref, kc_ref, vp_ref, vc_ref, b_ref, s_ref, do_ref, lse_ref, *rest):
        dq_ref, dk_ref, dv_ref, db_ref, ds_ref = rest[-5:]
        n = pl.program_id(0)

        @pl.when(n == 0)
        def _():
            dk_ref[...] = jnp.zeros_like(dk_ref)
            dv_ref[...] = jnp.zeros_like(dv_ref)
            db_ref[...] = jnp.zeros_like(db_ref)
            ds_ref[...] = jnp.zeros_like(ds_ref)

        valid = _band_valid(n == 0)
        lane = lax.broadcasted_iota(jnp.int32, (1, LANE), 1)
        q = q_ref[...]
        do = do_ref[...]
        lse_all = lse_ref[...]
        kall = jnp.concatenate([kp_ref[...], kc_ref[...]], axis=0)
        vall = jnp.concatenate([vp_ref[...], vc_ref[...]], axis=0)
        low = _low_half()
        dqs, dks, dvs = [], [], []
        dsink_row = jnp.zeros((1, LANE), F32)
        ns = ATTN_STACK_BWD
        for kv in range(N_KV_HEADS):
            k2, v2 = _dup_kv(kall, kv), _dup_kv(vall, kv)
            dk_acc = jnp.zeros((2 * BLOCK, LANE), F32)
            dv_acc = jnp.zeros((2 * BLOCK, LANE), F32)
            for h0 in range(kv * GROUP, (kv + 1) * GROUP, ns):
                qg, dog = _stack_heads(q, h0, ns), _stack_heads(do, h0, ns)
                lse_g = jnp.concatenate([lse_all[:, h0 + g:h0 + g + 1][None] for g in range(ns)], axis=0)
                p, psink = _group_probs(qg, k2, b_ref[h0:h0 + ns], _group_sinks(s_ref, h0, ns), lse_g, valid)
                dp = lax.dot_general(dog, v2, (_DIMS["nt"], ((), ())), preferred_element_type=F32)
                dp = dp.reshape(ns, BLOCK, 2 * BLOCK)
                delta = jnp.sum(p * dp, axis=-1, keepdims=True)
                ds = p * (dp - delta)
                db_ref[h0:h0 + ns] += ds
                dsink = -jnp.sum(psink * delta, axis=1, keepdims=True)
                for g in range(ns):
                    dsink_row = dsink_row + jnp.where(lane == h0 + g, dsink[g], 0.0)
                dsb = ds.reshape(ns * BLOCK, 2 * BLOCK).astype(BF16)
                pb = p.reshape(ns * BLOCK, 2 * BLOCK).astype(BF16)
                dqs.append(_unstack_heads(jnp.dot(dsb, k2, preferred_element_type=F32) * scale, ns))
                dk_acc = dk_acc + lax.dot_general(dsb, qg, (_DIMS["tn"], ((), ())), preferred_element_type=F32)
                dv_acc = dv_acc + lax.dot_general(pb, dog, (_DIMS["tn"], ((), ())), preferred_element_type=F32)
            dks.append(_fold_halves(dk_acc) * scale)
            dvs.append(_fold_halves(dv_acc))
        dq_ref[...] = jnp.concatenate(dqs, axis=1).astype(BF16)
        rows = pl.ds(pl.multiple_of(n * BLOCK, BLOCK), 2 * BLOCK)
        dk_ref[rows, :] += jnp.concatenate([jnp.where(low, dks[0], dks[1]), jnp.where(low, dks[2], dks[3])], axis=1)
        dv_ref[rows, :] += jnp.concatenate([jnp.where(low, dvs[0], dvs[1]), jnp.where(low, dvs[2], dvs[3])], axis=1)
        ds_ref[0:1, :] += dsink_row

    res, jres = _call(
        body, name=name, grid=(nblk,), in_specs=in_specs,
        out_specs=[dq_spec,
                   pl.BlockSpec((t + BLOCK, KV_WIDTH), lambda n: (0, 0)),
                   pl.BlockSpec((t + BLOCK, KV_WIDTH), lambda n: (0, 0)),
                   pl.BlockSpec((N_Q_HEADS, BLOCK, 2 * BLOCK), lambda n: (0, 0, 0)),
                   pl.BlockSpec((8, LANE), lambda n: (0, 0))],
        out_shape=[dq_shape,
                   jax.ShapeDtypeStruct((t + BLOCK, KV_WIDTH), F32),
                   jax.ShapeDtypeStruct((t + BLOCK, KV_WIDTH), F32),
                   jax.ShapeDtypeStruct((N_Q_HEADS, BLOCK, 2 * BLOCK), F32),
                   jax.ShapeDtypeStruct((8, LANE), F32)],
        sem=("arbitrary",), args=args, jobs=jobs, aliases=aliases)
    return (res, jres) if jobs else res


def _bucket_table():
    qi = np.arange(BLOCK)[:, None]
    ki = np.arange(2 * BLOCK)[None, :]
    rel = np.maximum(qi + BLOCK - ki, 0)
    max_exact = N_BUCKETS // 2
    relf = np.maximum(rel, 1).astype(np.float32)
    large = max_exact + (np.log(relf / max_exact) / math.log(MAX_DISTANCE / max_exact)
                         * (N_BUCKETS - max_exact)).astype(np.int32)
    large = np.minimum(large, N_BUCKETS - 1)
    return np.where(rel < max_exact, rel, large).astype(np.int32)


def _rel_bias_fwd(rel_bias, onehot, name):
    n = onehot.shape[1]
    tn = _tile(n, 8192, LANE)

    def body(r_ref, o_ref, out_ref):
        out_ref[...] = lax.dot_general(r_ref[...], o_ref[...], (_DIMS["tn"], ((), ())),
                                       precision=lax.Precision.HIGHEST, preferred_element_type=F32)

    return pl.pallas_call(
        body, name=name, grid=(n // tn,),
        in_specs=[pl.BlockSpec(rel_bias.shape, lambda j: (0, 0)), pl.BlockSpec((N_BUCKETS, tn), lambda j: (0, j))],
        out_specs=pl.BlockSpec((N_Q_HEADS, tn), lambda j: (0, j)),
        out_shape=jax.ShapeDtypeStruct((N_Q_HEADS, n), F32),
        compiler_params=_params(("parallel",)),
    )(rel_bias, onehot)


def _rel_bias_bwd(dbias, onehot, name):
    def body(d_ref, o_ref, out_ref):
        out_ref[...] = lax.dot_general(d_ref[...], o_ref[...], (_DIMS["nt"], ((), ())),
                                       precision=lax.Precision.HIGHEST, preferred_element_type=F32)

    full = pl.BlockSpec(dbias.shape, lambda: (0, 0))
    return pl.pallas_call(
        body, name=name, in_specs=[full, full],
        out_specs=pl.BlockSpec((N_Q_HEADS, N_BUCKETS), lambda: (0, 0)),
        out_shape=jax.ShapeDtypeStruct((N_Q_HEADS, N_BUCKETS), F32),
        compiler_params=_params(()),
    )(dbias, onehot)


def _part_specs(gparts, part_index, tr, cc):
    blk = pl.BlockSpec((tr, cc), lambda i, *_: (i, 0))
    specs = []
    for i, _ in enumerate(gparts):
        if i in part_index:
            ix = part_index[i]
            if isinstance(ix, int):
                specs.append(pl.BlockSpec((None, tr, cc), functools.partial(lambda i, *_, ix: (ix, i, 0), ix=ix)))
            else:
                specs.append(pl.BlockSpec((None, tr, cc), lambda i, ix_ref: (ix_ref[0], i, 0)))
        else:
            specs.append(blk)
    return blk, specs


def _sum_parts(gparts, part_index, name):
    r, cc = gparts[0].shape[-2:]
    tr = _tile(r, 256)
    dyn = [i for i, ix in part_index.items() if not isinstance(ix, int)]
    assert len(dyn) <= 1

    def body(*refs):
        refs = refs[len(dyn):]
        g = refs[0][...].astype(F32)
        for gr in refs[1:-1]:
            g = g + gr[...].astype(F32)
        refs[-1][...] = g

    blk, specs = _part_specs(gparts, part_index, tr, cc)
    grid_spec = pltpu.PrefetchScalarGridSpec(num_scalar_prefetch=len(dyn), grid=(r // tr,), in_specs=specs, out_specs=blk)
    return pl.pallas_call(
        body, name=name, grid_spec=grid_spec, out_shape=jax.ShapeDtypeStruct((r, cc), F32),
        compiler_params=_params(("parallel",)),
    )(*[part_index[i] for i in dyn], *gparts)


def _adamw(w, m, v, gparts, name, part_index=None, jobs=()):
    r, cc = w.shape
    tr = _tile(r, 128)
    np_ = len(gparts)
    part_index = part_index or {}
    dyn = [i for i, ix in part_index.items() if not isinstance(ix, int)]
    assert len(dyn) <= 1
    bc1 = 1.0 / (1.0 - ADAM_B1 ** ADAM_STEP)
    bc2 = 1.0 / (1.0 - ADAM_B2 ** ADAM_STEP)

    def body(*refs):
        refs = refs[len(dyn):]
        w_ref, m_ref, v_ref = refs[:3]
        g_refs = refs[3:3 + np_]
        g_out, d_out, m_out, v_out = refs[3 + np_:]
        g = g_refs[0][...].astype(F32)
        for gr in g_refs[1:]:
            g = g + gr[...].astype(F32)
        mn = ADAM_B1 * m_ref[...] + (1.0 - ADAM_B1) * g
        vn = ADAM_B2 * v_ref[...] + (1.0 - ADAM_B2) * (g * g)
        g_out[...] = g
        m_out[...] = mn
        v_out[...] = vn
        d_out[...] = -ADAM_LR * ((mn * bc1) / (jnp.sqrt(vn * bc2) + ADAM_EPS) + ADAM_WD * w_ref[...])

    blk, g_specs = _part_specs(gparts, part_index, tr, cc)
    if jobs:
        assert not dyn
        return _call(body, name=name, grid=(r // tr,), in_specs=[blk, blk, blk] + g_specs, out_specs=[blk] * 4,
                     out_shape=[jax.ShapeDtypeStruct((r, cc), F32)] * 4, sem=("arbitrary",),
                     args=[w, m, v, *gparts], jobs=jobs)
    grid_spec = pltpu.PrefetchScalarGridSpec(
        num_scalar_prefetch=len(dyn), grid=(r // tr,),
        in_specs=[blk, blk, blk] + g_specs, out_specs=[blk] * 4)
    return pl.pallas_call(
        body, name=name, grid_spec=grid_spec,
        out_shape=[jax.ShapeDtypeStruct((r, cc), F32)] * 4,
        compiler_params=_params(("parallel",)),
    )(*[part_index[i] for i in dyn], w, m, v, *gparts)


def _row(vec):
    return vec.reshape(1, -1)


def kernel(x, c, w_ada, b_ada, norm1_g, w_in, conv_w, conv_b, lru_wa, lru_ba, lru_wx, lru_bx, lru_lambda, w_lru_out, w_attn_out, attn_sinks, rel_bias, w_out, norm2_g, w_ff1, w_ff2, final_g, loss_target, m_w_ada, m_b_ada, m_norm1_g, m_w_in, m_conv_w, m_conv_b, m_lru_wa, m_lru_ba, m_lru_wx, m_lru_bx, m_lru_lambda, m_w_lru_out, m_w_attn_out, m_attn_sinks, m_rel_bias, m_w_out, m_norm2_g, m_w_ff1, m_w_ff2, m_final_g, v_w_ada, v_b_ada, v_norm1_g, v_w_in, v_conv_w, v_conv_b, v_lru_wa, v_lru_ba, v_lru_wx, v_lru_bx, v_lru_lambda, v_w_lru_out, v_w_attn_out, v_attn_sinks, v_rel_bias, v_w_out, v_norm2_g, v_w_ff1, v_w_ff2, v_final_g):
    dd = D_MODEL
    t = x.shape[1]
    ix, iy, ic = lax.axis_index("x"), lax.axis_index("y"), lax.axis_index("c")
    me = 4 * ix + 2 * iy + ic
    chip = 2 * ix + iy
    c_idx = jnp.reshape(ic, (1,)).astype(jnp.int32)
    chip_idx = jnp.reshape(chip, (1,)).astype(jnp.int32)

    xs = x[0]
    target = loss_target[0]
    ncol_ada = w_ada.shape[2]
    cshard = conv_w.shape[2]

    shards = [jnp.transpose(w_in[0]).astype(BF16), w_lru_out[0].astype(BF16), w_attn_out[0].astype(BF16),
              w_out[0].astype(BF16), jnp.transpose(w_ff1[0]).astype(BF16), w_ff2[0].astype(BF16)]
    s_in, s_lo, s_ao, s_o, s_f1, s_f2 = shards

    def full(g):
        return g.reshape(N_DEV * g.shape[1], dd)

    def stack(g):
        return g.reshape(N_DEV, g.shape[0] // N_DEV, dd)

    (w_in_pair,) = _run_job(_pair_job([s_in]), "ag_w_in_pair")
    w_in_pair = lax.dynamic_update_slice_in_dim(w_in_pair, s_in[None], ic, axis=0)

    pack0 = jnp.zeros((8, dd), F32).at[0:1, :].set(c).at[1:1 + CONV_WIDTH, 0:cshard].set(conv_w[0])
    (g0,) = _all_gather([pack0], "ag_cond")
    c_all = g0[:, 0, :]
    conv_w_full = jnp.transpose(g0[:, 1:1 + CONV_WIDTH, 0:cshard], (1, 0, 2)).reshape(CONV_WIDTH, dd)
    b_cols = lax.dynamic_slice(b_ada, (0, me * ncol_ada), (1, ncol_ada))
    mod_cols = _ada_fwd(c_all, w_ada[0], b_cols, "ada_fwd")
    (g1,) = _all_gather([mod_cols], "ag_mod")
    mod_mine = lax.dynamic_index_in_dim(g1, me, axis=1, keepdims=False).reshape(6, dd)
    mod = jnp.concatenate([mod_mine, jnp.zeros((2, dd), F32)], axis=0)

    bucket = _bucket_table()
    onehot = jnp.asarray((bucket.reshape(1, -1) == np.arange(N_BUCKETS)[:, None]).astype(np.float32))
    bias = _rel_bias_fwd(rel_bias, onehot, "rel_bias_fwd").reshape(N_Q_HEADS, BLOCK, 2 * BLOCK)
    sinks = attn_sinks.reshape(N_Q_HEADS)
    lp = jnp.concatenate([conv_b, lru_ba, lru_bx, lru_lambda, conv_w_full], axis=0)
    wa, wx = lru_wa[0], lru_wx[0]

    tm = _tile(t, 1024)
    tn = 512
    h = _norm_mod(xs, norm1_g, mod, 0, "norm1")
    tn2 = 1024
    chunk_w = 2 * s_in.shape[0]

    def ids(*ks):
        return jnp.stack([jnp.asarray(k, jnp.int32) for k in ks])

    tn_in = _tile(chunk_w, 1024, LANE)
    (proj,), ((g_near,),) = _mm(
        h, w_in_pair.reshape(chunk_w, dd), "nt", tm=tm, tn=tn_in, outs=[BF16], name="in_proj_own",
        chunks=(ids(chip), chunk_w, ids(0)), n_total=IN_WIDTH, jobs=[_gather_job([w_in_pair], (0.7,), "near")])
    near_ids = ids(2 * (1 - ix) + iy, 2 * ix + 1 - iy)
    (proj,), ((g_far,), (g_o,)) = _mm(
        h, full(g_near), "nt", tm=tm, tn=tn_in, outs=[BF16], name="in_proj_near",
        chunks=(near_ids, chunk_w, near_ids), into=proj,
        jobs=[_gather_job([g_near], (0.6,), "far"), _gather_job([s_o], (0.6, 0.85))])
    far_chip = 2 * (1 - ix) + 1 - iy
    (proj,), ((g_lo,),) = _mm(
        h, full(g_far), "nt", tm=tm, tn=tn_in, outs=[BF16], name="in_proj_far",
        chunks=(ids(far_chip), chunk_w, ids(far_chip)), into=proj, jobs=[_gather_job([s_lo], (0.5, 0.8))])
    far_slots = lax.dynamic_slice_in_dim(g_far, 2 * far_chip, 2, axis=0)
    win_t = lax.dynamic_update_slice_in_dim(g_near, far_slots, 2 * far_chip, axis=0)
    win_t = full(lax.dynamic_update_slice_in_dim(win_t, w_in_pair, 2 * chip, axis=0))
    (g_o,) = _fill_own([g_o], [s_o])
    (rec, ya), ((g_f1,),) = _lru_fwd(proj, lp, wa, wx, "lru_fwd", jobs=[_gather_job([s_f1], (0.6, 0.9))])
    (att, lse), ((g_ao,),) = _attn_fwd(proj, bias, sinks, "attn_fwd", jobs=[_gather_job([s_ao], (0.4, 0.7))])
    g_lo, g_ao, g_f1 = _fill_own([g_lo, g_ao, g_f1], [s_lo, s_ao, s_f1])
    wlo, wao, wo, wf1_t = full(g_lo), full(g_ao), full(g_o), full(g_f1)
    (y_a,) = _mm(ya, wlo, "nn", tm=tm, tn=tn2, outs=[BF16], name="lru_out")

    def merge_epi(acc, ya_t, ga_t, gb_t):
        return acc, _sigmoid(ga_t.astype(F32)) * ya_t.astype(F32) + _sigmoid(gb_t.astype(F32)) * acc

    y_b, merged = _mm(att, wao, "nn", tm=tm, tn=tn, outs=[BF16, BF16], epi=merge_epi,
                      tiles=[(y_a, 0), (proj, OFF_GA // tn), (proj, OFF_GB // tn)], name="attn_out_merge")

    def resid_epi(row):
        def epi(acc, x_t, mod_t):
            return acc, x_t + mod_t[row:row + 1, :] * acc
        return epi

    o1, x1 = _mm(merged, wo, "nn", tm=tm, tn=tn2, outs=[BF16, F32], epi=resid_epi(2),
                 tiles=[(xs, 0)], rows=[mod], name="out_proj")
    h2 = _norm_mod(x1, norm2_g, mod, 3, "norm2")

    def relu2_epi(acc):
        rl = jnp.maximum(acc, 0.0)
        return acc, rl * rl

    (f_pre, ff), ((g_f2,),) = _mm(h2, wf1_t, "nt", tm=tm, tn=1024, outs=[BF16, BF16], epi=relu2_epi, name="ff1",
                                  jobs=[_gather_job([s_f2], (0.6, 0.9))])
    wf2 = full(_fill_own([g_f2], [s_f2])[0])
    tmh = _tile(t, 512)
    o2, x2 = _mm(ff, wf2, "nn", tm=tmh, tn=tn, outs=[BF16, F32], epi=resid_epi(5),
                 tiles=[(x1, 0)], rows=[mod], name="ff2")

    loss_blk, dx2, do2, st_f = _final_loss(x2, o2, _row(final_g), target, mod, "final_loss")
    loss = lax.psum(loss_blk[0, 0], ("x", "y", "c"))

    def drelu2_epi(acc, f_t):
        return (acc * (2.0 * jnp.maximum(f_t.astype(F32), 0.0)),)

    tmw, tnw, tkw = 512, dd, _tile(t, 2048)
    (df,) = _mm(do2, wf2, "nt", tm=tm, tn=1024, outs=[BF16], epi=drelu2_epi, tiles=[(f_pre, 0)], name="ff2_dx")
    (dwf2,) = _mm(ff, do2, "tn", tm=tmw, tn=tnw, tk=tkw, outs=[BF16], name="ff2_dw")
    (dh2,), ((sib_f2,),) = _mm(df, wf1_t, "nn", tm=tmh, tn=tn, outs=[BF16], name="ff1_dx",
                              jobs=[_sibling_job([stack(dwf2)])])
    pair_f2 = _pair_add(stack(dwf2), sib_f2, c_idx, "rs_pair_add_w_ff2")
    (dwf1_t,), ((chips_f2,),) = _mm(df, h2, "tn", tm=tmw, tn=tnw, tk=tkw, outs=[BF16], name="ff1_dw",
                                    jobs=[_chips_job([pair_f2])])
    (dx1, st_2, do1), ((sib_f1,),) = _norm_mod_bwd(x1, norm2_g, mod, 3, dh2, dx2, (o1, 2), "norm2_bwd",
                                                   jobs=[_sibling_job([stack(dwf1_t)])])
    pair_f1 = _pair_add(stack(dwf1_t), sib_f1, c_idx, "rs_pair_add_w_ff1")

    def dmerge_epi(acc, ga_t, gb_t, ya_t, yb_t):
        sa, sb = _sigmoid(ga_t.astype(F32)), _sigmoid(gb_t.astype(F32))
        return (acc * sa, acc * sb, acc * ya_t.astype(F32) * sa * (1.0 - sa), acc * yb_t.astype(F32) * sb * (1.0 - sb))

    tmh = _tile(t, 512)
    dy_a, dy_b, dproj, dgb = _mm(do1, wo, "nt", tm=tm, tn=tn, outs=[BF16] * 4, epi=dmerge_epi,
                                 tiles=[(proj, OFF_GA // tn), (proj, OFF_GB // tn), (y_a, 0), (y_b, 0)],
                                 wide={2: (IN_WIDTH, OFF_GA // tn)}, name="out_proj_dx")
    (dwo,) = _mm(merged, do1, "tn", tm=tmw, tn=tnw, tk=tkw, outs=[BF16], name="out_proj_dw")
    (datt,) = _mm(dy_b, wao, "nt", tm=tm, tn=tn2, outs=[BF16], name="attn_out_dx")
    (dwao,) = _mm(att, dy_b, "tn", tm=tmw, tn=tnw, tk=tkw, outs=[BF16], name="attn_out_dw")
    (dya,) = _mm(dy_a, wlo, "nt", tm=tm, tn=tn2, outs=[BF16], name="lru_out_dx")
    (dwlo,) = _mm(ya, dy_a, "tn", tm=tmw, tn=tnw, tk=tkw, outs=[BF16], name="lru_out_dw")
    (dproj, dk_pad, dv_pad, dbias, dsinks), (sib_3, (chips_f1,)) = _attn_bwd(
        proj, datt, lse, bias, sinks, "attn_bwd", into=dproj,
        jobs=[_sibling_job([stack(dwlo), stack(dwao), stack(dwo)]), _chips_job([pair_f1])])
    pair_lo, pair_ao, pair_o = [_pair_add(stack(g_), s_, c_idx, "rs_pair_add_" + nm_)
                                for g_, s_, nm_ in zip((dwlo, dwao, dwo), sib_3, ("w_lru_out", "w_attn_out", "w_out"))]
    (dproj, st_l, dwa, dwx), ((chips_lo, chips_ao, chips_o),) = _lru_bwd(
        proj, rec, dya, lp, wa, wx, "lru_bwd", into=dproj, jobs=[_chips_job([pair_lo, pair_ao, pair_o])])
    dkv = jnp.concatenate([dk_pad[BLOCK:].astype(BF16), dv_pad[BLOCK:].astype(BF16)], axis=1)
    dproj = lax.dynamic_update_slice_in_dim(dproj, dkv, OFF_K, axis=1)
    dproj = lax.dynamic_update_slice_in_dim(dproj, dgb, OFF_GB, axis=1)
    lru_mats = jnp.concatenate([dwa.reshape(-1, dd), dwx.reshape(-1, dd)], axis=0)
    (dwin_t,), ((mats_all,),) = _mm(dproj, h, "tn", tm=768, tn=tnw, tk=tkw, outs=[BF16], name="in_proj_dw",
                                   jobs=[_gather_job([lru_mats], (0.3, 0.6))])
    (mats_all,) = _fill_own([mats_all], [lru_mats])
    (sib_in,) = _exchange_sibling([stack(dwin_t)], "rs_sibling_w_in")
    pair_in = _pair_add(stack(dwin_t), sib_in, c_idx, "rs_pair_add_w_in", widths=(896, 896, 256))
    n_rb = t // tm
    dx_tiles = dict(tm=tm, tn=tn2, tk=IN_WIDTH // 4, outs=[BF16])
    if n_rb >= 2:
        (dh,), ((chips_in_0,),) = _mm(dproj, win_t, "nn", name="in_proj_dx_top", rows_part=(0, n_rb // 2),
                                     jobs=[_chips_job(pair_in[0:1])], **dx_tiles)
        (dh,), ((chips_in_1,),) = _mm(dproj, win_t, "nn", name="in_proj_dx_bottom", into=dh,
                                     rows_part=(n_rb // 2, n_rb - n_rb // 2), jobs=[_chips_job(pair_in[1:2])], **dx_tiles)
    else:
        (dh,), ((chips_in_0,), (chips_in_1,)) = _mm(
            dproj, win_t, "nn", name="in_proj_dx", jobs=[_chips_job(pair_in[0:1]), _chips_job(pair_in[1:2])], **dx_tiles)
    (grad_x, st_1), ((chips_in_2,),) = _norm_mod_bwd(xs, norm1_g, mod, 0, dh, dx1, None, "norm1_bwd",
                                                     jobs=[_chips_job(pair_in[2:3])])
    chips_in = [chips_in_0, chips_in_1, chips_in_2]

    drel = jnp.transpose(_rel_bias_bwd(dbias.reshape(N_Q_HEADS, -1), onehot, "rel_bias_bwd"))

    misc = jnp.concatenate([drel.reshape(1, -1), dsinks[0:1, 0:N_Q_HEADS],
                            jnp.zeros((1, dd - N_BUCKETS * N_Q_HEADS - N_Q_HEADS), F32)], axis=1)
    zero_row = jnp.zeros((1, dd), F32)
    small = jnp.concatenate([
        st_1[0:2], st_2[3:4], st_2[0:2], st_f[1:2], zero_row, zero_row,
        st_1[2:3], st_l[0:1], st_l[1:2], st_l[2:3], st_l[3:4], st_2[2:3], st_f[0:1], misc,
        st_l[4:8], jnp.zeros((4, dd), F32)], axis=0)
    own_and_chips = {0: chip_idx, 1: 0, 2: 1, 3: 2}
    g_in = jnp.transpose(jnp.concatenate(
        [_sum_parts([p_, c_, c_, c_], own_and_chips, "sum_w_in_%d" % q_)
         for q_, (p_, c_) in enumerate(zip(pair_in, chips_in))], axis=1))
    w_in_res, ((small_all,),) = _adamw(w_in[0], m_w_in[0], v_w_in[0], [g_in], "adamw_w_in",
                                       jobs=[_gather_job([small], (0.3, 0.6))])
    (small_all,) = _fill_own([small_all], [small])

    def pack_small(b_, n1, cb, ba, bx, lam, n2, fg, rb, sk):
        misc_ = jnp.concatenate([rb.reshape(1, -1), sk.reshape(1, -1),
                                 jnp.zeros((1, dd - N_BUCKETS * N_Q_HEADS - N_Q_HEADS), F32)], axis=1)
        return jnp.concatenate([b_.reshape(6, dd), jnp.zeros((2, dd), F32), n1, cb, ba, bx, lam, n2, _row(fg), misc_,
                                jnp.zeros((8, dd), F32)], axis=0)

    def pack_mats(wa_, wx_):
        return jnp.concatenate([wa_.reshape(-1, dd), wx_.reshape(-1, dd)], axis=0)

    every = {i: i for i in range(N_DEV)}
    w_s = pack_small(b_ada, norm1_g, conv_b, lru_ba, lru_bx, lru_lambda, norm2_g, final_g, rel_bias, attn_sinks)
    m_s = pack_small(m_b_ada, m_norm1_g, m_conv_b, m_lru_ba, m_lru_bx, m_lru_lambda, m_norm2_g, m_final_g, m_rel_bias, m_attn_sinks)
    v_s = pack_small(v_b_ada, v_norm1_g, v_conv_b, v_lru_ba, v_lru_bx, v_lru_lambda, v_norm2_g, v_final_g, v_rel_bias, v_attn_sinks)
    small_res = _adamw(w_s, m_s, v_s, [small_all] * N_DEV, "adamw_small", part_index=every)
    mats_res = _adamw(pack_mats(lru_wa, lru_wx), pack_mats(m_lru_wa, m_lru_wx), pack_mats(v_lru_wa, v_lru_wx),
                      [mats_all] * N_DEV, "adamw_lru_mats", part_index=every)

    def unpack_small(s, mt):
        nb_ = N_BUCKETS * N_Q_HEADS
        half = mt.shape[0] // 2
        return dict(
            b_ada=s[0:6].reshape(1, 6 * dd), norm1_g=s[8:9], conv_b=s[9:10], lru_ba=s[10:11], lru_bx=s[11:12],
            lru_lambda=s[12:13], norm2_g=s[13:14], final_g=s[14], rel_bias=s[15, 0:nb_].reshape(N_BUCKETS, N_Q_HEADS),
            attn_sinks=s[15:16, nb_:nb_ + N_Q_HEADS],
            lru_wa=mt[:half].reshape(1, LRU_BLOCKS, LRU_BLOCK_W, LRU_BLOCK_W),
            lru_wx=mt[half:].reshape(1, LRU_BLOCKS, LRU_BLOCK_W, LRU_BLOCK_W))

    res = {k: [None] * 4 for k in ("w_ada", "w_in", "conv_w", "w_lru_out", "w_attn_out", "w_out", "w_ff1", "w_ff2")}
    for q_, (s_, mt_) in enumerate(zip(small_res, mats_res)):
        for k_, val in unpack_small(s_, mt_).items():
            res.setdefault(k_, [None] * 4)[q_] = val

    g_conv = lax.dynamic_slice(small_res[0][16:16 + CONV_WIDTH], (0, me * cshard), (CONV_WIDTH, cshard))
    conv_res = _adamw(conv_w[0], m_conv_w[0], v_conv_w[0], [g_conv], "adamw_conv")
    res["conv_w"] = [r_[None] for r_ in conv_res]

    dmod_all = small_all[:, 0:6, :].reshape(N_DEV, 6 * dd)
    dmod_cols = lax.dynamic_slice(dmod_all, (0, me * ncol_ada), (N_DEV, ncol_ada))
    g_ada = _ada_bwd(jnp.transpose(c_all), dmod_cols, "ada_bwd")
    ada_res = _adamw(w_ada[0], m_w_ada[0], v_w_ada[0], [g_ada], "adamw_ada")
    res["w_ada"] = [r_[None] for r_ in ada_res]

    pair = [None, pair_lo, pair_ao, pair_o, pair_f1, pair_f2]
    from_chips = [None, chips_lo, chips_ao, chips_o, chips_f1, chips_f2]

    def summed(i):
        return [pair[i], from_chips[i], from_chips[i], from_chips[i]], own_and_chips

    def sum_only(i, name):
        parts, index = summed(i)
        return _sum_parts(parts, index, name)

    res["w_in"] = [r_[None] for r_ in w_in_res]
    g_ff1 = jnp.transpose(sum_only(4, "sum_w_ff1"))
    res["w_ff1"] = [r_[None] for r_ in _adamw(w_ff1[0], m_w_ff1[0], v_w_ff1[0], [g_ff1], "adamw_w_ff1")]
    for i, (nm, w_, m_, v_) in {1: ("w_lru_out", w_lru_out, m_w_lru_out, v_w_lru_out),
                                2: ("w_attn_out", w_attn_out, m_w_attn_out, v_w_attn_out),
                                3: ("w_out", w_out, m_w_out, v_w_out),
                                5: ("w_ff2", w_ff2, m_w_ff2, v_w_ff2)}.items():
        parts, index = summed(i)
        res[nm] = [r_[None] for r_ in _adamw(w_[0], m_[0], v_[0], parts, "adamw_" + nm, part_index=index)]

    order = ["w_ada", "b_ada", "norm1_g", "w_in", "conv_w", "conv_b", "lru_wa", "lru_ba", "lru_wx", "lru_bx",
             "lru_lambda", "w_lru_out", "w_attn_out", "attn_sinks", "rel_bias", "w_out", "norm2_g", "w_ff1",
             "w_ff2", "final_g"]
    out = [loss, grad_x[None]]
    for q_ in range(4):
        out += [res[k_][q_] for k_ in order]
    return tuple(out)
```

```python
import functools
import math

import numpy as np
import jax
import jax.numpy as jnp
from jax import lax
from jax.experimental import pallas as pl
from jax.experimental.pallas import tpu as pltpu

F32 = jnp.float32
BF16 = jnp.bfloat16
MESH = pl.DeviceIdType.MESH

D_MODEL = 2048
N_Q_HEADS = 32
N_KV_HEADS = 4
GROUP = N_Q_HEADS // N_KV_HEADS
HEAD_DIM = 64
KV_WIDTH = N_KV_HEADS * HEAD_DIM
BLOCK = 128
NEG_INF = -1e30
N_BUCKETS = 32
MAX_DISTANCE = 128
LRU_BLOCKS = 16
LRU_BLOCK_W = 128
CONV_WIDTH = 4
LRU_C = 8.0
D_FF = 4 * D_MODEL
EPS = 1e-6
IN_WIDTH = 5 * D_MODEL + 2 * KV_WIDTH
OFF_LX, OFF_LG, OFF_Q, OFF_K, OFF_V, OFF_GA, OFF_GB = 0, 2048, 4096, 6144, 6400, 6656, 8704

ADAM_LR, ADAM_B1, ADAM_B2, ADAM_EPS, ADAM_WD, ADAM_STEP = 0.001, 0.9, 0.999, 1e-08, 0.01, 10

N_DEV = 8
VMEM_LIMIT_BYTES = 48 * 1024 * 1024
LANE = 128
LRU_CW = 512


def _params(sem, **kw):
    return pltpu.CompilerParams(dimension_semantics=sem, vmem_limit_bytes=VMEM_LIMIT_BYTES, **kw)


def _tile(n, pref, mult=8):
    if n <= pref:
        return n
    t = (pref // mult) * mult
    while t >= mult:
        if n % t == 0:
            return t
        t -= mult
    return n


def _sigmoid(x):
    return 0.5 * jnp.tanh(0.5 * x) + 0.5


def _gelu_tanh(x):
    k = math.sqrt(2.0 / math.pi)
    t = jnp.tanh(k * (x + 0.044715 * x * x * x))
    return 0.5 * x * (1.0 + t), t


def _gelu_tanh_grad(x, t):
    k = math.sqrt(2.0 / math.pi)
    return 0.5 * (1.0 + t) + 0.5 * x * (1.0 - t * t) * k * (1.0 + 3.0 * 0.044715 * x * x)


def _log1p(e):
    u = 1.0 + e
    return jnp.where(u == 1.0, e, jnp.log(u) * (e / jnp.where(u == 1.0, 1.0, u - 1.0)))


def _softplus(x):
    return jnp.maximum(x, 0.0) + _log1p(jnp.exp(-jnp.abs(x)))


def _neg_expm1(x, exp_x):
    series = -x * (1.0 + x * (0.5 + x * (1.0 / 6.0 + x * (1.0 / 24.0))))
    return jnp.where(x > -0.0625, series, 1.0 - exp_x)


def _my_place():
    return lax.axis_index("x"), lax.axis_index("y"), lax.axis_index("c")


def _all_gather(arrs, name):
    return _fill_own(_run_job(_gather_job(arrs), name), arrs)


def _fill_own(stacks, shards):
    me = 4 * lax.axis_index("x") + 2 * lax.axis_index("y") + lax.axis_index("c")
    out = []
    for g, s in zip(stacks, shards):
        if g.size * g.dtype.itemsize <= (1 << 20):
            slot = lax.broadcasted_iota(jnp.int32, g.shape, 0)
            out.append(jnp.where(slot == me, s[None], g))
        else:
            out.append(lax.dynamic_update_slice_in_dim(g, s[None], me, axis=0))
    return out


class _Job:
    def __init__(self, ins, outs, sems, start, finish, forwards=(), forward_at=()):
        self.ins, self.outs, self.sems = list(ins), list(outs), list(sems)
        self.start, self.finish, self.forwards, self.forward_at = start, finish, list(forwards), list(forward_at)
        assert len(self.forwards) <= len(self.forward_at)


def _gather_job(arrs, forward_at=(0.6, 0.85), part="all"):
    n = len(arrs)
    whole, near, far = part == "all", part == "near", part == "far"

    def copies(ins, outs, sems):
        send_sems, recv_sems = sems
        x, y, c = _my_place()
        sib = (x, y, 1 - c)
        xn, yn, dg = (1 - x, y), (x, 1 - y), (1 - x, 1 - y)

        def slot(px, py, pc):
            return 4 * px + 2 * py + pc

        def cp(a, k, src, dst_slot, to):
            return pltpu.make_async_remote_copy(
                src_ref=src, dst_ref=outs[a].at[dst_slot], send_sem=send_sems.at[a, k], recv_sem=recv_sems.at[a, k],
                device_id=to, device_id_type=MESH)

        def arrival(a, k, dst_slot):
            return cp(a, k, outs[a].at[dst_slot], dst_slot, sib)

        me_slot = slot(x, y, c)
        sends, relay, passes, arrive = {}, {}, {}, {}
        for a in range(n):
            if whole:
                own = ins[a]
                sends[a, 0] = cp(a, 0, own, me_slot, sib)
            if near:
                own = ins[a].at[c]
            if whole or near:
                sends[a, 1] = cp(a, 1, own, me_slot, (*xn, c))
                sends[a, 2] = cp(a, 2, own, me_slot, (*yn, c))
                passes[a, 4] = cp(a, 4, outs[a].at[slot(*xn, c)], slot(*xn, c), sib)
                passes[a, 5] = cp(a, 5, outs[a].at[slot(*yn, c)], slot(*yn, c), sib)
            if whole or far:
                source = outs[a] if whole else ins[a]
                relayed = c * slot(*yn, c) + (1 - c) * slot(*xn, c)
                relay[a, 3] = cp(a, 3, source.at[relayed], relayed, (x + c * (1 - 2 * x), y + (1 - c) * (1 - 2 * y), c))
                passes[a, 6] = cp(a, 6, outs[a].at[slot(*dg, c)], slot(*dg, c), sib)
            arrive[a, 0] = arrival(a, 0, slot(x, y, 1 - c))
            arrive[a, 1] = arrival(a, 1, slot(*xn, c))
            arrive[a, 2] = arrival(a, 2, slot(*yn, c))
            arrive[a, 3] = arrival(a, 3, slot(*dg, c))
            arrive[a, 4] = arrival(a, 4, slot(*xn, 1 - c))
            arrive[a, 5] = arrival(a, 5, slot(*yn, 1 - c))
            arrive[a, 6] = arrival(a, 6, slot(*dg, 1 - c))
        return sends, relay, passes, arrive

    def pick(d, ks):
        return [d[a, k] for a in range(n) for k in ks if (a, k) in d]

    def start(ins, outs, sems):
        sends, relay, _, _ = copies(ins, outs, sems)
        for cp in pick(sends, (0, 1, 2)) + (pick(relay, (3,)) if far else []):
            cp.start()

    def forward_neighbours(ins, outs, sems):
        _, relay, passes, arrive = copies(ins, outs, sems)
        for cp in pick(arrive, (1, 2)):
            cp.wait_recv()
        for cp in pick(relay, (3,)) + pick(passes, (4, 5)):
            cp.start()

    def forward_diagonal(ins, outs, sems):
        _, _, passes, arrive = copies(ins, outs, sems)
        for cp in pick(arrive, (3,)):
            cp.wait_recv()
        for cp in pick(passes, (6,)):
            cp.start()

    def finish(ins, outs, sems):
        sends, relay, passes, arrive = copies(ins, outs, sems)
        for cp in pick(arrive, (6,) if far else (0, 4, 5, 6) if whole else (4, 5)):
            cp.wait_recv()
        for cp in pick(sends, (0, 1, 2)) + pick(relay, (3,)) + pick(passes, (4, 5, 6)):
            cp.wait_send()

    shapes = [a.shape[-2:] for a in arrs]
    forwards = [forward_diagonal] if far else [forward_neighbours] if near else [forward_neighbours, forward_diagonal]
    return _Job(arrs, [jax.ShapeDtypeStruct((N_DEV,) + s, a.dtype) for s, a in zip(shapes, arrs)],
                [pltpu.SemaphoreType.DMA((n, 7)), pltpu.SemaphoreType.DMA((n, 7))],
                start, finish, forwards, forward_at)


def _pair_job(arrs):
    n = len(arrs)

    def copies(ins, outs, sems):
        send_sems, recv_sems = sems
        x, y, c = _my_place()
        remote = [pltpu.make_async_remote_copy(
            src_ref=ins[a], dst_ref=outs[a].at[c], send_sem=send_sems.at[a], recv_sem=recv_sems.at[a],
            device_id=(x, y, 1 - c), device_id_type=MESH) for a in range(n)]
        arrive = [pltpu.make_async_remote_copy(
            src_ref=ins[a], dst_ref=outs[a].at[1 - c], send_sem=send_sems.at[a], recv_sem=recv_sems.at[a],
            device_id=(x, y, 1 - c), device_id_type=MESH) for a in range(n)]
        return remote, arrive

    def start(ins, outs, sems):
        for cp in copies(ins, outs, sems)[0]:
            cp.start()

    def finish(ins, outs, sems):
        remote, arrive = copies(ins, outs, sems)
        for cp in arrive:
            cp.wait_recv()
        for cp in remote:
            cp.wait_send()

    return _Job(arrs, [jax.ShapeDtypeStruct((2,) + a.shape, a.dtype) for a in arrs],
                [pltpu.SemaphoreType.DMA((n,)), pltpu.SemaphoreType.DMA((n,))], start, finish)


def _run_job(job, name):
    ni, no = len(job.ins), len(job.outs)

    def body(*refs):
        ins, outs, sems = refs[:ni], refs[ni:ni + no], refs[ni + no:]
        job.start(ins, outs, sems)
        for fwd in job.forwards:
            fwd(ins, outs, sems)
        job.finish(ins, outs, sems)

    any_spec = pl.BlockSpec(memory_space=pl.ANY)
    return pl.pallas_call(
        body, name=name, out_shape=job.outs, in_specs=[any_spec] * ni, out_specs=[any_spec] * no,
        scratch_shapes=job.sems,
    )(*job.ins)


def _call(body, *, name, grid, in_specs, out_specs, out_shape, scratch_shapes=(), sem, args, jobs=(),
          prefetch=(), aliases=None):
    in_specs, out_specs, out_shape, scratch_shapes = list(in_specs), list(out_specs), list(out_shape), list(scratch_shapes)
    prefetch = list(prefetch)
    if not jobs and not prefetch and not aliases:
        res = pl.pallas_call(body, name=name, grid=grid, in_specs=in_specs, out_specs=out_specs, out_shape=out_shape,
                             scratch_shapes=scratch_shapes, compiler_params=_params(sem))(*args)
        return list(res), []
    n_in, n_out, n_scr = len(in_specs), len(out_specs), len(scratch_shapes)
    j_in = [len(j.ins) for j in jobs]
    j_out = [len(j.outs) for j in jobs]
    j_sem = [len(j.sems) for j in jobs]
    total = int(np.prod(grid))
    any_spec = pl.BlockSpec(memory_space=pl.ANY)

    def wrapped(*refs):
        pos = [0]

        def take(k):
            part = refs[pos[0]:pos[0] + k]
            pos[0] += k
            return part

        take(len(prefetch))
        ins = take(n_in)
        jins = [take(k) for k in j_in]
        outs = take(n_out)
        jouts = [take(k) for k in j_out]
        scr = take(n_scr)
        jsems = [take(k) for k in j_sem]
        step = pl.program_id(0)
        for d in range(1, len(grid)):
            step = step * grid[d] + pl.program_id(d)
        for j, job in enumerate(jobs):
            pl.when(step == 0)(functools.partial(job.start, jins[j], jouts[j], jsems[j]))
            for fwd, frac in zip(job.forwards, job.forward_at):
                at = min(int(total * frac), total - 1)
                pl.when(step == at)(functools.partial(fwd, jins[j], jouts[j], jsems[j]))
        body(*ins, *outs, *scr)
        for j, job in enumerate(jobs):
            pl.when(step == total - 1)(functools.partial(job.finish, jins[j], jouts[j], jsems[j]))

    grid_spec = pltpu.PrefetchScalarGridSpec(
        num_scalar_prefetch=len(prefetch), grid=grid,
        in_specs=in_specs + [any_spec] * sum(j_in),
        out_specs=out_specs + [any_spec] * sum(j_out),
        scratch_shapes=scratch_shapes + [s for j in jobs for s in j.sems])
    res = pl.pallas_call(
        wrapped, name=name, grid_spec=grid_spec,
        out_shape=out_shape + [o for j in jobs for o in j.outs],
        input_output_aliases={len(prefetch) + i: o for i, o in (aliases or {}).items()},
        compiler_params=_params(("arbitrary",) * len(grid)),
    )(*prefetch, *args, *[a for j in jobs for a in j.ins])
    res = list(res)
    own, rest = res[:n_out], res[n_out:]
    per_job = []
    for k in j_out:
        per_job.append(rest[:k])
        rest = rest[k:]
    return own, per_job


def _exchange_sibling(arrs, name):
    return _run_job(_sibling_job(arrs), name)


def _sibling_job(arrs):
    n = len(arrs)

    def copies(ins, outs, sems):
        send_sems, recv_sems = sems
        x, y, c = _my_place()
        return [pltpu.make_async_remote_copy(
            src_ref=ins[a].at[2 * k + 1 - c], dst_ref=outs[a].at[k],
            send_sem=send_sems.at[a, k], recv_sem=recv_sems.at[a, k],
            device_id=(x, y, 1 - c), device_id_type=MESH) for a in range(n) for k in range(4)]

    def start(ins, outs, sems):
        for cp in copies(ins, outs, sems):
            cp.start()

    def finish(ins, outs, sems):
        for cp in copies(ins, outs, sems):
            cp.wait()

    return _Job(arrs, [jax.ShapeDtypeStruct((4,) + a.shape[1:], a.dtype) for a in arrs],
                [pltpu.SemaphoreType.DMA((n, 4)), pltpu.SemaphoreType.DMA((n, 4))], start, finish)


def _chips_job(arrs):
    n = len(arrs)

    def copies(ins, outs, sems):
        send_sems, recv_sems = sems
        x, y, c = _my_place()
        chips = [(1 - x, y), (x, 1 - y), (1 - x, 1 - y)]
        return [pltpu.make_async_remote_copy(
            src_ref=ins[a].at[2 * px + py], dst_ref=outs[a].at[j],
            send_sem=send_sems.at[a, j], recv_sem=recv_sems.at[a, j],
            device_id=(px, py, c), device_id_type=MESH) for a in range(n) for j, (px, py) in enumerate(chips)]

    def start(ins, outs, sems):
        for cp in copies(ins, outs, sems):
            cp.start()

    def finish(ins, outs, sems):
        for cp in copies(ins, outs, sems):
            cp.wait()

    return _Job(arrs, [jax.ShapeDtypeStruct((3,) + a.shape[1:], a.dtype) for a in arrs],
                [pltpu.SemaphoreType.DMA((n, 3)), pltpu.SemaphoreType.DMA((n, 3))], start, finish)


def _pair_add(stack, from_sibling, c_idx, name, widths=None):
    _, r, cc = stack.shape
    tr = _tile(r, 512)
    ws = [cc] if widths is None else list(widths)
    assert sum(ws) == cc
    starts = [sum(ws[:q]) for q in range(len(ws))]

    def body(c_ref, a_ref, b_ref, *o_refs):
        res = a_ref[...].astype(F32) + b_ref[...].astype(F32)
        for o_ref, s0, w_ in zip(o_refs, starts, ws):
            o_ref[...] = res[:, s0:s0 + w_].astype(o_ref.dtype)

    grid_spec = pltpu.PrefetchScalarGridSpec(
        num_scalar_prefetch=1, grid=(4, r // tr),
        in_specs=[pl.BlockSpec((None, tr, cc), lambda k, i, c_ref: (2 * k + c_ref[0], i, 0)),
                  pl.BlockSpec((None, tr, cc), lambda k, i, c_ref: (k, i, 0))],
        out_specs=[pl.BlockSpec((None, tr, w_), lambda k, i, c_ref: (k, i, 0)) for w_ in ws])
    res = pl.pallas_call(
        body, name=name, grid_spec=grid_spec,
        out_shape=[jax.ShapeDtypeStruct((4, r, w_), BF16) for w_ in ws],
        compiler_params=_params(("parallel", "parallel")),
    )(c_idx, stack, from_sibling)
    return res[0] if widths is None else list(res)


_DIMS = {"nn": ((1,), (0,)), "nt": ((1,), (1,)), "tn": ((0,), (0,))}


def _mm(a, b, mode, *, tm, tn, tk=None, outs, epi=None, tiles=(), rows=(), name, jobs=(), chunks=None, into=None,
        n_total=None, rows_part=None, wide=None):
    wide = wide or {}
    if mode == "tn":
        kk, m = a.shape
    else:
        m, kk = a.shape
    n_all = b.shape[0] if mode == "nt" else b.shape[1]
    nn = n_all if chunks is None else chunks[0].shape[0] * chunks[1]
    n_all = n_all if n_total is None else n_total
    tk = kk if tk is None else tk
    assert m % tm == 0 and nn % tn == 0 and kk % tk == 0, (name, a.shape, b.shape, tm, tn, tk)
    nk = kk // tk
    nt, nr, no = len(tiles), len(rows), len(outs)
    n_into = 0 if into is None else 1

    def body(*refs):
        a_ref, b_ref = refs[:2]
        tile_refs = refs[2:2 + nt]
        row_refs = refs[2 + nt:2 + nt + nr]
        out_refs = refs[2 + nt + nr + n_into:2 + nt + nr + n_into + no]
        part = lax.dot_general(a_ref[...], b_ref[...], (_DIMS[mode], ((), ())), preferred_element_type=F32)

        def finish(acc):
            if epi is None:
                res = (acc,)
            else:
                res = epi(acc, *[t[...] for t in tile_refs], *[r[...] for r in row_refs])
            for o_ref, val in zip(out_refs, res):
                o_ref[...] = val.astype(o_ref.dtype)

        if nk == 1:
            finish(part)
        else:
            acc_ref = refs[-1]
            k = pl.program_id(2)

            @pl.when(k == 0)
            def _():
                acc_ref[...] = part

            @pl.when(k > 0)
            def _():
                acc_ref[...] += part

            @pl.when(k == nk - 1)
            def _():
                finish(acc_ref[...])

    if chunks is None:
        col = b_row = lambda j, *_: j
        prefetch = []
    else:
        assert mode == "nt" and not tiles and not rows and chunks[1] % tn == 0
        per = chunks[1] // tn
        col = lambda j, ids, b_ids: ids[j // per] * per + j % per
        b_row = lambda j, ids, b_ids: b_ids[j // per] * per + j % per
        prefetch = [chunks[0], chunks[2]]
    row0, n_row_blocks = (0, m // tm) if rows_part is None else rows_part
    assert rows_part is None or (mode != "tn" and not tiles)
    if mode == "tn":
        a_spec = pl.BlockSpec((tk, tm), lambda i, j, k, *_: (k, i))
    else:
        a_spec = pl.BlockSpec((tm, tk), lambda i, j, k, *_: (i + row0, k))
    if mode == "nt":
        b_spec = pl.BlockSpec((tn, tk), lambda i, j, k, *s: (b_row(j, *s), k))
    else:
        b_spec = pl.BlockSpec((tk, tn), lambda i, j, k, *_: (k, j))
    tile_specs = [pl.BlockSpec((tm, tn), functools.partial(lambda i, j, k, off: (i, j + off), off=off))
                  for _, off in tiles]
    row_specs = [pl.BlockSpec((r.shape[0], tn), lambda i, j, k: (0, j)) for r in rows]
    out_spec = pl.BlockSpec((tm, tn), lambda i, j, k, *s: (i + row0, col(j, *s)))
    args = [a, b, *[t for t, _ in tiles], *rows]
    in_specs = [a_spec, b_spec] + tile_specs + row_specs
    aliases = None
    if into is not None:
        aliases = {len(args): 0}
        args.append(into)
        in_specs.append(pl.BlockSpec(memory_space=pl.ANY))
    res, jres = _call(
        body, name=name, grid=(n_row_blocks, nn // tn, nk),
        in_specs=in_specs,
        out_specs=[out_spec if q not in wide else
                   pl.BlockSpec((tm, tn), functools.partial(lambda i, j, k, off: (i, j + off), off=wide[q][1]))
                   for q in range(no)],
        out_shape=[jax.ShapeDtypeStruct((m, wide[q][0] if q in wide else n_all), dt) for q, dt in enumerate(outs)],
        scratch_shapes=[pltpu.VMEM((tm, tn), F32)] if nk > 1 else [],
        sem=("parallel", "parallel", "arbitrary"),
        args=args, jobs=jobs, prefetch=prefetch, aliases=aliases)
    return (res, jres) if jobs else res


def _ada_fwd(c_all, w_ada, b_ada_cols, name):
    nb, dd = c_all.shape
    ncol = w_ada.shape[1]
    tn = _tile(ncol, 512, LANE)

    def body(c_ref, w_ref, b_ref, o_ref):
        cv = c_ref[...]
        act = (cv * _sigmoid(cv)).astype(BF16)
        o_ref[...] = jnp.dot(act, w_ref[...].astype(BF16), preferred_element_type=F32) + b_ref[...]

    return pl.pallas_call(
        body, name=name, grid=(ncol // tn,),
        in_specs=[pl.BlockSpec((nb, dd), lambda j: (0, 0)), pl.BlockSpec((dd, tn), lambda j: (0, j)),
                  pl.BlockSpec((1, tn), lambda j: (0, j))],
        out_specs=pl.BlockSpec((nb, tn), lambda j: (0, j)),
        out_shape=jax.ShapeDtypeStruct((nb, ncol), F32),
        compiler_params=_params(("parallel",)),
    )(c_all, w_ada, b_ada_cols)


def _ada_bwd(c_all_t, dmod_cols, name):
    dd, nb = c_all_t.shape
    ncol = dmod_cols.shape[1]
    tr = _tile(dd, 256)

    def body(c_ref, d_ref, o_ref):
        cv = c_ref[...]
        act = (cv * _sigmoid(cv)).astype(BF16).astype(F32)
        dm = d_ref[...].astype(BF16).astype(F32)
        acc = act[:, 0:1] * dm[0:1, :]
        for bi in range(1, nb):
            acc = acc + act[:, bi:bi + 1] * dm[bi:bi + 1, :]
        o_ref[...] = acc

    return pl.pallas_call(
        body, name=name, grid=(dd // tr,),
        in_specs=[pl.BlockSpec((tr, nb), lambda i: (i, 0)), pl.BlockSpec((nb, ncol), lambda i: (0, 0))],
        out_specs=pl.BlockSpec((tr, ncol), lambda i: (i, 0)),
        out_shape=jax.ShapeDtypeStruct((dd, ncol), F32),
        compiler_params=_params(("parallel",)),
    )(c_all_t, dmod_cols)


def _norm_mod(x, gain, mod, row0, name):
    t, dd = x.shape
    tt = _tile(t, 512)

    def body(x_ref, g_ref, m_ref, h_ref):
        xv = x_ref[...]
        r = lax.rsqrt(jnp.mean(xv * xv, axis=-1, keepdims=True) + EPS)
        shift, scale = m_ref[row0:row0 + 1, :], m_ref[row0 + 1:row0 + 2, :]
        h_ref[...] = ((xv * r * g_ref[...]) * (1.0 + scale) + shift).astype(BF16)

    return pl.pallas_call(
        body, name=name, grid=(t // tt,),
        in_specs=[pl.BlockSpec((tt, dd), lambda i: (i, 0)), pl.BlockSpec((1, dd), lambda i: (0, 0)),
                  pl.BlockSpec((8, dd), lambda i: (0, 0))],
        out_specs=pl.BlockSpec((tt, dd), lambda i: (i, 0)),
        out_shape=jax.ShapeDtypeStruct((t, dd), BF16),
        compiler_params=_params(("parallel",)),
    )(x, gain, mod)


def _norm_mod_bwd(x, gain, mod, row0, dh, dres, branch, name, jobs=()):
    t, dd = x.shape
    tt = _tile(t, 512)
    has_branch = branch is not None

    def body(*refs):
        if has_branch:
            x_ref, g_ref, m_ref, dh_ref, dres_ref, o_ref, dx_ref, st_ref, do_ref = refs[:-1]
        else:
            x_ref, g_ref, m_ref, dh_ref, dres_ref, dx_ref, st_ref = refs[:-1]

        acc_ref = refs[-1]
        step_i = pl.program_id(0)

        @pl.when(step_i == 0)
        def _():
            acc_ref[...] = jnp.zeros_like(acc_ref)

        def fold(v):
            out = v[0:8]
            for q in range(1, rc // 8):
                out = out + v[8 * q:8 * q + 8]
            return out

        def chunk(ci, carry):
            rows = pl.ds(pl.multiple_of(ci * rc, rc), rc)
            xv = x_ref[rows, :]
            g = g_ref[...]
            r = lax.rsqrt(jnp.mean(xv * xv, axis=-1, keepdims=True) + EPS)
            nv = xv * r
            dhv = dh_ref[rows, :].astype(F32)
            acc_ref[0] += fold(dhv)
            acc_ref[1] += fold(dhv * (nv * g))
            dng = dhv * (1.0 + m_ref[row0 + 1:row0 + 2, :])
            acc_ref[2] += fold(dng * nv)
            dn = dng * g
            dx = dres_ref[rows, :] + r * (dn - nv * jnp.mean(dn * nv, axis=-1, keepdims=True))
            dx_ref[rows, :] = dx
            if has_branch:
                acc_ref[3] += fold(dx * o_ref[rows, :].astype(F32))
                do_ref[rows, :] = (dx * m_ref[branch[1]:branch[1] + 1, :]).astype(BF16)
            return carry

        lax.fori_loop(0, tt // rc, chunk, 0, unroll=4)

        @pl.when(step_i == nsteps - 1)
        def _():
            st_ref[...] = jnp.zeros_like(st_ref)
            for k in range(4):
                st_ref[k:k + 1, :] = jnp.sum(acc_ref[k], axis=0, keepdims=True)

    rc = 16 if tt % 16 == 0 else 8
    nsteps = t // tt
    row = pl.BlockSpec((tt, dd), lambda i: (i, 0))
    in_specs = [row, pl.BlockSpec((1, dd), lambda i: (0, 0)), pl.BlockSpec((8, dd), lambda i: (0, 0)), row, row]
    args = [x, gain, mod, dh, dres]
    out_specs = [row, pl.BlockSpec((8, dd), lambda i: (0, 0))]
    out_shape = [jax.ShapeDtypeStruct((t, dd), F32), jax.ShapeDtypeStruct((8, dd), F32)]
    if has_branch:
        in_specs.append(row)
        args.append(branch[0])
        out_specs.append(row)
        out_shape.append(jax.ShapeDtypeStruct((t, dd), BF16))
    res, jres = _call(body, name=name, grid=(t // tt,), in_specs=in_specs, out_specs=out_specs, out_shape=out_shape,
                      scratch_shapes=[pltpu.VMEM((4, 8, dd), F32)], sem=("arbitrary",), args=args, jobs=jobs)
    return (res, jres) if jobs else res


def _final_loss(x2, o2, final_g, target, mod, name):
    t, dd = x2.shape
    tt = _tile(t, 256)
    nsteps = t // tt

    def body(x_ref, o_ref, g_ref, t_ref, m_ref, loss_ref, dx_ref, do_ref, st_ref, acc_ref):
        i = pl.program_id(0)

        @pl.when(i == 0)
        def _():
            acc_ref[...] = jnp.zeros_like(acc_ref)

        def fold(v):
            out = v[0:8]
            for q in range(1, rc // 8):
                out = out + v[8 * q:8 * q + 8]
            return out

        def chunk(ci, carry):
            rows = pl.ds(pl.multiple_of(ci * rc, rc), rc)
            xv = x_ref[rows, :]
            g = g_ref[...]
            r = lax.rsqrt(jnp.mean(xv * xv, axis=-1, keepdims=True) + EPS)
            nv = xv * r
            err = nv * g - t_ref[rows, :]
            acc_ref[0] += fold(err * err)
            dy = err * (1.0 / dd)
            acc_ref[1] += fold(dy * nv)
            dn = dy * g
            dx = r * (dn - nv * jnp.mean(dn * nv, axis=-1, keepdims=True))
            dx_ref[rows, :] = dx
            acc_ref[2] += fold(dx * o_ref[rows, :].astype(F32))
            do_ref[rows, :] = (dx * m_ref[5:6, :]).astype(BF16)
            return carry

        lax.fori_loop(0, tt // rc, chunk, 0, unroll=4)

        @pl.when(i == nsteps - 1)
        def _():
            lsum = jnp.sum(acc_ref[0], axis=0, keepdims=True)
            total = jnp.sum(lsum, axis=-1, keepdims=True) * (0.5 / dd)
            loss_ref[...] = jnp.broadcast_to(total, loss_ref.shape)
            st_ref[...] = jnp.zeros_like(st_ref)
            st_ref[0:1, :] = jnp.sum(acc_ref[1], axis=0, keepdims=True)
            st_ref[1:2, :] = jnp.sum(acc_ref[2], axis=0, keepdims=True)

    rc = 16 if tt % 16 == 0 else 8

    row = pl.BlockSpec((tt, dd), lambda i: (i, 0))
    return pl.pallas_call(
        body, name=name, grid=(nsteps,),
        in_specs=[row, row, pl.BlockSpec((1, dd), lambda i: (0, 0)), row, pl.BlockSpec((8, dd), lambda i: (0, 0))],
        out_specs=[pl.BlockSpec((8, LANE), lambda i: (0, 0)), row, row, pl.BlockSpec((8, dd), lambda i: (0, 0))],
        out_shape=[jax.ShapeDtypeStruct((8, LANE), F32), jax.ShapeDtypeStruct((t, dd), F32),
                   jax.ShapeDtypeStruct((t, dd), BF16), jax.ShapeDtypeStruct((8, dd), F32)],
        scratch_shapes=[pltpu.VMEM((3, 8, dd), F32)],
        compiler_params=_params(("arbitrary",)),
    )(x2, o2, final_g, target, mod)


def _lru_gates(xt, prev8, lp, wa_ref, wx_ref, first_tile, cw):
    tt = xt.shape[0]
    ext = jnp.concatenate([prev8, xt], axis=0)
    xs = [pltpu.roll(ext, s, 0)[8:, :] if s else xt for s in range(CONV_WIDTH)]
    xc = lp[0:1, :] + lp[7:8, :] * xs[0] + lp[6:7, :] * xs[1] + lp[5:6, :] * xs[2] + lp[4:5, :] * xs[3]
    xcb = xc.astype(BF16)
    za, zi = [], []
    for jb in range(cw // LRU_BLOCK_W):
        sl = slice(jb * LRU_BLOCK_W, (jb + 1) * LRU_BLOCK_W)
        za.append(jnp.dot(xcb[:, sl], wa_ref[jb].astype(BF16), preferred_element_type=F32))
        zi.append(jnp.dot(xcb[:, sl], wx_ref[jb].astype(BF16), preferred_element_type=F32))
    ra = _sigmoid(jnp.concatenate(za, axis=1) + lp[1:2, :])
    ri = _sigmoid(jnp.concatenate(zi, axis=1) + lp[2:3, :])
    sp = _softplus(-lp[3:4, :])
    log_a = -LRU_C * ra * sp
    av = jnp.exp(log_a)
    seq_start = jnp.logical_and(first_tile, lax.broadcasted_iota(jnp.int32, (tt, 1), 0) == 0)
    m2 = _neg_expm1(2.0 * log_a, av * av)
    inv_mult = jnp.where(seq_start, 1.0, lax.rsqrt(jnp.maximum(m2, 1e-30)))
    mult = jnp.where(seq_start, 1.0, m2 * inv_mult)
    return xs, xc, xcb, ra, ri, sp, av, mult, inv_mult, seq_start


def _scan_groups(a, u, reverse):
    tt, cw = a.shape
    a3, u3 = a.reshape(tt // 8, 8, cw), u.reshape(tt // 8, 8, cw)
    r = lax.broadcasted_iota(jnp.int32, (1, 8, 1), 1)
    for s in (1, 2, 4):
        shift = 8 - s if reverse else s
        take = r < 8 - s if reverse else r >= s
        u3 = jnp.where(take, a3 * pltpu.roll(u3, shift, 1) + u3, u3)
        a3 = jnp.where(take, a3 * pltpu.roll(a3, shift, 1), a3)
    return a3, u3


def _scan_carry(a_s, u_s, h_s, carry, reverse):
    ng = a_s.shape[0]

    def step(i, h):
        g = ng - 1 - i if reverse else i
        hg = u_s[g] + a_s[g] * h
        h_s[g] = hg
        return hg[0:1, :] if reverse else hg[7:8, :]

    return lax.fori_loop(0, ng, step, carry, unroll=4)


def _lru_fwd(proj, lp, wa, wx, name, jobs=()):
    t = proj.shape[0]
    w = D_MODEL
    cw = LRU_CW
    tt = _tile(t, 256)
    nb = cw // LRU_BLOCK_W

    def body(lx_ref, lg_ref, lp_ref, wa_ref, wx_ref, rec_ref, ya_ref, a_s, u_s, h_s, halo, carry):
        ti = pl.program_id(1)

        @pl.when(ti == 0)
        def _():
            halo[...] = jnp.zeros_like(halo)
            carry[...] = jnp.zeros_like(carry)

        xt = lx_ref[...].astype(F32)
        lp_v = lp_ref[...]
        _, xc, _, _, ri, _, av, mult, _, _ = _lru_gates(xt, halo[...], lp_v, wa_ref, wx_ref, ti == 0, cw)
        halo[...] = xt[tt - 8:, :]
        a_s[...], u_s[...] = _scan_groups(av, mult * (ri * xc), False)
        carry[...] = _scan_carry(a_s, u_s, h_s, carry[...], False)
        rec = h_s[...].reshape(tt, cw)
        rec_ref[...] = rec.astype(BF16)
        gelu, _ = _gelu_tanh(lg_ref[...].astype(F32))
        ya_ref[...] = (rec * gelu).astype(BF16)

    off_g = OFF_LG // cw
    res, jres = _call(
        body, name=name, grid=(w // cw, t // tt),
        in_specs=[pl.BlockSpec((tt, cw), lambda ci, ti: (ti, ci)),
                  pl.BlockSpec((tt, cw), lambda ci, ti: (ti, off_g + ci)),
                  pl.BlockSpec((8, cw), lambda ci, ti: (0, ci)),
                  pl.BlockSpec((nb, LRU_BLOCK_W, LRU_BLOCK_W), lambda ci, ti: (ci, 0, 0)),
                  pl.BlockSpec((nb, LRU_BLOCK_W, LRU_BLOCK_W), lambda ci, ti: (ci, 0, 0))],
        out_specs=[pl.BlockSpec((tt, cw), lambda ci, ti: (ti, ci))] * 2,
        out_shape=[jax.ShapeDtypeStruct((t, w), BF16)] * 2,
        scratch_shapes=[pltpu.VMEM((tt // 8, 8, cw), F32)] * 3 + [pltpu.VMEM((8, cw), F32), pltpu.VMEM((1, cw), F32)],
        sem=("parallel", "arbitrary"), args=[proj, proj, lp, wa, wx], jobs=jobs)
    return (res, jres) if jobs else res


def _lru_bwd(proj, rec, dya, lp, wa, wx, name, jobs=(), into=None):
    t = proj.shape[0]
    w = D_MODEL
    cw = LRU_CW if into is None else w
    tt = _tile(t, 256 if into is None else 128)
    nt = t // tt
    nb = cw // LRU_BLOCK_W
    r8 = tt // 8

    def body(lx_ref, lxp_ref, lg_ref, rec_ref, recp_ref, dya_ref, lp_ref, wa_ref, wx_ref, *rest):
        if into is None:
            dlx_ref, dlg_ref = rest[0], rest[1]
        else:
            dlx_ref, dlg_ref = rest[1].at[:, 0:cw], rest[1].at[:, cw:2 * cw]
        st_ref, dwa_ref, dwx_ref, a_s, d_s, dh_s, nhalo, carry = rest[-8:]
        step_i = pl.program_id(1)
        ti = nt - 1 - step_i

        @pl.when(step_i == 0)
        def _():
            st_ref[...] = jnp.zeros_like(st_ref)
            dwa_ref[...] = jnp.zeros_like(dwa_ref)
            dwx_ref[...] = jnp.zeros_like(dwx_ref)
            nhalo[...] = jnp.zeros_like(nhalo)
            carry[...] = jnp.zeros_like(carry)

        first = ti == 0
        keep = jnp.where(first, 0.0, 1.0)
        xt = lx_ref[...].astype(F32)
        prev8 = lxp_ref[...].astype(F32) * keep
        lp_v = lp_ref[...]
        xs, xc, xcb, ra, ri, sp, av, mult, inv_mult, seq_start = _lru_gates(xt, prev8, lp_v, wa_ref, wx_ref, first, cw)

        lg = lg_ref[...].astype(F32)
        gelu, th = _gelu_tanh(lg)
        dyav = dya_ref[...].astype(F32)
        recv = rec_ref[...].astype(F32)
        dlg_ref[...] = (dyav * recv * _gelu_tanh_grad(lg, th)).astype(BF16)

        drec = dyav * gelu
        e_in = carry[...]
        a_s[...], d_s[...] = _scan_groups(av, av * drec, True)
        carry[...] = _scan_carry(a_s, d_s, dh_s, e_in, True)
        e_next = jnp.concatenate([dh_s[...].reshape(tt, cw), jnp.broadcast_to(e_in, (8, cw))], axis=0)
        dh = drec + pltpu.roll(e_next, tt + 7, 0)[:tt, :]
        rprev8 = recp_ref[...].astype(F32) * keep
        hprev = pltpu.roll(jnp.concatenate([rprev8, recv], axis=0), 1, 0)[8:, :]
        da = dh * hprev
        dmult = jnp.where(seq_start, 0.0, dh * ri * xc)
        dri = dh * mult * xc
        dxc = dh * mult * ri
        dlog_a = da * av - dmult * (av * av) * inv_mult
        dra = dlog_a * (-LRU_C * sp)
        lam = lp_v[3:4, :]
        st_ref[3:4, :] += jnp.sum(dlog_a * ra, axis=0, keepdims=True) * (LRU_C * _sigmoid(-lam))
        dza = dra * ra * (1.0 - ra)
        dzi = dri * ri * (1.0 - ri)
        st_ref[1:2, :] += jnp.sum(dza, axis=0, keepdims=True)
        st_ref[2:3, :] += jnp.sum(dzi, axis=0, keepdims=True)
        dzab, dzib = dza.astype(BF16), dzi.astype(BF16)
        back = []
        for jb in range(nb):
            sl = slice(jb * LRU_BLOCK_W, (jb + 1) * LRU_BLOCK_W)
            dwa_ref[jb] += lax.dot_general(xcb[:, sl], dzab[:, sl], (_DIMS["tn"], ((), ())), preferred_element_type=F32)
            dwx_ref[jb] += lax.dot_general(xcb[:, sl], dzib[:, sl], (_DIMS["tn"], ((), ())), preferred_element_type=F32)
            back.append(
                lax.dot_general(dzab[:, sl], wa_ref[jb].astype(BF16), (_DIMS["nt"], ((), ())), preferred_element_type=F32)
                + lax.dot_general(dzib[:, sl], wx_ref[jb].astype(BF16), (_DIMS["nt"], ((), ())), preferred_element_type=F32))
        dxc = dxc + jnp.concatenate(back, axis=1)
        st_ref[0:1, :] += jnp.sum(dxc, axis=0, keepdims=True)
        for k in range(CONV_WIDTH):
            st_ref[4 + k:5 + k, :] += jnp.sum(dxc * xs[CONV_WIDTH - 1 - k], axis=0, keepdims=True)
        ext = jnp.concatenate([dxc, nhalo[...]], axis=0)
        dlx = lp_v[7:8, :] * dxc
        for s in range(1, CONV_WIDTH):
            dlx = dlx + lp_v[7 - s:8 - s, :] * pltpu.roll(ext, tt + 8 - s, 0)[:tt, :]
        dlx_ref[...] = dlx.astype(BF16)
        nhalo[...] = dxc[0:8, :]

    off_g = OFF_LG // cw
    tile = lambda off: pl.BlockSpec((tt, cw), lambda ci, si: (nt - 1 - si, off + ci))
    prev = lambda off: pl.BlockSpec((8, cw), lambda ci, si: (jnp.maximum((nt - 1 - si) * r8 - 1, 0), off + ci))
    wspec = pl.BlockSpec((nb, LRU_BLOCK_W, LRU_BLOCK_W), lambda ci, si: (ci, 0, 0))
    st_spec = pl.BlockSpec((8, cw), lambda ci, si: (0, ci))
    in_specs = [tile(0), prev(0), tile(off_g), tile(0), prev(0), tile(0), st_spec, wspec, wspec]
    args = [proj, proj, proj, rec, rec, dya, lp, wa, wx]
    if into is None:
        d_specs = [tile(0), tile(0)]
        d_shapes = [jax.ShapeDtypeStruct((t, w), BF16), jax.ShapeDtypeStruct((t, w), BF16)]
        aliases = None
    else:
        d_specs = [pl.BlockSpec((tt, 2 * cw), lambda ci, si: (nt - 1 - si, 0))]
        d_shapes = [jax.ShapeDtypeStruct(into.shape, BF16)]
        aliases = {len(args): 0}
        args.append(into)
        in_specs.append(pl.BlockSpec(memory_space=pl.ANY))
    res, jres = _call(
        body, name=name, grid=(w // cw, nt), in_specs=in_specs,
        out_specs=d_specs + [st_spec, wspec, wspec],
        out_shape=d_shapes + [jax.ShapeDtypeStruct((8, w), F32),
                              jax.ShapeDtypeStruct((LRU_BLOCKS, LRU_BLOCK_W, LRU_BLOCK_W), F32),
                              jax.ShapeDtypeStruct((LRU_BLOCKS, LRU_BLOCK_W, LRU_BLOCK_W), F32)],
        scratch_shapes=[pltpu.VMEM((tt // 8, 8, cw), F32)] * 3 + [pltpu.VMEM((8, cw), F32), pltpu.VMEM((1, cw), F32)],
        sem=("parallel", "arbitrary"), args=args, jobs=jobs, aliases=aliases)
    return (res, jres) if jobs else res


def _band_valid(first_block):
    qi = lax.broadcasted_iota(jnp.int32, (BLOCK, 2 * BLOCK), 0)
    ki = lax.broadcasted_iota(jnp.int32, (BLOCK, 2 * BLOCK), 1)
    rel = qi + BLOCK - ki
    valid = jnp.logical_and(rel >= 0, rel < BLOCK)
    return jnp.logical_and(valid, jnp.logical_or(ki >= BLOCK, jnp.logical_not(first_block)))


ATTN_STACK_BWD = 8


def _low_half():
    return lax.broadcasted_iota(jnp.int32, (1, LANE), 1) < HEAD_DIM


def _stack_heads(x, h0, ns):
    low = _low_half()
    parts = []
    for g in range(ns):
        h = h0 + g
        slab = x[:, (h // 2) * LANE:(h // 2 + 1) * LANE]
        parts.append(jnp.where(low if h % 2 == 0 else jnp.logical_not(low), slab, jnp.zeros_like(slab)))
    return jnp.concatenate(parts, axis=0)


def _unstack_heads(y, ns):
    low = _low_half()
    pairs = [jnp.where(low, y[(2 * j) * BLOCK:(2 * j + 1) * BLOCK], y[(2 * j + 1) * BLOCK:(2 * j + 2) * BLOCK])
             for j in range(ns // 2)]
    return pairs[0] if len(pairs) == 1 else jnp.concatenate(pairs, axis=1)


def _dup_kv(xall, kv):
    part = xall[:, kv * HEAD_DIM:(kv + 1) * HEAD_DIM]
    return jnp.concatenate([part, part], axis=1)


def _fold_halves(a):
    return a + pltpu.roll(a, HEAD_DIM, 1)


def _group_probs(qg, k2, bias_g, sink_g, lse_g, valid):
    s = lax.dot_general(qg, k2, (_DIMS["nt"], ((), ())), preferred_element_type=F32) * (HEAD_DIM ** -0.5)
    s = jnp.where(valid[None], s.reshape(bias_g.shape) + bias_g, NEG_INF)
    return jnp.exp(s - lse_g), jnp.exp(sink_g - lse_g)


def _group_sinks(s_ref, h0, ns):
    return jnp.concatenate([jnp.full((1, BLOCK, 1), s_ref[h0 + g], F32) for g in range(ns)], axis=0)


def _attn_specs(t):
    nblk = t // BLOCK
    q_spec = pl.BlockSpec((BLOCK, D_MODEL), lambda n: (n, OFF_Q // D_MODEL))
    kc = pl.BlockSpec((BLOCK, KV_WIDTH), lambda n: (n, OFF_K // KV_WIDTH))
    kp = pl.BlockSpec((BLOCK, KV_WIDTH), lambda n: (jnp.maximum(n - 1, 0), OFF_K // KV_WIDTH))
    vc = pl.BlockSpec((BLOCK, KV_WIDTH), lambda n: (n, OFF_V // KV_WIDTH))
    vp = pl.BlockSpec((BLOCK, KV_WIDTH), lambda n: (jnp.maximum(n - 1, 0), OFF_V // KV_WIDTH))
    bias_spec = pl.BlockSpec((N_Q_HEADS, BLOCK, 2 * BLOCK), lambda n: (0, 0, 0))
    sink_spec = pl.BlockSpec(memory_space=pltpu.SMEM)
    return nblk, [q_spec, kp, kc, vp, vc, bias_spec, sink_spec]


def _attn_fwd(proj, bias, sinks, name, jobs=()):
    t = proj.shape[0]
    nblk, in_specs = _attn_specs(t)

    def body(q_ref, kp_ref, kc_ref, vp_ref, vc_ref, b_ref, s_ref, o_ref, lse_ref):
        valid = _band_valid(pl.program_id(0) == 0)
        lane = lax.broadcasted_iota(jnp.int32, (1, LANE), 1)
        q = q_ref[...]
        kall = jnp.concatenate([kp_ref[...], kc_ref[...]], axis=0)
        vall = jnp.concatenate([vp_ref[...], vc_ref[...]], axis=0)
        outs = []
        lse = jnp.zeros((BLOCK, LANE), F32)
        for h in range(N_Q_HEADS):
            kv = h // GROUP
            kk = kall[:, kv * HEAD_DIM:(kv + 1) * HEAD_DIM]
            vv = vall[:, kv * HEAD_DIM:(kv + 1) * HEAD_DIM]
            s = lax.dot_general(q[:, h * HEAD_DIM:(h + 1) * HEAD_DIM], kk, (_DIMS["nt"], ((), ())),
                                preferred_element_type=F32) * (HEAD_DIM ** -0.5)
            s = jnp.where(valid, s + b_ref[h], NEG_INF)
            sink = s_ref[h]
            m = jnp.maximum(jnp.max(s, axis=-1, keepdims=True), sink)
            e = jnp.exp(s - m)
            den = jnp.sum(e, axis=-1, keepdims=True) + jnp.exp(sink - m)
            lse = jnp.where(lane == h, m + jnp.log(den), lse)
            outs.append(jnp.dot((e * (1.0 / den)).astype(BF16), vv, preferred_element_type=F32))
        o_ref[...] = jnp.concatenate(outs, axis=1).astype(BF16)
        lse_ref[...] = lse

    res, jres = _call(
        body, name=name, grid=(nblk,), in_specs=in_specs,
        out_specs=[pl.BlockSpec((BLOCK, D_MODEL), lambda n: (n, 0)), pl.BlockSpec((BLOCK, LANE), lambda n: (n, 0))],
        out_shape=[jax.ShapeDtypeStruct((t, D_MODEL), BF16), jax.ShapeDtypeStruct((t, LANE), F32)],
        sem=("parallel",), args=[proj, proj, proj, proj, proj, bias, sinks], jobs=jobs)
    return (res, jres) if jobs else res


def _attn_bwd(proj, datt, lse, bias, sinks, name, jobs=(), into=None):
    t = proj.shape[0]
    nblk, in_specs = _attn_specs(t)
    in_specs = in_specs + [pl.BlockSpec((BLOCK, D_MODEL), lambda n: (n, 0)), pl.BlockSpec((BLOCK, LANE), lambda n: (n, 0))]
    scale = HEAD_DIM ** -0.5
    args = [proj, proj, proj, proj, proj, bias, sinks, datt, lse]
    dq_spec = pl.BlockSpec((BLOCK, D_MODEL), lambda n: (n, 0))
    dq_shape = jax.ShapeDtypeStruct((t, D_MODEL), BF16)
    aliases = None
    if into is not None:
        aliases = {len(args): 0}
        args.append(into)
        in_specs.append(pl.BlockSpec(memory_space=pl.ANY))
        dq_spec = pl.BlockSpec((BLOCK, D_MODEL), lambda n: (n, OFF_Q // D_MODEL))
        dq_shape = jax.ShapeDtypeStruct(into.shape, BF16)

    def body(q_ref, kp_ref, kc_ref, vp_ref, vc_ref, b_ref, s_ref, do_ref, lse_ref, *rest):
        dq_ref, dk_ref, dv_ref, db_ref, ds_ref = rest[-5:]
        n = pl.program_id(0)

        @pl.when(n == 0)
        def _():
            dk_ref[...] = jnp.zeros_like(dk_ref)
            dv_ref[...] = jnp.zeros_like(dv_ref)
            db_ref[...] = jnp.zeros_like(db_ref)
            ds_ref[...] = jnp.zeros_like(ds_ref)

        valid = _band_valid(n == 0)
        lane = lax.broadcasted_iota(jnp.int32, (1, LANE), 1)
        q = q_ref[...]
        do = do_ref[...]
        lse_all = lse_ref[...]
        kall = jnp.concatenate([kp_ref[...], kc_ref[...]], axis=0)
        vall = jnp.concatenate([vp_ref[...], vc_ref[...]], axis=0)
        low = _low_half()
        dqs, dks, dvs = [], [], []
        dsink_row = jnp.zeros((1, LANE), F32)
        ns = ATTN_STACK_BWD
        for kv in range(N_KV_HEADS):
            k2, v2 = _dup_kv(kall, kv), _dup_kv(vall, kv)
            dk_acc = jnp.zeros((2 * BLOCK, LANE), F32)
            dv_acc = jnp.zeros((2 * BLOCK, LANE), F32)
            for h0 in range(kv * GROUP, (kv + 1) * GROUP, ns):
                qg, dog = _stack_heads(q, h0, ns), _stack_heads(do, h0, ns)
                lse_g = jnp.concatenate([lse_all[:, h0 + g:h0 + g + 1][None] for g in range(ns)], axis=0)
                p, psink = _group_probs(qg, k2, b_ref[h0:h0 + ns], _group_sinks(s_ref, h0, ns), lse_g, valid)
                dp = lax.dot_general(dog, v2, (_DIMS["nt"], ((), ())), preferred_element_type=F32)
                dp = dp.reshape(ns, BLOCK, 2 * BLOCK)
                delta = jnp.sum(p * dp, axis=-1, keepdims=True)
                ds = p * (dp - delta)
                db_ref[h0:h0 + ns] += ds
                dsink = -jnp.sum(psink * delta, axis=1, keepdims=True)
                for g in range(ns):
                    dsink_row = dsink_row + jnp.where(lane == h0 + g, dsink[g], 0.0)
                dsb = ds.reshape(ns * BLOCK, 2 * BLOCK).astype(BF16)
                pb = p.reshape(ns * BLOCK, 2 * BLOCK).astype(BF16)
                dqs.append(_unstack_heads(jnp.dot(dsb, k2, preferred_element_type=F32) * scale, ns))
                dk_acc = dk_acc + lax.dot_general(dsb, qg, (_DIMS["tn"], ((), ())), preferred_element_type=F32)
                dv_acc = dv_acc + lax.dot_general(pb, dog, (_DIMS["tn"], ((), ())), preferred_element_type=F32)
            dks.append(_fold_halves(dk_acc) * scale)
            dvs.append(_fold_halves(dv_acc))
        dq_ref[...] = jnp.concatenate(dqs, axis=1).astype(BF16)
        rows = pl.ds(pl.multiple_of(n * BLOCK, BLOCK), 2 * BLOCK)
        dk_ref[rows, :] += jnp.concatenate([jnp.where(low, dks[0], dks[1]), jnp.where(low, dks[2], dks[3])], axis=1)
        dv_ref[rows, :] += jnp.concatenate([jnp.where(low, dvs[0], dvs[1]), jnp.where(low, dvs[2], dvs[3])], axis=1)
        ds_ref[0:1, :] += dsink_row

    res, jres = _call(
        body, name=name, grid=(nblk,), in_specs=in_specs,
        out_specs=[dq_spec,
                   pl.BlockSpec((t + BLOCK, KV_WIDTH), lambda n: (0, 0)),
                   pl.BlockSpec((t + BLOCK, KV_WIDTH), lambda n: (0, 0)),
                   pl.BlockSpec((N_Q_HEADS, BLOCK, 2 * BLOCK), lambda n: (0, 0, 0)),
                   pl.BlockSpec((8, LANE), lambda n: (0, 0))],
        out_shape=[dq_shape,
                   jax.ShapeDtypeStruct((t + BLOCK, KV_WIDTH), F32),
                   jax.ShapeDtypeStruct((t + BLOCK, KV_WIDTH), F32),
                   jax.ShapeDtypeStruct((N_Q_HEADS, BLOCK, 2 * BLOCK), F32),
                   jax.ShapeDtypeStruct((8, LANE), F32)],
        sem=("arbitrary",), args=args, jobs=jobs, aliases=aliases)
    return (res, jres) if jobs else res


def _bucket_table():
    qi = np.arange(BLOCK)[:, None]
    ki = np.arange(2 * BLOCK)[None, :]
    rel = np.maximum(qi + BLOCK - ki, 0)
    max_exact = N_BUCKETS // 2
    relf = np.maximum(rel, 1).astype(np.float32)
    large = max_exact + (np.log(relf / max_exact) / math.log(MAX_DISTANCE / max_exact)
                         * (N_BUCKETS - max_exact)).astype(np.int32)
    large = np.minimum(large, N_BUCKETS - 1)
    return np.where(rel < max_exact, rel, large).astype(np.int32)


def _rel_bias_fwd(rel_bias, onehot, name):
    n = onehot.shape[1]
    tn = _tile(n, 8192, LANE)

    def body(r_ref, o_ref, out_ref):
        out_ref[...] = lax.dot_general(r_ref[...], o_ref[...], (_DIMS["tn"], ((), ())),
                                       precision=lax.Precision.HIGHEST, preferred_element_type=F32)

    return pl.pallas_call(
        body, name=name, grid=(n // tn,),
        in_specs=[pl.BlockSpec(rel_bias.shape, lambda j: (0, 0)), pl.BlockSpec((N_BUCKETS, tn), lambda j: (0, j))],
        out_specs=pl.BlockSpec((N_Q_HEADS, tn), lambda j: (0, j)),
        out_shape=jax.ShapeDtypeStruct((N_Q_HEADS, n), F32),
        compiler_params=_params(("parallel",)),
    )(rel_bias, onehot)


def _rel_bias_bwd(dbias, onehot, name):
    def body(d_ref, o_ref, out_ref):
        out_ref[...] = lax.dot_general(d_ref[...], o_ref[...], (_DIMS["nt"], ((), ())),
                                       precision=lax.Precision.HIGHEST, preferred_element_type=F32)

    full = pl.BlockSpec(dbias.shape, lambda: (0, 0))
    return pl.pallas_call(
        body, name=name, in_specs=[full, full],
        out_specs=pl.BlockSpec((N_Q_HEADS, N_BUCKETS), lambda: (0, 0)),
        out_shape=jax.ShapeDtypeStruct((N_Q_HEADS, N_BUCKETS), F32),
        compiler_params=_params(()),
    )(dbias, onehot)


def _part_specs(gparts, part_index, tr, cc):
    blk = pl.BlockSpec((tr, cc), lambda i, *_: (i, 0))
    specs = []
    for i, _ in enumerate(gparts):
        if i in part_index:
            ix = part_index[i]
            if isinstance(ix, int):
                specs.append(pl.BlockSpec((None, tr, cc), functools.partial(lambda i, *_, ix: (ix, i, 0), ix=ix)))
            else:
                specs.append(pl.BlockSpec((None, tr, cc), lambda i, ix_ref: (ix_ref[0], i, 0)))
        else:
            specs.append(blk)
    return blk, specs


def _sum_parts(gparts, part_index, name):
    r, cc = gparts[0].shape[-2:]
    tr = _tile(r, 256)
    dyn = [i for i, ix in part_index.items() if not isinstance(ix, int)]
    assert len(dyn) <= 1

    def body(*refs):
        refs = refs[len(dyn):]
        g = refs[0][...].astype(F32)
        for gr in refs[1:-1]:
            g = g + gr[...].astype(F32)
        refs[-1][...] = g

    blk, specs = _part_specs(gparts, part_index, tr, cc)
    grid_spec = pltpu.PrefetchScalarGridSpec(num_scalar_prefetch=len(dyn), grid=(r // tr,), in_specs=specs, out_specs=blk)
    return pl.pallas_call(
        body, name=name, grid_spec=grid_spec, out_shape=jax.ShapeDtypeStruct((r, cc), F32),
        compiler_params=_params(("parallel",)),
    )(*[part_index[i] for i in dyn], *gparts)


def _adamw(w, m, v, gparts, name, part_index=None, jobs=()):
    r, cc = w.shape
    tr = _tile(r, 128)
    np_ = len(gparts)
    part_index = part_index or {}
    dyn = [i for i, ix in part_index.items() if not isinstance(ix, int)]
    assert len(dyn) <= 1
    bc1 = 1.0 / (1.0 - ADAM_B1 ** ADAM_STEP)
    bc2 = 1.0 / (1.0 - ADAM_B2 ** ADAM_STEP)

    def body(*refs):
        refs = refs[len(dyn):]
        w_ref, m_ref, v_ref = refs[:3]
        g_refs = refs[3:3 + np_]
        g_out, d_out, m_out, v_out = refs[3 + np_:]
        g = g_refs[0][...].astype(F32)
        for gr in g_refs[1:]:
            g = g + gr[...].astype(F32)
        mn = ADAM_B1 * m_ref[...] + (1.0 - ADAM_B1) * g
        vn = ADAM_B2 * v_ref[...] + (1.0 - ADAM_B2) * (g * g)
        g_out[...] = g
        m_out[...] = mn
        v_out[...] = vn
        d_out[...] = -ADAM_LR * ((mn * bc1) / (jnp.sqrt(vn * bc2) + ADAM_EPS) + ADAM_WD * w_ref[...])

    blk, g_specs = _part_specs(gparts, part_index, tr, cc)
    if jobs:
        assert not dyn
        return _call(body, name=name, grid=(r // tr,), in_specs=[blk, blk, blk] + g_specs, out_specs=[blk] * 4,
                     out_shape=[jax.ShapeDtypeStruct((r, cc), F32)] * 4, sem=("arbitrary",),
                     args=[w, m, v, *gparts], jobs=jobs)
    grid_spec = pltpu.PrefetchScalarGridSpec(
        num_scalar_prefetch=len(dyn), grid=(r // tr,),
        in_specs=[blk, blk, blk] + g_specs, out_specs=[blk] * 4)
    return pl.pallas_call(
        body, name=name, grid_spec=grid_spec,
        out_shape=[jax.ShapeDtypeStruct((r, cc), F32)] * 4,
        compiler_params=_params(("parallel",)),
    )(*[part_index[i] for i in dyn], w, m, v, *gparts)


def _row(vec):
    return vec.reshape(1, -1)


def kernel(x, c, w_ada, b_ada, norm1_g, w_in, conv_w, conv_b, lru_wa, lru_ba, lru_wx, lru_bx, lru_lambda, w_lru_out, w_attn_out, attn_sinks, rel_bias, w_out, norm2_g, w_ff1, w_ff2, final_g, loss_target, m_w_ada, m_b_ada, m_norm1_g, m_w_in, m_conv_w, m_conv_b, m_lru_wa, m_lru_ba, m_lru_wx, m_lru_bx, m_lru_lambda, m_w_lru_out, m_w_attn_out, m_attn_sinks, m_rel_bias, m_w_out, m_norm2_g, m_w_ff1, m_w_ff2, m_final_g, v_w_ada, v_b_ada, v_norm1_g, v_w_in, v_conv_w, v_conv_b, v_lru_wa, v_lru_ba, v_lru_wx, v_lru_bx, v_lru_lambda, v_w_lru_out, v_w_attn_out, v_attn_sinks, v_rel_bias, v_w_out, v_norm2_g, v_w_ff1, v_w_ff2, v_final_g):
    dd = D_MODEL
    t = x.shape[1]
    ix, iy, ic = lax.axis_index("x"), lax.axis_index("y"), lax.axis_index("c")
    me = 4 * ix + 2 * iy + ic
    chip = 2 * ix + iy
    c_idx = jnp.reshape(ic, (1,)).astype(jnp.int32)
    chip_idx = jnp.reshape(chip, (1,)).astype(jnp.int32)

    xs = x[0]
    target = loss_target[0]
    ncol_ada = w_ada.shape[2]
    cshard = conv_w.shape[2]

    shards = [jnp.transpose(w_in[0]).astype(BF16), w_lru_out[0].astype(BF16), w_attn_out[0].astype(BF16),
              w_out[0].astype(BF16), jnp.transpose(w_ff1[0]).astype(BF16), w_ff2[0].astype(BF16)]
    s_in, s_lo, s_ao, s_o, s_f1, s_f2 = shards

    def full(g):
        return g.reshape(N_DEV * g.shape[1], dd)

    def stack(g):
        return g.reshape(N_DEV, g.shape[0] // N_DEV, dd)

    (w_in_pair,) = _run_job(_pair_job([s_in]), "ag_w_in_pair")
    w_in_pair = lax.dynamic_update_slice_in_dim(w_in_pair, s_in[None], ic, axis=0)

    pack0 = jnp.zeros((8, dd), F32).at[0:1, :].set(c).at[1:1 + CONV_WIDTH, 0:cshard].set(conv_w[0])
    (g0,) = _all_gather([pack0], "ag_cond")
    c_all = g0[:, 0, :]
    conv_w_full = jnp.transpose(g0[:, 1:1 + CONV_WIDTH, 0:cshard], (1, 0, 2)).reshape(CONV_WIDTH, dd)
    b_cols = lax.dynamic_slice(b_ada, (0, me * ncol_ada), (1, ncol_ada))
    mod_cols = _ada_fwd(c_all, w_ada[0], b_cols, "ada_fwd")
    (g1,) = _all_gather([mod_cols], "ag_mod")
    mod_mine = lax.dynamic_index_in_dim(g1, me, axis=1, keepdims=False).reshape(6, dd)
    mod = jnp.concatenate([mod_mine, jnp.zeros((2, dd), F32)], axis=0)

    bucket = _bucket_table()
    onehot = jnp.asarray((bucket.reshape(1, -1) == np.arange(N_BUCKETS)[:, None]).astype(np.float32))
    bias = _rel_bias_fwd(rel_bias, onehot, "rel_bias_fwd").reshape(N_Q_HEADS, BLOCK, 2 * BLOCK)
    sinks = attn_sinks.reshape(N_Q_HEADS)
    lp = jnp.concatenate([conv_b, lru_ba, lru_bx, lru_lambda, conv_w_full], axis=0)
    wa, wx = lru_wa[0], lru_wx[0]

    tm = _tile(t, 1024)
    tn = 512
    h = _norm_mod(xs, norm1_g, mod, 0, "norm1")
    tn2 = 1024
    chunk_w = 2 * s_in.shape[0]

    def ids(*ks):
        return jnp.stack([jnp.asarray(k, jnp.int32) for k in ks])

    tn_in = _tile(chunk_w, 1024, LANE)
    (proj,), ((g_near,),) = _mm(
        h, w_in_pair.reshape(chunk_w, dd), "nt", tm=tm, tn=tn_in, outs=[BF16], name="in_proj_own",
        chunks=(ids(chip), chunk_w, ids(0)), n_total=IN_WIDTH, jobs=[_gather_job([w_in_pair], (0.7,), "near")])
    near_ids = ids(2 * (1 - ix) + iy, 2 * ix + 1 - iy)
    (proj,), ((g_far,), (g_o,)) = _mm(
        h, full(g_near), "nt", tm=tm, tn=tn_in, outs=[BF16], name="in_proj_near",
        chunks=(near_ids, chunk_w, near_ids), into=proj,
        jobs=[_gather_job([g_near], (0.6,), "far"), _gather_job([s_o], (0.6, 0.85))])
    far_chip = 2 * (1 - ix) + 1 - iy
    (proj,), ((g_lo,),) = _mm(
        h, full(g_far), "nt", tm=tm, tn=tn_in, outs=[BF16], name="in_proj_far",
        chunks=(ids(far_chip), chunk_w, ids(far_chip)), into=proj, jobs=[_gather_job([s_lo], (0.5, 0.8))])
    far_slots = lax.dynamic_slice_in_dim(g_far, 2 * far_chip, 2, axis=0)
    win_t = lax.dynamic_update_slice_in_dim(g_near, far_slots, 2 * far_chip, axis=0)
    win_t = full(lax.dynamic_update_slice_in_dim(win_t, w_in_pair, 2 * chip, axis=0))
    (g_o,) = _fill_own([g_o], [s_o])
    (rec, ya), ((g_f1,),) = _lru_fwd(proj, lp, wa, wx, "lru_fwd", jobs=[_gather_job([s_f1], (0.6, 0.9))])
    (att, lse), ((g_ao,),) = _attn_fwd(proj, bias, sinks, "attn_fwd", jobs=[_gather_job([s_ao], (0.4, 0.7))])
    g_lo, g_ao, g_f1 = _fill_own([g_lo, g_ao, g_f1], [s_lo, s_ao, s_f1])
    wlo, wao, wo, wf1_t = full(g_lo), full(g_ao), full(g_o), full(g_f1)
    (y_a,) = _mm(ya, wlo, "nn", tm=tm, tn=tn2, outs=[BF16], name="lru_out")

    def merge_epi(acc, ya_t, ga_t, gb_t):
        return acc, _sigmoid(ga_t.astype(F32)) * ya_t.astype(F32) + _sigmoid(gb_t.astype(F32)) * acc

    y_b, merged = _mm(att, wao, "nn", tm=tm, tn=tn, outs=[BF16, BF16], epi=merge_epi,
                      tiles=[(y_a, 0), (proj, OFF_GA // tn), (proj, OFF_GB // tn)], name="attn_out_merge")

    def resid_epi(row):
        def epi(acc, x_t, mod_t):
            return acc, x_t + mod_t[row:row + 1, :] * acc
        return epi

    o1, x1 = _mm(merged, wo, "nn", tm=tm, tn=tn2, outs=[BF16, F32], epi=resid_epi(2),
                 tiles=[(xs, 0)], rows=[mod], name="out_proj")
    h2 = _norm_mod(x1, norm2_g, mod, 3, "norm2")

    def relu2_epi(acc):
        rl = jnp.maximum(acc, 0.0)
        return acc, rl * rl

    (f_pre, ff), ((g_f2,),) = _mm(h2, wf1_t, "nt", tm=tm, tn=1024, outs=[BF16, BF16], epi=relu2_epi, name="ff1",
                                  jobs=[_gather_job([s_f2], (0.6, 0.9))])
    wf2 = full(_fill_own([g_f2], [s_f2])[0])
    tmh = _tile(t, 512)
    o2, x2 = _mm(ff, wf2, "nn", tm=tmh, tn=tn, outs=[BF16, F32], epi=resid_epi(5),
                 tiles=[(x1, 0)], rows=[mod], name="ff2")

    loss_blk, dx2, do2, st_f = _final_loss(x2, o2, _row(final_g), target, mod, "final_loss")
    loss = lax.psum(loss_blk[0, 0], ("x", "y", "c"))

    def drelu2_epi(acc, f_t):
        return (acc * (2.0 * jnp.maximum(f_t.astype(F32), 0.0)),)

    tmw, tnw, tkw = 512, dd, _tile(t, 2048)
    (df,) = _mm(do2, wf2, "nt", tm=tm, tn=1024, outs=[BF16], epi=drelu2_epi, tiles=[(f_pre, 0)], name="ff2_dx")
    (dwf2,) = _mm(ff, do2, "tn", tm=tmw, tn=tnw, tk=tkw, outs=[BF16], name="ff2_dw")
    (dh2,), ((sib_f2,),) = _mm(df, wf1_t, "nn", tm=tmh, tn=tn, outs=[BF16], name="ff1_dx",
                              jobs=[_sibling_job([stack(dwf2)])])
    pair_f2 = _pair_add(stack(dwf2), sib_f2, c_idx, "rs_pair_add_w_ff2")
    (dwf1_t,), ((chips_f2,),) = _mm(df, h2, "tn", tm=tmw, tn=tnw, tk=tkw, outs=[BF16], name="ff1_dw",
                                    jobs=[_chips_job([pair_f2])])
    (dx1, st_2, do1), ((sib_f1,),) = _norm_mod_bwd(x1, norm2_g, mod, 3, dh2, dx2, (o1, 2), "norm2_bwd",
                                                   jobs=[_sibling_job([stack(dwf1_t)])])
    pair_f1 = _pair_add(stack(dwf1_t), sib_f1, c_idx, "rs_pair_add_w_ff1")

    def dmerge_epi(acc, ga_t, gb_t, ya_t, yb_t):
        sa, sb = _sigmoid(ga_t.astype(F32)), _sigmoid(gb_t.astype(F32))
        return (acc * sa, acc * sb, acc * ya_t.astype(F32) * sa * (1.0 - sa), acc * yb_t.astype(F32) * sb * (1.0 - sb))

    tmh = _tile(t, 512)
    dy_a, dy_b, dproj, dgb = _mm(do1, wo, "nt", tm=tm, tn=tn, outs=[BF16] * 4, epi=dmerge_epi,
                                 tiles=[(proj, OFF_GA // tn), (proj, OFF_GB // tn), (y_a, 0), (y_b, 0)],
                                 wide={2: (IN_WIDTH, OFF_GA // tn)}, name="out_proj_dx")
    (dwo,) = _mm(merged, do1, "tn", tm=tmw, tn=tnw, tk=tkw, outs=[BF16], name="out_proj_dw")
    (datt,) = _mm(dy_b, wao, "nt", tm=tm, tn=tn2, outs=[BF16], name="attn_out_dx")
    (dwao,) = _mm(att, dy_b, "tn", tm=tmw, tn=tnw, tk=tkw, outs=[BF16], name="attn_out_dw")
    (dya,) = _mm(dy_a, wlo, "nt", tm=tm, tn=tn2, outs=[BF16], name="lru_out_dx")
    (dwlo,) = _mm(ya, dy_a, "tn", tm=tmw, tn=tnw, tk=tkw, outs=[BF16], name="lru_out_dw")
    (dproj, dk_pad, dv_pad, dbias, dsinks), (sib_3, (chips_f1,)) = _attn_bwd(
        proj, datt, lse, bias, sinks, "attn_bwd", into=dproj,
        jobs=[_sibling_job([stack(dwlo), stack(dwao), stack(dwo)]), _chips_job([pair_f1])])
    pair_lo, pair_ao, pair_o = [_pair_add(stack(g_), s_, c_idx, "rs_pair_add_" + nm_)
                                for g_, s_, nm_ in zip((dwlo, dwao, dwo), sib_3, ("w_lru_out", "w_attn_out", "w_out"))]
    (dproj, st_l, dwa, dwx), ((chips_lo, chips_ao, chips_o),) = _lru_bwd(
        proj, rec, dya, lp, wa, wx, "lru_bwd", into=dproj, jobs=[_chips_job([pair_lo, pair_ao, pair_o])])
    dkv = jnp.concatenate([dk_pad[BLOCK:].astype(BF16), dv_pad[BLOCK:].astype(BF16)], axis=1)
    dproj = lax.dynamic_update_slice_in_dim(dproj, dkv, OFF_K, axis=1)
    dproj = lax.dynamic_update_slice_in_dim(dproj, dgb, OFF_GB, axis=1)
    lru_mats = jnp.concatenate([dwa.reshape(-1, dd), dwx.reshape(-1, dd)], axis=0)
    (dwin_t,), ((mats_all,),) = _mm(dproj, h, "tn", tm=768, tn=tnw, tk=tkw, outs=[BF16], name="in_proj_dw",
                                   jobs=[_gather_job([lru_mats], (0.3, 0.6))])
    (mats_all,) = _fill_own([mats_all], [lru_mats])
    (sib_in,) = _exchange_sibling([stack(dwin_t)], "rs_sibling_w_in")
    pair_in = _pair_add(stack(dwin_t), sib_in, c_idx, "rs_pair_add_w_in", widths=(dd // 2, dd // 2))
    n_rb = t // tm
    dx_tiles = dict(tm=tm, tn=tn2, tk=IN_WIDTH // 4, outs=[BF16])
    if n_rb >= 2:
        (dh,), ((chips_in_0,),) = _mm(dproj, win_t, "nn", name="in_proj_dx_top", rows_part=(0, n_rb // 2),
                                     jobs=[_chips_job(pair_in[0:1])], **dx_tiles)
        (dh,), ((chips_in_1,),) = _mm(dproj, win_t, "nn", name="in_proj_dx_bottom", into=dh,
                                     rows_part=(n_rb // 2, n_rb - n_rb // 2), jobs=[_chips_job(pair_in[1:2])], **dx_tiles)
    else:
        (dh,), ((chips_in_0,), (chips_in_1,)) = _mm(
            dproj, win_t, "nn", name="in_proj_dx", jobs=[_chips_job(pair_in[0:1]), _chips_job(pair_in[1:2])], **dx_tiles)
    grad_x, st_1 = _norm_mod_bwd(xs, norm1_g, mod, 0, dh, dx1, None, "norm1_bwd")
    chips_in = [chips_in_0, chips_in_1]

    drel = jnp.transpose(_rel_bias_bwd(dbias.reshape(N_Q_HEADS, -1), onehot, "rel_bias_bwd"))

    misc = jnp.concatenate([drel.reshape(1, -1), dsinks[0:1, 0:N_Q_HEADS],
                            jnp.zeros((1, dd - N_BUCKETS * N_Q_HEADS - N_Q_HEADS), F32)], axis=1)
    zero_row = jnp.zeros((1, dd), F32)
    small = jnp.concatenate([
        st_1[0:2], st_2[3:4], st_2[0:2], st_f[1:2], zero_row, zero_row,
        st_1[2:3], st_l[0:1], st_l[1:2], st_l[2:3], st_l[3:4], st_2[2:3], st_f[0:1], misc,
        st_l[4:8], jnp.zeros((4, dd), F32)], axis=0)
    own_and_chips = {0: chip_idx, 1: 0, 2: 1, 3: 2}
    g_in = jnp.transpose(jnp.concatenate(
        [_sum_parts([p_, c_, c_, c_], own_and_chips, "sum_w_in_%d" % q_)
         for q_, (p_, c_) in enumerate(zip(pair_in, chips_in))], axis=1))
    w_in_res, ((small_all,),) = _adamw(w_in[0], m_w_in[0], v_w_in[0], [g_in], "adamw_w_in",
                                       jobs=[_gather_job([small], (0.3, 0.6))])
    (small_all,) = _fill_own([small_all], [small])

    def pack_small(b_, n1, cb, ba, bx, lam, n2, fg, rb, sk):
        misc_ = jnp.concatenate([rb.reshape(1, -1), sk.reshape(1, -1),
                                 jnp.zeros((1, dd - N_BUCKETS * N_Q_HEADS - N_Q_HEADS), F32)], axis=1)
        return jnp.concatenate([b_.reshape(6, dd), jnp.zeros((2, dd), F32), n1, cb, ba, bx, lam, n2, _row(fg), misc_,
                                jnp.zeros((8, dd), F32)], axis=0)

    def pack_mats(wa_, wx_):
        return jnp.concatenate([wa_.reshape(-1, dd), wx_.reshape(-1, dd)], axis=0)

    every = {i: i for i in range(N_DEV)}
    w_s = pack_small(b_ada, norm1_g, conv_b, lru_ba, lru_bx, lru_lambda, norm2_g, final_g, rel_bias, attn_sinks)
    m_s = pack_small(m_b_ada, m_norm1_g, m_conv_b, m_lru_ba, m_lru_bx, m_lru_lambda, m_norm2_g, m_final_g, m_rel_bias, m_attn_sinks)
    v_s = pack_small(v_b_ada, v_norm1_g, v_conv_b, v_lru_ba, v_lru_bx, v_lru_lambda, v_norm2_g, v_final_g, v_rel_bias, v_attn_sinks)
    small_res = _adamw(w_s, m_s, v_s, [small_all] * N_DEV, "adamw_small", part_index=every)
    mats_res = _adamw(pack_mats(lru_wa, lru_wx), pack_mats(m_lru_wa, m_lru_wx), pack_mats(v_lru_wa, v_lru_wx),
                      [mats_all] * N_DEV, "adamw_lru_mats", part_index=every)

    def unpack_small(s, mt):
        nb_ = N_BUCKETS * N_Q_HEADS
        half = mt.shape[0] // 2
        return dict(
            b_ada=s[0:6].reshape(1, 6 * dd), norm1_g=s[8:9], conv_b=s[9:10], lru_ba=s[10:11], lru_bx=s[11:12],
            lru_lambda=s[12:13], norm2_g=s[13:14], final_g=s[14], rel_bias=s[15, 0:nb_].reshape(N_BUCKETS, N_Q_HEADS),
            attn_sinks=s[15:16, nb_:nb_ + N_Q_HEADS],
            lru_wa=mt[:half].reshape(1, LRU_BLOCKS, LRU_BLOCK_W, LRU_BLOCK_W),
            lru_wx=mt[half:].reshape(1, LRU_BLOCKS, LRU_BLOCK_W, LRU_BLOCK_W))

    res = {k: [None] * 4 for k in ("w_ada", "w_in", "conv_w", "w_lru_out", "w_attn_out", "w_out", "w_ff1", "w_ff2")}
    for q_, (s_, mt_) in enumerate(zip(small_res, mats_res)):
        for k_, val in unpack_small(s_, mt_).items():
            res.setdefault(k_, [None] * 4)[q_] = val

    g_conv = lax.dynamic_slice(small_res[0][16:16 + CONV_WIDTH], (0, me * cshard), (CONV_WIDTH, cshard))
    conv_res = _adamw(conv_w[0], m_conv_w[0], v_conv_w[0], [g_conv], "adamw_conv")
    res["conv_w"] = [r_[None] for r_ in conv_res]

    dmod_all = small_all[:, 0:6, :].reshape(N_DEV, 6 * dd)
    dmod_cols = lax.dynamic_slice(dmod_all, (0, me * ncol_ada), (N_DEV, ncol_ada))
    g_ada = _ada_bwd(jnp.transpose(c_all), dmod_cols, "ada_bwd")
    ada_res = _adamw(w_ada[0], m_w_ada[0], v_w_ada[0], [g_ada], "adamw_ada")
    res["w_ada"] = [r_[None] for r_ in ada_res]

    pair = [None, pair_lo, pair_ao, pair_o, pair_f1, pair_f2]
    from_chips = [None, chips_lo, chips_ao, chips_o, chips_f1, chips_f2]

    def summed(i):
        return [pair[i], from_chips[i], from_chips[i], from_chips[i]], own_and_chips

    def sum_only(i, name):
        parts, index = summed(i)
        return _sum_parts(parts, index, name)

    res["w_in"] = [r_[None] for r_ in w_in_res]
    g_ff1 = jnp.transpose(sum_only(4, "sum_w_ff1"))
    res["w_ff1"] = [r_[None] for r_ in _adamw(w_ff1[0], m_w_ff1[0], v_w_ff1[0], [g_ff1], "adamw_w_ff1")]
    for i, (nm, w_, m_, v_) in {1: ("w_lru_out", w_lru_out, m_w_lru_out, v_w_lru_out),
                                2: ("w_attn_out", w_attn_out, m_w_attn_out, v_w_attn_out),
                                3: ("w_out", w_out, m_w_out, v_w_out),
                                5: ("w_ff2", w_ff2, m_w_ff2, v_w_ff2)}.items():
        parts, index = summed(i)
        res[nm] = [r_[None] for r_ in _adamw(w_[0], m_[0], v_[0], parts, "adamw_" + nm, part_index=index)]

    order = ["w_ada", "b_ada", "norm1_g", "w_in", "conv_w", "conv_b", "lru_wa", "lru_ba", "lru_wx", "lru_bx",
             "lru_lambda", "w_lru_out", "w_attn_out", "attn_sinks", "rel_bias", "w_out", "norm2_g", "w_ff1",
             "w_ff2", "final_g"]
    out = [loss, grad_x[None]]
    for q_ in range(4):
        out += [res[k_][q_] for k_ in order]
    return tuple(out)
```

```python
import functools
import math

import numpy as np
import jax
import jax.numpy as jnp
from jax import lax
from jax.experimental import pallas as pl
from jax.experimental.pallas import tpu as pltpu

F32 = jnp.float32
BF16 = jnp.bfloat16
MESH = pl.DeviceIdType.MESH

D_MODEL = 2048
N_Q_HEADS = 32
N_KV_HEADS = 4
GROUP = N_Q_HEADS // N_KV_HEADS
HEAD_DIM = 64
KV_WIDTH = N_KV_HEADS * HEAD_DIM
BLOCK = 128
NEG_INF = -1e30
N_BUCKETS = 32
MAX_DISTANCE = 128
LRU_BLOCKS = 16
LRU_BLOCK_W = 128
CONV_WIDTH = 4
LRU_C = 8.0
D_FF = 4 * D_MODEL
EPS = 1e-6
IN_WIDTH = 5 * D_MODEL + 2 * KV_WIDTH
OFF_LX, OFF_LG, OFF_Q, OFF_K, OFF_V, OFF_GA, OFF_GB = 0, 2048, 4096, 6144, 6400, 6656, 8704

ADAM_LR, ADAM_B1, ADAM_B2, ADAM_EPS, ADAM_WD, ADAM_STEP = 0.001, 0.9, 0.999, 1e-08, 0.01, 10

N_DEV = 8
VMEM_LIMIT_BYTES = 56 * 1024 * 1024
LANE = 128
LRU_CW = 512


def _params(sem, **kw):
    return pltpu.CompilerParams(dimension_semantics=sem, vmem_limit_bytes=VMEM_LIMIT_BYTES, **kw)


def _tile(n, pref, mult=8):
    if n <= pref:
        return n
    t = (pref // mult) * mult
    while t >= mult:
        if n % t == 0:
            return t
        t -= mult
    return n


def _sigmoid(x):
    return 0.5 * jnp.tanh(0.5 * x) + 0.5


def _gelu_tanh(x):
    k = math.sqrt(2.0 / math.pi)
    t = jnp.tanh(k * (x + 0.044715 * x * x * x))
    return 0.5 * x * (1.0 + t), t


def _gelu_tanh_grad(x, t):
    k = math.sqrt(2.0 / math.pi)
    return 0.5 * (1.0 + t) + 0.5 * x * (1.0 - t * t) * k * (1.0 + 3.0 * 0.044715 * x * x)


def _log1p(e):
    u = 1.0 + e
    return jnp.where(u == 1.0, e, jnp.log(u) * (e / jnp.where(u == 1.0, 1.0, u - 1.0)))


def _softplus(x):
    return jnp.maximum(x, 0.0) + _log1p(jnp.exp(-jnp.abs(x)))


def _neg_expm1(x, exp_x):
    series = -x * (1.0 + x * (0.5 + x * (1.0 / 6.0 + x * (1.0 / 24.0))))
    return jnp.where(x > -0.0625, series, 1.0 - exp_x)


def _my_place():
    return lax.axis_index("x"), lax.axis_index("y"), lax.axis_index("c")


def _all_gather(arrs, name):
    return _fill_own(_run_job(_gather_job(arrs), name), arrs)


def _fill_own(stacks, shards):
    me = 4 * lax.axis_index("x") + 2 * lax.axis_index("y") + lax.axis_index("c")
    out = []
    for g, s in zip(stacks, shards):
        if g.size * g.dtype.itemsize <= (1 << 20):
            slot = lax.broadcasted_iota(jnp.int32, g.shape, 0)
            out.append(jnp.where(slot == me, s[None], g))
        else:
            out.append(lax.dynamic_update_slice_in_dim(g, s[None], me, axis=0))
    return out


class _Job:
    def __init__(self, ins, outs, sems, start, finish, forwards=(), forward_at=()):
        self.ins, self.outs, self.sems = list(ins), list(outs), list(sems)
        self.start, self.finish, self.forwards, self.forward_at = start, finish, list(forwards), list(forward_at)
        assert len(self.forwards) <= len(self.forward_at)


def _gather_job(arrs, forward_at=(0.6, 0.85), part="all"):
    n = len(arrs)
    whole, near, far = part == "all", part == "near", part == "far"

    def copies(ins, outs, sems):
        send_sems, recv_sems = sems
        x, y, c = _my_place()
        sib = (x, y, 1 - c)
        xn, yn, dg = (1 - x, y), (x, 1 - y), (1 - x, 1 - y)

        def slot(px, py, pc):
            return 4 * px + 2 * py + pc

        def cp(a, k, src, dst_slot, to):
            return pltpu.make_async_remote_copy(
                src_ref=src, dst_ref=outs[a].at[dst_slot], send_sem=send_sems.at[a, k], recv_sem=recv_sems.at[a, k],
                device_id=to, device_id_type=MESH)

        def arrival(a, k, dst_slot):
            return cp(a, k, outs[a].at[dst_slot], dst_slot, sib)

        me_slot = slot(x, y, c)
        sends, relay, passes, arrive = {}, {}, {}, {}
        for a in range(n):
            if whole:
                own = ins[a]
                sends[a, 0] = cp(a, 0, own, me_slot, sib)
            if near:
                own = ins[a].at[c]
            if whole or near:
                sends[a, 1] = cp(a, 1, own, me_slot, (*xn, c))
                sends[a, 2] = cp(a, 2, own, me_slot, (*yn, c))
                passes[a, 4] = cp(a, 4, outs[a].at[slot(*xn, c)], slot(*xn, c), sib)
                passes[a, 5] = cp(a, 5, outs[a].at[slot(*yn, c)], slot(*yn, c), sib)
            if whole or far:
                source = outs[a] if whole else ins[a]
                relayed = c * slot(*yn, c) + (1 - c) * slot(*xn, c)
                relay[a, 3] = cp(a, 3, source.at[relayed], relayed, (x + c * (1 - 2 * x), y + (1 - c) * (1 - 2 * y), c))
                passes[a, 6] = cp(a, 6, outs[a].at[slot(*dg, c)], slot(*dg, c), sib)
            arrive[a, 0] = arrival(a, 0, slot(x, y, 1 - c))
            arrive[a, 1] = arrival(a, 1, slot(*xn, c))
            arrive[a, 2] = arrival(a, 2, slot(*yn, c))
            arrive[a, 3] = arrival(a, 3, slot(*dg, c))
            arrive[a, 4] = arrival(a, 4, slot(*xn, 1 - c))
            arrive[a, 5] = arrival(a, 5, slot(*yn, 1 - c))
            arrive[a, 6] = arrival(a, 6, slot(*dg, 1 - c))
        return sends, relay, passes, arrive

    def pick(d, ks):
        return [d[a, k] for a in range(n) for k in ks if (a, k) in d]

    def start(ins, outs, sems):
        sends, relay, _, _ = copies(ins, outs, sems)
        for cp in pick(sends, (0, 1, 2)) + (pick(relay, (3,)) if far else []):
            cp.start()

    def forward_neighbours(ins, outs, sems):
        _, relay, passes, arrive = copies(ins, outs, sems)
        for cp in pick(arrive, (1, 2)):
            cp.wait_recv()
        for cp in pick(relay, (3,)) + pick(passes, (4, 5)):
            cp.start()

    def forward_diagonal(ins, outs, sems):
        _, _, passes, arrive = copies(ins, outs, sems)
        for cp in pick(arrive, (3,)):
            cp.wait_recv()
        for cp in pick(passes, (6,)):
            cp.start()

    def finish(ins, outs, sems):
        sends, relay, passes, arrive = copies(ins, outs, sems)
        for cp in pick(arrive, (6,) if far else (0, 4, 5, 6) if whole else (4, 5)):
            cp.wait_recv()
        for cp in pick(sends, (0, 1, 2)) + pick(relay, (3,)) + pick(passes, (4, 5, 6)):
            cp.wait_send()

    shapes = [a.shape[-2:] for a in arrs]
    forwards = [forward_diagonal] if far else [forward_neighbours] if near else [forward_neighbours, forward_diagonal]
    return _Job(arrs, [jax.ShapeDtypeStruct((N_DEV,) + s, a.dtype) for s, a in zip(shapes, arrs)],
                [pltpu.SemaphoreType.DMA((n, 7)), pltpu.SemaphoreType.DMA((n, 7))],
                start, finish, forwards, forward_at)


def _pair_job(arrs):
    n = len(arrs)

    def copies(ins, outs, sems):
        send_sems, recv_sems = sems
        x, y, c = _my_place()
        remote = [pltpu.make_async_remote_copy(
            src_ref=ins[a], dst_ref=outs[a].at[c], send_sem=send_sems.at[a], recv_sem=recv_sems.at[a],
            device_id=(x, y, 1 - c), device_id_type=MESH) for a in range(n)]
        arrive = [pltpu.make_async_remote_copy(
            src_ref=ins[a], dst_ref=outs[a].at[1 - c], send_sem=send_sems.at[a], recv_sem=recv_sems.at[a],
            device_id=(x, y, 1 - c), device_id_type=MESH) for a in range(n)]
        return remote, arrive

    def start(ins, outs, sems):
        for cp in copies(ins, outs, sems)[0]:
            cp.start()

    def finish(ins, outs, sems):
        remote, arrive = copies(ins, outs, sems)
        for cp in arrive:
            cp.wait_recv()
        for cp in remote:
            cp.wait_send()

    return _Job(arrs, [jax.ShapeDtypeStruct((2,) + a.shape, a.dtype) for a in arrs],
                [pltpu.SemaphoreType.DMA((n,)), pltpu.SemaphoreType.DMA((n,))], start, finish)


def _run_job(job, name):
    ni, no = len(job.ins), len(job.outs)

    def body(*refs):
        ins, outs, sems = refs[:ni], refs[ni:ni + no], refs[ni + no:]
        job.start(ins, outs, sems)
        for fwd in job.forwards:
            fwd(ins, outs, sems)
        job.finish(ins, outs, sems)

    any_spec = pl.BlockSpec(memory_space=pl.ANY)
    return pl.pallas_call(
        body, name=name, out_shape=job.outs, in_specs=[any_spec] * ni, out_specs=[any_spec] * no,
        scratch_shapes=job.sems,
    )(*job.ins)


def _call(body, *, name, grid, in_specs, out_specs, out_shape, scratch_shapes=(), sem, args, jobs=(),
          prefetch=(), aliases=None):
    in_specs, out_specs, out_shape, scratch_shapes = list(in_specs), list(out_specs), list(out_shape), list(scratch_shapes)
    prefetch = list(prefetch)
    if not jobs and not prefetch and not aliases:
        res = pl.pallas_call(body, name=name, grid=grid, in_specs=in_specs, out_specs=out_specs, out_shape=out_shape,
                             scratch_shapes=scratch_shapes, compiler_params=_params(sem))(*args)
        return list(res), []
    n_in, n_out, n_scr = len(in_specs), len(out_specs), len(scratch_shapes)
    j_in = [len(j.ins) for j in jobs]
    j_out = [len(j.outs) for j in jobs]
    j_sem = [len(j.sems) for j in jobs]
    total = int(np.prod(grid))
    any_spec = pl.BlockSpec(memory_space=pl.ANY)

    def wrapped(*refs):
        pos = [0]

        def take(k):
            part = refs[pos[0]:pos[0] + k]
            pos[0] += k
            return part

        take(len(prefetch))
        ins = take(n_in)
        jins = [take(k) for k in j_in]
        outs = take(n_out)
        jouts = [take(k) for k in j_out]
        scr = take(n_scr)
        jsems = [take(k) for k in j_sem]
        step = pl.program_id(0)
        for d in range(1, len(grid)):
            step = step * grid[d] + pl.program_id(d)
        for j, job in enumerate(jobs):
            pl.when(step == 0)(functools.partial(job.start, jins[j], jouts[j], jsems[j]))
            for fwd, frac in zip(job.forwards, job.forward_at):
                at = min(int(total * frac), total - 1)
                pl.when(step == at)(functools.partial(fwd, jins[j], jouts[j], jsems[j]))
        body(*ins, *outs, *scr)
        for j, job in enumerate(jobs):
            pl.when(step == total - 1)(functools.partial(job.finish, jins[j], jouts[j], jsems[j]))

    grid_spec = pltpu.PrefetchScalarGridSpec(
        num_scalar_prefetch=len(prefetch), grid=grid,
        in_specs=in_specs + [any_spec] * sum(j_in),
        out_specs=out_specs + [any_spec] * sum(j_out),
        scratch_shapes=scratch_shapes + [s for j in jobs for s in j.sems])
    res = pl.pallas_call(
        wrapped, name=name, grid_spec=grid_spec,
        out_shape=out_shape + [o for j in jobs for o in j.outs],
        input_output_aliases={len(prefetch) + i: o for i, o in (aliases or {}).items()},
        compiler_params=_params(("arbitrary",) * len(grid)),
    )(*prefetch, *args, *[a for j in jobs for a in j.ins])
    res = list(res)
    own, rest = res[:n_out], res[n_out:]
    per_job = []
    for k in j_out:
        per_job.append(rest[:k])
        rest = rest[k:]
    return own, per_job


def _exchange_sibling(arrs, name):
    return _run_job(_sibling_job(arrs), name)


def _sibling_job(arrs):
    n = len(arrs)

    def copies(ins, outs, sems):
        send_sems, recv_sems = sems
        x, y, c = _my_place()
        return [pltpu.make_async_remote_copy(
            src_ref=ins[a].at[2 * k + 1 - c], dst_ref=outs[a].at[k],
            send_sem=send_sems.at[a, k], recv_sem=recv_sems.at[a, k],
            device_id=(x, y, 1 - c), device_id_type=MESH) for a in range(n) for k in range(4)]

    def start(ins, outs, sems):
        for cp in copies(ins, outs, sems):
            cp.start()

    def finish(ins, outs, sems):
        for cp in copies(ins, outs, sems):
            cp.wait()

    return _Job(arrs, [jax.ShapeDtypeStruct((4,) + a.shape[1:], a.dtype) for a in arrs],
                [pltpu.SemaphoreType.DMA((n, 4)), pltpu.SemaphoreType.DMA((n, 4))], start, finish)


def _chips_job(arrs):
    n = len(arrs)

    def copies(ins, outs, sems):
        send_sems, recv_sems = sems
        x, y, c = _my_place()
        chips = [(1 - x, y), (x, 1 - y), (1 - x, 1 - y)]
        return [pltpu.make_async_remote_copy(
            src_ref=ins[a].at[2 * px + py], dst_ref=outs[a].at[j],
            send_sem=send_sems.at[a, j], recv_sem=recv_sems.at[a, j],
            device_id=(px, py, c), device_id_type=MESH) for a in range(n) for j, (px, py) in enumerate(chips)]

    def start(ins, outs, sems):
        for cp in copies(ins, outs, sems):
            cp.start()

    def finish(ins, outs, sems):
        for cp in copies(ins, outs, sems):
            cp.wait()

    return _Job(arrs, [jax.ShapeDtypeStruct((3,) + a.shape[1:], a.dtype) for a in arrs],
                [pltpu.SemaphoreType.DMA((n, 3)), pltpu.SemaphoreType.DMA((n, 3))], start, finish)


def _pair_add(stack, from_sibling, c_idx, name, widths=None):
    _, r, cc = stack.shape
    tr = _tile(r, 512)
    ws = [cc] if widths is None else list(widths)
    assert sum(ws) == cc
    starts = [sum(ws[:q]) for q in range(len(ws))]

    def body(c_ref, a_ref, b_ref, *o_refs):
        res = a_ref[...].astype(F32) + b_ref[...].astype(F32)
        for o_ref, s0, w_ in zip(o_refs, starts, ws):
            o_ref[...] = res[:, s0:s0 + w_].astype(o_ref.dtype)

    grid_spec = pltpu.PrefetchScalarGridSpec(
        num_scalar_prefetch=1, grid=(4, r // tr),
        in_specs=[pl.BlockSpec((None, tr, cc), lambda k, i, c_ref: (2 * k + c_ref[0], i, 0)),
                  pl.BlockSpec((None, tr, cc), lambda k, i, c_ref: (k, i, 0))],
        out_specs=[pl.BlockSpec((None, tr, w_), lambda k, i, c_ref: (k, i, 0)) for w_ in ws])
    res = pl.pallas_call(
        body, name=name, grid_spec=grid_spec,
        out_shape=[jax.ShapeDtypeStruct((4, r, w_), BF16) for w_ in ws],
        compiler_params=_params(("parallel", "parallel")),
    )(c_idx, stack, from_sibling)
    return res[0] if widths is None else list(res)


_DIMS = {"nn": ((1,), (0,)), "nt": ((1,), (1,)), "tn": ((0,), (0,))}


def _mm(a, b, mode, *, tm, tn, tk=None, outs, epi=None, tiles=(), rows=(), name, jobs=(), chunks=None, into=None,
        n_total=None, rows_part=None, wide=None, ring=False):
    wide = wide or {}
    if mode == "tn":
        kk, m = a.shape
    else:
        m, kk = a.shape
    n_all = b.shape[0] if mode == "nt" else b.shape[1]
    nn = n_all if chunks is None else chunks[0].shape[0] * chunks[1]
    n_all = n_all if n_total is None else n_total
    tk = kk if tk is None else tk
    assert m % tm == 0 and nn % tn == 0 and kk % tk == 0, (name, a.shape, b.shape, tm, tn, tk)
    nk = kk // tk
    nt, nr, no = len(tiles), len(rows), len(outs)
    n_into = 0 if into is None else 1

    n_col_blocks = nn // tn
    n_steps = (m // tm) * n_col_blocks

    def body(*refs):
        a_ref, b_ref = refs[:2]
        tile_refs = refs[2:2 + nt]
        row_refs = refs[2 + nt:2 + nt + nr]
        out_refs = refs[2 + nt + nr + n_into:2 + nt + nr + n_into + no]
        if ring:
            slots, slot_sems = refs[-2], refs[-1]
            step = pl.program_id(0) * n_col_blocks + pl.program_id(1)

            def fetch(of_step, q):
                first = pl.multiple_of((of_step % n_col_blocks) * tn, tn)
                return pltpu.make_async_copy(b_ref.at[:, pl.ds(first, tn)], slots.at[q], slot_sems.at[q])

            for q in range(3):
                @pl.when(jnp.logical_and(step == 0, q < n_steps))
                def _(q=q):
                    fetch(q, q).start()

                @pl.when(jnp.logical_and(jnp.logical_and(step > 0, step + 2 < n_steps), (step + 2) % 3 == q))
                def _(q=q):
                    fetch(step + 2, q).start()
        else:
            part = lax.dot_general(a_ref[...], b_ref[...], (_DIMS[mode], ((), ())), preferred_element_type=F32)

        def finish(acc):
            if epi is None:
                res = (acc,)
            else:
                res = epi(acc, *[t[...] for t in tile_refs], *[r[...] for r in row_refs])
            for o_ref, val in zip(out_refs, res):
                o_ref[...] = val.astype(o_ref.dtype)

        if ring:
            for q in range(3):
                @pl.when(step % 3 == q)
                def _(q=q):
                    fetch(step, q).wait()
                    finish(jnp.dot(a_ref[...], slots[q], preferred_element_type=F32))
        elif nk == 1:
            finish(part)
        else:
            acc_ref = refs[-1]
            k = pl.program_id(2)

            @pl.when(k == 0)
            def _():
                acc_ref[...] = part

            @pl.when(k > 0)
            def _():
                acc_ref[...] += part

            @pl.when(k == nk - 1)
            def _():
                finish(acc_ref[...])

    if chunks is None:
        col = b_row = lambda j, *_: j
        prefetch = []
    else:
        assert mode == "nt" and not tiles and not rows and chunks[1] % tn == 0
        per = chunks[1] // tn
        col = lambda j, ids, b_ids: ids[j // per] * per + j % per
        b_row = lambda j, ids, b_ids: b_ids[j // per] * per + j % per
        prefetch = [chunks[0], chunks[2]]
    row0, n_row_blocks = (0, m // tm) if rows_part is None else rows_part
    assert rows_part is None or (mode != "tn" and not tiles)
    if mode == "tn":
        a_spec = pl.BlockSpec((tk, tm), lambda i, j, k, *_: (k, i))
    else:
        a_spec = pl.BlockSpec((tm, tk), lambda i, j, k, *_: (i + row0, k))
    if mode == "nt":
        b_spec = pl.BlockSpec((tn, tk), lambda i, j, k, *s: (b_row(j, *s), k))
    else:
        b_spec = pl.BlockSpec((tk, tn), lambda i, j, k, *_: (k, j))
    scratch = [pltpu.VMEM((tm, tn), F32)] if nk > 1 else []
    if ring:
        assert mode == "nn" and nk == 1 and chunks is None and rows_part is None
        b_spec = pl.BlockSpec(memory_space=pl.ANY)
        scratch = [pltpu.VMEM((3, kk, tn), b.dtype), pltpu.SemaphoreType.DMA((3,))]
    tile_specs = [pl.BlockSpec((tm, tn), functools.partial(lambda i, j, k, off: (i, j + off), off=off))
                  for _, off in tiles]
    row_specs = [pl.BlockSpec((r.shape[0], tn), lambda i, j, k: (0, j)) for r in rows]
    out_spec = pl.BlockSpec((tm, tn), lambda i, j, k, *s: (i + row0, col(j, *s)))
    args = [a, b, *[t for t, _ in tiles], *rows]
    in_specs = [a_spec, b_spec] + tile_specs + row_specs
    aliases = None
    if into is not None:
        aliases = {len(args): 0}
        args.append(into)
        in_specs.append(pl.BlockSpec(memory_space=pl.ANY))
    res, jres = _call(
        body, name=name, grid=(n_row_blocks, nn // tn, nk),
        in_specs=in_specs,
        out_specs=[out_spec if q not in wide else
                   pl.BlockSpec((tm, tn), functools.partial(lambda i, j, k, off: (i, j + off), off=wide[q][1]))
                   for q in range(no)],
        out_shape=[jax.ShapeDtypeStruct((m, wide[q][0] if q in wide else n_all), dt) for q, dt in enumerate(outs)],
        scratch_shapes=scratch,
        sem=("arbitrary",) * 3 if ring else ("parallel", "parallel", "arbitrary"),
        args=args, jobs=jobs, prefetch=prefetch, aliases=aliases)
    return (res, jres) if jobs else res


def _ada_fwd(c_all, w_ada, b_ada_cols, name):
    nb, dd = c_all.shape
    ncol = w_ada.shape[1]
    tn = _tile(ncol, 512, LANE)

    def body(c_ref, w_ref, b_ref, o_ref):
        cv = c_ref[...]
        act = (cv * _sigmoid(cv)).astype(BF16)
        o_ref[...] = jnp.dot(act, w_ref[...].astype(BF16), preferred_element_type=F32) + b_ref[...]

    return pl.pallas_call(
        body, name=name, grid=(ncol // tn,),
        in_specs=[pl.BlockSpec((nb, dd), lambda j: (0, 0)), pl.BlockSpec((dd, tn), lambda j: (0, j)),
                  pl.BlockSpec((1, tn), lambda j: (0, j))],
        out_specs=pl.BlockSpec((nb, tn), lambda j: (0, j)),
        out_shape=jax.ShapeDtypeStruct((nb, ncol), F32),
        compiler_params=_params(("parallel",)),
    )(c_all, w_ada, b_ada_cols)


def _ada_bwd(c_all_t, dmod_cols, name):
    dd, nb = c_all_t.shape
    ncol = dmod_cols.shape[1]
    tr = _tile(dd, 256)

    def body(c_ref, d_ref, o_ref):
        cv = c_ref[...]
        act = (cv * _sigmoid(cv)).astype(BF16).astype(F32)
        dm = d_ref[...].astype(BF16).astype(F32)
        acc = act[:, 0:1] * dm[0:1, :]
        for bi in range(1, nb):
            acc = acc + act[:, bi:bi + 1] * dm[bi:bi + 1, :]
        o_ref[...] = acc

    return pl.pallas_call(
        body, name=name, grid=(dd // tr,),
        in_specs=[pl.BlockSpec((tr, nb), lambda i: (i, 0)), pl.BlockSpec((nb, ncol), lambda i: (0, 0))],
        out_specs=pl.BlockSpec((tr, ncol), lambda i: (i, 0)),
        out_shape=jax.ShapeDtypeStruct((dd, ncol), F32),
        compiler_params=_params(("parallel",)),
    )(c_all_t, dmod_cols)


def _norm_mod(x, gain, mod, row0, name):
    t, dd = x.shape
    tt = _tile(t, 512)

    def body(x_ref, g_ref, m_ref, h_ref):
        xv = x_ref[...]
        r = lax.rsqrt(jnp.mean(xv * xv, axis=-1, keepdims=True) + EPS)
        shift, scale = m_ref[row0:row0 + 1, :], m_ref[row0 + 1:row0 + 2, :]
        h_ref[...] = ((xv * r * g_ref[...]) * (1.0 + scale) + shift).astype(BF16)

    return pl.pallas_call(
        body, name=name, grid=(t // tt,),
        in_specs=[pl.BlockSpec((tt, dd), lambda i: (i, 0)), pl.BlockSpec((1, dd), lambda i: (0, 0)),
                  pl.BlockSpec((8, dd), lambda i: (0, 0))],
        out_specs=pl.BlockSpec((tt, dd), lambda i: (i, 0)),
        out_shape=jax.ShapeDtypeStruct((t, dd), BF16),
        compiler_params=_params(("parallel",)),
    )(x, gain, mod)


def _norm_mod_bwd(x, gain, mod, row0, dh, dres, branch, name, jobs=()):
    t, dd = x.shape
    tt = _tile(t, 512)
    has_branch = branch is not None

    def body(*refs):
        if has_branch:
            x_ref, g_ref, m_ref, dh_ref, dres_ref, o_ref, dx_ref, st_ref, do_ref = refs[:-1]
        else:
            x_ref, g_ref, m_ref, dh_ref, dres_ref, dx_ref, st_ref = refs[:-1]

        acc_ref = refs[-1]
        step_i = pl.program_id(0)

        @pl.when(step_i == 0)
        def _():
            acc_ref[...] = jnp.zeros_like(acc_ref)

        def fold(v):
            out = v[0:8]
            for q in range(1, rc // 8):
                out = out + v[8 * q:8 * q + 8]
            return out

        def chunk(ci, carry):
            rows = pl.ds(pl.multiple_of(ci * rc, rc), rc)
            xv = x_ref[rows, :]
            g = g_ref[...]
            r = lax.rsqrt(jnp.mean(xv * xv, axis=-1, keepdims=True) + EPS)
            nv = xv * r
            dhv = dh_ref[rows, :].astype(F32)
            acc_ref[0] += fold(dhv)
            acc_ref[1] += fold(dhv * (nv * g))
            dng = dhv * (1.0 + m_ref[row0 + 1:row0 + 2, :])
            acc_ref[2] += fold(dng * nv)
            dn = dng * g
            dx = dres_ref[rows, :] + r * (dn - nv * jnp.mean(dn * nv, axis=-1, keepdims=True))
            dx_ref[rows, :] = dx
            if has_branch:
                acc_ref[3] += fold(dx * o_ref[rows, :].astype(F32))
                do_ref[rows, :] = (dx * m_ref[branch[1]:branch[1] + 1, :]).astype(BF16)
            return carry

        lax.fori_loop(0, tt // rc, chunk, 0, unroll=4)

        @pl.when(step_i == nsteps - 1)
        def _():
            st_ref[...] = jnp.zeros_like(st_ref)
            for k in range(4):
                st_ref[k:k + 1, :] = jnp.sum(acc_ref[k], axis=0, keepdims=True)

    rc = 16 if tt % 16 == 0 else 8
    nsteps = t // tt
    row = pl.BlockSpec((tt, dd), lambda i: (i, 0))
    in_specs = [row, pl.BlockSpec((1, dd), lambda i: (0, 0)), pl.BlockSpec((8, dd), lambda i: (0, 0)), row, row]
    args = [x, gain, mod, dh, dres]
    out_specs = [row, pl.BlockSpec((8, dd), lambda i: (0, 0))]
    out_shape = [jax.ShapeDtypeStruct((t, dd), F32), jax.ShapeDtypeStruct((8, dd), F32)]
    if has_branch:
        in_specs.append(row)
        args.append(branch[0])
        out_specs.append(row)
        out_shape.append(jax.ShapeDtypeStruct((t, dd), BF16))
    res, jres = _call(body, name=name, grid=(t // tt,), in_specs=in_specs, out_specs=out_specs, out_shape=out_shape,
                      scratch_shapes=[pltpu.VMEM((4, 8, dd), F32)], sem=("arbitrary",), args=args, jobs=jobs)
    return (res, jres) if jobs else res


def _final_loss(x2, o2, final_g, target, mod, name):
    t, dd = x2.shape
    tt = _tile(t, 256)
    nsteps = t // tt

    def body(x_ref, o_ref, g_ref, t_ref, m_ref, loss_ref, dx_ref, do_ref, st_ref, lacc):
        i = pl.program_id(0)

        @pl.when(i == 0)
        def _():
            st_ref[...] = jnp.zeros_like(st_ref)
            lacc[...] = jnp.zeros_like(lacc)

        xv = x_ref[...]
        g = g_ref[...]
        r = lax.rsqrt(jnp.mean(xv * xv, axis=-1, keepdims=True) + EPS)
        nv = xv * r
        err = nv * g - t_ref[...]
        lacc[...] += jnp.sum(err * err, axis=0, keepdims=True)
        dy = err * (1.0 / dd)
        st_ref[0:1, :] += jnp.sum(dy * nv, axis=0, keepdims=True)
        dn = dy * g
        dx = r * (dn - nv * jnp.mean(dn * nv, axis=-1, keepdims=True))
        dx_ref[...] = dx
        st_ref[1:2, :] += jnp.sum(dx * o_ref[...].astype(F32), axis=0, keepdims=True)
        do_ref[...] = (dx * m_ref[5:6, :]).astype(BF16)

        @pl.when(i == nsteps - 1)
        def _():
            total = jnp.sum(lacc[...], axis=-1, keepdims=True) * (0.5 / dd)
            loss_ref[...] = jnp.broadcast_to(total, loss_ref.shape)

    row = pl.BlockSpec((tt, dd), lambda i: (i, 0))
    return pl.pallas_call(
        body, name=name, grid=(nsteps,),
        in_specs=[row, row, pl.BlockSpec((1, dd), lambda i: (0, 0)), row, pl.BlockSpec((8, dd), lambda i: (0, 0))],
        out_specs=[pl.BlockSpec((8, LANE), lambda i: (0, 0)), row, row, pl.BlockSpec((8, dd), lambda i: (0, 0))],
        out_shape=[jax.ShapeDtypeStruct((8, LANE), F32), jax.ShapeDtypeStruct((t, dd), F32),
                   jax.ShapeDtypeStruct((t, dd), BF16), jax.ShapeDtypeStruct((8, dd), F32)],
        scratch_shapes=[pltpu.VMEM((1, dd), F32)],
        compiler_params=_params(("arbitrary",)),
    )(x2, o2, final_g, target, mod)


def _lru_gates(xt, prev8, lp, wa_ref, wx_ref, first_tile, cw):
    tt = xt.shape[0]
    ext = jnp.concatenate([prev8, xt], axis=0)
    xs = [pltpu.roll(ext, s, 0)[8:, :] if s else xt for s in range(CONV_WIDTH)]
    xc = lp[0:1, :] + lp[7:8, :] * xs[0] + lp[6:7, :] * xs[1] + lp[5:6, :] * xs[2] + lp[4:5, :] * xs[3]
    xcb = xc.astype(BF16)
    za, zi = [], []
    for jb in range(cw // LRU_BLOCK_W):
        sl = slice(jb * LRU_BLOCK_W, (jb + 1) * LRU_BLOCK_W)
        za.append(jnp.dot(xcb[:, sl], wa_ref[jb].astype(BF16), preferred_element_type=F32))
        zi.append(jnp.dot(xcb[:, sl], wx_ref[jb].astype(BF16), preferred_element_type=F32))
    ra = _sigmoid(jnp.concatenate(za, axis=1) + lp[1:2, :])
    ri = _sigmoid(jnp.concatenate(zi, axis=1) + lp[2:3, :])
    sp = _softplus(-lp[3:4, :])
    log_a = -LRU_C * ra * sp
    av = jnp.exp(log_a)
    seq_start = jnp.logical_and(first_tile, lax.broadcasted_iota(jnp.int32, (tt, 1), 0) == 0)
    m2 = _neg_expm1(2.0 * log_a, av * av)
    inv_mult = jnp.where(seq_start, 1.0, lax.rsqrt(jnp.maximum(m2, 1e-30)))
    mult = jnp.where(seq_start, 1.0, m2 * inv_mult)
    return xs, xc, xcb, ra, ri, sp, av, mult, inv_mult, seq_start


def _scan_groups(a, u, reverse):
    tt, cw = a.shape
    a3, u3 = a.reshape(tt // 8, 8, cw), u.reshape(tt // 8, 8, cw)
    r = lax.broadcasted_iota(jnp.int32, (1, 8, 1), 1)
    for s in (1, 2, 4):
        shift = 8 - s if reverse else s
        take = r < 8 - s if reverse else r >= s
        u3 = jnp.where(take, a3 * pltpu.roll(u3, shift, 1) + u3, u3)
        a3 = jnp.where(take, a3 * pltpu.roll(a3, shift, 1), a3)
    return a3, u3


def _scan_carry(a_s, u_s, h_s, carry, reverse):
    ng = a_s.shape[0]

    def step(i, h):
        g = ng - 1 - i if reverse else i
        hg = u_s[g] + a_s[g] * h
        h_s[g] = hg
        return hg[0:1, :] if reverse else hg[7:8, :]

    return lax.fori_loop(0, ng, step, carry, unroll=4)


def _lru_fwd(proj, lp, wa, wx, name, jobs=()):
    t = proj.shape[0]
    w = D_MODEL
    cw = LRU_CW
    tt = _tile(t, 256)
    nb = cw // LRU_BLOCK_W

    def body(lx_ref, lg_ref, lp_ref, wa_ref, wx_ref, rec_ref, ya_ref, a_s, u_s, h_s, halo, carry):
        ti = pl.program_id(1)

        @pl.when(ti == 0)
        def _():
            halo[...] = jnp.zeros_like(halo)
            carry[...] = jnp.zeros_like(carry)

        xt = lx_ref[...].astype(F32)
        lp_v = lp_ref[...]
        _, xc, _, _, ri, _, av, mult, _, _ = _lru_gates(xt, halo[...], lp_v, wa_ref, wx_ref, ti == 0, cw)
        halo[...] = xt[tt - 8:, :]
        a_s[...], u_s[...] = _scan_groups(av, mult * (ri * xc), False)
        carry[...] = _scan_carry(a_s, u_s, h_s, carry[...], False)
        rec = h_s[...].reshape(tt, cw)
        rec_ref[...] = rec.astype(BF16)
        gelu, _ = _gelu_tanh(lg_ref[...].astype(F32))
        ya_ref[...] = (rec * gelu).astype(BF16)

    off_g = OFF_LG // cw
    res, jres = _call(
        body, name=name, grid=(w // cw, t // tt),
        in_specs=[pl.BlockSpec((tt, cw), lambda ci, ti: (ti, ci)),
                  pl.BlockSpec((tt, cw), lambda ci, ti: (ti, off_g + ci)),
                  pl.BlockSpec((8, cw), lambda ci, ti: (0, ci)),
                  pl.BlockSpec((nb, LRU_BLOCK_W, LRU_BLOCK_W), lambda ci, ti: (ci, 0, 0)),
                  pl.BlockSpec((nb, LRU_BLOCK_W, LRU_BLOCK_W), lambda ci, ti: (ci, 0, 0))],
        out_specs=[pl.BlockSpec((tt, cw), lambda ci, ti: (ti, ci))] * 2,
        out_shape=[jax.ShapeDtypeStruct((t, w), BF16)] * 2,
        scratch_shapes=[pltpu.VMEM((tt // 8, 8, cw), F32)] * 3 + [pltpu.VMEM((8, cw), F32), pltpu.VMEM((1, cw), F32)],
        sem=("parallel", "arbitrary"), args=[proj, proj, lp, wa, wx], jobs=jobs)
    return (res, jres) if jobs else res


def _lru_bwd(proj, rec, dya, lp, wa, wx, name, jobs=(), into=None):
    t = proj.shape[0]
    w = D_MODEL
    cw = LRU_CW if into is None else w
    tt = _tile(t, 256 if into is None else 128)
    nt = t // tt
    nb = cw // LRU_BLOCK_W
    r8 = tt // 8

    def body(lx_ref, lxp_ref, lg_ref, rec_ref, recp_ref, dya_ref, lp_ref, wa_ref, wx_ref, *rest):
        if into is None:
            dlx_ref, dlg_ref = rest[0], rest[1]
        else:
            dlx_ref, dlg_ref = rest[1].at[:, 0:cw], rest[1].at[:, cw:2 * cw]
        st_ref, dwa_ref, dwx_ref, a_s, d_s, dh_s, nhalo, carry = rest[-8:]
        step_i = pl.program_id(1)
        ti = nt - 1 - step_i

        @pl.when(step_i == 0)
        def _():
            st_ref[...] = jnp.zeros_like(st_ref)
            dwa_ref[...] = jnp.zeros_like(dwa_ref)
            dwx_ref[...] = jnp.zeros_like(dwx_ref)
            nhalo[...] = jnp.zeros_like(nhalo)
            carry[...] = jnp.zeros_like(carry)

        first = ti == 0
        keep = jnp.where(first, 0.0, 1.0)
        xt = lx_ref[...].astype(F32)
        prev8 = lxp_ref[...].astype(F32) * keep
        lp_v = lp_ref[...]
        xs, xc, xcb, ra, ri, sp, av, mult, inv_mult, seq_start = _lru_gates(xt, prev8, lp_v, wa_ref, wx_ref, first, cw)

        lg = lg_ref[...].astype(F32)
        gelu, th = _gelu_tanh(lg)
        dyav = dya_ref[...].astype(F32)
        recv = rec_ref[...].astype(F32)
        dlg_ref[...] = (dyav * recv * _gelu_tanh_grad(lg, th)).astype(BF16)

        drec = dyav * gelu
        e_in = carry[...]
        a_s[...], d_s[...] = _scan_groups(av, av * drec, True)
        carry[...] = _scan_carry(a_s, d_s, dh_s, e_in, True)
        e_next = jnp.concatenate([dh_s[...].reshape(tt, cw), jnp.broadcast_to(e_in, (8, cw))], axis=0)
        dh = drec + pltpu.roll(e_next, tt + 7, 0)[:tt, :]
        rprev8 = recp_ref[...].astype(F32) * keep
        hprev = pltpu.roll(jnp.concatenate([rprev8, recv], axis=0), 1, 0)[8:, :]
        da = dh * hprev
        dmult = jnp.where(seq_start, 0.0, dh * ri * xc)
        dri = dh * mult * xc
        dxc = dh * mult * ri
        dlog_a = da * av - dmult * (av * av) * inv_mult
        dra = dlog_a * (-LRU_C * sp)
        lam = lp_v[3:4, :]
        st_ref[3:4, :] += jnp.sum(dlog_a * ra, axis=0, keepdims=True) * (LRU_C * _sigmoid(-lam))
        dza = dra * ra * (1.0 - ra)
        dzi = dri * ri * (1.0 - ri)
        st_ref[1:2, :] += jnp.sum(dza, axis=0, keepdims=True)
        st_ref[2:3, :] += jnp.sum(dzi, axis=0, keepdims=True)
        dzab, dzib = dza.astype(BF16), dzi.astype(BF16)
        back = []
        for jb in range(nb):
            sl = slice(jb * LRU_BLOCK_W, (jb + 1) * LRU_BLOCK_W)
            dwa_ref[jb] += lax.dot_general(xcb[:, sl], dzab[:, sl], (_DIMS["tn"], ((), ())), preferred_element_type=F32)
            dwx_ref[jb] += lax.dot_general(xcb[:, sl], dzib[:, sl], (_DIMS["tn"], ((), ())), preferred_element_type=F32)
            back.append(
                lax.dot_general(dzab[:, sl], wa_ref[jb].astype(BF16), (_DIMS["nt"], ((), ())), preferred_element_type=F32)
                + lax.dot_general(dzib[:, sl], wx_ref[jb].astype(BF16), (_DIMS["nt"], ((), ())), preferred_element_type=F32))
        dxc = dxc + jnp.concatenate(back, axis=1)
        st_ref[0:1, :] += jnp.sum(dxc, axis=0, keepdims=True)
        for k in range(CONV_WIDTH):
            st_ref[4 + k:5 + k, :] += jnp.sum(dxc * xs[CONV_WIDTH - 1 - k], axis=0, keepdims=True)
        ext = jnp.concatenate([dxc, nhalo[...]], axis=0)
        dlx = lp_v[7:8, :] * dxc
        for s in range(1, CONV_WIDTH):
            dlx = dlx + lp_v[7 - s:8 - s, :] * pltpu.roll(ext, tt + 8 - s, 0)[:tt, :]
        dlx_ref[...] = dlx.astype(BF16)
        nhalo[...] = dxc[0:8, :]

    off_g = OFF_LG // cw
    tile = lambda off: pl.BlockSpec((tt, cw), lambda ci, si: (nt - 1 - si, off + ci))
    prev = lambda off: pl.BlockSpec((8, cw), lambda ci, si: (jnp.maximum((nt - 1 - si) * r8 - 1, 0), off + ci))
    wspec = pl.BlockSpec((nb, LRU_BLOCK_W, LRU_BLOCK_W), lambda ci, si: (ci, 0, 0))
    st_spec = pl.BlockSpec((8, cw), lambda ci, si: (0, ci))
    in_specs = [tile(0), prev(0), tile(off_g), tile(0), prev(0), tile(0), st_spec, wspec, wspec]
    args = [proj, proj, proj, rec, rec, dya, lp, wa, wx]
    if into is None:
        d_specs = [tile(0), tile(0)]
        d_shapes = [jax.ShapeDtypeStruct((t, w), BF16), jax.ShapeDtypeStruct((t, w), BF16)]
        aliases = None
    else:
        d_specs = [pl.BlockSpec((tt, 2 * cw), lambda ci, si: (nt - 1 - si, 0))]
        d_shapes = [jax.ShapeDtypeStruct(into.shape, BF16)]
        aliases = {len(args): 0}
        args.append(into)
        in_specs.append(pl.BlockSpec(memory_space=pl.ANY))
    res, jres = _call(
        body, name=name, grid=(w // cw, nt), in_specs=in_specs,
        out_specs=d_specs + [st_spec, wspec, wspec],
        out_shape=d_shapes + [jax.ShapeDtypeStruct((8, w), F32),
                              jax.ShapeDtypeStruct((LRU_BLOCKS, LRU_BLOCK_W, LRU_BLOCK_W), F32),
                              jax.ShapeDtypeStruct((LRU_BLOCKS, LRU_BLOCK_W, LRU_BLOCK_W), F32)],
        scratch_shapes=[pltpu.VMEM((tt // 8, 8, cw), F32)] * 3 + [pltpu.VMEM((8, cw), F32), pltpu.VMEM((1, cw), F32)],
        sem=("parallel", "arbitrary"), args=args, jobs=jobs, aliases=aliases)
    return (res, jres) if jobs else res


def _band_valid(first_block):
    qi = lax.broadcasted_iota(jnp.int32, (BLOCK, 2 * BLOCK), 0)
    ki = lax.broadcasted_iota(jnp.int32, (BLOCK, 2 * BLOCK), 1)
    rel = qi + BLOCK - ki
    valid = jnp.logical_and(rel >= 0, rel < BLOCK)
    return jnp.logical_and(valid, jnp.logical_or(ki >= BLOCK, jnp.logical_not(first_block)))


ATTN_STACK_BWD = 8


def _low_half():
    return lax.broadcasted_iota(jnp.int32, (1, LANE), 1) < HEAD_DIM


def _stack_heads(x, h0, ns):
    low = _low_half()
    parts = []
    for g in range(ns):
        h = h0 + g
        slab = x[:, (h // 2) * LANE:(h // 2 + 1) * LANE]
        parts.append(jnp.where(low if h % 2 == 0 else jnp.logical_not(low), slab, jnp.zeros_like(slab)))
    return jnp.concatenate(parts, axis=0)


def _unstack_heads(y, ns):
    low = _low_half()
    pairs = [jnp.where(low, y[(2 * j) * BLOCK:(2 * j + 1) * BLOCK], y[(2 * j + 1) * BLOCK:(2 * j + 2) * BLOCK])
             for j in range(ns // 2)]
    return pairs[0] if len(pairs) == 1 else jnp.concatenate(pairs, axis=1)


def _dup_kv(xall, kv):
    part = xall[:, kv * HEAD_DIM:(kv + 1) * HEAD_DIM]
    return jnp.concatenate([part, part], axis=1)


def _fold_halves(a):
    return a + pltpu.roll(a, HEAD_DIM, 1)


def _group_probs(qg, k2, bias_g, sink_g, lse_g, valid):
    s = lax.dot_general(qg, k2, (_DIMS["nt"], ((), ())), preferred_element_type=F32) * (HEAD_DIM ** -0.5)
    s = jnp.where(valid[None], s.reshape(bias_g.shape) + bias_g, NEG_INF)
    return jnp.exp(s - lse_g), jnp.exp(sink_g - lse_g)


def _group_sinks(s_ref, h0, ns):
    return jnp.concatenate([jnp.full((1, BLOCK, 1), s_ref[h0 + g], F32) for g in range(ns)], axis=0)


def _attn_specs(t):
    nblk = t // BLOCK
    q_spec = pl.BlockSpec((BLOCK, D_MODEL), lambda n: (n, OFF_Q // D_MODEL))
    kc = pl.BlockSpec((BLOCK, KV_WIDTH), lambda n: (n, OFF_K // KV_WIDTH))
    kp = pl.BlockSpec((BLOCK, KV_WIDTH), lambda n: (jnp.maximum(n - 1, 0), OFF_K // KV_WIDTH))
    vc = pl.BlockSpec((BLOCK, KV_WIDTH), lambda n: (n, OFF_V // KV_WIDTH))
    vp = pl.BlockSpec((BLOCK, KV_WIDTH), lambda n: (jnp.maximum(n - 1, 0), OFF_V // KV_WIDTH))
    bias_spec = pl.BlockSpec((N_Q_HEADS, BLOCK, 2 * BLOCK), lambda n: (0, 0, 0))
    sink_spec = pl.BlockSpec(memory_space=pltpu.SMEM)
    return nblk, [q_spec, kp, kc, vp, vc, bias_spec, sink_spec]


def _attn_fwd(proj, bias, sinks, name, jobs=()):
    t = proj.shape[0]
    nblk, in_specs = _attn_specs(t)

    def body(q_ref, kp_ref, kc_ref, vp_ref, vc_ref, b_ref, s_ref, o_ref, lse_ref):
        valid = _band_valid(pl.program_id(0) == 0)
        lane = lax.broadcasted_iota(jnp.int32, (1, LANE), 1)
        q = q_ref[...]
        kall = jnp.concatenate([kp_ref[...], kc_ref[...]], axis=0)
        vall = jnp.concatenate([vp_ref[...], vc_ref[...]], axis=0)
        outs = []
        lse = jnp.zeros((BLOCK, LANE), F32)
        for h in range(N_Q_HEADS):
            kv = h // GROUP
            kk = kall[:, kv * HEAD_DIM:(kv + 1) * HEAD_DIM]
            vv = vall[:, kv * HEAD_DIM:(kv + 1) * HEAD_DIM]
            s = lax.dot_general(q[:, h * HEAD_DIM:(h + 1) * HEAD_DIM], kk, (_DIMS["nt"], ((), ())),
                                preferred_element_type=F32) * (HEAD_DIM ** -0.5)
            s = jnp.where(valid, s + b_ref[h], NEG_INF)
            sink = s_ref[h]
            m = jnp.maximum(jnp.max(s, axis=-1, keepdims=True), sink)
            e = jnp.exp(s - m)
            den = jnp.sum(e, axis=-1, keepdims=True) + jnp.exp(sink - m)
            lse = jnp.where(lane == h, m + jnp.log(den), lse)
            outs.append(jnp.dot((e * (1.0 / den)).astype(BF16), vv, preferred_element_type=F32))
        o_ref[...] = jnp.concatenate(outs, axis=1).astype(BF16)
        lse_ref[...] = lse

    res, jres = _call(
        body, name=name, grid=(nblk,), in_specs=in_specs,
        out_specs=[pl.BlockSpec((BLOCK, D_MODEL), lambda n: (n, 0)), pl.BlockSpec((BLOCK, LANE), lambda n: (n, 0))],
        out_shape=[jax.ShapeDtypeStruct((t, D_MODEL), BF16), jax.ShapeDtypeStruct((t, LANE), F32)],
        sem=("parallel",), args=[proj, proj, proj, proj, proj, bias, sinks], jobs=jobs)
    return (res, jres) if jobs else res


def _attn_bwd(proj, datt, lse, bias, sinks, name, jobs=(), into=None):
    t = proj.shape[0]
    nblk, in_specs = _attn_specs(t)
    in_specs = in_specs + [pl.BlockSpec((BLOCK, D_MODEL), lambda n: (n, 0)), pl.BlockSpec((BLOCK, LANE), lambda n: (n, 0))]
    scale = HEAD_DIM ** -0.5
    args = [proj, proj, proj, proj, proj, bias, sinks, datt, lse]
    dq_spec = pl.BlockSpec((BLOCK, D_MODEL), lambda n: (n, 0))
    dq_shape = jax.ShapeDtypeStruct((t, D_MODEL), BF16)
    aliases = None
    if into is not None:
        aliases = {len(args): 0}
        args.append(into)
        in_specs.append(pl.BlockSpec(memory_space=pl.ANY))
        dq_spec = pl.BlockSpec((BLOCK, D_MODEL), lambda n: (n, OFF_Q // D_MODEL))
        dq_shape = jax.ShapeDtypeStruct(into.shape, BF16)

    def body(q_ref, kp_ref, kc_ref, vp_ref, vc_ref, b_ref, s_ref, do_ref, lse_ref, *rest):
        dq_ref, dk_ref, dv_ref, db_ref, ds_ref = rest[-5:]
        n = pl.program_id(0)

        @pl.when(n == 0)
        def _():
            dk_ref[...] = jnp.zeros_like(dk_ref)
            dv_ref[...] = jnp.zeros_like(dv_ref)
            db_ref[...] = jnp.zeros_like(db_ref)
            ds_ref[...] = jnp.zeros_like(ds_ref)

        valid = _band_valid(n == 0)
        lane = lax.broadcasted_iota(jnp.int32, (1, LANE), 1)
        q = q_ref[...]
        do = do_ref[...]
        lse_all = lse_ref[...]
        kall = jnp.concatenate([kp_ref[...], kc_ref[...]], axis=0)
        vall = jnp.concatenate([vp_ref[...], vc_ref[...]], axis=0)
        low = _low_half()
        dqs, dks, dvs = [], [], []
        dsink_row = jnp.zeros((1, LANE), F32)
        ns = ATTN_STACK_BWD
        for kv in range(N_KV_HEADS):
            k2, v2 = _dup_kv(kall, kv), _dup_kv(vall, kv)
            dk_acc = jnp.zeros((2 * BLOCK, LANE), F32)
            dv_acc = jnp.zeros((2 * BLOCK, LANE), F32)
            for h0 in range(kv * GROUP, (kv + 1) * GROUP, ns):
                qg, dog = _stack_heads(q, h0, ns), _stack_heads(do, h0, ns)
                lse_g = jnp.concatenate([lse_all[:, h0 + g:h0 + g + 1][None] for g in range(ns)], axis=0)
                p, psink = _group_probs(qg, k2, b_ref[h0:h0 + ns], _group_sinks(s_ref, h0, ns), lse_g, valid)
                dp = lax.dot_general(dog, v2, (_DIMS["nt"], ((), ())), preferred_element_type=F32)
                dp = dp.reshape(ns, BLOCK, 2 * BLOCK)
                delta = jnp.sum(p * dp, axis=-1, keepdims=True)
                ds = p * (dp - delta)
                db_ref[h0:h0 + ns] += ds
                dsink = -jnp.sum(psink * delta, axis=1, keepdims=True)
                for g in range(ns):
                    dsink_row = dsink_row + jnp.where(lane == h0 + g, dsink[g], 0.0)
                dsb = ds.reshape(ns * BLOCK, 2 * BLOCK).astype(BF16)
                pb = p.reshape(ns * BLOCK, 2 * BLOCK).astype(BF16)
                dqs.append(_unstack_heads(jnp.dot(dsb, k2, preferred_element_type=F32) * scale, ns))
                dk_acc = dk_acc + lax.dot_general(dsb, qg, (_DIMS["tn"], ((), ())), preferred_element_type=F32)
                dv_acc = dv_acc + lax.dot_general(pb, dog, (_DIMS["tn"], ((), ())), preferred_element_type=F32)
            dks.append(_fold_halves(dk_acc) * scale)
            dvs.append(_fold_halves(dv_acc))
        dq_ref[...] = jnp.concatenate(dqs, axis=1).astype(BF16)
        rows = pl.ds(pl.multiple_of(n * BLOCK, BLOCK), 2 * BLOCK)
        dk_ref[rows, :] += jnp.concatenate([jnp.where(low, dks[0], dks[1]), jnp.where(low, dks[2], dks[3])], axis=1)
        dv_ref[rows, :] += jnp.concatenate([jnp.where(low, dvs[0], dvs[1]), jnp.where(low, dvs[2], dvs[3])], axis=1)
        ds_ref[0:1, :] += dsink_row

    res, jres = _call(
        body, name=name, grid=(nblk,), in_specs=in_specs,
        out_specs=[dq_spec,
                   pl.BlockSpec((t + BLOCK, KV_WIDTH), lambda n: (0, 0)),
                   pl.BlockSpec((t + BLOCK, KV_WIDTH), lambda n: (0, 0)),
                   pl.BlockSpec((N_Q_HEADS, BLOCK, 2 * BLOCK), lambda n: (0, 0, 0)),
                   pl.BlockSpec((8, LANE), lambda n: (0, 0))],
        out_shape=[dq_shape,
                   jax.ShapeDtypeStruct((t + BLOCK, KV_WIDTH), F32),
                   jax.ShapeDtypeStruct((t + BLOCK, KV_WIDTH), F32),
                   jax.ShapeDtypeStruct((N_Q_HEADS, BLOCK, 2 * BLOCK), F32),
                   jax.ShapeDtypeStruct((8, LANE), F32)],
        sem=("arbitrary",), args=args, jobs=jobs, aliases=aliases)
    return (res, jres) if jobs else res


def _bucket_table():
    qi = np.arange(BLOCK)[:, None]
    ki = np.arange(2 * BLOCK)[None, :]
    rel = np.maximum(qi + BLOCK - ki, 0)
    max_exact = N_BUCKETS // 2
    relf = np.maximum(rel, 1).astype(np.float32)
    large = max_exact + (np.log(relf / max_exact) / math.log(MAX_DISTANCE / max_exact)
                         * (N_BUCKETS - max_exact)).astype(np.int32)
    large = np.minimum(large, N_BUCKETS - 1)
    return np.where(rel < max_exact, rel, large).astype(np.int32)


def _rel_bias_fwd(rel_bias, onehot, name):
    n = onehot.shape[1]
    tn = _tile(n, 8192, LANE)

    def body(r_ref, o_ref, out_ref):
        out_ref[...] = lax.dot_general(r_ref[...], o_ref[...], (_DIMS["tn"], ((), ())),
                                       precision=lax.Precision.HIGHEST, preferred_element_type=F32)

    return pl.pallas_call(
        body, name=name, grid=(n // tn,),
        in_specs=[pl.BlockSpec(rel_bias.shape, lambda j: (0, 0)), pl.BlockSpec((N_BUCKETS, tn), lambda j: (0, j))],
        out_specs=pl.BlockSpec((N_Q_HEADS, tn), lambda j: (0, j)),
        out_shape=jax.ShapeDtypeStruct((N_Q_HEADS, n), F32),
        compiler_params=_params(("parallel",)),
    )(rel_bias, onehot)


def _rel_bias_bwd(dbias, onehot, name):
    def body(d_ref, o_ref, out_ref):
        out_ref[...] = lax.dot_general(d_ref[...], o_ref[...], (_DIMS["nt"], ((), ())),
                                       precision=lax.Precision.HIGHEST, preferred_element_type=F32)

    full = pl.BlockSpec(dbias.shape, lambda: (0, 0))
    return pl.pallas_call(
        body, name=name, in_specs=[full, full],
        out_specs=pl.BlockSpec((N_Q_HEADS, N_BUCKETS), lambda: (0, 0)),
        out_shape=jax.ShapeDtypeStruct((N_Q_HEADS, N_BUCKETS), F32),
        compiler_params=_params(()),
    )(dbias, onehot)


def _part_specs(gparts, part_index, tr, cc):
    blk = pl.BlockSpec((tr, cc), lambda i, *_: (i, 0))
    specs = []
    for i, _ in enumerate(gparts):
        if i in part_index:
            ix = part_index[i]
            if isinstance(ix, int):
                specs.append(pl.BlockSpec((None, tr, cc), functools.partial(lambda i, *_, ix: (ix, i, 0), ix=ix)))
            else:
                specs.append(pl.BlockSpec((None, tr, cc), lambda i, ix_ref: (ix_ref[0], i, 0)))
        else:
            specs.append(blk)
    return blk, specs


def _sum_parts(gparts, part_index, name):
    r, cc = gparts[0].shape[-2:]
    tr = _tile(r, 256)
    dyn = [i for i, ix in part_index.items() if not isinstance(ix, int)]
    assert len(dyn) <= 1

    def body(*refs):
        refs = refs[len(dyn):]
        g = refs[0][...].astype(F32)
        for gr in refs[1:-1]:
            g = g + gr[...].astype(F32)
        refs[-1][...] = g

    blk, specs = _part_specs(gparts, part_index, tr, cc)
    grid_spec = pltpu.PrefetchScalarGridSpec(num_scalar_prefetch=len(dyn), grid=(r // tr,), in_specs=specs, out_specs=blk)
    return pl.pallas_call(
        body, name=name, grid_spec=grid_spec, out_shape=jax.ShapeDtypeStruct((r, cc), F32),
        compiler_params=_params(("parallel",)),
    )(*[part_index[i] for i in dyn], *gparts)


def _adamw(w, m, v, gparts, name, part_index=None, jobs=()):
    r, cc = w.shape
    tr = _tile(r, 128)
    np_ = len(gparts)
    part_index = part_index or {}
    dyn = [i for i, ix in part_index.items() if not isinstance(ix, int)]
    assert len(dyn) <= 1
    bc1 = 1.0 / (1.0 - ADAM_B1 ** ADAM_STEP)
    bc2 = 1.0 / (1.0 - ADAM_B2 ** ADAM_STEP)

    def body(*refs):
        refs = refs[len(dyn):]
        w_ref, m_ref, v_ref = refs[:3]
        g_refs = refs[3:3 + np_]
        g_out, d_out, m_out, v_out = refs[3 + np_:]
        g = g_refs[0][...].astype(F32)
        for gr in g_refs[1:]:
            g = g + gr[...].astype(F32)
        mn = ADAM_B1 * m_ref[...] + (1.0 - ADAM_B1) * g
        vn = ADAM_B2 * v_ref[...] + (1.0 - ADAM_B2) * (g * g)
        g_out[...] = g
        m_out[...] = mn
        v_out[...] = vn
        d_out[...] = -ADAM_LR * ((mn * bc1) / (jnp.sqrt(vn * bc2) + ADAM_EPS) + ADAM_WD * w_ref[...])

    blk, g_specs = _part_specs(gparts, part_index, tr, cc)
    if jobs:
        assert not dyn
        return _call(body, name=name, grid=(r // tr,), in_specs=[blk, blk, blk] + g_specs, out_specs=[blk] * 4,
                     out_shape=[jax.ShapeDtypeStruct((r, cc), F32)] * 4, sem=("arbitrary",),
                     args=[w, m, v, *gparts], jobs=jobs)
    grid_spec = pltpu.PrefetchScalarGridSpec(
        num_scalar_prefetch=len(dyn), grid=(r // tr,),
        in_specs=[blk, blk, blk] + g_specs, out_specs=[blk] * 4)
    return pl.pallas_call(
        body, name=name, grid_spec=grid_spec,
        out_shape=[jax.ShapeDtypeStruct((r, cc), F32)] * 4,
        compiler_params=_params(("parallel",)),
    )(*[part_index[i] for i in dyn], w, m, v, *gparts)


def _row(vec):
    return vec.reshape(1, -1)


def kernel(x, c, w_ada, b_ada, norm1_g, w_in, conv_w, conv_b, lru_wa, lru_ba, lru_wx, lru_bx, lru_lambda, w_lru_out, w_attn_out, attn_sinks, rel_bias, w_out, norm2_g, w_ff1, w_ff2, final_g, loss_target, m_w_ada, m_b_ada, m_norm1_g, m_w_in, m_conv_w, m_conv_b, m_lru_wa, m_lru_ba, m_lru_wx, m_lru_bx, m_lru_lambda, m_w_lru_out, m_w_attn_out, m_attn_sinks, m_rel_bias, m_w_out, m_norm2_g, m_w_ff1, m_w_ff2, m_final_g, v_w_ada, v_b_ada, v_norm1_g, v_w_in, v_conv_w, v_conv_b, v_lru_wa, v_lru_ba, v_lru_wx, v_lru_bx, v_lru_lambda, v_w_lru_out, v_w_attn_out, v_attn_sinks, v_rel_bias, v_w_out, v_norm2_g, v_w_ff1, v_w_ff2, v_final_g):
    dd = D_MODEL
    t = x.shape[1]
    ix, iy, ic = lax.axis_index("x"), lax.axis_index("y"), lax.axis_index("c")
    me = 4 * ix + 2 * iy + ic
    chip = 2 * ix + iy
    c_idx = jnp.reshape(ic, (1,)).astype(jnp.int32)
    chip_idx = jnp.reshape(chip, (1,)).astype(jnp.int32)

    xs = x[0]
    target = loss_target[0]
    ncol_ada = w_ada.shape[2]
    cshard = conv_w.shape[2]

    shards = [jnp.transpose(w_in[0]).astype(BF16), w_lru_out[0].astype(BF16), w_attn_out[0].astype(BF16),
              w_out[0].astype(BF16), jnp.transpose(w_ff1[0]).astype(BF16), w_ff2[0].astype(BF16)]
    s_in, s_lo, s_ao, s_o, s_f1, s_f2 = shards

    def full(g):
        return g.reshape(N_DEV * g.shape[1], dd)

    def stack(g):
        return g.reshape(N_DEV, g.shape[0] // N_DEV, dd)

    (w_in_pair,) = _run_job(_pair_job([s_in]), "ag_w_in_pair")
    w_in_pair = lax.dynamic_update_slice_in_dim(w_in_pair, s_in[None], ic, axis=0)

    pack0 = jnp.zeros((8, dd), F32).at[0:1, :].set(c).at[1:1 + CONV_WIDTH, 0:cshard].set(conv_w[0])
    (g0,) = _all_gather([pack0], "ag_cond")
    c_all = g0[:, 0, :]
    conv_w_full = jnp.transpose(g0[:, 1:1 + CONV_WIDTH, 0:cshard], (1, 0, 2)).reshape(CONV_WIDTH, dd)
    b_cols = lax.dynamic_slice(b_ada, (0, me * ncol_ada), (1, ncol_ada))
    mod_cols = _ada_fwd(c_all, w_ada[0], b_cols, "ada_fwd")
    (g1,) = _all_gather([mod_cols], "ag_mod")
    mod_mine = lax.dynamic_index_in_dim(g1, me, axis=1, keepdims=False).reshape(6, dd)
    mod = jnp.concatenate([mod_mine, jnp.zeros((2, dd), F32)], axis=0)

    bucket = _bucket_table()
    onehot = jnp.asarray((bucket.reshape(1, -1) == np.arange(N_BUCKETS)[:, None]).astype(np.float32))
    bias = _rel_bias_fwd(rel_bias, onehot, "rel_bias_fwd").reshape(N_Q_HEADS, BLOCK, 2 * BLOCK)
    sinks = attn_sinks.reshape(N_Q_HEADS)
    lp = jnp.concatenate([conv_b, lru_ba, lru_bx, lru_lambda, conv_w_full], axis=0)
    wa, wx = lru_wa[0], lru_wx[0]

    tm = _tile(t, 1024)
    tn = 512
    h = _norm_mod(xs, norm1_g, mod, 0, "norm1")
    tn2 = 1024
    chunk_w = 2 * s_in.shape[0]

    def ids(*ks):
        return jnp.stack([jnp.asarray(k, jnp.int32) for k in ks])

    tn_in = _tile(chunk_w, 1024, LANE)
    (proj,), ((g_near,),) = _mm(
        h, w_in_pair.reshape(chunk_w, dd), "nt", tm=tm, tn=tn_in, outs=[BF16], name="in_proj_own",
        chunks=(ids(chip), chunk_w, ids(0)), n_total=IN_WIDTH, jobs=[_gather_job([w_in_pair], (0.7,), "near")])
    near_ids = ids(2 * (1 - ix) + iy, 2 * ix + 1 - iy)
    (proj,), ((g_far,), (g_o,)) = _mm(
        h, full(g_near), "nt", tm=tm, tn=tn_in, outs=[BF16], name="in_proj_near",
        chunks=(near_ids, chunk_w, near_ids), into=proj,
        jobs=[_gather_job([g_near], (0.6,), "far"), _gather_job([s_o], (0.6, 0.85))])
    far_chip = 2 * (1 - ix) + 1 - iy
    (proj,), ((g_lo,),) = _mm(
        h, full(g_far), "nt", tm=tm, tn=tn_in, outs=[BF16], name="in_proj_far",
        chunks=(ids(far_chip), chunk_w, ids(far_chip)), into=proj, jobs=[_gather_job([s_lo], (0.5, 0.8))])
    far_slots = lax.dynamic_slice_in_dim(g_far, 2 * far_chip, 2, axis=0)
    win_t = lax.dynamic_update_slice_in_dim(g_near, far_slots, 2 * far_chip, axis=0)
    win_t = full(lax.dynamic_update_slice_in_dim(win_t, w_in_pair, 2 * chip, axis=0))
    (g_o,) = _fill_own([g_o], [s_o])
    (rec, ya), ((g_f1,),) = _lru_fwd(proj, lp, wa, wx, "lru_fwd", jobs=[_gather_job([s_f1], (0.6, 0.9))])
    (att, lse), ((g_ao,),) = _attn_fwd(proj, bias, sinks, "attn_fwd", jobs=[_gather_job([s_ao], (0.4, 0.7))])
    g_lo, g_ao, g_f1 = _fill_own([g_lo, g_ao, g_f1], [s_lo, s_ao, s_f1])
    wlo, wao, wo, wf1_t = full(g_lo), full(g_ao), full(g_o), full(g_f1)
    (y_a,) = _mm(ya, wlo, "nn", tm=tm, tn=tn2, outs=[BF16], name="lru_out")

    def merge_epi(acc, ya_t, ga_t, gb_t):
        return acc, _sigmoid(ga_t.astype(F32)) * ya_t.astype(F32) + _sigmoid(gb_t.astype(F32)) * acc

    y_b, merged = _mm(att, wao, "nn", tm=tm, tn=tn, outs=[BF16, BF16], epi=merge_epi,
                      tiles=[(y_a, 0), (proj, OFF_GA // tn), (proj, OFF_GB // tn)], name="attn_out_merge")

    def resid_epi(row):
        def epi(acc, x_t, mod_t):
            return acc, x_t + mod_t[row:row + 1, :] * acc
        return epi

    o1, x1 = _mm(merged, wo, "nn", tm=tm, tn=tn2, outs=[BF16, F32], epi=resid_epi(2),
                 tiles=[(xs, 0)], rows=[mod], name="out_proj")
    h2 = _norm_mod(x1, norm2_g, mod, 3, "norm2")

    def relu2_epi(acc):
        rl = jnp.maximum(acc, 0.0)
        return acc, rl * rl

    (f_pre, ff), ((g_f2,),) = _mm(h2, wf1_t, "nt", tm=tm, tn=1024, outs=[BF16, BF16], epi=relu2_epi, name="ff1",
                                  jobs=[_gather_job([s_f2], (0.6, 0.9))])
    wf2 = full(_fill_own([g_f2], [s_f2])[0])
    tmh = _tile(t, 512)
    o2, x2 = _mm(ff, wf2, "nn", tm=tmh, tn=tn, outs=[BF16, F32], epi=resid_epi(5),
                 tiles=[(x1, 0)], rows=[mod], ring=True, name="ff2")

    loss_blk, dx2, do2, st_f = _final_loss(x2, o2, _row(final_g), target, mod, "final_loss")
    loss = lax.psum(loss_blk[0, 0], ("x", "y", "c"))

    def drelu2_epi(acc, f_t):
        return (acc * (2.0 * jnp.maximum(f_t.astype(F32), 0.0)),)

    tmw, tnw, tkw = 512, dd, _tile(t, 2048)
    (df,) = _mm(do2, wf2, "nt", tm=tm, tn=1024, outs=[BF16], epi=drelu2_epi, tiles=[(f_pre, 0)], name="ff2_dx")
    (dwf2,) = _mm(ff, do2, "tn", tm=tmw, tn=tnw, tk=tkw, outs=[BF16], name="ff2_dw")
    (dh2,), ((sib_f2,),) = _mm(df, wf1_t, "nn", tm=tmh, tn=tn, outs=[BF16], ring=True, name="ff1_dx",
                              jobs=[_sibling_job([stack(dwf2)])])
    pair_f2 = _pair_add(stack(dwf2), sib_f2, c_idx, "rs_pair_add_w_ff2")
    (dwf1_t,), ((chips_f2,),) = _mm(df, h2, "tn", tm=tmw, tn=tnw, tk=tkw, outs=[BF16], name="ff1_dw",
                                    jobs=[_chips_job([pair_f2])])
    (dx1, st_2, do1), ((sib_f1,),) = _norm_mod_bwd(x1, norm2_g, mod, 3, dh2, dx2, (o1, 2), "norm2_bwd",
                                                   jobs=[_sibling_job([stack(dwf1_t)])])
    pair_f1 = _pair_add(stack(dwf1_t), sib_f1, c_idx, "rs_pair_add_w_ff1")

    def dmerge_epi(acc, ga_t, gb_t, ya_t, yb_t):
        sa, sb = _sigmoid(ga_t.astype(F32)), _sigmoid(gb_t.astype(F32))
        return (acc * sa, acc * sb, acc * ya_t.astype(F32) * sa * (1.0 - sa), acc * yb_t.astype(F32) * sb * (1.0 - sb))

    tmh = _tile(t, 512)
    dy_a, dy_b, dproj, dgb = _mm(do1, wo, "nt", tm=tm, tn=tn, outs=[BF16] * 4, epi=dmerge_epi,
                                 tiles=[(proj, OFF_GA // tn), (proj, OFF_GB // tn), (y_a, 0), (y_b, 0)],
                                 wide={2: (IN_WIDTH, OFF_GA // tn)}, name="out_proj_dx")
    (dwo,) = _mm(merged, do1, "tn", tm=tmw, tn=tnw, tk=tkw, outs=[BF16], name="out_proj_dw")
    (datt,) = _mm(dy_b, wao, "nt", tm=tm, tn=tn2, outs=[BF16], name="attn_out_dx")
    (dwao,) = _mm(att, dy_b, "tn", tm=tmw, tn=tnw, tk=tkw, outs=[BF16], name="attn_out_dw")
    (dya,) = _mm(dy_a, wlo, "nt", tm=tm, tn=tn2, outs=[BF16], name="lru_out_dx")
    (dwlo,) = _mm(ya, dy_a, "tn", tm=tmw, tn=tnw, tk=tkw, outs=[BF16], name="lru_out_dw")
    (dproj, dk_pad, dv_pad, dbias, dsinks), (sib_3, (chips_f1,)) = _attn_bwd(
        proj, datt, lse, bias, sinks, "attn_bwd", into=dproj,
        jobs=[_sibling_job([stack(dwlo), stack(dwao), stack(dwo)]), _chips_job([pair_f1])])
    pair_lo, pair_ao, pair_o = [_pair_add(stack(g_), s_, c_idx, "rs_pair_add_" + nm_)
                                for g_, s_, nm_ in zip((dwlo, dwao, dwo), sib_3, ("w_lru_out", "w_attn_out", "w_out"))]
    (dproj, st_l, dwa, dwx), ((chips_lo, chips_ao, chips_o),) = _lru_bwd(
        proj, rec, dya, lp, wa, wx, "lru_bwd", into=dproj, jobs=[_chips_job([pair_lo, pair_ao, pair_o])])
    dkv = jnp.concatenate([dk_pad[BLOCK:].astype(BF16), dv_pad[BLOCK:].astype(BF16)], axis=1)
    dproj = lax.dynamic_update_slice_in_dim(dproj, dkv, OFF_K, axis=1)
    dproj = lax.dynamic_update_slice_in_dim(dproj, dgb, OFF_GB, axis=1)
    lru_mats = jnp.concatenate([dwa.reshape(-1, dd), dwx.reshape(-1, dd)], axis=0)
    (dwin_t,), ((mats_all,),) = _mm(dproj, h, "tn", tm=768, tn=tnw, tk=tkw, outs=[BF16], name="in_proj_dw",
                                   jobs=[_gather_job([lru_mats], (0.3, 0.6))])
    (mats_all,) = _fill_own([mats_all], [lru_mats])
    (sib_in,) = _exchange_sibling([stack(dwin_t)], "rs_sibling_w_in")
    pair_in = _pair_add(stack(dwin_t), sib_in, c_idx, "rs_pair_add_w_in", widths=(dd // 2, dd // 2))
    n_rb = t // tm
    dx_tiles = dict(tm=tm, tn=tn2, tk=IN_WIDTH // 4, outs=[BF16])
    if n_rb >= 2:
        (dh,), ((chips_in_0,),) = _mm(dproj, win_t, "nn", name="in_proj_dx_top", rows_part=(0, n_rb // 2),
                                     jobs=[_chips_job(pair_in[0:1])], **dx_tiles)
        (dh,), ((chips_in_1,),) = _mm(dproj, win_t, "nn", name="in_proj_dx_bottom", into=dh,
                                     rows_part=(n_rb // 2, n_rb - n_rb // 2), jobs=[_chips_job(pair_in[1:2])], **dx_tiles)
    else:
        (dh,), ((chips_in_0,), (chips_in_1,)) = _mm(
            dproj, win_t, "nn", name="in_proj_dx", jobs=[_chips_job(pair_in[0:1]), _chips_job(pair_in[1:2])], **dx_tiles)
    grad_x, st_1 = _norm_mod_bwd(xs, norm1_g, mod, 0, dh, dx1, None, "norm1_bwd")
    chips_in = [chips_in_0, chips_in_1]

    drel = jnp.transpose(_rel_bias_bwd(dbias.reshape(N_Q_HEADS, -1), onehot, "rel_bias_bwd"))

    misc = jnp.concatenate([drel.reshape(1, -1), dsinks[0:1, 0:N_Q_HEADS],
                            jnp.zeros((1, dd - N_BUCKETS * N_Q_HEADS - N_Q_HEADS), F32)], axis=1)
    zero_row = jnp.zeros((1, dd), F32)
    small = jnp.concatenate([
        st_1[0:2], st_2[3:4], st_2[0:2], st_f[1:2], zero_row, zero_row,
        st_1[2:3], st_l[0:1], st_l[1:2], st_l[2:3], st_l[3:4], st_2[2:3], st_f[0:1], misc,
        st_l[4:8], jnp.zeros((4, dd), F32)], axis=0)
    own_and_chips = {0: chip_idx, 1: 0, 2: 1, 3: 2}
    g_in = jnp.transpose(jnp.concatenate(
        [_sum_parts([p_, c_, c_, c_], own_and_chips, "sum_w_in_%d" % q_)
         for q_, (p_, c_) in enumerate(zip(pair_in, chips_in))], axis=1))
    w_in_res, ((small_all,),) = _adamw(w_in[0], m_w_in[0], v_w_in[0], [g_in], "adamw_w_in",
                                       jobs=[_gather_job([small], (0.3, 0.6))])
    (small_all,) = _fill_own([small_all], [small])

    def pack_small(b_, n1, cb, ba, bx, lam, n2, fg, rb, sk):
        misc_ = jnp.concatenate([rb.reshape(1, -1), sk.reshape(1, -1),
                                 jnp.zeros((1, dd - N_BUCKETS * N_Q_HEADS - N_Q_HEADS), F32)], axis=1)
        return jnp.concatenate([b_.reshape(6, dd), jnp.zeros((2, dd), F32), n1, cb, ba, bx, lam, n2, _row(fg), misc_,
                                jnp.zeros((8, dd), F32)], axis=0)

    def pack_mats(wa_, wx_):
        return jnp.concatenate([wa_.reshape(-1, dd), wx_.reshape(-1, dd)], axis=0)

    every = {i: i for i in range(N_DEV)}
    w_s = pack_small(b_ada, norm1_g, conv_b, lru_ba, lru_bx, lru_lambda, norm2_g, final_g, rel_bias, attn_sinks)
    m_s = pack_small(m_b_ada, m_norm1_g, m_conv_b, m_lru_ba, m_lru_bx, m_lru_lambda, m_norm2_g, m_final_g, m_rel_bias, m_attn_sinks)
    v_s = pack_small(v_b_ada, v_norm1_g, v_conv_b, v_lru_ba, v_lru_bx, v_lru_lambda, v_norm2_g, v_final_g, v_rel_bias, v_attn_sinks)
    small_res = _adamw(w_s, m_s, v_s, [small_all] * N_DEV, "adamw_small", part_index=every)
    mats_res = _adamw(pack_mats(lru_wa, lru_wx), pack_mats(m_lru_wa, m_lru_wx), pack_mats(v_lru_wa, v_lru_wx),
                      [mats_all] * N_DEV, "adamw_lru_mats", part_index=every)

    def unpack_small(s, mt):
        nb_ = N_BUCKETS * N_Q_HEADS
        half = mt.shape[0] // 2
        return dict(
            b_ada=s[0:6].reshape(1, 6 * dd), norm1_g=s[8:9], conv_b=s[9:10], lru_ba=s[10:11], lru_bx=s[11:12],
            lru_lambda=s[12:13], norm2_g=s[13:14], final_g=s[14], rel_bias=s[15, 0:nb_].reshape(N_BUCKETS, N_Q_HEADS),
            attn_sinks=s[15:16, nb_:nb_ + N_Q_HEADS],
            lru_wa=mt[:half].reshape(1, LRU_BLOCKS, LRU_BLOCK_W, LRU_BLOCK_W),
            lru_wx=mt[half:].reshape(1, LRU_BLOCKS, LRU_BLOCK_W, LRU_BLOCK_W))

    res = {k: [None] * 4 for k in ("w_ada", "w_in", "conv_w", "w_lru_out", "w_attn_out", "w_out", "w_ff1", "w_ff2")}
    for q_, (s_, mt_) in enumerate(zip(small_res, mats_res)):
        for k_, val in unpack_small(s_, mt_).items():
            res.setdefault(k_, [None] * 4)[q_] = val

    g_conv = lax.dynamic_slice(small_res[0][16:16 + CONV_WIDTH], (0, me * cshard), (CONV_WIDTH, cshard))
    conv_res = _adamw(conv_w[0], m_conv_w[0], v_conv_w[0], [g_conv], "adamw_conv")
    res["conv_w"] = [r_[None] for r_ in conv_res]

    dmod_all = small_all[:, 0:6, :].reshape(N_DEV, 6 * dd)
    dmod_cols = lax.dynamic_slice(dmod_all, (0, me * ncol_ada), (N_DEV, ncol_ada))
    g_ada = _ada_bwd(jnp.transpose(c_all), dmod_cols, "ada_bwd")
    ada_res = _adamw(w_ada[0], m_w_ada[0], v_w_ada[0], [g_ada], "adamw_ada")
    res["w_ada"] = [r_[None] for r_ in ada_res]

    pair = [None, pair_lo, pair_ao, pair_o, pair_f1, pair_f2]
    from_chips = [None, chips_lo, chips_ao, chips_o, chips_f1, chips_f2]

    def summed(i):
        return [pair[i], from_chips[i], from_chips[i], from_chips[i]], own_and_chips

    def sum_only(i, name):
        parts, index = summed(i)
        return _sum_parts(parts, index, name)

    res["w_in"] = [r_[None] for r_ in w_in_res]
    g_ff1 = jnp.transpose(sum_only(4, "sum_w_ff1"))
    res["w_ff1"] = [r_[None] for r_ in _adamw(w_ff1[0], m_w_ff1[0], v_w_ff1[0], [g_ff1], "adamw_w_ff1")]
    for i, (nm, w_, m_, v_) in {1: ("w_lru_out", w_lru_out, m_w_lru_out, v_w_lru_out),
                                2: ("w_attn_out", w_attn_out, m_w_attn_out, v_w_attn_out),
                                3: ("w_out", w_out, m_w_out, v_w_out),
                                5: ("w_ff2", w_ff2, m_w_ff2, v_w_ff2)}.items():
        parts, index = summed(i)
        res[nm] = [r_[None] for r_ in _adamw(w_[0], m_[0], v_[0], parts, "adamw_" + nm, part_index=index)]

    order = ["w_ada", "b_ada", "norm1_g", "w_in", "conv_w", "conv_b", "lru_wa", "lru_ba", "lru_wx", "lru_bx",
             "lru_lambda", "w_lru_out", "w_attn_out", "attn_sinks", "rel_bias", "w_out", "norm2_g", "w_ff1",
             "w_ff2", "final_g"]
    out = [loss, grad_x[None]]
    for q_ in range(4):
        out += [res[k_][q_] for k_ in order]
    return tuple(out)
```

```python
import functools
import math

import numpy as np
import jax
import jax.numpy as jnp
from jax import lax
from jax.experimental import pallas as pl
from jax.experimental.pallas import tpu as pltpu

F32 = jnp.float32
BF16 = jnp.bfloat16
MESH = pl.DeviceIdType.MESH

D_MODEL = 2048
N_Q_HEADS = 32
N_KV_HEADS = 4
GROUP = N_Q_HEADS // N_KV_HEADS
HEAD_DIM = 64
KV_WIDTH = N_KV_HEADS * HEAD_DIM
BLOCK = 128
NEG_INF = -1e30
N_BUCKETS = 32
MAX_DISTANCE = 128
LRU_BLOCKS = 16
LRU_BLOCK_W = 128
CONV_WIDTH = 4
LRU_C = 8.0
D_FF = 4 * D_MODEL
EPS = 1e-6
IN_WIDTH = 5 * D_MODEL + 2 * KV_WIDTH
OFF_LX, OFF_LG, OFF_Q, OFF_K, OFF_V, OFF_GA, OFF_GB = 0, 2048, 4096, 6144, 6400, 6656, 8704

ADAM_LR, ADAM_B1, ADAM_B2, ADAM_EPS, ADAM_WD, ADAM_STEP = 0.001, 0.9, 0.999, 1e-08, 0.01, 10

N_DEV = 8
VMEM_LIMIT_BYTES = 48 * 1024 * 1024
VMEM_LIMIT_RING_BYTES = 56 * 1024 * 1024
LANE = 128
LRU_CW = 512


def _params(sem, vmem=None):
    return pltpu.CompilerParams(dimension_semantics=sem, vmem_limit_bytes=VMEM_LIMIT_BYTES if vmem is None else vmem)


def _tile(n, pref, mult=8):
    if n <= pref:
        return n
    t = (pref // mult) * mult
    while t >= mult:
        if n % t == 0:
            return t
        t -= mult
    return n


def _sigmoid(x):
    return 0.5 * jnp.tanh(0.5 * x) + 0.5


def _gelu_tanh(x):
    k = math.sqrt(2.0 / math.pi)
    t = jnp.tanh(k * (x + 0.044715 * x * x * x))
    return 0.5 * x * (1.0 + t), t


def _gelu_tanh_grad(x, t):
    k = math.sqrt(2.0 / math.pi)
    return 0.5 * (1.0 + t) + 0.5 * x * (1.0 - t * t) * k * (1.0 + 3.0 * 0.044715 * x * x)


def _log1p(e):
    u = 1.0 + e
    return jnp.where(u == 1.0, e, jnp.log(u) * (e / jnp.where(u == 1.0, 1.0, u - 1.0)))


def _softplus(x):
    return jnp.maximum(x, 0.0) + _log1p(jnp.exp(-jnp.abs(x)))


def _neg_expm1(x, exp_x):
    series = -x * (1.0 + x * (0.5 + x * (1.0 / 6.0 + x * (1.0 / 24.0))))
    return jnp.where(x > -0.0625, series, 1.0 - exp_x)


def _my_place():
    return lax.axis_index("x"), lax.axis_index("y"), lax.axis_index("c")


def _all_gather(arrs, name):
    return _fill_own(_run_job(_gather_job(arrs), name), arrs)


def _fill_own(stacks, shards):
    me = 4 * lax.axis_index("x") + 2 * lax.axis_index("y") + lax.axis_index("c")
    out = []
    for g, s in zip(stacks, shards):
        if g.size * g.dtype.itemsize <= (1 << 20):
            slot = lax.broadcasted_iota(jnp.int32, g.shape, 0)
            out.append(jnp.where(slot == me, s[None], g))
        else:
            out.append(lax.dynamic_update_slice_in_dim(g, s[None], me, axis=0))
    return out


class _Job:
    def __init__(self, ins, outs, sems, start, finish, forwards=(), forward_at=()):
        self.ins, self.outs, self.sems = list(ins), list(outs), list(sems)
        self.start, self.finish, self.forwards, self.forward_at = start, finish, list(forwards), list(forward_at)
        assert len(self.forwards) <= len(self.forward_at)


def _gather_job(arrs, forward_at=(0.6, 0.85), part="all"):
    n = len(arrs)
    whole, near, far = part == "all", part == "near", part == "far"

    def copies(ins, outs, sems):
        send_sems, recv_sems = sems
        x, y, c = _my_place()
        sib = (x, y, 1 - c)
        xn, yn, dg = (1 - x, y), (x, 1 - y), (1 - x, 1 - y)

        def slot(px, py, pc):
            return 4 * px + 2 * py + pc

        def cp(a, k, src, dst_slot, to):
            return pltpu.make_async_remote_copy(
                src_ref=src, dst_ref=outs[a].at[dst_slot], send_sem=send_sems.at[a, k], recv_sem=recv_sems.at[a, k],
                device_id=to, device_id_type=MESH)

        def arrival(a, k, dst_slot):
            return cp(a, k, outs[a].at[dst_slot], dst_slot, sib)

        me_slot = slot(x, y, c)
        sends, relay, passes, arrive = {}, {}, {}, {}
        for a in range(n):
            if whole:
                own = ins[a]
                sends[a, 0] = cp(a, 0, own, me_slot, sib)
            if near:
                own = ins[a].at[c]
            if whole or near:
                sends[a, 1] = cp(a, 1, own, me_slot, (*xn, c))
                sends[a, 2] = cp(a, 2, own, me_slot, (*yn, c))
                passes[a, 4] = cp(a, 4, outs[a].at[slot(*xn, c)], slot(*xn, c), sib)
                passes[a, 5] = cp(a, 5, outs[a].at[slot(*yn, c)], slot(*yn, c), sib)
            if whole or far:
                source = outs[a] if whole else ins[a]
                relayed = c * slot(*yn, c) + (1 - c) * slot(*xn, c)
                relay[a, 3] = cp(a, 3, source.at[relayed], relayed, (x + c * (1 - 2 * x), y + (1 - c) * (1 - 2 * y), c))
                passes[a, 6] = cp(a, 6, outs[a].at[slot(*dg, c)], slot(*dg, c), sib)
            arrive[a, 0] = arrival(a, 0, slot(x, y, 1 - c))
            arrive[a, 1] = arrival(a, 1, slot(*xn, c))
            arrive[a, 2] = arrival(a, 2, slot(*yn, c))
            arrive[a, 3] = arrival(a, 3, slot(*dg, c))
            arrive[a, 4] = arrival(a, 4, slot(*xn, 1 - c))
            arrive[a, 5] = arrival(a, 5, slot(*yn, 1 - c))
            arrive[a, 6] = arrival(a, 6, slot(*dg, 1 - c))
        return sends, relay, passes, arrive

    def pick(d, ks):
        return [d[a, k] for a in range(n) for k in ks if (a, k) in d]

    def start(ins, outs, sems):
        sends, relay, _, _ = copies(ins, outs, sems)
        for cp in pick(sends, (0, 1, 2)) + (pick(relay, (3,)) if far else []):
            cp.start()

    def forward_neighbours(ins, outs, sems):
        _, relay, passes, arrive = copies(ins, outs, sems)
        for cp in pick(arrive, (1, 2)):
            cp.wait_recv()
        for cp in pick(relay, (3,)) + pick(passes, (4, 5)):
            cp.start()

    def forward_diagonal(ins, outs, sems):
        _, _, passes, arrive = copies(ins, outs, sems)
        for cp in pick(arrive, (3,)):
            cp.wait_recv()
        for cp in pick(passes, (6,)):
            cp.start()

    def finish(ins, outs, sems):
        sends, relay, passes, arrive = copies(ins, outs, sems)
        for cp in pick(arrive, (6,) if far else (0, 4, 5, 6) if whole else (4, 5)):
            cp.wait_recv()
        for cp in pick(sends, (0, 1, 2)) + pick(relay, (3,)) + pick(passes, (4, 5, 6)):
            cp.wait_send()

    shapes = [a.shape[-2:] for a in arrs]
    forwards = [forward_diagonal] if far else [forward_neighbours] if near else [forward_neighbours, forward_diagonal]
    return _Job(arrs, [jax.ShapeDtypeStruct((N_DEV,) + s, a.dtype) for s, a in zip(shapes, arrs)],
                [pltpu.SemaphoreType.DMA((n, 7)), pltpu.SemaphoreType.DMA((n, 7))],
                start, finish, forwards, forward_at)


def _pair_job(arrs):
    n = len(arrs)

    def copies(ins, outs, sems):
        send_sems, recv_sems = sems
        x, y, c = _my_place()
        remote = [pltpu.make_async_remote_copy(
            src_ref=ins[a], dst_ref=outs[a].at[c], send_sem=send_sems.at[a], recv_sem=recv_sems.at[a],
            device_id=(x, y, 1 - c), device_id_type=MESH) for a in range(n)]
        arrive = [pltpu.make_async_remote_copy(
            src_ref=ins[a], dst_ref=outs[a].at[1 - c], send_sem=send_sems.at[a], recv_sem=recv_sems.at[a],
            device_id=(x, y, 1 - c), device_id_type=MESH) for a in range(n)]
        return remote, arrive

    def start(ins, outs, sems):
        for cp in copies(ins, outs, sems)[0]:
            cp.start()

    def finish(ins, outs, sems):
        remote, arrive = copies(ins, outs, sems)
        for cp in arrive:
            cp.wait_recv()
        for cp in remote:
            cp.wait_send()

    return _Job(arrs, [jax.ShapeDtypeStruct((2,) + a.shape, a.dtype) for a in arrs],
                [pltpu.SemaphoreType.DMA((n,)), pltpu.SemaphoreType.DMA((n,))], start, finish)


def _run_job(job, name):
    ni, no = len(job.ins), len(job.outs)

    def body(*refs):
        ins, outs, sems = refs[:ni], refs[ni:ni + no], refs[ni + no:]
        job.start(ins, outs, sems)
        for fwd in job.forwards:
            fwd(ins, outs, sems)
        job.finish(ins, outs, sems)

    any_spec = pl.BlockSpec(memory_space=pl.ANY)
    return pl.pallas_call(
        body, name=name, out_shape=job.outs, in_specs=[any_spec] * ni, out_specs=[any_spec] * no,
        scratch_shapes=job.sems,
    )(*job.ins)


def _call(body, *, name, grid, in_specs, out_specs, out_shape, scratch_shapes=(), sem, args, jobs=(),
          prefetch=(), aliases=None, vmem=None):
    in_specs, out_specs, out_shape, scratch_shapes = list(in_specs), list(out_specs), list(out_shape), list(scratch_shapes)
    prefetch = list(prefetch)
    if not jobs and not prefetch and not aliases:
        res = pl.pallas_call(body, name=name, grid=grid, in_specs=in_specs, out_specs=out_specs, out_shape=out_shape,
                             scratch_shapes=scratch_shapes, compiler_params=_params(sem, vmem))(*args)
        return list(res), []
    n_in, n_out, n_scr = len(in_specs), len(out_specs), len(scratch_shapes)
    j_in = [len(j.ins) for j in jobs]
    j_out = [len(j.outs) for j in jobs]
    j_sem = [len(j.sems) for j in jobs]
    total = int(np.prod(grid))
    any_spec = pl.BlockSpec(memory_space=pl.ANY)

    def wrapped(*refs):
        pos = [0]

        def take(k):
            part = refs[pos[0]:pos[0] + k]
            pos[0] += k
            return part

        take(len(prefetch))
        ins = take(n_in)
        jins = [take(k) for k in j_in]
        outs = take(n_out)
        jouts = [take(k) for k in j_out]
        scr = take(n_scr)
        jsems = [take(k) for k in j_sem]
        step = pl.program_id(0)
        for d in range(1, len(grid)):
            step = step * grid[d] + pl.program_id(d)
        for j, job in enumerate(jobs):
            pl.when(step == 0)(functools.partial(job.start, jins[j], jouts[j], jsems[j]))
            for fwd, frac in zip(job.forwards, job.forward_at):
                at = min(int(total * frac), total - 1)
                pl.when(step == at)(functools.partial(fwd, jins[j], jouts[j], jsems[j]))
        body(*ins, *outs, *scr)
        for j, job in enumerate(jobs):
            pl.when(step == total - 1)(functools.partial(job.finish, jins[j], jouts[j], jsems[j]))

    grid_spec = pltpu.PrefetchScalarGridSpec(
        num_scalar_prefetch=len(prefetch), grid=grid,
        in_specs=in_specs + [any_spec] * sum(j_in),
        out_specs=out_specs + [any_spec] * sum(j_out),
        scratch_shapes=scratch_shapes + [s for j in jobs for s in j.sems])
    res = pl.pallas_call(
        wrapped, name=name, grid_spec=grid_spec,
        out_shape=out_shape + [o for j in jobs for o in j.outs],
        input_output_aliases={len(prefetch) + i: o for i, o in (aliases or {}).items()},
        compiler_params=_params(("arbitrary",) * len(grid), vmem),
    )(*prefetch, *args, *[a for j in jobs for a in j.ins])
    res = list(res)
    own, rest = res[:n_out], res[n_out:]
    per_job = []
    for k in j_out:
        per_job.append(rest[:k])
        rest = rest[k:]
    return own, per_job


def _exchange_sibling(arrs, name):
    return _run_job(_sibling_job(arrs), name)


def _sibling_job(arrs):
    n = len(arrs)

    def copies(ins, outs, sems):
        send_sems, recv_sems = sems
        x, y, c = _my_place()
        return [pltpu.make_async_remote_copy(
            src_ref=ins[a].at[2 * k + 1 - c], dst_ref=outs[a].at[k],
            send_sem=send_sems.at[a, k], recv_sem=recv_sems.at[a, k],
            device_id=(x, y, 1 - c), device_id_type=MESH) for a in range(n) for k in range(4)]

    def start(ins, outs, sems):
        for cp in copies(ins, outs, sems):
            cp.start()

    def finish(ins, outs, sems):
        for cp in copies(ins, outs, sems):
            cp.wait()

    return _Job(arrs, [jax.ShapeDtypeStruct((4,) + a.shape[1:], a.dtype) for a in arrs],
                [pltpu.SemaphoreType.DMA((n, 4)), pltpu.SemaphoreType.DMA((n, 4))], start, finish)


def _chips_job(arrs):
    n = len(arrs)

    def copies(ins, outs, sems):
        send_sems, recv_sems = sems
        x, y, c = _my_place()
        chips = [(1 - x, y), (x, 1 - y), (1 - x, 1 - y)]
        return [pltpu.make_async_remote_copy(
            src_ref=ins[a].at[2 * px + py], dst_ref=outs[a].at[j],
            send_sem=send_sems.at[a, j], recv_sem=recv_sems.at[a, j],
            device_id=(px, py, c), device_id_type=MESH) for a in range(n) for j, (px, py) in enumerate(chips)]

    def start(ins, outs, sems):
        for cp in copies(ins, outs, sems):
            cp.start()

    def finish(ins, outs, sems):
        for cp in copies(ins, outs, sems):
            cp.wait()

    return _Job(arrs, [jax.ShapeDtypeStruct((3,) + a.shape[1:], a.dtype) for a in arrs],
                [pltpu.SemaphoreType.DMA((n, 3)), pltpu.SemaphoreType.DMA((n, 3))], start, finish)


def _pair_add(stack, from_sibling, c_idx, name, widths=None):
    _, r, cc = stack.shape
    tr = _tile(r, 512)
    ws = [cc] if widths is None else list(widths)
    assert sum(ws) == cc
    starts = [sum(ws[:q]) for q in range(len(ws))]

    def body(c_ref, a_ref, b_ref, *o_refs):
        res = a_ref[...].astype(F32) + b_ref[...].astype(F32)
        for o_ref, s0, w_ in zip(o_refs, starts, ws):
            o_ref[...] = res[:, s0:s0 + w_].astype(o_ref.dtype)

    grid_spec = pltpu.PrefetchScalarGridSpec(
        num_scalar_prefetch=1, grid=(4, r // tr),
        in_specs=[pl.BlockSpec((None, tr, cc), lambda k, i, c_ref: (2 * k + c_ref[0], i, 0)),
                  pl.BlockSpec((None, tr, cc), lambda k, i, c_ref: (k, i, 0))],
        out_specs=[pl.BlockSpec((None, tr, w_), lambda k, i, c_ref: (k, i, 0)) for w_ in ws])
    res = pl.pallas_call(
        body, name=name, grid_spec=grid_spec,
        out_shape=[jax.ShapeDtypeStruct((4, r, w_), BF16) for w_ in ws],
        compiler_params=_params(("parallel", "parallel")),
    )(c_idx, stack, from_sibling)
    return res[0] if widths is None else list(res)


_DIMS = {"nn": ((1,), (0,)), "nt": ((1,), (1,)), "tn": ((0,), (0,))}


def _mm(a, b, mode, *, tm, tn, tk=None, outs, epi=None, tiles=(), rows=(), name, jobs=(), chunks=None, into=None,
        n_total=None, rows_part=None, wide=None, ring=False):
    wide = wide or {}
    if mode == "tn":
        kk, m = a.shape
    else:
        m, kk = a.shape
    n_all = b.shape[0] if mode == "nt" else b.shape[1]
    nn = n_all if chunks is None else chunks[0].shape[0] * chunks[1]
    n_all = n_all if n_total is None else n_total
    tk = kk if tk is None else tk
    assert m % tm == 0 and nn % tn == 0 and kk % tk == 0, (name, a.shape, b.shape, tm, tn, tk)
    nk = kk // tk
    nt, nr, no = len(tiles), len(rows), len(outs)
    n_into = 0 if into is None else 1

    n_col_blocks = nn // tn
    n_steps = (m // tm) * n_col_blocks

    def body(*refs):
        a_ref, b_ref = refs[:2]
        tile_refs = refs[2:2 + nt]
        row_refs = refs[2 + nt:2 + nt + nr]
        out_refs = refs[2 + nt + nr + n_into:2 + nt + nr + n_into + no]
        if ring:
            slots, slot_sems = refs[-2], refs[-1]
            step = pl.program_id(0) * n_col_blocks + pl.program_id(1)

            def fetch(of_step, q):
                first = pl.multiple_of((of_step % n_col_blocks) * tn, tn)
                return pltpu.make_async_copy(b_ref.at[:, pl.ds(first, tn)], slots.at[q], slot_sems.at[q])

            for q in range(3):
                @pl.when(jnp.logical_and(step == 0, q < n_steps))
                def _(q=q):
                    fetch(q, q).start()

                @pl.when(jnp.logical_and(jnp.logical_and(step > 0, step + 2 < n_steps), (step + 2) % 3 == q))
                def _(q=q):
                    fetch(step + 2, q).start()
        else:
            part = lax.dot_general(a_ref[...], b_ref[...], (_DIMS[mode], ((), ())), preferred_element_type=F32)

        def finish(acc):
            if epi is None:
                res = (acc,)
            else:
                res = epi(acc, *[t[...] for t in tile_refs], *[r[...] for r in row_refs])
            for o_ref, val in zip(out_refs, res):
                o_ref[...] = val.astype(o_ref.dtype)

        if ring:
            for q in range(3):
                @pl.when(step % 3 == q)
                def _(q=q):
                    fetch(step, q).wait()
                    finish(jnp.dot(a_ref[...], slots[q], preferred_element_type=F32))
        elif nk == 1:
            finish(part)
        else:
            acc_ref = refs[-1]
            k = pl.program_id(2)

            @pl.when(k == 0)
            def _():
                acc_ref[...] = part

            @pl.when(k > 0)
            def _():
                acc_ref[...] += part

            @pl.when(k == nk - 1)
            def _():
                finish(acc_ref[...])

    if chunks is None:
        col = b_row = lambda j, *_: j
        prefetch = []
    else:
        assert mode == "nt" and not tiles and not rows and chunks[1] % tn == 0
        per = chunks[1] // tn
        col = lambda j, ids, b_ids: ids[j // per] * per + j % per
        b_row = lambda j, ids, b_ids: b_ids[j // per] * per + j % per
        prefetch = [chunks[0], chunks[2]]
    row0, n_row_blocks = (0, m // tm) if rows_part is None else rows_part
    assert rows_part is None or (mode != "tn" and not tiles)
    if mode == "tn":
        a_spec = pl.BlockSpec((tk, tm), lambda i, j, k, *_: (k, i))
    else:
        a_spec = pl.BlockSpec((tm, tk), lambda i, j, k, *_: (i + row0, k))
    if mode == "nt":
        b_spec = pl.BlockSpec((tn, tk), lambda i, j, k, *s: (b_row(j, *s), k))
    else:
        b_spec = pl.BlockSpec((tk, tn), lambda i, j, k, *_: (k, j))
    scratch = [pltpu.VMEM((tm, tn), F32)] if nk > 1 else []
    if ring:
        assert mode == "nn" and nk == 1 and chunks is None and rows_part is None
        b_spec = pl.BlockSpec(memory_space=pl.ANY)
        scratch = [pltpu.VMEM((3, kk, tn), b.dtype), pltpu.SemaphoreType.DMA((3,))]
    tile_specs = [pl.BlockSpec((tm, tn), functools.partial(lambda i, j, k, off: (i, j + off), off=off))
                  for _, off in tiles]
    row_specs = [pl.BlockSpec((r.shape[0], tn), lambda i, j, k: (0, j)) for r in rows]
    out_spec = pl.BlockSpec((tm, tn), lambda i, j, k, *s: (i + row0, col(j, *s)))
    args = [a, b, *[t for t, _ in tiles], *rows]
    in_specs = [a_spec, b_spec] + tile_specs + row_specs
    aliases = None
    if into is not None:
        aliases = {len(args): 0}
        args.append(into)
        in_specs.append(pl.BlockSpec(memory_space=pl.ANY))
    res, jres = _call(
        body, name=name, grid=(n_row_blocks, nn // tn, nk),
        in_specs=in_specs,
        out_specs=[out_spec if q not in wide else
                   pl.BlockSpec((tm, tn), functools.partial(lambda i, j, k, off: (i, j + off), off=wide[q][1]))
                   for q in range(no)],
        out_shape=[jax.ShapeDtypeStruct((m, wide[q][0] if q in wide else n_all), dt) for q, dt in enumerate(outs)],
        scratch_shapes=scratch,
        sem=("arbitrary",) * 3 if ring else ("parallel", "parallel", "arbitrary"),
        args=args, jobs=jobs, prefetch=prefetch, aliases=aliases, vmem=VMEM_LIMIT_RING_BYTES if ring else None)
    return (res, jres) if jobs else res


def _ada_fwd(c_all, w_ada, b_ada_cols, name):
    nb, dd = c_all.shape
    ncol = w_ada.shape[1]
    tn = _tile(ncol, 512, LANE)

    def body(c_ref, w_ref, b_ref, o_ref):
        cv = c_ref[...]
        act = (cv * _sigmoid(cv)).astype(BF16)
        o_ref[...] = jnp.dot(act, w_ref[...].astype(BF16), preferred_element_type=F32) + b_ref[...]

    return pl.pallas_call(
        body, name=name, grid=(ncol // tn,),
        in_specs=[pl.BlockSpec((nb, dd), lambda j: (0, 0)), pl.BlockSpec((dd, tn), lambda j: (0, j)),
                  pl.BlockSpec((1, tn), lambda j: (0, j))],
        out_specs=pl.BlockSpec((nb, tn), lambda j: (0, j)),
        out_shape=jax.ShapeDtypeStruct((nb, ncol), F32),
        compiler_params=_params(("parallel",)),
    )(c_all, w_ada, b_ada_cols)


def _ada_bwd(c_all_t, dmod_cols, name):
    dd, nb = c_all_t.shape
    ncol = dmod_cols.shape[1]
    tr = _tile(dd, 256)

    def body(c_ref, d_ref, o_ref):
        cv = c_ref[...]
        act = (cv * _sigmoid(cv)).astype(BF16).astype(F32)
        dm = d_ref[...].astype(BF16).astype(F32)
        acc = act[:, 0:1] * dm[0:1, :]
        for bi in range(1, nb):
            acc = acc + act[:, bi:bi + 1] * dm[bi:bi + 1, :]
        o_ref[...] = acc

    return pl.pallas_call(
        body, name=name, grid=(dd // tr,),
        in_specs=[pl.BlockSpec((tr, nb), lambda i: (i, 0)), pl.BlockSpec((nb, ncol), lambda i: (0, 0))],
        out_specs=pl.BlockSpec((tr, ncol), lambda i: (i, 0)),
        out_shape=jax.ShapeDtypeStruct((dd, ncol), F32),
        compiler_params=_params(("parallel",)),
    )(c_all_t, dmod_cols)


def _norm_mod(x, gain, mod, row0, name):
    t, dd = x.shape
    tt = _tile(t, 512)

    def body(x_ref, g_ref, m_ref, h_ref):
        xv = x_ref[...]
        r = lax.rsqrt(jnp.mean(xv * xv, axis=-1, keepdims=True) + EPS)
        shift, scale = m_ref[row0:row0 + 1, :], m_ref[row0 + 1:row0 + 2, :]
        h_ref[...] = ((xv * r * g_ref[...]) * (1.0 + scale) + shift).astype(BF16)

    return pl.pallas_call(
        body, name=name, grid=(t // tt,),
        in_specs=[pl.BlockSpec((tt, dd), lambda i: (i, 0)), pl.BlockSpec((1, dd), lambda i: (0, 0)),
                  pl.BlockSpec((8, dd), lambda i: (0, 0))],
        out_specs=pl.BlockSpec((tt, dd), lambda i: (i, 0)),
        out_shape=jax.ShapeDtypeStruct((t, dd), BF16),
        compiler_params=_params(("parallel",)),
    )(x, gain, mod)


def _norm_mod_bwd(x, gain, mod, row0, dh, dres, branch, name, jobs=()):
    t, dd = x.shape
    tt = _tile(t, 512)
    has_branch = branch is not None

    def body(*refs):
        if has_branch:
            x_ref, g_ref, m_ref, dh_ref, dres_ref, o_ref, dx_ref, st_ref, do_ref = refs[:-1]
        else:
            x_ref, g_ref, m_ref, dh_ref, dres_ref, dx_ref, st_ref = refs[:-1]

        acc_ref = refs[-1]
        step_i = pl.program_id(0)

        @pl.when(step_i == 0)
        def _():
            acc_ref[...] = jnp.zeros_like(acc_ref)

        def fold(v):
            out = v[0:8]
            for q in range(1, rc // 8):
                out = out + v[8 * q:8 * q + 8]
            return out

        def chunk(ci, carry):
            rows = pl.ds(pl.multiple_of(ci * rc, rc), rc)
            xv = x_ref[rows, :]
            g = g_ref[...]
            r = lax.rsqrt(jnp.mean(xv * xv, axis=-1, keepdims=True) + EPS)
            nv = xv * r
            dhv = dh_ref[rows, :].astype(F32)
            acc_ref[0] += fold(dhv)
            acc_ref[1] += fold(dhv * (nv * g))
            dng = dhv * (1.0 + m_ref[row0 + 1:row0 + 2, :])
            acc_ref[2] += fold(dng * nv)
            dn = dng * g
            dx = dres_ref[rows, :] + r * (dn - nv * jnp.mean(dn * nv, axis=-1, keepdims=True))
            dx_ref[rows, :] = dx
            if has_branch:
                acc_ref[3] += fold(dx * o_ref[rows, :].astype(F32))
                do_ref[rows, :] = (dx * m_ref[branch[1]:branch[1] + 1, :]).astype(BF16)
            return carry

        lax.fori_loop(0, tt // rc, chunk, 0, unroll=4)

        @pl.when(step_i == nsteps - 1)
        def _():
            st_ref[...] = jnp.zeros_like(st_ref)
            for k in range(4):
                st_ref[k:k + 1, :] = jnp.sum(acc_ref[k], axis=0, keepdims=True)

    rc = 16 if tt % 16 == 0 else 8
    nsteps = t // tt
    row = pl.BlockSpec((tt, dd), lambda i: (i, 0))
    in_specs = [row, pl.BlockSpec((1, dd), lambda i: (0, 0)), pl.BlockSpec((8, dd), lambda i: (0, 0)), row, row]
    args = [x, gain, mod, dh, dres]
    out_specs = [row, pl.BlockSpec((8, dd), lambda i: (0, 0))]
    out_shape = [jax.ShapeDtypeStruct((t, dd), F32), jax.ShapeDtypeStruct((8, dd), F32)]
    if has_branch:
        in_specs.append(row)
        args.append(branch[0])
        out_specs.append(row)
        out_shape.append(jax.ShapeDtypeStruct((t, dd), BF16))
    res, jres = _call(body, name=name, grid=(t // tt,), in_specs=in_specs, out_specs=out_specs, out_shape=out_shape,
                      scratch_shapes=[pltpu.VMEM((4, 8, dd), F32)], sem=("arbitrary",), args=args, jobs=jobs)
    return (res, jres) if jobs else res


def _final_loss(x2, o2, final_g, target, mod, name):
    t, dd = x2.shape
    tt = _tile(t, 256)
    nsteps = t // tt

    def body(x_ref, o_ref, g_ref, t_ref, m_ref, loss_ref, dx_ref, do_ref, st_ref, lacc):
        i = pl.program_id(0)

        @pl.when(i == 0)
        def _():
            st_ref[...] = jnp.zeros_like(st_ref)
            lacc[...] = jnp.zeros_like(lacc)

        xv = x_ref[...]
        g = g_ref[...]
        r = lax.rsqrt(jnp.mean(xv * xv, axis=-1, keepdims=True) + EPS)
        nv = xv * r
        err = nv * g - t_ref[...]
        lacc[...] += jnp.sum(err * err, axis=0, keepdims=True)
        dy = err * (1.0 / dd)
        st_ref[0:1, :] += jnp.sum(dy * nv, axis=0, keepdims=True)
        dn = dy * g
        dx = r * (dn - nv * jnp.mean(dn * nv, axis=-1, keepdims=True))
        dx_ref[...] = dx
        st_ref[1:2, :] += jnp.sum(dx * o_ref[...].astype(F32), axis=0, keepdims=True)
        do_ref[...] = (dx * m_ref[5:6, :]).astype(BF16)

        @pl.when(i == nsteps - 1)
        def _():
            total = jnp.sum(lacc[...], axis=-1, keepdims=True) * (0.5 / dd)
            loss_ref[...] = jnp.broadcast_to(total, loss_ref.shape)

    row = pl.BlockSpec((tt, dd), lambda i: (i, 0))
    return pl.pallas_call(
        body, name=name, grid=(nsteps,),
        in_specs=[row, row, pl.BlockSpec((1, dd), lambda i: (0, 0)), row, pl.BlockSpec((8, dd), lambda i: (0, 0))],
        out_specs=[pl.BlockSpec((8, LANE), lambda i: (0, 0)), row, row, pl.BlockSpec((8, dd), lambda i: (0, 0))],
        out_shape=[jax.ShapeDtypeStruct((8, LANE), F32), jax.ShapeDtypeStruct((t, dd), F32),
                   jax.ShapeDtypeStruct((t, dd), BF16), jax.ShapeDtypeStruct((8, dd), F32)],
        scratch_shapes=[pltpu.VMEM((1, dd), F32)],
        compiler_params=_params(("arbitrary",)),
    )(x2, o2, final_g, target, mod)


def _lru_gates(xt, prev8, lp, wa_ref, wx_ref, first_tile, cw):
    tt = xt.shape[0]
    ext = jnp.concatenate([prev8, xt], axis=0)
    xs = [pltpu.roll(ext, s, 0)[8:, :] if s else xt for s in range(CONV_WIDTH)]
    xc = lp[0:1, :] + lp[7:8, :] * xs[0] + lp[6:7, :] * xs[1] + lp[5:6, :] * xs[2] + lp[4:5, :] * xs[3]
    xcb = xc.astype(BF16)
    za, zi = [], []
    for jb in range(cw // LRU_BLOCK_W):
        sl = slice(jb * LRU_BLOCK_W, (jb + 1) * LRU_BLOCK_W)
        za.append(jnp.dot(xcb[:, sl], wa_ref[jb].astype(BF16), preferred_element_type=F32))
        zi.append(jnp.dot(xcb[:, sl], wx_ref[jb].astype(BF16), preferred_element_type=F32))
    ra = _sigmoid(jnp.concatenate(za, axis=1) + lp[1:2, :])
    ri = _sigmoid(jnp.concatenate(zi, axis=1) + lp[2:3, :])
    sp = _softplus(-lp[3:4, :])
    log_a = -LRU_C * ra * sp
    av = jnp.exp(log_a)
    seq_start = jnp.logical_and(first_tile, lax.broadcasted_iota(jnp.int32, (tt, 1), 0) == 0)
    m2 = _neg_expm1(2.0 * log_a, av * av)
    inv_mult = jnp.where(seq_start, 1.0, lax.rsqrt(jnp.maximum(m2, 1e-30)))
    mult = jnp.where(seq_start, 1.0, m2 * inv_mult)
    return xs, xc, xcb, ra, ri, sp, av, mult, inv_mult, seq_start


def _scan_groups(a, u, reverse):
    tt, cw = a.shape
    a3, u3 = a.reshape(tt // 8, 8, cw), u.reshape(tt // 8, 8, cw)
    r = lax.broadcasted_iota(jnp.int32, (1, 8, 1), 1)
    for s in (1, 2, 4):
        shift = 8 - s if reverse else s
        take = r < 8 - s if reverse else r >= s
        u3 = jnp.where(take, a3 * pltpu.roll(u3, shift, 1) + u3, u3)
        a3 = jnp.where(take, a3 * pltpu.roll(a3, shift, 1), a3)
    return a3, u3


def _scan_carry(a_s, u_s, h_s, carry, reverse):
    ng = a_s.shape[0]

    def step(i, h):
        g = ng - 1 - i if reverse else i
        hg = u_s[g] + a_s[g] * h
        h_s[g] = hg
        return hg[0:1, :] if reverse else hg[7:8, :]

    return lax.fori_loop(0, ng, step, carry, unroll=4)


def _lru_fwd(proj, lp, wa, wx, name, jobs=()):
    t = proj.shape[0]
    w = D_MODEL
    cw = LRU_CW
    tt = _tile(t, 256)
    nb = cw // LRU_BLOCK_W

    def body(lx_ref, lg_ref, lp_ref, wa_ref, wx_ref, rec_ref, ya_ref, a_s, u_s, h_s, halo, carry):
        ti = pl.program_id(1)

        @pl.when(ti == 0)
        def _():
            halo[...] = jnp.zeros_like(halo)
            carry[...] = jnp.zeros_like(carry)

        xt = lx_ref[...].astype(F32)
        lp_v = lp_ref[...]
        _, xc, _, _, ri, _, av, mult, _, _ = _lru_gates(xt, halo[...], lp_v, wa_ref, wx_ref, ti == 0, cw)
        halo[...] = xt[tt - 8:, :]
        a_s[...], u_s[...] = _scan_groups(av, mult * (ri * xc), False)
        carry[...] = _scan_carry(a_s, u_s, h_s, carry[...], False)
        rec = h_s[...].reshape(tt, cw)
        rec_ref[...] = rec.astype(BF16)
        gelu, _ = _gelu_tanh(lg_ref[...].astype(F32))
        ya_ref[...] = (rec * gelu).astype(BF16)

    off_g = OFF_LG // cw
    res, jres = _call(
        body, name=name, grid=(w // cw, t // tt),
        in_specs=[pl.BlockSpec((tt, cw), lambda ci, ti: (ti, ci)),
                  pl.BlockSpec((tt, cw), lambda ci, ti: (ti, off_g + ci)),
                  pl.BlockSpec((8, cw), lambda ci, ti: (0, ci)),
                  pl.BlockSpec((nb, LRU_BLOCK_W, LRU_BLOCK_W), lambda ci, ti: (ci, 0, 0)),
                  pl.BlockSpec((nb, LRU_BLOCK_W, LRU_BLOCK_W), lambda ci, ti: (ci, 0, 0))],
        out_specs=[pl.BlockSpec((tt, cw), lambda ci, ti: (ti, ci))] * 2,
        out_shape=[jax.ShapeDtypeStruct((t, w), BF16)] * 2,
        scratch_shapes=[pltpu.VMEM((tt // 8, 8, cw), F32)] * 3 + [pltpu.VMEM((8, cw), F32), pltpu.VMEM((1, cw), F32)],
        sem=("parallel", "arbitrary"), args=[proj, proj, lp, wa, wx], jobs=jobs)
    return (res, jres) if jobs else res


def _lru_bwd(proj, rec, dya, lp, wa, wx, name, jobs=(), into=None):
    t = proj.shape[0]
    w = D_MODEL
    cw = LRU_CW if into is None else w
    tt = _tile(t, 256 if into is None else 128)
    nt = t // tt
    nb = cw // LRU_BLOCK_W
    r8 = tt // 8

    def body(lx_ref, lxp_ref, lg_ref, rec_ref, recp_ref, dya_ref, lp_ref, wa_ref, wx_ref, *rest):
        if into is None:
            dlx_ref, dlg_ref = rest[0], rest[1]
        else:
            dlx_ref, dlg_ref = rest[1].at[:, 0:cw], rest[1].at[:, cw:2 * cw]
        st_ref, dwa_ref, dwx_ref, a_s, d_s, dh_s, nhalo, carry = rest[-8:]
        step_i = pl.program_id(1)
        ti = nt - 1 - step_i

        @pl.when(step_i == 0)
        def _():
            st_ref[...] = jnp.zeros_like(st_ref)
            dwa_ref[...] = jnp.zeros_like(dwa_ref)
            dwx_ref[...] = jnp.zeros_like(dwx_ref)
            nhalo[...] = jnp.zeros_like(nhalo)
            carry[...] = jnp.zeros_like(carry)

        first = ti == 0
        keep = jnp.where(first, 0.0, 1.0)
        xt = lx_ref[...].astype(F32)
        prev8 = lxp_ref[...].astype(F32) * keep
        lp_v = lp_ref[...]
        xs, xc, xcb, ra, ri, sp, av, mult, inv_mult, seq_start = _lru_gates(xt, prev8, lp_v, wa_ref, wx_ref, first, cw)

        lg = lg_ref[...].astype(F32)
        gelu, th = _gelu_tanh(lg)
        dyav = dya_ref[...].astype(F32)
        recv = rec_ref[...].astype(F32)
        dlg_ref[...] = (dyav * recv * _gelu_tanh_grad(lg, th)).astype(BF16)

        drec = dyav * gelu
        e_in = carry[...]
        a_s[...], d_s[...] = _scan_groups(av, av * drec, True)
        carry[...] = _scan_carry(a_s, d_s, dh_s, e_in, True)
        e_next = jnp.concatenate([dh_s[...].reshape(tt, cw), jnp.broadcast_to(e_in, (8, cw))], axis=0)
        dh = drec + pltpu.roll(e_next, tt + 7, 0)[:tt, :]
        rprev8 = recp_ref[...].astype(F32) * keep
        hprev = pltpu.roll(jnp.concatenate([rprev8, recv], axis=0), 1, 0)[8:, :]
        da = dh * hprev
        dmult = jnp.where(seq_start, 0.0, dh * ri * xc)
        dri = dh * mult * xc
        dxc = dh * mult * ri
        dlog_a = da * av - dmult * (av * av) * inv_mult
        dra = dlog_a * (-LRU_C * sp)
        lam = lp_v[3:4, :]
        st_ref[3:4, :] += jnp.sum(dlog_a * ra, axis=0, keepdims=True) * (LRU_C * _sigmoid(-lam))
        dza = dra * ra * (1.0 - ra)
        dzi = dri * ri * (1.0 - ri)
        st_ref[1:2, :] += jnp.sum(dza, axis=0, keepdims=True)
        st_ref[2:3, :] += jnp.sum(dzi, axis=0, keepdims=True)
        dzab, dzib = dza.astype(BF16), dzi.astype(BF16)
        back = []
        for jb in range(nb):
            sl = slice(jb * LRU_BLOCK_W, (jb + 1) * LRU_BLOCK_W)
            dwa_ref[jb] += lax.dot_general(xcb[:, sl], dzab[:, sl], (_DIMS["tn"], ((), ())), preferred_element_type=F32)
            dwx_ref[jb] += lax.dot_general(xcb[:, sl], dzib[:, sl], (_DIMS["tn"], ((), ())), preferred_element_type=F32)
            back.append(
                lax.dot_general(dzab[:, sl], wa_ref[jb].astype(BF16), (_DIMS["nt"], ((), ())), preferred_element_type=F32)
                + lax.dot_general(dzib[:, sl], wx_ref[jb].astype(BF16), (_DIMS["nt"], ((), ())), preferred_element_type=F32))
        dxc = dxc + jnp.concatenate(back, axis=1)
        st_ref[0:1, :] += jnp.sum(dxc, axis=0, keepdims=True)
        for k in range(CONV_WIDTH):
            st_ref[4 + k:5 + k, :] += jnp.sum(dxc * xs[CONV_WIDTH - 1 - k], axis=0, keepdims=True)
        ext = jnp.concatenate([dxc, nhalo[...]], axis=0)
        dlx = lp_v[7:8, :] * dxc
        for s in range(1, CONV_WIDTH):
            dlx = dlx + lp_v[7 - s:8 - s, :] * pltpu.roll(ext, tt + 8 - s, 0)[:tt, :]
        dlx_ref[...] = dlx.astype(BF16)
        nhalo[...] = dxc[0:8, :]

    off_g = OFF_LG // cw
    tile = lambda off: pl.BlockSpec((tt, cw), lambda ci, si: (nt - 1 - si, off + ci))
    prev = lambda off: pl.BlockSpec((8, cw), lambda ci, si: (jnp.maximum((nt - 1 - si) * r8 - 1, 0), off + ci))
    wspec = pl.BlockSpec((nb, LRU_BLOCK_W, LRU_BLOCK_W), lambda ci, si: (ci, 0, 0))
    st_spec = pl.BlockSpec((8, cw), lambda ci, si: (0, ci))
    in_specs = [tile(0), prev(0), tile(off_g), tile(0), prev(0), tile(0), st_spec, wspec, wspec]
    args = [proj, proj, proj, rec, rec, dya, lp, wa, wx]
    if into is None:
        d_specs = [tile(0), tile(0)]
        d_shapes = [jax.ShapeDtypeStruct((t, w), BF16), jax.ShapeDtypeStruct((t, w), BF16)]
        aliases = None
    else:
        d_specs = [pl.BlockSpec((tt, 2 * cw), lambda ci, si: (nt - 1 - si, 0))]
        d_shapes = [jax.ShapeDtypeStruct(into.shape, BF16)]
        aliases = {len(args): 0}
        args.append(into)
        in_specs.append(pl.BlockSpec(memory_space=pl.ANY))
    res, jres = _call(
        body, name=name, grid=(w // cw, nt), in_specs=in_specs,
        out_specs=d_specs + [st_spec, wspec, wspec],
        out_shape=d_shapes + [jax.ShapeDtypeStruct((8, w), F32),
                              jax.ShapeDtypeStruct((LRU_BLOCKS, LRU_BLOCK_W, LRU_BLOCK_W), F32),
                              jax.ShapeDtypeStruct((LRU_BLOCKS, LRU_BLOCK_W, LRU_BLOCK_W), F32)],
        scratch_shapes=[pltpu.VMEM((tt // 8, 8, cw), F32)] * 3 + [pltpu.VMEM((8, cw), F32), pltpu.VMEM((1, cw), F32)],
        sem=("parallel", "arbitrary"), args=args, jobs=jobs, aliases=aliases)
    return (res, jres) if jobs else res


def _band_valid(first_block):
    qi = lax.broadcasted_iota(jnp.int32, (BLOCK, 2 * BLOCK), 0)
    ki = lax.broadcasted_iota(jnp.int32, (BLOCK, 2 * BLOCK), 1)
    rel = qi + BLOCK - ki
    valid = jnp.logical_and(rel >= 0, rel < BLOCK)
    return jnp.logical_and(valid, jnp.logical_or(ki >= BLOCK, jnp.logical_not(first_block)))


ATTN_STACK_BWD = 8


def _low_half():
    return lax.broadcasted_iota(jnp.int32, (1, LANE), 1) < HEAD_DIM


def _stack_heads(x, h0, ns):
    low = _low_half()
    parts = []
    for g in range(ns):
        h = h0 + g
        slab = x[:, (h // 2) * LANE:(h // 2 + 1) * LANE]
        parts.append(jnp.where(low if h % 2 == 0 else jnp.logical_not(low), slab, jnp.zeros_like(slab)))
    return jnp.concatenate(parts, axis=0)


def _unstack_heads(y, ns):
    low = _low_half()
    pairs = [jnp.where(low, y[(2 * j) * BLOCK:(2 * j + 1) * BLOCK], y[(2 * j + 1) * BLOCK:(2 * j + 2) * BLOCK])
             for j in range(ns // 2)]
    return pairs[0] if len(pairs) == 1 else jnp.concatenate(pairs, axis=1)


def _dup_kv(xall, kv):
    part = xall[:, kv * HEAD_DIM:(kv + 1) * HEAD_DIM]
    return jnp.concatenate([part, part], axis=1)


def _fold_halves(a):
    return a + pltpu.roll(a, HEAD_DIM, 1)


def _group_probs(qg, k2, bias_g, sink_g, lse_g, valid):
    s = lax.dot_general(qg, k2, (_DIMS["nt"], ((), ())), preferred_element_type=F32) * (HEAD_DIM ** -0.5)
    s = jnp.where(valid[None], s.reshape(bias_g.shape) + bias_g, NEG_INF)
    return jnp.exp(s - lse_g), jnp.exp(sink_g - lse_g)


def _group_sinks(s_ref, h0, ns):
    return jnp.concatenate([jnp.full((1, BLOCK, 1), s_ref[h0 + g], F32) for g in range(ns)], axis=0)


def _attn_specs(t):
    nblk = t // BLOCK
    q_spec = pl.BlockSpec((BLOCK, D_MODEL), lambda n: (n, OFF_Q // D_MODEL))
    kc = pl.BlockSpec((BLOCK, KV_WIDTH), lambda n: (n, OFF_K // KV_WIDTH))
    kp = pl.BlockSpec((BLOCK, KV_WIDTH), lambda n: (jnp.maximum(n - 1, 0), OFF_K // KV_WIDTH))
    vc = pl.BlockSpec((BLOCK, KV_WIDTH), lambda n: (n, OFF_V // KV_WIDTH))
    vp = pl.BlockSpec((BLOCK, KV_WIDTH), lambda n: (jnp.maximum(n - 1, 0), OFF_V // KV_WIDTH))
    bias_spec = pl.BlockSpec((N_Q_HEADS, BLOCK, 2 * BLOCK), lambda n: (0, 0, 0))
    sink_spec = pl.BlockSpec(memory_space=pltpu.SMEM)
    return nblk, [q_spec, kp, kc, vp, vc, bias_spec, sink_spec]


def _attn_fwd(proj, bias, sinks, name, jobs=()):
    t = proj.shape[0]
    nblk, in_specs = _attn_specs(t)

    def body(q_ref, kp_ref, kc_ref, vp_ref, vc_ref, b_ref, s_ref, o_ref, lse_ref):
        valid = _band_valid(pl.program_id(0) == 0)
        lane = lax.broadcasted_iota(jnp.int32, (1, LANE), 1)
        q = q_ref[...]
        kall = jnp.concatenate([kp_ref[...], kc_ref[...]], axis=0)
        vall = jnp.concatenate([vp_ref[...], vc_ref[...]], axis=0)
        outs = []
        lse = jnp.zeros((BLOCK, LANE), F32)
        for h in range(N_Q_HEADS):
            kv = h // GROUP
            kk = kall[:, kv * HEAD_DIM:(kv + 1) * HEAD_DIM]
            vv = vall[:, kv * HEAD_DIM:(kv + 1) * HEAD_DIM]
            s = lax.dot_general(q[:, h * HEAD_DIM:(h + 1) * HEAD_DIM], kk, (_DIMS["nt"], ((), ())),
                                preferred_element_type=F32) * (HEAD_DIM ** -0.5)
            s = jnp.where(valid, s + b_ref[h], NEG_INF)
            sink = s_ref[h]
            m = jnp.maximum(jnp.max(s, axis=-1, keepdims=True), sink)
            e = jnp.exp(s - m)
            den = jnp.sum(e, axis=-1, keepdims=True) + jnp.exp(sink - m)
            lse = jnp.where(lane == h, m + jnp.log(den), lse)
            outs.append(jnp.dot((e * (1.0 / den)).astype(BF16), vv, preferred_element_type=F32))
        o_ref[...] = jnp.concatenate(outs, axis=1).astype(BF16)
        lse_ref[...] = lse

    res, jres = _call(
        body, name=name, grid=(nblk,), in_specs=in_specs,
        out_specs=[pl.BlockSpec((BLOCK, D_MODEL), lambda n: (n, 0)), pl.BlockSpec((BLOCK, LANE), lambda n: (n, 0))],
        out_shape=[jax.ShapeDtypeStruct((t, D_MODEL), BF16), jax.ShapeDtypeStruct((t, LANE), F32)],
        sem=("parallel",), args=[proj, proj, proj, proj, proj, bias, sinks], jobs=jobs)
    return (res, jres) if jobs else res


def _attn_bwd(proj, datt, lse, bias, sinks, name, jobs=(), into=None):
    t = proj.shape[0]
    nblk, in_specs = _attn_specs(t)
    in_specs = in_specs + [pl.BlockSpec((BLOCK, D_MODEL), lambda n: (n, 0)), pl.BlockSpec((BLOCK, LANE), lambda n: (n, 0))]
    scale = HEAD_DIM ** -0.5
    args = [proj, proj, proj, proj, proj, bias, sinks, datt, lse]
    dq_spec = pl.BlockSpec((BLOCK, D_MODEL), lambda n: (n, 0))
    dq_shape = jax.ShapeDtypeStruct((t, D_MODEL), BF16)
    aliases = None
    if into is not None:
        aliases = {len(args): 0}
        args.append(into)
        in_specs.append(pl.BlockSpec(memory_space=pl.ANY))
        dq_spec = pl.BlockSpec((BLOCK, D_MODEL), lambda n: (n, OFF_Q // D_MODEL))
        dq_shape = jax.ShapeDtypeStruct(into.shape, BF16)

    def body(q_ref, kp_ref, kc_ref, vp_ref, vc_ref, b_ref, s_ref, do_ref, lse_ref, *rest):
        dq_ref, dk_ref, dv_ref, db_ref, ds_ref = rest[-5:]
        n = pl.program_id(0)

        @pl.when(n == 0)
        def _():
            dk_ref[...] = jnp.zeros_like(dk_ref)
            dv_ref[...] = jnp.zeros_like(dv_ref)
            db_ref[...] = jnp.zeros_like(db_ref)
            ds_ref[...] = jnp.zeros_like(ds_ref)

        valid = _band_valid(n == 0)
        lane = lax.broadcasted_iota(jnp.int32, (1, LANE), 1)
        q = q_ref[...]
        do = do_ref[...]
        lse_all = lse_ref[...]
        kall = jnp.concatenate([kp_ref[...], kc_ref[...]], axis=0)
        vall = jnp.concatenate([vp_ref[...], vc_ref[...]], axis=0)
        low = _low_half()
        dqs, dks, dvs = [], [], []
        dsink_row = jnp.zeros((1, LANE), F32)
        ns = ATTN_STACK_BWD
        for kv in range(N_KV_HEADS):
            k2, v2 = _dup_kv(kall, kv), _dup_kv(vall, kv)
            dk_acc = jnp.zeros((2 * BLOCK, LANE), F32)
            dv_acc = jnp.zeros((2 * BLOCK, LANE), F32)
            for h0 in range(kv * GROUP, (kv + 1) * GROUP, ns):
                qg, dog = _stack_heads(q, h0, ns), _stack_heads(do, h0, ns)
                lse_g = jnp.concatenate([lse_all[:, h0 + g:h0 + g + 1][None] for g in range(ns)], axis=0)
                p, psink = _group_probs(qg, k2, b_ref[h0:h0 + ns], _group_sinks(s_ref, h0, ns), lse_g, valid)
                dp = lax.dot_general(dog, v2, (_DIMS["nt"], ((), ())), preferred_element_type=F32)
                dp = dp.reshape(ns, BLOCK, 2 * BLOCK)
                delta = jnp.sum(p * dp, axis=-1, keepdims=True)
                ds = p * (dp - delta)
                db_ref[h0:h0 + ns] += ds
                dsink = -jnp.sum(psink * delta, axis=1, keepdims=True)
                for g in range(ns):
                    dsink_row = dsink_row + jnp.where(lane == h0 + g, dsink[g], 0.0)
                dsb = ds.reshape(ns * BLOCK, 2 * BLOCK).astype(BF16)
                pb = p.reshape(ns * BLOCK, 2 * BLOCK).astype(BF16)
                dqs.append(_unstack_heads(jnp.dot(dsb, k2, preferred_element_type=F32) * scale, ns))
                dk_acc = dk_acc + lax.dot_general(dsb, qg, (_DIMS["tn"], ((), ())), preferred_element_type=F32)
                dv_acc = dv_acc + lax.dot_general(pb, dog, (_DIMS["tn"], ((), ())), preferred_element_type=F32)
            dks.append(_fold_halves(dk_acc) * scale)
            dvs.append(_fold_halves(dv_acc))
        dq_ref[...] = jnp.concatenate(dqs, axis=1).astype(BF16)
        rows = pl.ds(pl.multiple_of(n * BLOCK, BLOCK), 2 * BLOCK)
        dk_ref[rows, :] += jnp.concatenate([jnp.where(low, dks[0], dks[1]), jnp.where(low, dks[2], dks[3])], axis=1)
        dv_ref[rows, :] += jnp.concatenate([jnp.where(low, dvs[0], dvs[1]), jnp.where(low, dvs[2], dvs[3])], axis=1)
        ds_ref[0:1, :] += dsink_row

    res, jres = _call(
        body, name=name, grid=(nblk,), in_specs=in_specs,
        out_specs=[dq_spec,
                   pl.BlockSpec((t + BLOCK, KV_WIDTH), lambda n: (0, 0)),
                   pl.BlockSpec((t + BLOCK, KV_WIDTH), lambda n: (0, 0)),
                   pl.BlockSpec((N_Q_HEADS, BLOCK, 2 * BLOCK), lambda n: (0, 0, 0)),
                   pl.BlockSpec((8, LANE), lambda n: (0, 0))],
        out_shape=[dq_shape,
                   jax.ShapeDtypeStruct((t + BLOCK, KV_WIDTH), F32),
                   jax.ShapeDtypeStruct((t + BLOCK, KV_WIDTH), F32),
                   jax.ShapeDtypeStruct((N_Q_HEADS, BLOCK, 2 * BLOCK), F32),
                   jax.ShapeDtypeStruct((8, LANE), F32)],
        sem=("arbitrary",), args=args, jobs=jobs, aliases=aliases)
    return (res, jres) if jobs else res


def _bucket_table():
    qi = np.arange(BLOCK)[:, None]
    ki = np.arange(2 * BLOCK)[None, :]
    rel = np.maximum(qi + BLOCK - ki, 0)
    max_exact = N_BUCKETS // 2
    relf = np.maximum(rel, 1).astype(np.float32)
    large = max_exact + (np.log(relf / max_exact) / math.log(MAX_DISTANCE / max_exact)
                         * (N_BUCKETS - max_exact)).astype(np.int32)
    large = np.minimum(large, N_BUCKETS - 1)
    return np.where(rel < max_exact, rel, large).astype(np.int32)


def _rel_bias_fwd(rel_bias, onehot, name):
    n = onehot.shape[1]
    tn = _tile(n, 8192, LANE)

    def body(r_ref, o_ref, out_ref):
        out_ref[...] = lax.dot_general(r_ref[...], o_ref[...], (_DIMS["tn"], ((), ())),
                                       precision=lax.Precision.HIGHEST, preferred_element_type=F32)

    return pl.pallas_call(
        body, name=name, grid=(n // tn,),
        in_specs=[pl.BlockSpec(rel_bias.shape, lambda j: (0, 0)), pl.BlockSpec((N_BUCKETS, tn), lambda j: (0, j))],
        out_specs=pl.BlockSpec((N_Q_HEADS, tn), lambda j: (0, j)),
        out_shape=jax.ShapeDtypeStruct((N_Q_HEADS, n), F32),
        compiler_params=_params(("parallel",)),
    )(rel_bias, onehot)


def _rel_bias_bwd(dbias, onehot, name):
    def body(d_ref, o_ref, out_ref):
        out_ref[...] = lax.dot_general(d_ref[...], o_ref[...], (_DIMS["nt"], ((), ())),
                                       precision=lax.Precision.HIGHEST, preferred_element_type=F32)

    full = pl.BlockSpec(dbias.shape, lambda: (0, 0))
    return pl.pallas_call(
        body, name=name, in_specs=[full, full],
        out_specs=pl.BlockSpec((N_Q_HEADS, N_BUCKETS), lambda: (0, 0)),
        out_shape=jax.ShapeDtypeStruct((N_Q_HEADS, N_BUCKETS), F32),
        compiler_params=_params(()),
    )(dbias, onehot)


def _part_specs(gparts, part_index, tr, cc):
    blk = pl.BlockSpec((tr, cc), lambda i, *_: (i, 0))
    specs = []
    for i, _ in enumerate(gparts):
        if i in part_index:
            ix = part_index[i]
            if isinstance(ix, int):
                specs.append(pl.BlockSpec((None, tr, cc), functools.partial(lambda i, *_, ix: (ix, i, 0), ix=ix)))
            else:
                specs.append(pl.BlockSpec((None, tr, cc), lambda i, ix_ref: (ix_ref[0], i, 0)))
        else:
            specs.append(blk)
    return blk, specs


def _sum_parts(gparts, part_index, name):
    r, cc = gparts[0].shape[-2:]
    tr = _tile(r, 256)
    dyn = [i for i, ix in part_index.items() if not isinstance(ix, int)]
    assert len(dyn) <= 1

    def body(*refs):
        refs = refs[len(dyn):]
        g = refs[0][...].astype(F32)
        for gr in refs[1:-1]:
            g = g + gr[...].astype(F32)
        refs[-1][...] = g

    blk, specs = _part_specs(gparts, part_index, tr, cc)
    grid_spec = pltpu.PrefetchScalarGridSpec(num_scalar_prefetch=len(dyn), grid=(r // tr,), in_specs=specs, out_specs=blk)
    return pl.pallas_call(
        body, name=name, grid_spec=grid_spec, out_shape=jax.ShapeDtypeStruct((r, cc), F32),
        compiler_params=_params(("parallel",)),
    )(*[part_index[i] for i in dyn], *gparts)


def _adamw(w, m, v, gparts, name, part_index=None, jobs=()):
    r, cc = w.shape
    tr = _tile(r, 128)
    np_ = len(gparts)
    part_index = part_index or {}
    dyn = [i for i, ix in part_index.items() if not isinstance(ix, int)]
    assert len(dyn) <= 1
    bc1 = 1.0 / (1.0 - ADAM_B1 ** ADAM_STEP)
    bc2 = 1.0 / (1.0 - ADAM_B2 ** ADAM_STEP)

    def body(*refs):
        refs = refs[len(dyn):]
        w_ref, m_ref, v_ref = refs[:3]
        g_refs = refs[3:3 + np_]
        g_out, d_out, m_out, v_out = refs[3 + np_:]
        g = g_refs[0][...].astype(F32)
        for gr in g_refs[1:]:
            g = g + gr[...].astype(F32)
        mn = ADAM_B1 * m_ref[...] + (1.0 - ADAM_B1) * g
        vn = ADAM_B2 * v_ref[...] + (1.0 - ADAM_B2) * (g * g)
        g_out[...] = g
        m_out[...] = mn
        v_out[...] = vn
        d_out[...] = -ADAM_LR * ((mn * bc1) / (jnp.sqrt(vn * bc2) + ADAM_EPS) + ADAM_WD * w_ref[...])

    blk, g_specs = _part_specs(gparts, part_index, tr, cc)
    if jobs:
        assert not dyn
        return _call(body, name=name, grid=(r // tr,), in_specs=[blk, blk, blk] + g_specs, out_specs=[blk] * 4,
                     out_shape=[jax.ShapeDtypeStruct((r, cc), F32)] * 4, sem=("arbitrary",),
                     args=[w, m, v, *gparts], jobs=jobs)
    grid_spec = pltpu.PrefetchScalarGridSpec(
        num_scalar_prefetch=len(dyn), grid=(r // tr,),
        in_specs=[blk, blk, blk] + g_specs, out_specs=[blk] * 4)
    return pl.pallas_call(
        body, name=name, grid_spec=grid_spec,
        out_shape=[jax.ShapeDtypeStruct((r, cc), F32)] * 4,
        compiler_params=_params(("parallel",)),
    )(*[part_index[i] for i in dyn], w, m, v, *gparts)


def _row(vec):
    return vec.reshape(1, -1)


def kernel(x, c, w_ada, b_ada, norm1_g, w_in, conv_w, conv_b, lru_wa, lru_ba, lru_wx, lru_bx, lru_lambda, w_lru_out, w_attn_out, attn_sinks, rel_bias, w_out, norm2_g, w_ff1, w_ff2, final_g, loss_target, m_w_ada, m_b_ada, m_norm1_g, m_w_in, m_conv_w, m_conv_b, m_lru_wa, m_lru_ba, m_lru_wx, m_lru_bx, m_lru_lambda, m_w_lru_out, m_w_attn_out, m_attn_sinks, m_rel_bias, m_w_out, m_norm2_g, m_w_ff1, m_w_ff2, m_final_g, v_w_ada, v_b_ada, v_norm1_g, v_w_in, v_conv_w, v_conv_b, v_lru_wa, v_lru_ba, v_lru_wx, v_lru_bx, v_lru_lambda, v_w_lru_out, v_w_attn_out, v_attn_sinks, v_rel_bias, v_w_out, v_norm2_g, v_w_ff1, v_w_ff2, v_final_g):
    dd = D_MODEL
    t = x.shape[1]
    ix, iy, ic = lax.axis_index("x"), lax.axis_index("y"), lax.axis_index("c")
    me = 4 * ix + 2 * iy + ic
    chip = 2 * ix + iy
    c_idx = jnp.reshape(ic, (1,)).astype(jnp.int32)
    chip_idx = jnp.reshape(chip, (1,)).astype(jnp.int32)

    xs = x[0]
    target = loss_target[0]
    ncol_ada = w_ada.shape[2]
    cshard = conv_w.shape[2]

    shards = [jnp.transpose(w_in[0]).astype(BF16), w_lru_out[0].astype(BF16), w_attn_out[0].astype(BF16),
              w_out[0].astype(BF16), jnp.transpose(w_ff1[0]).astype(BF16), w_ff2[0].astype(BF16)]
    s_in, s_lo, s_ao, s_o, s_f1, s_f2 = shards

    def full(g):
        return g.reshape(N_DEV * g.shape[1], dd)

    def stack(g):
        return g.reshape(N_DEV, g.shape[0] // N_DEV, dd)

    (w_in_pair,) = _run_job(_pair_job([s_in]), "ag_w_in_pair")
    w_in_pair = lax.dynamic_update_slice_in_dim(w_in_pair, s_in[None], ic, axis=0)

    pack0 = jnp.zeros((8, dd), F32).at[0:1, :].set(c).at[1:1 + CONV_WIDTH, 0:cshard].set(conv_w[0])
    (g0,) = _all_gather([pack0], "ag_cond")
    c_all = g0[:, 0, :]
    conv_w_full = jnp.transpose(g0[:, 1:1 + CONV_WIDTH, 0:cshard], (1, 0, 2)).reshape(CONV_WIDTH, dd)
    b_cols = lax.dynamic_slice(b_ada, (0, me * ncol_ada), (1, ncol_ada))
    mod_cols = _ada_fwd(c_all, w_ada[0], b_cols, "ada_fwd")
    (g1,) = _all_gather([mod_cols], "ag_mod")
    mod_mine = lax.dynamic_index_in_dim(g1, me, axis=1, keepdims=False).reshape(6, dd)
    mod = jnp.concatenate([mod_mine, jnp.zeros((2, dd), F32)], axis=0)

    bucket = _bucket_table()
    onehot = jnp.asarray((bucket.reshape(1, -1) == np.arange(N_BUCKETS)[:, None]).astype(np.float32))
    bias = _rel_bias_fwd(rel_bias, onehot, "rel_bias_fwd").reshape(N_Q_HEADS, BLOCK, 2 * BLOCK)
    sinks = attn_sinks.reshape(N_Q_HEADS)
    lp = jnp.concatenate([conv_b, lru_ba, lru_bx, lru_lambda, conv_w_full], axis=0)
    wa, wx = lru_wa[0], lru_wx[0]

    tm = _tile(t, 1024)
    tn = 512
    h = _norm_mod(xs, norm1_g, mod, 0, "norm1")
    tn2 = 1024
    chunk_w = 2 * s_in.shape[0]

    def ids(*ks):
        return jnp.stack([jnp.asarray(k, jnp.int32) for k in ks])

    tn_in = _tile(chunk_w, 1024, LANE)
    (proj,), ((g_near,),) = _mm(
        h, w_in_pair.reshape(chunk_w, dd), "nt", tm=tm, tn=tn_in, outs=[BF16], name="in_proj_own",
        chunks=(ids(chip), chunk_w, ids(0)), n_total=IN_WIDTH, jobs=[_gather_job([w_in_pair], (0.7,), "near")])
    near_ids = ids(2 * (1 - ix) + iy, 2 * ix + 1 - iy)
    (proj,), ((g_far,), (g_o,)) = _mm(
        h, full(g_near), "nt", tm=tm, tn=tn_in, outs=[BF16], name="in_proj_near",
        chunks=(near_ids, chunk_w, near_ids), into=proj,
        jobs=[_gather_job([g_near], (0.6,), "far"), _gather_job([s_o], (0.6, 0.85))])
    far_chip = 2 * (1 - ix) + 1 - iy
    (proj,), ((g_lo,),) = _mm(
        h, full(g_far), "nt", tm=tm, tn=tn_in, outs=[BF16], name="in_proj_far",
        chunks=(ids(far_chip), chunk_w, ids(far_chip)), into=proj, jobs=[_gather_job([s_lo], (0.5, 0.8))])
    far_slots = lax.dynamic_slice_in_dim(g_far, 2 * far_chip, 2, axis=0)
    win_t = lax.dynamic_update_slice_in_dim(g_near, far_slots, 2 * far_chip, axis=0)
    win_t = full(lax.dynamic_update_slice_in_dim(win_t, w_in_pair, 2 * chip, axis=0))
    (g_o,) = _fill_own([g_o], [s_o])
    (rec, ya), ((g_f1,),) = _lru_fwd(proj, lp, wa, wx, "lru_fwd", jobs=[_gather_job([s_f1], (0.6, 0.9))])
    (att, lse), ((g_ao,),) = _attn_fwd(proj, bias, sinks, "attn_fwd", jobs=[_gather_job([s_ao], (0.4, 0.7))])
    g_lo, g_ao, g_f1 = _fill_own([g_lo, g_ao, g_f1], [s_lo, s_ao, s_f1])
    wlo, wao, wo, wf1_t = full(g_lo), full(g_ao), full(g_o), full(g_f1)
    (y_a,) = _mm(ya, wlo, "nn", tm=tm, tn=tn2, outs=[BF16], name="lru_out")

    def merge_epi(acc, ya_t, ga_t, gb_t):
        return acc, _sigmoid(ga_t.astype(F32)) * ya_t.astype(F32) + _sigmoid(gb_t.astype(F32)) * acc

    y_b, merged = _mm(att, wao, "nn", tm=tm, tn=tn, outs=[BF16, BF16], epi=merge_epi,
                      tiles=[(y_a, 0), (proj, OFF_GA // tn), (proj, OFF_GB // tn)], name="attn_out_merge")

    def resid_epi(row):
        def epi(acc, x_t, mod_t):
            return acc, x_t + mod_t[row:row + 1, :] * acc
        return epi

    o1, x1 = _mm(merged, wo, "nn", tm=tm, tn=tn2, outs=[BF16, F32], epi=resid_epi(2),
                 tiles=[(xs, 0)], rows=[mod], name="out_proj")
    h2 = _norm_mod(x1, norm2_g, mod, 3, "norm2")

    def relu2_epi(acc):
        rl = jnp.maximum(acc, 0.0)
        return acc, rl * rl

    (f_pre, ff), ((g_f2,),) = _mm(h2, wf1_t, "nt", tm=tm, tn=1024, outs=[BF16, BF16], epi=relu2_epi, name="ff1",
                                  jobs=[_gather_job([s_f2], (0.6, 0.9))])
    wf2 = full(_fill_own([g_f2], [s_f2])[0])
    tmh = _tile(t, 512)
    o2, x2 = _mm(ff, wf2, "nn", tm=tmh, tn=tn, outs=[BF16, F32], epi=resid_epi(5),
                 tiles=[(x1, 0)], rows=[mod], ring=True, name="ff2")

    loss_blk, dx2, do2, st_f = _final_loss(x2, o2, _row(final_g), target, mod, "final_loss")
    loss = lax.psum(loss_blk[0, 0], ("x", "y", "c"))

    def drelu2_epi(acc, f_t):
        return (acc * (2.0 * jnp.maximum(f_t.astype(F32), 0.0)),)

    tmw, tnw, tkw = 512, dd, _tile(t, 2048)
    (df,) = _mm(do2, wf2, "nt", tm=tm, tn=1024, outs=[BF16], epi=drelu2_epi, tiles=[(f_pre, 0)], name="ff2_dx")
    (dwf2,) = _mm(ff, do2, "tn", tm=tmw, tn=tnw, tk=tkw, outs=[BF16], name="ff2_dw")
    (dh2,), ((sib_f2,),) = _mm(df, wf1_t, "nn", tm=tmh, tn=tn, outs=[BF16], ring=True, name="ff1_dx",
                              jobs=[_sibling_job([stack(dwf2)])])
    pair_f2 = _pair_add(stack(dwf2), sib_f2, c_idx, "rs_pair_add_w_ff2")
    (dwf1_t,), ((chips_f2,),) = _mm(df, h2, "tn", tm=tmw, tn=tnw, tk=tkw, outs=[BF16], name="ff1_dw",
                                    jobs=[_chips_job([pair_f2])])
    (dx1, st_2, do1), ((sib_f1,),) = _norm_mod_bwd(x1, norm2_g, mod, 3, dh2, dx2, (o1, 2), "norm2_bwd",
                                                   jobs=[_sibling_job([stack(dwf1_t)])])
    pair_f1 = _pair_add(stack(dwf1_t), sib_f1, c_idx, "rs_pair_add_w_ff1")

    def dmerge_epi(acc, ga_t, gb_t, ya_t, yb_t):
        sa, sb = _sigmoid(ga_t.astype(F32)), _sigmoid(gb_t.astype(F32))
        return (acc * sa, acc * sb, acc * ya_t.astype(F32) * sa * (1.0 - sa), acc * yb_t.astype(F32) * sb * (1.0 - sb))

    tmh = _tile(t, 512)
    dy_a, dy_b, dproj, dgb = _mm(do1, wo, "nt", tm=tm, tn=tn, outs=[BF16] * 4, epi=dmerge_epi,
                                 tiles=[(proj, OFF_GA // tn), (proj, OFF_GB // tn), (y_a, 0), (y_b, 0)],
                                 wide={2: (IN_WIDTH, OFF_GA // tn)}, name="out_proj_dx")
    (dwo,) = _mm(merged, do1, "tn", tm=tmw, tn=tnw, tk=tkw, outs=[BF16], name="out_proj_dw")
    (datt,) = _mm(dy_b, wao, "nt", tm=tm, tn=tn2, outs=[BF16], name="attn_out_dx")
    (dwao,) = _mm(att, dy_b, "tn", tm=tmw, tn=tnw, tk=tkw, outs=[BF16], name="attn_out_dw")
    (dya,) = _mm(dy_a, wlo, "nt", tm=tm, tn=tn2, outs=[BF16], name="lru_out_dx")
    (dwlo,) = _mm(ya, dy_a, "tn", tm=tmw, tn=tnw, tk=tkw, outs=[BF16], name="lru_out_dw")
    (dproj, dk_pad, dv_pad, dbias, dsinks), (sib_3, (chips_f1,)) = _attn_bwd(
        proj, datt, lse, bias, sinks, "attn_bwd", into=dproj,
        jobs=[_sibling_job([stack(dwlo), stack(dwao), stack(dwo)]), _chips_job([pair_f1])])
    pair_lo, pair_ao, pair_o = [_pair_add(stack(g_), s_, c_idx, "rs_pair_add_" + nm_)
                                for g_, s_, nm_ in zip((dwlo, dwao, dwo), sib_3, ("w_lru_out", "w_attn_out", "w_out"))]
    (dproj, st_l, dwa, dwx), ((chips_lo, chips_ao, chips_o),) = _lru_bwd(
        proj, rec, dya, lp, wa, wx, "lru_bwd", into=dproj, jobs=[_chips_job([pair_lo, pair_ao, pair_o])])
    dkv = jnp.concatenate([dk_pad[BLOCK:].astype(BF16), dv_pad[BLOCK:].astype(BF16)], axis=1)
    dproj = lax.dynamic_update_slice_in_dim(dproj, dkv, OFF_K, axis=1)
    dproj = lax.dynamic_update_slice_in_dim(dproj, dgb, OFF_GB, axis=1)
    lru_mats = jnp.concatenate([dwa.reshape(-1, dd), dwx.reshape(-1, dd)], axis=0)
    (dwin_t,), ((mats_all,),) = _mm(dproj, h, "tn", tm=768, tn=tnw, tk=tkw, outs=[BF16], name="in_proj_dw",
                                   jobs=[_gather_job([lru_mats], (0.3, 0.6))])
    (mats_all,) = _fill_own([mats_all], [lru_mats])
    (sib_in,) = _exchange_sibling([stack(dwin_t)], "rs_sibling_w_in")
    pair_in = _pair_add(stack(dwin_t), sib_in, c_idx, "rs_pair_add_w_in", widths=(dd // 2, dd // 2))
    n_rb = t // tm
    dx_tiles = dict(tm=tm, tn=tn2, tk=IN_WIDTH // 4, outs=[BF16])
    if n_rb >= 2:
        (dh,), ((chips_in_0,),) = _mm(dproj, win_t, "nn", name="in_proj_dx_top", rows_part=(0, n_rb // 2),
                                     jobs=[_chips_job(pair_in[0:1])], **dx_tiles)
        (dh,), ((chips_in_1,),) = _mm(dproj, win_t, "nn", name="in_proj_dx_bottom", into=dh,
                                     rows_part=(n_rb // 2, n_rb - n_rb // 2), jobs=[_chips_job(pair_in[1:2])], **dx_tiles)
    else:
        (dh,), ((chips_in_0,), (chips_in_1,)) = _mm(
            dproj, win_t, "nn", name="in_proj_dx", jobs=[_chips_job(pair_in[0:1]), _chips_job(pair_in[1:2])], **dx_tiles)
    grad_x, st_1 = _norm_mod_bwd(xs, norm1_g, mod, 0, dh, dx1, None, "norm1_bwd")
    chips_in = [chips_in_0, chips_in_1]

    drel = jnp.transpose(_rel_bias_bwd(dbias.reshape(N_Q_HEADS, -1), onehot, "rel_bias_bwd"))

    misc = jnp.concatenate([drel.reshape(1, -1), dsinks[0:1, 0:N_Q_HEADS],
                            jnp.zeros((1, dd - N_BUCKETS * N_Q_HEADS - N_Q_HEADS), F32)], axis=1)
    zero_row = jnp.zeros((1, dd), F32)
    small = jnp.concatenate([
        st_1[0:2], st_2[3:4], st_2[0:2], st_f[1:2], zero_row, zero_row,
        st_1[2:3], st_l[0:1], st_l[1:2], st_l[2:3], st_l[3:4], st_2[2:3], st_f[0:1], misc,
        st_l[4:8], jnp.zeros((4, dd), F32)], axis=0)
    own_and_chips = {0: chip_idx, 1: 0, 2: 1, 3: 2}
    g_in = jnp.transpose(jnp.concatenate(
        [_sum_parts([p_, c_, c_, c_], own_and_chips, "sum_w_in_%d" % q_)
         for q_, (p_, c_) in enumerate(zip(pair_in, chips_in))], axis=1))
    w_in_res, ((small_all,),) = _adamw(w_in[0], m_w_in[0], v_w_in[0], [g_in], "adamw_w_in",
                                       jobs=[_gather_job([small], (0.3, 0.6))])
    (small_all,) = _fill_own([small_all], [small])

    def pack_small(b_, n1, cb, ba, bx, lam, n2, fg, rb, sk):
        misc_ = jnp.concatenate([rb.reshape(1, -1), sk.reshape(1, -1),
                                 jnp.zeros((1, dd - N_BUCKETS * N_Q_HEADS - N_Q_HEADS), F32)], axis=1)
        return jnp.concatenate([b_.reshape(6, dd), jnp.zeros((2, dd), F32), n1, cb, ba, bx, lam, n2, _row(fg), misc_,
                                jnp.zeros((8, dd), F32)], axis=0)

    def pack_mats(wa_, wx_):
        return jnp.concatenate([wa_.reshape(-1, dd), wx_.reshape(-1, dd)], axis=0)

    every = {i: i for i in range(N_DEV)}
    w_s = pack_small(b_ada, norm1_g, conv_b, lru_ba, lru_bx, lru_lambda, norm2_g, final_g, rel_bias, attn_sinks)
    m_s = pack_small(m_b_ada, m_norm1_g, m_conv_b, m_lru_ba, m_lru_bx, m_lru_lambda, m_norm2_g, m_final_g, m_rel_bias, m_attn_sinks)
    v_s = pack_small(v_b_ada, v_norm1_g, v_conv_b, v_lru_ba, v_lru_bx, v_lru_lambda, v_norm2_g, v_final_g, v_rel_bias, v_attn_sinks)
    small_res = _adamw(w_s, m_s, v_s, [small_all] * N_DEV, "adamw_small", part_index=every)
    mats_res = _adamw(pack_mats(lru_wa, lru_wx), pack_mats(m_lru_wa, m_lru_wx), pack_mats(v_lru_wa, v_lru_wx),
                      [mats_all] * N_DEV, "adamw_lru_mats", part_index=every)

    def unpack_small(s, mt):
        nb_ = N_BUCKETS * N_Q_HEADS
        half = mt.shape[0] // 2
        return dict(
            b_ada=s[0:6].reshape(1, 6 * dd), norm1_g=s[8:9], conv_b=s[9:10], lru_ba=s[10:11], lru_bx=s[11:12],
            lru_lambda=s[12:13], norm2_g=s[13:14], final_g=s[14], rel_bias=s[15, 0:nb_].reshape(N_BUCKETS, N_Q_HEADS),
            attn_sinks=s[15:16, nb_:nb_ + N_Q_HEADS],
            lru_wa=mt[:half].reshape(1, LRU_BLOCKS, LRU_BLOCK_W, LRU_BLOCK_W),
            lru_wx=mt[half:].reshape(1, LRU_BLOCKS, LRU_BLOCK_W, LRU_BLOCK_W))

    res = {k: [None] * 4 for k in ("w_ada", "w_in", "conv_w", "w_lru_out", "w_attn_out", "w_out", "w_ff1", "w_ff2")}
    for q_, (s_, mt_) in enumerate(zip(small_res, mats_res)):
        for k_, val in unpack_small(s_, mt_).items():
            res.setdefault(k_, [None] * 4)[q_] = val

    g_conv = lax.dynamic_slice(small_res[0][16:16 + CONV_WIDTH], (0, me * cshard), (CONV_WIDTH, cshard))
    conv_res = _adamw(conv_w[0], m_conv_w[0], v_conv_w[0], [g_conv], "adamw_conv")
    res["conv_w"] = [r_[None] for r_ in conv_res]

    dmod_all = small_all[:, 0:6, :].reshape(N_DEV, 6 * dd)
    dmod_cols = lax.dynamic_slice(dmod_all, (0, me * ncol_ada), (N_DEV, ncol_ada))
    g_ada = _ada_bwd(jnp.transpose(c_all), dmod_cols, "ada_bwd")
    ada_res = _adamw(w_ada[0], m_w_ada[0], v_w_ada[0], [g_ada], "adamw_ada")
    res["w_ada"] = [r_[None] for r_ in ada_res]

    pair = [None, pair_lo, pair_ao, pair_o, pair_f1, pair_f2]
    from_chips = [None, chips_lo, chips_ao, chips_o, chips_f1, chips_f2]

    def summed(i):
        return [pair[i], from_chips[i], from_chips[i], from_chips[i]], own_and_chips

    def sum_only(i, name):
        parts, index = summed(i)
        return _sum_parts(parts, index, name)

    res["w_in"] = [r_[None] for r_ in w_in_res]
    g_ff1 = jnp.transpose(sum_only(4, "sum_w_ff1"))
    res["w_ff1"] = [r_[None] for r_ in _adamw(w_ff1[0], m_w_ff1[0], v_w_ff1[0], [g_ff1], "adamw_w_ff1")]
    for i, (nm, w_, m_, v_) in {1: ("w_lru_out", w_lru_out, m_w_lru_out, v_w_lru_out),
                                2: ("w_attn_out", w_attn_out, m_w_attn_out, v_w_attn_out),
                                3: ("w_out", w_out, m_w_out, v_w_out),
                                5: ("w_ff2", w_ff2, m_w_ff2, v_w_ff2)}.items():
        parts, index = summed(i)
        res[nm] = [r_[None] for r_ in _adamw(w_[0], m_[0], v_[0], parts, "adamw_" + nm, part_index=index)]

    order = ["w_ada", "b_ada", "norm1_g", "w_in", "conv_w", "conv_b", "lru_wa", "lru_ba", "lru_wx", "lru_bx",
             "lru_lambda", "w_lru_out", "w_attn_out", "attn_sinks", "rel_bias", "w_out", "norm2_g", "w_ff1",
             "w_ff2", "final_g"]
    out = [loss, grad_x[None]]
    for q_ in range(4):
        out += [res[k_][q_] for k_ in order]
    return tuple(out)
```
